```python
import jax
import jax.numpy as jnp
from jax import lax
import numpy as np

D_MODEL = 2048
BATCH = 4
SEQ = 4096
DEPTH = 1

MIX_WIDTH = D_MODEL
ATT_WIDTH = MIX_WIDTH // 2
HGRN_WIDTH = MIX_WIDTH - ATT_WIDTH
HEAD_DIM = 128
N_ATT_HEADS = ATT_WIDTH // HEAD_DIM
N_KV_GROUPS = 2
HEADS_PER_GROUP = N_ATT_HEADS // N_KV_GROUPS
KV_WIDTH = N_KV_GROUPS * HEAD_DIM
CMP_BLOCK = 32
CMP_STRIDE = 16
SEL_BLOCK = 64
N_SEL = 16
WINDOW = 512
Q_BLOCK = 64
FORCE_BONUS = 1.0e4
ROPE_THETA = 10000.0
HGRN_HEAD_DIM = 128
N_HGRN_HEADS = HGRN_WIDTH // HGRN_HEAD_DIM
HGRN_CHUNK = 64
N_EXPERT_GROUPS = 4
EXPERTS_PER_GROUP = 8
N_EXPERTS = N_EXPERT_GROUPS * EXPERTS_PER_GROUP
EXPERT_TOP_K = 2
D_EXPERT = 512
IN_COLS = ATT_WIDTH + 6 * KV_WIDTH + 3 * N_ATT_HEADS + 4 * HGRN_WIDTH
EPS = 1e-6

kernel_name = 'hybrid_nsa_hgrn2_hmoe_block'


def rmsnorm(x, g):
    xf = x.astype(jnp.float32)
    y = xf * lax.rsqrt(jnp.mean(xf * xf, axis=-1, keepdims=True) + EPS)
    return (y * g.astype(jnp.float32)).astype(x.dtype)


def rope(t, pos):
    d = t.shape[-1]
    inv_freq = ROPE_THETA ** (-jnp.arange(0, d, 2, dtype=jnp.float32) / d)
    ang = pos.astype(jnp.float32)[:, :, None, None] * inv_freq
    cos, sin = jnp.cos(ang), jnp.sin(ang)
    tf = t.astype(jnp.float32)
    t1, t2 = tf[..., : d // 2], tf[..., d // 2:]
    return jnp.concatenate([t1 * cos - t2 * sin, t2 * cos + t1 * sin], axis=-1).astype(t.dtype)


def masked_softmax(s, mask):
    s = jnp.where(mask, s.astype(jnp.float32), -jnp.inf)
    m = jnp.max(s, axis=-1, keepdims=True)
    m = jnp.where(jnp.isfinite(m), m, 0.0)
    p = jnp.exp(s - m)
    return p / jnp.maximum(jnp.sum(p, axis=-1, keepdims=True), 1e-30)


def compress_blocks(t, pos_emb, w1, w2):
    B, S, G, d = t.shape
    nc = (S - CMP_BLOCK) // CMP_STRIDE + 1
    idx = (jnp.arange(nc) * CMP_STRIDE)[:, None] + jnp.arange(CMP_BLOCK)[None, :]
    blk = t[:, idx] + pos_emb[None, None, :, None, :]
    blk = blk.transpose(0, 1, 3, 2, 4).reshape(B, nc, G, CMP_BLOCK * d)
    return jax.nn.silu(blk @ w1) @ w2


def nsa_attention(q, k_cmp, v_cmp, k_slc, v_slc, k_win, v_win, gate_logits, pos,
                  pe_k, w1_k, w2_k, pe_v, w1_v, w2_v):
    B, S = q.shape[:2]
    G, H, d = N_KV_GROUPS, HEADS_PER_GROUP, HEAD_DIM
    scale = d ** -0.5
    t_idx = jnp.arange(S)
    qg = rope(q, pos).reshape(B, S, G, H, d)
    k_cmp, k_slc, k_win = rope(k_cmp, pos), rope(k_slc, pos), rope(k_win, pos)

    kc = compress_blocks(k_cmp, pe_k, w1_k, w2_k)
    vc = compress_blocks(v_cmp, pe_v, w1_v, w2_v)
    nc = kc.shape[1]
    c_start = jnp.arange(nc) * CMP_STRIDE
    c_mask = (c_start + CMP_BLOCK - 1)[None, :] <= t_idx[:, None]
    s_c = jnp.einsum('bsghd,bcgd->bghsc', qg, kc) * scale
    p_c = masked_softmax(s_c, c_mask)
    o_cmp = jnp.einsum('bghsc,bcgd->bsghd', p_c, vc.astype(jnp.float32))

    nsel = S // SEL_BLOCK
    n_top = min(N_SEL, nsel)
    j_start = jnp.arange(nsel) * SEL_BLOCK
    overlap = ((c_start[:, None] < j_start[None, :] + SEL_BLOCK)
               & (c_start[:, None] + CMP_BLOCK > j_start[None, :])).astype(jnp.float32)
    imp = jnp.einsum('bghsc,cj->bgsj', p_c, overlap)
    cur = (t_idx // SEL_BLOCK)[:, None]
    jj = jnp.arange(nsel)[None, :]
    forced = (jj == 0) | (jj == cur) | (jj == cur - 1)
    imp = jnp.where(jj <= cur, imp + jnp.where(forced, FORCE_BONUS, 0.0), -jnp.inf)
    _, sel_idx = lax.top_k(imp, n_top)

    kb = k_slc.reshape(B, nsel, SEL_BLOCK, G, d).transpose(0, 3, 1, 2, 4)
    vb = v_slc.reshape(B, nsel, SEL_BLOCK, G, d).transpose(0, 3, 1, 2, 4)
    kw = jnp.pad(k_win, ((0, 0), (WINDOW, 0), (0, 0), (0, 0)))
    vw = jnp.pad(v_win, ((0, 0), (WINDOW, 0), (0, 0), (0, 0)))
    b_ix = jnp.arange(B)[:, None, None, None]
    g_ix = jnp.arange(G)[None, :, None, None]
    in_blk = jnp.arange(SEL_BLOCK)
    w_off = jnp.arange(Q_BLOCK + WINDOW) - WINDOW

    def query_block(q0):
        tq = q0 + jnp.arange(Q_BLOCK)
        qb = lax.dynamic_slice_in_dim(qg, q0, Q_BLOCK, axis=1)
        ib = lax.dynamic_slice_in_dim(sel_idx, q0, Q_BLOCK, axis=2)
        ks = kb[b_ix, g_ix, ib]
        vs = vb[b_ix, g_ix, ib]
        s_s = jnp.einsum('bqghd,bgqnkd->bghqnk', qb, ks) * scale
        kpos = ib[..., None] * SEL_BLOCK + in_blk
        m_s = (kpos <= tq[None, None, :, None, None]).reshape(B, G, 1, Q_BLOCK, -1)
        p_s = masked_softmax(s_s.reshape(B, G, H, Q_BLOCK, -1), m_s).reshape(s_s.shape)
        o_s = jnp.einsum('bghqnk,bgqnkd->bqghd', p_s, vs.astype(jnp.float32))
        kwb = lax.dynamic_slice_in_dim(kw, q0, Q_BLOCK + WINDOW, axis=1)
        vwb = lax.dynamic_slice_in_dim(vw, q0, Q_BLOCK + WINDOW, axis=1)
        kpos_w = q0 + w_off
        dist = tq[:, None] - kpos_w[None, :]
        m_w = (dist >= 0) & (dist < WINDOW) & (kpos_w[None, :] >= 0)
        s_w = jnp.einsum('bqghd,bkgd->bghqk', qb, kwb) * scale
        p_w = masked_softmax(s_w, m_w)
        o_w = jnp.einsum('bghqk,bkgd->bqghd', p_w, vwb.astype(jnp.float32))
        return o_s, o_w

    o_s, o_w = lax.map(query_block, jnp.arange(S // Q_BLOCK) * Q_BLOCK)
    o_s = o_s.transpose(1, 0, 2, 3, 4, 5).reshape(B, S, G, H, d)
    o_w = o_w.transpose(1, 0, 2, 3, 4, 5).reshape(B, S, G, H, d)
    gt = jax.nn.sigmoid(gate_logits.astype(jnp.float32)).reshape(B, S, G, H, 3)
    o = gt[..., 0:1] * o_cmp + gt[..., 1:2] * o_s + gt[..., 2:3] * o_w
    return o.reshape(B, S, ATT_WIDTH).astype(q.dtype)


def hgrn2(q, f_logit, i, g, lb, g_norm):
    B, S, H, dk = q.shape
    nch = S // HGRN_CHUNK
    lb = lb.astype(jnp.float32).reshape(H, dk)
    f = lb + (1.0 - lb) * jax.nn.sigmoid(f_logit.astype(jnp.float32))
    log_f = jnp.log(f)
    k = 1.0 - f
    qf = jax.nn.silu(q.astype(jnp.float32))
    v = i.astype(jnp.float32)

    def to_chunks(t):
        return t.reshape(B, nch, HGRN_CHUNK, H, -1).transpose(1, 0, 3, 2, 4)

    causal = jnp.tril(jnp.ones((HGRN_CHUNK, HGRN_CHUNK), dtype=bool))

    def step(state, inp):
        qc, kc, vc, lc = inp
        bcum = jnp.cumsum(lc, axis=2)
        rel = jnp.where(causal[:, :, None], bcum[:, :, :, None, :] - bcum[:, :, None, :, :], -jnp.inf)
        a = jnp.einsum('bhid,bhjd,bhijd->bhij', qc, kc, jnp.exp(rel))
        o = (jnp.einsum('bhij,bhje->bhie', a, vc)
             + jnp.einsum('bhid,bhde->bhie', qc * jnp.exp(bcum), state))
        blast = bcum[:, :, -1:, :]
        state = (jnp.exp(blast[:, :, 0, :, None]) * state
                 + jnp.einsum('bhjd,bhje->bhde', kc * jnp.exp(blast - bcum), vc))
        return state, o

    s0 = jnp.zeros((B, H, dk, v.shape[-1]), jnp.float32)
    _, o = lax.scan(step, s0, (to_chunks(qf), to_chunks(k), to_chunks(v), to_chunks(log_f)))
    o = o.transpose(1, 0, 3, 2, 4).reshape(B, S, H, -1)
    o = o * lax.rsqrt(jnp.mean(o * o, axis=-1, keepdims=True) + EPS) * g_norm.astype(jnp.float32).reshape(H, -1)
    o = o * jax.nn.silu(g.astype(jnp.float32))
    return o.reshape(B, S, HGRN_WIDTH).astype(q.dtype)


def hybrid_mixer(h, pos, w_in, w_out, pe_k, w1_k, w2_k, pe_v, w1_v, w2_v, lb, g_norm):
    B, S, _ = h.shape
    proj = h @ w_in
    sizes = [ATT_WIDTH] + [KV_WIDTH] * 6 + [3 * N_ATT_HEADS] + [HGRN_WIDTH] * 4
    cuts = [int(v) for v in np.cumsum(sizes)[:-1]]
    parts = jnp.split(proj, cuts, axis=-1)
    q = parts[0].reshape(B, S, N_ATT_HEADS, HEAD_DIM)
    k_c, v_c, k_s, v_s, k_w, v_w = [p.reshape(B, S, N_KV_GROUPS, HEAD_DIM) for p in parts[1:7]]
    gate_logits = parts[7].reshape(B, S, N_ATT_HEADS, 3)
    hq, hf, hi, hg = [p.reshape(B, S, N_HGRN_HEADS, HGRN_HEAD_DIM) for p in parts[8:]]
    o_att = nsa_attention(q, k_c, v_c, k_s, v_s, k_w, v_w, gate_logits, pos,
                          pe_k, w1_k, w2_k, pe_v, w1_v, w2_v)
    o_rec = hgrn2(hq, hf, hi, hg, lb, g_norm)
    return jnp.concatenate([o_att, o_rec], axis=-1) @ w_out


def hier_moe(h, w_group, b_group, w_router, b_router, w_gate, w_up, w_down):
    B, S, D = h.shape
    T = B * S
    ht = h.reshape(T, D)
    lg = (ht @ w_group + b_group).astype(jnp.float32)
    pg = jax.nn.softmax(lg, axis=-1)
    _, gsel = lax.top_k(lg, 1)
    pg_sel = jnp.take_along_axis(pg, gsel, axis=-1)
    le = (ht @ w_router + b_router).astype(jnp.float32).reshape(T, N_EXPERT_GROUPS, EXPERTS_PER_GROUP)
    le_sel = jnp.take_along_axis(le, gsel[:, :, None], axis=1)[:, 0]
    pe = jax.nn.softmax(le_sel, axis=-1)
    topv, topi = lax.top_k(pe, EXPERT_TOP_K)
    topv = topv / jnp.sum(topv, axis=-1, keepdims=True)
    w_in_grp = jnp.sum(jax.nn.one_hot(topi, EXPERTS_PER_GROUP, dtype=jnp.float32) * topv[..., None], axis=1)
    cw = (jax.nn.one_hot(gsel[:, 0], N_EXPERT_GROUPS, dtype=jnp.float32)[:, :, None]
          * (w_in_grp * pg_sel)[:, None, :])
    y = jnp.zeros((T, D), jnp.float32)
    for gi in range(N_EXPERT_GROUPS):
        hid = (jax.nn.silu(jnp.einsum('td,edf->tef', ht, w_gate[gi]))
               * jnp.einsum('td,edf->tef', ht, w_up[gi]))
        y = y + jnp.einsum('tef,efd->td', hid * cw[:, gi, :, None], w_down[gi])
    return y.reshape(B, S, D).astype(h.dtype)


def setup_inputs(seed: int = 0) -> dict:
    key = jax.random.key(seed)
    ks = jax.random.split(key, 32)
    f32 = jnp.float32

    def nrm(k, shape, scale):
        return jax.random.normal(k, shape, f32) * scale

    L, D = DEPTH, D_MODEL
    offset = jax.random.randint(ks[2], (BATCH, 1), 0, 1024, dtype=jnp.int32)
    return {
        'x': nrm(ks[0], (BATCH, SEQ, D), 1.0),
        'c': nrm(ks[1], (BATCH, D), 1.0),
        'positions': offset + jnp.arange(SEQ, dtype=jnp.int32)[None, :],
        'w_ada': nrm(ks[3], (L, D, 6 * D), 0.5 * D ** -0.5),
        'b_ada': nrm(ks[4], (L, 6 * D), 0.02),
        'g_pre_mix': 1.0 + nrm(ks[5], (L, D), 0.05),
        'g_post_mix': 1.0 + nrm(ks[6], (L, D), 0.05),
        'g_pre_ffn': 1.0 + nrm(ks[7], (L, D), 0.05),
        'g_post_ffn': 1.0 + nrm(ks[8], (L, D), 0.05),
        'w_in': nrm(ks[9], (L, D, IN_COLS), D ** -0.5),
        'w_out': nrm(ks[10], (L, MIX_WIDTH, D), MIX_WIDTH ** -0.5),
        'cmp_pe_k': nrm(ks[11], (L, CMP_BLOCK, HEAD_DIM), 0.02),
        'cmp_w1_k': nrm(ks[12], (L, CMP_BLOCK * HEAD_DIM, HEAD_DIM), (CMP_BLOCK * HEAD_DIM) ** -0.5),
        'cmp_w2_k': nrm(ks[13], (L, HEAD_DIM, HEAD_DIM), HEAD_DIM ** -0.5),
        'cmp_pe_v': nrm(ks[14], (L, CMP_BLOCK, HEAD_DIM), 0.02),
        'cmp_w1_v': nrm(ks[15], (L, CMP_BLOCK * HEAD_DIM, HEAD_DIM), (CMP_BLOCK * HEAD_DIM) ** -0.5),
        'cmp_w2_v': nrm(ks[16], (L, HEAD_DIM, HEAD_DIM), HEAD_DIM ** -0.5),
        'hgrn_lb_logits': nrm(ks[17], (L + 1, HGRN_WIDTH), 0.5),
        'hgrn_g_norm': 1.0 + nrm(ks[18], (L, HGRN_WIDTH), 0.05),
        'w_group': nrm(ks[19], (L, D, N_EXPERT_GROUPS), D ** -0.5),
        'b_group': nrm(ks[20], (L, N_EXPERT_GROUPS), 0.01),
        'w_router': nrm(ks[21], (L, D, N_EXPERTS), D ** -0.5),
        'b_router': nrm(ks[22], (L, N_EXPERTS), 0.01),
        'w_gate': nrm(ks[23], (L, N_EXPERT_GROUPS, EXPERTS_PER_GROUP, D, D_EXPERT), D ** -0.5),
        'w_up': nrm(ks[24], (L, N_EXPERT_GROUPS, EXPERTS_PER_GROUP, D, D_EXPERT), D ** -0.5),
        'w_down': nrm(ks[25], (L, N_EXPERT_GROUPS, EXPERTS_PER_GROUP, D_EXPERT, D), D_EXPERT ** -0.5),
    }


def reference(x, c, positions, w_ada, b_ada, g_pre_mix, g_post_mix, g_pre_ffn, g_post_ffn,
              w_in, w_out, cmp_pe_k, cmp_w1_k, cmp_w2_k, cmp_pe_v, cmp_w1_v, cmp_w2_v,
              hgrn_lb_logits, hgrn_g_norm, w_group, b_group, w_router, b_router,
              w_gate, w_up, w_down):
    lb_table = jnp.cumsum(jax.nn.softmax(hgrn_lb_logits.astype(jnp.float32), axis=0), axis=0)
    for l in range(DEPTH):
        mod = (jax.nn.silu(c) @ w_ada[l] + b_ada[l])[:, None, :]
        sh1, sc1, gt1, sh2, sc2, gt2 = jnp.split(mod, 6, axis=-1)
        h = rmsnorm(x, g_pre_mix[l]) * (1.0 + sc1) + sh1
        y = hybrid_mixer(h, positions, w_in[l], w_out[l], cmp_pe_k[l], cmp_w1_k[l], cmp_w2_k[l],
                         cmp_pe_v[l], cmp_w1_v[l], cmp_w2_v[l], lb_table[l], hgrn_g_norm[l])
        x = x + gt1 * rmsnorm(y, g_post_mix[l])
        h = rmsnorm(x, g_pre_ffn[l]) * (1.0 + sc2) + sh2
        y = hier_moe(h, w_group[l], b_group[l], w_router[l], b_router[l], w_gate[l], w_up[l], w_down[l])
        x = x + gt2 * rmsnorm(y, g_post_ffn[l])
    return x
```

```python
import functools

import numpy as np
import jax
import jax.numpy as jnp
from jax import lax
from jax.experimental import pallas as pl
from jax.experimental.pallas import tpu as pltpu

F32 = jnp.float32
BF16 = jnp.bfloat16

HEAD_DIM = 128
N_KV_GROUPS = 2
HEADS_PER_GROUP = 4
N_ATT_HEADS = N_KV_GROUPS * HEADS_PER_GROUP
CMP_BLOCK = 32
CMP_STRIDE = 16
SEL_BLOCK = 64
N_SEL = 16
WINDOW = 512
FORCE_BONUS = 1.0e4
ROPE_THETA = 10000.0
N_HGRN_HEADS = 8
HGRN_CHUNK = 64
N_EXPERT_GROUPS = 4
EXPERTS_PER_GROUP = 8
N_EXPERTS = N_EXPERT_GROUPS * EXPERTS_PER_GROUP
EPS = 1e-6

LANES = 128
VMEM_LIMIT = 56 * 1024 * 1024

ATT_W = N_ATT_HEADS * HEAD_DIM
KV_W = N_KV_GROUPS * HEAD_DIM
COL_Q = 0
COL_KC = COL_Q + ATT_W
COL_KS = COL_KC + KV_W
COL_KW = COL_KS + KV_W
COL_VC = COL_KW + KV_W
COL_VS = COL_VC + KV_W
COL_VW = COL_VS + KV_W
COL_HQ = COL_VW + KV_W
COL_HF = COL_HQ + ATT_W
COL_HI = COL_HF + ATT_W
COL_HG = COL_HI + ATT_W
PROJ_W = COL_HG + ATT_W
ROPE_W = COL_VC

MASK_BIG = float(2 ** 30)
LOG2E = 1.4426950408889634
ATT_SCALE = HEAD_DIM ** -0.5

_NT = (((1,), (1,)), ((), ()))
_TN = (((0,), (0,)), ((), ()))


def _cparams(sem):
    return pltpu.CompilerParams(dimension_semantics=sem, vmem_limit_bytes=VMEM_LIMIT)


def _dot(a, b, **kw):
    return jnp.dot(a, b, preferred_element_type=F32, **kw)


def _dot_nt(a, b):
    return lax.dot_general(a, b, _NT, preferred_element_type=F32)


def _sigmoid(x):
    return 1.0 / (1.0 + jnp.exp(-x))


def _silu(x):
    return x * _sigmoid(x)


def _rms(x, g):
    return x * lax.rsqrt(jnp.mean(x * x, axis=-1, keepdims=True) + EPS) * g


def _lane_col(x, lane, col):
    return jnp.sum(jnp.where(lane == col, x, 0.0), axis=-1, keepdims=True)


def _ada_kernel(c_ref, w_ref, b_ref, o_ref):
    s = _silu(c_ref[...])
    o_ref[...] = _dot(s, w_ref[...], precision=lax.Precision.HIGHEST) + b_ref[...]


def _ada_mod(c8, w_ada, b_ada, tn=512):
    rows, d = c8.shape
    n = w_ada.shape[1]
    return pl.pallas_call(
        _ada_kernel,
        grid=(n // tn,),
        in_specs=[pl.BlockSpec((rows, d), lambda j: (0, 0)),
                  pl.BlockSpec((d, tn), lambda j: (0, j)),
                  pl.BlockSpec((1, tn), lambda j: (0, j))],
        out_specs=pl.BlockSpec((rows, tn), lambda j: (0, j)),
        out_shape=jax.ShapeDtypeStruct((rows, n), F32),
        compiler_params=_cparams(("parallel",)),
        name="ada_mod",
    )(c8, w_ada, b_ada.reshape(1, n))


def _rope_kernel(ang_ref, cos_ref, sin_ref):
    a = ang_ref[...]
    lane = lax.broadcasted_iota(jnp.int32, a.shape, 1)
    cos_ref[...] = jnp.cos(a)
    sin_ref[...] = jnp.where(lane < HEAD_DIM // 2, -jnp.sin(a), jnp.sin(a))


def _rope_tables(positions, tm):
    t = positions.size
    inv_freq = ROPE_THETA ** (-jnp.arange(0, HEAD_DIM, 2, dtype=F32) / HEAD_DIM)
    ang = positions.reshape(t, 1).astype(F32) * jnp.concatenate([inv_freq, inv_freq])[None, :]
    spec = pl.BlockSpec((tm, HEAD_DIM), lambda i: (i, 0))
    return pl.pallas_call(
        _rope_kernel,
        grid=(t // tm,),
        in_specs=[spec],
        out_specs=[spec, spec],
        out_shape=[jax.ShapeDtypeStruct((t, HEAD_DIM), F32)] * 2,
        compiler_params=_cparams(("parallel",)),
        name="rope_tables",
    )(ang)


def _inproj_kernel(x_ref, mod_ref, g_ref, w_ref, wg_ref, cos_ref, sin_ref,
                   proj_ref, gates_ref, h_s, *, tn):
    j = pl.program_id(1)

    @pl.when(j == 0)
    def _():
        m = mod_ref[0]
        h = _rms(x_ref[...], g_ref[...]) * (1.0 + m[1:2, :]) + m[0:1, :]
        hb = h.astype(BF16)
        h_s[...] = hb
        gates_ref[...] = _dot(hb, wg_ref[...])

    acc = _dot(h_s[...], w_ref[...])
    heads = tn // HEAD_DIM
    n_rope_tiles = ROPE_W // tn
    rem_heads = (ROPE_W % tn) // HEAD_DIM

    def store(n_rot):
        for c in range(heads):
            a = acc[:, c * HEAD_DIM:(c + 1) * HEAD_DIM]
            if c < n_rot:
                a = a * cos_ref[...] + pltpu.roll(a, HEAD_DIM // 2, axis=1) * sin_ref[...]
            proj_ref[:, c * HEAD_DIM:(c + 1) * HEAD_DIM] = a.astype(BF16)

    @pl.when(j < n_rope_tiles)
    def _():
        store(heads)

    if rem_heads:
        @pl.when(j == n_rope_tiles)
        def _():
            store(rem_heads)

    @pl.when(j >= n_rope_tiles + (1 if rem_heads else 0))
    def _():
        store(0)


def _inproj(x2, mod3, g_pre, w_in_p, w_gates, cos_t, sin_t, seq, tm, tn=512):
    t, d = x2.shape
    per_b = seq // tm
    return pl.pallas_call(
        functools.partial(_inproj_kernel, tn=tn),
        grid=(t // tm, PROJ_W // tn),
        in_specs=[pl.BlockSpec((tm, d), lambda i, j: (i, 0)),
                  pl.BlockSpec((1, 6, d), lambda i, j: (i // per_b, 0, 0)),
                  pl.BlockSpec((1, d), lambda i, j: (0, 0)),
                  pl.BlockSpec((d, tn), lambda i, j: (0, j)),
                  pl.BlockSpec((d, LANES), lambda i, j: (0, 0)),
                  pl.BlockSpec((tm, HEAD_DIM), lambda i, j: (i, 0)),
                  pl.BlockSpec((tm, HEAD_DIM), lambda i, j: (i, 0))],
        out_specs=[pl.BlockSpec((tm, tn), lambda i, j: (i, j)),
                   pl.BlockSpec((tm, LANES), lambda i, j: (i, 0))],
        out_shape=[jax.ShapeDtypeStruct((t, PROJ_W), BF16),
                   jax.ShapeDtypeStruct((t, LANES), F32)],
        scratch_shapes=[pltpu.VMEM((tm, d), BF16)],
        compiler_params=_cparams(("parallel", "arbitrary")),
        name="inproj",
    )(x2, mod3, g_pre, w_in_p, w_gates, cos_t, sin_t)


def _compress_kernel(h_ref, w1_ref, pe_ref, w2_ref, o_ref):
    hb = h_ref[0]
    half = hb.shape[1]
    ya = _dot(hb, w1_ref[0:half, :])
    yb = _dot(hb, w1_ref[half:2 * half, :])
    const = _dot(pe_ref[...], w1_ref[...])
    n = ya.shape[0]
    yb_next = pltpu.roll(yb, n - 1, axis=0)
    hid = _silu(ya + yb_next + const[0:1, :])
    out = _dot(hid.astype(BF16), w2_ref[...])
    row = lax.broadcasted_iota(jnp.int32, out.shape, 0)
    o_ref[0] = jnp.where(row < n - 1, out, 0.0).astype(BF16)


def _compress(hblk, w1, pe, w2):
    bg, n, half = hblk.shape
    pe8 = jnp.zeros((8, 2 * half), BF16).at[0].set(pe.reshape(-1).astype(BF16))
    return pl.pallas_call(
        _compress_kernel,
        grid=(bg,),
        in_specs=[pl.BlockSpec((1, n, half), lambda i: (i, 0, 0)),
                  pl.BlockSpec((2 * half, HEAD_DIM), lambda i: (0, 0)),
                  pl.BlockSpec((8, 2 * half), lambda i: (0, 0)),
                  pl.BlockSpec((HEAD_DIM, HEAD_DIM), lambda i: (0, 0))],
        out_specs=pl.BlockSpec((1, n, HEAD_DIM), lambda i: (i, 0, 0)),
        out_shape=jax.ShapeDtypeStruct((bg, n, HEAD_DIM), BF16),
        compiler_params=_cparams(("parallel",)),
        name="compress",
    )(hblk, w1.astype(BF16), pe8, w2.astype(BF16))


def _cmp_kernel(q_ref, kc_ref, vc_ref, gates_ref, ov_ref, ocmp_ref, selb_ref, *, tq, nsel):
    g = pl.program_id(0) % N_KV_GROUPS
    q0 = pl.program_id(1) * tq
    kc = kc_ref[0]
    vc = vc_ref[0]
    nc = kc.shape[0]
    t_idx = q0 + lax.broadcasted_iota(jnp.int32, (tq, nc), 0)
    c_idx = lax.broadcasted_iota(jnp.int32, (tq, nc), 1)
    c_ok = c_idx * CMP_STRIDE + (CMP_BLOCK - 1) <= t_idx
    gl = gates_ref[...]
    lane = lax.broadcasted_iota(jnp.int32, (tq, LANES), 1)
    psum = jnp.zeros((tq, nc), F32)
    for h in range(HEADS_PER_GROUP):
        qh = q_ref[:, h * HEAD_DIM:(h + 1) * HEAD_DIM]
        s = jnp.where(c_ok, _dot_nt(qh, kc) * ATT_SCALE, -jnp.inf)
        m = jnp.max(s, axis=-1, keepdims=True)
        m = jnp.where(m == -jnp.inf, 0.0, m)
        p = jnp.exp(s - m)
        p = p / jnp.maximum(jnp.sum(p, axis=-1, keepdims=True), 1e-30)
        psum = psum + p
        o = _dot(p.astype(BF16), vc)
        glog = _lane_col(gl, lane, (g * HEADS_PER_GROUP + h) * 3)
        ocmp_ref[:, h * HEAD_DIM:(h + 1) * HEAD_DIM] = o * _sigmoid(glog)

    imp = _dot(psum, ov_ref[...], precision=lax.Precision.HIGHEST)
    cur = (q0 + lax.broadcasted_iota(jnp.int32, (tq, LANES), 0)) // SEL_BLOCK
    forced = (lane == 0) | (lane == cur) | (lane == cur - 1)
    imp = jnp.where(lane <= cur, imp + jnp.where(forced, FORCE_BONUS, 0.0), -jnp.inf)
    imp_t = imp.T[0:nsel, :]
    row = lax.broadcasted_iota(jnp.int32, (nsel, tq), 0)
    cnt = jnp.zeros((nsel, tq), F32)
    for k in range(nsel):
        vk = imp_t[k:k + 1, :]
        cnt = cnt + jnp.where(vk > imp_t, 1.0, 0.0)
        cnt = cnt + jnp.where((vk == imp_t) & (row > k), 1.0, 0.0)
    sel = (cnt < float(N_SEL)) & (imp_t > -jnp.inf)
    bias_t = jnp.where(sel, 0.0, -MASK_BIG)
    if nsel < LANES:
        bias_t = jnp.concatenate([bias_t, jnp.zeros((LANES - nsel, tq), F32)], axis=0)
    selb_ref[...] = bias_t.T.astype(BF16)


def _cmp_attn(proj, kc, vc, gates, batch, seq, tq):
    t = proj.shape[0]
    nq = seq // tq
    nc = kc.shape[1]
    nsel = seq // SEL_BLOCK
    assert nsel <= LANES and nsel % 8 == 0
    c_start = np.arange(nc) * CMP_STRIDE
    j_start = np.arange(LANES) * SEL_BLOCK
    ov = ((c_start[:, None] < j_start[None, :] + SEL_BLOCK)
          & (c_start[:, None] + CMP_BLOCK > j_start[None, :])
          & (np.arange(LANES)[None, :] < nsel)).astype(np.float32)
    gq = HEADS_PER_GROUP * HEAD_DIM
    qmap = lambda bg, i: ((bg // N_KV_GROUPS) * nq + i, bg % N_KV_GROUPS)
    return pl.pallas_call(
        functools.partial(_cmp_kernel, tq=tq, nsel=nsel),
        grid=(batch * N_KV_GROUPS, nq),
        in_specs=[pl.BlockSpec((tq, gq), qmap),
                  pl.BlockSpec((1, nc, HEAD_DIM), lambda bg, i: (bg, 0, 0)),
                  pl.BlockSpec((1, nc, HEAD_DIM), lambda bg, i: (bg, 0, 0)),
                  pl.BlockSpec((tq, LANES), lambda bg, i: ((bg // N_KV_GROUPS) * nq + i, 0)),
                  pl.BlockSpec((nc, LANES), lambda bg, i: (0, 0))],
        out_specs=[pl.BlockSpec((tq, gq), qmap),
                   pl.BlockSpec((tq, LANES), lambda bg, i: (bg * nq + i, 0))],
        out_shape=[jax.ShapeDtypeStruct((t, ATT_W), F32),
                   jax.ShapeDtypeStruct((batch * N_KV_GROUPS * seq, LANES), BF16)],
        compiler_params=_cparams(("parallel", "parallel")),
        name="cmp_attn",
    )(proj, kc, vc, gates, jnp.asarray(ov))


def _selwin_kernel(q_ref, selb_ref, ocmp_ref, gates_ref, ks_ref, vs_ref, kw_ref, vw_ref,
                   o_ref, kaug_s, vaug_s, qaug_s, *, tq, tk, seq, nwin):
    g = pl.program_id(0) % N_KV_GROUPS
    qi = pl.program_id(1)
    q0 = qi * tq
    hq = HEADS_PER_GROUP * tq
    c2 = ATT_SCALE * LOG2E

    @pl.when(qi == 0)
    def _():
        r = lax.broadcasted_iota(jnp.int32, (seq, HEAD_DIM), 0)
        ln = lax.broadcasted_iota(jnp.int32, (seq, HEAD_DIM), 1)
        kaug_s[:, 0:HEAD_DIM] = ks_ref[...]
        kaug_s[:, HEAD_DIM:2 * HEAD_DIM] = jnp.where(r // SEL_BLOCK == ln, 1.0, 0.0).astype(BF16)
        vaug_s[:, 0:HEAD_DIM] = vs_ref[...]
        vaug_s[:, HEAD_DIM:2 * HEAD_DIM] = jnp.where(ln == 0, 1.0, 0.0).astype(BF16)

    for h in range(HEADS_PER_GROUP):
        qaug_s[h * tq:(h + 1) * tq, 0:HEAD_DIM] = q_ref[:, h * HEAD_DIM:(h + 1) * HEAD_DIM]
        qaug_s[h * tq:(h + 1) * tq, HEAD_DIM:2 * HEAD_DIM] = selb_ref[...]
    qa = qaug_s[...]

    def step(kt, carry, causal):
        m, acc = carry
        ks = kaug_s[pl.ds(pl.multiple_of(kt * tk, tk), tk), :]
        vs = vaug_s[pl.ds(pl.multiple_of(kt * tk, tk), tk), :]
        s = _dot_nt(qa, ks)
        if causal:
            t_pos = q0 + (lax.broadcasted_iota(jnp.int32, (hq, tk), 0) & (tq - 1))
            k_pos = kt * tk + lax.broadcasted_iota(jnp.int32, (hq, tk), 1)
            s = jnp.where(k_pos <= t_pos, s, -jnp.inf)
        m_new = jnp.maximum(m, jnp.max(s, axis=-1, keepdims=True))
        alpha = jnp.exp2((m - m_new) * c2)
        p = jnp.exp2((s - m_new) * c2)
        acc = alpha * acc + _dot(p.astype(BF16), vs)
        return m_new, acc

    n_full = q0 // tk
    carry = (jnp.full((hq, 1), -jnp.inf, F32), jnp.zeros((hq, 2 * HEAD_DIM), F32))
    carry = lax.fori_loop(0, n_full, lambda kt, c: step(kt, c, False), carry)
    _, acc = step(n_full, carry, True)
    o_sel = acc[:, 0:HEAD_DIM] / acc[:, HEAD_DIM:HEAD_DIM + 1]

    nq = seq // tq
    w0 = jnp.clip(qi - (nwin - 1), 0, nq - nwin) * tq
    wlen = nwin * tq
    kwin = kw_ref[pl.ds(pl.multiple_of(w0, tq), wlen), :]
    vwin = vw_ref[pl.ds(pl.multiple_of(w0, tq), wlen), :]
    sw = _dot_nt(qaug_s[:, 0:HEAD_DIM], kwin)
    t_pos = q0 + (lax.broadcasted_iota(jnp.int32, (hq, wlen), 0) & (tq - 1))
    dist = t_pos - (w0 + lax.broadcasted_iota(jnp.int32, (hq, wlen), 1))
    sw = jnp.where((dist >= 0) & (dist < WINDOW), sw, -jnp.inf)
    mw = jnp.max(sw, axis=-1, keepdims=True)
    pw = jnp.exp2((sw - mw) * c2)
    lw = jnp.sum(pw, axis=-1, keepdims=True)
    o_win = _dot(pw.astype(BF16), vwin) / lw

    gl = gates_ref[...]
    lane = lax.broadcasted_iota(jnp.int32, (tq, LANES), 1)
    for h in range(HEADS_PER_GROUP):
        gcol = (g * HEADS_PER_GROUP + h) * 3
        g_s = _sigmoid(_lane_col(gl, lane, gcol + 1))
        g_w = _sigmoid(_lane_col(gl, lane, gcol + 2))
        o = (ocmp_ref[:, h * HEAD_DIM:(h + 1) * HEAD_DIM]
             + g_s * o_sel[h * tq:(h + 1) * tq, :] + g_w * o_win[h * tq:(h + 1) * tq, :])
        o_ref[:, h * HEAD_DIM:(h + 1) * HEAD_DIM] = o.astype(BF16)


def _selwin_attn(proj, selb, ocmp, gates, batch, seq, tq, tk):
    t = proj.shape[0]
    nq = seq // tq
    assert tk % tq == 0 and tq & (tq - 1) == 0 and seq % tk == 0
    nwin = min(WINDOW // tq + 1, nq)
    assert WINDOW % tq == 0
    gq = HEADS_PER_GROUP * HEAD_DIM
    qmap = lambda bg, i: ((bg // N_KV_GROUPS) * nq + i, bg % N_KV_GROUPS)

    def kvspec(col):
        return pl.BlockSpec((seq, HEAD_DIM),
                            lambda bg, i: (bg // N_KV_GROUPS, col // HEAD_DIM + bg % N_KV_GROUPS))

    return pl.pallas_call(
        functools.partial(_selwin_kernel, tq=tq, tk=tk, seq=seq, nwin=nwin),
        grid=(batch * N_KV_GROUPS, nq),
        in_specs=[pl.BlockSpec((tq, gq), qmap),
                  pl.BlockSpec((tq, LANES), lambda bg, i: (bg * nq + i, 0)),
                  pl.BlockSpec((tq, gq), qmap),
                  pl.BlockSpec((tq, LANES), lambda bg, i: ((bg // N_KV_GROUPS) * nq + i, 0)),
                  kvspec(COL_KS), kvspec(COL_VS), kvspec(COL_KW), kvspec(COL_VW)],
        out_specs=pl.BlockSpec((tq, gq), qmap),
        out_shape=jax.ShapeDtypeStruct((t, ATT_W), BF16),
        scratch_shapes=[pltpu.VMEM((seq, 2 * HEAD_DIM), BF16),
                        pltpu.VMEM((seq, 2 * HEAD_DIM), BF16),
                        pltpu.VMEM((HEADS_PER_GROUP * tq, 2 * HEAD_DIM), BF16)],
        compiler_params=_cparams(("parallel", "arbitrary")),
        name="selwin_attn",
    )(proj, selb, ocmp, gates, proj, proj, proj, proj)


_HGRN_LEVELS = (32, 16, 8, 4, 2, 1)


def _hgrn_tables():
    c = HGRN_CHUNK
    idx = np.arange(c)
    le = idx[None, :] <= idx[:, None]
    gt = idx[None, :] > idx[:, None]
    mats = [le, gt]
    masks = []
    for s in _HGRN_LEVELS:
        blk = idx // s
        mats.append((idx[None, :] >= (blk * s)[:, None]) & le)
        mats.append(gt & (idx[None, :] <= (blk * s + s - 1)[:, None]))
        masks.append(((idx[:, None] // (2 * s)) == (idx[None, :] // (2 * s)))
                     & ((blk % 2) == 1)[:, None] & ((blk % 2) == 0)[None, :])
    masks.append(idx[:, None] == idx[None, :])
    return (np.concatenate(mats, 0).astype(np.float32),
            np.stack(masks, 0).astype(np.float32))


def _hgrn_kernel(hq_ref, hf_ref, hi_ref, hg_ref, lbl_ref, gn_ref, w_ref, mk_ref,
                 o_ref, st_s, *, ts):
    c = HGRN_CHUNK

    @pl.when(pl.program_id(2) == 0)
    def _():
        st_s[...] = jnp.zeros_like(st_s)

    lbl = lbl_ref[...]
    e = jnp.exp(lbl - jnp.max(lbl, axis=0, keepdims=True))
    lb = e[0:1, :] / jnp.sum(e, axis=0, keepdims=True)
    gn = gn_ref[...]
    wmat = w_ref[...]

    def chunk(ci, carry):
        sl = pl.ds(pl.multiple_of(ci * c, c), c)
        f = lb + (1.0 - lb) * _sigmoid(hf_ref[sl, :].astype(F32))
        lf = jnp.log(f)
        k = 1.0 - f
        qf = _silu(hq_ref[sl, :].astype(F32))
        v = hi_ref[sl, :]
        lf_a = lf.astype(BF16)
        r1 = lf - lf_a.astype(F32)
        lf_b = r1.astype(BF16)
        lf_c = (r1 - lf_b.astype(F32)).astype(BF16)
        ex = _dot(wmat, lf_a) + _dot(wmat, lf_b) + _dot(wmat, lf_c)
        b = ex[0:c, :]
        qhat = (qf * jnp.exp(b)).astype(BF16)
        khat = (k * jnp.exp(ex[c:2 * c, :])).astype(BF16)
        st = st_s[...]
        a = _dot_nt(qf.astype(BF16), k.astype(BF16)) * mk_ref[len(_HGRN_LEVELS)]
        for li in range(len(_HGRN_LEVELS)):
            er = ex[(2 + 2 * li) * c:(3 + 2 * li) * c, :]
            ec = ex[(3 + 2 * li) * c:(4 + 2 * li) * c, :]
            a = a + _dot_nt((qf * jnp.exp(er)).astype(BF16),
                            (k * jnp.exp(ec)).astype(BF16)) * mk_ref[li]
        o = _dot_nt(qhat, st.astype(BF16)) + _dot(a.astype(BF16), v)
        st_s[...] = st * jnp.exp(b[c - 1:c, :]) + lax.dot_general(
            v, khat, _TN, preferred_element_type=F32)
        o = o * lax.rsqrt(jnp.mean(o * o, axis=-1, keepdims=True) + EPS) * gn
        o_ref[sl, :] = (o * _silu(hg_ref[sl, :].astype(F32))).astype(BF16)
        return carry

    lax.fori_loop(0, ts // c, chunk, 0)


def _hgrn(proj, lb_logits, g_norm, batch, seq, ts):
    t = proj.shape[0]
    ns = seq // ts
    wmat, masks = _hgrn_tables()

    def colspec(col):
        return pl.BlockSpec((ts, HEAD_DIM), lambda b, h, s: (b * ns + s, col // HEAD_DIM + h))

    nl = lb_logits.shape[0]
    return pl.pallas_call(
        functools.partial(_hgrn_kernel, ts=ts),
        grid=(batch, N_HGRN_HEADS, ns),
        in_specs=[colspec(COL_HQ), colspec(COL_HF), colspec(COL_HI), colspec(COL_HG),
                  pl.BlockSpec((nl, HEAD_DIM), lambda b, h, s: (0, h)),
                  pl.BlockSpec((1, HEAD_DIM), lambda b, h, s: (0, h)),
                  pl.BlockSpec(wmat.shape, lambda b, h, s: (0, 0)),
                  pl.BlockSpec(masks.shape, lambda b, h, s: (0, 0, 0))],
        out_specs=pl.BlockSpec((ts, HEAD_DIM), lambda b, h, s: (b * ns + s, h)),
        out_shape=jax.ShapeDtypeStruct((t, ATT_W), BF16),
        scratch_shapes=[pltpu.VMEM((HEAD_DIM, HEAD_DIM), F32)],
        compiler_params=_cparams(("parallel", "parallel", "arbitrary")),
        name="hgrn2",
    )(proj, proj, proj, proj, lb_logits, g_norm, jnp.asarray(wmat, BF16), jnp.asarray(masks))


def _outproj_kernel(oa_ref, or_ref, wa_ref, wb_ref, x_ref, mod_ref, gpost_ref, gpre_ref,
                    wrh_ref, wrl_ref, br_ref, x1_ref, h2_ref, lg_ref):
    m = mod_ref[0]
    y = _dot(oa_ref[...], wa_ref[...]) + _dot(or_ref[...], wb_ref[...])
    x1 = x_ref[...] + m[2:3, :] * _rms(y, gpost_ref[...])
    x1_ref[...] = x1
    h = _rms(x1, gpre_ref[...]) * (1.0 + m[4:5, :]) + m[3:4, :]
    h2_ref[...] = h
    h_hi = h.astype(BF16)
    h_lo = (h - h_hi.astype(F32)).astype(BF16)
    lg_ref[...] = (_dot(h_hi, wrh_ref[...]) + _dot(h_lo, wrh_ref[...])
                   + _dot(h_hi, wrl_ref[...]) + br_ref[...])


def _outproj(o_att, o_rec, w_out, x2, mod3, g_post, g_pre, wr, br, seq, tm):
    t, d = x2.shape
    per_b = seq // tm
    wr_hi = wr.astype(BF16)
    wr_lo = (wr - wr_hi.astype(F32)).astype(BF16)
    wa = w_out[:ATT_W].astype(BF16)
    wb = w_out[ATT_W:].astype(BF16)
    row = lambda i: (i, 0)
    fixed = lambda i: (0, 0)
    return pl.pallas_call(
        _outproj_kernel,
        grid=(t // tm,),
        in_specs=[pl.BlockSpec((tm, ATT_W), row), pl.BlockSpec((tm, ATT_W), row),
                  pl.BlockSpec((ATT_W, d), fixed), pl.BlockSpec((ATT_W, d), fixed),
                  pl.BlockSpec((tm, d), row),
                  pl.BlockSpec((1, 6, d), lambda i: (i // per_b, 0, 0)),
                  pl.BlockSpec((1, d), fixed), pl.BlockSpec((1, d), fixed),
                  pl.BlockSpec((d, LANES), fixed), pl.BlockSpec((d, LANES), fixed),
                  pl.BlockSpec((1, LANES), fixed)],
        out_specs=[pl.BlockSpec((tm, d), row), pl.BlockSpec((tm, d), row),
                   pl.BlockSpec((tm, LANES), row)],
        out_shape=[jax.ShapeDtypeStruct((t, d), F32), jax.ShapeDtypeStruct((t, d), F32),
                   jax.ShapeDtypeStruct((t, LANES), F32)],
        compiler_params=_cparams(("parallel",)),
        name="outproj",
    )(o_att, o_rec, wa, wb, x2, mod3, g_post, g_pre, wr_hi, wr_lo, br)


G_LANE0 = N_EXPERTS


def _route_kernel(lg_ref, tri_ref, info_ref, cnt_ref, carry_s):
    @pl.when(pl.program_id(0) == 0)
    def _():
        carry_s[...] = jnp.zeros_like(carry_s)

    x = lg_ref[...]
    lane = lax.broadcasted_iota(jnp.int32, x.shape, 1)
    big = jnp.int32(10 ** 6)
    rmax = lambda a: jnp.max(a, axis=-1, keepdims=True)
    rmin = lambda a: jnp.min(a, axis=-1, keepdims=True)
    rsum = lambda a: jnp.sum(a, axis=-1, keepdims=True)

    is_g = (lane >= G_LANE0) & (lane < G_LANE0 + N_EXPERT_GROUPS)
    lgm = jnp.where(is_g, x, -jnp.inf)
    mg = rmax(lgm)
    pg_sel = 1.0 / rsum(jnp.where(is_g, jnp.exp(lgm - mg), 0.0))
    gsel = rmin(jnp.where(lgm == mg, lane, big)) - G_LANE0

    is_e = (lane >= gsel * EXPERTS_PER_GROUP) & (lane < (gsel + 1) * EXPERTS_PER_GROUP)
    lem = jnp.where(is_e, x, -jnp.inf)
    pe = jnp.where(is_e, jnp.exp(lem - rmax(lem)), 0.0)
    pe = pe / rsum(pe)
    pe = jnp.where(is_e, pe, -1.0)
    v1 = rmax(pe)
    i1 = rmin(jnp.where(pe == v1, lane, big))
    pe2 = jnp.where(lane == i1, -1.0, pe)
    v2 = rmax(pe2)
    i2 = rmin(jnp.where(pe2 == v2, lane, big))
    w1 = v1 / (v1 + v2) * pg_sel
    w2 = v2 / (v1 + v2) * pg_sel

    oh1 = jnp.where(lane == i1, 1.0, 0.0)
    oh2 = jnp.where(lane == i2, 1.0, 0.0)
    both = oh1 + oh2
    before = _dot(tri_ref[...], both.astype(BF16)) + carry_s[0:1, :]
    r1 = rsum(oh1 * before)
    r2 = rsum(oh2 * before)
    carry_s[0:1, :] = carry_s[0:1, :] + jnp.sum(both, axis=0, keepdims=True)
    cnt_ref[...] = carry_s[...]

    cols = (i1.astype(F32), i2.astype(F32), w1, w2, r1, r2)
    info = jnp.zeros(x.shape, F32)
    for ci, col in enumerate(cols):
        info = jnp.where(lane == ci, col, info)
    info_ref[...] = info


def _route(logits, tm):
    t = logits.shape[0]
    tri = jnp.asarray(np.tril(np.ones((tm, tm), np.float32), -1), BF16)
    return pl.pallas_call(
        _route_kernel,
        grid=(t // tm,),
        in_specs=[pl.BlockSpec((tm, LANES), lambda i: (i, 0)),
                  pl.BlockSpec((tm, tm), lambda i: (0, 0))],
        out_specs=[pl.BlockSpec((tm, LANES), lambda i: (i, 0)),
                   pl.BlockSpec((8, LANES), lambda i: (0, 0))],
        out_shape=[jax.ShapeDtypeStruct((t, LANES), F32),
                   jax.ShapeDtypeStruct((8, LANES), F32)],
        scratch_shapes=[pltpu.VMEM((8, LANES), F32)],
        compiler_params=_cparams(("arbitrary",)),
        name="route",
    )(logits, tri)


def _row_copy(src, i, dst, j, sem):
    return pltpu.make_async_copy(src.at[pl.ds(i, 1)], dst.at[pl.ds(j, 1)], sem)


def _dispatch_kernel(pos_ref, h_ref, xs_in, xs_out, sem, *, tm):
    del xs_in

    def start(t, c):
        _row_copy(h_ref, t, xs_out, pos_ref[0, 0, 2 * t], sem).start()
        _row_copy(h_ref, t, xs_out, pos_ref[0, 0, 2 * t + 1], sem).start()
        return c

    def wait(t, c):
        _row_copy(h_ref, 0, xs_out, 0, sem).wait()
        _row_copy(h_ref, 0, xs_out, 0, sem).wait()
        return c

    lax.fori_loop(0, tm, start, 0)
    lax.fori_loop(0, tm, wait, 0)


def _dispatch(pos3, h2, xs0, tm):
    t, d = h2.shape
    return pl.pallas_call(
        functools.partial(_dispatch_kernel, tm=tm),
        grid=(t // tm,),
        in_specs=[pl.BlockSpec((1, 1, 2 * tm), lambda i: (i, 0, 0), memory_space=pltpu.SMEM),
                  pl.BlockSpec((tm, d), lambda i: (i, 0)),
                  pl.BlockSpec(memory_space=pl.ANY)],
        out_specs=pl.BlockSpec(memory_space=pl.ANY),
        out_shape=jax.ShapeDtypeStruct(xs0.shape, xs0.dtype),
        input_output_aliases={2: 0},
        scratch_shapes=[pltpu.SemaphoreType.DMA(())],
        compiler_params=_cparams(("arbitrary",)),
        name="moe_dispatch",
    )(pos3, h2, xs0)


def _expert_kernel(te_ref, nu_ref, xs_ref, wg_ref, wu_ref, wd_ref, ys_ref, wg_s, wu_s, wd_s):
    t = pl.program_id(0)

    @pl.when(t < nu_ref[0])
    def _():
        prev = te_ref[jnp.maximum(t - 1, 0)]

        @pl.when((t == 0) | (te_ref[t] != prev))
        def _():
            wg_s[...] = wg_ref[0].astype(BF16)
            wu_s[...] = wu_ref[0].astype(BF16)
            wd_s[...] = wd_ref[0].astype(BF16)

        x = xs_ref[...].astype(BF16)
        hid = _silu(_dot(x, wg_s[...])) * _dot(x, wu_s[...])
        ys_ref[...] = _dot(hid.astype(BF16), wd_s[...])

    @pl.when(t >= nu_ref[0])
    def _():
        ys_ref[...] = jnp.zeros_like(ys_ref)


def _experts(tile_expert, n_used, xs, w_gate, w_up, w_down, tm):
    p, d = xs.shape
    f = w_gate.shape[-1]
    n_tiles = p // tm
    wmap = lambda t, te, nu: (te[t], 0, 0)
    return pl.pallas_call(
        _expert_kernel,
        grid_spec=pltpu.PrefetchScalarGridSpec(
            num_scalar_prefetch=2,
            grid=(n_tiles,),
            in_specs=[pl.BlockSpec((tm, d), lambda t, te, nu: (t, 0)),
                      pl.BlockSpec((1, d, f), wmap),
                      pl.BlockSpec((1, d, f), wmap),
                      pl.BlockSpec((1, f, d), wmap)],
            out_specs=pl.BlockSpec((tm, d), lambda t, te, nu: (t, 0)),
            scratch_shapes=[pltpu.VMEM((d, f), BF16), pltpu.VMEM((d, f), BF16),
                            pltpu.VMEM((f, d), BF16)]),
        out_shape=jax.ShapeDtypeStruct((p, d), F32),
        compiler_params=_cparams(("arbitrary",)),
        name="moe_experts",
    )(tile_expert, n_used, xs, w_gate, w_up, w_down)


def _combine_kernel(pos_ref, info_ref, x1_ref, mod_ref, g_ref, ys_ref, o_ref, buf, sem, *, tm):
    def start(t, c):
        _row_copy(ys_ref, pos_ref[0, 0, 2 * t], buf.at[0], t, sem).start()
        _row_copy(ys_ref, pos_ref[0, 0, 2 * t + 1], buf.at[1], t, sem).start()
        return c

    def wait(t, c):
        _row_copy(ys_ref, 0, buf.at[0], 0, sem).wait()
        _row_copy(ys_ref, 0, buf.at[1], 0, sem).wait()
        return c

    lax.fori_loop(0, tm, start, 0)
    lax.fori_loop(0, tm, wait, 0)
    info = info_ref[...]
    y = info[:, 2:3] * buf[0] + info[:, 3:4] * buf[1]
    o_ref[...] = x1_ref[...] + mod_ref[0][5:6, :] * _rms(y, g_ref[...])


def _combine(pos3, info, x1, mod3, g_post, ys, seq, tm):
    t, d = x1.shape
    per_b = seq // tm
    return pl.pallas_call(
        functools.partial(_combine_kernel, tm=tm),
        grid=(t // tm,),
        in_specs=[pl.BlockSpec((1, 1, 2 * tm), lambda i: (i, 0, 0), memory_space=pltpu.SMEM),
                  pl.BlockSpec((tm, LANES), lambda i: (i, 0)),
                  pl.BlockSpec((tm, d), lambda i: (i, 0)),
                  pl.BlockSpec((1, 6, d), lambda i: (i // per_b, 0, 0)),
                  pl.BlockSpec((1, d), lambda i: (0, 0)),
                  pl.BlockSpec(memory_space=pl.ANY)],
        out_specs=pl.BlockSpec((tm, d), lambda i: (i, 0)),
        out_shape=jax.ShapeDtypeStruct((t, d), F32),
        scratch_shapes=[pltpu.VMEM((2, tm, d), F32), pltpu.SemaphoreType.DMA(())],
        compiler_params=_cparams(("arbitrary",)),
        name="moe_combine",
    )(pos3, info, x1, mod3, g_post, ys)


def _moe(h2, logits, x1, mod3, g_post, w_gate, w_up, w_down, seq, tm_route, tm_exp, tm_row):
    t, d = h2.shape
    info, cnt = _route(logits, tm_route)
    counts = cnt[0, :N_EXPERTS].astype(jnp.int32)
    tiles_e = (counts + tm_exp - 1) // tm_exp
    tile_end = jnp.cumsum(tiles_e)
    offs = (tile_end - tiles_e) * tm_exp
    n_used = tile_end[-1]
    e12 = info[:, 0:2].astype(jnp.int32)
    pos = offs[e12] + info[:, 4:6].astype(jnp.int32)
    n_tiles = (2 * t) // tm_exp + N_EXPERTS
    tidx = jnp.minimum(jnp.arange(n_tiles, dtype=jnp.int32), n_used - 1)
    tile_expert = jnp.sum(tidx[:, None] >= tile_end[None, :], axis=1).astype(jnp.int32)
    pos3 = pos.reshape(t // tm_row, 1, 2 * tm_row)

    ew = w_gate.shape
    wg = w_gate.reshape(N_EXPERTS, ew[-2], ew[-1])
    wu = w_up.reshape(N_EXPERTS, ew[-2], ew[-1])
    wd = w_down.reshape(N_EXPERTS, ew[-1], ew[-2])
    xs = _dispatch(pos3, h2, jnp.zeros((n_tiles * tm_exp, d), F32), tm_row)
    ys = _experts(tile_expert, n_used.reshape(1).astype(jnp.int32), xs, wg, wu, wd, tm_exp)
    return _combine(pos3, info, x1, mod3, g_post, ys, seq, tm_row)


def _permute_w_in(w_in):
    sizes = [ATT_W] + [KV_W] * 6 + [3 * N_ATT_HEADS] + [ATT_W] * 4
    cuts = np.cumsum(sizes)[:-1]
    q, kc, vc, ks, vs, kw, vw, gates, hq, hf, hi, hg = jnp.split(w_in, cuts, axis=1)
    w_p = jnp.concatenate([q, kc, ks, kw, vc, vs, vw, hq, hf, hi, hg], axis=1).astype(BF16)
    w_g = jnp.pad(gates, ((0, 0), (0, LANES - gates.shape[1]))).astype(BF16)
    return w_p, w_g


def _half_blocks(proj, col, batch, seq):
    a = proj[:, col:col + KV_W].reshape(batch, seq, N_KV_GROUPS, HEAD_DIM)
    a = a.transpose(0, 2, 1, 3)
    return a.reshape(batch * N_KV_GROUPS, seq // CMP_STRIDE, CMP_STRIDE * HEAD_DIM)


def _block(x, c, positions, w_ada, b_ada, g_pre_mix, g_post_mix, g_pre_ffn, g_post_ffn,
           w_in, w_out, pe_k, w1_k, w2_k, pe_v, w1_v, w2_v, lb_logits, g_norm,
           w_group, b_group, w_router, b_router, w_gate, w_up, w_down, cfg):
    batch, seq, d = x.shape
    t = batch * seq
    x2 = x.reshape(t, d)

    c8 = jnp.zeros((8, d), F32).at[:batch].set(c)
    mod3 = _ada_mod(c8, w_ada, b_ada)[:batch].reshape(batch, 6, d)
    cos_t, sin_t = _rope_tables(positions, cfg["tm_rope"])
    w_p, w_g = _permute_w_in(w_in)
    proj, gates = _inproj(x2, mod3, g_pre_mix.reshape(1, d), w_p, w_g, cos_t, sin_t,
                          seq, cfg["tm_in"])

    kc = _compress(_half_blocks(proj, COL_KC, batch, seq), w1_k, pe_k, w2_k)
    vc = _compress(_half_blocks(proj, COL_VC, batch, seq), w1_v, pe_v, w2_v)
    ocmp, selb = _cmp_attn(proj, kc, vc, gates, batch, seq, cfg["tq_cmp"])
    o_att = _selwin_attn(proj, selb, ocmp, gates, batch, seq, cfg["tq"], cfg["tk"])
    o_rec = _hgrn(proj, lb_logits, g_norm.reshape(1, -1), batch, seq, cfg["ts_hgrn"])

    wr = jnp.concatenate([w_router, w_group], axis=1)
    wr = jnp.pad(wr, ((0, 0), (0, LANES - wr.shape[1])))
    br = jnp.pad(jnp.concatenate([b_router, b_group]), (0, LANES - N_EXPERTS - N_EXPERT_GROUPS))
    x1, h2, logits = _outproj(o_att, o_rec, w_out, x2, mod3, g_post_mix.reshape(1, d),
                              g_pre_ffn.reshape(1, d), wr, br.reshape(1, LANES), seq,
                              cfg["tm_out"])
    out = _moe(h2, logits, x1, mod3, g_post_ffn.reshape(1, d), w_gate, w_up, w_down, seq,
               cfg["tm_route"], cfg["tm_exp"], cfg["tm_row"])
    return out.reshape(batch, seq, d)


def _config(seq):
    return dict(tm_rope=min(1024, seq), tm_in=min(1024, seq), tq_cmp=min(256, seq),
                tq=min(256, seq), tk=min(512, seq), ts_hgrn=min(512, seq),
                tm_out=min(256, seq), tm_route=min(256, seq), tm_exp=256,
                tm_row=min(256, seq))


def kernel(x, c, positions, w_ada, b_ada, g_pre_mix, g_post_mix, g_pre_ffn, g_post_ffn, w_in, w_out, cmp_pe_k, cmp_w1_k, cmp_w2_k, cmp_pe_v, cmp_w1_v, cmp_w2_v, hgrn_lb_logits, hgrn_g_norm, w_group, b_group, w_router, b_router, w_gate, w_up, w_down):
    assert w_ada.shape[0] == 1, "single-layer block"
    return _block(x, c, positions, w_ada[0], b_ada[0], g_pre_mix[0], g_post_mix[0],
                  g_pre_ffn[0], g_post_ffn[0], w_in[0], w_out[0], cmp_pe_k[0], cmp_w1_k[0],
                  cmp_w2_k[0], cmp_pe_v[0], cmp_w1_v[0], cmp_w2_v[0], hgrn_lb_logits,
                  hgrn_g_norm[0], w_group[0], b_group[0], w_router[0], b_router[0],
                  w_gate[0], w_up[0], w_down[0], _config(x.shape[1]))
```

```python
import functools

import numpy as np
import jax
import jax.numpy as jnp
from jax import lax
from jax.experimental import pallas as pl
from jax.experimental.pallas import tpu as pltpu

F32 = jnp.float32
BF16 = jnp.bfloat16

HEAD_DIM = 128
N_KV_GROUPS = 2
HEADS_PER_GROUP = 4
N_ATT_HEADS = N_KV_GROUPS * HEADS_PER_GROUP
CMP_BLOCK = 32
CMP_STRIDE = 16
SEL_BLOCK = 64
N_SEL = 16
WINDOW = 512
FORCE_BONUS = 1.0e4
ROPE_THETA = 10000.0
N_HGRN_HEADS = 8
HGRN_CHUNK = 64
N_EXPERT_GROUPS = 4
EXPERTS_PER_GROUP = 8
N_EXPERTS = N_EXPERT_GROUPS * EXPERTS_PER_GROUP
EPS = 1e-6

LANES = 128
VMEM_LIMIT = 56 * 1024 * 1024

ATT_W = N_ATT_HEADS * HEAD_DIM
KV_W = N_KV_GROUPS * HEAD_DIM
COL_Q = 0
COL_KC = COL_Q + ATT_W
COL_KS = COL_KC + KV_W
COL_KW = COL_KS + KV_W
COL_VC = COL_KW + KV_W
COL_VS = COL_VC + KV_W
COL_VW = COL_VS + KV_W
COL_HQ = COL_VW + KV_W
COL_HF = COL_HQ + ATT_W
COL_HI = COL_HF + ATT_W
COL_HG = COL_HI + ATT_W
PROJ_W = COL_HG + ATT_W
ROPE_W = COL_VC

MASK_BIG = float(2 ** 30)
LOG2E = 1.4426950408889634
Q_PRESCALE = HEAD_DIM ** -0.5 * LOG2E

_NT = (((1,), (1,)), ((), ()))
_TN = (((0,), (0,)), ((), ()))


def _cparams(sem):
    return pltpu.CompilerParams(dimension_semantics=sem, vmem_limit_bytes=VMEM_LIMIT)


def _dot(a, b, **kw):
    return jnp.dot(a, b, preferred_element_type=F32, **kw)


def _dot_nt(a, b):
    return lax.dot_general(a, b, _NT, preferred_element_type=F32)


def _sigmoid(x):
    return 1.0 / (1.0 + jnp.exp(-x))


def _silu(x):
    return x * _sigmoid(x)


def _rms(x, g):
    return x * lax.rsqrt(jnp.mean(x * x, axis=-1, keepdims=True) + EPS) * g


def _lane_col(x, lane, col):
    return jnp.sum(jnp.where(lane == col, x, 0.0), axis=-1, keepdims=True)


def _ada_kernel(c_ref, w_ref, b_ref, o_ref):
    s = _silu(c_ref[...])
    o_ref[...] = _dot(s, w_ref[...], precision=lax.Precision.HIGHEST) + b_ref[...]


def _ada_mod(c8, w_ada, b_ada, tn=512):
    rows, d = c8.shape
    n = w_ada.shape[1]
    return pl.pallas_call(
        _ada_kernel,
        grid=(n // tn,),
        in_specs=[pl.BlockSpec((rows, d), lambda j: (0, 0)),
                  pl.BlockSpec((d, tn), lambda j: (0, j)),
                  pl.BlockSpec((1, tn), lambda j: (0, j))],
        out_specs=pl.BlockSpec((rows, tn), lambda j: (0, j)),
        out_shape=jax.ShapeDtypeStruct((rows, n), F32),
        compiler_params=_cparams(("parallel",)),
        name="ada_mod",
    )(c8, w_ada, b_ada.reshape(1, n))


def _rope_kernel(ang_ref, cos_ref, sin_ref):
    a = ang_ref[...]
    lane = lax.broadcasted_iota(jnp.int32, a.shape, 1)
    cos_ref[...] = jnp.cos(a)
    sin_ref[...] = jnp.where(lane < HEAD_DIM // 2, -jnp.sin(a), jnp.sin(a))


def _rope_tables(positions, tm):
    t = positions.size
    inv_freq = ROPE_THETA ** (-jnp.arange(0, HEAD_DIM, 2, dtype=F32) / HEAD_DIM)
    ang = positions.reshape(t, 1).astype(F32) * jnp.concatenate([inv_freq, inv_freq])[None, :]
    spec = pl.BlockSpec((tm, HEAD_DIM), lambda i: (i, 0))
    return pl.pallas_call(
        _rope_kernel,
        grid=(t // tm,),
        in_specs=[spec],
        out_specs=[spec, spec],
        out_shape=[jax.ShapeDtypeStruct((t, HEAD_DIM), F32)] * 2,
        compiler_params=_cparams(("parallel",)),
        name="rope_tables",
    )(ang)


def _inproj_kernel(x_ref, mod_ref, g_ref, w_ref, wg_ref, cos_ref, sin_ref,
                   proj_ref, gates_ref, h_s, *, tn):
    j = pl.program_id(1)

    @pl.when(j == 0)
    def _():
        m = mod_ref[0]
        h = _rms(x_ref[...], g_ref[...]) * (1.0 + m[1:2, :]) + m[0:1, :]
        hb = h.astype(BF16)
        h_s[...] = hb
        gates_ref[...] = _dot(hb, wg_ref[...])

    heads = tn // HEAD_DIM
    n_rope_tiles = ROPE_W // tn
    rem_heads = (ROPE_W % tn) // HEAD_DIM

    def store(n_rot, scale=None):
        acc = _dot(h_s[...], w_ref[...])
        for c in range(heads):
            a = acc[:, c * HEAD_DIM:(c + 1) * HEAD_DIM]
            if c < n_rot:
                a = a * cos_ref[...] + pltpu.roll(a, HEAD_DIM // 2, axis=1) * sin_ref[...]
            if scale is not None:
                a = a * scale
            proj_ref[:, c * HEAD_DIM:(c + 1) * HEAD_DIM] = a.astype(BF16)

    @pl.when(j < ATT_W // tn)
    def _():
        store(heads, Q_PRESCALE)

    @pl.when((j >= ATT_W // tn) & (j < n_rope_tiles))
    def _():
        store(heads)

    if rem_heads:
        @pl.when(j == n_rope_tiles)
        def _():
            store(rem_heads)

    @pl.when(j >= n_rope_tiles + (1 if rem_heads else 0))
    def _():
        store(0)


def _inproj(x2, mod3, g_pre, w_in_p, w_gates, cos_t, sin_t, seq, tm, tn=512):
    t, d = x2.shape
    per_b = seq // tm
    return pl.pallas_call(
        functools.partial(_inproj_kernel, tn=tn),
        grid=(t // tm, PROJ_W // tn),
        in_specs=[pl.BlockSpec((tm, d), lambda i, j: (i, 0)),
                  pl.BlockSpec((1, 6, d), lambda i, j: (i // per_b, 0, 0)),
                  pl.BlockSpec((1, d), lambda i, j: (0, 0)),
                  pl.BlockSpec((d, tn), lambda i, j: (0, j)),
                  pl.BlockSpec((d, LANES), lambda i, j: (0, 0)),
                  pl.BlockSpec((tm, HEAD_DIM), lambda i, j: (i, 0)),
                  pl.BlockSpec((tm, HEAD_DIM), lambda i, j: (i, 0))],
        out_specs=[pl.BlockSpec((tm, tn), lambda i, j: (i, j)),
                   pl.BlockSpec((tm, LANES), lambda i, j: (i, 0))],
        out_shape=[jax.ShapeDtypeStruct((t, PROJ_W), BF16),
                   jax.ShapeDtypeStruct((t, LANES), F32)],
        scratch_shapes=[pltpu.VMEM((tm, d), BF16)],
        compiler_params=_cparams(("parallel", "arbitrary")),
        name="inproj",
    )(x2, mod3, g_pre, w_in_p, w_gates, cos_t, sin_t)


def _compress_kernel(h_ref, w1_ref, pe_ref, w2_ref, o_ref):
    hb = h_ref[0]
    half = hb.shape[1]
    ya = _dot(hb, w1_ref[0:half, :])
    yb = _dot(hb, w1_ref[half:2 * half, :])
    const = _dot(pe_ref[...], w1_ref[...])
    n = ya.shape[0]
    yb_next = pltpu.roll(yb, n - 1, axis=0)
    hid = _silu(ya + yb_next + const[0:1, :])
    out = _dot(hid.astype(BF16), w2_ref[...])
    row = lax.broadcasted_iota(jnp.int32, out.shape, 0)
    o_ref[0] = jnp.where(row < n - 1, out, 0.0).astype(BF16)


def _compress(hblk, w1, pe, w2):
    bg, n, half = hblk.shape
    pe8 = jnp.zeros((8, 2 * half), BF16).at[0].set(pe.reshape(-1).astype(BF16))
    return pl.pallas_call(
        _compress_kernel,
        grid=(bg,),
        in_specs=[pl.BlockSpec((1, n, half), lambda i: (i, 0, 0)),
                  pl.BlockSpec((2 * half, HEAD_DIM), lambda i: (0, 0)),
                  pl.BlockSpec((8, 2 * half), lambda i: (0, 0)),
                  pl.BlockSpec((HEAD_DIM, HEAD_DIM), lambda i: (0, 0))],
        out_specs=pl.BlockSpec((1, n, HEAD_DIM), lambda i: (i, 0, 0)),
        out_shape=jax.ShapeDtypeStruct((bg, n, HEAD_DIM), BF16),
        compiler_params=_cparams(("parallel",)),
        name="compress",
    )(hblk, w1.astype(BF16), pe8, w2.astype(BF16))


def _cmp_kernel(q_ref, kc_ref, vc_ref, gates_ref, ov_ref, ocmp_ref, selb_ref, *, tq, nsel):
    g = pl.program_id(0) % N_KV_GROUPS
    q0 = pl.program_id(1) * tq
    kc = kc_ref[0]
    vc = vc_ref[0]
    nc = kc.shape[0]
    t_idx = q0 + lax.broadcasted_iota(jnp.int32, (tq, nc), 0)
    c_idx = lax.broadcasted_iota(jnp.int32, (tq, nc), 1)
    c_ok = c_idx * CMP_STRIDE + (CMP_BLOCK - 1) <= t_idx
    gl = gates_ref[...]
    lane = lax.broadcasted_iota(jnp.int32, (tq, LANES), 1)
    psum = jnp.zeros((tq, nc), F32)
    for h in range(HEADS_PER_GROUP):
        qh = q_ref[:, h * HEAD_DIM:(h + 1) * HEAD_DIM]
        s = jnp.where(c_ok, _dot_nt(qh, kc), -jnp.inf)
        m = jnp.max(s, axis=-1, keepdims=True)
        m = jnp.where(m == -jnp.inf, 0.0, m)
        p = jnp.exp2(s - m)
        p = p / jnp.maximum(jnp.sum(p, axis=-1, keepdims=True), 1e-30)
        psum = psum + p
        o = _dot(p.astype(BF16), vc)
        glog = _lane_col(gl, lane, (g * HEADS_PER_GROUP + h) * 3)
        ocmp_ref[:, h * HEAD_DIM:(h + 1) * HEAD_DIM] = o * _sigmoid(glog)

    imp = _dot(psum, ov_ref[...], precision=lax.Precision.HIGHEST)
    cur = (q0 + lax.broadcasted_iota(jnp.int32, (tq, LANES), 0)) // SEL_BLOCK
    forced = (lane == 0) | (lane == cur) | (lane == cur - 1)
    imp = jnp.where(lane <= cur, imp + jnp.where(forced, FORCE_BONUS, 0.0), -jnp.inf)
    imp_t = imp.T[0:nsel, :]
    row = lax.broadcasted_iota(jnp.int32, (nsel, tq), 0)
    cnt = jnp.zeros((nsel, tq), F32)
    for k in range(nsel):
        vk = imp_t[k:k + 1, :]
        cnt = cnt + jnp.where(vk > imp_t, 1.0, 0.0)
        cnt = cnt + jnp.where((vk == imp_t) & (row > k), 1.0, 0.0)
    sel = (cnt < float(N_SEL)) & (imp_t > -jnp.inf)
    bias_t = jnp.where(sel, 0.0, -MASK_BIG)
    if nsel < LANES:
        bias_t = jnp.concatenate([bias_t, jnp.zeros((LANES - nsel, tq), F32)], axis=0)
    selb_ref[...] = bias_t.T.astype(BF16)


def _cmp_attn(proj, kc, vc, gates, batch, seq, tq):
    t = proj.shape[0]
    nq = seq // tq
    nc = kc.shape[1]
    nsel = seq // SEL_BLOCK
    assert nsel <= LANES and nsel % 8 == 0
    c_start = np.arange(nc) * CMP_STRIDE
    j_start = np.arange(LANES) * SEL_BLOCK
    ov = ((c_start[:, None] < j_start[None, :] + SEL_BLOCK)
          & (c_start[:, None] + CMP_BLOCK > j_start[None, :])
          & (np.arange(LANES)[None, :] < nsel)).astype(np.float32)
    gq = HEADS_PER_GROUP * HEAD_DIM
    qmap = lambda bg, i: ((bg // N_KV_GROUPS) * nq + i, bg % N_KV_GROUPS)
    return pl.pallas_call(
        functools.partial(_cmp_kernel, tq=tq, nsel=nsel),
        grid=(batch * N_KV_GROUPS, nq),
        in_specs=[pl.BlockSpec((tq, gq), qmap),
                  pl.BlockSpec((1, nc, HEAD_DIM), lambda bg, i: (bg, 0, 0)),
                  pl.BlockSpec((1, nc, HEAD_DIM), lambda bg, i: (bg, 0, 0)),
                  pl.BlockSpec((tq, LANES), lambda bg, i: ((bg // N_KV_GROUPS) * nq + i, 0)),
                  pl.BlockSpec((nc, LANES), lambda bg, i: (0, 0))],
        out_specs=[pl.BlockSpec((tq, gq), qmap),
                   pl.BlockSpec((tq, LANES), lambda bg, i: (bg * nq + i, 0))],
        out_shape=[jax.ShapeDtypeStruct((t, ATT_W), F32),
                   jax.ShapeDtypeStruct((batch * N_KV_GROUPS * seq, LANES), BF16)],
        compiler_params=_cparams(("parallel", "parallel")),
        name="cmp_attn",
    )(proj, kc, vc, gates, jnp.asarray(ov))


def _selwin_kernel(q_ref, selb_ref, ocmp_ref, gates_ref, ks_ref, vs_ref, kw_ref, vw_ref,
                   cb_ref, wb_ref, o_ref, kaug_s, vaug_s, vwaug_s, qaug_s, sbuf_s, m_s, acc_s,
                   swin_s, *, tq, tk, seq, nwin):
    g = pl.program_id(0) % N_KV_GROUPS
    qi = pl.program_id(1)
    q0 = qi * tq
    hq = HEADS_PER_GROUP * tq
    half = hq // 2

    @pl.when(qi == 0)
    def _():
        r = lax.broadcasted_iota(jnp.int32, (seq, HEAD_DIM), 0)
        ln = lax.broadcasted_iota(jnp.int32, (seq, HEAD_DIM), 1)
        ones_col = jnp.where(ln == 0, 1.0, 0.0).astype(BF16)
        kaug_s[:, 0:HEAD_DIM] = ks_ref[...]
        kaug_s[:, HEAD_DIM:2 * HEAD_DIM] = jnp.where(r // SEL_BLOCK == ln, 1.0, 0.0).astype(BF16)
        vaug_s[:, 0:HEAD_DIM] = vs_ref[...]
        vaug_s[:, HEAD_DIM:2 * HEAD_DIM] = ones_col
        vwaug_s[:, 0:HEAD_DIM] = vw_ref[...]
        vwaug_s[:, HEAD_DIM:2 * HEAD_DIM] = ones_col

    for h in range(HEADS_PER_GROUP):
        qaug_s[h * tq:(h + 1) * tq, 0:HEAD_DIM] = q_ref[:, h * HEAD_DIM:(h + 1) * HEAD_DIM]
        qaug_s[h * tq:(h + 1) * tq, HEAD_DIM:2 * HEAD_DIM] = selb_ref[...]
    qa = qaug_s[...]

    def scores(kt, slot):
        ks = kaug_s[pl.ds(pl.multiple_of(kt * tk, tk), tk), :]
        sbuf_s[slot] = _dot_nt(qa, ks)

    def absorb(kt, slot, diagonal=False):
        vs = vaug_s[pl.ds(pl.multiple_of(kt * tk, tk), tk), :]
        s = sbuf_s[slot]
        if diagonal:
            bias = cb_ref[(q0 - kt * tk) // tq]
            s = (s.reshape(HEADS_PER_GROUP, tq, tk) + bias[None]).reshape(hq, tk)
        m = m_s[...]
        m_new = jnp.maximum(m, jnp.max(s, axis=-1, keepdims=True))
        m_s[...] = m_new
        pb = jnp.exp2(s - m_new[:, 0:1]).astype(BF16)
        if diagonal:
            pv = jnp.concatenate([_dot(pb[0:half], vs), _dot(pb[half:hq], vs)], axis=0)
        else:
            pv = _dot(pb, vs)
        acc_s[...] = jnp.exp2(m - m_new)[:, 0:1] * acc_s[...] + pv

    n_full = q0 // tk
    m_s[...] = jnp.full(m_s.shape, -jnp.inf, F32)
    acc_s[...] = jnp.zeros(acc_s.shape, F32)
    scores(0, 0)

    def pair(j, c):
        scores(2 * j + 1, 1)
        absorb(2 * j, 0)
        scores(2 * j + 2, 0)
        absorb(2 * j + 1, 1)
        return c

    lax.fori_loop(0, n_full // 2, pair, 0)
    odd = n_full % 2

    @pl.when(odd == 1)
    def _():
        scores(n_full, 1)
        absorb(n_full - 1, 0)

    nq = seq // tq
    w0 = jnp.clip(qi - (nwin - 1), 0, nq - nwin) * tq
    wlen = nwin * tq
    kwin = kw_ref[pl.ds(pl.multiple_of(w0, tq), wlen), :]
    vwin = vwaug_s[pl.ds(pl.multiple_of(w0, tq), wlen), :]
    swin_s[0:half, :] = _dot_nt(qaug_s[0:half, 0:HEAD_DIM], kwin)
    swin_s[half:hq, :] = _dot_nt(qaug_s[half:hq, 0:HEAD_DIM], kwin)
    absorb(n_full, odd, diagonal=True)
    acc = acc_s[...]
    o_sel = acc[:, 0:HEAD_DIM] / acc[:, HEAD_DIM:HEAD_DIM + 1]

    wbias = wb_ref[(q0 - w0) // tq]
    sw = (swin_s[...].reshape(HEADS_PER_GROUP, tq, wlen) + wbias[None]).reshape(hq, wlen)
    pwb = jnp.exp2(sw - jnp.max(sw, axis=-1, keepdims=True)).astype(BF16)
    ow = jnp.concatenate([_dot(pwb[0:half], vwin), _dot(pwb[half:hq], vwin)], axis=0)
    o_win = ow[:, 0:HEAD_DIM] / ow[:, HEAD_DIM:HEAD_DIM + 1]

    gl = gates_ref[...]
    lane = lax.broadcasted_iota(jnp.int32, (tq, LANES), 1)
    for h in range(HEADS_PER_GROUP):
        gcol = (g * HEADS_PER_GROUP + h) * 3
        g_s = _sigmoid(_lane_col(gl, lane, gcol + 1))
        g_w = _sigmoid(_lane_col(gl, lane, gcol + 2))
        o = (ocmp_ref[:, h * HEAD_DIM:(h + 1) * HEAD_DIM]
             + g_s * o_sel[h * tq:(h + 1) * tq, :] + g_w * o_win[h * tq:(h + 1) * tq, :])
        o_ref[:, h * HEAD_DIM:(h + 1) * HEAD_DIM] = o.astype(BF16)


def _selwin_attn(proj, selb, ocmp, gates, batch, seq, tq, tk):
    t = proj.shape[0]
    nq = seq // tq
    assert tk % tq == 0 and tq & (tq - 1) == 0 and seq % tk == 0
    nwin = min(WINDOW // tq + 1, nq)
    assert WINDOW % tq == 0
    gq = HEADS_PER_GROUP * HEAD_DIM
    hq = HEADS_PER_GROUP * tq
    r = np.arange(tq)[:, None]
    cbias = np.stack([np.where(np.arange(tk)[None, :] <= r + off * tq, 0.0, -np.inf)
                      for off in range(tk // tq)]).astype(np.float32)
    dist = [r + off * tq - np.arange(nwin * tq)[None, :] for off in range(nwin)]
    wbias = np.stack([np.where((d >= 0) & (d < WINDOW), 0.0, -np.inf)
                      for d in dist]).astype(np.float32)
    qmap = lambda bg, i: ((bg // N_KV_GROUPS) * nq + i, bg % N_KV_GROUPS)

    def kvspec(col):
        return pl.BlockSpec((seq, HEAD_DIM),
                            lambda bg, i: (bg // N_KV_GROUPS, col // HEAD_DIM + bg % N_KV_GROUPS))

    return pl.pallas_call(
        functools.partial(_selwin_kernel, tq=tq, tk=tk, seq=seq, nwin=nwin),
        grid=(batch * N_KV_GROUPS, nq),
        in_specs=[pl.BlockSpec((tq, gq), qmap),
                  pl.BlockSpec((tq, LANES), lambda bg, i: (bg * nq + i, 0)),
                  pl.BlockSpec((tq, gq), qmap),
                  pl.BlockSpec((tq, LANES), lambda bg, i: ((bg // N_KV_GROUPS) * nq + i, 0)),
                  kvspec(COL_KS), kvspec(COL_VS), kvspec(COL_KW), kvspec(COL_VW),
                  pl.BlockSpec(cbias.shape, lambda bg, i: (0, 0, 0)),
                  pl.BlockSpec(wbias.shape, lambda bg, i: (0, 0, 0))],
        out_specs=pl.BlockSpec((tq, gq), qmap),
        out_shape=jax.ShapeDtypeStruct((t, ATT_W), BF16),
        scratch_shapes=[pltpu.VMEM((seq, 2 * HEAD_DIM), BF16),
                        pltpu.VMEM((seq, 2 * HEAD_DIM), BF16),
                        pltpu.VMEM((seq, 2 * HEAD_DIM), BF16),
                        pltpu.VMEM((hq, 2 * HEAD_DIM), BF16),
                        pltpu.VMEM((2, hq, tk), F32),
                        pltpu.VMEM((hq, LANES), F32),
                        pltpu.VMEM((hq, 2 * HEAD_DIM), F32),
                        pltpu.VMEM((hq, nwin * tq), F32)],
        compiler_params=_cparams(("parallel", "arbitrary")),
        name="selwin_attn",
    )(proj, selb, ocmp, gates, proj, proj, proj, proj, jnp.asarray(cbias), jnp.asarray(wbias))


_HGRN_LEVELS = (32, 16, 8, 4, 2, 1)


def _hgrn_tables():
    c = HGRN_CHUNK
    nl = len(_HGRN_LEVELS)
    idx = np.arange(c)
    i, j = idx[:, None], idx[None, :]
    mats = [j <= i, j > i]
    lvl = np.full((c, c), nl + 1, np.int32)
    for li, s in enumerate(_HGRN_LEVELS):
        mats.append((j >= (i // s) * s) & (j <= i))
        mats.append((j > i) & (j <= (i // s) * s + s - 1))
        lvl[((i // (2 * s)) == (j // (2 * s))) & (((i // s) % 2) == 1) & (((j // s) % 2) == 0)] = li
    lvl[i == j] = nl
    w = np.concatenate(mats, 0).astype(np.float32)
    return np.concatenate([w, w], axis=1), lvl


def _hgrn_head(hq_ref, hf_ref, hi_ref, hg_ref, lbl_ref, gn_ref, w_ref, lvl_ref, o_ref, st_s,
               hh, ts):
    c = HGRN_CHUNK
    nc = ts // c
    nl = len(_HGRN_LEVELS)
    cs = slice(hh * HEAD_DIM, (hh + 1) * HEAD_DIM)
    lbl = lbl_ref[:, cs]
    e = jnp.exp(lbl - jnp.max(lbl, axis=0, keepdims=True))
    lb = e[0:1, :] / jnp.sum(e, axis=0, keepdims=True)

    f = lb + (1.0 - lb) * _sigmoid(hf_ref[:, cs].astype(F32))
    k = 1.0 - f
    qf = _silu(hq_ref[:, cs].astype(F32))
    v = hi_ref[:, cs]

    lf = jnp.log(f) * LOG2E
    lf_hi = lf.astype(BF16)
    lf_lo = (lf - lf_hi.astype(F32)).astype(BF16)
    side = lambda a: jnp.concatenate([a[ci * c:(ci + 1) * c, :] for ci in range(nc)], axis=1)
    ex = _dot(w_ref[...], jnp.concatenate([side(lf_hi), side(lf_lo)], axis=0))

    def rows(bi):
        return jnp.concatenate(
            [ex[bi * c:(bi + 1) * c, ci * HEAD_DIM:(ci + 1) * HEAD_DIM] for ci in range(nc)], axis=0)

    b = rows(0)
    qhat = (qf * jnp.exp2(b)).astype(BF16)
    khat = (k * jnp.exp2(rows(1))).astype(BF16)
    lvl = lvl_ref[...]
    masks = [lvl == li for li in range(nl + 1)]
    qs = [(qf * jnp.exp2(rows(2 + 2 * li))).astype(BF16) for li in range(nl)] + [qf.astype(BF16)]
    ks = [(k * jnp.exp2(rows(3 + 2 * li))).astype(BF16) for li in range(nl)] + [k.astype(BF16)]
    o_intra = []
    for ci in range(nc):
        sl = slice(ci * c, (ci + 1) * c)
        a = jnp.zeros((c, c), F32)
        for li in range(nl + 1):
            a = jnp.where(masks[li], _dot_nt(qs[li][sl, :], ks[li][sl, :]), a)
        o_intra.append(_dot(a.astype(BF16), v[sl, :]))

    st = st_s[hh]
    outs = []
    for ci in range(nc):
        sl = slice(ci * c, (ci + 1) * c)
        outs.append(o_intra[ci] + _dot_nt(qhat[sl, :], st.astype(BF16)))
        st = st * jnp.exp2(b[(ci + 1) * c - 1:(ci + 1) * c, :]) + lax.dot_general(
            v[sl, :], khat[sl, :], _TN, preferred_element_type=F32)
    st_s[hh] = st
    o = jnp.concatenate(outs, axis=0)
    o = o * lax.rsqrt(jnp.mean(o * o, axis=-1, keepdims=True) + EPS) * gn_ref[:, cs]
    o_ref[:, cs] = (o * _silu(hg_ref[:, cs].astype(F32))).astype(BF16)


def _hgrn_kernel(hq_ref, hf_ref, hi_ref, hg_ref, lbl_ref, gn_ref, w_ref, lvl_ref,
                 o_ref, st_s, *, ts, hb):
    @pl.when(pl.program_id(2) == 0)
    def _():
        st_s[...] = jnp.zeros_like(st_s)

    for hh in range(hb):
        _hgrn_head(hq_ref, hf_ref, hi_ref, hg_ref, lbl_ref, gn_ref, w_ref, lvl_ref, o_ref, st_s,
                   hh, ts)


def _hgrn(proj, lb_logits, g_norm, batch, seq, ts, hb):
    t = proj.shape[0]
    ns = seq // ts
    wmat, lvl = _hgrn_tables()
    w = hb * HEAD_DIM

    def colspec(col):
        return pl.BlockSpec((ts, w), lambda b, h, s: (b * ns + s, col // w + h))

    nl = lb_logits.shape[0]
    return pl.pallas_call(
        functools.partial(_hgrn_kernel, ts=ts, hb=hb),
        grid=(batch, N_HGRN_HEADS // hb, ns),
        in_specs=[colspec(COL_HQ), colspec(COL_HF), colspec(COL_HI), colspec(COL_HG),
                  pl.BlockSpec((nl, w), lambda b, h, s: (0, h)),
                  pl.BlockSpec((1, w), lambda b, h, s: (0, h)),
                  pl.BlockSpec(wmat.shape, lambda b, h, s: (0, 0)),
                  pl.BlockSpec(lvl.shape, lambda b, h, s: (0, 0))],
        out_specs=pl.BlockSpec((ts, w), lambda b, h, s: (b * ns + s, h)),
        out_shape=jax.ShapeDtypeStruct((t, ATT_W), BF16),
        scratch_shapes=[pltpu.VMEM((hb, HEAD_DIM, HEAD_DIM), F32)],
        compiler_params=_cparams(("parallel", "parallel", "arbitrary")),
        name="hgrn2",
    )(proj, proj, proj, proj, lb_logits, g_norm, jnp.asarray(wmat, BF16), jnp.asarray(lvl))


def _outproj_kernel(oa_ref, or_ref, wa_ref, wb_ref, x_ref, mod_ref, gpost_ref, gpre_ref,
                    wrh_ref, wrl_ref, br_ref, x1_ref, h2_ref, lg_ref):
    m = mod_ref[0]
    y = _dot(oa_ref[...], wa_ref[...]) + _dot(or_ref[...], wb_ref[...])
    x1 = x_ref[...] + m[2:3, :] * _rms(y, gpost_ref[...])
    x1_ref[...] = x1
    h = _rms(x1, gpre_ref[...]) * (1.0 + m[4:5, :]) + m[3:4, :]
    h2_ref[...] = h
    h_hi = h.astype(BF16)
    h_lo = (h - h_hi.astype(F32)).astype(BF16)
    lg_ref[...] = (_dot(h_hi, wrh_ref[...]) + _dot(h_lo, wrh_ref[...])
                   + _dot(h_hi, wrl_ref[...]) + br_ref[...])


def _outproj(o_att, o_rec, w_out, x2, mod3, g_post, g_pre, wr, br, seq, tm):
    t, d = x2.shape
    per_b = seq // tm
    wr_hi = wr.astype(BF16)
    wr_lo = (wr - wr_hi.astype(F32)).astype(BF16)
    wa = w_out[:ATT_W].astype(BF16)
    wb = w_out[ATT_W:].astype(BF16)
    row = lambda i: (i, 0)
    fixed = lambda i: (0, 0)
    return pl.pallas_call(
        _outproj_kernel,
        grid=(t // tm,),
        in_specs=[pl.BlockSpec((tm, ATT_W), row), pl.BlockSpec((tm, ATT_W), row),
                  pl.BlockSpec((ATT_W, d), fixed), pl.BlockSpec((ATT_W, d), fixed),
                  pl.BlockSpec((tm, d), row),
                  pl.BlockSpec((1, 6, d), lambda i: (i // per_b, 0, 0)),
                  pl.BlockSpec((1, d), fixed), pl.BlockSpec((1, d), fixed),
                  pl.BlockSpec((d, LANES), fixed), pl.BlockSpec((d, LANES), fixed),
                  pl.BlockSpec((1, LANES), fixed)],
        out_specs=[pl.BlockSpec((tm, d), row), pl.BlockSpec((tm, d), row),
                   pl.BlockSpec((tm, LANES), row)],
        out_shape=[jax.ShapeDtypeStruct((t, d), F32), jax.ShapeDtypeStruct((t, d), F32),
                   jax.ShapeDtypeStruct((t, LANES), F32)],
        compiler_params=_cparams(("parallel",)),
        name="outproj",
    )(o_att, o_rec, wa, wb, x2, mod3, g_post, g_pre, wr_hi, wr_lo, br)


G_LANE0 = N_EXPERTS


def _route_kernel(lg_ref, tri_ref, info_ref, cnt_ref, carry_s):
    @pl.when(pl.program_id(0) == 0)
    def _():
        carry_s[...] = jnp.zeros_like(carry_s)

    x = lg_ref[...]
    lane = lax.broadcasted_iota(jnp.int32, x.shape, 1)
    big = jnp.int32(10 ** 6)
    rmax = lambda a: jnp.max(a, axis=-1, keepdims=True)
    rmin = lambda a: jnp.min(a, axis=-1, keepdims=True)
    rsum = lambda a: jnp.sum(a, axis=-1, keepdims=True)

    is_g = (lane >= G_LANE0) & (lane < G_LANE0 + N_EXPERT_GROUPS)
    lgm = jnp.where(is_g, x, -jnp.inf)
    mg = rmax(lgm)
    pg_sel = 1.0 / rsum(jnp.where(is_g, jnp.exp(lgm - mg), 0.0))
    gsel = rmin(jnp.where(lgm == mg, lane, big)) - G_LANE0

    is_e = (lane >= gsel * EXPERTS_PER_GROUP) & (lane < (gsel + 1) * EXPERTS_PER_GROUP)
    lem = jnp.where(is_e, x, -jnp.inf)
    pe = jnp.where(is_e, jnp.exp(lem - rmax(lem)), 0.0)
    pe = pe / rsum(pe)
    pe = jnp.where(is_e, pe, -1.0)
    v1 = rmax(pe)
    i1 = rmin(jnp.where(pe == v1, lane, big))
    pe2 = jnp.where(lane == i1, -1.0, pe)
    v2 = rmax(pe2)
    i2 = rmin(jnp.where(pe2 == v2, lane, big))
    w1 = v1 / (v1 + v2) * pg_sel
    w2 = v2 / (v1 + v2) * pg_sel

    oh1 = jnp.where(lane == i1, 1.0, 0.0)
    oh2 = jnp.where(lane == i2, 1.0, 0.0)
    both = oh1 + oh2
    before = _dot(tri_ref[...], both.astype(BF16)) + carry_s[0:1, :]
    r1 = rsum(oh1 * before)
    r2 = rsum(oh2 * before)
    carry_s[0:1, :] = carry_s[0:1, :] + jnp.sum(both, axis=0, keepdims=True)
    cnt_ref[...] = carry_s[...]

    cols = (i1.astype(F32), i2.astype(F32), w1, w2, r1, r2)
    info = jnp.zeros(x.shape, F32)
    for ci, col in enumerate(cols):
        info = jnp.where(lane == ci, col, info)
    info_ref[...] = info


def _route(logits, tm):
    t = logits.shape[0]
    tri = jnp.asarray(np.tril(np.ones((tm, tm), np.float32), -1), BF16)
    return pl.pallas_call(
        _route_kernel,
        grid=(t // tm,),
        in_specs=[pl.BlockSpec((tm, LANES), lambda i: (i, 0)),
                  pl.BlockSpec((tm, tm), lambda i: (0, 0))],
        out_specs=[pl.BlockSpec((tm, LANES), lambda i: (i, 0)),
                   pl.BlockSpec((8, LANES), lambda i: (0, 0))],
        out_shape=[jax.ShapeDtypeStruct((t, LANES), F32),
                   jax.ShapeDtypeStruct((8, LANES), F32)],
        scratch_shapes=[pltpu.VMEM((8, LANES), F32)],
        compiler_params=_cparams(("arbitrary",)),
        name="route",
    )(logits, tri)


def _row_copy(src, i, dst, j, sem):
    return pltpu.make_async_copy(src.at[pl.ds(i, 1)], dst.at[pl.ds(j, 1)], sem)


def _dispatch_kernel(pos_ref, h_ref, xs_in, xs_out, sem, *, tm):
    del xs_in

    def start(t, c):
        _row_copy(h_ref, t, xs_out, pos_ref[0, 0, 2 * t], sem).start()
        _row_copy(h_ref, t, xs_out, pos_ref[0, 0, 2 * t + 1], sem).start()
        return c

    def wait(t, c):
        _row_copy(h_ref, 0, xs_out, 0, sem).wait()
        _row_copy(h_ref, 0, xs_out, 0, sem).wait()
        return c

    lax.fori_loop(0, tm, start, 0, unroll=8)
    lax.fori_loop(0, tm, wait, 0, unroll=8)


def _dispatch(pos3, h2, xs0, tm):
    t, d = h2.shape
    return pl.pallas_call(
        functools.partial(_dispatch_kernel, tm=tm),
        grid=(t // tm,),
        in_specs=[pl.BlockSpec((1, 1, 2 * tm), lambda i: (i, 0, 0), memory_space=pltpu.SMEM),
                  pl.BlockSpec((tm, d), lambda i: (i, 0)),
                  pl.BlockSpec(memory_space=pl.ANY)],
        out_specs=pl.BlockSpec(memory_space=pl.ANY),
        out_shape=jax.ShapeDtypeStruct(xs0.shape, xs0.dtype),
        input_output_aliases={2: 0},
        scratch_shapes=[pltpu.SemaphoreType.DMA(())],
        compiler_params=_cparams(("arbitrary",)),
        name="moe_dispatch",
    )(pos3, h2, xs0)


def _expert_kernel(te_ref, nu_ref, xs_ref, wg_ref, wu_ref, wd_ref, ys_ref, wg_s, wu_s, wd_s):
    t = pl.program_id(0)

    @pl.when(t < nu_ref[0])
    def _():
        prev = te_ref[jnp.maximum(t - 1, 0)]

        @pl.when((t == 0) | (te_ref[t] != prev))
        def _():
            wg_s[...] = wg_ref[0].astype(BF16)
            wu_s[...] = wu_ref[0].astype(BF16)
            wd_s[...] = wd_ref[0].astype(BF16)

        x = xs_ref[...].astype(BF16)
        hid = _silu(_dot(x, wg_s[...])) * _dot(x, wu_s[...])
        ys_ref[...] = _dot(hid.astype(BF16), wd_s[...])

    @pl.when(t >= nu_ref[0])
    def _():
        ys_ref[...] = jnp.zeros_like(ys_ref)


def _experts(tile_expert, n_used, xs, w_gate, w_up, w_down, tm):
    p, d = xs.shape
    f = w_gate.shape[-1]
    n_tiles = p // tm
    wmap = lambda t, te, nu: (te[t], 0, 0)
    return pl.pallas_call(
        _expert_kernel,
        grid_spec=pltpu.PrefetchScalarGridSpec(
            num_scalar_prefetch=2,
            grid=(n_tiles,),
            in_specs=[pl.BlockSpec((tm, d), lambda t, te, nu: (t, 0)),
                      pl.BlockSpec((1, d, f), wmap),
                      pl.BlockSpec((1, d, f), wmap),
                      pl.BlockSpec((1, f, d), wmap)],
            out_specs=pl.BlockSpec((tm, d), lambda t, te, nu: (t, 0)),
            scratch_shapes=[pltpu.VMEM((d, f), BF16), pltpu.VMEM((d, f), BF16),
                            pltpu.VMEM((f, d), BF16)]),
        out_shape=jax.ShapeDtypeStruct((p, d), F32),
        compiler_params=_cparams(("arbitrary",)),
        name="moe_experts",
    )(tile_expert, n_used, xs, w_gate, w_up, w_down)


def _combine_kernel(pos_ref, info_ref, x1_ref, mod_ref, g_ref, ys_ref, o_ref, buf, sem, *, tm):
    def start(t, c):
        _row_copy(ys_ref, pos_ref[0, 0, 2 * t], buf.at[0], t, sem).start()
        _row_copy(ys_ref, pos_ref[0, 0, 2 * t + 1], buf.at[1], t, sem).start()
        return c

    def wait(t, c):
        _row_copy(ys_ref, 0, buf.at[0], 0, sem).wait()
        _row_copy(ys_ref, 0, buf.at[1], 0, sem).wait()
        return c

    lax.fori_loop(0, tm, start, 0, unroll=8)
    lax.fori_loop(0, tm, wait, 0, unroll=8)
    info = info_ref[...]
    y = info[:, 2:3] * buf[0] + info[:, 3:4] * buf[1]
    o_ref[...] = x1_ref[...] + mod_ref[0][5:6, :] * _rms(y, g_ref[...])


def _combine(pos3, info, x1, mod3, g_post, ys, seq, tm):
    t, d = x1.shape
    per_b = seq // tm
    return pl.pallas_call(
        functools.partial(_combine_kernel, tm=tm),
        grid=(t // tm,),
        in_specs=[pl.BlockSpec((1, 1, 2 * tm), lambda i: (i, 0, 0), memory_space=pltpu.SMEM),
                  pl.BlockSpec((tm, LANES), lambda i: (i, 0)),
                  pl.BlockSpec((tm, d), lambda i: (i, 0)),
                  pl.BlockSpec((1, 6, d), lambda i: (i // per_b, 0, 0)),
                  pl.BlockSpec((1, d), lambda i: (0, 0)),
                  pl.BlockSpec(memory_space=pl.ANY)],
        out_specs=pl.BlockSpec((tm, d), lambda i: (i, 0)),
        out_shape=jax.ShapeDtypeStruct((t, d), F32),
        scratch_shapes=[pltpu.VMEM((2, tm, d), F32), pltpu.SemaphoreType.DMA(())],
        compiler_params=_cparams(("arbitrary",)),
        name="moe_combine",
    )(pos3, info, x1, mod3, g_post, ys)


def _moe(h2, logits, x1, mod3, g_post, w_gate, w_up, w_down, seq, tm_route, tm_exp, tm_row):
    t, d = h2.shape
    info, cnt = _route(logits, tm_route)
    counts = cnt[0, :N_EXPERTS].astype(jnp.int32)
    tiles_e = (counts + tm_exp - 1) // tm_exp
    tile_end = jnp.cumsum(tiles_e)
    offs = (tile_end - tiles_e) * tm_exp
    n_used = tile_end[-1]
    e12 = info[:, 0:2].astype(jnp.int32)
    pos = offs[e12] + info[:, 4:6].astype(jnp.int32)
    n_tiles = (2 * t) // tm_exp + N_EXPERTS
    tidx = jnp.minimum(jnp.arange(n_tiles, dtype=jnp.int32), n_used - 1)
    tile_expert = jnp.sum(tidx[:, None] >= tile_end[None, :], axis=1).astype(jnp.int32)
    pos3 = pos.reshape(t // tm_row, 1, 2 * tm_row)

    ew = w_gate.shape
    wg = w_gate.reshape(N_EXPERTS, ew[-2], ew[-1])
    wu = w_up.reshape(N_EXPERTS, ew[-2], ew[-1])
    wd = w_down.reshape(N_EXPERTS, ew[-1], ew[-2])
    xs = _dispatch(pos3, h2, jnp.zeros((n_tiles * tm_exp, d), F32), tm_row)
    ys = _experts(tile_expert, n_used.reshape(1).astype(jnp.int32), xs, wg, wu, wd, tm_exp)
    return _combine(pos3, info, x1, mod3, g_post, ys, seq, tm_row)


def _permute_w_in(w_in):
    sizes = [ATT_W] + [KV_W] * 6 + [3 * N_ATT_HEADS] + [ATT_W] * 4
    cuts = np.cumsum(sizes)[:-1]
    q, kc, vc, ks, vs, kw, vw, gates, hq, hf, hi, hg = jnp.split(w_in, cuts, axis=1)
    w_p = jnp.concatenate([q, kc, ks, kw, vc, vs, vw, hq, hf, hi, hg], axis=1).astype(BF16)
    w_g = jnp.pad(gates, ((0, 0), (0, LANES - gates.shape[1]))).astype(BF16)
    return w_p, w_g


def _half_blocks(proj, col, batch, seq):
    a = proj[:, col:col + KV_W].reshape(batch, seq, N_KV_GROUPS, HEAD_DIM)
    a = a.transpose(0, 2, 1, 3)
    return a.reshape(batch * N_KV_GROUPS, seq // CMP_STRIDE, CMP_STRIDE * HEAD_DIM)


def _block(x, c, positions, w_ada, b_ada, g_pre_mix, g_post_mix, g_pre_ffn, g_post_ffn,
           w_in, w_out, pe_k, w1_k, w2_k, pe_v, w1_v, w2_v, lb_logits, g_norm,
           w_group, b_group, w_router, b_router, w_gate, w_up, w_down, cfg):
    batch, seq, d = x.shape
    t = batch * seq
    x2 = x.reshape(t, d)

    c8 = jnp.zeros((8, d), F32).at[:batch].set(c)
    mod3 = _ada_mod(c8, w_ada, b_ada)[:batch].reshape(batch, 6, d)
    cos_t, sin_t = _rope_tables(positions, cfg["tm_rope"])
    w_p, w_g = _permute_w_in(w_in)
    proj, gates = _inproj(x2, mod3, g_pre_mix.reshape(1, d), w_p, w_g, cos_t, sin_t,
                          seq, cfg["tm_in"])

    kc = _compress(_half_blocks(proj, COL_KC, batch, seq), w1_k, pe_k, w2_k)
    vc = _compress(_half_blocks(proj, COL_VC, batch, seq), w1_v, pe_v, w2_v)
    ocmp, selb = _cmp_attn(proj, kc, vc, gates, batch, seq, cfg["tq_cmp"])
    o_att = _selwin_attn(proj, selb, ocmp, gates, batch, seq, cfg["tq"], cfg["tk"])
    o_rec = _hgrn(proj, lb_logits, g_norm.reshape(1, -1), batch, seq, cfg["ts_hgrn"],
                  cfg["hb_hgrn"])

    wr = jnp.concatenate([w_router, w_group], axis=1)
    wr = jnp.pad(wr, ((0, 0), (0, LANES - wr.shape[1])))
    br = jnp.pad(jnp.concatenate([b_router, b_group]), (0, LANES - N_EXPERTS - N_EXPERT_GROUPS))
    x1, h2, logits = _outproj(o_att, o_rec, w_out, x2, mod3, g_post_mix.reshape(1, d),
                              g_pre_ffn.reshape(1, d), wr, br.reshape(1, LANES), seq,
                              cfg["tm_out"])
    out = _moe(h2, logits, x1, mod3, g_post_ffn.reshape(1, d), w_gate, w_up, w_down, seq,
               cfg["tm_route"], cfg["tm_exp"], cfg["tm_row"])
    return out.reshape(batch, seq, d)


def _config(seq):
    return dict(tm_rope=min(1024, seq), tm_in=min(1024, seq), tq_cmp=min(256, seq),
                tq=min(256, seq), tk=min(512, seq), ts_hgrn=min(256, seq), hb_hgrn=4,
                tm_out=min(256, seq), tm_route=min(256, seq), tm_exp=256,
                tm_row=min(256, seq))


def kernel(x, c, positions, w_ada, b_ada, g_pre_mix, g_post_mix, g_pre_ffn, g_post_ffn, w_in, w_out, cmp_pe_k, cmp_w1_k, cmp_w2_k, cmp_pe_v, cmp_w1_v, cmp_w2_v, hgrn_lb_logits, hgrn_g_norm, w_group, b_group, w_router, b_router, w_gate, w_up, w_down):
    assert w_ada.shape[0] == 1, "single-layer block"
    return _block(x, c, positions, w_ada[0], b_ada[0], g_pre_mix[0], g_post_mix[0],
                  g_pre_ffn[0], g_post_ffn[0], w_in[0], w_out[0], cmp_pe_k[0], cmp_w1_k[0],
                  cmp_w2_k[0], cmp_pe_v[0], cmp_w1_v[0], cmp_w2_v[0], hgrn_lb_logits,
                  hgrn_g_norm[0], w_group[0], b_group[0], w_router[0], b_router[0],
                  w_gate[0], w_up[0], w_down[0], _config(x.shape[1]))
```

```python
import functools

import numpy as np
import jax
import jax.numpy as jnp
from jax import lax
from jax.experimental import pallas as pl
from jax.experimental.pallas import tpu as pltpu

F32 = jnp.float32
BF16 = jnp.bfloat16
U32 = jnp.uint32
HI_HALF = 0xFFFF0000

HEAD_DIM = 128
N_KV_GROUPS = 2
HEADS_PER_GROUP = 4
N_ATT_HEADS = N_KV_GROUPS * HEADS_PER_GROUP
CMP_BLOCK = 32
CMP_STRIDE = 16
SEL_BLOCK = 64
N_SEL = 16
WINDOW = 512
FORCE_BONUS = 1.0e4
ROPE_THETA = 10000.0
N_HGRN_HEADS = 8
HGRN_CHUNK = 64
N_EXPERT_GROUPS = 4
EXPERTS_PER_GROUP = 8
N_EXPERTS = N_EXPERT_GROUPS * EXPERTS_PER_GROUP
EPS = 1e-6

LANES = 128
VMEM_LIMIT = 56 * 1024 * 1024

ATT_W = N_ATT_HEADS * HEAD_DIM
KV_W = N_KV_GROUPS * HEAD_DIM
COL_Q = 0
COL_KC = COL_Q + ATT_W
COL_KS = COL_KC + KV_W
COL_KW = COL_KS + KV_W
COL_VC = COL_KW + KV_W
COL_VS = COL_VC + KV_W
COL_VW = COL_VS + KV_W
COL_HQ = COL_VW + KV_W
COL_HF = COL_HQ + ATT_W
COL_HI = COL_HF + ATT_W
COL_HG = COL_HI + ATT_W
PROJ_W = COL_HG + ATT_W
ROPE_W = COL_VC

MASK_BIG = float(2 ** 30)
LOG2E = 1.4426950408889634
Q_PRESCALE = HEAD_DIM ** -0.5 * LOG2E

_NT = (((1,), (1,)), ((), ()))
_TN = (((0,), (0,)), ((), ()))


def _cparams(sem):
    return pltpu.CompilerParams(dimension_semantics=sem, vmem_limit_bytes=VMEM_LIMIT)


def _dot(a, b, **kw):
    return jnp.dot(a, b, preferred_element_type=F32, **kw)


def _dot_nt(a, b):
    return lax.dot_general(a, b, _NT, preferred_element_type=F32)


def _sigmoid(x):
    return 1.0 / (1.0 + jnp.exp(-x))


def _silu(x):
    return x * _sigmoid(x)


def _rms(x, g):
    return x * lax.rsqrt(jnp.mean(x * x, axis=-1, keepdims=True) + EPS) * g


def _lane_col(x, lane, col):
    return jnp.sum(jnp.where(lane == col, x, 0.0), axis=-1, keepdims=True)


def _pack_halves(a):
    half = a.shape[1] // 2
    lo = pltpu.bitcast(a[:, :half].astype(BF16).astype(F32), U32)
    hi = pltpu.bitcast(a[:, half:].astype(BF16).astype(F32), U32)
    return lax.shift_right_logical(lo, U32(16)) | (hi & U32(HI_HALF))


def _unpack_halves(w):
    lo = pltpu.bitcast(lax.shift_left(w, U32(16)), F32)
    hi = pltpu.bitcast(w & U32(HI_HALF), F32)
    return lo, hi


def _ada_kernel(c_ref, w_ref, b_ref, o_ref):
    s = _silu(c_ref[...])
    o_ref[...] = _dot(s, w_ref[...], precision=lax.Precision.HIGHEST) + b_ref[...]


def _ada_mod(c8, w_ada, b_ada, tn=512):
    rows, d = c8.shape
    n = w_ada.shape[1]
    return pl.pallas_call(
        _ada_kernel,
        grid=(n // tn,),
        in_specs=[pl.BlockSpec((rows, d), lambda j: (0, 0)),
                  pl.BlockSpec((d, tn), lambda j: (0, j)),
                  pl.BlockSpec((1, tn), lambda j: (0, j))],
        out_specs=pl.BlockSpec((rows, tn), lambda j: (0, j)),
        out_shape=jax.ShapeDtypeStruct((rows, n), F32),
        compiler_params=_cparams(("parallel",)),
        name="ada_mod",
    )(c8, w_ada, b_ada.reshape(1, n))


def _rope_kernel(ang_ref, cos_ref, sin_ref):
    a = ang_ref[...]
    lane = lax.broadcasted_iota(jnp.int32, a.shape, 1)
    cos_ref[...] = jnp.cos(a)
    sin_ref[...] = jnp.where(lane < HEAD_DIM // 2, -jnp.sin(a), jnp.sin(a))


def _rope_tables(positions, tm):
    t = positions.size
    inv_freq = ROPE_THETA ** (-jnp.arange(0, HEAD_DIM, 2, dtype=F32) / HEAD_DIM)
    ang = positions.reshape(t, 1).astype(F32) * jnp.concatenate([inv_freq, inv_freq])[None, :]
    spec = pl.BlockSpec((tm, HEAD_DIM), lambda i: (i, 0))
    return pl.pallas_call(
        _rope_kernel,
        grid=(t // tm,),
        in_specs=[spec],
        out_specs=[spec, spec],
        out_shape=[jax.ShapeDtypeStruct((t, HEAD_DIM), F32)] * 2,
        compiler_params=_cparams(("parallel",)),
        name="rope_tables",
    )(ang)


def _inproj_kernel(x_ref, mod_ref, g_ref, w_ref, wg_ref, cos_ref, sin_ref,
                   proj_ref, gates_ref, h_s, *, tn):
    j = pl.program_id(1)

    @pl.when(j == 0)
    def _():
        m = mod_ref[0]
        h = _rms(x_ref[...], g_ref[...]) * (1.0 + m[1:2, :]) + m[0:1, :]
        hb = h.astype(BF16)
        h_s[...] = hb
        gates_ref[...] = _dot(hb, wg_ref[...])

    heads = tn // HEAD_DIM
    n_rope_tiles = ROPE_W // tn
    rem_heads = (ROPE_W % tn) // HEAD_DIM

    def store(n_rot, scale=None):
        acc = _dot(h_s[...], w_ref[...])
        for c in range(heads):
            a = acc[:, c * HEAD_DIM:(c + 1) * HEAD_DIM]
            if c < n_rot:
                a = a * cos_ref[...] + pltpu.roll(a, HEAD_DIM // 2, axis=1) * sin_ref[...]
            if scale is not None:
                a = a * scale
            proj_ref[:, c * HEAD_DIM:(c + 1) * HEAD_DIM] = a.astype(BF16)

    @pl.when(j < ATT_W // tn)
    def _():
        store(heads, Q_PRESCALE)

    @pl.when((j >= ATT_W // tn) & (j < n_rope_tiles))
    def _():
        store(heads)

    if rem_heads:
        @pl.when(j == n_rope_tiles)
        def _():
            store(rem_heads)

    @pl.when(j >= n_rope_tiles + (1 if rem_heads else 0))
    def _():
        store(0)


def _inproj(x2, mod3, g_pre, w_in_p, w_gates, cos_t, sin_t, seq, tm, tn=512):
    t, d = x2.shape
    per_b = seq // tm
    return pl.pallas_call(
        functools.partial(_inproj_kernel, tn=tn),
        grid=(t // tm, PROJ_W // tn),
        in_specs=[pl.BlockSpec((tm, d), lambda i, j: (i, 0)),
                  pl.BlockSpec((1, 6, d), lambda i, j: (i // per_b, 0, 0)),
                  pl.BlockSpec((1, d), lambda i, j: (0, 0)),
                  pl.BlockSpec((d, tn), lambda i, j: (0, j)),
                  pl.BlockSpec((d, LANES), lambda i, j: (0, 0)),
                  pl.BlockSpec((tm, HEAD_DIM), lambda i, j: (i, 0)),
                  pl.BlockSpec((tm, HEAD_DIM), lambda i, j: (i, 0))],
        out_specs=[pl.BlockSpec((tm, tn), lambda i, j: (i, j)),
                   pl.BlockSpec((tm, LANES), lambda i, j: (i, 0))],
        out_shape=[jax.ShapeDtypeStruct((t, PROJ_W), BF16),
                   jax.ShapeDtypeStruct((t, LANES), F32)],
        scratch_shapes=[pltpu.VMEM((tm, d), BF16)],
        compiler_params=_cparams(("parallel", "arbitrary")),
        name="inproj",
    )(x2, mod3, g_pre, w_in_p, w_gates, cos_t, sin_t)


def _compress_kernel(h_ref, w1_ref, pe_ref, w2_ref, o_ref):
    hb = h_ref[0]
    half = hb.shape[1]
    ya = _dot(hb, w1_ref[0:half, :])
    yb = _dot(hb, w1_ref[half:2 * half, :])
    const = _dot(pe_ref[...], w1_ref[...])
    n = ya.shape[0]
    yb_next = pltpu.roll(yb, n - 1, axis=0)
    hid = _silu(ya + yb_next + const[0:1, :])
    out = _dot(hid.astype(BF16), w2_ref[...])
    row = lax.broadcasted_iota(jnp.int32, out.shape, 0)
    o_ref[0] = jnp.where(row < n - 1, out, 0.0).astype(BF16)


def _compress(hblk, w1, pe, w2):
    bg, n, half = hblk.shape
    pe8 = jnp.zeros((8, 2 * half), BF16).at[0].set(pe.reshape(-1).astype(BF16))
    return pl.pallas_call(
        _compress_kernel,
        grid=(bg,),
        in_specs=[pl.BlockSpec((1, n, half), lambda i: (i, 0, 0)),
                  pl.BlockSpec((2 * half, HEAD_DIM), lambda i: (0, 0)),
                  pl.BlockSpec((8, 2 * half), lambda i: (0, 0)),
                  pl.BlockSpec((HEAD_DIM, HEAD_DIM), lambda i: (0, 0))],
        out_specs=pl.BlockSpec((1, n, HEAD_DIM), lambda i: (i, 0, 0)),
        out_shape=jax.ShapeDtypeStruct((bg, n, HEAD_DIM), BF16),
        compiler_params=_cparams(("parallel",)),
        name="compress",
    )(hblk, w1.astype(BF16), pe8, w2.astype(BF16))


def _cmp_kernel(q_ref, kc_ref, vc_ref, gates_ref, ov_ref, ocmp_ref, selb_ref, *, tq, nsel):
    g = pl.program_id(0) % N_KV_GROUPS
    q0 = pl.program_id(1) * tq
    kc = kc_ref[0]
    vc = vc_ref[0]
    nc = kc.shape[0]
    t_idx = q0 + lax.broadcasted_iota(jnp.int32, (tq, nc), 0)
    c_idx = lax.broadcasted_iota(jnp.int32, (tq, nc), 1)
    c_ok = c_idx * CMP_STRIDE + (CMP_BLOCK - 1) <= t_idx
    gl = gates_ref[...]
    lane = lax.broadcasted_iota(jnp.int32, (tq, LANES), 1)
    psum = jnp.zeros((tq, nc), F32)
    for h in range(HEADS_PER_GROUP):
        qh = q_ref[:, h * HEAD_DIM:(h + 1) * HEAD_DIM]
        s = jnp.where(c_ok, _dot_nt(qh, kc), -jnp.inf)
        m = jnp.max(s, axis=-1, keepdims=True)
        m = jnp.where(m == -jnp.inf, 0.0, m)
        p = jnp.exp2(s - m)
        p = p / jnp.maximum(jnp.sum(p, axis=-1, keepdims=True), 1e-30)
        psum = psum + p
        o = _dot(p.astype(BF16), vc)
        glog = _lane_col(gl, lane, (g * HEADS_PER_GROUP + h) * 3)
        ocmp_ref[:, h * HEAD_DIM:(h + 1) * HEAD_DIM] = o * _sigmoid(glog)

    imp = _dot(psum, ov_ref[...], precision=lax.Precision.HIGHEST)
    cur = (q0 + lax.broadcasted_iota(jnp.int32, (tq, LANES), 0)) // SEL_BLOCK
    forced = (lane == 0) | (lane == cur) | (lane == cur - 1)
    imp = jnp.where(lane <= cur, imp + jnp.where(forced, FORCE_BONUS, 0.0), -jnp.inf)
    imp_t = imp.T[0:nsel, :]
    row = lax.broadcasted_iota(jnp.int32, (nsel, tq), 0)
    cnt = jnp.zeros((nsel, tq), F32)
    for k in range(nsel):
        vk = imp_t[k:k + 1, :]
        cnt = cnt + jnp.where(vk > imp_t, 1.0, 0.0)
        cnt = cnt + jnp.where((vk == imp_t) & (row > k), 1.0, 0.0)
    sel = (cnt < float(N_SEL)) & (imp_t > -jnp.inf)
    bias_t = jnp.where(sel, 0.0, -MASK_BIG)
    if nsel < LANES:
        bias_t = jnp.concatenate([bias_t, jnp.zeros((LANES - nsel, tq), F32)], axis=0)
    selb_ref[...] = bias_t.T.astype(BF16)


def _cmp_attn(proj, kc, vc, gates, batch, seq, tq):
    t = proj.shape[0]
    nq = seq // tq
    nc = kc.shape[1]
    nsel = seq // SEL_BLOCK
    assert nsel <= LANES and nsel % 8 == 0
    c_start = np.arange(nc) * CMP_STRIDE
    j_start = np.arange(LANES) * SEL_BLOCK
    ov = ((c_start[:, None] < j_start[None, :] + SEL_BLOCK)
          & (c_start[:, None] + CMP_BLOCK > j_start[None, :])
          & (np.arange(LANES)[None, :] < nsel)).astype(np.float32)
    gq = HEADS_PER_GROUP * HEAD_DIM
    qmap = lambda bg, i: ((bg // N_KV_GROUPS) * nq + i, bg % N_KV_GROUPS)
    return pl.pallas_call(
        functools.partial(_cmp_kernel, tq=tq, nsel=nsel),
        grid=(batch * N_KV_GROUPS, nq),
        in_specs=[pl.BlockSpec((tq, gq), qmap),
                  pl.BlockSpec((1, nc, HEAD_DIM), lambda bg, i: (bg, 0, 0)),
                  pl.BlockSpec((1, nc, HEAD_DIM), lambda bg, i: (bg, 0, 0)),
                  pl.BlockSpec((tq, LANES), lambda bg, i: ((bg // N_KV_GROUPS) * nq + i, 0)),
                  pl.BlockSpec((nc, LANES), lambda bg, i: (0, 0))],
        out_specs=[pl.BlockSpec((tq, gq), qmap),
                   pl.BlockSpec((tq, LANES), lambda bg, i: (bg * nq + i, 0))],
        out_shape=[jax.ShapeDtypeStruct((t, ATT_W), F32),
                   jax.ShapeDtypeStruct((batch * N_KV_GROUPS * seq, LANES), BF16)],
        compiler_params=_cparams(("parallel", "parallel")),
        name="cmp_attn",
    )(proj, kc, vc, gates, jnp.asarray(ov))


def _selwin_kernel(q_ref, selb_ref, ocmp_ref, gates_ref, ks_ref, vs_ref, kw_ref, vw_ref,
                   cb_ref, wb_ref, o_ref, kaug_s, vaug_s, vwaug_s, qaug_s, sbuf_s, m_s, acc_s,
                   swin_s, *, tq, tk, seq, nwin):
    g = pl.program_id(0) % N_KV_GROUPS
    qi = pl.program_id(1)
    q0 = qi * tq
    hq = HEADS_PER_GROUP * tq
    half = hq // 2

    @pl.when(qi == 0)
    def _():
        r = lax.broadcasted_iota(jnp.int32, (seq, HEAD_DIM), 0)
        ln = lax.broadcasted_iota(jnp.int32, (seq, HEAD_DIM), 1)
        ones_col = jnp.where(ln == 0, 1.0, 0.0).astype(BF16)
        kaug_s[:, 0:HEAD_DIM] = ks_ref[...]
        kaug_s[:, HEAD_DIM:2 * HEAD_DIM] = jnp.where(r // SEL_BLOCK == ln, 1.0, 0.0).astype(BF16)
        vaug_s[:, 0:HEAD_DIM] = vs_ref[...]
        vaug_s[:, HEAD_DIM:2 * HEAD_DIM] = ones_col
        vwaug_s[:, 0:HEAD_DIM] = vw_ref[...]
        vwaug_s[:, HEAD_DIM:2 * HEAD_DIM] = ones_col

    for h in range(HEADS_PER_GROUP):
        qaug_s[h * tq:(h + 1) * tq, 0:HEAD_DIM] = q_ref[:, h * HEAD_DIM:(h + 1) * HEAD_DIM]
        qaug_s[h * tq:(h + 1) * tq, HEAD_DIM:2 * HEAD_DIM] = selb_ref[...]
    qa = qaug_s[...]

    def scores(kt, slot):
        ks = kaug_s[pl.ds(pl.multiple_of(kt * tk, tk), tk), :]
        sbuf_s[slot] = _dot_nt(qa, ks)

    def absorb(kt, slot, diagonal=False):
        vs = vaug_s[pl.ds(pl.multiple_of(kt * tk, tk), tk), :]
        s = sbuf_s[slot]
        if diagonal:
            bias = cb_ref[(q0 - kt * tk) // tq]
            s = (s.reshape(HEADS_PER_GROUP, tq, tk) + bias[None]).reshape(hq, tk)
        m = m_s[...]
        m_new = jnp.maximum(m, jnp.max(s, axis=-1, keepdims=True))
        m_s[...] = m_new
        pb = jnp.exp2(s - m_new[:, 0:1]).astype(BF16)
        if diagonal:
            pv = jnp.concatenate([_dot(pb[0:half], vs), _dot(pb[half:hq], vs)], axis=0)
        else:
            pv = _dot(pb, vs)
        acc_s[...] = jnp.exp2(m - m_new)[:, 0:1] * acc_s[...] + pv

    n_full = q0 // tk
    m_s[...] = jnp.full(m_s.shape, -jnp.inf, F32)
    acc_s[...] = jnp.zeros(acc_s.shape, F32)
    scores(0, 0)

    def pair(j, c):
        scores(2 * j + 1, 1)
        absorb(2 * j, 0)
        scores(2 * j + 2, 0)
        absorb(2 * j + 1, 1)
        return c

    lax.fori_loop(0, n_full // 2, pair, 0)
    odd = n_full % 2

    @pl.when(odd == 1)
    def _():
        scores(n_full, 1)
        absorb(n_full - 1, 0)

    nq = seq // tq
    w0 = jnp.clip(qi - (nwin - 1), 0, nq - nwin) * tq
    wlen = nwin * tq
    kwin = kw_ref[pl.ds(pl.multiple_of(w0, tq), wlen), :]
    vwin = vwaug_s[pl.ds(pl.multiple_of(w0, tq), wlen), :]
    swin_s[0:half, :] = _dot_nt(qaug_s[0:half, 0:HEAD_DIM], kwin)
    swin_s[half:hq, :] = _dot_nt(qaug_s[half:hq, 0:HEAD_DIM], kwin)
    absorb(n_full, odd, diagonal=True)
    acc = acc_s[...]
    o_sel = acc[:, 0:HEAD_DIM] / acc[:, HEAD_DIM:HEAD_DIM + 1]

    wbias = wb_ref[(q0 - w0) // tq]
    sw = (swin_s[...].reshape(HEADS_PER_GROUP, tq, wlen) + wbias[None]).reshape(hq, wlen)
    pwb = jnp.exp2(sw - jnp.max(sw, axis=-1, keepdims=True)).astype(BF16)
    ow = jnp.concatenate([_dot(pwb[0:half], vwin), _dot(pwb[half:hq], vwin)], axis=0)
    o_win = ow[:, 0:HEAD_DIM] / ow[:, HEAD_DIM:HEAD_DIM + 1]

    gl = gates_ref[...]
    lane = lax.broadcasted_iota(jnp.int32, (tq, LANES), 1)
    for h in range(HEADS_PER_GROUP):
        gcol = (g * HEADS_PER_GROUP + h) * 3
        g_s = _sigmoid(_lane_col(gl, lane, gcol + 1))
        g_w = _sigmoid(_lane_col(gl, lane, gcol + 2))
        o = (ocmp_ref[:, h * HEAD_DIM:(h + 1) * HEAD_DIM]
             + g_s * o_sel[h * tq:(h + 1) * tq, :] + g_w * o_win[h * tq:(h + 1) * tq, :])
        o_ref[:, h * HEAD_DIM:(h + 1) * HEAD_DIM] = o.astype(BF16)


def _selwin_attn(proj, selb, ocmp, gates, batch, seq, tq, tk):
    t = proj.shape[0]
    nq = seq // tq
    assert tk % tq == 0 and tq & (tq - 1) == 0 and seq % tk == 0
    nwin = min(WINDOW // tq + 1, nq)
    assert WINDOW % tq == 0
    gq = HEADS_PER_GROUP * HEAD_DIM
    hq = HEADS_PER_GROUP * tq
    r = np.arange(tq)[:, None]
    cbias = np.stack([np.where(np.arange(tk)[None, :] <= r + off * tq, 0.0, -np.inf)
                      for off in range(tk // tq)]).astype(np.float32)
    dist = [r + off * tq - np.arange(nwin * tq)[None, :] for off in range(nwin)]
    wbias = np.stack([np.where((d >= 0) & (d < WINDOW), 0.0, -np.inf)
                      for d in dist]).astype(np.float32)
    qmap = lambda bg, i: ((bg // N_KV_GROUPS) * nq + i, bg % N_KV_GROUPS)

    def kvspec(col):
        return pl.BlockSpec((seq, HEAD_DIM),
                            lambda bg, i: (bg // N_KV_GROUPS, col // HEAD_DIM + bg % N_KV_GROUPS))

    return pl.pallas_call(
        functools.partial(_selwin_kernel, tq=tq, tk=tk, seq=seq, nwin=nwin),
        grid=(batch * N_KV_GROUPS, nq),
        in_specs=[pl.BlockSpec((tq, gq), qmap),
                  pl.BlockSpec((tq, LANES), lambda bg, i: (bg * nq + i, 0)),
                  pl.BlockSpec((tq, gq), qmap),
                  pl.BlockSpec((tq, LANES), lambda bg, i: ((bg // N_KV_GROUPS) * nq + i, 0)),
                  kvspec(COL_KS), kvspec(COL_VS), kvspec(COL_KW), kvspec(COL_VW),
                  pl.BlockSpec(cbias.shape, lambda bg, i: (0, 0, 0)),
                  pl.BlockSpec(wbias.shape, lambda bg, i: (0, 0, 0))],
        out_specs=pl.BlockSpec((tq, gq), qmap),
        out_shape=jax.ShapeDtypeStruct((t, ATT_W), BF16),
        scratch_shapes=[pltpu.VMEM((seq, 2 * HEAD_DIM), BF16),
                        pltpu.VMEM((seq, 2 * HEAD_DIM), BF16),
                        pltpu.VMEM((seq, 2 * HEAD_DIM), BF16),
                        pltpu.VMEM((hq, 2 * HEAD_DIM), BF16),
                        pltpu.VMEM((2, hq, tk), F32),
                        pltpu.VMEM((hq, LANES), F32),
                        pltpu.VMEM((hq, 2 * HEAD_DIM), F32),
                        pltpu.VMEM((hq, nwin * tq), F32)],
        compiler_params=_cparams(("parallel", "arbitrary")),
        name="selwin_attn",
    )(proj, selb, ocmp, gates, proj, proj, proj, proj, jnp.asarray(cbias), jnp.asarray(wbias))


_HGRN_LEVELS = (32, 16, 8, 4, 2, 1)


def _hgrn_tables():
    c = HGRN_CHUNK
    nl = len(_HGRN_LEVELS)
    idx = np.arange(c)
    i, j = idx[:, None], idx[None, :]
    mats = [j <= i, j > i]
    lvl = np.full((c, c), nl + 1, np.int32)
    for li, s in enumerate(_HGRN_LEVELS):
        mats.append((j >= (i // s) * s) & (j <= i))
        mats.append((j > i) & (j <= (i // s) * s + s - 1))
        lvl[((i // (2 * s)) == (j // (2 * s))) & (((i // s) % 2) == 1) & (((j // s) % 2) == 0)] = li
    lvl[i == j] = nl
    w = np.concatenate(mats, 0).astype(np.float32)
    return np.concatenate([w, w], axis=1), lvl


def _hgrn_head(hq_ref, hf_ref, hi_ref, hg_ref, lbl_ref, gn_ref, w_ref, lvl_ref, o_ref, st_s,
               hh, ts):
    c = HGRN_CHUNK
    nc = ts // c
    nl = len(_HGRN_LEVELS)
    cs = slice(hh * HEAD_DIM, (hh + 1) * HEAD_DIM)
    lbl = lbl_ref[:, cs]
    e = jnp.exp(lbl - jnp.max(lbl, axis=0, keepdims=True))
    lb = e[0:1, :] / jnp.sum(e, axis=0, keepdims=True)

    f = lb + (1.0 - lb) * _sigmoid(hf_ref[:, cs].astype(F32))
    k = 1.0 - f
    qf = _silu(hq_ref[:, cs].astype(F32))
    v = hi_ref[:, cs]

    lf = jnp.log(f) * LOG2E
    lf_hi = lf.astype(BF16)
    lf_lo = (lf - lf_hi.astype(F32)).astype(BF16)
    side = lambda a: jnp.concatenate([a[ci * c:(ci + 1) * c, :] for ci in range(nc)], axis=1)
    ex = _dot(w_ref[...], jnp.concatenate([side(lf_hi), side(lf_lo)], axis=0))

    def rows(bi):
        return jnp.concatenate(
            [ex[bi * c:(bi + 1) * c, ci * HEAD_DIM:(ci + 1) * HEAD_DIM] for ci in range(nc)], axis=0)

    b = rows(0)
    qhat = (qf * jnp.exp2(b)).astype(BF16)
    khat = (k * jnp.exp2(rows(1))).astype(BF16)
    lvl = lvl_ref[...]
    masks = [lvl == li for li in range(nl + 1)]
    qs = [(qf * jnp.exp2(rows(2 + 2 * li))).astype(BF16) for li in range(nl)] + [qf.astype(BF16)]
    ks = [(k * jnp.exp2(rows(3 + 2 * li))).astype(BF16) for li in range(nl)] + [k.astype(BF16)]
    o_intra = []
    for ci in range(nc):
        sl = slice(ci * c, (ci + 1) * c)
        a = jnp.zeros((c, c), F32)
        for li in range(nl + 1):
            a = jnp.where(masks[li], _dot_nt(qs[li][sl, :], ks[li][sl, :]), a)
        o_intra.append(_dot(a.astype(BF16), v[sl, :]))

    st = st_s[hh]
    outs = []
    for ci in range(nc):
        sl = slice(ci * c, (ci + 1) * c)
        outs.append(o_intra[ci] + _dot_nt(qhat[sl, :], st.astype(BF16)))
        st = st * jnp.exp2(b[(ci + 1) * c - 1:(ci + 1) * c, :]) + lax.dot_general(
            v[sl, :], khat[sl, :], _TN, preferred_element_type=F32)
    st_s[hh] = st
    o = jnp.concatenate(outs, axis=0)
    o = o * lax.rsqrt(jnp.mean(o * o, axis=-1, keepdims=True) + EPS) * gn_ref[:, cs]
    o_ref[:, cs] = (o * _silu(hg_ref[:, cs].astype(F32))).astype(BF16)


def _hgrn_kernel(hq_ref, hf_ref, hi_ref, hg_ref, lbl_ref, gn_ref, w_ref, lvl_ref,
                 o_ref, st_s, *, ts, hb):
    @pl.when(pl.program_id(2) == 0)
    def _():
        st_s[...] = jnp.zeros_like(st_s)

    for hh in range(hb):
        _hgrn_head(hq_ref, hf_ref, hi_ref, hg_ref, lbl_ref, gn_ref, w_ref, lvl_ref, o_ref, st_s,
                   hh, ts)


def _hgrn(proj, lb_logits, g_norm, batch, seq, ts, hb):
    t = proj.shape[0]
    ns = seq // ts
    wmat, lvl = _hgrn_tables()
    w = hb * HEAD_DIM

    def colspec(col):
        return pl.BlockSpec((ts, w), lambda b, h, s: (b * ns + s, col // w + h))

    nl = lb_logits.shape[0]
    return pl.pallas_call(
        functools.partial(_hgrn_kernel, ts=ts, hb=hb),
        grid=(batch, N_HGRN_HEADS // hb, ns),
        in_specs=[colspec(COL_HQ), colspec(COL_HF), colspec(COL_HI), colspec(COL_HG),
                  pl.BlockSpec((nl, w), lambda b, h, s: (0, h)),
                  pl.BlockSpec((1, w), lambda b, h, s: (0, h)),
                  pl.BlockSpec(wmat.shape, lambda b, h, s: (0, 0)),
                  pl.BlockSpec(lvl.shape, lambda b, h, s: (0, 0))],
        out_specs=pl.BlockSpec((ts, w), lambda b, h, s: (b * ns + s, h)),
        out_shape=jax.ShapeDtypeStruct((t, ATT_W), BF16),
        scratch_shapes=[pltpu.VMEM((hb, HEAD_DIM, HEAD_DIM), F32)],
        compiler_params=_cparams(("parallel", "parallel", "arbitrary")),
        name="hgrn2",
    )(proj, proj, proj, proj, lb_logits, g_norm, jnp.asarray(wmat, BF16), jnp.asarray(lvl))


def _outproj_kernel(oa_ref, or_ref, wa_ref, wb_ref, x_ref, mod_ref, gpost_ref, gpre_ref,
                    wrh_ref, wrl_ref, br_ref, x1_ref, h2_ref, lg_ref):
    m = mod_ref[0]
    y = _dot(oa_ref[...], wa_ref[...]) + _dot(or_ref[...], wb_ref[...])
    x1 = x_ref[...] + m[2:3, :] * _rms(y, gpost_ref[...])
    x1_ref[...] = x1
    h = _rms(x1, gpre_ref[...]) * (1.0 + m[4:5, :]) + m[3:4, :]
    h2_ref[...] = _pack_halves(h)
    h_hi = h.astype(BF16)
    h_lo = (h - h_hi.astype(F32)).astype(BF16)
    lg_ref[...] = (_dot(h_hi, wrh_ref[...]) + _dot(h_lo, wrh_ref[...])
                   + _dot(h_hi, wrl_ref[...]) + br_ref[...])


def _outproj(o_att, o_rec, w_out, x2, mod3, g_post, g_pre, wr, br, seq, tm):
    t, d = x2.shape
    per_b = seq // tm
    wr_hi = wr.astype(BF16)
    wr_lo = (wr - wr_hi.astype(F32)).astype(BF16)
    wa = w_out[:ATT_W].astype(BF16)
    wb = w_out[ATT_W:].astype(BF16)
    row = lambda i: (i, 0)
    fixed = lambda i: (0, 0)
    return pl.pallas_call(
        _outproj_kernel,
        grid=(t // tm,),
        in_specs=[pl.BlockSpec((tm, ATT_W), row), pl.BlockSpec((tm, ATT_W), row),
                  pl.BlockSpec((ATT_W, d), fixed), pl.BlockSpec((ATT_W, d), fixed),
                  pl.BlockSpec((tm, d), row),
                  pl.BlockSpec((1, 6, d), lambda i: (i // per_b, 0, 0)),
                  pl.BlockSpec((1, d), fixed), pl.BlockSpec((1, d), fixed),
                  pl.BlockSpec((d, LANES), fixed), pl.BlockSpec((d, LANES), fixed),
                  pl.BlockSpec((1, LANES), fixed)],
        out_specs=[pl.BlockSpec((tm, d), row), pl.BlockSpec((tm, d // 2), row),
                   pl.BlockSpec((tm, LANES), row)],
        out_shape=[jax.ShapeDtypeStruct((t, d), F32), jax.ShapeDtypeStruct((t, d // 2), U32),
                   jax.ShapeDtypeStruct((t, LANES), F32)],
        compiler_params=_cparams(("parallel",)),
        name="outproj",
    )(o_att, o_rec, wa, wb, x2, mod3, g_post, g_pre, wr_hi, wr_lo, br)


G_LANE0 = N_EXPERTS


def _route_kernel(lg_ref, tri_ref, info_ref, cnt_ref, carry_s):
    @pl.when(pl.program_id(0) == 0)
    def _():
        carry_s[...] = jnp.zeros_like(carry_s)

    x = lg_ref[...]
    lane = lax.broadcasted_iota(jnp.int32, x.shape, 1)
    big = jnp.int32(10 ** 6)
    rmax = lambda a: jnp.max(a, axis=-1, keepdims=True)
    rmin = lambda a: jnp.min(a, axis=-1, keepdims=True)
    rsum = lambda a: jnp.sum(a, axis=-1, keepdims=True)

    is_g = (lane >= G_LANE0) & (lane < G_LANE0 + N_EXPERT_GROUPS)
    lgm = jnp.where(is_g, x, -jnp.inf)
    mg = rmax(lgm)
    pg_sel = 1.0 / rsum(jnp.where(is_g, jnp.exp(lgm - mg), 0.0))
    gsel = rmin(jnp.where(lgm == mg, lane, big)) - G_LANE0

    is_e = (lane >= gsel * EXPERTS_PER_GROUP) & (lane < (gsel + 1) * EXPERTS_PER_GROUP)
    lem = jnp.where(is_e, x, -jnp.inf)
    pe = jnp.where(is_e, jnp.exp(lem - rmax(lem)), 0.0)
    pe = pe / rsum(pe)
    pe = jnp.where(is_e, pe, -1.0)
    v1 = rmax(pe)
    i1 = rmin(jnp.where(pe == v1, lane, big))
    pe2 = jnp.where(lane == i1, -1.0, pe)
    v2 = rmax(pe2)
    i2 = rmin(jnp.where(pe2 == v2, lane, big))
    w1 = v1 / (v1 + v2) * pg_sel
    w2 = v2 / (v1 + v2) * pg_sel

    oh1 = jnp.where(lane == i1, 1.0, 0.0)
    oh2 = jnp.where(lane == i2, 1.0, 0.0)
    both = oh1 + oh2
    before = _dot(tri_ref[...], both.astype(BF16)) + carry_s[0:1, :]
    r1 = rsum(oh1 * before)
    r2 = rsum(oh2 * before)
    carry_s[0:1, :] = carry_s[0:1, :] + jnp.sum(both, axis=0, keepdims=True)
    cnt_ref[...] = carry_s[...]

    cols = (i1.astype(F32), i2.astype(F32), w1, w2, r1, r2)
    info = jnp.zeros(x.shape, F32)
    for ci, col in enumerate(cols):
        info = jnp.where(lane == ci, col, info)
    info_ref[...] = info


def _route(logits, tm):
    t = logits.shape[0]
    tri = jnp.asarray(np.tril(np.ones((tm, tm), np.float32), -1), BF16)
    return pl.pallas_call(
        _route_kernel,
        grid=(t // tm,),
        in_specs=[pl.BlockSpec((tm, LANES), lambda i: (i, 0)),
                  pl.BlockSpec((tm, tm), lambda i: (0, 0))],
        out_specs=[pl.BlockSpec((tm, LANES), lambda i: (i, 0)),
                   pl.BlockSpec((8, LANES), lambda i: (0, 0))],
        out_shape=[jax.ShapeDtypeStruct((t, LANES), F32),
                   jax.ShapeDtypeStruct((8, LANES), F32)],
        scratch_shapes=[pltpu.VMEM((8, LANES), F32)],
        compiler_params=_cparams(("arbitrary",)),
        name="route",
    )(logits, tri)


def _row_copy(src, i, dst, j, sem):
    return pltpu.make_async_copy(src.at[pl.ds(i, 1)], dst.at[pl.ds(j, 1)], sem)


def _expert_kernel(te_ref, nu_ref, src0_ref, srcn_ref, dst_ref, h_ref, wg_ref, wu_ref, wd_ref,
                   y_ref, xbuf, obuf, gsem, ssem, wg_s, wu_s, wd_s, *, tm, standin_row):
    t = pl.program_id(0)
    nu = nu_ref[0]
    slot = t % 2
    half = wg_s.shape[0] // 2

    def gather(src_ref, s):
        for r in range(tm):
            _row_copy(h_ref, src_ref[0, 0, r], xbuf.at[s], r, gsem.at[s]).start()

    def scatter(s, row_of):
        for r in range(tm):
            _row_copy(obuf.at[s], r, y_ref, row_of(r), ssem.at[s]).start()

    def wait_rows(buf, sem, s):
        pltpu.make_async_copy(buf.at[s], buf.at[s], sem.at[s]).wait()

    @pl.when(t < nu)
    def _():
        @pl.when(t == 0)
        def _():
            gather(src0_ref, 0)
            obuf[1] = jnp.zeros(obuf.shape[1:], obuf.dtype)

            def fill(row0, s):
                return pltpu.make_async_copy(obuf.at[1], y_ref.at[pl.ds(row0, tm)], ssem.at[s])

            fill(standin_row - 2 * tm, 0).start()
            fill(standin_row - tm, 0).start()
            fill(standin_row - 2 * tm, 0).wait()
            fill(standin_row - tm, 0).wait()
            fill(standin_row, 1).start()

        @pl.when((t == 0) | (te_ref[t] != te_ref[jnp.maximum(t - 1, 0)]))
        def _():
            wg_s[...] = wg_ref[0].astype(BF16)
            wu_s[...] = wu_ref[0].astype(BF16)
            wd_s[...] = wd_ref[0].astype(BF16)

        gather(srcn_ref, 1 - slot)
        wait_rows(xbuf, gsem, slot)
        x_lo, x_hi = _unpack_halves(xbuf[slot])
        x_lo = x_lo.astype(BF16)
        x_hi = x_hi.astype(BF16)
        g = _dot(x_lo, wg_s[0:half, :]) + _dot(x_hi, wg_s[half:2 * half, :])
        u = _dot(x_lo, wu_s[0:half, :]) + _dot(x_hi, wu_s[half:2 * half, :])
        hid = (_silu(g) * u).astype(BF16)
        obuf[slot] = _pack_halves(_dot(hid, wd_s[...]))
        scatter(slot, lambda r: dst_ref[0, 0, r])
        wait_rows(obuf, ssem, 1 - slot)

        @pl.when(t == nu - 1)
        def _():
            wait_rows(obuf, ssem, slot)
            wait_rows(xbuf, gsem, 1 - slot)


def _experts(tile_expert, n_used, src3, dst3, h2p, w_gate, w_up, w_down, y_rows, tm,
             standin_row):
    n_tiles = src3.shape[0]
    dp = h2p.shape[1]
    d, f = w_gate.shape[1:]
    wmap = lambda t, te, nu: (te[t], 0, 0)
    smem = lambda imap: pl.BlockSpec((1, 1, tm), imap, memory_space=pltpu.SMEM)
    return pl.pallas_call(
        functools.partial(_expert_kernel, tm=tm, standin_row=standin_row),
        grid_spec=pltpu.PrefetchScalarGridSpec(
            num_scalar_prefetch=2,
            grid=(n_tiles,),
            in_specs=[smem(lambda t, te, nu: (0, 0, 0)),
                      smem(lambda t, te, nu: (jnp.minimum(t + 1, nu[0] - 1), 0, 0)),
                      smem(lambda t, te, nu: (t, 0, 0)),
                      pl.BlockSpec(memory_space=pl.ANY),
                      pl.BlockSpec((1, d, f), wmap),
                      pl.BlockSpec((1, d, f), wmap),
                      pl.BlockSpec((1, f, d), wmap)],
            out_specs=pl.BlockSpec(memory_space=pl.ANY),
            scratch_shapes=[pltpu.VMEM((2, tm, dp), U32), pltpu.VMEM((2, tm, dp), U32),
                            pltpu.SemaphoreType.DMA((2,)), pltpu.SemaphoreType.DMA((2,)),
                            pltpu.VMEM((d, f), BF16), pltpu.VMEM((d, f), BF16),
                            pltpu.VMEM((f, d), BF16)]),
        out_shape=jax.ShapeDtypeStruct((y_rows, dp), U32),
        compiler_params=_cparams(("arbitrary",)),
        name="moe_experts",
    )(tile_expert, n_used, src3, src3, dst3, h2p, w_gate, w_up, w_down)


def _final_kernel(info_ref, x1_ref, mod_ref, g_ref, ya_ref, yb_ref, o_ref):
    info = info_ref[...]
    w1 = info[:, 2:3]
    w2 = info[:, 3:4]
    a_lo, a_hi = _unpack_halves(ya_ref[...])
    b_lo, b_hi = _unpack_halves(yb_ref[...])
    y_lo = w1 * a_lo + w2 * b_lo
    y_hi = w1 * a_hi + w2 * b_hi
    half = y_lo.shape[1]
    ms = (jnp.sum(y_lo * y_lo, axis=-1, keepdims=True)
          + jnp.sum(y_hi * y_hi, axis=-1, keepdims=True)) / (2 * half)
    inv = lax.rsqrt(ms + EPS)
    gt = mod_ref[0][5:6, :]
    g = g_ref[...]
    o_ref[:, 0:half] = x1_ref[:, 0:half] + gt[:, 0:half] * (y_lo * inv * g[:, 0:half])
    o_ref[:, half:] = x1_ref[:, half:] + gt[:, half:] * (y_hi * inv * g[:, half:])


def _moe_final(info, x1, mod3, g_post, y2, seq, tm):
    t, d = x1.shape
    per_b = seq // tm
    nb = t // tm
    return pl.pallas_call(
        _final_kernel,
        grid=(nb,),
        in_specs=[pl.BlockSpec((tm, LANES), lambda i: (i, 0)),
                  pl.BlockSpec((tm, d), lambda i: (i, 0)),
                  pl.BlockSpec((1, 6, d), lambda i: (i // per_b, 0, 0)),
                  pl.BlockSpec((1, d), lambda i: (0, 0)),
                  pl.BlockSpec((tm, d // 2), lambda i: (i, 0)),
                  pl.BlockSpec((tm, d // 2), lambda i: (nb + i, 0))],
        out_specs=pl.BlockSpec((tm, d), lambda i: (i, 0)),
        out_shape=jax.ShapeDtypeStruct((t, d), F32),
        compiler_params=_cparams(("parallel",)),
        name="moe_final",
    )(info, x1, mod3, g_post, y2, y2)


def _moe(h2p, logits, x1, mod3, g_post, w_gate, w_up, w_down, seq, tm_route, tm_exp, tm_fin):
    t, dp = h2p.shape
    info, cnt = _route(logits, tm_route)
    counts = cnt[0, :N_EXPERTS].astype(jnp.int32)
    tiles_e = (counts + tm_exp - 1) // tm_exp
    tile_end = jnp.cumsum(tiles_e)
    offs = (tile_end - tiles_e) * tm_exp
    n_used = tile_end[-1]
    e12 = info[:, 0:2].astype(jnp.int32)
    pos = offs[e12] + info[:, 4:6].astype(jnp.int32)
    n_tiles = (2 * t) // tm_exp + N_EXPERTS
    tidx = jnp.minimum(jnp.arange(n_tiles, dtype=jnp.int32), n_used - 1)
    tile_expert = jnp.sum(tidx[:, None] >= tile_end[None, :], axis=1).astype(jnp.int32)
    tok = jnp.arange(t, dtype=jnp.int32)
    slot_ids = jnp.arange(n_tiles * tm_exp, dtype=jnp.int32)
    flat = jnp.concatenate([pos[:, 0], pos[:, 1]])
    spare = 2 * t + ((slot_ids // tm_exp) % 2) * tm_exp + slot_ids % tm_exp
    src = jnp.zeros_like(slot_ids).at[flat].set(jnp.concatenate([tok, tok]), unique_indices=True)
    dst = spare.at[flat].set(jnp.concatenate([tok, t + tok]), unique_indices=True)

    ew = w_gate.shape
    wg = w_gate.reshape(N_EXPERTS, ew[-2], ew[-1])
    wu = w_up.reshape(N_EXPERTS, ew[-2], ew[-1])
    wd = w_down.reshape(N_EXPERTS, ew[-1], ew[-2])
    y2 = _experts(tile_expert, n_used.reshape(1).astype(jnp.int32),
                  src.reshape(n_tiles, 1, tm_exp), dst.reshape(n_tiles, 1, tm_exp),
                  h2p, wg, wu, wd, 2 * t + 3 * tm_exp, tm_exp, 2 * t + 2 * tm_exp)
    return _moe_final(info, x1, mod3, g_post, y2, seq, tm_fin)


def _permute_w_in(w_in):
    sizes = [ATT_W] + [KV_W] * 6 + [3 * N_ATT_HEADS] + [ATT_W] * 4
    cuts = np.cumsum(sizes)[:-1]
    q, kc, vc, ks, vs, kw, vw, gates, hq, hf, hi, hg = jnp.split(w_in, cuts, axis=1)
    w_p = jnp.concatenate([q, kc, ks, kw, vc, vs, vw, hq, hf, hi, hg], axis=1).astype(BF16)
    w_g = jnp.pad(gates, ((0, 0), (0, LANES - gates.shape[1]))).astype(BF16)
    return w_p, w_g


def _half_blocks(proj, col, batch, seq):
    a = proj[:, col:col + KV_W].reshape(batch, seq, N_KV_GROUPS, HEAD_DIM)
    a = a.transpose(0, 2, 1, 3)
    return a.reshape(batch * N_KV_GROUPS, seq // CMP_STRIDE, CMP_STRIDE * HEAD_DIM)


def _block(x, c, positions, w_ada, b_ada, g_pre_mix, g_post_mix, g_pre_ffn, g_post_ffn,
           w_in, w_out, pe_k, w1_k, w2_k, pe_v, w1_v, w2_v, lb_logits, g_norm,
           w_group, b_group, w_router, b_router, w_gate, w_up, w_down, cfg):
    batch, seq, d = x.shape
    t = batch * seq
    x2 = x.reshape(t, d)

    c8 = jnp.zeros((8, d), F32).at[:batch].set(c)
    mod3 = _ada_mod(c8, w_ada, b_ada)[:batch].reshape(batch, 6, d)
    cos_t, sin_t = _rope_tables(positions, cfg["tm_rope"])
    w_p, w_g = _permute_w_in(w_in)
    proj, gates = _inproj(x2, mod3, g_pre_mix.reshape(1, d), w_p, w_g, cos_t, sin_t,
                          seq, cfg["tm_in"])

    kc = _compress(_half_blocks(proj, COL_KC, batch, seq), w1_k, pe_k, w2_k)
    vc = _compress(_half_blocks(proj, COL_VC, batch, seq), w1_v, pe_v, w2_v)
    ocmp, selb = _cmp_attn(proj, kc, vc, gates, batch, seq, cfg["tq_cmp"])
    o_att = _selwin_attn(proj, selb, ocmp, gates, batch, seq, cfg["tq"], cfg["tk"])
    o_rec = _hgrn(proj, lb_logits, g_norm.reshape(1, -1), batch, seq, cfg["ts_hgrn"],
                  cfg["hb_hgrn"])

    wr = jnp.concatenate([w_router, w_group], axis=1)
    wr = jnp.pad(wr, ((0, 0), (0, LANES - wr.shape[1])))
    br = jnp.pad(jnp.concatenate([b_router, b_group]), (0, LANES - N_EXPERTS - N_EXPERT_GROUPS))
    x1, h2p, logits = _outproj(o_att, o_rec, w_out, x2, mod3, g_post_mix.reshape(1, d),
                               g_pre_ffn.reshape(1, d), wr, br.reshape(1, LANES), seq,
                               cfg["tm_out"])
    out = _moe(h2p, logits, x1, mod3, g_post_ffn.reshape(1, d), w_gate, w_up, w_down, seq,
               cfg["tm_route"], cfg["tm_exp"], cfg["tm_fin"])
    return out.reshape(batch, seq, d)


def _config(seq):
    return dict(tm_rope=min(1024, seq), tm_in=min(1024, seq), tq_cmp=min(256, seq),
                tq=min(256, seq), tk=min(512, seq), ts_hgrn=min(256, seq), hb_hgrn=4,
                tm_out=min(256, seq), tm_route=min(256, seq), tm_exp=256,
                tm_fin=min(512, seq))


def kernel(x, c, positions, w_ada, b_ada, g_pre_mix, g_post_mix, g_pre_ffn, g_post_ffn, w_in, w_out, cmp_pe_k, cmp_w1_k, cmp_w2_k, cmp_pe_v, cmp_w1_v, cmp_w2_v, hgrn_lb_logits, hgrn_g_norm, w_group, b_group, w_router, b_router, w_gate, w_up, w_down):
    assert w_ada.shape[0] == 1, "single-layer block"
    return _block(x, c, positions, w_ada[0], b_ada[0], g_pre_mix[0], g_post_mix[0],
                  g_pre_ffn[0], g_post_ffn[0], w_in[0], w_out[0], cmp_pe_k[0], cmp_w1_k[0],
                  cmp_w2_k[0], cmp_pe_v[0], cmp_w1_v[0], cmp_w2_v[0], hgrn_lb_logits,
                  hgrn_g_norm[0], w_group[0], b_group[0], w_router[0], b_router[0],
                  w_gate[0], w_up[0], w_down[0], _config(x.shape[1]))
```

```python
import functools

import numpy as np
import jax
import jax.numpy as jnp
from jax import lax
from jax.experimental import pallas as pl
from jax.experimental.pallas import tpu as pltpu

F32 = jnp.float32
BF16 = jnp.bfloat16

HEAD_DIM = 128
N_KV_GROUPS = 2
HEADS_PER_GROUP = 4
N_ATT_HEADS = N_KV_GROUPS * HEADS_PER_GROUP
CMP_BLOCK = 32
CMP_STRIDE = 16
SEL_BLOCK = 64
N_SEL = 16
WINDOW = 512
FORCE_BONUS = 1.0e4
ROPE_THETA = 10000.0
N_HGRN_HEADS = 8
HGRN_CHUNK = 64
N_EXPERT_GROUPS = 4
EXPERTS_PER_GROUP = 8
N_EXPERTS = N_EXPERT_GROUPS * EXPERTS_PER_GROUP
EPS = 1e-6

LANES = 128
VMEM_LIMIT = 56 * 1024 * 1024

ATT_W = N_ATT_HEADS * HEAD_DIM
KV_W = N_KV_GROUPS * HEAD_DIM
COL_Q = 0
COL_KC = COL_Q + ATT_W
COL_KS = COL_KC + KV_W
COL_KW = COL_KS + KV_W
COL_VC = COL_KW + KV_W
COL_VS = COL_VC + KV_W
COL_VW = COL_VS + KV_W
COL_HQ = COL_VW + KV_W
COL_HF = COL_HQ + ATT_W
COL_HI = COL_HF + ATT_W
COL_HG = COL_HI + ATT_W
PROJ_W = COL_HG + ATT_W
ROPE_W = COL_VC

MASK_BIG = float(2 ** 30)
LOG2E = 1.4426950408889634
Q_PRESCALE = HEAD_DIM ** -0.5 * LOG2E

_NT = (((1,), (1,)), ((), ()))
_TN = (((0,), (0,)), ((), ()))


def _cparams(sem):
    return pltpu.CompilerParams(dimension_semantics=sem, vmem_limit_bytes=VMEM_LIMIT)


def _dot(a, b, **kw):
    return jnp.dot(a, b, preferred_element_type=F32, **kw)


def _dot_nt(a, b):
    return lax.dot_general(a, b, _NT, preferred_element_type=F32)


def _sigmoid(x):
    return 1.0 / (1.0 + jnp.exp(-x))


def _silu(x):
    return x * _sigmoid(x)


def _rms(x, g):
    return x * lax.rsqrt(jnp.mean(x * x, axis=-1, keepdims=True) + EPS) * g


def _lane_col(x, lane, col):
    return jnp.sum(jnp.where(lane == col, x, 0.0), axis=-1, keepdims=True)


def _ada_kernel(c_ref, w_ref, b_ref, o_ref):
    s = _silu(c_ref[...])
    o_ref[...] = _dot(s, w_ref[...], precision=lax.Precision.HIGHEST) + b_ref[...]


def _ada_mod(c8, w_ada, b_ada, tn=512):
    rows, d = c8.shape
    n = w_ada.shape[1]
    return pl.pallas_call(
        _ada_kernel,
        grid=(n // tn,),
        in_specs=[pl.BlockSpec((rows, d), lambda j: (0, 0)),
                  pl.BlockSpec((d, tn), lambda j: (0, j)),
                  pl.BlockSpec((1, tn), lambda j: (0, j))],
        out_specs=pl.BlockSpec((rows, tn), lambda j: (0, j)),
        out_shape=jax.ShapeDtypeStruct((rows, n), F32),
        compiler_params=_cparams(("parallel",)),
        name="ada_mod",
    )(c8, w_ada, b_ada.reshape(1, n))


def _rope_kernel(ang_ref, cos_ref, sin_ref):
    a = ang_ref[...]
    lane = lax.broadcasted_iota(jnp.int32, a.shape, 1)
    cos_ref[...] = jnp.cos(a)
    sin_ref[...] = jnp.where(lane < HEAD_DIM // 2, -jnp.sin(a), jnp.sin(a))


def _rope_tables(positions, tm):
    t = positions.size
    inv_freq = ROPE_THETA ** (-jnp.arange(0, HEAD_DIM, 2, dtype=F32) / HEAD_DIM)
    ang = positions.reshape(t, 1).astype(F32) * jnp.concatenate([inv_freq, inv_freq])[None, :]
    spec = pl.BlockSpec((tm, HEAD_DIM), lambda i: (i, 0))
    return pl.pallas_call(
        _rope_kernel,
        grid=(t // tm,),
        in_specs=[spec],
        out_specs=[spec, spec],
        out_shape=[jax.ShapeDtypeStruct((t, HEAD_DIM), F32)] * 2,
        compiler_params=_cparams(("parallel",)),
        name="rope_tables",
    )(ang)


def _inproj_kernel(x_ref, mod_ref, g_ref, w_ref, wg_ref, cos_ref, sin_ref,
                   proj_ref, gates_ref, h_s, *, tn):
    j = pl.program_id(1)

    @pl.when(j == 0)
    def _():
        m = mod_ref[0]
        h = _rms(x_ref[...], g_ref[...]) * (1.0 + m[1:2, :]) + m[0:1, :]
        hb = h.astype(BF16)
        h_s[...] = hb
        gates_ref[...] = _dot(hb, wg_ref[...])

    heads = tn // HEAD_DIM
    n_rope_tiles = ROPE_W // tn
    rem_heads = (ROPE_W % tn) // HEAD_DIM

    def store(n_rot, scale=None):
        acc = _dot(h_s[...], w_ref[...])
        for c in range(heads):
            a = acc[:, c * HEAD_DIM:(c + 1) * HEAD_DIM]
            if c < n_rot:
                a = a * cos_ref[...] + pltpu.roll(a, HEAD_DIM // 2, axis=1) * sin_ref[...]
            if scale is not None:
                a = a * scale
            proj_ref[:, c * HEAD_DIM:(c + 1) * HEAD_DIM] = a.astype(BF16)

    @pl.when(j < ATT_W // tn)
    def _():
        store(heads, Q_PRESCALE)

    @pl.when((j >= ATT_W // tn) & (j < n_rope_tiles))
    def _():
        store(heads)

    if rem_heads:
        @pl.when(j == n_rope_tiles)
        def _():
            store(rem_heads)

    @pl.when(j >= n_rope_tiles + (1 if rem_heads else 0))
    def _():
        store(0)


def _inproj(x2, mod3, g_pre, w_in_p, w_gates, cos_t, sin_t, seq, tm, tn=512):
    t, d = x2.shape
    per_b = seq // tm
    return pl.pallas_call(
        functools.partial(_inproj_kernel, tn=tn),
        grid=(t // tm, PROJ_W // tn),
        in_specs=[pl.BlockSpec((tm, d), lambda i, j: (i, 0)),
                  pl.BlockSpec((1, 6, d), lambda i, j: (i // per_b, 0, 0)),
                  pl.BlockSpec((1, d), lambda i, j: (0, 0)),
                  pl.BlockSpec((d, tn), lambda i, j: (0, j)),
                  pl.BlockSpec((d, LANES), lambda i, j: (0, 0)),
                  pl.BlockSpec((tm, HEAD_DIM), lambda i, j: (i, 0)),
                  pl.BlockSpec((tm, HEAD_DIM), lambda i, j: (i, 0))],
        out_specs=[pl.BlockSpec((tm, tn), lambda i, j: (i, j)),
                   pl.BlockSpec((tm, LANES), lambda i, j: (i, 0))],
        out_shape=[jax.ShapeDtypeStruct((t, PROJ_W), BF16),
                   jax.ShapeDtypeStruct((t, LANES), F32)],
        scratch_shapes=[pltpu.VMEM((tm, d), BF16)],
        compiler_params=_cparams(("parallel", "arbitrary")),
        name="inproj",
    )(x2, mod3, g_pre, w_in_p, w_gates, cos_t, sin_t)


def _compress_kernel(h_ref, w1_ref, pe_ref, w2_ref, o_ref):
    hb = h_ref[0]
    half = hb.shape[1]
    ya = _dot(hb, w1_ref[0:half, :])
    yb = _dot(hb, w1_ref[half:2 * half, :])
    const = _dot(pe_ref[...], w1_ref[...])
    n = ya.shape[0]
    yb_next = pltpu.roll(yb, n - 1, axis=0)
    hid = _silu(ya + yb_next + const[0:1, :])
    out = _dot(hid.astype(BF16), w2_ref[...])
    row = lax.broadcasted_iota(jnp.int32, out.shape, 0)
    o_ref[0] = jnp.where(row < n - 1, out, 0.0).astype(BF16)


def _compress(hblk, w1, pe, w2):
    bg, n, half = hblk.shape
    pe8 = jnp.zeros((8, 2 * half), BF16).at[0].set(pe.reshape(-1).astype(BF16))
    return pl.pallas_call(
        _compress_kernel,
        grid=(bg,),
        in_specs=[pl.BlockSpec((1, n, half), lambda i: (i, 0, 0)),
                  pl.BlockSpec((2 * half, HEAD_DIM), lambda i: (0, 0)),
                  pl.BlockSpec((8, 2 * half), lambda i: (0, 0)),
                  pl.BlockSpec((HEAD_DIM, HEAD_DIM), lambda i: (0, 0))],
        out_specs=pl.BlockSpec((1, n, HEAD_DIM), lambda i: (i, 0, 0)),
        out_shape=jax.ShapeDtypeStruct((bg, n, HEAD_DIM), BF16),
        compiler_params=_cparams(("parallel",)),
        name="compress",
    )(hblk, w1.astype(BF16), pe8, w2.astype(BF16))


def _cmp_kernel(q_ref, kc_ref, vc_ref, gates_ref, ov_ref, ocmp_ref, selb_ref, *, tq, nsel):
    g = pl.program_id(0) % N_KV_GROUPS
    q0 = pl.program_id(1) * tq
    kc = kc_ref[0]
    vc = vc_ref[0]
    nc = kc.shape[0]
    t_idx = q0 + lax.broadcasted_iota(jnp.int32, (tq, nc), 0)
    c_idx = lax.broadcasted_iota(jnp.int32, (tq, nc), 1)
    c_ok = c_idx * CMP_STRIDE + (CMP_BLOCK - 1) <= t_idx
    gl = gates_ref[...]
    lane = lax.broadcasted_iota(jnp.int32, (tq, LANES), 1)
    psum = jnp.zeros((tq, nc), F32)
    for h in range(HEADS_PER_GROUP):
        qh = q_ref[:, h * HEAD_DIM:(h + 1) * HEAD_DIM]
        s = jnp.where(c_ok, _dot_nt(qh, kc), -jnp.inf)
        m = jnp.max(s, axis=-1, keepdims=True)
        m = jnp.where(m == -jnp.inf, 0.0, m)
        p = jnp.exp2(s - m)
        p = p / jnp.maximum(jnp.sum(p, axis=-1, keepdims=True), 1e-30)
        psum = psum + p
        o = _dot(p.astype(BF16), vc)
        glog = _lane_col(gl, lane, (g * HEADS_PER_GROUP + h) * 3)
        ocmp_ref[:, h * HEAD_DIM:(h + 1) * HEAD_DIM] = o * _sigmoid(glog)

    imp = _dot(psum, ov_ref[...], precision=lax.Precision.HIGHEST)
    cur = (q0 + lax.broadcasted_iota(jnp.int32, (tq, LANES), 0)) // SEL_BLOCK
    forced = (lane == 0) | (lane == cur) | (lane == cur - 1)
    imp = jnp.where(lane <= cur, imp + jnp.where(forced, FORCE_BONUS, 0.0), -jnp.inf)
    imp_t = imp.T[0:nsel, :]
    row = lax.broadcasted_iota(jnp.int32, (nsel, tq), 0)
    cnt = jnp.zeros((nsel, tq), F32)
    for k in range(nsel):
        vk = imp_t[k:k + 1, :]
        cnt = cnt + jnp.where(vk > imp_t, 1.0, 0.0)
        cnt = cnt + jnp.where((vk == imp_t) & (row > k), 1.0, 0.0)
    sel = (cnt < float(N_SEL)) & (imp_t > -jnp.inf)
    bias_t = jnp.where(sel, 0.0, -MASK_BIG)
    if nsel < LANES:
        bias_t = jnp.concatenate([bias_t, jnp.zeros((LANES - nsel, tq), F32)], axis=0)
    selb_ref[...] = bias_t.T.astype(BF16)


def _cmp_attn(proj, kc, vc, gates, batch, seq, tq):
    t = proj.shape[0]
    nq = seq // tq
    nc = kc.shape[1]
    nsel = seq // SEL_BLOCK
    assert nsel <= LANES and nsel % 8 == 0
    c_start = np.arange(nc) * CMP_STRIDE
    j_start = np.arange(LANES) * SEL_BLOCK
    ov = ((c_start[:, None] < j_start[None, :] + SEL_BLOCK)
          & (c_start[:, None] + CMP_BLOCK > j_start[None, :])
          & (np.arange(LANES)[None, :] < nsel)).astype(np.float32)
    gq = HEADS_PER_GROUP * HEAD_DIM
    qmap = lambda bg, i: ((bg // N_KV_GROUPS) * nq + i, bg % N_KV_GROUPS)
    return pl.pallas_call(
        functools.partial(_cmp_kernel, tq=tq, nsel=nsel),
        grid=(batch * N_KV_GROUPS, nq),
        in_specs=[pl.BlockSpec((tq, gq), qmap),
                  pl.BlockSpec((1, nc, HEAD_DIM), lambda bg, i: (bg, 0, 0)),
                  pl.BlockSpec((1, nc, HEAD_DIM), lambda bg, i: (bg, 0, 0)),
                  pl.BlockSpec((tq, LANES), lambda bg, i: ((bg // N_KV_GROUPS) * nq + i, 0)),
                  pl.BlockSpec((nc, LANES), lambda bg, i: (0, 0))],
        out_specs=[pl.BlockSpec((tq, gq), qmap),
                   pl.BlockSpec((tq, LANES), lambda bg, i: (bg * nq + i, 0))],
        out_shape=[jax.ShapeDtypeStruct((t, ATT_W), F32),
                   jax.ShapeDtypeStruct((batch * N_KV_GROUPS * seq, LANES), BF16)],
        compiler_params=_cparams(("parallel", "parallel")),
        name="cmp_attn",
    )(proj, kc, vc, gates, jnp.asarray(ov))


def _selwin_kernel(q_ref, selb_ref, ocmp_ref, gates_ref, ks_ref, vs_ref, kw_ref, vw_ref,
                   cb_ref, wb_ref, o_ref, kaug_s, vaug_s, vwaug_s, qaug_s, sbuf_s, m_s, acc_s,
                   swin_s, *, tq, tk, seq, nwin):
    g = pl.program_id(0) % N_KV_GROUPS
    qi = pl.program_id(1)
    q0 = qi * tq
    hq = HEADS_PER_GROUP * tq
    half = hq // 2

    @pl.when(qi == 0)
    def _():
        r = lax.broadcasted_iota(jnp.int32, (seq, HEAD_DIM), 0)
        ln = lax.broadcasted_iota(jnp.int32, (seq, HEAD_DIM), 1)
        ones_col = jnp.where(ln == 0, 1.0, 0.0).astype(BF16)
        kaug_s[:, 0:HEAD_DIM] = ks_ref[...]
        kaug_s[:, HEAD_DIM:2 * HEAD_DIM] = jnp.where(r // SEL_BLOCK == ln, 1.0, 0.0).astype(BF16)
        vaug_s[:, 0:HEAD_DIM] = vs_ref[...]
        vaug_s[:, HEAD_DIM:2 * HEAD_DIM] = ones_col
        vwaug_s[:, 0:HEAD_DIM] = vw_ref[...]
        vwaug_s[:, HEAD_DIM:2 * HEAD_DIM] = ones_col

    for h in range(HEADS_PER_GROUP):
        qaug_s[h * tq:(h + 1) * tq, 0:HEAD_DIM] = q_ref[:, h * HEAD_DIM:(h + 1) * HEAD_DIM]
        qaug_s[h * tq:(h + 1) * tq, HEAD_DIM:2 * HEAD_DIM] = selb_ref[...]
    qa = qaug_s[...]

    def scores(kt, slot):
        ks = kaug_s[pl.ds(pl.multiple_of(kt * tk, tk), tk), :]
        sbuf_s[slot] = _dot_nt(qa, ks)

    def absorb(kt, slot, diagonal=False):
        vs = vaug_s[pl.ds(pl.multiple_of(kt * tk, tk), tk), :]
        s = sbuf_s[slot]
        if diagonal:
            bias = cb_ref[(q0 - kt * tk) // tq]
            s = (s.reshape(HEADS_PER_GROUP, tq, tk) + bias[None]).reshape(hq, tk)
        m = m_s[...]
        m_new = jnp.maximum(m, jnp.max(s, axis=-1, keepdims=True))
        m_s[...] = m_new
        pb = jnp.exp2(s - m_new[:, 0:1]).astype(BF16)
        if diagonal:
            pv = jnp.concatenate([_dot(pb[0:half], vs), _dot(pb[half:hq], vs)], axis=0)
        else:
            pv = _dot(pb, vs)
        acc_s[...] = jnp.exp2(m - m_new)[:, 0:1] * acc_s[...] + pv

    n_full = q0 // tk
    m_s[...] = jnp.full(m_s.shape, -jnp.inf, F32)
    acc_s[...] = jnp.zeros(acc_s.shape, F32)
    scores(0, 0)

    def pair(j, c):
        scores(2 * j + 1, 1)
        absorb(2 * j, 0)
        scores(2 * j + 2, 0)
        absorb(2 * j + 1, 1)
        return c

    lax.fori_loop(0, n_full // 2, pair, 0)
    odd = n_full % 2

    @pl.when(odd == 1)
    def _():
        scores(n_full, 1)
        absorb(n_full - 1, 0)

    nq = seq // tq
    w0 = jnp.clip(qi - (nwin - 1), 0, nq - nwin) * tq
    wlen = nwin * tq
    kwin = kw_ref[pl.ds(pl.multiple_of(w0, tq), wlen), :]
    vwin = vwaug_s[pl.ds(pl.multiple_of(w0, tq), wlen), :]
    swin_s[0:half, :] = _dot_nt(qaug_s[0:half, 0:HEAD_DIM], kwin)
    swin_s[half:hq, :] = _dot_nt(qaug_s[half:hq, 0:HEAD_DIM], kwin)
    absorb(n_full, odd, diagonal=True)
    acc = acc_s[...]
    o_sel = acc[:, 0:HEAD_DIM] / acc[:, HEAD_DIM:HEAD_DIM + 1]

    wbias = wb_ref[(q0 - w0) // tq]
    sw = (swin_s[...].reshape(HEADS_PER_GROUP, tq, wlen) + wbias[None]).reshape(hq, wlen)
    pwb = jnp.exp2(sw - jnp.max(sw, axis=-1, keepdims=True)).astype(BF16)
    ow = jnp.concatenate([_dot(pwb[0:half], vwin), _dot(pwb[half:hq], vwin)], axis=0)
    o_win = ow[:, 0:HEAD_DIM] / ow[:, HEAD_DIM:HEAD_DIM + 1]

    gl = gates_ref[...]
    lane = lax.broadcasted_iota(jnp.int32, (tq, LANES), 1)
    for h in range(HEADS_PER_GROUP):
        gcol = (g * HEADS_PER_GROUP + h) * 3
        g_s = _sigmoid(_lane_col(gl, lane, gcol + 1))
        g_w = _sigmoid(_lane_col(gl, lane, gcol + 2))
        o = (ocmp_ref[:, h * HEAD_DIM:(h + 1) * HEAD_DIM]
             + g_s * o_sel[h * tq:(h + 1) * tq, :] + g_w * o_win[h * tq:(h + 1) * tq, :])
        o_ref[:, h * HEAD_DIM:(h + 1) * HEAD_DIM] = o.astype(BF16)


def _selwin_attn(proj, selb, ocmp, gates, batch, seq, tq, tk):
    t = proj.shape[0]
    nq = seq // tq
    assert tk % tq == 0 and tq & (tq - 1) == 0 and seq % tk == 0
    nwin = min(WINDOW // tq + 1, nq)
    assert WINDOW % tq == 0
    gq = HEADS_PER_GROUP * HEAD_DIM
    hq = HEADS_PER_GROUP * tq
    r = np.arange(tq)[:, None]
    cbias = np.stack([np.where(np.arange(tk)[None, :] <= r + off * tq, 0.0, -np.inf)
                      for off in range(tk // tq)]).astype(np.float32)
    dist = [r + off * tq - np.arange(nwin * tq)[None, :] for off in range(nwin)]
    wbias = np.stack([np.where((d >= 0) & (d < WINDOW), 0.0, -np.inf)
                      for d in dist]).astype(np.float32)
    qmap = lambda bg, i: ((bg // N_KV_GROUPS) * nq + i, bg % N_KV_GROUPS)

    def kvspec(col):
        return pl.BlockSpec((seq, HEAD_DIM),
                            lambda bg, i: (bg // N_KV_GROUPS, col // HEAD_DIM + bg % N_KV_GROUPS))

    return pl.pallas_call(
        functools.partial(_selwin_kernel, tq=tq, tk=tk, seq=seq, nwin=nwin),
        grid=(batch * N_KV_GROUPS, nq),
        in_specs=[pl.BlockSpec((tq, gq), qmap),
                  pl.BlockSpec((tq, LANES), lambda bg, i: (bg * nq + i, 0)),
                  pl.BlockSpec((tq, gq), qmap),
                  pl.BlockSpec((tq, LANES), lambda bg, i: ((bg // N_KV_GROUPS) * nq + i, 0)),
                  kvspec(COL_KS), kvspec(COL_VS), kvspec(COL_KW), kvspec(COL_VW),
                  pl.BlockSpec(cbias.shape, lambda bg, i: (0, 0, 0)),
                  pl.BlockSpec(wbias.shape, lambda bg, i: (0, 0, 0))],
        out_specs=pl.BlockSpec((tq, gq), qmap),
        out_shape=jax.ShapeDtypeStruct((t, ATT_W), BF16),
        scratch_shapes=[pltpu.VMEM((seq, 2 * HEAD_DIM), BF16),
                        pltpu.VMEM((seq, 2 * HEAD_DIM), BF16),
                        pltpu.VMEM((seq, 2 * HEAD_DIM), BF16),
                        pltpu.VMEM((hq, 2 * HEAD_DIM), BF16),
                        pltpu.VMEM((2, hq, tk), F32),
                        pltpu.VMEM((hq, LANES), F32),
                        pltpu.VMEM((hq, 2 * HEAD_DIM), F32),
                        pltpu.VMEM((hq, nwin * tq), F32)],
        compiler_params=_cparams(("parallel", "arbitrary")),
        name="selwin_attn",
    )(proj, selb, ocmp, gates, proj, proj, proj, proj, jnp.asarray(cbias), jnp.asarray(wbias))


_HGRN_LEVELS = (32, 16, 8, 4, 2, 1)


def _hgrn_tables():
    c = HGRN_CHUNK
    nl = len(_HGRN_LEVELS)
    idx = np.arange(c)
    i, j = idx[:, None], idx[None, :]
    mats = [j <= i, j > i]
    lvl = np.full((c, c), nl + 1, np.int32)
    for li, s in enumerate(_HGRN_LEVELS):
        mats.append((j >= (i // s) * s) & (j <= i))
        mats.append((j > i) & (j <= (i // s) * s + s - 1))
        lvl[((i // (2 * s)) == (j // (2 * s))) & (((i // s) % 2) == 1) & (((j // s) % 2) == 0)] = li
    lvl[i == j] = nl
    w = np.concatenate(mats, 0).astype(np.float32)
    return np.concatenate([w, w], axis=1), lvl


def _hgrn_head(hq_ref, hf_ref, hi_ref, hg_ref, lbl_ref, gn_ref, w_ref, lvl_ref, o_ref, st_s,
               hh, ts):
    c = HGRN_CHUNK
    nc = ts // c
    nl = len(_HGRN_LEVELS)
    cs = slice(hh * HEAD_DIM, (hh + 1) * HEAD_DIM)
    lbl = lbl_ref[:, cs]
    e = jnp.exp(lbl - jnp.max(lbl, axis=0, keepdims=True))
    lb = e[0:1, :] / jnp.sum(e, axis=0, keepdims=True)

    f = lb + (1.0 - lb) * _sigmoid(hf_ref[:, cs].astype(F32))
    k = 1.0 - f
    qf = _silu(hq_ref[:, cs].astype(F32))
    v = hi_ref[:, cs]

    lf = jnp.log(f) * LOG2E
    lf_hi = lf.astype(BF16)
    lf_lo = (lf - lf_hi.astype(F32)).astype(BF16)
    side = lambda a: jnp.concatenate([a[ci * c:(ci + 1) * c, :] for ci in range(nc)], axis=1)
    ex = _dot(w_ref[...], jnp.concatenate([side(lf_hi), side(lf_lo)], axis=0))

    def rows(bi):
        return jnp.concatenate(
            [ex[bi * c:(bi + 1) * c, ci * HEAD_DIM:(ci + 1) * HEAD_DIM] for ci in range(nc)], axis=0)

    b = rows(0)
    qhat = (qf * jnp.exp2(b)).astype(BF16)
    khat = (k * jnp.exp2(rows(1))).astype(BF16)
    lvl = lvl_ref[...]
    masks = [lvl == li for li in range(nl + 1)]
    qs = [(qf * jnp.exp2(rows(2 + 2 * li))).astype(BF16) for li in range(nl)] + [qf.astype(BF16)]
    ks = [(k * jnp.exp2(rows(3 + 2 * li))).astype(BF16) for li in range(nl)] + [k.astype(BF16)]
    o_intra = []
    for ci in range(nc):
        sl = slice(ci * c, (ci + 1) * c)
        a = jnp.zeros((c, c), F32)
        for li in range(nl + 1):
            a = jnp.where(masks[li], _dot_nt(qs[li][sl, :], ks[li][sl, :]), a)
        o_intra.append(_dot(a.astype(BF16), v[sl, :]))

    st = st_s[hh]
    outs = []
    for ci in range(nc):
        sl = slice(ci * c, (ci + 1) * c)
        outs.append(o_intra[ci] + _dot_nt(qhat[sl, :], st.astype(BF16)))
        st = st * jnp.exp2(b[(ci + 1) * c - 1:(ci + 1) * c, :]) + lax.dot_general(
            v[sl, :], khat[sl, :], _TN, preferred_element_type=F32)
    st_s[hh] = st
    o = jnp.concatenate(outs, axis=0)
    o = o * lax.rsqrt(jnp.mean(o * o, axis=-1, keepdims=True) + EPS) * gn_ref[:, cs]
    o_ref[:, cs] = (o * _silu(hg_ref[:, cs].astype(F32))).astype(BF16)


def _hgrn_kernel(hq_ref, hf_ref, hi_ref, hg_ref, lbl_ref, gn_ref, w_ref, lvl_ref,
                 o_ref, st_s, *, ts, hb):
    @pl.when(pl.program_id(2) == 0)
    def _():
        st_s[...] = jnp.zeros_like(st_s)

    for hh in range(hb):
        _hgrn_head(hq_ref, hf_ref, hi_ref, hg_ref, lbl_ref, gn_ref, w_ref, lvl_ref, o_ref, st_s,
                   hh, ts)


def _hgrn(proj, lb_logits, g_norm, batch, seq, ts, hb):
    t = proj.shape[0]
    ns = seq // ts
    wmat, lvl = _hgrn_tables()
    w = hb * HEAD_DIM

    def colspec(col):
        return pl.BlockSpec((ts, w), lambda b, h, s: (b * ns + s, col // w + h))

    nl = lb_logits.shape[0]
    return pl.pallas_call(
        functools.partial(_hgrn_kernel, ts=ts, hb=hb),
        grid=(batch, N_HGRN_HEADS // hb, ns),
        in_specs=[colspec(COL_HQ), colspec(COL_HF), colspec(COL_HI), colspec(COL_HG),
                  pl.BlockSpec((nl, w), lambda b, h, s: (0, h)),
                  pl.BlockSpec((1, w), lambda b, h, s: (0, h)),
                  pl.BlockSpec(wmat.shape, lambda b, h, s: (0, 0)),
                  pl.BlockSpec(lvl.shape, lambda b, h, s: (0, 0))],
        out_specs=pl.BlockSpec((ts, w), lambda b, h, s: (b * ns + s, h)),
        out_shape=jax.ShapeDtypeStruct((t, ATT_W), BF16),
        scratch_shapes=[pltpu.VMEM((hb, HEAD_DIM, HEAD_DIM), F32)],
        compiler_params=_cparams(("parallel", "parallel", "arbitrary")),
        name="hgrn2",
    )(proj, proj, proj, proj, lb_logits, g_norm, jnp.asarray(wmat, BF16), jnp.asarray(lvl))


def _outproj_kernel(oa_ref, or_ref, wa_ref, wb_ref, x_ref, mod_ref, gpost_ref, gpre_ref,
                    wrh_ref, wrl_ref, br_ref, x1_ref, h2_ref, lg_ref):
    m = mod_ref[0]
    y = _dot(oa_ref[...], wa_ref[...]) + _dot(or_ref[...], wb_ref[...])
    x1 = x_ref[...] + m[2:3, :] * _rms(y, gpost_ref[...])
    x1_ref[...] = x1
    h = _rms(x1, gpre_ref[...]) * (1.0 + m[4:5, :]) + m[3:4, :]
    h2_ref[...] = h
    h_hi = h.astype(BF16)
    h_lo = (h - h_hi.astype(F32)).astype(BF16)
    lg_ref[...] = (_dot(h_hi, wrh_ref[...]) + _dot(h_lo, wrh_ref[...])
                   + _dot(h_hi, wrl_ref[...]) + br_ref[...])


def _outproj(o_att, o_rec, w_out, x2, mod3, g_post, g_pre, wr, br, seq, tm):
    t, d = x2.shape
    per_b = seq // tm
    wr_hi = wr.astype(BF16)
    wr_lo = (wr - wr_hi.astype(F32)).astype(BF16)
    wa = w_out[:ATT_W].astype(BF16)
    wb = w_out[ATT_W:].astype(BF16)
    row = lambda i: (i, 0)
    fixed = lambda i: (0, 0)
    return pl.pallas_call(
        _outproj_kernel,
        grid=(t // tm,),
        in_specs=[pl.BlockSpec((tm, ATT_W), row), pl.BlockSpec((tm, ATT_W), row),
                  pl.BlockSpec((ATT_W, d), fixed), pl.BlockSpec((ATT_W, d), fixed),
                  pl.BlockSpec((tm, d), row),
                  pl.BlockSpec((1, 6, d), lambda i: (i // per_b, 0, 0)),
                  pl.BlockSpec((1, d), fixed), pl.BlockSpec((1, d), fixed),
                  pl.BlockSpec((d, LANES), fixed), pl.BlockSpec((d, LANES), fixed),
                  pl.BlockSpec((1, LANES), fixed)],
        out_specs=[pl.BlockSpec((tm, d), row), pl.BlockSpec((tm, d), row),
                   pl.BlockSpec((tm, LANES), row)],
        out_shape=[jax.ShapeDtypeStruct((t, d), F32), jax.ShapeDtypeStruct((t, d), F32),
                   jax.ShapeDtypeStruct((t, LANES), F32)],
        compiler_params=_cparams(("parallel",)),
        name="outproj",
    )(o_att, o_rec, wa, wb, x2, mod3, g_post, g_pre, wr_hi, wr_lo, br)


G_LANE0 = N_EXPERTS


def _route_kernel(lg_ref, tri_ref, info_ref, cnt_ref, carry_s):
    @pl.when(pl.program_id(0) == 0)
    def _():
        carry_s[...] = jnp.zeros_like(carry_s)

    x = lg_ref[...]
    lane = lax.broadcasted_iota(jnp.int32, x.shape, 1)
    big = jnp.int32(10 ** 6)
    rmax = lambda a: jnp.max(a, axis=-1, keepdims=True)
    rmin = lambda a: jnp.min(a, axis=-1, keepdims=True)
    rsum = lambda a: jnp.sum(a, axis=-1, keepdims=True)

    is_g = (lane >= G_LANE0) & (lane < G_LANE0 + N_EXPERT_GROUPS)
    lgm = jnp.where(is_g, x, -jnp.inf)
    mg = rmax(lgm)
    pg_sel = 1.0 / rsum(jnp.where(is_g, jnp.exp(lgm - mg), 0.0))
    gsel = rmin(jnp.where(lgm == mg, lane, big)) - G_LANE0

    is_e = (lane >= gsel * EXPERTS_PER_GROUP) & (lane < (gsel + 1) * EXPERTS_PER_GROUP)
    lem = jnp.where(is_e, x, -jnp.inf)
    pe = jnp.where(is_e, jnp.exp(lem - rmax(lem)), 0.0)
    pe = pe / rsum(pe)
    pe = jnp.where(is_e, pe, -1.0)
    v1 = rmax(pe)
    i1 = rmin(jnp.where(pe == v1, lane, big))
    pe2 = jnp.where(lane == i1, -1.0, pe)
    v2 = rmax(pe2)
    i2 = rmin(jnp.where(pe2 == v2, lane, big))
    w1 = v1 / (v1 + v2) * pg_sel
    w2 = v2 / (v1 + v2) * pg_sel

    oh1 = jnp.where(lane == i1, 1.0, 0.0)
    oh2 = jnp.where(lane == i2, 1.0, 0.0)
    both = oh1 + oh2
    before = _dot(tri_ref[...], both.astype(BF16)) + carry_s[0:1, :]
    r1 = rsum(oh1 * before)
    r2 = rsum(oh2 * before)
    carry_s[0:1, :] = carry_s[0:1, :] + jnp.sum(both, axis=0, keepdims=True)
    cnt_ref[...] = carry_s[...]

    cols = (i1.astype(F32), i2.astype(F32), w1, w2, r1, r2)
    info = jnp.zeros(x.shape, F32)
    for ci, col in enumerate(cols):
        info = jnp.where(lane == ci, col, info)
    info_ref[...] = info


def _route(logits, tm):
    t = logits.shape[0]
    tri = jnp.asarray(np.tril(np.ones((tm, tm), np.float32), -1), BF16)
    return pl.pallas_call(
        _route_kernel,
        grid=(t // tm,),
        in_specs=[pl.BlockSpec((tm, LANES), lambda i: (i, 0)),
                  pl.BlockSpec((tm, tm), lambda i: (0, 0))],
        out_specs=[pl.BlockSpec((tm, LANES), lambda i: (i, 0)),
                   pl.BlockSpec((8, LANES), lambda i: (0, 0))],
        out_shape=[jax.ShapeDtypeStruct((t, LANES), F32),
                   jax.ShapeDtypeStruct((8, LANES), F32)],
        scratch_shapes=[pltpu.VMEM((8, LANES), F32)],
        compiler_params=_cparams(("arbitrary",)),
        name="route",
    )(logits, tri)


def _row_copy(src, i, dst, j, sem):
    return pltpu.make_async_copy(src.at[pl.ds(i, 1)], dst.at[pl.ds(j, 1)], sem)


def _expert_kernel(te_ref, nu_ref, src0_ref, srcn_ref, dst_ref, h_ref, wg_ref, wu_ref, wd_ref,
                   y_ref, xbuf, obuf, gsem, ssem, wg_s, wu_s, wd_s, *, tm, standin_row):
    t = pl.program_id(0)
    nu = nu_ref[0]
    slot = t % 2

    def gather(src_ref, s):
        for r in range(tm):
            _row_copy(h_ref, src_ref[0, 0, r], xbuf.at[s], r, gsem.at[s]).start(priority=r % 2)

    def scatter(s, row_of):
        for r in range(tm):
            _row_copy(obuf.at[s], r, y_ref, row_of(r), ssem.at[s]).start(priority=r % 2)

    def wait_rows(buf, sem, s):
        pltpu.make_async_copy(buf.at[s], buf.at[s], sem.at[s]).wait()

    @pl.when(t < nu)
    def _():
        @pl.when(t == 0)
        def _():
            gather(src0_ref, 0)
            obuf[1] = jnp.zeros(obuf.shape[1:], obuf.dtype)

            def fill(row0, s):
                return pltpu.make_async_copy(obuf.at[1], y_ref.at[pl.ds(row0, tm)], ssem.at[s])

            fill(standin_row - 2 * tm, 0).start()
            fill(standin_row - tm, 0).start()
            fill(standin_row - 2 * tm, 0).wait()
            fill(standin_row - tm, 0).wait()
            fill(standin_row, 1).start()

        @pl.when((t == 0) | (te_ref[t] != te_ref[jnp.maximum(t - 1, 0)]))
        def _():
            wg_s[...] = wg_ref[0].astype(BF16)
            wu_s[...] = wu_ref[0].astype(BF16)
            wd_s[...] = wd_ref[0].astype(BF16)

        gather(srcn_ref, 1 - slot)
        wait_rows(xbuf, gsem, slot)
        x = xbuf[slot].astype(BF16)
        hid = (_silu(_dot(x, wg_s[...])) * _dot(x, wu_s[...])).astype(BF16)
        obuf[slot] = _dot(hid, wd_s[...])
        scatter(slot, lambda r: dst_ref[0, 0, r])
        wait_rows(obuf, ssem, 1 - slot)

        @pl.when(t == nu - 1)
        def _():
            wait_rows(obuf, ssem, slot)
            wait_rows(xbuf, gsem, 1 - slot)


def _experts(tile_expert, n_used, src3, dst3, h2, w_gate, w_up, w_down, y_rows, tm,
             standin_row):
    n_tiles = src3.shape[0]
    d, f = w_gate.shape[1:]
    wmap = lambda t, te, nu: (te[t], 0, 0)
    smem = lambda imap: pl.BlockSpec((1, 1, tm), imap, memory_space=pltpu.SMEM)
    return pl.pallas_call(
        functools.partial(_expert_kernel, tm=tm, standin_row=standin_row),
        grid_spec=pltpu.PrefetchScalarGridSpec(
            num_scalar_prefetch=2,
            grid=(n_tiles,),
            in_specs=[smem(lambda t, te, nu: (0, 0, 0)),
                      smem(lambda t, te, nu: (jnp.minimum(t + 1, nu[0] - 1), 0, 0)),
                      smem(lambda t, te, nu: (t, 0, 0)),
                      pl.BlockSpec(memory_space=pl.ANY),
                      pl.BlockSpec((1, d, f), wmap),
                      pl.BlockSpec((1, d, f), wmap),
                      pl.BlockSpec((1, f, d), wmap)],
            out_specs=pl.BlockSpec(memory_space=pl.ANY),
            scratch_shapes=[pltpu.VMEM((2, tm, d), F32), pltpu.VMEM((2, tm, d), F32),
                            pltpu.SemaphoreType.DMA((2,)), pltpu.SemaphoreType.DMA((2,)),
                            pltpu.VMEM((d, f), BF16), pltpu.VMEM((d, f), BF16),
                            pltpu.VMEM((f, d), BF16)]),
        out_shape=jax.ShapeDtypeStruct((y_rows, d), F32),
        compiler_params=_cparams(("arbitrary",)),
        name="moe_experts",
    )(tile_expert, n_used, src3, src3, dst3, h2, w_gate, w_up, w_down)


def _final_kernel(info_ref, x1_ref, mod_ref, g_ref, ya_ref, yb_ref, o_ref):
    info = info_ref[...]
    y = info[:, 2:3] * ya_ref[...] + info[:, 3:4] * yb_ref[...]
    o_ref[...] = x1_ref[...] + mod_ref[0][5:6, :] * _rms(y, g_ref[...])


def _moe_final(info, x1, mod3, g_post, y2, seq, tm):
    t, d = x1.shape
    per_b = seq // tm
    nb = t // tm
    return pl.pallas_call(
        _final_kernel,
        grid=(nb,),
        in_specs=[pl.BlockSpec((tm, LANES), lambda i: (i, 0)),
                  pl.BlockSpec((tm, d), lambda i: (i, 0)),
                  pl.BlockSpec((1, 6, d), lambda i: (i // per_b, 0, 0)),
                  pl.BlockSpec((1, d), lambda i: (0, 0)),
                  pl.BlockSpec((tm, d), lambda i: (i, 0)),
                  pl.BlockSpec((tm, d), lambda i: (nb + i, 0))],
        out_specs=pl.BlockSpec((tm, d), lambda i: (i, 0)),
        out_shape=jax.ShapeDtypeStruct((t, d), F32),
        compiler_params=_cparams(("parallel",)),
        name="moe_final",
    )(info, x1, mod3, g_post, y2, y2)


def _slot_map_kernel(pos_ref, src_in, dst_in, src_ref, dst_ref, sem, *, tm, n_tokens):
    @pl.when(pl.program_id(0) == 0)
    def _():
        c_src = pltpu.make_async_copy(src_in, src_ref, sem.at[0])
        c_dst = pltpu.make_async_copy(dst_in, dst_ref, sem.at[1])
        c_src.start()
        c_dst.start()
        c_src.wait()
        c_dst.wait()

    base = pl.program_id(0) * tm

    def body(i, c):
        tok = base + i
        p0 = pos_ref[0, 0, 2 * i]
        p1 = pos_ref[0, 0, 2 * i + 1]
        src_ref[p0] = tok
        src_ref[p1] = tok
        dst_ref[p0] = tok
        dst_ref[p1] = n_tokens + tok
        return c

    lax.fori_loop(0, tm, body, 0, unroll=8)


def _slot_map(pos, src0, dst0, tm):
    t = pos.shape[0]
    whole = pl.BlockSpec(memory_space=pltpu.SMEM)
    return pl.pallas_call(
        functools.partial(_slot_map_kernel, tm=tm, n_tokens=t),
        grid=(t // tm,),
        in_specs=[pl.BlockSpec((1, 1, 2 * tm), lambda i: (i, 0, 0), memory_space=pltpu.SMEM),
                  pl.BlockSpec(memory_space=pl.ANY), pl.BlockSpec(memory_space=pl.ANY)],
        out_specs=[whole, whole],
        out_shape=[jax.ShapeDtypeStruct(src0.shape, jnp.int32)] * 2,
        scratch_shapes=[pltpu.SemaphoreType.DMA((2,))],
        compiler_params=_cparams(("arbitrary",)),
        name="slot_map",
    )(pos.reshape(t // tm, 1, 2 * tm), src0, dst0)


def _moe(h2, logits, x1, mod3, g_post, w_gate, w_up, w_down, seq, tm_route, tm_exp, tm_fin):
    t = h2.shape[0]
    info, cnt = _route(logits, tm_route)
    counts = cnt[0, :N_EXPERTS].astype(jnp.int32)
    tiles_e = (counts + tm_exp - 1) // tm_exp
    tile_end = jnp.cumsum(tiles_e)
    offs = (tile_end - tiles_e) * tm_exp
    n_used = tile_end[-1]
    e12 = info[:, 0:2].astype(jnp.int32)
    pos = offs[e12] + info[:, 4:6].astype(jnp.int32)
    n_tiles = (2 * t) // tm_exp + N_EXPERTS
    tidx = jnp.minimum(jnp.arange(n_tiles, dtype=jnp.int32), n_used - 1)
    tile_expert = jnp.sum(tidx[:, None] >= tile_end[None, :], axis=1).astype(jnp.int32)
    slot_ids = jnp.arange(n_tiles * tm_exp, dtype=jnp.int32)
    spare = 2 * t + ((slot_ids // tm_exp) % 2) * tm_exp + slot_ids % tm_exp
    src, dst = _slot_map(pos, jnp.zeros_like(slot_ids), spare, tm_route)

    ew = w_gate.shape
    wg = w_gate.reshape(N_EXPERTS, ew[-2], ew[-1])
    wu = w_up.reshape(N_EXPERTS, ew[-2], ew[-1])
    wd = w_down.reshape(N_EXPERTS, ew[-1], ew[-2])
    y2 = _experts(tile_expert, n_used.reshape(1).astype(jnp.int32),
                  src.reshape(n_tiles, 1, tm_exp), dst.reshape(n_tiles, 1, tm_exp),
                  h2, wg, wu, wd, 2 * t + 3 * tm_exp, tm_exp, 2 * t + 2 * tm_exp)
    return _moe_final(info, x1, mod3, g_post, y2, seq, tm_fin)


def _permute_w_in(w_in):
    sizes = [ATT_W] + [KV_W] * 6 + [3 * N_ATT_HEADS] + [ATT_W] * 4
    cuts = np.cumsum(sizes)[:-1]
    q, kc, vc, ks, vs, kw, vw, gates, hq, hf, hi, hg = jnp.split(w_in, cuts, axis=1)
    w_p = jnp.concatenate([q, kc, ks, kw, vc, vs, vw, hq, hf, hi, hg], axis=1).astype(BF16)
    w_g = jnp.pad(gates, ((0, 0), (0, LANES - gates.shape[1]))).astype(BF16)
    return w_p, w_g


def _half_blocks(proj, col, batch, seq):
    a = proj[:, col:col + KV_W].reshape(batch, seq, N_KV_GROUPS, HEAD_DIM)
    a = a.transpose(0, 2, 1, 3)
    return a.reshape(batch * N_KV_GROUPS, seq // CMP_STRIDE, CMP_STRIDE * HEAD_DIM)


def _block(x, c, positions, w_ada, b_ada, g_pre_mix, g_post_mix, g_pre_ffn, g_post_ffn,
           w_in, w_out, pe_k, w1_k, w2_k, pe_v, w1_v, w2_v, lb_logits, g_norm,
           w_group, b_group, w_router, b_router, w_gate, w_up, w_down, cfg):
    batch, seq, d = x.shape
    t = batch * seq
    x2 = x.reshape(t, d)

    c8 = jnp.zeros((8, d), F32).at[:batch].set(c)
    mod3 = _ada_mod(c8, w_ada, b_ada)[:batch].reshape(batch, 6, d)
    cos_t, sin_t = _rope_tables(positions, cfg["tm_rope"])
    w_p, w_g = _permute_w_in(w_in)
    proj, gates = _inproj(x2, mod3, g_pre_mix.reshape(1, d), w_p, w_g, cos_t, sin_t,
                          seq, cfg["tm_in"])

    kc = _compress(_half_blocks(proj, COL_KC, batch, seq), w1_k, pe_k, w2_k)
    vc = _compress(_half_blocks(proj, COL_VC, batch, seq), w1_v, pe_v, w2_v)
    ocmp, selb = _cmp_attn(proj, kc, vc, gates, batch, seq, cfg["tq_cmp"])
    o_att = _selwin_attn(proj, selb, ocmp, gates, batch, seq, cfg["tq"], cfg["tk"])
    o_rec = _hgrn(proj, lb_logits, g_norm.reshape(1, -1), batch, seq, cfg["ts_hgrn"],
                  cfg["hb_hgrn"])

    wr = jnp.concatenate([w_router, w_group], axis=1)
    wr = jnp.pad(wr, ((0, 0), (0, LANES - wr.shape[1])))
    br = jnp.pad(jnp.concatenate([b_router, b_group]), (0, LANES - N_EXPERTS - N_EXPERT_GROUPS))
    x1, h2, logits = _outproj(o_att, o_rec, w_out, x2, mod3, g_post_mix.reshape(1, d),
                               g_pre_ffn.reshape(1, d), wr, br.reshape(1, LANES), seq,
                               cfg["tm_out"])
    out = _moe(h2, logits, x1, mod3, g_post_ffn.reshape(1, d), w_gate, w_up, w_down, seq,
               cfg["tm_route"], cfg["tm_exp"], cfg["tm_fin"])
    return out.reshape(batch, seq, d)


def _config(seq):
    return dict(tm_rope=min(1024, seq), tm_in=min(1024, seq), tq_cmp=min(256, seq),
                tq=min(256, seq), tk=min(512, seq), ts_hgrn=min(256, seq), hb_hgrn=4,
                tm_out=min(256, seq), tm_route=min(256, seq), tm_exp=256,
                tm_fin=min(512, seq))


def kernel(x, c, positions, w_ada, b_ada, g_pre_mix, g_post_mix, g_pre_ffn, g_post_ffn, w_in, w_out, cmp_pe_k, cmp_w1_k, cmp_w2_k, cmp_pe_v, cmp_w1_v, cmp_w2_v, hgrn_lb_logits, hgrn_g_norm, w_group, b_group, w_router, b_router, w_gate, w_up, w_down):
    assert w_ada.shape[0] == 1, "single-layer block"
    return _block(x, c, positions, w_ada[0], b_ada[0], g_pre_mix[0], g_post_mix[0],
                  g_pre_ffn[0], g_post_ffn[0], w_in[0], w_out[0], cmp_pe_k[0], cmp_w1_k[0],
                  cmp_w2_k[0], cmp_pe_v[0], cmp_w1_v[0], cmp_w2_v[0], hgrn_lb_logits,
                  hgrn_g_norm[0], w_group[0], b_group[0], w_router[0], b_router[0],
                  w_gate[0], w_up[0], w_down[0], _config(x.shape[1]))
```

```python
import functools

import numpy as np
import jax
import jax.numpy as jnp
from jax import lax
from jax.experimental import pallas as pl
from jax.experimental.pallas import tpu as pltpu

F32 = jnp.float32
BF16 = jnp.bfloat16

HEAD_DIM = 128
N_KV_GROUPS = 2
HEADS_PER_GROUP = 4
N_ATT_HEADS = N_KV_GROUPS * HEADS_PER_GROUP
CMP_BLOCK = 32
CMP_STRIDE = 16
SEL_BLOCK = 64
N_SEL = 16
WINDOW = 512
FORCE_BONUS = 1.0e4
ROPE_THETA = 10000.0
N_HGRN_HEADS = 8
HGRN_CHUNK = 64
N_EXPERT_GROUPS = 4
EXPERTS_PER_GROUP = 8
N_EXPERTS = N_EXPERT_GROUPS * EXPERTS_PER_GROUP
EPS = 1e-6

LANES = 128
VMEM_LIMIT = 56 * 1024 * 1024

ATT_W = N_ATT_HEADS * HEAD_DIM
KV_W = N_KV_GROUPS * HEAD_DIM
COL_Q = 0
COL_KC = COL_Q + ATT_W
COL_KS = COL_KC + KV_W
COL_KW = COL_KS + KV_W
COL_VC = COL_KW + KV_W
COL_VS = COL_VC + KV_W
COL_VW = COL_VS + KV_W
COL_HQ = COL_VW + KV_W
COL_HF = COL_HQ + ATT_W
COL_HI = COL_HF + ATT_W
COL_HG = COL_HI + ATT_W
PROJ_W = COL_HG + ATT_W
ROPE_W = COL_VC

MASK_BIG = float(2 ** 30)
LOG2E = 1.4426950408889634
Q_PRESCALE = HEAD_DIM ** -0.5 * LOG2E

_NT = (((1,), (1,)), ((), ()))
_TN = (((0,), (0,)), ((), ()))


def _cparams(sem):
    return pltpu.CompilerParams(dimension_semantics=sem, vmem_limit_bytes=VMEM_LIMIT)


def _dot(a, b, **kw):
    return jnp.dot(a, b, preferred_element_type=F32, **kw)


def _dot_nt(a, b):
    return lax.dot_general(a, b, _NT, preferred_element_type=F32)


def _sigmoid(x):
    return 1.0 / (1.0 + jnp.exp(-x))


def _silu(x):
    return x * _sigmoid(x)


def _rms(x, g):
    return x * lax.rsqrt(jnp.mean(x * x, axis=-1, keepdims=True) + EPS) * g


def _lane_col(x, lane, col):
    return jnp.sum(jnp.where(lane == col, x, 0.0), axis=-1, keepdims=True)


def _ada_kernel(c_ref, w_ref, b_ref, o_ref):
    s = _silu(c_ref[...])
    o_ref[...] = _dot(s, w_ref[...], precision=lax.Precision.HIGHEST) + b_ref[...]


def _ada_mod(c8, w_ada, b_ada, tn=512):
    rows, d = c8.shape
    n = w_ada.shape[1]
    return pl.pallas_call(
        _ada_kernel,
        grid=(n // tn,),
        in_specs=[pl.BlockSpec((rows, d), lambda j: (0, 0)),
                  pl.BlockSpec((d, tn), lambda j: (0, j)),
                  pl.BlockSpec((1, tn), lambda j: (0, j))],
        out_specs=pl.BlockSpec((rows, tn), lambda j: (0, j)),
        out_shape=jax.ShapeDtypeStruct((rows, n), F32),
        compiler_params=_cparams(("parallel",)),
        name="ada_mod",
    )(c8, w_ada, b_ada.reshape(1, n))


def _rope_kernel(ang_ref, cos_ref, sin_ref):
    a = ang_ref[...]
    lane = lax.broadcasted_iota(jnp.int32, a.shape, 1)
    cos_ref[...] = jnp.cos(a)
    sin_ref[...] = jnp.where(lane < HEAD_DIM // 2, -jnp.sin(a), jnp.sin(a))


def _rope_tables(positions, tm):
    t = positions.size
    inv_freq = ROPE_THETA ** (-jnp.arange(0, HEAD_DIM, 2, dtype=F32) / HEAD_DIM)
    ang = positions.reshape(t, 1).astype(F32) * jnp.concatenate([inv_freq, inv_freq])[None, :]
    spec = pl.BlockSpec((tm, HEAD_DIM), lambda i: (i, 0))
    return pl.pallas_call(
        _rope_kernel,
        grid=(t // tm,),
        in_specs=[spec],
        out_specs=[spec, spec],
        out_shape=[jax.ShapeDtypeStruct((t, HEAD_DIM), F32)] * 2,
        compiler_params=_cparams(("parallel",)),
        name="rope_tables",
    )(ang)


def _inproj_kernel(x_ref, mod_ref, g_ref, w_ref, wg_ref, cos_ref, sin_ref,
                   proj_ref, gates_ref, h_s, *, tn):
    j = pl.program_id(1)

    @pl.when(j == 0)
    def _():
        m = mod_ref[0]
        h = _rms(x_ref[...], g_ref[...]) * (1.0 + m[1:2, :]) + m[0:1, :]
        hb = h.astype(BF16)
        h_s[...] = hb
        gates_ref[...] = _dot(hb, wg_ref[...])

    heads = tn // HEAD_DIM
    n_rope_tiles = ROPE_W // tn
    rem_heads = (ROPE_W % tn) // HEAD_DIM

    def store(n_rot, scale=None):
        acc = _dot(h_s[...], w_ref[...])
        for c in range(heads):
            a = acc[:, c * HEAD_DIM:(c + 1) * HEAD_DIM]
            if c < n_rot:
                a = a * cos_ref[...] + pltpu.roll(a, HEAD_DIM // 2, axis=1) * sin_ref[...]
            if scale is not None:
                a = a * scale
            proj_ref[:, c * HEAD_DIM:(c + 1) * HEAD_DIM] = a.astype(BF16)

    @pl.when(j < ATT_W // tn)
    def _():
        store(heads, Q_PRESCALE)

    @pl.when((j >= ATT_W // tn) & (j < n_rope_tiles))
    def _():
        store(heads)

    if rem_heads:
        @pl.when(j == n_rope_tiles)
        def _():
            store(rem_heads)

    @pl.when(j >= n_rope_tiles + (1 if rem_heads else 0))
    def _():
        store(0)


def _inproj(x2, mod3, g_pre, w_in_p, w_gates, cos_t, sin_t, seq, tm, tn=512):
    t, d = x2.shape
    per_b = seq // tm
    return pl.pallas_call(
        functools.partial(_inproj_kernel, tn=tn),
        grid=(t // tm, PROJ_W // tn),
        in_specs=[pl.BlockSpec((tm, d), lambda i, j: (i, 0)),
                  pl.BlockSpec((1, 6, d), lambda i, j: (i // per_b, 0, 0)),
                  pl.BlockSpec((1, d), lambda i, j: (0, 0)),
                  pl.BlockSpec((d, tn), lambda i, j: (0, j)),
                  pl.BlockSpec((d, LANES), lambda i, j: (0, 0)),
                  pl.BlockSpec((tm, HEAD_DIM), lambda i, j: (i, 0)),
                  pl.BlockSpec((tm, HEAD_DIM), lambda i, j: (i, 0))],
        out_specs=[pl.BlockSpec((tm, tn), lambda i, j: (i, j)),
                   pl.BlockSpec((tm, LANES), lambda i, j: (i, 0))],
        out_shape=[jax.ShapeDtypeStruct((t, PROJ_W), BF16),
                   jax.ShapeDtypeStruct((t, LANES), F32)],
        scratch_shapes=[pltpu.VMEM((tm, d), BF16)],
        compiler_params=_cparams(("parallel", "arbitrary")),
        name="inproj",
    )(x2, mod3, g_pre, w_in_p, w_gates, cos_t, sin_t)


def _compress_kernel(h_ref, w1_ref, pe_ref, w2_ref, o_ref):
    hb = h_ref[0]
    half = hb.shape[1]
    ya = _dot(hb, w1_ref[0:half, :])
    yb = _dot(hb, w1_ref[half:2 * half, :])
    const = _dot(pe_ref[...], w1_ref[...])
    n = ya.shape[0]
    yb_next = pltpu.roll(yb, n - 1, axis=0)
    hid = _silu(ya + yb_next + const[0:1, :])
    out = _dot(hid.astype(BF16), w2_ref[...])
    row = lax.broadcasted_iota(jnp.int32, out.shape, 0)
    o_ref[0] = jnp.where(row < n - 1, out, 0.0).astype(BF16)


def _compress(hblk, w1, pe, w2):
    bg, n, half = hblk.shape
    pe8 = jnp.zeros((8, 2 * half), BF16).at[0].set(pe.reshape(-1).astype(BF16))
    return pl.pallas_call(
        _compress_kernel,
        grid=(bg,),
        in_specs=[pl.BlockSpec((1, n, half), lambda i: (i, 0, 0)),
                  pl.BlockSpec((2 * half, HEAD_DIM), lambda i: (0, 0)),
                  pl.BlockSpec((8, 2 * half), lambda i: (0, 0)),
                  pl.BlockSpec((HEAD_DIM, HEAD_DIM), lambda i: (0, 0))],
        out_specs=pl.BlockSpec((1, n, HEAD_DIM), lambda i: (i, 0, 0)),
        out_shape=jax.ShapeDtypeStruct((bg, n, HEAD_DIM), BF16),
        compiler_params=_cparams(("parallel",)),
        name="compress",
    )(hblk, w1.astype(BF16), pe8, w2.astype(BF16))


def _cmp_kernel(q_ref, kc_ref, vc_ref, gates_ref, ov_ref, ocmp_ref, selb_ref, *, tq, nsel):
    g = pl.program_id(0) % N_KV_GROUPS
    q0 = pl.program_id(1) * tq
    kc = kc_ref[0]
    vc = vc_ref[0]
    nc = kc.shape[0]
    t_idx = q0 + lax.broadcasted_iota(jnp.int32, (tq, nc), 0)
    c_idx = lax.broadcasted_iota(jnp.int32, (tq, nc), 1)
    c_ok = c_idx * CMP_STRIDE + (CMP_BLOCK - 1) <= t_idx
    gl = gates_ref[...]
    lane = lax.broadcasted_iota(jnp.int32, (tq, LANES), 1)
    psum = jnp.zeros((tq, nc), F32)
    for h in range(HEADS_PER_GROUP):
        qh = q_ref[:, h * HEAD_DIM:(h + 1) * HEAD_DIM]
        s = jnp.where(c_ok, _dot_nt(qh, kc), -jnp.inf)
        m = jnp.max(s, axis=-1, keepdims=True)
        m = jnp.where(m == -jnp.inf, 0.0, m)
        p = jnp.exp2(s - m)
        p = p / jnp.maximum(jnp.sum(p, axis=-1, keepdims=True), 1e-30)
        psum = psum + p
        o = _dot(p.astype(BF16), vc)
        glog = _lane_col(gl, lane, (g * HEADS_PER_GROUP + h) * 3)
        ocmp_ref[:, h * HEAD_DIM:(h + 1) * HEAD_DIM] = o * _sigmoid(glog)

    imp = _dot(psum, ov_ref[...], precision=lax.Precision.HIGHEST)
    cur = (q0 + lax.broadcasted_iota(jnp.int32, (tq, LANES), 0)) // SEL_BLOCK
    forced = (lane == 0) | (lane == cur) | (lane == cur - 1)
    imp = jnp.where(lane <= cur, imp + jnp.where(forced, FORCE_BONUS, 0.0), -jnp.inf)
    imp_t = imp.T[0:nsel, :]
    row = lax.broadcasted_iota(jnp.int32, (nsel, tq), 0)
    cnt = jnp.zeros((nsel, tq), F32)
    for k in range(nsel):
        vk = imp_t[k:k + 1, :]
        cnt = cnt + jnp.where(vk > imp_t, 1.0, 0.0)
        cnt = cnt + jnp.where((vk == imp_t) & (row > k), 1.0, 0.0)
    sel = (cnt < float(N_SEL)) & (imp_t > -jnp.inf)
    bias_t = jnp.where(sel, 0.0, -MASK_BIG)
    if nsel < LANES:
        bias_t = jnp.concatenate([bias_t, jnp.zeros((LANES - nsel, tq), F32)], axis=0)
    selb_ref[...] = bias_t.T.astype(BF16)


def _cmp_attn(proj, kc, vc, gates, batch, seq, tq):
    t = proj.shape[0]
    nq = seq // tq
    nc = kc.shape[1]
    nsel = seq // SEL_BLOCK
    assert nsel <= LANES and nsel % 8 == 0
    c_start = np.arange(nc) * CMP_STRIDE
    j_start = np.arange(LANES) * SEL_BLOCK
    ov = ((c_start[:, None] < j_start[None, :] + SEL_BLOCK)
          & (c_start[:, None] + CMP_BLOCK > j_start[None, :])
          & (np.arange(LANES)[None, :] < nsel)).astype(np.float32)
    gq = HEADS_PER_GROUP * HEAD_DIM
    qmap = lambda bg, i: ((bg // N_KV_GROUPS) * nq + i, bg % N_KV_GROUPS)
    return pl.pallas_call(
        functools.partial(_cmp_kernel, tq=tq, nsel=nsel),
        grid=(batch * N_KV_GROUPS, nq),
        in_specs=[pl.BlockSpec((tq, gq), qmap),
                  pl.BlockSpec((1, nc, HEAD_DIM), lambda bg, i: (bg, 0, 0)),
                  pl.BlockSpec((1, nc, HEAD_DIM), lambda bg, i: (bg, 0, 0)),
                  pl.BlockSpec((tq, LANES), lambda bg, i: ((bg // N_KV_GROUPS) * nq + i, 0)),
                  pl.BlockSpec((nc, LANES), lambda bg, i: (0, 0))],
        out_specs=[pl.BlockSpec((tq, gq), qmap),
                   pl.BlockSpec((tq, LANES), lambda bg, i: (bg * nq + i, 0))],
        out_shape=[jax.ShapeDtypeStruct((t, ATT_W), F32),
                   jax.ShapeDtypeStruct((batch * N_KV_GROUPS * seq, LANES), BF16)],
        compiler_params=_cparams(("parallel", "parallel")),
        name="cmp_attn",
    )(proj, kc, vc, gates, jnp.asarray(ov))


def _selwin_kernel(q_ref, selb_ref, ocmp_ref, gates_ref, ks_ref, vs_ref, kw_ref, vw_ref,
                   cb_ref, wb_ref, o_ref, kaug_s, vaug_s, vwaug_s, qaug_s, sbuf_s, m_s, acc_s,
                   swin_s, *, tq, tk, seq, nwin):
    g = pl.program_id(0) % N_KV_GROUPS
    qi = pl.program_id(1)
    q0 = qi * tq
    hq = HEADS_PER_GROUP * tq
    half = hq // 2

    @pl.when(qi == 0)
    def _():
        r = lax.broadcasted_iota(jnp.int32, (seq, HEAD_DIM), 0)
        ln = lax.broadcasted_iota(jnp.int32, (seq, HEAD_DIM), 1)
        ones_col = jnp.where(ln == 0, 1.0, 0.0).astype(BF16)
        kaug_s[:, 0:HEAD_DIM] = ks_ref[...]
        kaug_s[:, HEAD_DIM:2 * HEAD_DIM] = jnp.where(r // SEL_BLOCK == ln, 1.0, 0.0).astype(BF16)
        vaug_s[:, 0:HEAD_DIM] = vs_ref[...]
        vaug_s[:, HEAD_DIM:2 * HEAD_DIM] = ones_col
        vwaug_s[:, 0:HEAD_DIM] = vw_ref[...]
        vwaug_s[:, HEAD_DIM:2 * HEAD_DIM] = ones_col

    for h in range(HEADS_PER_GROUP):
        qaug_s[h * tq:(h + 1) * tq, 0:HEAD_DIM] = q_ref[:, h * HEAD_DIM:(h + 1) * HEAD_DIM]
        qaug_s[h * tq:(h + 1) * tq, HEAD_DIM:2 * HEAD_DIM] = selb_ref[...]
    qa = qaug_s[...]

    def scores(kt, slot):
        ks = kaug_s[pl.ds(pl.multiple_of(kt * tk, tk), tk), :]
        sbuf_s[slot] = _dot_nt(qa, ks)

    def absorb(kt, slot, diagonal=False):
        vs = vaug_s[pl.ds(pl.multiple_of(kt * tk, tk), tk), :]
        s = sbuf_s[slot]
        if diagonal:
            bias = cb_ref[(q0 - kt * tk) // tq]
            s = (s.reshape(HEADS_PER_GROUP, tq, tk) + bias[None]).reshape(hq, tk)
        m = m_s[...]
        m_new = jnp.maximum(m, jnp.max(s, axis=-1, keepdims=True))
        m_s[...] = m_new
        pb = jnp.exp2(s - m_new[:, 0:1]).astype(BF16)
        if diagonal:
            pv = jnp.concatenate([_dot(pb[0:half], vs), _dot(pb[half:hq], vs)], axis=0)
        else:
            pv = _dot(pb, vs)
        acc_s[...] = jnp.exp2(m - m_new)[:, 0:1] * acc_s[...] + pv

    n_full = q0 // tk
    m_s[...] = jnp.full(m_s.shape, -jnp.inf, F32)
    acc_s[...] = jnp.zeros(acc_s.shape, F32)
    scores(0, 0)

    def pair(j, c):
        scores(2 * j + 1, 1)
        absorb(2 * j, 0)
        scores(2 * j + 2, 0)
        absorb(2 * j + 1, 1)
        return c

    lax.fori_loop(0, n_full // 2, pair, 0)
    odd = n_full % 2

    @pl.when(odd == 1)
    def _():
        scores(n_full, 1)
        absorb(n_full - 1, 0)

    nq = seq // tq
    w0 = jnp.clip(qi - (nwin - 1), 0, nq - nwin) * tq
    wlen = nwin * tq
    kwin = kw_ref[pl.ds(pl.multiple_of(w0, tq), wlen), :]
    vwin = vwaug_s[pl.ds(pl.multiple_of(w0, tq), wlen), :]
    swin_s[0:half, :] = _dot_nt(qaug_s[0:half, 0:HEAD_DIM], kwin)
    swin_s[half:hq, :] = _dot_nt(qaug_s[half:hq, 0:HEAD_DIM], kwin)
    absorb(n_full, odd, diagonal=True)
    acc = acc_s[...]
    o_sel = acc[:, 0:HEAD_DIM] / acc[:, HEAD_DIM:HEAD_DIM + 1]

    wbias = wb_ref[(q0 - w0) // tq]
    sw = (swin_s[...].reshape(HEADS_PER_GROUP, tq, wlen) + wbias[None]).reshape(hq, wlen)
    pwb = jnp.exp2(sw - jnp.max(sw, axis=-1, keepdims=True)).astype(BF16)
    ow = jnp.concatenate([_dot(pwb[0:half], vwin), _dot(pwb[half:hq], vwin)], axis=0)
    o_win = ow[:, 0:HEAD_DIM] / ow[:, HEAD_DIM:HEAD_DIM + 1]

    gl = gates_ref[...]
    lane = lax.broadcasted_iota(jnp.int32, (tq, LANES), 1)
    for h in range(HEADS_PER_GROUP):
        gcol = (g * HEADS_PER_GROUP + h) * 3
        g_s = _sigmoid(_lane_col(gl, lane, gcol + 1))
        g_w = _sigmoid(_lane_col(gl, lane, gcol + 2))
        o = (ocmp_ref[:, h * HEAD_DIM:(h + 1) * HEAD_DIM]
             + g_s * o_sel[h * tq:(h + 1) * tq, :] + g_w * o_win[h * tq:(h + 1) * tq, :])
        o_ref[:, h * HEAD_DIM:(h + 1) * HEAD_DIM] = o.astype(BF16)


def _selwin_attn(proj, selb, ocmp, gates, batch, seq, tq, tk):
    t = proj.shape[0]
    nq = seq // tq
    assert tk % tq == 0 and tq & (tq - 1) == 0 and seq % tk == 0
    nwin = min(WINDOW // tq + 1, nq)
    assert WINDOW % tq == 0
    gq = HEADS_PER_GROUP * HEAD_DIM
    hq = HEADS_PER_GROUP * tq
    r = np.arange(tq)[:, None]
    cbias = np.stack([np.where(np.arange(tk)[None, :] <= r + off * tq, 0.0, -np.inf)
                      for off in range(tk // tq)]).astype(np.float32)
    dist = [r + off * tq - np.arange(nwin * tq)[None, :] for off in range(nwin)]
    wbias = np.stack([np.where((d >= 0) & (d < WINDOW), 0.0, -np.inf)
                      for d in dist]).astype(np.float32)
    qmap = lambda bg, i: ((bg // N_KV_GROUPS) * nq + i, bg % N_KV_GROUPS)

    def kvspec(col):
        return pl.BlockSpec((seq, HEAD_DIM),
                            lambda bg, i: (bg // N_KV_GROUPS, col // HEAD_DIM + bg % N_KV_GROUPS))

    return pl.pallas_call(
        functools.partial(_selwin_kernel, tq=tq, tk=tk, seq=seq, nwin=nwin),
        grid=(batch * N_KV_GROUPS, nq),
        in_specs=[pl.BlockSpec((tq, gq), qmap),
                  pl.BlockSpec((tq, LANES), lambda bg, i: (bg * nq + i, 0)),
                  pl.BlockSpec((tq, gq), qmap),
                  pl.BlockSpec((tq, LANES), lambda bg, i: ((bg // N_KV_GROUPS) * nq + i, 0)),
                  kvspec(COL_KS), kvspec(COL_VS), kvspec(COL_KW), kvspec(COL_VW),
                  pl.BlockSpec(cbias.shape, lambda bg, i: (0, 0, 0)),
                  pl.BlockSpec(wbias.shape, lambda bg, i: (0, 0, 0))],
        out_specs=pl.BlockSpec((tq, gq), qmap),
        out_shape=jax.ShapeDtypeStruct((t, ATT_W), BF16),
        scratch_shapes=[pltpu.VMEM((seq, 2 * HEAD_DIM), BF16),
                        pltpu.VMEM((seq, 2 * HEAD_DIM), BF16),
                        pltpu.VMEM((seq, 2 * HEAD_DIM), BF16),
                        pltpu.VMEM((hq, 2 * HEAD_DIM), BF16),
                        pltpu.VMEM((2, hq, tk), F32),
                        pltpu.VMEM((hq, LANES), F32),
                        pltpu.VMEM((hq, 2 * HEAD_DIM), F32),
                        pltpu.VMEM((hq, nwin * tq), F32)],
        compiler_params=_cparams(("parallel", "arbitrary")),
        name="selwin_attn",
    )(proj, selb, ocmp, gates, proj, proj, proj, proj, jnp.asarray(cbias), jnp.asarray(wbias))


_HGRN_LEVELS = (32, 16, 8, 4, 2, 1)


def _hgrn_tables():
    c = HGRN_CHUNK
    nl = len(_HGRN_LEVELS)
    idx = np.arange(c)
    i, j = idx[:, None], idx[None, :]
    mats = [j <= i, j > i]
    lvl = np.full((c, c), nl + 1, np.int32)
    for li, s in enumerate(_HGRN_LEVELS):
        mats.append((j >= (i // s) * s) & (j <= i))
        mats.append((j > i) & (j <= (i // s) * s + s - 1))
        lvl[((i // (2 * s)) == (j // (2 * s))) & (((i // s) % 2) == 1) & (((j // s) % 2) == 0)] = li
    lvl[i == j] = nl
    w = np.concatenate(mats, 0).astype(np.float32)
    return np.concatenate([w, w], axis=1), lvl


def _hgrn_head(hq_ref, hf_ref, hi_ref, hg_ref, lbl_ref, gn_ref, w_ref, lvl_ref, o_ref, st_s,
               hh, ts):
    c = HGRN_CHUNK
    nc = ts // c
    nl = len(_HGRN_LEVELS)
    cs = slice(hh * HEAD_DIM, (hh + 1) * HEAD_DIM)
    lbl = lbl_ref[:, cs]
    e = jnp.exp(lbl - jnp.max(lbl, axis=0, keepdims=True))
    lb = e[0:1, :] / jnp.sum(e, axis=0, keepdims=True)

    f = lb + (1.0 - lb) * _sigmoid(hf_ref[:, cs].astype(F32))
    k = 1.0 - f
    qf = _silu(hq_ref[:, cs].astype(F32))
    v = hi_ref[:, cs]

    lf = jnp.log(f) * LOG2E
    lf_hi = lf.astype(BF16)
    lf_lo = (lf - lf_hi.astype(F32)).astype(BF16)
    side = lambda a: jnp.concatenate([a[ci * c:(ci + 1) * c, :] for ci in range(nc)], axis=1)
    ex = _dot(w_ref[...], jnp.concatenate([side(lf_hi), side(lf_lo)], axis=0))

    def rows(bi):
        return jnp.concatenate(
            [ex[bi * c:(bi + 1) * c, ci * HEAD_DIM:(ci + 1) * HEAD_DIM] for ci in range(nc)], axis=0)

    b = rows(0)
    qhat = (qf * jnp.exp2(b)).astype(BF16)
    khat = (k * jnp.exp2(rows(1))).astype(BF16)
    lvl = lvl_ref[...]
    masks = [lvl == li for li in range(nl + 1)]
    qs = [(qf * jnp.exp2(rows(2 + 2 * li))).astype(BF16) for li in range(nl)] + [qf.astype(BF16)]
    ks = [(k * jnp.exp2(rows(3 + 2 * li))).astype(BF16) for li in range(nl)] + [k.astype(BF16)]
    o_intra = []
    for ci in range(nc):
        sl = slice(ci * c, (ci + 1) * c)
        a = jnp.zeros((c, c), F32)
        for li in range(nl + 1):
            a = jnp.where(masks[li], _dot_nt(qs[li][sl, :], ks[li][sl, :]), a)
        o_intra.append(_dot(a.astype(BF16), v[sl, :]))

    st = st_s[hh]
    outs = []
    for ci in range(nc):
        sl = slice(ci * c, (ci + 1) * c)
        outs.append(o_intra[ci] + _dot_nt(qhat[sl, :], st.astype(BF16)))
        st = st * jnp.exp2(b[(ci + 1) * c - 1:(ci + 1) * c, :]) + lax.dot_general(
            v[sl, :], khat[sl, :], _TN, preferred_element_type=F32)
    st_s[hh] = st
    o = jnp.concatenate(outs, axis=0)
    o = o * lax.rsqrt(jnp.mean(o * o, axis=-1, keepdims=True) + EPS) * gn_ref[:, cs]
    o_ref[:, cs] = (o * _silu(hg_ref[:, cs].astype(F32))).astype(BF16)


def _hgrn_kernel(hq_ref, hf_ref, hi_ref, hg_ref, lbl_ref, gn_ref, w_ref, lvl_ref,
                 o_ref, st_s, *, ts, hb):
    @pl.when(pl.program_id(2) == 0)
    def _():
        st_s[...] = jnp.zeros_like(st_s)

    for hh in range(hb):
        _hgrn_head(hq_ref, hf_ref, hi_ref, hg_ref, lbl_ref, gn_ref, w_ref, lvl_ref, o_ref, st_s,
                   hh, ts)


def _hgrn(proj, lb_logits, g_norm, batch, seq, ts, hb):
    t = proj.shape[0]
    ns = seq // ts
    wmat, lvl = _hgrn_tables()
    w = hb * HEAD_DIM

    def colspec(col):
        return pl.BlockSpec((ts, w), lambda b, h, s: (b * ns + s, col // w + h))

    nl = lb_logits.shape[0]
    return pl.pallas_call(
        functools.partial(_hgrn_kernel, ts=ts, hb=hb),
        grid=(batch, N_HGRN_HEADS // hb, ns),
        in_specs=[colspec(COL_HQ), colspec(COL_HF), colspec(COL_HI), colspec(COL_HG),
                  pl.BlockSpec((nl, w), lambda b, h, s: (0, h)),
                  pl.BlockSpec((1, w), lambda b, h, s: (0, h)),
                  pl.BlockSpec(wmat.shape, lambda b, h, s: (0, 0)),
                  pl.BlockSpec(lvl.shape, lambda b, h, s: (0, 0))],
        out_specs=pl.BlockSpec((ts, w), lambda b, h, s: (b * ns + s, h)),
        out_shape=jax.ShapeDtypeStruct((t, ATT_W), BF16),
        scratch_shapes=[pltpu.VMEM((hb, HEAD_DIM, HEAD_DIM), F32)],
        compiler_params=_cparams(("parallel", "parallel", "arbitrary")),
        name="hgrn2",
    )(proj, proj, proj, proj, lb_logits, g_norm, jnp.asarray(wmat, BF16), jnp.asarray(lvl))


def _outproj_kernel(oa_ref, or_ref, wa_ref, wb_ref, x_ref, mod_ref, gpost_ref, gpre_ref,
                    wrh_ref, wrl_ref, br_ref, x1_ref, h2_ref, lg_ref):
    m = mod_ref[0]
    y = _dot(oa_ref[...], wa_ref[...]) + _dot(or_ref[...], wb_ref[...])
    x1 = x_ref[...] + m[2:3, :] * _rms(y, gpost_ref[...])
    x1_ref[...] = x1
    h = _rms(x1, gpre_ref[...]) * (1.0 + m[4:5, :]) + m[3:4, :]
    _to_slabs(h2_ref, 0, h)
    h_hi = h.astype(BF16)
    h_lo = (h - h_hi.astype(F32)).astype(BF16)
    lg_ref[...] = (_dot(h_hi, wrh_ref[...]) + _dot(h_lo, wrh_ref[...])
                   + _dot(h_hi, wrl_ref[...]) + br_ref[...])


def _outproj(o_att, o_rec, w_out, x2, mod3, g_post, g_pre, wr, br, seq, tm):
    t, d = x2.shape
    per_b = seq // tm
    wr_hi = wr.astype(BF16)
    wr_lo = (wr - wr_hi.astype(F32)).astype(BF16)
    wa = w_out[:ATT_W].astype(BF16)
    wb = w_out[ATT_W:].astype(BF16)
    row = lambda i: (i, 0)
    fixed = lambda i: (0, 0)
    return pl.pallas_call(
        _outproj_kernel,
        grid=(t // tm,),
        in_specs=[pl.BlockSpec((tm, ATT_W), row), pl.BlockSpec((tm, ATT_W), row),
                  pl.BlockSpec((ATT_W, d), fixed), pl.BlockSpec((ATT_W, d), fixed),
                  pl.BlockSpec((tm, d), row),
                  pl.BlockSpec((1, 6, d), lambda i: (i // per_b, 0, 0)),
                  pl.BlockSpec((1, d), fixed), pl.BlockSpec((1, d), fixed),
                  pl.BlockSpec((d, LANES), fixed), pl.BlockSpec((d, LANES), fixed),
                  pl.BlockSpec((1, LANES), fixed)],
        out_specs=[pl.BlockSpec((tm, d), row), pl.BlockSpec((tm * (d // LANES), LANES), row),
                   pl.BlockSpec((tm, LANES), row)],
        out_shape=[jax.ShapeDtypeStruct((t, d), F32),
                   jax.ShapeDtypeStruct((t * (d // LANES), LANES), F32),
                   jax.ShapeDtypeStruct((t, LANES), F32)],
        compiler_params=_cparams(("parallel",)),
        name="outproj",
    )(o_att, o_rec, wa, wb, x2, mod3, g_post, g_pre, wr_hi, wr_lo, br)


G_LANE0 = N_EXPERTS


def _route_kernel(lg_ref, tri_ref, info_ref, cnt_ref, carry_s):
    @pl.when(pl.program_id(0) == 0)
    def _():
        carry_s[...] = jnp.zeros_like(carry_s)

    x = lg_ref[...]
    lane = lax.broadcasted_iota(jnp.int32, x.shape, 1)
    big = jnp.int32(10 ** 6)
    rmax = lambda a: jnp.max(a, axis=-1, keepdims=True)
    rmin = lambda a: jnp.min(a, axis=-1, keepdims=True)
    rsum = lambda a: jnp.sum(a, axis=-1, keepdims=True)

    is_g = (lane >= G_LANE0) & (lane < G_LANE0 + N_EXPERT_GROUPS)
    lgm = jnp.where(is_g, x, -jnp.inf)
    mg = rmax(lgm)
    pg_sel = 1.0 / rsum(jnp.where(is_g, jnp.exp(lgm - mg), 0.0))
    gsel = rmin(jnp.where(lgm == mg, lane, big)) - G_LANE0

    is_e = (lane >= gsel * EXPERTS_PER_GROUP) & (lane < (gsel + 1) * EXPERTS_PER_GROUP)
    lem = jnp.where(is_e, x, -jnp.inf)
    pe = jnp.where(is_e, jnp.exp(lem - rmax(lem)), 0.0)
    pe = pe / rsum(pe)
    pe = jnp.where(is_e, pe, -1.0)
    v1 = rmax(pe)
    i1 = rmin(jnp.where(pe == v1, lane, big))
    pe2 = jnp.where(lane == i1, -1.0, pe)
    v2 = rmax(pe2)
    i2 = rmin(jnp.where(pe2 == v2, lane, big))
    w1 = v1 / (v1 + v2) * pg_sel
    w2 = v2 / (v1 + v2) * pg_sel

    oh1 = jnp.where(lane == i1, 1.0, 0.0)
    oh2 = jnp.where(lane == i2, 1.0, 0.0)
    both = oh1 + oh2
    before = _dot(tri_ref[...], both.astype(BF16)) + carry_s[0:1, :]
    r1 = rsum(oh1 * before)
    r2 = rsum(oh2 * before)
    carry_s[0:1, :] = carry_s[0:1, :] + jnp.sum(both, axis=0, keepdims=True)
    cnt_ref[...] = carry_s[...]

    cols = (i1.astype(F32), i2.astype(F32), w1, w2, r1, r2)
    info = jnp.zeros(x.shape, F32)
    for ci, col in enumerate(cols):
        info = jnp.where(lane == ci, col, info)
    info_ref[...] = info


def _route(logits, tm):
    t = logits.shape[0]
    tri = jnp.asarray(np.tril(np.ones((tm, tm), np.float32), -1), BF16)
    return pl.pallas_call(
        _route_kernel,
        grid=(t // tm,),
        in_specs=[pl.BlockSpec((tm, LANES), lambda i: (i, 0)),
                  pl.BlockSpec((tm, tm), lambda i: (0, 0))],
        out_specs=[pl.BlockSpec((tm, LANES), lambda i: (i, 0)),
                   pl.BlockSpec((8, LANES), lambda i: (0, 0))],
        out_shape=[jax.ShapeDtypeStruct((t, LANES), F32),
                   jax.ShapeDtypeStruct((8, LANES), F32)],
        scratch_shapes=[pltpu.VMEM((8, LANES), F32)],
        compiler_params=_cparams(("arbitrary",)),
        name="route",
    )(logits, tri)


def _to_slabs(ref, row0, x):
    n, d = x.shape
    slab = d // LANES
    for c in range(slab):
        ref[pl.ds(row0 + c, n, stride=slab), :] = x[:, c * LANES:(c + 1) * LANES]


def _from_slabs(ref, row0, n, slab):
    return jnp.concatenate([ref[pl.ds(row0 + c, n, stride=slab), :] for c in range(slab)], axis=1)


def _expert_kernel(te_ref, nu_ref, src0_ref, srcn_ref, dst_ref, h_ref, wg_ref, wu_ref, wd_ref,
                   y_ref, xbuf, obuf, gsem, ssem, wg_s, wu_s, wd_s, *, tm, standin_row):
    t = pl.program_id(0)
    nu = nu_ref[0]
    slot = t % 2
    slab = wg_s.shape[0] // LANES
    tile = tm * slab

    def token(ref, i):
        return ref.at[pl.ds(pl.multiple_of(i * slab, slab), slab)]

    def gather(src_ref, s):
        for r in range(tm):
            pltpu.make_async_copy(token(h_ref, src_ref[0, 0, r]), token(xbuf, s * tm + r),
                                  gsem.at[s]).start(priority=r % 2)

    def scatter(s, row_of):
        for r in range(tm):
            pltpu.make_async_copy(token(obuf, s * tm + r), token(y_ref, row_of(r)),
                                  ssem.at[s]).start(priority=r % 2)

    def tile_of(buf, s):
        return buf.at[pl.ds(pl.multiple_of(s * tile, tile), tile)]

    def wait_tile(buf, sem, s):
        pltpu.make_async_copy(tile_of(buf, s), tile_of(buf, s), sem.at[s]).wait()

    @pl.when(t < nu)
    def _():
        @pl.when(t == 0)
        def _():
            gather(src0_ref, 0)
            obuf[tile:2 * tile, :] = jnp.zeros((tile, LANES), obuf.dtype)

            def fill(row0, s):
                return pltpu.make_async_copy(tile_of(obuf, 1),
                                             y_ref.at[pl.ds(row0 * slab, tile)], ssem.at[s])

            fill(standin_row - 2 * tm, 0).start()
            fill(standin_row - tm, 0).start()
            fill(standin_row - 2 * tm, 0).wait()
            fill(standin_row - tm, 0).wait()
            fill(standin_row, 1).start()

        @pl.when((t == 0) | (te_ref[t] != te_ref[jnp.maximum(t - 1, 0)]))
        def _():
            wg_s[...] = wg_ref[0].astype(BF16)
            wu_s[...] = wu_ref[0].astype(BF16)
            wd_s[...] = wd_ref[0].astype(BF16)

        gather(srcn_ref, 1 - slot)
        wait_tile(xbuf, gsem, slot)
        x = _from_slabs(xbuf, slot * tile, tm, slab).astype(BF16)
        hid = (_silu(_dot(x, wg_s[...])) * _dot(x, wu_s[...])).astype(BF16)
        _to_slabs(obuf, slot * tile, _dot(hid, wd_s[...]))
        scatter(slot, lambda r: dst_ref[0, 0, r])
        wait_tile(obuf, ssem, 1 - slot)

        @pl.when(t == nu - 1)
        def _():
            wait_tile(obuf, ssem, slot)
            wait_tile(xbuf, gsem, 1 - slot)


def _experts(tile_expert, n_used, src3, dst3, h2, w_gate, w_up, w_down, y_rows, tm,
             standin_row):
    n_tiles = src3.shape[0]
    d, f = w_gate.shape[1:]
    slab = d // LANES
    wmap = lambda t, te, nu: (te[t], 0, 0)
    smem = lambda imap: pl.BlockSpec((1, 1, tm), imap, memory_space=pltpu.SMEM)
    return pl.pallas_call(
        functools.partial(_expert_kernel, tm=tm, standin_row=standin_row),
        grid_spec=pltpu.PrefetchScalarGridSpec(
            num_scalar_prefetch=2,
            grid=(n_tiles,),
            in_specs=[smem(lambda t, te, nu: (0, 0, 0)),
                      smem(lambda t, te, nu: (jnp.minimum(t + 1, nu[0] - 1), 0, 0)),
                      smem(lambda t, te, nu: (t, 0, 0)),
                      pl.BlockSpec(memory_space=pl.ANY),
                      pl.BlockSpec((1, d, f), wmap),
                      pl.BlockSpec((1, d, f), wmap),
                      pl.BlockSpec((1, f, d), wmap)],
            out_specs=pl.BlockSpec(memory_space=pl.ANY),
            scratch_shapes=[pltpu.VMEM((2 * tm * slab, LANES), F32),
                            pltpu.VMEM((2 * tm * slab, LANES), F32),
                            pltpu.SemaphoreType.DMA((2,)), pltpu.SemaphoreType.DMA((2,)),
                            pltpu.VMEM((d, f), BF16), pltpu.VMEM((d, f), BF16),
                            pltpu.VMEM((f, d), BF16)]),
        out_shape=jax.ShapeDtypeStruct((y_rows * slab, LANES), F32),
        compiler_params=_cparams(("arbitrary",)),
        name="moe_experts",
    )(tile_expert, n_used, src3, src3, dst3, h2, w_gate, w_up, w_down)


def _final_kernel(info_ref, x1_ref, mod_ref, g_ref, ya_ref, yb_ref, o_ref):
    info = info_ref[...]
    tm, d = x1_ref.shape
    ya = _from_slabs(ya_ref, 0, tm, d // LANES)
    yb = _from_slabs(yb_ref, 0, tm, d // LANES)
    y = info[:, 2:3] * ya + info[:, 3:4] * yb
    o_ref[...] = x1_ref[...] + mod_ref[0][5:6, :] * _rms(y, g_ref[...])


def _moe_final(info, x1, mod3, g_post, y2, seq, tm):
    t, d = x1.shape
    per_b = seq // tm
    nb = t // tm
    return pl.pallas_call(
        _final_kernel,
        grid=(nb,),
        in_specs=[pl.BlockSpec((tm, LANES), lambda i: (i, 0)),
                  pl.BlockSpec((tm, d), lambda i: (i, 0)),
                  pl.BlockSpec((1, 6, d), lambda i: (i // per_b, 0, 0)),
                  pl.BlockSpec((1, d), lambda i: (0, 0)),
                  pl.BlockSpec((tm * (d // LANES), LANES), lambda i: (i, 0)),
                  pl.BlockSpec((tm * (d // LANES), LANES), lambda i: (nb + i, 0))],
        out_specs=pl.BlockSpec((tm, d), lambda i: (i, 0)),
        out_shape=jax.ShapeDtypeStruct((t, d), F32),
        compiler_params=_cparams(("parallel",)),
        name="moe_final",
    )(info, x1, mod3, g_post, y2, y2)


def _slot_map_kernel(pos_ref, src_in, dst_in, src_ref, dst_ref, sem, *, tm, n_tokens):
    @pl.when(pl.program_id(0) == 0)
    def _():
        c_src = pltpu.make_async_copy(src_in, src_ref, sem.at[0])
        c_dst = pltpu.make_async_copy(dst_in, dst_ref, sem.at[1])
        c_src.start()
        c_dst.start()
        c_src.wait()
        c_dst.wait()

    base = pl.program_id(0) * tm

    def body(i, c):
        tok = base + i
        p0 = pos_ref[0, 0, 2 * i]
        p1 = pos_ref[0, 0, 2 * i + 1]
        src_ref[p0] = tok
        src_ref[p1] = tok
        dst_ref[p0] = tok
        dst_ref[p1] = n_tokens + tok
        return c

    lax.fori_loop(0, tm, body, 0, unroll=8)


def _slot_map(pos, src0, dst0, tm):
    t = pos.shape[0]
    whole = pl.BlockSpec(memory_space=pltpu.SMEM)
    return pl.pallas_call(
        functools.partial(_slot_map_kernel, tm=tm, n_tokens=t),
        grid=(t // tm,),
        in_specs=[pl.BlockSpec((1, 1, 2 * tm), lambda i: (i, 0, 0), memory_space=pltpu.SMEM),
                  pl.BlockSpec(memory_space=pl.ANY), pl.BlockSpec(memory_space=pl.ANY)],
        out_specs=[whole, whole],
        out_shape=[jax.ShapeDtypeStruct(src0.shape, jnp.int32)] * 2,
        scratch_shapes=[pltpu.SemaphoreType.DMA((2,))],
        compiler_params=_cparams(("arbitrary",)),
        name="slot_map",
    )(pos.reshape(t // tm, 1, 2 * tm), src0, dst0)


def _moe(h2, logits, x1, mod3, g_post, w_gate, w_up, w_down, seq, tm_route, tm_exp, tm_fin):
    t = x1.shape[0]
    info, cnt = _route(logits, tm_route)
    counts = cnt[0, :N_EXPERTS].astype(jnp.int32)
    tiles_e = (counts + tm_exp - 1) // tm_exp
    tile_end = jnp.cumsum(tiles_e)
    offs = (tile_end - tiles_e) * tm_exp
    n_used = tile_end[-1]
    e12 = info[:, 0:2].astype(jnp.int32)
    pos = offs[e12] + info[:, 4:6].astype(jnp.int32)
    n_tiles = (2 * t) // tm_exp + N_EXPERTS
    tidx = jnp.minimum(jnp.arange(n_tiles, dtype=jnp.int32), n_used - 1)
    tile_expert = jnp.sum(tidx[:, None] >= tile_end[None, :], axis=1).astype(jnp.int32)
    slot_ids = jnp.arange(n_tiles * tm_exp, dtype=jnp.int32)
    spare = 2 * t + ((slot_ids // tm_exp) % 2) * tm_exp + slot_ids % tm_exp
    src, dst = _slot_map(pos, jnp.zeros_like(slot_ids), spare, tm_route)

    ew = w_gate.shape
    wg = w_gate.reshape(N_EXPERTS, ew[-2], ew[-1])
    wu = w_up.reshape(N_EXPERTS, ew[-2], ew[-1])
    wd = w_down.reshape(N_EXPERTS, ew[-1], ew[-2])
    y2 = _experts(tile_expert, n_used.reshape(1).astype(jnp.int32),
                  src.reshape(n_tiles, 1, tm_exp), dst.reshape(n_tiles, 1, tm_exp),
                  h2, wg, wu, wd, 2 * t + 3 * tm_exp, tm_exp, 2 * t + 2 * tm_exp)
    return _moe_final(info, x1, mod3, g_post, y2, seq, tm_fin)


def _permute_w_in(w_in):
    sizes = [ATT_W] + [KV_W] * 6 + [3 * N_ATT_HEADS] + [ATT_W] * 4
    cuts = np.cumsum(sizes)[:-1]
    q, kc, vc, ks, vs, kw, vw, gates, hq, hf, hi, hg = jnp.split(w_in, cuts, axis=1)
    w_p = jnp.concatenate([q, kc, ks, kw, vc, vs, vw, hq, hf, hi, hg], axis=1).astype(BF16)
    w_g = jnp.pad(gates, ((0, 0), (0, LANES - gates.shape[1]))).astype(BF16)
    return w_p, w_g


def _half_blocks(proj, col, batch, seq):
    a = proj[:, col:col + KV_W].reshape(batch, seq, N_KV_GROUPS, HEAD_DIM)
    a = a.transpose(0, 2, 1, 3)
    return a.reshape(batch * N_KV_GROUPS, seq // CMP_STRIDE, CMP_STRIDE * HEAD_DIM)


def _block(x, c, positions, w_ada, b_ada, g_pre_mix, g_post_mix, g_pre_ffn, g_post_ffn,
           w_in, w_out, pe_k, w1_k, w2_k, pe_v, w1_v, w2_v, lb_logits, g_norm,
           w_group, b_group, w_router, b_router, w_gate, w_up, w_down, cfg):
    batch, seq, d = x.shape
    t = batch * seq
    x2 = x.reshape(t, d)

    c8 = jnp.zeros((8, d), F32).at[:batch].set(c)
    mod3 = _ada_mod(c8, w_ada, b_ada)[:batch].reshape(batch, 6, d)
    cos_t, sin_t = _rope_tables(positions, cfg["tm_rope"])
    w_p, w_g = _permute_w_in(w_in)
    proj, gates = _inproj(x2, mod3, g_pre_mix.reshape(1, d), w_p, w_g, cos_t, sin_t,
                          seq, cfg["tm_in"])

    kc = _compress(_half_blocks(proj, COL_KC, batch, seq), w1_k, pe_k, w2_k)
    vc = _compress(_half_blocks(proj, COL_VC, batch, seq), w1_v, pe_v, w2_v)
    ocmp, selb = _cmp_attn(proj, kc, vc, gates, batch, seq, cfg["tq_cmp"])
    o_att = _selwin_attn(proj, selb, ocmp, gates, batch, seq, cfg["tq"], cfg["tk"])
    o_rec = _hgrn(proj, lb_logits, g_norm.reshape(1, -1), batch, seq, cfg["ts_hgrn"],
                  cfg["hb_hgrn"])

    wr = jnp.concatenate([w_router, w_group], axis=1)
    wr = jnp.pad(wr, ((0, 0), (0, LANES - wr.shape[1])))
    br = jnp.pad(jnp.concatenate([b_router, b_group]), (0, LANES - N_EXPERTS - N_EXPERT_GROUPS))
    x1, h2, logits = _outproj(o_att, o_rec, w_out, x2, mod3, g_post_mix.reshape(1, d),
                               g_pre_ffn.reshape(1, d), wr, br.reshape(1, LANES), seq,
                               cfg["tm_out"])
    out = _moe(h2, logits, x1, mod3, g_post_ffn.reshape(1, d), w_gate, w_up, w_down, seq,
               cfg["tm_route"], cfg["tm_exp"], cfg["tm_fin"])
    return out.reshape(batch, seq, d)


def _config(seq):
    return dict(tm_rope=min(1024, seq), tm_in=min(1024, seq), tq_cmp=min(256, seq),
                tq=min(256, seq), tk=min(512, seq), ts_hgrn=min(256, seq), hb_hgrn=4,
                tm_out=min(256, seq), tm_route=min(256, seq), tm_exp=256,
                tm_fin=min(512, seq))


def kernel(x, c, positions, w_ada, b_ada, g_pre_mix, g_post_mix, g_pre_ffn, g_post_ffn, w_in, w_out, cmp_pe_k, cmp_w1_k, cmp_w2_k, cmp_pe_v, cmp_w1_v, cmp_w2_v, hgrn_lb_logits, hgrn_g_norm, w_group, b_group, w_router, b_router, w_gate, w_up, w_down):
    assert w_ada.shape[0] == 1, "single-layer block"
    return _block(x, c, positions, w_ada[0], b_ada[0], g_pre_mix[0], g_post_mix[0],
                  g_pre_ffn[0], g_post_ffn[0], w_in[0], w_out[0], cmp_pe_k[0], cmp_w1_k[0],
                  cmp_w2_k[0], cmp_pe_v[0], cmp_w1_v[0], cmp_w2_v[0], hgrn_lb_logits,
                  hgrn_g_norm[0], w_group[0], b_group[0], w_router[0], b_router[0],
                  w_gate[0], w_up[0], w_down[0], _config(x.shape[1]))
```

```python
import functools

import numpy as np
import jax
import jax.numpy as jnp
from jax import lax
from jax.experimental import pallas as pl
from jax.experimental.pallas import tpu as pltpu

F32 = jnp.float32
BF16 = jnp.bfloat16

HEAD_DIM = 128
N_KV_GROUPS = 2
HEADS_PER_GROUP = 4
N_ATT_HEADS = N_KV_GROUPS * HEADS_PER_GROUP
CMP_BLOCK = 32
CMP_STRIDE = 16
SEL_BLOCK = 64
N_SEL = 16
WINDOW = 512
FORCE_BONUS = 1.0e4
ROPE_THETA = 10000.0
N_HGRN_HEADS = 8
HGRN_CHUNK = 64
N_EXPERT_GROUPS = 4
EXPERTS_PER_GROUP = 8
N_EXPERTS = N_EXPERT_GROUPS * EXPERTS_PER_GROUP
EPS = 1e-6

LANES = 128
VMEM_LIMIT = 56 * 1024 * 1024

ATT_W = N_ATT_HEADS * HEAD_DIM
KV_W = N_KV_GROUPS * HEAD_DIM
COL_Q = 0
COL_KC = COL_Q + ATT_W
COL_KS = COL_KC + KV_W
COL_KW = COL_KS + KV_W
COL_VC = COL_KW + KV_W
COL_VS = COL_VC + KV_W
COL_VW = COL_VS + KV_W
COL_HQ = COL_VW + KV_W
COL_HF = COL_HQ + ATT_W
COL_HI = COL_HF + ATT_W
COL_HG = COL_HI + ATT_W
PROJ_W = COL_HG + ATT_W
ROPE_W = COL_VC

MASK_BIG = float(2 ** 30)
LOG2E = 1.4426950408889634
Q_PRESCALE = HEAD_DIM ** -0.5 * LOG2E

_NT = (((1,), (1,)), ((), ()))
_TN = (((0,), (0,)), ((), ()))


def _cparams(sem):
    return pltpu.CompilerParams(dimension_semantics=sem, vmem_limit_bytes=VMEM_LIMIT)


def _dot(a, b, **kw):
    return jnp.dot(a, b, preferred_element_type=F32, **kw)


def _dot_nt(a, b):
    return lax.dot_general(a, b, _NT, preferred_element_type=F32)


def _sigmoid(x):
    return 1.0 / (1.0 + jnp.exp(-x))


def _silu(x):
    return x * _sigmoid(x)


def _rms(x, g):
    return x * lax.rsqrt(jnp.mean(x * x, axis=-1, keepdims=True) + EPS) * g


def _lane_col(x, lane, col):
    return jnp.sum(jnp.where(lane == col, x, 0.0), axis=-1, keepdims=True)


def _ada_kernel(c_ref, w_ref, b_ref, o_ref):
    s = _silu(c_ref[...])
    o_ref[...] = _dot(s, w_ref[...], precision=lax.Precision.HIGHEST) + b_ref[...]


def _ada_mod(c8, w_ada, b_ada, tn=512):
    rows, d = c8.shape
    n = w_ada.shape[1]
    return pl.pallas_call(
        _ada_kernel,
        grid=(n // tn,),
        in_specs=[pl.BlockSpec((rows, d), lambda j: (0, 0)),
                  pl.BlockSpec((d, tn), lambda j: (0, j)),
                  pl.BlockSpec((1, tn), lambda j: (0, j))],
        out_specs=pl.BlockSpec((rows, tn), lambda j: (0, j)),
        out_shape=jax.ShapeDtypeStruct((rows, n), F32),
        compiler_params=_cparams(("parallel",)),
        name="ada_mod",
    )(c8, w_ada, b_ada.reshape(1, n))


def _rope_kernel(ang_ref, cos_ref, sin_ref):
    a = ang_ref[...]
    lane = lax.broadcasted_iota(jnp.int32, a.shape, 1)
    cos_ref[...] = jnp.cos(a)
    sin_ref[...] = jnp.where(lane < HEAD_DIM // 2, -jnp.sin(a), jnp.sin(a))


def _rope_tables(positions, tm):
    t = positions.size
    inv_freq = ROPE_THETA ** (-jnp.arange(0, HEAD_DIM, 2, dtype=F32) / HEAD_DIM)
    ang = positions.reshape(t, 1).astype(F32) * jnp.concatenate([inv_freq, inv_freq])[None, :]
    spec = pl.BlockSpec((tm, HEAD_DIM), lambda i: (i, 0))
    return pl.pallas_call(
        _rope_kernel,
        grid=(t // tm,),
        in_specs=[spec],
        out_specs=[spec, spec],
        out_shape=[jax.ShapeDtypeStruct((t, HEAD_DIM), F32)] * 2,
        compiler_params=_cparams(("parallel",)),
        name="rope_tables",
    )(ang)


def _inproj_kernel(x_ref, mod_ref, g_ref, w_ref, wg_ref, cos_ref, sin_ref,
                   proj_ref, gates_ref, h_s, *, tn):
    j = pl.program_id(1)

    @pl.when(j == 0)
    def _():
        m = mod_ref[0]
        h = _rms(x_ref[...], g_ref[...]) * (1.0 + m[1:2, :]) + m[0:1, :]
        hb = h.astype(BF16)
        h_s[...] = hb
        gates_ref[...] = _dot(hb, wg_ref[...])

    heads = tn // HEAD_DIM
    n_rope_tiles = ROPE_W // tn
    rem_heads = (ROPE_W % tn) // HEAD_DIM

    def store(n_rot, scale=None):
        acc = _dot(h_s[...], w_ref[...])
        for c in range(heads):
            a = acc[:, c * HEAD_DIM:(c + 1) * HEAD_DIM]
            if c < n_rot:
                a = a * cos_ref[...] + pltpu.roll(a, HEAD_DIM // 2, axis=1) * sin_ref[...]
            if scale is not None:
                a = a * scale
            proj_ref[:, c * HEAD_DIM:(c + 1) * HEAD_DIM] = a.astype(BF16)

    @pl.when(j < ATT_W // tn)
    def _():
        store(heads, Q_PRESCALE)

    @pl.when((j >= ATT_W // tn) & (j < n_rope_tiles))
    def _():
        store(heads)

    if rem_heads:
        @pl.when(j == n_rope_tiles)
        def _():
            store(rem_heads)

    @pl.when(j >= n_rope_tiles + (1 if rem_heads else 0))
    def _():
        store(0)


def _inproj(x2, mod3, g_pre, w_in_p, w_gates, cos_t, sin_t, seq, tm, tn=512):
    t, d = x2.shape
    per_b = seq // tm
    return pl.pallas_call(
        functools.partial(_inproj_kernel, tn=tn),
        grid=(t // tm, PROJ_W // tn),
        in_specs=[pl.BlockSpec((tm, d), lambda i, j: (i, 0)),
                  pl.BlockSpec((1, 6, d), lambda i, j: (i // per_b, 0, 0)),
                  pl.BlockSpec((1, d), lambda i, j: (0, 0)),
                  pl.BlockSpec((d, tn), lambda i, j: (0, j)),
                  pl.BlockSpec((d, LANES), lambda i, j: (0, 0)),
                  pl.BlockSpec((tm, HEAD_DIM), lambda i, j: (i, 0)),
                  pl.BlockSpec((tm, HEAD_DIM), lambda i, j: (i, 0))],
        out_specs=[pl.BlockSpec((tm, tn), lambda i, j: (i, j)),
                   pl.BlockSpec((tm, LANES), lambda i, j: (i, 0))],
        out_shape=[jax.ShapeDtypeStruct((t, PROJ_W), BF16),
                   jax.ShapeDtypeStruct((t, LANES), F32)],
        scratch_shapes=[pltpu.VMEM((tm, d), BF16)],
        compiler_params=_cparams(("parallel", "arbitrary")),
        name="inproj",
    )(x2, mod3, g_pre, w_in_p, w_gates, cos_t, sin_t)


def _compress_kernel(h_ref, w1_ref, pe_ref, w2_ref, o_ref):
    hb = h_ref[0]
    half = hb.shape[1]
    ya = _dot(hb, w1_ref[0:half, :])
    yb = _dot(hb, w1_ref[half:2 * half, :])
    const = _dot(pe_ref[...], w1_ref[...])
    n = ya.shape[0]
    yb_next = pltpu.roll(yb, n - 1, axis=0)
    hid = _silu(ya + yb_next + const[0:1, :])
    out = _dot(hid.astype(BF16), w2_ref[...])
    row = lax.broadcasted_iota(jnp.int32, out.shape, 0)
    o_ref[0] = jnp.where(row < n - 1, out, 0.0).astype(BF16)


def _compress(hblk, w1, pe, w2):
    bg, n, half = hblk.shape
    pe8 = jnp.zeros((8, 2 * half), BF16).at[0].set(pe.reshape(-1).astype(BF16))
    return pl.pallas_call(
        _compress_kernel,
        grid=(bg,),
        in_specs=[pl.BlockSpec((1, n, half), lambda i: (i, 0, 0)),
                  pl.BlockSpec((2 * half, HEAD_DIM), lambda i: (0, 0)),
                  pl.BlockSpec((8, 2 * half), lambda i: (0, 0)),
                  pl.BlockSpec((HEAD_DIM, HEAD_DIM), lambda i: (0, 0))],
        out_specs=pl.BlockSpec((1, n, HEAD_DIM), lambda i: (i, 0, 0)),
        out_shape=jax.ShapeDtypeStruct((bg, n, HEAD_DIM), BF16),
        compiler_params=_cparams(("parallel",)),
        name="compress",
    )(hblk, w1.astype(BF16), pe8, w2.astype(BF16))


def _cmp_kernel(q_ref, kc_ref, vc_ref, gates_ref, ov_ref, ocmp_ref, selb_ref, *, tq, nsel):
    g = pl.program_id(0) % N_KV_GROUPS
    q0 = pl.program_id(1) * tq
    kc = kc_ref[0]
    vc = vc_ref[0]
    nc = kc.shape[0]
    t_idx = q0 + lax.broadcasted_iota(jnp.int32, (tq, nc), 0)
    c_idx = lax.broadcasted_iota(jnp.int32, (tq, nc), 1)
    c_ok = c_idx * CMP_STRIDE + (CMP_BLOCK - 1) <= t_idx
    gl = gates_ref[...]
    lane = lax.broadcasted_iota(jnp.int32, (tq, LANES), 1)
    psum = jnp.zeros((tq, nc), F32)
    for h in range(HEADS_PER_GROUP):
        qh = q_ref[:, h * HEAD_DIM:(h + 1) * HEAD_DIM]
        s = jnp.where(c_ok, _dot_nt(qh, kc), -jnp.inf)
        m = jnp.max(s, axis=-1, keepdims=True)
        m = jnp.where(m == -jnp.inf, 0.0, m)
        p = jnp.exp2(s - m)
        p = p / jnp.maximum(jnp.sum(p, axis=-1, keepdims=True), 1e-30)
        psum = psum + p
        o = _dot(p.astype(BF16), vc)
        glog = _lane_col(gl, lane, (g * HEADS_PER_GROUP + h) * 3)
        ocmp_ref[:, h * HEAD_DIM:(h + 1) * HEAD_DIM] = o * _sigmoid(glog)

    imp = _dot(psum, ov_ref[...], precision=lax.Precision.HIGHEST)
    cur = (q0 + lax.broadcasted_iota(jnp.int32, (tq, LANES), 0)) // SEL_BLOCK
    forced = (lane == 0) | (lane == cur) | (lane == cur - 1)
    imp = jnp.where(lane <= cur, imp + jnp.where(forced, FORCE_BONUS, 0.0), -jnp.inf)
    imp_t = imp.T[0:nsel, :]
    row = lax.broadcasted_iota(jnp.int32, (nsel, tq), 0)
    cnt = jnp.zeros((nsel, tq), F32)
    for k in range(nsel):
        vk = imp_t[k:k + 1, :]
        cnt = cnt + jnp.where(vk > imp_t, 1.0, 0.0)
        cnt = cnt + jnp.where((vk == imp_t) & (row > k), 1.0, 0.0)
    sel = (cnt < float(N_SEL)) & (imp_t > -jnp.inf)
    bias_t = jnp.where(sel, 0.0, -MASK_BIG)
    if nsel < LANES:
        bias_t = jnp.concatenate([bias_t, jnp.zeros((LANES - nsel, tq), F32)], axis=0)
    selb_ref[...] = bias_t.T.astype(BF16)


def _cmp_attn(proj, kc, vc, gates, batch, seq, tq):
    t = proj.shape[0]
    nq = seq // tq
    nc = kc.shape[1]
    nsel = seq // SEL_BLOCK
    assert nsel <= LANES and nsel % 8 == 0
    c_start = np.arange(nc) * CMP_STRIDE
    j_start = np.arange(LANES) * SEL_BLOCK
    ov = ((c_start[:, None] < j_start[None, :] + SEL_BLOCK)
          & (c_start[:, None] + CMP_BLOCK > j_start[None, :])
          & (np.arange(LANES)[None, :] < nsel)).astype(np.float32)
    gq = HEADS_PER_GROUP * HEAD_DIM
    qmap = lambda bg, i: ((bg // N_KV_GROUPS) * nq + i, bg % N_KV_GROUPS)
    return pl.pallas_call(
        functools.partial(_cmp_kernel, tq=tq, nsel=nsel),
        grid=(batch * N_KV_GROUPS, nq),
        in_specs=[pl.BlockSpec((tq, gq), qmap),
                  pl.BlockSpec((1, nc, HEAD_DIM), lambda bg, i: (bg, 0, 0)),
                  pl.BlockSpec((1, nc, HEAD_DIM), lambda bg, i: (bg, 0, 0)),
                  pl.BlockSpec((tq, LANES), lambda bg, i: ((bg // N_KV_GROUPS) * nq + i, 0)),
                  pl.BlockSpec((nc, LANES), lambda bg, i: (0, 0))],
        out_specs=[pl.BlockSpec((tq, gq), qmap),
                   pl.BlockSpec((tq, LANES), lambda bg, i: (bg * nq + i, 0))],
        out_shape=[jax.ShapeDtypeStruct((t, ATT_W), F32),
                   jax.ShapeDtypeStruct((batch * N_KV_GROUPS * seq, LANES), BF16)],
        compiler_params=_cparams(("parallel", "parallel")),
        name="cmp_attn",
    )(proj, kc, vc, gates, jnp.asarray(ov))


def _selwin_kernel(q_ref, selb_ref, ocmp_ref, gates_ref, ks_ref, vs_ref, kw_ref, vw_ref,
                   cb_ref, wb_ref, o_ref, kaug_s, vaug_s, vwaug_s, qaug_s, sbuf_s, m_s, acc_s,
                   swin_s, *, tq, tk, seq, nwin):
    g = pl.program_id(0) % N_KV_GROUPS
    qi = pl.program_id(1)
    q0 = qi * tq
    hq = HEADS_PER_GROUP * tq
    half = hq // 2

    @pl.when(qi == 0)
    def _():
        r = lax.broadcasted_iota(jnp.int32, (seq, HEAD_DIM), 0)
        ln = lax.broadcasted_iota(jnp.int32, (seq, HEAD_DIM), 1)
        ones_col = jnp.where(ln == 0, 1.0, 0.0).astype(BF16)
        kaug_s[:, 0:HEAD_DIM] = ks_ref[...]
        kaug_s[:, HEAD_DIM:2 * HEAD_DIM] = jnp.where(r // SEL_BLOCK == ln, 1.0, 0.0).astype(BF16)
        vaug_s[:, 0:HEAD_DIM] = vs_ref[...]
        vaug_s[:, HEAD_DIM:2 * HEAD_DIM] = ones_col
        vwaug_s[:, 0:HEAD_DIM] = vw_ref[...]
        vwaug_s[:, HEAD_DIM:2 * HEAD_DIM] = ones_col

    for h in range(HEADS_PER_GROUP):
        qaug_s[h * tq:(h + 1) * tq, 0:HEAD_DIM] = q_ref[:, h * HEAD_DIM:(h + 1) * HEAD_DIM]
        qaug_s[h * tq:(h + 1) * tq, HEAD_DIM:2 * HEAD_DIM] = selb_ref[...]
    qa = qaug_s[...]

    def scores(kt, slot):
        ks = kaug_s[pl.ds(pl.multiple_of(kt * tk, tk), tk), :]
        sbuf_s[slot] = _dot_nt(qa, ks)

    def absorb(kt, slot, diagonal=False):
        vs = vaug_s[pl.ds(pl.multiple_of(kt * tk, tk), tk), :]
        s = sbuf_s[slot]
        if diagonal:
            bias = cb_ref[(q0 - kt * tk) // tq]
            s = (s.reshape(HEADS_PER_GROUP, tq, tk) + bias[None]).reshape(hq, tk)
        m = m_s[...]
        m_new = jnp.maximum(m, jnp.max(s, axis=-1, keepdims=True))
        m_s[...] = m_new
        pb = jnp.exp2(s - m_new[:, 0:1]).astype(BF16)
        if diagonal:
            pv = jnp.concatenate([_dot(pb[0:half], vs), _dot(pb[half:hq], vs)], axis=0)
        else:
            pv = _dot(pb, vs)
        acc_s[...] = jnp.exp2(m - m_new)[:, 0:1] * acc_s[...] + pv

    n_full = q0 // tk
    m_s[...] = jnp.full(m_s.shape, -jnp.inf, F32)
    acc_s[...] = jnp.zeros(acc_s.shape, F32)
    scores(0, 0)

    def pair(j, c):
        scores(2 * j + 1, 1)
        absorb(2 * j, 0)
        scores(2 * j + 2, 0)
        absorb(2 * j + 1, 1)
        return c

    lax.fori_loop(0, n_full // 2, pair, 0)
    odd = n_full % 2

    @pl.when(odd == 1)
    def _():
        scores(n_full, 1)
        absorb(n_full - 1, 0)

    nq = seq // tq
    w0 = jnp.clip(qi - (nwin - 1), 0, nq - nwin) * tq
    wlen = nwin * tq
    kwin = kw_ref[pl.ds(pl.multiple_of(w0, tq), wlen), :]
    vwin = vwaug_s[pl.ds(pl.multiple_of(w0, tq), wlen), :]
    swin_s[0:half, :] = _dot_nt(qaug_s[0:half, 0:HEAD_DIM], kwin)
    swin_s[half:hq, :] = _dot_nt(qaug_s[half:hq, 0:HEAD_DIM], kwin)
    absorb(n_full, odd, diagonal=True)
    acc = acc_s[...]
    o_sel = acc[:, 0:HEAD_DIM] / acc[:, HEAD_DIM:HEAD_DIM + 1]

    wbias = wb_ref[(q0 - w0) // tq]
    sw = (swin_s[...].reshape(HEADS_PER_GROUP, tq, wlen) + wbias[None]).reshape(hq, wlen)
    pwb = jnp.exp2(sw - jnp.max(sw, axis=-1, keepdims=True)).astype(BF16)
    ow = jnp.concatenate([_dot(pwb[0:half], vwin), _dot(pwb[half:hq], vwin)], axis=0)
    o_win = ow[:, 0:HEAD_DIM] / ow[:, HEAD_DIM:HEAD_DIM + 1]

    gl = gates_ref[...]
    lane = lax.broadcasted_iota(jnp.int32, (tq, LANES), 1)
    for h in range(HEADS_PER_GROUP):
        gcol = (g * HEADS_PER_GROUP + h) * 3
        g_s = _sigmoid(_lane_col(gl, lane, gcol + 1))
        g_w = _sigmoid(_lane_col(gl, lane, gcol + 2))
        o = (ocmp_ref[:, h * HEAD_DIM:(h + 1) * HEAD_DIM]
             + g_s * o_sel[h * tq:(h + 1) * tq, :] + g_w * o_win[h * tq:(h + 1) * tq, :])
        o_ref[:, h * HEAD_DIM:(h + 1) * HEAD_DIM] = o.astype(BF16)


def _selwin_attn(proj, selb, ocmp, gates, batch, seq, tq, tk):
    t = proj.shape[0]
    nq = seq // tq
    assert tk % tq == 0 and tq & (tq - 1) == 0 and seq % tk == 0
    nwin = min(WINDOW // tq + 1, nq)
    assert WINDOW % tq == 0
    gq = HEADS_PER_GROUP * HEAD_DIM
    hq = HEADS_PER_GROUP * tq
    r = np.arange(tq)[:, None]
    cbias = np.stack([np.where(np.arange(tk)[None, :] <= r + off * tq, 0.0, -np.inf)
                      for off in range(tk // tq)]).astype(np.float32)
    dist = [r + off * tq - np.arange(nwin * tq)[None, :] for off in range(nwin)]
    wbias = np.stack([np.where((d >= 0) & (d < WINDOW), 0.0, -np.inf)
                      for d in dist]).astype(np.float32)
    qmap = lambda bg, i: ((bg // N_KV_GROUPS) * nq + i, bg % N_KV_GROUPS)

    def kvspec(col):
        return pl.BlockSpec((seq, HEAD_DIM),
                            lambda bg, i: (bg // N_KV_GROUPS, col // HEAD_DIM + bg % N_KV_GROUPS))

    return pl.pallas_call(
        functools.partial(_selwin_kernel, tq=tq, tk=tk, seq=seq, nwin=nwin),
        grid=(batch * N_KV_GROUPS, nq),
        in_specs=[pl.BlockSpec((tq, gq), qmap),
                  pl.BlockSpec((tq, LANES), lambda bg, i: (bg * nq + i, 0)),
                  pl.BlockSpec((tq, gq), qmap),
                  pl.BlockSpec((tq, LANES), lambda bg, i: ((bg // N_KV_GROUPS) * nq + i, 0)),
                  kvspec(COL_KS), kvspec(COL_VS), kvspec(COL_KW), kvspec(COL_VW),
                  pl.BlockSpec(cbias.shape, lambda bg, i: (0, 0, 0)),
                  pl.BlockSpec(wbias.shape, lambda bg, i: (0, 0, 0))],
        out_specs=pl.BlockSpec((tq, gq), qmap),
        out_shape=jax.ShapeDtypeStruct((t, ATT_W), BF16),
        scratch_shapes=[pltpu.VMEM((seq, 2 * HEAD_DIM), BF16),
                        pltpu.VMEM((seq, 2 * HEAD_DIM), BF16),
                        pltpu.VMEM((seq, 2 * HEAD_DIM), BF16),
                        pltpu.VMEM((hq, 2 * HEAD_DIM), BF16),
                        pltpu.VMEM((2, hq, tk), F32),
                        pltpu.VMEM((hq, LANES), F32),
                        pltpu.VMEM((hq, 2 * HEAD_DIM), F32),
                        pltpu.VMEM((hq, nwin * tq), F32)],
        compiler_params=_cparams(("parallel", "arbitrary")),
        name="selwin_attn",
    )(proj, selb, ocmp, gates, proj, proj, proj, proj, jnp.asarray(cbias), jnp.asarray(wbias))


_HGRN_LEVELS = (32, 16, 8, 4, 2, 1)


def _hgrn_tables():
    c = HGRN_CHUNK
    nl = len(_HGRN_LEVELS)
    idx = np.arange(c)
    i, j = idx[:, None], idx[None, :]
    mats = [j <= i, j > i]
    lvl = np.full((c, c), nl + 1, np.int32)
    for li, s in enumerate(_HGRN_LEVELS):
        mats.append((j >= (i // s) * s) & (j <= i))
        mats.append((j > i) & (j <= (i // s) * s + s - 1))
        lvl[((i // (2 * s)) == (j // (2 * s))) & (((i // s) % 2) == 1) & (((j // s) % 2) == 0)] = li
    lvl[i == j] = nl
    w = np.concatenate(mats, 0).astype(np.float32)
    return np.concatenate([w, w], axis=1), lvl


def _hgrn_head(hq_ref, hf_ref, hi_ref, hg_ref, lbl_ref, gn_ref, w_ref, lvl_ref, o_ref, st_s,
               hh, ts):
    c = HGRN_CHUNK
    nc = ts // c
    nl = len(_HGRN_LEVELS)
    cs = slice(hh * HEAD_DIM, (hh + 1) * HEAD_DIM)
    lbl = lbl_ref[:, cs]
    e = jnp.exp(lbl - jnp.max(lbl, axis=0, keepdims=True))
    lb = e[0:1, :] / jnp.sum(e, axis=0, keepdims=True)

    f = lb + (1.0 - lb) * _sigmoid(hf_ref[:, cs].astype(F32))
    k = 1.0 - f
    qf = _silu(hq_ref[:, cs].astype(F32))
    v = hi_ref[:, cs]

    lf = jnp.log(f) * LOG2E
    lf_hi = lf.astype(BF16)
    lf_lo = (lf - lf_hi.astype(F32)).astype(BF16)
    side = lambda a: jnp.concatenate([a[ci * c:(ci + 1) * c, :] for ci in range(nc)], axis=1)
    ex = _dot(w_ref[...], jnp.concatenate([side(lf_hi), side(lf_lo)], axis=0))

    def rows(bi):
        return jnp.concatenate(
            [ex[bi * c:(bi + 1) * c, ci * HEAD_DIM:(ci + 1) * HEAD_DIM] for ci in range(nc)], axis=0)

    b = rows(0)
    qhat = (qf * jnp.exp2(b)).astype(BF16)
    khat = (k * jnp.exp2(rows(1))).astype(BF16)
    lvl = lvl_ref[...]
    masks = [lvl == li for li in range(nl + 1)]
    qs = [(qf * jnp.exp2(rows(2 + 2 * li))).astype(BF16) for li in range(nl)] + [qf.astype(BF16)]
    ks = [(k * jnp.exp2(rows(3 + 2 * li))).astype(BF16) for li in range(nl)] + [k.astype(BF16)]
    o_intra = []
    for ci in range(nc):
        sl = slice(ci * c, (ci + 1) * c)
        a = jnp.zeros((c, c), F32)
        for li in range(nl + 1):
            a = jnp.where(masks[li], _dot_nt(qs[li][sl, :], ks[li][sl, :]), a)
        o_intra.append(_dot(a.astype(BF16), v[sl, :]))

    st = st_s[hh]
    outs = []
    for ci in range(nc):
        sl = slice(ci * c, (ci + 1) * c)
        outs.append(o_intra[ci] + _dot_nt(qhat[sl, :], st.astype(BF16)))
        st = st * jnp.exp2(b[(ci + 1) * c - 1:(ci + 1) * c, :]) + lax.dot_general(
            v[sl, :], khat[sl, :], _TN, preferred_element_type=F32)
    st_s[hh] = st
    o = jnp.concatenate(outs, axis=0)
    o = o * lax.rsqrt(jnp.mean(o * o, axis=-1, keepdims=True) + EPS) * gn_ref[:, cs]
    o_ref[:, cs] = (o * _silu(hg_ref[:, cs].astype(F32))).astype(BF16)


def _hgrn_kernel(hq_ref, hf_ref, hi_ref, hg_ref, lbl_ref, gn_ref, w_ref, lvl_ref,
                 o_ref, st_s, *, ts, hb):
    @pl.when(pl.program_id(2) == 0)
    def _():
        st_s[...] = jnp.zeros_like(st_s)

    for hh in range(hb):
        _hgrn_head(hq_ref, hf_ref, hi_ref, hg_ref, lbl_ref, gn_ref, w_ref, lvl_ref, o_ref, st_s,
                   hh, ts)


def _hgrn(proj, lb_logits, g_norm, batch, seq, ts, hb):
    t = proj.shape[0]
    ns = seq // ts
    wmat, lvl = _hgrn_tables()
    w = hb * HEAD_DIM

    def colspec(col):
        return pl.BlockSpec((ts, w), lambda b, h, s: (b * ns + s, col // w + h))

    nl = lb_logits.shape[0]
    return pl.pallas_call(
        functools.partial(_hgrn_kernel, ts=ts, hb=hb),
        grid=(batch, N_HGRN_HEADS // hb, ns),
        in_specs=[colspec(COL_HQ), colspec(COL_HF), colspec(COL_HI), colspec(COL_HG),
                  pl.BlockSpec((nl, w), lambda b, h, s: (0, h)),
                  pl.BlockSpec((1, w), lambda b, h, s: (0, h)),
                  pl.BlockSpec(wmat.shape, lambda b, h, s: (0, 0)),
                  pl.BlockSpec(lvl.shape, lambda b, h, s: (0, 0))],
        out_specs=pl.BlockSpec((ts, w), lambda b, h, s: (b * ns + s, h)),
        out_shape=jax.ShapeDtypeStruct((t, ATT_W), BF16),
        scratch_shapes=[pltpu.VMEM((hb, HEAD_DIM, HEAD_DIM), F32)],
        compiler_params=_cparams(("parallel", "parallel", "arbitrary")),
        name="hgrn2",
    )(proj, proj, proj, proj, lb_logits, g_norm, jnp.asarray(wmat, BF16), jnp.asarray(lvl))


def _outproj_kernel(oa_ref, or_ref, wa_ref, wb_ref, x_ref, mod_ref, gpost_ref, gpre_ref,
                    wrh_ref, wrl_ref, br_ref, x1_ref, h2_ref, lg_ref, stage):
    m = mod_ref[0]
    y = _dot(oa_ref[...], wa_ref[...]) + _dot(or_ref[...], wb_ref[...])
    x1 = x_ref[...] + m[2:3, :] * _rms(y, gpost_ref[...])
    x1_ref[...] = x1
    h = _rms(x1, gpre_ref[...]) * (1.0 + m[4:5, :]) + m[3:4, :]
    h2_ref[...] = _to_slabs(stage, h)
    h_hi = h.astype(BF16)
    h_lo = (h - h_hi.astype(F32)).astype(BF16)
    lg_ref[...] = (_dot(h_hi, wrh_ref[...]) + _dot(h_lo, wrh_ref[...])
                   + _dot(h_hi, wrl_ref[...]) + br_ref[...])


def _outproj(o_att, o_rec, w_out, x2, mod3, g_post, g_pre, wr, br, seq, tm):
    t, d = x2.shape
    per_b = seq // tm
    wr_hi = wr.astype(BF16)
    wr_lo = (wr - wr_hi.astype(F32)).astype(BF16)
    wa = w_out[:ATT_W].astype(BF16)
    wb = w_out[ATT_W:].astype(BF16)
    row = lambda i: (i, 0)
    fixed = lambda i: (0, 0)
    return pl.pallas_call(
        _outproj_kernel,
        grid=(t // tm,),
        in_specs=[pl.BlockSpec((tm, ATT_W), row), pl.BlockSpec((tm, ATT_W), row),
                  pl.BlockSpec((ATT_W, d), fixed), pl.BlockSpec((ATT_W, d), fixed),
                  pl.BlockSpec((tm, d), row),
                  pl.BlockSpec((1, 6, d), lambda i: (i // per_b, 0, 0)),
                  pl.BlockSpec((1, d), fixed), pl.BlockSpec((1, d), fixed),
                  pl.BlockSpec((d, LANES), fixed), pl.BlockSpec((d, LANES), fixed),
                  pl.BlockSpec((1, LANES), fixed)],
        out_specs=[pl.BlockSpec((tm, d), row), pl.BlockSpec((tm * (d // LANES), LANES), row),
                   pl.BlockSpec((tm, LANES), row)],
        out_shape=[jax.ShapeDtypeStruct((t, d), F32),
                   jax.ShapeDtypeStruct((t * (d // LANES), LANES), BF16),
                   jax.ShapeDtypeStruct((t, LANES), F32)],
        scratch_shapes=[pltpu.VMEM((tm * (d // LANES), LANES), F32)],
        compiler_params=_cparams(("parallel",)),
        name="outproj",
    )(o_att, o_rec, wa, wb, x2, mod3, g_post, g_pre, wr_hi, wr_lo, br)


G_LANE0 = N_EXPERTS


def _route_kernel(lg_ref, tri_ref, info_ref, cnt_ref, carry_s):
    @pl.when(pl.program_id(0) == 0)
    def _():
        carry_s[...] = jnp.zeros_like(carry_s)

    x = lg_ref[...]
    lane = lax.broadcasted_iota(jnp.int32, x.shape, 1)
    big = jnp.int32(10 ** 6)
    rmax = lambda a: jnp.max(a, axis=-1, keepdims=True)
    rmin = lambda a: jnp.min(a, axis=-1, keepdims=True)
    rsum = lambda a: jnp.sum(a, axis=-1, keepdims=True)

    is_g = (lane >= G_LANE0) & (lane < G_LANE0 + N_EXPERT_GROUPS)
    lgm = jnp.where(is_g, x, -jnp.inf)
    mg = rmax(lgm)
    pg_sel = 1.0 / rsum(jnp.where(is_g, jnp.exp(lgm - mg), 0.0))
    gsel = rmin(jnp.where(lgm == mg, lane, big)) - G_LANE0

    is_e = (lane >= gsel * EXPERTS_PER_GROUP) & (lane < (gsel + 1) * EXPERTS_PER_GROUP)
    lem = jnp.where(is_e, x, -jnp.inf)
    pe = jnp.where(is_e, jnp.exp(lem - rmax(lem)), 0.0)
    pe = pe / rsum(pe)
    pe = jnp.where(is_e, pe, -1.0)
    v1 = rmax(pe)
    i1 = rmin(jnp.where(pe == v1, lane, big))
    pe2 = jnp.where(lane == i1, -1.0, pe)
    v2 = rmax(pe2)
    i2 = rmin(jnp.where(pe2 == v2, lane, big))
    w1 = v1 / (v1 + v2) * pg_sel
    w2 = v2 / (v1 + v2) * pg_sel

    oh1 = jnp.where(lane == i1, 1.0, 0.0)
    oh2 = jnp.where(lane == i2, 1.0, 0.0)
    both = oh1 + oh2
    before = _dot(tri_ref[...], both.astype(BF16)) + carry_s[0:1, :]
    r1 = rsum(oh1 * before)
    r2 = rsum(oh2 * before)
    carry_s[0:1, :] = carry_s[0:1, :] + jnp.sum(both, axis=0, keepdims=True)
    cnt_ref[...] = carry_s[...]

    cols = (i1.astype(F32), i2.astype(F32), w1, w2, r1, r2)
    info = jnp.zeros(x.shape, F32)
    for ci, col in enumerate(cols):
        info = jnp.where(lane == ci, col, info)
    info_ref[...] = info


def _route(logits, tm):
    t = logits.shape[0]
    tri = jnp.asarray(np.tril(np.ones((tm, tm), np.float32), -1), BF16)
    return pl.pallas_call(
        _route_kernel,
        grid=(t // tm,),
        in_specs=[pl.BlockSpec((tm, LANES), lambda i: (i, 0)),
                  pl.BlockSpec((tm, tm), lambda i: (0, 0))],
        out_specs=[pl.BlockSpec((tm, LANES), lambda i: (i, 0)),
                   pl.BlockSpec((8, LANES), lambda i: (0, 0))],
        out_shape=[jax.ShapeDtypeStruct((t, LANES), F32),
                   jax.ShapeDtypeStruct((8, LANES), F32)],
        scratch_shapes=[pltpu.VMEM((8, LANES), F32)],
        compiler_params=_cparams(("arbitrary",)),
        name="route",
    )(logits, tri)


def _to_slabs(stage, x):
    n, d = x.shape
    slab = d // LANES
    for c in range(slab):
        stage[pl.ds(c, n, stride=slab), :] = x[:, c * LANES:(c + 1) * LANES]
    return stage[...].astype(BF16)


def _from_slabs(stage, slabs, n):
    slab = slabs.shape[0] // n
    stage[...] = slabs.astype(F32)
    return jnp.concatenate([stage[pl.ds(c, n, stride=slab), :] for c in range(slab)], axis=1)


def _expert_kernel(te_ref, nu_ref, src0_ref, srcn_ref, dst_ref, h_ref, wg_ref, wu_ref, wd_ref,
                   y_ref, xbuf, obuf, stage, gsem, ssem, wg_s, wu_s, wd_s, *, tm, standin_row):
    t = pl.program_id(0)
    nu = nu_ref[0]
    slot = t % 2
    slab = wg_s.shape[0] // LANES
    tile = tm * slab

    def token(ref, i):
        return ref.at[pl.ds(pl.multiple_of(i * slab, slab), slab)]

    def gather(src_ref, s):
        for r in range(tm):
            pltpu.make_async_copy(token(h_ref, src_ref[0, 0, r]), token(xbuf, s * tm + r),
                                  gsem.at[s]).start(priority=r % 2)

    def scatter(s, row_of):
        for r in range(tm):
            pltpu.make_async_copy(token(obuf, s * tm + r), token(y_ref, row_of(r)),
                                  ssem.at[s]).start(priority=r % 2)

    def tile_of(buf, s):
        return buf.at[pl.ds(pl.multiple_of(s * tile, tile), tile)]

    def wait_tile(buf, sem, s):
        pltpu.make_async_copy(tile_of(buf, s), tile_of(buf, s), sem.at[s]).wait()

    @pl.when(t < nu)
    def _():
        @pl.when(t == 0)
        def _():
            gather(src0_ref, 0)
            obuf[tile:2 * tile, :] = jnp.zeros((tile, LANES), obuf.dtype)

            def fill(row0, s):
                return pltpu.make_async_copy(tile_of(obuf, 1),
                                             y_ref.at[pl.ds(row0 * slab, tile)], ssem.at[s])

            fill(standin_row - 2 * tm, 0).start()
            fill(standin_row - tm, 0).start()
            fill(standin_row - 2 * tm, 0).wait()
            fill(standin_row - tm, 0).wait()
            fill(standin_row, 1).start()

        @pl.when((t == 0) | (te_ref[t] != te_ref[jnp.maximum(t - 1, 0)]))
        def _():
            wg_s[...] = wg_ref[0].astype(BF16)
            wu_s[...] = wu_ref[0].astype(BF16)
            wd_s[...] = wd_ref[0].astype(BF16)

        gather(srcn_ref, 1 - slot)
        wait_tile(xbuf, gsem, slot)
        row0 = pl.multiple_of(slot * tile, tile)
        x = _from_slabs(stage, xbuf[pl.ds(row0, tile), :], tm).astype(BF16)
        hid = (_silu(_dot(x, wg_s[...])) * _dot(x, wu_s[...])).astype(BF16)
        obuf[pl.ds(row0, tile), :] = _to_slabs(stage, _dot(hid, wd_s[...]))
        scatter(slot, lambda r: dst_ref[0, 0, r])
        wait_tile(obuf, ssem, 1 - slot)

        @pl.when(t == nu - 1)
        def _():
            wait_tile(obuf, ssem, slot)
            wait_tile(xbuf, gsem, 1 - slot)


def _experts(tile_expert, n_used, src3, dst3, h2, w_gate, w_up, w_down, y_rows, tm,
             standin_row):
    n_tiles = src3.shape[0]
    d, f = w_gate.shape[1:]
    slab = d // LANES
    wmap = lambda t, te, nu: (te[t], 0, 0)
    smem = lambda imap: pl.BlockSpec((1, 1, tm), imap, memory_space=pltpu.SMEM)
    return pl.pallas_call(
        functools.partial(_expert_kernel, tm=tm, standin_row=standin_row),
        grid_spec=pltpu.PrefetchScalarGridSpec(
            num_scalar_prefetch=2,
            grid=(n_tiles,),
            in_specs=[smem(lambda t, te, nu: (0, 0, 0)),
                      smem(lambda t, te, nu: (jnp.minimum(t + 1, nu[0] - 1), 0, 0)),
                      smem(lambda t, te, nu: (t, 0, 0)),
                      pl.BlockSpec(memory_space=pl.ANY),
                      pl.BlockSpec((1, d, f), wmap),
                      pl.BlockSpec((1, d, f), wmap),
                      pl.BlockSpec((1, f, d), wmap)],
            out_specs=pl.BlockSpec(memory_space=pl.ANY),
            scratch_shapes=[pltpu.VMEM((2 * tm * slab, LANES), BF16),
                            pltpu.VMEM((2 * tm * slab, LANES), BF16),
                            pltpu.VMEM((tm * slab, LANES), F32),
                            pltpu.SemaphoreType.DMA((2,)), pltpu.SemaphoreType.DMA((2,)),
                            pltpu.VMEM((d, f), BF16), pltpu.VMEM((d, f), BF16),
                            pltpu.VMEM((f, d), BF16)]),
        out_shape=jax.ShapeDtypeStruct((y_rows * slab, LANES), BF16),
        compiler_params=_cparams(("arbitrary",)),
        name="moe_experts",
    )(tile_expert, n_used, src3, src3, dst3, h2, w_gate, w_up, w_down)


def _final_kernel(info_ref, x1_ref, mod_ref, g_ref, ya_ref, yb_ref, o_ref, stage_a, stage_b):
    info = info_ref[...]
    tm = x1_ref.shape[0]
    ya = _from_slabs(stage_a, ya_ref[...], tm)
    yb = _from_slabs(stage_b, yb_ref[...], tm)
    y = info[:, 2:3] * ya + info[:, 3:4] * yb
    o_ref[...] = x1_ref[...] + mod_ref[0][5:6, :] * _rms(y, g_ref[...])


def _moe_final(info, x1, mod3, g_post, y2, seq, tm):
    t, d = x1.shape
    per_b = seq // tm
    nb = t // tm
    return pl.pallas_call(
        _final_kernel,
        grid=(nb,),
        in_specs=[pl.BlockSpec((tm, LANES), lambda i: (i, 0)),
                  pl.BlockSpec((tm, d), lambda i: (i, 0)),
                  pl.BlockSpec((1, 6, d), lambda i: (i // per_b, 0, 0)),
                  pl.BlockSpec((1, d), lambda i: (0, 0)),
                  pl.BlockSpec((tm * (d // LANES), LANES), lambda i: (i, 0)),
                  pl.BlockSpec((tm * (d // LANES), LANES), lambda i: (nb + i, 0))],
        out_specs=pl.BlockSpec((tm, d), lambda i: (i, 0)),
        out_shape=jax.ShapeDtypeStruct((t, d), F32),
        scratch_shapes=[pltpu.VMEM((tm * (d // LANES), LANES), F32)] * 2,
        compiler_params=_cparams(("parallel",)),
        name="moe_final",
    )(info, x1, mod3, g_post, y2, y2)


def _slot_map_kernel(pos_ref, src_in, dst_in, src_ref, dst_ref, sem, *, tm, n_tokens):
    @pl.when(pl.program_id(0) == 0)
    def _():
        c_src = pltpu.make_async_copy(src_in, src_ref, sem.at[0])
        c_dst = pltpu.make_async_copy(dst_in, dst_ref, sem.at[1])
        c_src.start()
        c_dst.start()
        c_src.wait()
        c_dst.wait()

    base = pl.program_id(0) * tm

    def body(i, c):
        tok = base + i
        p0 = pos_ref[0, 0, 2 * i]
        p1 = pos_ref[0, 0, 2 * i + 1]
        src_ref[p0] = tok
        src_ref[p1] = tok
        dst_ref[p0] = tok
        dst_ref[p1] = n_tokens + tok
        return c

    lax.fori_loop(0, tm, body, 0, unroll=8)


def _slot_map(pos, src0, dst0, tm):
    t = pos.shape[0]
    whole = pl.BlockSpec(memory_space=pltpu.SMEM)
    return pl.pallas_call(
        functools.partial(_slot_map_kernel, tm=tm, n_tokens=t),
        grid=(t // tm,),
        in_specs=[pl.BlockSpec((1, 1, 2 * tm), lambda i: (i, 0, 0), memory_space=pltpu.SMEM),
                  pl.BlockSpec(memory_space=pl.ANY), pl.BlockSpec(memory_space=pl.ANY)],
        out_specs=[whole, whole],
        out_shape=[jax.ShapeDtypeStruct(src0.shape, jnp.int32)] * 2,
        scratch_shapes=[pltpu.SemaphoreType.DMA((2,))],
        compiler_params=_cparams(("arbitrary",)),
        name="slot_map",
    )(pos.reshape(t // tm, 1, 2 * tm), src0, dst0)


def _moe(h2, logits, x1, mod3, g_post, w_gate, w_up, w_down, seq, tm_route, tm_exp, tm_fin):
    t = x1.shape[0]
    info, cnt = _route(logits, tm_route)
    counts = cnt[0, :N_EXPERTS].astype(jnp.int32)
    tiles_e = (counts + tm_exp - 1) // tm_exp
    tile_end = jnp.cumsum(tiles_e)
    offs = (tile_end - tiles_e) * tm_exp
    n_used = tile_end[-1]
    e12 = info[:, 0:2].astype(jnp.int32)
    pos = offs[e12] + info[:, 4:6].astype(jnp.int32)
    n_tiles = (2 * t) // tm_exp + N_EXPERTS
    tidx = jnp.minimum(jnp.arange(n_tiles, dtype=jnp.int32), n_used - 1)
    tile_expert = jnp.sum(tidx[:, None] >= tile_end[None, :], axis=1).astype(jnp.int32)
    slot_ids = jnp.arange(n_tiles * tm_exp, dtype=jnp.int32)
    spare = 2 * t + ((slot_ids // tm_exp) % 2) * tm_exp + slot_ids % tm_exp
    src, dst = _slot_map(pos, jnp.zeros_like(slot_ids), spare, tm_route)

    ew = w_gate.shape
    wg = w_gate.reshape(N_EXPERTS, ew[-2], ew[-1])
    wu = w_up.reshape(N_EXPERTS, ew[-2], ew[-1])
    wd = w_down.reshape(N_EXPERTS, ew[-1], ew[-2])
    y2 = _experts(tile_expert, n_used.reshape(1).astype(jnp.int32),
                  src.reshape(n_tiles, 1, tm_exp), dst.reshape(n_tiles, 1, tm_exp),
                  h2, wg, wu, wd, 2 * t + 3 * tm_exp, tm_exp, 2 * t + 2 * tm_exp)
    return _moe_final(info, x1, mod3, g_post, y2, seq, tm_fin)


def _permute_w_in(w_in):
    sizes = [ATT_W] + [KV_W] * 6 + [3 * N_ATT_HEADS] + [ATT_W] * 4
    cuts = np.cumsum(sizes)[:-1]
    q, kc, vc, ks, vs, kw, vw, gates, hq, hf, hi, hg = jnp.split(w_in, cuts, axis=1)
    w_p = jnp.concatenate([q, kc, ks, kw, vc, vs, vw, hq, hf, hi, hg], axis=1).astype(BF16)
    w_g = jnp.pad(gates, ((0, 0), (0, LANES - gates.shape[1]))).astype(BF16)
    return w_p, w_g


def _half_blocks(proj, col, batch, seq):
    a = proj[:, col:col + KV_W].reshape(batch, seq, N_KV_GROUPS, HEAD_DIM)
    a = a.transpose(0, 2, 1, 3)
    return a.reshape(batch * N_KV_GROUPS, seq // CMP_STRIDE, CMP_STRIDE * HEAD_DIM)


def _block(x, c, positions, w_ada, b_ada, g_pre_mix, g_post_mix, g_pre_ffn, g_post_ffn,
           w_in, w_out, pe_k, w1_k, w2_k, pe_v, w1_v, w2_v, lb_logits, g_norm,
           w_group, b_group, w_router, b_router, w_gate, w_up, w_down, cfg):
    batch, seq, d = x.shape
    t = batch * seq
    x2 = x.reshape(t, d)

    c8 = jnp.zeros((8, d), F32).at[:batch].set(c)
    mod3 = _ada_mod(c8, w_ada, b_ada)[:batch].reshape(batch, 6, d)
    cos_t, sin_t = _rope_tables(positions, cfg["tm_rope"])
    w_p, w_g = _permute_w_in(w_in)
    proj, gates = _inproj(x2, mod3, g_pre_mix.reshape(1, d), w_p, w_g, cos_t, sin_t,
                          seq, cfg["tm_in"])

    kc = _compress(_half_blocks(proj, COL_KC, batch, seq), w1_k, pe_k, w2_k)
    vc = _compress(_half_blocks(proj, COL_VC, batch, seq), w1_v, pe_v, w2_v)
    ocmp, selb = _cmp_attn(proj, kc, vc, gates, batch, seq, cfg["tq_cmp"])
    o_att = _selwin_attn(proj, selb, ocmp, gates, batch, seq, cfg["tq"], cfg["tk"])
    o_rec = _hgrn(proj, lb_logits, g_norm.reshape(1, -1), batch, seq, cfg["ts_hgrn"],
                  cfg["hb_hgrn"])

    wr = jnp.concatenate([w_router, w_group], axis=1)
    wr = jnp.pad(wr, ((0, 0), (0, LANES - wr.shape[1])))
    br = jnp.pad(jnp.concatenate([b_router, b_group]), (0, LANES - N_EXPERTS - N_EXPERT_GROUPS))
    x1, h2, logits = _outproj(o_att, o_rec, w_out, x2, mod3, g_post_mix.reshape(1, d),
                               g_pre_ffn.reshape(1, d), wr, br.reshape(1, LANES), seq,
                               cfg["tm_out"])
    out = _moe(h2, logits, x1, mod3, g_post_ffn.reshape(1, d), w_gate, w_up, w_down, seq,
               cfg["tm_route"], cfg["tm_exp"], cfg["tm_fin"])
    return out.reshape(batch, seq, d)


def _config(seq):
    return dict(tm_rope=min(1024, seq), tm_in=min(1024, seq), tq_cmp=min(256, seq),
                tq=min(256, seq), tk=min(512, seq), ts_hgrn=min(256, seq), hb_hgrn=4,
                tm_out=min(256, seq), tm_route=min(256, seq), tm_exp=256,
                tm_fin=min(256, seq))


def kernel(x, c, positions, w_ada, b_ada, g_pre_mix, g_post_mix, g_pre_ffn, g_post_ffn, w_in, w_out, cmp_pe_k, cmp_w1_k, cmp_w2_k, cmp_pe_v, cmp_w1_v, cmp_w2_v, hgrn_lb_logits, hgrn_g_norm, w_group, b_group, w_router, b_router, w_gate, w_up, w_down):
    assert w_ada.shape[0] == 1, "single-layer block"
    return _block(x, c, positions, w_ada[0], b_ada[0], g_pre_mix[0], g_post_mix[0],
                  g_pre_ffn[0], g_post_ffn[0], w_in[0], w_out[0], cmp_pe_k[0], cmp_w1_k[0],
                  cmp_w2_k[0], cmp_pe_v[0], cmp_w1_v[0], cmp_w2_v[0], hgrn_lb_logits,
                  hgrn_g_norm[0], w_group[0], b_group[0], w_router[0], b_router[0],
                  w_gate[0], w_up[0], w_down[0], _config(x.shape[1]))
```

```python
import functools

import numpy as np
import jax
import jax.numpy as jnp
from jax import lax
from jax.experimental import pallas as pl
from jax.experimental.pallas import tpu as pltpu

F32 = jnp.float32
BF16 = jnp.bfloat16

HEAD_DIM = 128
N_KV_GROUPS = 2
HEADS_PER_GROUP = 4
N_ATT_HEADS = N_KV_GROUPS * HEADS_PER_GROUP
CMP_BLOCK = 32
CMP_STRIDE = 16
SEL_BLOCK = 64
N_SEL = 16
WINDOW = 512
FORCE_BONUS = 1.0e4
ROPE_THETA = 10000.0
N_HGRN_HEADS = 8
HGRN_CHUNK = 64
N_EXPERT_GROUPS = 4
EXPERTS_PER_GROUP = 8
N_EXPERTS = N_EXPERT_GROUPS * EXPERTS_PER_GROUP
EPS = 1e-6

LANES = 128
VMEM_LIMIT = 56 * 1024 * 1024

ATT_W = N_ATT_HEADS * HEAD_DIM
KV_W = N_KV_GROUPS * HEAD_DIM
COL_Q = 0
COL_KC = COL_Q + ATT_W
COL_KS = COL_KC + KV_W
COL_KW = COL_KS + KV_W
COL_VC = COL_KW + KV_W
COL_VS = COL_VC + KV_W
COL_VW = COL_VS + KV_W
COL_HQ = COL_VW + KV_W
COL_HF = COL_HQ + ATT_W
COL_HI = COL_HF + ATT_W
COL_HG = COL_HI + ATT_W
PROJ_W = COL_HG + ATT_W
ROPE_W = COL_VC

MASK_BIG = float(2 ** 30)
LOG2E = 1.4426950408889634
Q_PRESCALE = HEAD_DIM ** -0.5 * LOG2E

_NT = (((1,), (1,)), ((), ()))
_TN = (((0,), (0,)), ((), ()))


def _cparams(sem):
    return pltpu.CompilerParams(dimension_semantics=sem, vmem_limit_bytes=VMEM_LIMIT)


def _dot(a, b, **kw):
    return jnp.dot(a, b, preferred_element_type=F32, **kw)


def _dot_nt(a, b):
    return lax.dot_general(a, b, _NT, preferred_element_type=F32)


def _sigmoid(x):
    return 1.0 / (1.0 + jnp.exp(-x))


def _silu(x):
    return x * _sigmoid(x)


def _rms(x, g):
    return x * lax.rsqrt(jnp.mean(x * x, axis=-1, keepdims=True) + EPS) * g


def _lane_col(x, lane, col):
    return jnp.sum(jnp.where(lane == col, x, 0.0), axis=-1, keepdims=True)


def _ada_kernel(c_ref, w_ref, b_ref, o_ref):
    s = _silu(c_ref[...])
    o_ref[...] = _dot(s, w_ref[...], precision=lax.Precision.HIGHEST) + b_ref[...]


def _ada_mod(c8, w_ada, b_ada, tn=512):
    rows, d = c8.shape
    n = w_ada.shape[1]
    return pl.pallas_call(
        _ada_kernel,
        grid=(n // tn,),
        in_specs=[pl.BlockSpec((rows, d), lambda j: (0, 0)),
                  pl.BlockSpec((d, tn), lambda j: (0, j)),
                  pl.BlockSpec((1, tn), lambda j: (0, j))],
        out_specs=pl.BlockSpec((rows, tn), lambda j: (0, j)),
        out_shape=jax.ShapeDtypeStruct((rows, n), F32),
        compiler_params=_cparams(("parallel",)),
        name="ada_mod",
    )(c8, w_ada, b_ada.reshape(1, n))


def _rope_kernel(ang_ref, cos_ref, sin_ref):
    a = ang_ref[...]
    lane = lax.broadcasted_iota(jnp.int32, a.shape, 1)
    cos_ref[...] = jnp.cos(a)
    sin_ref[...] = jnp.where(lane < HEAD_DIM // 2, -jnp.sin(a), jnp.sin(a))


def _rope_tables(positions, tm):
    t = positions.size
    inv_freq = ROPE_THETA ** (-jnp.arange(0, HEAD_DIM, 2, dtype=F32) / HEAD_DIM)
    ang = positions.reshape(t, 1).astype(F32) * jnp.concatenate([inv_freq, inv_freq])[None, :]
    spec = pl.BlockSpec((tm, HEAD_DIM), lambda i: (i, 0))
    return pl.pallas_call(
        _rope_kernel,
        grid=(t // tm,),
        in_specs=[spec],
        out_specs=[spec, spec],
        out_shape=[jax.ShapeDtypeStruct((t, HEAD_DIM), F32)] * 2,
        compiler_params=_cparams(("parallel",)),
        name="rope_tables",
    )(ang)


def _inproj_kernel(x_ref, mod_ref, g_ref, w_ref, wg_ref, cos_ref, sin_ref,
                   proj_ref, gates_ref, h_s, *, tn):
    j = pl.program_id(1)

    @pl.when(j == 0)
    def _():
        m = mod_ref[0]
        h = _rms(x_ref[...], g_ref[...]) * (1.0 + m[1:2, :]) + m[0:1, :]
        hb = h.astype(BF16)
        h_s[...] = hb
        gates_ref[...] = _dot(hb, wg_ref[...])

    heads = tn // HEAD_DIM
    n_rope_tiles = ROPE_W // tn
    rem_heads = (ROPE_W % tn) // HEAD_DIM

    def store(n_rot, scale=None):
        acc = _dot(h_s[...], w_ref[...])
        for c in range(heads):
            a = acc[:, c * HEAD_DIM:(c + 1) * HEAD_DIM]
            if c < n_rot:
                a = a * cos_ref[...] + pltpu.roll(a, HEAD_DIM // 2, axis=1) * sin_ref[...]
            if scale is not None:
                a = a * scale
            proj_ref[:, c * HEAD_DIM:(c + 1) * HEAD_DIM] = a.astype(BF16)

    @pl.when(j < ATT_W // tn)
    def _():
        store(heads, Q_PRESCALE)

    @pl.when((j >= ATT_W // tn) & (j < n_rope_tiles))
    def _():
        store(heads)

    if rem_heads:
        @pl.when(j == n_rope_tiles)
        def _():
            store(rem_heads)

    @pl.when(j >= n_rope_tiles + (1 if rem_heads else 0))
    def _():
        store(0)


def _inproj(x2, mod3, g_pre, w_in_p, w_gates, cos_t, sin_t, seq, tm, tn=512):
    t, d = x2.shape
    per_b = seq // tm
    return pl.pallas_call(
        functools.partial(_inproj_kernel, tn=tn),
        grid=(t // tm, PROJ_W // tn),
        in_specs=[pl.BlockSpec((tm, d), lambda i, j: (i, 0)),
                  pl.BlockSpec((1, 6, d), lambda i, j: (i // per_b, 0, 0)),
                  pl.BlockSpec((1, d), lambda i, j: (0, 0)),
                  pl.BlockSpec((d, tn), lambda i, j: (0, j)),
                  pl.BlockSpec((d, LANES), lambda i, j: (0, 0)),
                  pl.BlockSpec((tm, HEAD_DIM), lambda i, j: (i, 0)),
                  pl.BlockSpec((tm, HEAD_DIM), lambda i, j: (i, 0))],
        out_specs=[pl.BlockSpec((tm, tn), lambda i, j: (i, j)),
                   pl.BlockSpec((tm, LANES), lambda i, j: (i, 0))],
        out_shape=[jax.ShapeDtypeStruct((t, PROJ_W), BF16),
                   jax.ShapeDtypeStruct((t, LANES), F32)],
        scratch_shapes=[pltpu.VMEM((tm, d), BF16)],
        compiler_params=_cparams(("parallel", "arbitrary")),
        name="inproj",
    )(x2, mod3, g_pre, w_in_p, w_gates, cos_t, sin_t)


def _compress_kernel(h_ref, w1_ref, pe_ref, w2_ref, o_ref):
    hb = h_ref[0]
    half = hb.shape[1]
    ya = _dot(hb, w1_ref[0:half, :])
    yb = _dot(hb, w1_ref[half:2 * half, :])
    const = _dot(pe_ref[...], w1_ref[...])
    n = ya.shape[0]
    yb_next = pltpu.roll(yb, n - 1, axis=0)
    hid = _silu(ya + yb_next + const[0:1, :])
    out = _dot(hid.astype(BF16), w2_ref[...])
    row = lax.broadcasted_iota(jnp.int32, out.shape, 0)
    o_ref[0] = jnp.where(row < n - 1, out, 0.0).astype(BF16)


def _compress(hblk, w1, pe, w2):
    bg, n, half = hblk.shape
    pe8 = jnp.zeros((8, 2 * half), BF16).at[0].set(pe.reshape(-1).astype(BF16))
    return pl.pallas_call(
        _compress_kernel,
        grid=(bg,),
        in_specs=[pl.BlockSpec((1, n, half), lambda i: (i, 0, 0)),
                  pl.BlockSpec((2 * half, HEAD_DIM), lambda i: (0, 0)),
                  pl.BlockSpec((8, 2 * half), lambda i: (0, 0)),
                  pl.BlockSpec((HEAD_DIM, HEAD_DIM), lambda i: (0, 0))],
        out_specs=pl.BlockSpec((1, n, HEAD_DIM), lambda i: (i, 0, 0)),
        out_shape=jax.ShapeDtypeStruct((bg, n, HEAD_DIM), BF16),
        compiler_params=_cparams(("parallel",)),
        name="compress",
    )(hblk, w1.astype(BF16), pe8, w2.astype(BF16))


def _cmp_kernel(q_ref, kc_ref, vc_ref, gates_ref, ov_ref, ocmp_ref, selb_ref, *, tq, nsel):
    g = pl.program_id(0) % N_KV_GROUPS
    q0 = pl.program_id(1) * tq
    kc = kc_ref[0]
    vc = vc_ref[0]
    nc = kc.shape[0]
    t_idx = q0 + lax.broadcasted_iota(jnp.int32, (tq, nc), 0)
    c_idx = lax.broadcasted_iota(jnp.int32, (tq, nc), 1)
    c_ok = c_idx * CMP_STRIDE + (CMP_BLOCK - 1) <= t_idx
    gl = gates_ref[...]
    lane = lax.broadcasted_iota(jnp.int32, (tq, LANES), 1)
    psum = jnp.zeros((tq, nc), F32)
    for h in range(HEADS_PER_GROUP):
        qh = q_ref[:, h * HEAD_DIM:(h + 1) * HEAD_DIM]
        s = jnp.where(c_ok, _dot_nt(qh, kc), -jnp.inf)
        m = jnp.max(s, axis=-1, keepdims=True)
        m = jnp.where(m == -jnp.inf, 0.0, m)
        p = jnp.exp2(s - m)
        p = p / jnp.maximum(jnp.sum(p, axis=-1, keepdims=True), 1e-30)
        psum = psum + p
        o = _dot(p.astype(BF16), vc)
        glog = _lane_col(gl, lane, (g * HEADS_PER_GROUP + h) * 3)
        ocmp_ref[:, h * HEAD_DIM:(h + 1) * HEAD_DIM] = o * _sigmoid(glog)

    imp = _dot(psum, ov_ref[...], precision=lax.Precision.HIGHEST)
    cur = (q0 + lax.broadcasted_iota(jnp.int32, (tq, LANES), 0)) // SEL_BLOCK
    forced = (lane == 0) | (lane == cur) | (lane == cur - 1)
    imp = jnp.where(lane <= cur, imp + jnp.where(forced, FORCE_BONUS, 0.0), -jnp.inf)
    imp_t = imp.T[0:nsel, :]
    sub = 8
    groups = [imp_t[g * sub:(g + 1) * sub, :] for g in range(nsel // sub)]
    cnts = [jnp.zeros((sub, tq), F32) for _ in groups]
    row = lax.broadcasted_iota(jnp.int32, (sub, tq), 0)
    for k in range(nsel):
        vk = imp_t[k:k + 1, :]
        for g, grp in enumerate(groups):
            if k < g * sub:
                beats = vk >= grp
            elif k >= (g + 1) * sub:
                beats = vk > grp
            else:
                beats = (vk > grp) | ((vk == grp) & (row > k - g * sub))
            cnts[g] = cnts[g] + jnp.where(beats, 1.0, 0.0)
    cnt = jnp.concatenate(cnts, axis=0)
    sel = (cnt < float(N_SEL)) & (imp_t > -jnp.inf)
    bias_t = jnp.where(sel, 0.0, -MASK_BIG)
    if nsel < LANES:
        bias_t = jnp.concatenate([bias_t, jnp.zeros((LANES - nsel, tq), F32)], axis=0)
    selb_ref[...] = bias_t.T.astype(BF16)


def _cmp_attn(proj, kc, vc, gates, batch, seq, tq):
    t = proj.shape[0]
    nq = seq // tq
    nc = kc.shape[1]
    nsel = seq // SEL_BLOCK
    assert nsel <= LANES and nsel % 8 == 0
    c_start = np.arange(nc) * CMP_STRIDE
    j_start = np.arange(LANES) * SEL_BLOCK
    ov = ((c_start[:, None] < j_start[None, :] + SEL_BLOCK)
          & (c_start[:, None] + CMP_BLOCK > j_start[None, :])
          & (np.arange(LANES)[None, :] < nsel)).astype(np.float32)
    gq = HEADS_PER_GROUP * HEAD_DIM
    qmap = lambda bg, i: ((bg // N_KV_GROUPS) * nq + i, bg % N_KV_GROUPS)
    return pl.pallas_call(
        functools.partial(_cmp_kernel, tq=tq, nsel=nsel),
        grid=(batch * N_KV_GROUPS, nq),
        in_specs=[pl.BlockSpec((tq, gq), qmap),
                  pl.BlockSpec((1, nc, HEAD_DIM), lambda bg, i: (bg, 0, 0)),
                  pl.BlockSpec((1, nc, HEAD_DIM), lambda bg, i: (bg, 0, 0)),
                  pl.BlockSpec((tq, LANES), lambda bg, i: ((bg // N_KV_GROUPS) * nq + i, 0)),
                  pl.BlockSpec((nc, LANES), lambda bg, i: (0, 0))],
        out_specs=[pl.BlockSpec((tq, gq), qmap),
                   pl.BlockSpec((tq, LANES), lambda bg, i: (bg * nq + i, 0))],
        out_shape=[jax.ShapeDtypeStruct((t, ATT_W), F32),
                   jax.ShapeDtypeStruct((batch * N_KV_GROUPS * seq, LANES), BF16)],
        compiler_params=_cparams(("parallel", "parallel")),
        name="cmp_attn",
    )(proj, kc, vc, gates, jnp.asarray(ov))


def _selwin_kernel(q_ref, selb_ref, ocmp_ref, gates_ref, ks_ref, vs_ref, kw_ref, vw_ref,
                   cb_ref, wb_ref, o_ref, kaug_s, vaug_s, vwaug_s, qaug_s, sbuf_s, m_s, acc_s,
                   swin_s, *, tq, tk, seq, nwin):
    g = pl.program_id(0) % N_KV_GROUPS
    qi = pl.program_id(1)
    q0 = qi * tq
    hq = HEADS_PER_GROUP * tq
    half = hq // 2

    @pl.when(qi == 0)
    def _():
        r = lax.broadcasted_iota(jnp.int32, (seq, HEAD_DIM), 0)
        ln = lax.broadcasted_iota(jnp.int32, (seq, HEAD_DIM), 1)
        ones_col = jnp.where(ln == 0, 1.0, 0.0).astype(BF16)
        kaug_s[:, 0:HEAD_DIM] = ks_ref[...]
        kaug_s[:, HEAD_DIM:2 * HEAD_DIM] = jnp.where(r // SEL_BLOCK == ln, 1.0, 0.0).astype(BF16)
        vaug_s[:, 0:HEAD_DIM] = vs_ref[...]
        vaug_s[:, HEAD_DIM:2 * HEAD_DIM] = ones_col
        vwaug_s[:, 0:HEAD_DIM] = vw_ref[...]
        vwaug_s[:, HEAD_DIM:2 * HEAD_DIM] = ones_col

    for h in range(HEADS_PER_GROUP):
        qaug_s[h * tq:(h + 1) * tq, 0:HEAD_DIM] = q_ref[:, h * HEAD_DIM:(h + 1) * HEAD_DIM]
        qaug_s[h * tq:(h + 1) * tq, HEAD_DIM:2 * HEAD_DIM] = selb_ref[...]
    qa = qaug_s[...]

    def scores(kt, slot):
        ks = kaug_s[pl.ds(pl.multiple_of(kt * tk, tk), tk), :]
        sbuf_s[slot] = _dot_nt(qa, ks)

    def absorb(kt, slot, diagonal=False):
        vs = vaug_s[pl.ds(pl.multiple_of(kt * tk, tk), tk), :]
        s = sbuf_s[slot]
        if diagonal:
            bias = cb_ref[(q0 - kt * tk) // tq]
            s = (s.reshape(HEADS_PER_GROUP, tq, tk) + bias[None]).reshape(hq, tk)
        m = m_s[...]
        m_new = jnp.maximum(m, jnp.max(s, axis=-1, keepdims=True))
        m_s[...] = m_new
        pb = jnp.exp2(s - m_new[:, 0:1]).astype(BF16)
        if diagonal:
            pv = jnp.concatenate([_dot(pb[0:half], vs), _dot(pb[half:hq], vs)], axis=0)
        else:
            pv = _dot(pb, vs)
        acc_s[...] = jnp.exp2(m - m_new)[:, 0:1] * acc_s[...] + pv

    n_full = q0 // tk
    m_s[...] = jnp.full(m_s.shape, -jnp.inf, F32)
    acc_s[...] = jnp.zeros(acc_s.shape, F32)
    scores(0, 0)

    def pair(j, c):
        scores(2 * j + 1, 1)
        absorb(2 * j, 0)
        scores(2 * j + 2, 0)
        absorb(2 * j + 1, 1)
        return c

    lax.fori_loop(0, n_full // 2, pair, 0)
    odd = n_full % 2

    @pl.when(odd == 1)
    def _():
        scores(n_full, 1)
        absorb(n_full - 1, 0)

    nq = seq // tq
    w0 = jnp.clip(qi - (nwin - 1), 0, nq - nwin) * tq
    wlen = nwin * tq
    kwin = kw_ref[pl.ds(pl.multiple_of(w0, tq), wlen), :]
    vwin = vwaug_s[pl.ds(pl.multiple_of(w0, tq), wlen), :]
    swin_s[0:half, :] = _dot_nt(qaug_s[0:half, 0:HEAD_DIM], kwin)
    swin_s[half:hq, :] = _dot_nt(qaug_s[half:hq, 0:HEAD_DIM], kwin)
    absorb(n_full, odd, diagonal=True)
    acc = acc_s[...]
    o_sel = acc[:, 0:HEAD_DIM] / acc[:, HEAD_DIM:HEAD_DIM + 1]

    wbias = wb_ref[(q0 - w0) // tq]
    sw = (swin_s[...].reshape(HEADS_PER_GROUP, tq, wlen) + wbias[None]).reshape(hq, wlen)
    pwb = jnp.exp2(sw - jnp.max(sw, axis=-1, keepdims=True)).astype(BF16)
    ow = jnp.concatenate([_dot(pwb[0:half], vwin), _dot(pwb[half:hq], vwin)], axis=0)
    o_win = ow[:, 0:HEAD_DIM] / ow[:, HEAD_DIM:HEAD_DIM + 1]

    gl = gates_ref[...]
    lane = lax.broadcasted_iota(jnp.int32, (tq, LANES), 1)
    for h in range(HEADS_PER_GROUP):
        gcol = (g * HEADS_PER_GROUP + h) * 3
        g_s = _sigmoid(_lane_col(gl, lane, gcol + 1))
        g_w = _sigmoid(_lane_col(gl, lane, gcol + 2))
        o = (ocmp_ref[:, h * HEAD_DIM:(h + 1) * HEAD_DIM]
             + g_s * o_sel[h * tq:(h + 1) * tq, :] + g_w * o_win[h * tq:(h + 1) * tq, :])
        o_ref[:, h * HEAD_DIM:(h + 1) * HEAD_DIM] = o.astype(BF16)


def _selwin_attn(proj, selb, ocmp, gates, batch, seq, tq, tk):
    t = proj.shape[0]
    nq = seq // tq
    assert tk % tq == 0 and tq & (tq - 1) == 0 and seq % tk == 0
    nwin = min(WINDOW // tq + 1, nq)
    assert WINDOW % tq == 0
    gq = HEADS_PER_GROUP * HEAD_DIM
    hq = HEADS_PER_GROUP * tq
    r = np.arange(tq)[:, None]
    cbias = np.stack([np.where(np.arange(tk)[None, :] <= r + off * tq, 0.0, -np.inf)
                      for off in range(tk // tq)]).astype(np.float32)
    dist = [r + off * tq - np.arange(nwin * tq)[None, :] for off in range(nwin)]
    wbias = np.stack([np.where((d >= 0) & (d < WINDOW), 0.0, -np.inf)
                      for d in dist]).astype(np.float32)
    qmap = lambda bg, i: ((bg // N_KV_GROUPS) * nq + i, bg % N_KV_GROUPS)

    def kvspec(col):
        return pl.BlockSpec((seq, HEAD_DIM),
                            lambda bg, i: (bg // N_KV_GROUPS, col // HEAD_DIM + bg % N_KV_GROUPS))

    return pl.pallas_call(
        functools.partial(_selwin_kernel, tq=tq, tk=tk, seq=seq, nwin=nwin),
        grid=(batch * N_KV_GROUPS, nq),
        in_specs=[pl.BlockSpec((tq, gq), qmap),
                  pl.BlockSpec((tq, LANES), lambda bg, i: (bg * nq + i, 0)),
                  pl.BlockSpec((tq, gq), qmap),
                  pl.BlockSpec((tq, LANES), lambda bg, i: ((bg // N_KV_GROUPS) * nq + i, 0)),
                  kvspec(COL_KS), kvspec(COL_VS), kvspec(COL_KW), kvspec(COL_VW),
                  pl.BlockSpec(cbias.shape, lambda bg, i: (0, 0, 0)),
                  pl.BlockSpec(wbias.shape, lambda bg, i: (0, 0, 0))],
        out_specs=pl.BlockSpec((tq, gq), qmap),
        out_shape=jax.ShapeDtypeStruct((t, ATT_W), BF16),
        scratch_shapes=[pltpu.VMEM((seq, 2 * HEAD_DIM), BF16),
                        pltpu.VMEM((seq, 2 * HEAD_DIM), BF16),
                        pltpu.VMEM((seq, 2 * HEAD_DIM), BF16),
                        pltpu.VMEM((hq, 2 * HEAD_DIM), BF16),
                        pltpu.VMEM((2, hq, tk), F32),
                        pltpu.VMEM((hq, LANES), F32),
                        pltpu.VMEM((hq, 2 * HEAD_DIM), F32),
                        pltpu.VMEM((hq, nwin * tq), F32)],
        compiler_params=_cparams(("parallel", "arbitrary")),
        name="selwin_attn",
    )(proj, selb, ocmp, gates, proj, proj, proj, proj, jnp.asarray(cbias), jnp.asarray(wbias))


_HGRN_LEVELS = (32, 16, 8, 4, 2, 1)


def _hgrn_tables():
    c = HGRN_CHUNK
    nl = len(_HGRN_LEVELS)
    idx = np.arange(c)
    i, j = idx[:, None], idx[None, :]
    mats = [j <= i, j > i]
    lvl = np.full((c, c), nl + 1, np.int32)
    for li, s in enumerate(_HGRN_LEVELS):
        mats.append((j >= (i // s) * s) & (j <= i))
        mats.append((j > i) & (j <= (i // s) * s + s - 1))
        lvl[((i // (2 * s)) == (j // (2 * s))) & (((i // s) % 2) == 1) & (((j // s) % 2) == 0)] = li
    lvl[i == j] = nl
    w = np.concatenate(mats, 0).astype(np.float32)
    return np.concatenate([w, w], axis=1), lvl


def _hgrn_head(hq_ref, hf_ref, hi_ref, hg_ref, lbl_ref, gn_ref, w_ref, lvl_ref, o_ref, st_s,
               hh, ts):
    c = HGRN_CHUNK
    nc = ts // c
    nl = len(_HGRN_LEVELS)
    cs = slice(hh * HEAD_DIM, (hh + 1) * HEAD_DIM)
    lbl = lbl_ref[:, cs]
    e = jnp.exp(lbl - jnp.max(lbl, axis=0, keepdims=True))
    lb = e[0:1, :] / jnp.sum(e, axis=0, keepdims=True)

    f = lb + (1.0 - lb) * _sigmoid(hf_ref[:, cs].astype(F32))
    k = 1.0 - f
    qf = _silu(hq_ref[:, cs].astype(F32))
    v = hi_ref[:, cs]

    lf = jnp.log(f) * LOG2E
    lf_hi = lf.astype(BF16)
    lf_lo = (lf - lf_hi.astype(F32)).astype(BF16)
    side = lambda a: jnp.concatenate([a[ci * c:(ci + 1) * c, :] for ci in range(nc)], axis=1)
    ex = _dot(w_ref[...], jnp.concatenate([side(lf_hi), side(lf_lo)], axis=0))

    def rows(bi):
        return jnp.concatenate(
            [ex[bi * c:(bi + 1) * c, ci * HEAD_DIM:(ci + 1) * HEAD_DIM] for ci in range(nc)], axis=0)

    b = rows(0)
    qhat = (qf * jnp.exp2(b)).astype(BF16)
    khat = (k * jnp.exp2(rows(1))).astype(BF16)
    lvl = lvl_ref[...]
    masks = [lvl == li for li in range(nl + 1)]
    qs = [(qf * jnp.exp2(rows(2 + 2 * li))).astype(BF16) for li in range(nl)] + [qf.astype(BF16)]
    ks = [(k * jnp.exp2(rows(3 + 2 * li))).astype(BF16) for li in range(nl)] + [k.astype(BF16)]
    o_intra = []
    for ci in range(nc):
        sl = slice(ci * c, (ci + 1) * c)
        a = jnp.zeros((c, c), F32)
        for li in range(nl + 1):
            a = jnp.where(masks[li], _dot_nt(qs[li][sl, :], ks[li][sl, :]), a)
        o_intra.append(_dot(a.astype(BF16), v[sl, :]))

    st = st_s[hh]
    outs = []
    for ci in range(nc):
        sl = slice(ci * c, (ci + 1) * c)
        outs.append(o_intra[ci] + _dot_nt(qhat[sl, :], st.astype(BF16)))
        st = st * jnp.exp2(b[(ci + 1) * c - 1:(ci + 1) * c, :]) + lax.dot_general(
            v[sl, :], khat[sl, :], _TN, preferred_element_type=F32)
    st_s[hh] = st
    o = jnp.concatenate(outs, axis=0)
    o = o * lax.rsqrt(jnp.mean(o * o, axis=-1, keepdims=True) + EPS) * gn_ref[:, cs]
    o_ref[:, cs] = (o * _silu(hg_ref[:, cs].astype(F32))).astype(BF16)


def _hgrn_kernel(hq_ref, hf_ref, hi_ref, hg_ref, lbl_ref, gn_ref, w_ref, lvl_ref,
                 o_ref, st_s, *, ts, hb):
    @pl.when(pl.program_id(2) == 0)
    def _():
        st_s[...] = jnp.zeros_like(st_s)

    for hh in range(hb):
        _hgrn_head(hq_ref, hf_ref, hi_ref, hg_ref, lbl_ref, gn_ref, w_ref, lvl_ref, o_ref, st_s,
                   hh, ts)


def _hgrn(proj, lb_logits, g_norm, batch, seq, ts, hb):
    t = proj.shape[0]
    ns = seq // ts
    wmat, lvl = _hgrn_tables()
    w = hb * HEAD_DIM

    def colspec(col):
        return pl.BlockSpec((ts, w), lambda b, h, s: (b * ns + s, col // w + h))

    nl = lb_logits.shape[0]
    return pl.pallas_call(
        functools.partial(_hgrn_kernel, ts=ts, hb=hb),
        grid=(batch, N_HGRN_HEADS // hb, ns),
        in_specs=[colspec(COL_HQ), colspec(COL_HF), colspec(COL_HI), colspec(COL_HG),
                  pl.BlockSpec((nl, w), lambda b, h, s: (0, h)),
                  pl.BlockSpec((1, w), lambda b, h, s: (0, h)),
                  pl.BlockSpec(wmat.shape, lambda b, h, s: (0, 0)),
                  pl.BlockSpec(lvl.shape, lambda b, h, s: (0, 0))],
        out_specs=pl.BlockSpec((ts, w), lambda b, h, s: (b * ns + s, h)),
        out_shape=jax.ShapeDtypeStruct((t, ATT_W), BF16),
        scratch_shapes=[pltpu.VMEM((hb, HEAD_DIM, HEAD_DIM), F32)],
        compiler_params=_cparams(("parallel", "parallel", "arbitrary")),
        name="hgrn2",
    )(proj, proj, proj, proj, lb_logits, g_norm, jnp.asarray(wmat, BF16), jnp.asarray(lvl))


def _outproj_kernel(oa_ref, or_ref, wa_ref, wb_ref, x_ref, mod_ref, gpost_ref, gpre_ref,
                    wrh_ref, wrl_ref, br_ref, x1_ref, h2_ref, lg_ref, stage):
    m = mod_ref[0]
    y = _dot(oa_ref[...], wa_ref[...]) + _dot(or_ref[...], wb_ref[...])
    x1 = x_ref[...] + m[2:3, :] * _rms(y, gpost_ref[...])
    x1_ref[...] = x1
    h = _rms(x1, gpre_ref[...]) * (1.0 + m[4:5, :]) + m[3:4, :]
    h2_ref[...] = _to_slabs(stage, h)
    h_hi = h.astype(BF16)
    h_lo = (h - h_hi.astype(F32)).astype(BF16)
    lg_ref[...] = (_dot(h_hi, wrh_ref[...]) + _dot(h_lo, wrh_ref[...])
                   + _dot(h_hi, wrl_ref[...]) + br_ref[...])


def _outproj(o_att, o_rec, w_out, x2, mod3, g_post, g_pre, wr, br, seq, tm):
    t, d = x2.shape
    per_b = seq // tm
    wr_hi = wr.astype(BF16)
    wr_lo = (wr - wr_hi.astype(F32)).astype(BF16)
    wa = w_out[:ATT_W].astype(BF16)
    wb = w_out[ATT_W:].astype(BF16)
    row = lambda i: (i, 0)
    fixed = lambda i: (0, 0)
    return pl.pallas_call(
        _outproj_kernel,
        grid=(t // tm,),
        in_specs=[pl.BlockSpec((tm, ATT_W), row), pl.BlockSpec((tm, ATT_W), row),
                  pl.BlockSpec((ATT_W, d), fixed), pl.BlockSpec((ATT_W, d), fixed),
                  pl.BlockSpec((tm, d), row),
                  pl.BlockSpec((1, 6, d), lambda i: (i // per_b, 0, 0)),
                  pl.BlockSpec((1, d), fixed), pl.BlockSpec((1, d), fixed),
                  pl.BlockSpec((d, LANES), fixed), pl.BlockSpec((d, LANES), fixed),
                  pl.BlockSpec((1, LANES), fixed)],
        out_specs=[pl.BlockSpec((tm, d), row), pl.BlockSpec((tm * (d // LANES), LANES), row),
                   pl.BlockSpec((tm, LANES), row)],
        out_shape=[jax.ShapeDtypeStruct((t, d), F32),
                   jax.ShapeDtypeStruct((t * (d // LANES), LANES), BF16),
                   jax.ShapeDtypeStruct((t, LANES), F32)],
        scratch_shapes=[pltpu.VMEM((tm * (d // LANES), LANES), F32)],
        compiler_params=_cparams(("parallel",)),
        name="outproj",
    )(o_att, o_rec, wa, wb, x2, mod3, g_post, g_pre, wr_hi, wr_lo, br)


G_LANE0 = N_EXPERTS


def _route_kernel(lg_ref, tri_ref, info_ref, cnt_ref, carry_s):
    @pl.when(pl.program_id(0) == 0)
    def _():
        carry_s[...] = jnp.zeros_like(carry_s)

    x = lg_ref[...]
    lane = lax.broadcasted_iota(jnp.int32, x.shape, 1)
    big = jnp.int32(10 ** 6)
    rmax = lambda a: jnp.max(a, axis=-1, keepdims=True)
    rmin = lambda a: jnp.min(a, axis=-1, keepdims=True)
    rsum = lambda a: jnp.sum(a, axis=-1, keepdims=True)

    is_g = (lane >= G_LANE0) & (lane < G_LANE0 + N_EXPERT_GROUPS)
    lgm = jnp.where(is_g, x, -jnp.inf)
    mg = rmax(lgm)
    pg_sel = 1.0 / rsum(jnp.where(is_g, jnp.exp(lgm - mg), 0.0))
    gsel = rmin(jnp.where(lgm == mg, lane, big)) - G_LANE0

    is_e = (lane >= gsel * EXPERTS_PER_GROUP) & (lane < (gsel + 1) * EXPERTS_PER_GROUP)
    lem = jnp.where(is_e, x, -jnp.inf)
    pe = jnp.where(is_e, jnp.exp(lem - rmax(lem)), 0.0)
    pe = pe / rsum(pe)
    pe = jnp.where(is_e, pe, -1.0)
    v1 = rmax(pe)
    i1 = rmin(jnp.where(pe == v1, lane, big))
    pe2 = jnp.where(lane == i1, -1.0, pe)
    v2 = rmax(pe2)
    i2 = rmin(jnp.where(pe2 == v2, lane, big))
    w1 = v1 / (v1 + v2) * pg_sel
    w2 = v2 / (v1 + v2) * pg_sel

    oh1 = jnp.where(lane == i1, 1.0, 0.0)
    oh2 = jnp.where(lane == i2, 1.0, 0.0)
    both = oh1 + oh2
    before = _dot(tri_ref[...], both.astype(BF16)) + carry_s[0:1, :]
    r1 = rsum(oh1 * before)
    r2 = rsum(oh2 * before)
    carry_s[0:1, :] = carry_s[0:1, :] + jnp.sum(both, axis=0, keepdims=True)
    cnt_ref[...] = carry_s[...]

    cols = (i1.astype(F32), i2.astype(F32), w1, w2, r1, r2)
    info = jnp.zeros(x.shape, F32)
    for ci, col in enumerate(cols):
        info = jnp.where(lane == ci, col, info)
    info_ref[...] = info


def _route(logits, tm):
    t = logits.shape[0]
    tri = jnp.asarray(np.tril(np.ones((tm, tm), np.float32), -1), BF16)
    return pl.pallas_call(
        _route_kernel,
        grid=(t // tm,),
        in_specs=[pl.BlockSpec((tm, LANES), lambda i: (i, 0)),
                  pl.BlockSpec((tm, tm), lambda i: (0, 0))],
        out_specs=[pl.BlockSpec((tm, LANES), lambda i: (i, 0)),
                   pl.BlockSpec((8, LANES), lambda i: (0, 0))],
        out_shape=[jax.ShapeDtypeStruct((t, LANES), F32),
                   jax.ShapeDtypeStruct((8, LANES), F32)],
        scratch_shapes=[pltpu.VMEM((8, LANES), F32)],
        compiler_params=_cparams(("arbitrary",)),
        name="route",
    )(logits, tri)


def _to_slabs(stage, x):
    n, d = x.shape
    slab = d // LANES
    for c in range(slab):
        stage[pl.ds(c, n, stride=slab), :] = x[:, c * LANES:(c + 1) * LANES]
    return stage[...].astype(BF16)


def _from_slabs(stage, slabs, n):
    slab = slabs.shape[0] // n
    stage[...] = slabs.astype(F32)
    return jnp.concatenate([stage[pl.ds(c, n, stride=slab), :] for c in range(slab)], axis=1)


def _expert_kernel(te_ref, nu_ref, src0_ref, srcn_ref, dst_ref, h_ref, wg_ref, wu_ref, wd_ref,
                   y_ref, xbuf, obuf, stage, gsem, ssem, wg_s, wu_s, wd_s, *, tm, standin_row):
    t = pl.program_id(0)
    nu = nu_ref[0]
    slot = t % 2
    slab = wg_s.shape[0] // LANES
    tile = tm * slab

    def token(ref, i):
        return ref.at[pl.ds(pl.multiple_of(i * slab, slab), slab)]

    def gather(src_ref, s):
        for r in range(tm):
            pltpu.make_async_copy(token(h_ref, src_ref[0, 0, r]), token(xbuf, s * tm + r),
                                  gsem.at[s]).start(priority=1)

    def scatter(s, row_of):
        for r in range(tm):
            pltpu.make_async_copy(token(obuf, s * tm + r), token(y_ref, row_of(r)),
                                  ssem.at[s]).start(priority=1)

    def tile_of(buf, s):
        return buf.at[pl.ds(pl.multiple_of(s * tile, tile), tile)]

    def wait_tile(buf, sem, s):
        pltpu.make_async_copy(tile_of(buf, s), tile_of(buf, s), sem.at[s]).wait()

    @pl.when(t < nu)
    def _():
        @pl.when(t == 0)
        def _():
            gather(src0_ref, 0)
            obuf[tile:2 * tile, :] = jnp.zeros((tile, LANES), obuf.dtype)

            def fill(row0, s):
                return pltpu.make_async_copy(tile_of(obuf, 1),
                                             y_ref.at[pl.ds(row0 * slab, tile)], ssem.at[s])

            fill(standin_row - 2 * tm, 0).start()
            fill(standin_row - tm, 0).start()
            fill(standin_row - 2 * tm, 0).wait()
            fill(standin_row - tm, 0).wait()
            fill(standin_row, 1).start()

        @pl.when((t == 0) | (te_ref[t] != te_ref[jnp.maximum(t - 1, 0)]))
        def _():
            wg_s[...] = wg_ref[0].astype(BF16)
            wu_s[...] = wu_ref[0].astype(BF16)
            wd_s[...] = wd_ref[0].astype(BF16)

        gather(srcn_ref, 1 - slot)
        wait_tile(xbuf, gsem, slot)
        row0 = pl.multiple_of(slot * tile, tile)
        x = _from_slabs(stage, xbuf[pl.ds(row0, tile), :], tm).astype(BF16)
        hid = (_silu(_dot(x, wg_s[...])) * _dot(x, wu_s[...])).astype(BF16)
        obuf[pl.ds(row0, tile), :] = _to_slabs(stage, _dot(hid, wd_s[...]))
        scatter(slot, lambda r: dst_ref[0, 0, r])
        wait_tile(obuf, ssem, 1 - slot)

        @pl.when(t == nu - 1)
        def _():
            wait_tile(obuf, ssem, slot)
            wait_tile(xbuf, gsem, 1 - slot)


def _experts(tile_expert, n_used, src3, dst3, h2, w_gate, w_up, w_down, y_rows, tm,
             standin_row):
    n_tiles = src3.shape[0]
    d, f = w_gate.shape[1:]
    slab = d // LANES
    wmap = lambda t, te, nu: (te[t], 0, 0)
    smem = lambda imap: pl.BlockSpec((1, 1, tm), imap, memory_space=pltpu.SMEM)
    return pl.pallas_call(
        functools.partial(_expert_kernel, tm=tm, standin_row=standin_row),
        grid_spec=pltpu.PrefetchScalarGridSpec(
            num_scalar_prefetch=2,
            grid=(n_tiles,),
            in_specs=[smem(lambda t, te, nu: (0, 0, 0)),
                      smem(lambda t, te, nu: (jnp.minimum(t + 1, nu[0] - 1), 0, 0)),
                      smem(lambda t, te, nu: (t, 0, 0)),
                      pl.BlockSpec(memory_space=pl.ANY),
                      pl.BlockSpec((1, d, f), wmap),
                      pl.BlockSpec((1, d, f), wmap),
                      pl.BlockSpec((1, f, d), wmap)],
            out_specs=pl.BlockSpec(memory_space=pl.ANY),
            scratch_shapes=[pltpu.VMEM((2 * tm * slab, LANES), BF16),
                            pltpu.VMEM((2 * tm * slab, LANES), BF16),
                            pltpu.VMEM((tm * slab, LANES), F32),
                            pltpu.SemaphoreType.DMA((2,)), pltpu.SemaphoreType.DMA((2,)),
                            pltpu.VMEM((d, f), BF16), pltpu.VMEM((d, f), BF16),
                            pltpu.VMEM((f, d), BF16)]),
        out_shape=jax.ShapeDtypeStruct((y_rows * slab, LANES), BF16),
        compiler_params=_cparams(("arbitrary",)),
        name="moe_experts",
    )(tile_expert, n_used, src3, src3, dst3, h2, w_gate, w_up, w_down)


def _final_kernel(info_ref, x1_ref, mod_ref, g_ref, ya_ref, yb_ref, o_ref, stage_a, stage_b):
    info = info_ref[...]
    tm = x1_ref.shape[0]
    ya = _from_slabs(stage_a, ya_ref[...], tm)
    yb = _from_slabs(stage_b, yb_ref[...], tm)
    y = info[:, 2:3] * ya + info[:, 3:4] * yb
    o_ref[...] = x1_ref[...] + mod_ref[0][5:6, :] * _rms(y, g_ref[...])


def _moe_final(info, x1, mod3, g_post, y2, seq, tm):
    t, d = x1.shape
    per_b = seq // tm
    nb = t // tm
    return pl.pallas_call(
        _final_kernel,
        grid=(nb,),
        in_specs=[pl.BlockSpec((tm, LANES), lambda i: (i, 0)),
                  pl.BlockSpec((tm, d), lambda i: (i, 0)),
                  pl.BlockSpec((1, 6, d), lambda i: (i // per_b, 0, 0)),
                  pl.BlockSpec((1, d), lambda i: (0, 0)),
                  pl.BlockSpec((tm * (d // LANES), LANES), lambda i: (i, 0)),
                  pl.BlockSpec((tm * (d // LANES), LANES), lambda i: (nb + i, 0))],
        out_specs=pl.BlockSpec((tm, d), lambda i: (i, 0)),
        out_shape=jax.ShapeDtypeStruct((t, d), F32),
        scratch_shapes=[pltpu.VMEM((tm * (d // LANES), LANES), F32)] * 2,
        compiler_params=_cparams(("parallel",)),
        name="moe_final",
    )(info, x1, mod3, g_post, y2, y2)


def _slot_map_kernel(pos_ref, src_in, dst_in, src_ref, dst_ref, sem, *, tm, n_tokens):
    @pl.when(pl.program_id(0) == 0)
    def _():
        c_src = pltpu.make_async_copy(src_in, src_ref, sem.at[0])
        c_dst = pltpu.make_async_copy(dst_in, dst_ref, sem.at[1])
        c_src.start()
        c_dst.start()
        c_src.wait()
        c_dst.wait()

    base = pl.program_id(0) * tm

    def body(i, c):
        tok = base + i
        p0 = pos_ref[0, 0, 2 * i]
        p1 = pos_ref[0, 0, 2 * i + 1]
        src_ref[p0] = tok
        src_ref[p1] = tok
        dst_ref[p0] = tok
        dst_ref[p1] = n_tokens + tok
        return c

    lax.fori_loop(0, tm, body, 0, unroll=8)


def _slot_map(pos, src0, dst0, tm):
    t = pos.shape[0]
    whole = pl.BlockSpec(memory_space=pltpu.SMEM)
    return pl.pallas_call(
        functools.partial(_slot_map_kernel, tm=tm, n_tokens=t),
        grid=(t // tm,),
        in_specs=[pl.BlockSpec((1, 1, 2 * tm), lambda i: (i, 0, 0), memory_space=pltpu.SMEM),
                  pl.BlockSpec(memory_space=pl.ANY), pl.BlockSpec(memory_space=pl.ANY)],
        out_specs=[whole, whole],
        out_shape=[jax.ShapeDtypeStruct(src0.shape, jnp.int32)] * 2,
        scratch_shapes=[pltpu.SemaphoreType.DMA((2,))],
        compiler_params=_cparams(("arbitrary",)),
        name="slot_map",
    )(pos.reshape(t // tm, 1, 2 * tm), src0, dst0)


def _moe(h2, logits, x1, mod3, g_post, w_gate, w_up, w_down, seq, tm_route, tm_exp, tm_fin):
    t = x1.shape[0]
    info, cnt = _route(logits, tm_route)
    counts = cnt[0, :N_EXPERTS].astype(jnp.int32)
    tiles_e = (counts + tm_exp - 1) // tm_exp
    tile_end = jnp.cumsum(tiles_e)
    offs = (tile_end - tiles_e) * tm_exp
    n_used = tile_end[-1]
    e12 = info[:, 0:2].astype(jnp.int32)
    pos = offs[e12] + info[:, 4:6].astype(jnp.int32)
    n_tiles = (2 * t) // tm_exp + N_EXPERTS
    tidx = jnp.minimum(jnp.arange(n_tiles, dtype=jnp.int32), n_used - 1)
    tile_expert = jnp.sum(tidx[:, None] >= tile_end[None, :], axis=1).astype(jnp.int32)
    slot_ids = jnp.arange(n_tiles * tm_exp, dtype=jnp.int32)
    spare = 2 * t + ((slot_ids // tm_exp) % 2) * tm_exp + slot_ids % tm_exp
    src, dst = _slot_map(pos, jnp.zeros_like(slot_ids), spare, tm_route)

    ew = w_gate.shape
    wg = w_gate.reshape(N_EXPERTS, ew[-2], ew[-1])
    wu = w_up.reshape(N_EXPERTS, ew[-2], ew[-1])
    wd = w_down.reshape(N_EXPERTS, ew[-1], ew[-2])
    y2 = _experts(tile_expert, n_used.reshape(1).astype(jnp.int32),
                  src.reshape(n_tiles, 1, tm_exp), dst.reshape(n_tiles, 1, tm_exp),
                  h2, wg, wu, wd, 2 * t + 3 * tm_exp, tm_exp, 2 * t + 2 * tm_exp)
    return _moe_final(info, x1, mod3, g_post, y2, seq, tm_fin)


def _permute_w_in(w_in):
    sizes = [ATT_W] + [KV_W] * 6 + [3 * N_ATT_HEADS] + [ATT_W] * 4
    cuts = np.cumsum(sizes)[:-1]
    q, kc, vc, ks, vs, kw, vw, gates, hq, hf, hi, hg = jnp.split(w_in.astype(BF16), cuts, axis=1)
    w_p = jnp.concatenate([q, kc, ks, kw, vc, vs, vw, hq, hf, hi, hg], axis=1)
    w_g = jnp.pad(gates, ((0, 0), (0, LANES - gates.shape[1])))
    return w_p, w_g


def _half_blocks(proj, col, batch, seq):
    a = proj[:, col:col + KV_W].reshape(batch, seq, N_KV_GROUPS, HEAD_DIM)
    a = a.transpose(0, 2, 1, 3)
    return a.reshape(batch * N_KV_GROUPS, seq // CMP_STRIDE, CMP_STRIDE * HEAD_DIM)


def _block(x, c, positions, w_ada, b_ada, g_pre_mix, g_post_mix, g_pre_ffn, g_post_ffn,
           w_in, w_out, pe_k, w1_k, w2_k, pe_v, w1_v, w2_v, lb_logits, g_norm,
           w_group, b_group, w_router, b_router, w_gate, w_up, w_down, cfg):
    batch, seq, d = x.shape
    t = batch * seq
    x2 = x.reshape(t, d)

    c8 = jnp.zeros((8, d), F32).at[:batch].set(c)
    mod3 = _ada_mod(c8, w_ada, b_ada)[:batch].reshape(batch, 6, d)
    cos_t, sin_t = _rope_tables(positions, cfg["tm_rope"])
    w_p, w_g = _permute_w_in(w_in)
    proj, gates = _inproj(x2, mod3, g_pre_mix.reshape(1, d), w_p, w_g, cos_t, sin_t,
                          seq, cfg["tm_in"])

    kc = _compress(_half_blocks(proj, COL_KC, batch, seq), w1_k, pe_k, w2_k)
    vc = _compress(_half_blocks(proj, COL_VC, batch, seq), w1_v, pe_v, w2_v)
    ocmp, selb = _cmp_attn(proj, kc, vc, gates, batch, seq, cfg["tq_cmp"])
    o_att = _selwin_attn(proj, selb, ocmp, gates, batch, seq, cfg["tq"], cfg["tk"])
    o_rec = _hgrn(proj, lb_logits, g_norm.reshape(1, -1), batch, seq, cfg["ts_hgrn"],
                  cfg["hb_hgrn"])

    wr = jnp.concatenate([w_router, w_group], axis=1)
    wr = jnp.pad(wr, ((0, 0), (0, LANES - wr.shape[1])))
    br = jnp.pad(jnp.concatenate([b_router, b_group]), (0, LANES - N_EXPERTS - N_EXPERT_GROUPS))
    x1, h2, logits = _outproj(o_att, o_rec, w_out, x2, mod3, g_post_mix.reshape(1, d),
                               g_pre_ffn.reshape(1, d), wr, br.reshape(1, LANES), seq,
                               cfg["tm_out"])
    out = _moe(h2, logits, x1, mod3, g_post_ffn.reshape(1, d), w_gate, w_up, w_down, seq,
               cfg["tm_route"], cfg["tm_exp"], cfg["tm_fin"])
    return out.reshape(batch, seq, d)


def _config(seq):
    return dict(tm_rope=min(1024, seq), tm_in=min(1024, seq), tq_cmp=min(256, seq),
                tq=min(256, seq), tk=min(512, seq), ts_hgrn=min(256, seq), hb_hgrn=4,
                tm_out=min(256, seq), tm_route=min(256, seq), tm_exp=256,
                tm_fin=min(256, seq))


def kernel(x, c, positions, w_ada, b_ada, g_pre_mix, g_post_mix, g_pre_ffn, g_post_ffn, w_in, w_out, cmp_pe_k, cmp_w1_k, cmp_w2_k, cmp_pe_v, cmp_w1_v, cmp_w2_v, hgrn_lb_logits, hgrn_g_norm, w_group, b_group, w_router, b_router, w_gate, w_up, w_down):
    assert w_ada.shape[0] == 1, "single-layer block"
    return _block(x, c, positions, w_ada[0], b_ada[0], g_pre_mix[0], g_post_mix[0],
                  g_pre_ffn[0], g_post_ffn[0], w_in[0], w_out[0], cmp_pe_k[0], cmp_w1_k[0],
                  cmp_w2_k[0], cmp_pe_v[0], cmp_w1_v[0], cmp_w2_v[0], hgrn_lb_logits,
                  hgrn_g_norm[0], w_group[0], b_group[0], w_router[0], b_router[0],
                  w_gate[0], w_up[0], w_down[0], _config(x.shape[1]))
```

```python
import functools

import numpy as np
import jax
import jax.numpy as jnp
from jax import lax
from jax.experimental import pallas as pl
from jax.experimental.pallas import tpu as pltpu

F32 = jnp.float32
BF16 = jnp.bfloat16

HEAD_DIM = 128
N_KV_GROUPS = 2
HEADS_PER_GROUP = 4
N_ATT_HEADS = N_KV_GROUPS * HEADS_PER_GROUP
CMP_BLOCK = 32
CMP_STRIDE = 16
SEL_BLOCK = 64
N_SEL = 16
WINDOW = 512
FORCE_BONUS = 1.0e4
ROPE_THETA = 10000.0
N_HGRN_HEADS = 8
HGRN_CHUNK = 64
N_EXPERT_GROUPS = 4
EXPERTS_PER_GROUP = 8
N_EXPERTS = N_EXPERT_GROUPS * EXPERTS_PER_GROUP
EPS = 1e-6

LANES = 128
VMEM_LIMIT = 56 * 1024 * 1024

ATT_W = N_ATT_HEADS * HEAD_DIM
KV_W = N_KV_GROUPS * HEAD_DIM
COL_Q = 0
COL_KC = COL_Q + ATT_W
COL_KS = COL_KC + KV_W
COL_KW = COL_KS + KV_W
COL_VC = COL_KW + KV_W
COL_VS = COL_VC + KV_W
COL_VW = COL_VS + KV_W
COL_HQ = COL_VW + KV_W
COL_HF = COL_HQ + ATT_W
COL_HI = COL_HF + ATT_W
COL_HG = COL_HI + ATT_W
PROJ_W = COL_HG + ATT_W
ROPE_W = COL_VC

MASK_BIG = float(2 ** 30)
LOG2E = 1.4426950408889634
Q_PRESCALE = HEAD_DIM ** -0.5 * LOG2E

_NT = (((1,), (1,)), ((), ()))
_TN = (((0,), (0,)), ((), ()))


def _cparams(sem):
    return pltpu.CompilerParams(dimension_semantics=sem, vmem_limit_bytes=VMEM_LIMIT)


def _dot(a, b, **kw):
    return jnp.dot(a, b, preferred_element_type=F32, **kw)


def _dot_nt(a, b):
    return lax.dot_general(a, b, _NT, preferred_element_type=F32)


def _sigmoid(x):
    return 1.0 / (1.0 + jnp.exp(-x))


def _silu(x):
    return x * _sigmoid(x)


def _rms(x, g):
    return x * lax.rsqrt(jnp.mean(x * x, axis=-1, keepdims=True) + EPS) * g


def _mod_row(mod_ref, b, k):
    d = mod_ref.shape[1] // 6
    return mod_ref[pl.ds(b, 1), k * d:(k + 1) * d]


def _lane_col(x, lane, col):
    return jnp.sum(jnp.where(lane == col, x, 0.0), axis=-1, keepdims=True)


def _ada_kernel(c_ref, w_ref, b_ref, o_ref):
    s = _silu(c_ref[...])
    o_ref[...] = _dot(s, w_ref[...], precision=lax.Precision.HIGHEST) + b_ref[...]


def _ada_mod(c8, w_ada, b_ada, tn=512):
    rows, d = c8.shape
    n = w_ada.shape[1]
    return pl.pallas_call(
        _ada_kernel,
        grid=(n // tn,),
        in_specs=[pl.BlockSpec((rows, d), lambda j: (0, 0)),
                  pl.BlockSpec((d, tn), lambda j: (0, j)),
                  pl.BlockSpec((1, tn), lambda j: (0, j))],
        out_specs=pl.BlockSpec((rows, tn), lambda j: (0, j)),
        out_shape=jax.ShapeDtypeStruct((rows, n), F32),
        compiler_params=_cparams(("parallel",)),
        name="ada_mod",
    )(c8, w_ada, b_ada.reshape(1, n))


def _rope_kernel(ang_ref, cos_ref, sin_ref):
    a = ang_ref[...]
    lane = lax.broadcasted_iota(jnp.int32, a.shape, 1)
    cos_ref[...] = jnp.cos(a)
    sin_ref[...] = jnp.where(lane < HEAD_DIM // 2, -jnp.sin(a), jnp.sin(a))


def _rope_tables(positions, tm):
    t = positions.size
    inv_freq = ROPE_THETA ** (-jnp.arange(0, HEAD_DIM, 2, dtype=F32) / HEAD_DIM)
    ang = positions.reshape(t, 1).astype(F32) * jnp.concatenate([inv_freq, inv_freq])[None, :]
    spec = pl.BlockSpec((tm, HEAD_DIM), lambda i: (i, 0))
    return pl.pallas_call(
        _rope_kernel,
        grid=(t // tm,),
        in_specs=[spec],
        out_specs=[spec, spec],
        out_shape=[jax.ShapeDtypeStruct((t, HEAD_DIM), F32)] * 2,
        compiler_params=_cparams(("parallel",)),
        name="rope_tables",
    )(ang)


def _inproj_kernel(x_ref, mod_ref, g_ref, w_ref, wg_ref, cos_ref, sin_ref,
                   proj_ref, gates_ref, hbk_ref, hbv_ref, h_s, stage, *, tn, per_b):
    j = pl.program_id(1)

    @pl.when(j == 0)
    def _():
        b = pl.program_id(0) // per_b
        h = (_rms(x_ref[...], g_ref[...]) * (1.0 + _mod_row(mod_ref, b, 1))
             + _mod_row(mod_ref, b, 0))
        hb = h.astype(BF16)
        h_s[...] = hb
        gates_ref[...] = _dot(hb, wg_ref[...])

    heads = tn // HEAD_DIM
    rows = x_ref.shape[0]

    def store(jt):
        acc = _dot(h_s[...], w_ref[...])
        for c in range(heads):
            a = acc[:, c * HEAD_DIM:(c + 1) * HEAD_DIM]
            col = None if jt is None else jt * tn + c * HEAD_DIM
            if col is not None and col < ROPE_W:
                a = a * cos_ref[...] + pltpu.roll(a, HEAD_DIM // 2, axis=1) * sin_ref[...]
            if col is not None and col < ATT_W:
                a = a * Q_PRESCALE
            proj_ref[:, c * HEAD_DIM:(c + 1) * HEAD_DIM] = a.astype(BF16)
            for col0, hb_ref in ((COL_KC, hbk_ref), (COL_VC, hbv_ref)):
                if col is not None and col0 <= col < col0 + KV_W:
                    stage[...] = a
                    hb_ref[0, (col - col0) // HEAD_DIM] = jnp.concatenate(
                        [stage[pl.ds(i, rows // CMP_STRIDE, stride=CMP_STRIDE), :]
                         for i in range(CMP_STRIDE)], axis=1).astype(BF16)

    n_special = -(-COL_VS // tn)
    for jt in range(n_special):
        pl.when(j == jt)(functools.partial(store, jt))
    pl.when(j >= n_special)(functools.partial(store, None))


def _inproj(x2, mod, g_pre, w_in_p, w_gates, cos_t, sin_t, seq, tm, tn=512):
    t, d = x2.shape
    per_b = seq // tm
    assert COL_KC % tn + KV_W <= tn and COL_VC % tn + KV_W <= tn and tm % (16 * CMP_STRIDE) == 0
    hb_spec = pl.BlockSpec((1, N_KV_GROUPS, tm // CMP_STRIDE, CMP_STRIDE * HEAD_DIM),
                           lambda i, j: (i // per_b, 0, i % per_b, 0))
    hb_shape = jax.ShapeDtypeStruct(
        (t // seq, N_KV_GROUPS, seq // CMP_STRIDE, CMP_STRIDE * HEAD_DIM), BF16)
    return pl.pallas_call(
        functools.partial(_inproj_kernel, tn=tn, per_b=per_b),
        grid=(t // tm, PROJ_W // tn),
        in_specs=[pl.BlockSpec((tm, d), lambda i, j: (i, 0)),
                  pl.BlockSpec(mod.shape, lambda i, j: (0, 0)),
                  pl.BlockSpec((1, d), lambda i, j: (0, 0)),
                  pl.BlockSpec((d, tn), lambda i, j: (0, j)),
                  pl.BlockSpec((d, LANES), lambda i, j: (0, 0)),
                  pl.BlockSpec((tm, HEAD_DIM), lambda i, j: (i, 0)),
                  pl.BlockSpec((tm, HEAD_DIM), lambda i, j: (i, 0))],
        out_specs=[pl.BlockSpec((tm, tn), lambda i, j: (i, j)),
                   pl.BlockSpec((tm, LANES), lambda i, j: (i, 0)), hb_spec, hb_spec],
        out_shape=[jax.ShapeDtypeStruct((t, PROJ_W), BF16),
                   jax.ShapeDtypeStruct((t, LANES), F32), hb_shape, hb_shape],
        scratch_shapes=[pltpu.VMEM((tm, d), BF16), pltpu.VMEM((tm, HEAD_DIM), F32)],
        compiler_params=_cparams(("parallel", "arbitrary")),
        name="inproj",
    )(x2, mod, g_pre, w_in_p, w_gates, cos_t, sin_t)


def _compress_kernel(h_ref, w1_ref, pe_ref, w2_ref, o_ref):
    hb = h_ref[0]
    half = hb.shape[1]
    ya = _dot(hb, w1_ref[0:half, :])
    yb = _dot(hb, w1_ref[half:2 * half, :])
    const = _dot(pe_ref[...], w1_ref[...])
    n = ya.shape[0]
    yb_next = pltpu.roll(yb, n - 1, axis=0)
    hid = _silu(ya + yb_next + const[0:1, :])
    out = _dot(hid.astype(BF16), w2_ref[...])
    row = lax.broadcasted_iota(jnp.int32, out.shape, 0)
    o_ref[0] = jnp.where(row < n - 1, out, 0.0).astype(BF16)


def _compress(hblk, w1, pe, w2):
    bg, n, half = hblk.shape
    pe8 = jnp.zeros((8, 2 * half), BF16).at[0].set(pe.reshape(-1).astype(BF16))
    return pl.pallas_call(
        _compress_kernel,
        grid=(bg,),
        in_specs=[pl.BlockSpec((1, n, half), lambda i: (i, 0, 0)),
                  pl.BlockSpec((2 * half, HEAD_DIM), lambda i: (0, 0)),
                  pl.BlockSpec((8, 2 * half), lambda i: (0, 0)),
                  pl.BlockSpec((HEAD_DIM, HEAD_DIM), lambda i: (0, 0))],
        out_specs=pl.BlockSpec((1, n, HEAD_DIM), lambda i: (i, 0, 0)),
        out_shape=jax.ShapeDtypeStruct((bg, n, HEAD_DIM), BF16),
        compiler_params=_cparams(("parallel",)),
        name="compress",
    )(hblk, w1.astype(BF16), pe8, w2.astype(BF16))


def _cmp_kernel(q_ref, kc_ref, vc_ref, gates_ref, ov_ref, ocmp_ref, selb_ref, *, tq, nsel):
    g = pl.program_id(0) % N_KV_GROUPS
    q0 = pl.program_id(1) * tq
    kc = kc_ref[0]
    vc = vc_ref[0]
    nc = kc.shape[0]
    t_idx = q0 + lax.broadcasted_iota(jnp.int32, (tq, nc), 0)
    c_idx = lax.broadcasted_iota(jnp.int32, (tq, nc), 1)
    c_ok = c_idx * CMP_STRIDE + (CMP_BLOCK - 1) <= t_idx
    gl = gates_ref[...]
    lane = lax.broadcasted_iota(jnp.int32, (tq, LANES), 1)
    psum = jnp.zeros((tq, nc), F32)
    for h in range(HEADS_PER_GROUP):
        qh = q_ref[:, h * HEAD_DIM:(h + 1) * HEAD_DIM]
        s = jnp.where(c_ok, _dot_nt(qh, kc), -jnp.inf)
        m = jnp.max(s, axis=-1, keepdims=True)
        m = jnp.where(m == -jnp.inf, 0.0, m)
        p = jnp.exp2(s - m)
        p = p / jnp.maximum(jnp.sum(p, axis=-1, keepdims=True), 1e-30)
        psum = psum + p
        o = _dot(p.astype(BF16), vc)
        glog = _lane_col(gl, lane, (g * HEADS_PER_GROUP + h) * 3)
        ocmp_ref[:, h * HEAD_DIM:(h + 1) * HEAD_DIM] = (o * _sigmoid(glog)).astype(BF16)

    imp = _dot(psum, ov_ref[...], precision=lax.Precision.HIGHEST)
    cur = (q0 + lax.broadcasted_iota(jnp.int32, (tq, LANES), 0)) // SEL_BLOCK
    forced = (lane == 0) | (lane == cur) | (lane == cur - 1)
    imp = jnp.where(lane <= cur, imp + jnp.where(forced, FORCE_BONUS, 0.0), -jnp.inf)
    imp_t = imp.T[0:nsel, :]
    sub = 8
    groups = [imp_t[g * sub:(g + 1) * sub, :] for g in range(nsel // sub)]
    cnts = [jnp.zeros((sub, tq), F32) for _ in groups]
    row = lax.broadcasted_iota(jnp.int32, (sub, tq), 0)
    for k in range(nsel):
        vk = imp_t[k:k + 1, :]
        for g, grp in enumerate(groups):
            if k < g * sub:
                beats = vk >= grp
            elif k >= (g + 1) * sub:
                beats = vk > grp
            else:
                beats = (vk > grp) | ((vk == grp) & (row > k - g * sub))
            cnts[g] = cnts[g] + jnp.where(beats, 1.0, 0.0)
    cnt = jnp.concatenate(cnts, axis=0)
    sel = (cnt < float(N_SEL)) & (imp_t > -jnp.inf)
    bias_t = jnp.where(sel, 0.0, -MASK_BIG)
    if nsel < LANES:
        bias_t = jnp.concatenate([bias_t, jnp.zeros((LANES - nsel, tq), F32)], axis=0)
    selb_ref[...] = bias_t.T.astype(BF16)


def _cmp_attn(proj, kc, vc, gates, batch, seq, tq):
    t = proj.shape[0]
    nq = seq // tq
    nc = kc.shape[1]
    nsel = seq // SEL_BLOCK
    assert nsel <= LANES and nsel % 8 == 0
    c_start = np.arange(nc) * CMP_STRIDE
    j_start = np.arange(LANES) * SEL_BLOCK
    ov = ((c_start[:, None] < j_start[None, :] + SEL_BLOCK)
          & (c_start[:, None] + CMP_BLOCK > j_start[None, :])
          & (np.arange(LANES)[None, :] < nsel)).astype(np.float32)
    gq = HEADS_PER_GROUP * HEAD_DIM
    qmap = lambda bg, i: ((bg // N_KV_GROUPS) * nq + i, bg % N_KV_GROUPS)
    return pl.pallas_call(
        functools.partial(_cmp_kernel, tq=tq, nsel=nsel),
        grid=(batch * N_KV_GROUPS, nq),
        in_specs=[pl.BlockSpec((tq, gq), qmap),
                  pl.BlockSpec((1, nc, HEAD_DIM), lambda bg, i: (bg, 0, 0)),
                  pl.BlockSpec((1, nc, HEAD_DIM), lambda bg, i: (bg, 0, 0)),
                  pl.BlockSpec((tq, LANES), lambda bg, i: ((bg // N_KV_GROUPS) * nq + i, 0)),
                  pl.BlockSpec((nc, LANES), lambda bg, i: (0, 0))],
        out_specs=[pl.BlockSpec((tq, gq), qmap),
                   pl.BlockSpec((tq, LANES), lambda bg, i: (bg * nq + i, 0))],
        out_shape=[jax.ShapeDtypeStruct((t, ATT_W), BF16),
                   jax.ShapeDtypeStruct((batch * N_KV_GROUPS * seq, LANES), BF16)],
        compiler_params=_cparams(("parallel", "parallel")),
        name="cmp_attn",
    )(proj, kc, vc, gates, jnp.asarray(ov))


def _selwin_kernel(q_ref, selb_ref, ocmp_ref, gates_ref, ks_ref, vs_ref, kw_ref, vw_ref,
                   cb_ref, wb_ref, o_ref, kaug_s, vaug_s, vwaug_s, qaug_s, sbuf_s, m_s, acc_s,
                   swin_s, *, tq, tk, seq, nwin):
    g = pl.program_id(0) % N_KV_GROUPS
    qi = pl.program_id(1)
    q0 = qi * tq
    hq = HEADS_PER_GROUP * tq
    half = hq // 2

    @pl.when(qi == 0)
    def _():
        r = lax.broadcasted_iota(jnp.int32, (seq, HEAD_DIM), 0)
        ln = lax.broadcasted_iota(jnp.int32, (seq, HEAD_DIM), 1)
        ones_col = jnp.where(ln == 0, 1.0, 0.0).astype(BF16)
        kaug_s[:, 0:HEAD_DIM] = ks_ref[...]
        kaug_s[:, HEAD_DIM:2 * HEAD_DIM] = jnp.where(r // SEL_BLOCK == ln, 1.0, 0.0).astype(BF16)
        vaug_s[:, 0:HEAD_DIM] = vs_ref[...]
        vaug_s[:, HEAD_DIM:2 * HEAD_DIM] = ones_col
        vwaug_s[:, 0:HEAD_DIM] = vw_ref[...]
        vwaug_s[:, HEAD_DIM:2 * HEAD_DIM] = ones_col

    for h in range(HEADS_PER_GROUP):
        qaug_s[h * tq:(h + 1) * tq, 0:HEAD_DIM] = q_ref[:, h * HEAD_DIM:(h + 1) * HEAD_DIM]
        qaug_s[h * tq:(h + 1) * tq, HEAD_DIM:2 * HEAD_DIM] = selb_ref[...]
    qa = qaug_s[...]

    def scores(kt, slot):
        ks = kaug_s[pl.ds(pl.multiple_of(kt * tk, tk), tk), :]
        sbuf_s[slot] = _dot_nt(qa, ks)

    def absorb(kt, slot, diagonal=False):
        vs = vaug_s[pl.ds(pl.multiple_of(kt * tk, tk), tk), :]
        s = sbuf_s[slot]
        if diagonal:
            bias = cb_ref[(q0 - kt * tk) // tq]
            s = (s.reshape(HEADS_PER_GROUP, tq, tk) + bias[None]).reshape(hq, tk)
        m = m_s[...]
        m_new = jnp.maximum(m, jnp.max(s, axis=-1, keepdims=True))
        m_s[...] = m_new
        pb = jnp.exp2(s - m_new[:, 0:1]).astype(BF16)
        if diagonal:
            pv = jnp.concatenate([_dot(pb[0:half], vs), _dot(pb[half:hq], vs)], axis=0)
        else:
            pv = _dot(pb, vs)
        acc_s[...] = jnp.exp2(m - m_new)[:, 0:1] * acc_s[...] + pv

    n_full = q0 // tk
    m_s[...] = jnp.full(m_s.shape, -jnp.inf, F32)
    acc_s[...] = jnp.zeros(acc_s.shape, F32)
    scores(0, 0)

    def pair(j, c):
        scores(2 * j + 1, 1)
        absorb(2 * j, 0)
        scores(2 * j + 2, 0)
        absorb(2 * j + 1, 1)
        return c

    lax.fori_loop(0, n_full // 2, pair, 0)
    odd = n_full % 2

    @pl.when(odd == 1)
    def _():
        scores(n_full, 1)
        absorb(n_full - 1, 0)

    nq = seq // tq
    w0 = jnp.clip(qi - (nwin - 1), 0, nq - nwin) * tq
    wlen = nwin * tq
    kwin = kw_ref[pl.ds(pl.multiple_of(w0, tq), wlen), :]
    vwin = vwaug_s[pl.ds(pl.multiple_of(w0, tq), wlen), :]
    swin_s[0:half, :] = _dot_nt(qaug_s[0:half, 0:HEAD_DIM], kwin)
    swin_s[half:hq, :] = _dot_nt(qaug_s[half:hq, 0:HEAD_DIM], kwin)
    absorb(n_full, odd, diagonal=True)
    acc = acc_s[...]
    o_sel = acc[:, 0:HEAD_DIM] / acc[:, HEAD_DIM:HEAD_DIM + 1]

    wbias = wb_ref[(q0 - w0) // tq]
    sw = (swin_s[...].reshape(HEADS_PER_GROUP, tq, wlen) + wbias[None]).reshape(hq, wlen)
    pwb = jnp.exp2(sw - jnp.max(sw, axis=-1, keepdims=True)).astype(BF16)
    ow = jnp.concatenate([_dot(pwb[0:half], vwin), _dot(pwb[half:hq], vwin)], axis=0)
    o_win = ow[:, 0:HEAD_DIM] / ow[:, HEAD_DIM:HEAD_DIM + 1]

    gl = gates_ref[...]
    lane = lax.broadcasted_iota(jnp.int32, (tq, LANES), 1)
    for h in range(HEADS_PER_GROUP):
        gcol = (g * HEADS_PER_GROUP + h) * 3
        g_s = _sigmoid(_lane_col(gl, lane, gcol + 1))
        g_w = _sigmoid(_lane_col(gl, lane, gcol + 2))
        o = (ocmp_ref[:, h * HEAD_DIM:(h + 1) * HEAD_DIM].astype(F32)
             + g_s * o_sel[h * tq:(h + 1) * tq, :] + g_w * o_win[h * tq:(h + 1) * tq, :])
        o_ref[:, h * HEAD_DIM:(h + 1) * HEAD_DIM] = o.astype(BF16)


def _selwin_attn(proj, selb, ocmp, gates, batch, seq, tq, tk):
    t = proj.shape[0]
    nq = seq // tq
    assert tk % tq == 0 and tq & (tq - 1) == 0 and seq % tk == 0
    nwin = min(WINDOW // tq + 1, nq)
    assert WINDOW % tq == 0
    gq = HEADS_PER_GROUP * HEAD_DIM
    hq = HEADS_PER_GROUP * tq
    r = np.arange(tq)[:, None]
    cbias = np.stack([np.where(np.arange(tk)[None, :] <= r + off * tq, 0.0, -np.inf)
                      for off in range(tk // tq)]).astype(np.float32)
    dist = [r + off * tq - np.arange(nwin * tq)[None, :] for off in range(nwin)]
    wbias = np.stack([np.where((d >= 0) & (d < WINDOW), 0.0, -np.inf)
                      for d in dist]).astype(np.float32)
    qmap = lambda bg, i: ((bg // N_KV_GROUPS) * nq + i, bg % N_KV_GROUPS)

    def kvspec(col):
        return pl.BlockSpec((seq, HEAD_DIM),
                            lambda bg, i: (bg // N_KV_GROUPS, col // HEAD_DIM + bg % N_KV_GROUPS))

    return pl.pallas_call(
        functools.partial(_selwin_kernel, tq=tq, tk=tk, seq=seq, nwin=nwin),
        grid=(batch * N_KV_GROUPS, nq),
        in_specs=[pl.BlockSpec((tq, gq), qmap),
                  pl.BlockSpec((tq, LANES), lambda bg, i: (bg * nq + i, 0)),
                  pl.BlockSpec((tq, gq), qmap),
                  pl.BlockSpec((tq, LANES), lambda bg, i: ((bg // N_KV_GROUPS) * nq + i, 0)),
                  kvspec(COL_KS), kvspec(COL_VS), kvspec(COL_KW), kvspec(COL_VW),
                  pl.BlockSpec(cbias.shape, lambda bg, i: (0, 0, 0)),
                  pl.BlockSpec(wbias.shape, lambda bg, i: (0, 0, 0))],
        out_specs=pl.BlockSpec((tq, gq), qmap),
        out_shape=jax.ShapeDtypeStruct((t, ATT_W), BF16),
        scratch_shapes=[pltpu.VMEM((seq, 2 * HEAD_DIM), BF16),
                        pltpu.VMEM((seq, 2 * HEAD_DIM), BF16),
                        pltpu.VMEM((seq, 2 * HEAD_DIM), BF16),
                        pltpu.VMEM((hq, 2 * HEAD_DIM), BF16),
                        pltpu.VMEM((2, hq, tk), F32),
                        pltpu.VMEM((hq, LANES), F32),
                        pltpu.VMEM((hq, 2 * HEAD_DIM), F32),
                        pltpu.VMEM((hq, nwin * tq), F32)],
        compiler_params=_cparams(("parallel", "arbitrary")),
        name="selwin_attn",
    )(proj, selb, ocmp, gates, proj, proj, proj, proj, jnp.asarray(cbias), jnp.asarray(wbias))


_HGRN_LEVELS = (32, 16, 8, 4, 2, 1)


def _hgrn_tables():
    c = HGRN_CHUNK
    nl = len(_HGRN_LEVELS)
    idx = np.arange(c)
    i, j = idx[:, None], idx[None, :]
    mats = [j <= i, j > i]
    lvl = np.full((c, c), nl + 1, np.int32)
    for li, s in enumerate(_HGRN_LEVELS):
        mats.append((j >= (i // s) * s) & (j <= i))
        mats.append((j > i) & (j <= (i // s) * s + s - 1))
        lvl[((i // (2 * s)) == (j // (2 * s))) & (((i // s) % 2) == 1) & (((j // s) % 2) == 0)] = li
    lvl[i == j] = nl
    w = np.concatenate(mats, 0).astype(np.float32)
    return np.concatenate([w, w], axis=1), lvl


def _hgrn_head(hq_ref, hf_ref, hi_ref, hg_ref, lbl_ref, gn_ref, w_ref, lvl_ref, o_ref, st_s,
               hh, ts):
    c = HGRN_CHUNK
    nc = ts // c
    nl = len(_HGRN_LEVELS)
    cs = slice(hh * HEAD_DIM, (hh + 1) * HEAD_DIM)
    lbl = lbl_ref[:, cs]
    e = jnp.exp(lbl - jnp.max(lbl, axis=0, keepdims=True))
    lb = e[0:1, :] / jnp.sum(e, axis=0, keepdims=True)

    f = lb + (1.0 - lb) * _sigmoid(hf_ref[:, cs].astype(F32))
    k = 1.0 - f
    qf = _silu(hq_ref[:, cs].astype(F32))
    v = hi_ref[:, cs]

    lf = jnp.log(f) * LOG2E
    lf_hi = lf.astype(BF16)
    lf_lo = (lf - lf_hi.astype(F32)).astype(BF16)
    side = lambda a: jnp.concatenate([a[ci * c:(ci + 1) * c, :] for ci in range(nc)], axis=1)
    ex = _dot(w_ref[...], jnp.concatenate([side(lf_hi), side(lf_lo)], axis=0))

    def rows(bi):
        return jnp.concatenate(
            [ex[bi * c:(bi + 1) * c, ci * HEAD_DIM:(ci + 1) * HEAD_DIM] for ci in range(nc)], axis=0)

    b = rows(0)
    qhat = (qf * jnp.exp2(b)).astype(BF16)
    khat = (k * jnp.exp2(rows(1))).astype(BF16)
    lvl = lvl_ref[...]
    masks = [lvl == li for li in range(nl + 1)]
    qs = [(qf * jnp.exp2(rows(2 + 2 * li))).astype(BF16) for li in range(nl)] + [qf.astype(BF16)]
    ks = [(k * jnp.exp2(rows(3 + 2 * li))).astype(BF16) for li in range(nl)] + [k.astype(BF16)]
    o_intra = []
    for ci in range(nc):
        sl = slice(ci * c, (ci + 1) * c)
        a = jnp.zeros((c, c), F32)
        for li in range(nl + 1):
            a = jnp.where(masks[li], _dot_nt(qs[li][sl, :], ks[li][sl, :]), a)
        o_intra.append(_dot(a.astype(BF16), v[sl, :]))

    st = st_s[hh]
    outs = []
    for ci in range(nc):
        sl = slice(ci * c, (ci + 1) * c)
        outs.append(o_intra[ci] + _dot_nt(qhat[sl, :], st.astype(BF16)))
        st = st * jnp.exp2(b[(ci + 1) * c - 1:(ci + 1) * c, :]) + lax.dot_general(
            v[sl, :], khat[sl, :], _TN, preferred_element_type=F32)
    st_s[hh] = st
    o = jnp.concatenate(outs, axis=0)
    o = o * lax.rsqrt(jnp.mean(o * o, axis=-1, keepdims=True) + EPS) * gn_ref[:, cs]
    o_ref[:, cs] = (o * _silu(hg_ref[:, cs].astype(F32))).astype(BF16)


def _hgrn_kernel(hq_ref, hf_ref, hi_ref, hg_ref, lbl_ref, gn_ref, w_ref, lvl_ref,
                 o_ref, st_s, *, ts, hb):
    @pl.when(pl.program_id(2) == 0)
    def _():
        st_s[...] = jnp.zeros_like(st_s)

    for hh in range(hb):
        _hgrn_head(hq_ref, hf_ref, hi_ref, hg_ref, lbl_ref, gn_ref, w_ref, lvl_ref, o_ref, st_s,
                   hh, ts)


def _hgrn(proj, lb_logits, g_norm, batch, seq, ts, hb):
    t = proj.shape[0]
    ns = seq // ts
    wmat, lvl = _hgrn_tables()
    w = hb * HEAD_DIM

    def colspec(col):
        return pl.BlockSpec((ts, w), lambda b, h, s: (b * ns + s, col // w + h))

    nl = lb_logits.shape[0]
    return pl.pallas_call(
        functools.partial(_hgrn_kernel, ts=ts, hb=hb),
        grid=(batch, N_HGRN_HEADS // hb, ns),
        in_specs=[colspec(COL_HQ), colspec(COL_HF), colspec(COL_HI), colspec(COL_HG),
                  pl.BlockSpec((nl, w), lambda b, h, s: (0, h)),
                  pl.BlockSpec((1, w), lambda b, h, s: (0, h)),
                  pl.BlockSpec(wmat.shape, lambda b, h, s: (0, 0)),
                  pl.BlockSpec(lvl.shape, lambda b, h, s: (0, 0))],
        out_specs=pl.BlockSpec((ts, w), lambda b, h, s: (b * ns + s, h)),
        out_shape=jax.ShapeDtypeStruct((t, ATT_W), BF16),
        scratch_shapes=[pltpu.VMEM((hb, HEAD_DIM, HEAD_DIM), F32)],
        compiler_params=_cparams(("parallel", "parallel", "arbitrary")),
        name="hgrn2",
    )(proj, proj, proj, proj, lb_logits, g_norm, jnp.asarray(wmat, BF16), jnp.asarray(lvl))


def _outproj_kernel(oa_ref, or_ref, wa_ref, wb_ref, x_ref, mod_ref, gpost_ref, gpre_ref,
                    wr_ref, br_ref, x1_ref, h2_ref, lg_ref, stage, *, per_b):
    b = pl.program_id(0) // per_b
    y = _dot(oa_ref[...], wa_ref[...]) + _dot(or_ref[...], wb_ref[...])
    x1 = x_ref[...] + _mod_row(mod_ref, b, 2) * _rms(y, gpost_ref[...])
    x1_ref[...] = x1
    h = _rms(x1, gpre_ref[...]) * (1.0 + _mod_row(mod_ref, b, 4)) + _mod_row(mod_ref, b, 3)
    h2_ref[...] = _to_slabs(stage, h)
    h_hi = h.astype(BF16)
    h_lo = (h - h_hi.astype(F32)).astype(BF16)
    hh = _dot(h_hi, wr_ref[...])
    lg_ref[...] = (hh[:, 0:LANES] + hh[:, LANES:2 * LANES]
                   + _dot(h_lo, wr_ref[:, 0:LANES]) + br_ref[...])


def _outproj(o_att, o_rec, w_out, x2, mod, g_post, g_pre, wr, br, seq, tm):
    t, d = x2.shape
    per_b = seq // tm
    wr_hi = wr.astype(BF16)
    wr_cat = jnp.concatenate([wr_hi, (wr - wr_hi.astype(F32)).astype(BF16)], axis=1)
    wa = w_out[:ATT_W].astype(BF16)
    wb = w_out[ATT_W:].astype(BF16)
    row = lambda i: (i, 0)
    fixed = lambda i: (0, 0)
    return pl.pallas_call(
        functools.partial(_outproj_kernel, per_b=per_b),
        grid=(t // tm,),
        in_specs=[pl.BlockSpec((tm, ATT_W), row), pl.BlockSpec((tm, ATT_W), row),
                  pl.BlockSpec((ATT_W, d), fixed), pl.BlockSpec((ATT_W, d), fixed),
                  pl.BlockSpec((tm, d), row),
                  pl.BlockSpec(mod.shape, fixed),
                  pl.BlockSpec((1, d), fixed), pl.BlockSpec((1, d), fixed),
                  pl.BlockSpec((d, 2 * LANES), fixed), pl.BlockSpec((1, LANES), fixed)],
        out_specs=[pl.BlockSpec((tm, d), row), pl.BlockSpec((tm * (d // LANES), LANES), row),
                   pl.BlockSpec((tm, LANES), row)],
        out_shape=[jax.ShapeDtypeStruct((t, d), F32),
                   jax.ShapeDtypeStruct((t * (d // LANES), LANES), BF16),
                   jax.ShapeDtypeStruct((t, LANES), F32)],
        scratch_shapes=[pltpu.VMEM((tm * (d // LANES), LANES), F32)],
        compiler_params=_cparams(("parallel",)),
        name="outproj",
    )(o_att, o_rec, wa, wb, x2, mod, g_post, g_pre, wr_cat, br)


G_LANE0 = N_EXPERTS


def _route_kernel(lg_ref, tri_ref, info_ref, cnt_ref, carry_s):
    @pl.when(pl.program_id(0) == 0)
    def _():
        carry_s[...] = jnp.zeros_like(carry_s)

    x = lg_ref[...]
    lane = lax.broadcasted_iota(jnp.int32, x.shape, 1)
    big = jnp.int32(10 ** 6)
    rmax = lambda a: jnp.max(a, axis=-1, keepdims=True)
    rmin = lambda a: jnp.min(a, axis=-1, keepdims=True)
    rsum = lambda a: jnp.sum(a, axis=-1, keepdims=True)

    is_g = (lane >= G_LANE0) & (lane < G_LANE0 + N_EXPERT_GROUPS)
    lgm = jnp.where(is_g, x, -jnp.inf)
    mg = rmax(lgm)
    pg_sel = 1.0 / rsum(jnp.where(is_g, jnp.exp(lgm - mg), 0.0))
    gsel = rmin(jnp.where(lgm == mg, lane, big)) - G_LANE0

    is_e = (lane >= gsel * EXPERTS_PER_GROUP) & (lane < (gsel + 1) * EXPERTS_PER_GROUP)
    lem = jnp.where(is_e, x, -jnp.inf)
    pe = jnp.where(is_e, jnp.exp(lem - rmax(lem)), 0.0)
    pe = pe / rsum(pe)
    pe = jnp.where(is_e, pe, -1.0)
    v1 = rmax(pe)
    i1 = rmin(jnp.where(pe == v1, lane, big))
    pe2 = jnp.where(lane == i1, -1.0, pe)
    v2 = rmax(pe2)
    i2 = rmin(jnp.where(pe2 == v2, lane, big))
    w1 = v1 / (v1 + v2) * pg_sel
    w2 = v2 / (v1 + v2) * pg_sel

    oh1 = jnp.where(lane == i1, 1.0, 0.0)
    oh2 = jnp.where(lane == i2, 1.0, 0.0)
    both = oh1 + oh2
    before = _dot(tri_ref[...], both.astype(BF16)) + carry_s[0:1, :]
    r1 = rsum(oh1 * before)
    r2 = rsum(oh2 * before)
    carry_s[0:1, :] = carry_s[0:1, :] + jnp.sum(both, axis=0, keepdims=True)
    cnt_ref[...] = carry_s[...]

    cols = (i1.astype(F32), i2.astype(F32), w1, w2, r1, r2)
    info = jnp.zeros(x.shape, F32)
    for ci, col in enumerate(cols):
        info = jnp.where(lane == ci, col, info)
    info_ref[...] = info


def _route(logits, tm):
    t = logits.shape[0]
    tri = jnp.asarray(np.tril(np.ones((tm, tm), np.float32), -1), BF16)
    return pl.pallas_call(
        _route_kernel,
        grid=(t // tm,),
        in_specs=[pl.BlockSpec((tm, LANES), lambda i: (i, 0)),
                  pl.BlockSpec((tm, tm), lambda i: (0, 0))],
        out_specs=[pl.BlockSpec((tm, LANES), lambda i: (i, 0)),
                   pl.BlockSpec((8, LANES), lambda i: (0, 0))],
        out_shape=[jax.ShapeDtypeStruct((t, LANES), F32),
                   jax.ShapeDtypeStruct((8, LANES), F32)],
        scratch_shapes=[pltpu.VMEM((8, LANES), F32)],
        compiler_params=_cparams(("arbitrary",)),
        name="route",
    )(logits, tri)


def _to_slabs(stage, x):
    n, d = x.shape
    slab = d // LANES
    for c in range(slab):
        stage[pl.ds(c, n, stride=slab), :] = x[:, c * LANES:(c + 1) * LANES]
    return stage[...].astype(BF16)


def _from_slabs(stage, slabs, n):
    slab = slabs.shape[0] // n
    stage[...] = slabs.astype(F32)
    return jnp.concatenate([stage[pl.ds(c, n, stride=slab), :] for c in range(slab)], axis=1)


def _expert_kernel(te_ref, nu_ref, src0_ref, srcn_ref, dst_ref, h_ref, wg_ref, wu_ref, wd_ref,
                   y_ref, xbuf, obuf, stage, gsem, ssem, wg_s, wu_s, wd_s, *, tm, standin_row):
    t = pl.program_id(0)
    nu = nu_ref[0]
    slot = t % 2
    slab = wg_s.shape[0] // LANES
    tile = tm * slab

    def token(ref, i):
        return ref.at[pl.ds(pl.multiple_of(i * slab, slab), slab)]

    def gather(src_ref, s):
        for r in range(tm):
            pltpu.make_async_copy(token(h_ref, src_ref[0, 0, r]), token(xbuf, s * tm + r),
                                  gsem.at[s]).start()

    def scatter(s, row_of):
        for r in range(tm):
            pltpu.make_async_copy(token(obuf, s * tm + r), token(y_ref, row_of(r)),
                                  ssem.at[s]).start()

    def tile_of(buf, s):
        return buf.at[pl.ds(pl.multiple_of(s * tile, tile), tile)]

    def wait_tile(buf, sem, s):
        pltpu.make_async_copy(tile_of(buf, s), tile_of(buf, s), sem.at[s]).wait()

    @pl.when(t < nu)
    def _():
        @pl.when(t == 0)
        def _():
            gather(src0_ref, 0)
            obuf[tile:2 * tile, :] = jnp.zeros((tile, LANES), obuf.dtype)

            def fill(row0, s):
                return pltpu.make_async_copy(tile_of(obuf, 1),
                                             y_ref.at[pl.ds(row0 * slab, tile)], ssem.at[s])

            fill(standin_row - 2 * tm, 0).start()
            fill(standin_row - tm, 0).start()
            fill(standin_row - 2 * tm, 0).wait()
            fill(standin_row - tm, 0).wait()
            fill(standin_row, 1).start()

        @pl.when((t == 0) | (te_ref[t] != te_ref[jnp.maximum(t - 1, 0)]))
        def _():
            wg_s[...] = wg_ref[0].astype(BF16)
            wu_s[...] = wu_ref[0].astype(BF16)
            wd_s[...] = wd_ref[0].astype(BF16)

        gather(srcn_ref, 1 - slot)
        wait_tile(xbuf, gsem, slot)
        row0 = pl.multiple_of(slot * tile, tile)
        x = _from_slabs(stage, xbuf[pl.ds(row0, tile), :], tm).astype(BF16)
        hid = (_silu(_dot(x, wg_s[...])) * _dot(x, wu_s[...])).astype(BF16)
        obuf[pl.ds(row0, tile), :] = _to_slabs(stage, _dot(hid, wd_s[...]))
        scatter(slot, lambda r: dst_ref[0, 0, r])
        wait_tile(obuf, ssem, 1 - slot)

        @pl.when(t == nu - 1)
        def _():
            wait_tile(obuf, ssem, slot)
            wait_tile(xbuf, gsem, 1 - slot)


def _experts(tile_expert, n_used, src3, dst3, h2, w_gate, w_up, w_down, y_rows, tm,
             standin_row):
    n_tiles = src3.shape[0]
    d, f = w_gate.shape[1:]
    slab = d // LANES
    wmap = lambda t, te, nu: (te[t], 0, 0)
    smem = lambda imap: pl.BlockSpec((1, 1, tm), imap, memory_space=pltpu.SMEM)
    return pl.pallas_call(
        functools.partial(_expert_kernel, tm=tm, standin_row=standin_row),
        grid_spec=pltpu.PrefetchScalarGridSpec(
            num_scalar_prefetch=2,
            grid=(n_tiles,),
            in_specs=[smem(lambda t, te, nu: (0, 0, 0)),
                      smem(lambda t, te, nu: (jnp.minimum(t + 1, nu[0] - 1), 0, 0)),
                      smem(lambda t, te, nu: (t, 0, 0)),
                      pl.BlockSpec(memory_space=pl.ANY),
                      pl.BlockSpec((1, d, f), wmap),
                      pl.BlockSpec((1, d, f), wmap),
                      pl.BlockSpec((1, f, d), wmap)],
            out_specs=pl.BlockSpec(memory_space=pl.ANY),
            scratch_shapes=[pltpu.VMEM((2 * tm * slab, LANES), BF16),
                            pltpu.VMEM((2 * tm * slab, LANES), BF16),
                            pltpu.VMEM((tm * slab, LANES), F32),
                            pltpu.SemaphoreType.DMA((2,)), pltpu.SemaphoreType.DMA((2,)),
                            pltpu.VMEM((d, f), BF16), pltpu.VMEM((d, f), BF16),
                            pltpu.VMEM((f, d), BF16)]),
        out_shape=jax.ShapeDtypeStruct((y_rows * slab, LANES), BF16),
        compiler_params=_cparams(("arbitrary",)),
        name="moe_experts",
    )(tile_expert, n_used, src3, src3, dst3, h2, w_gate, w_up, w_down)


def _final_kernel(info_ref, x1_ref, mod_ref, g_ref, ya_ref, yb_ref, o_ref, stage_a, stage_b, *,
                  per_b):
    info = info_ref[...]
    tm = x1_ref.shape[0]
    ya = _from_slabs(stage_a, ya_ref[...], tm)
    yb = _from_slabs(stage_b, yb_ref[...], tm)
    y = info[:, 2:3] * ya + info[:, 3:4] * yb
    gate = _mod_row(mod_ref, pl.program_id(0) // per_b, 5)
    o_ref[...] = x1_ref[...] + gate * _rms(y, g_ref[...])


def _moe_final(info, x1, mod, g_post, y2, seq, tm):
    t, d = x1.shape
    per_b = seq // tm
    nb = t // tm
    return pl.pallas_call(
        functools.partial(_final_kernel, per_b=per_b),
        grid=(nb,),
        in_specs=[pl.BlockSpec((tm, LANES), lambda i: (i, 0)),
                  pl.BlockSpec((tm, d), lambda i: (i, 0)),
                  pl.BlockSpec(mod.shape, lambda i: (0, 0)),
                  pl.BlockSpec((1, d), lambda i: (0, 0)),
                  pl.BlockSpec((tm * (d // LANES), LANES), lambda i: (i, 0)),
                  pl.BlockSpec((tm * (d // LANES), LANES), lambda i: (nb + i, 0))],
        out_specs=pl.BlockSpec((tm, d), lambda i: (i, 0)),
        out_shape=jax.ShapeDtypeStruct((t, d), F32),
        scratch_shapes=[pltpu.VMEM((tm * (d // LANES), LANES), F32)] * 2,
        compiler_params=_cparams(("parallel",)),
        name="moe_final",
    )(info, x1, mod, g_post, y2, y2)


def _slot_map_kernel(pos_ref, src_in, dst_in, src_ref, dst_ref, sem, *, tm, n_tokens):
    @pl.when(pl.program_id(0) == 0)
    def _():
        c_src = pltpu.make_async_copy(src_in, src_ref, sem.at[0])
        c_dst = pltpu.make_async_copy(dst_in, dst_ref, sem.at[1])
        c_src.start()
        c_dst.start()
        c_src.wait()
        c_dst.wait()

    base = pl.program_id(0) * tm

    def body(i, c):
        tok = base + i
        p0 = pos_ref[0, 0, 2 * i]
        p1 = pos_ref[0, 0, 2 * i + 1]
        src_ref[p0] = tok
        src_ref[p1] = tok
        dst_ref[p0] = tok
        dst_ref[p1] = n_tokens + tok
        return c

    lax.fori_loop(0, tm, body, 0, unroll=8)


def _slot_map(pos, src0, dst0, tm):
    t = pos.shape[0]
    whole = pl.BlockSpec(memory_space=pltpu.SMEM)
    return pl.pallas_call(
        functools.partial(_slot_map_kernel, tm=tm, n_tokens=t),
        grid=(t // tm,),
        in_specs=[pl.BlockSpec((1, 1, 2 * tm), lambda i: (i, 0, 0), memory_space=pltpu.SMEM),
                  pl.BlockSpec(memory_space=pl.ANY), pl.BlockSpec(memory_space=pl.ANY)],
        out_specs=[whole, whole],
        out_shape=[jax.ShapeDtypeStruct(src0.shape, jnp.int32)] * 2,
        scratch_shapes=[pltpu.SemaphoreType.DMA((2,))],
        compiler_params=_cparams(("arbitrary",)),
        name="slot_map",
    )(pos.reshape(t // tm, 1, 2 * tm), src0, dst0)


def _moe(h2, logits, x1, mod, g_post, w_gate, w_up, w_down, seq, tm_route, tm_exp, tm_fin):
    t = x1.shape[0]
    info, cnt = _route(logits, tm_route)
    counts = cnt[0, :N_EXPERTS].astype(jnp.int32)
    tiles_e = (counts + tm_exp - 1) // tm_exp
    tile_end = jnp.cumsum(tiles_e)
    offs = (tile_end - tiles_e) * tm_exp
    n_used = tile_end[-1]

    def slot_of(k):
        e = info[:, k].astype(jnp.int32)
        start = jnp.sum(jnp.where(e[:, None] == jnp.arange(N_EXPERTS)[None, :], offs[None, :], 0),
                        axis=1)
        return start + info[:, 4 + k].astype(jnp.int32)

    pos = jnp.stack([slot_of(0), slot_of(1)], axis=1)
    n_tiles = (2 * t) // tm_exp + N_EXPERTS
    tidx = jnp.minimum(jnp.arange(n_tiles, dtype=jnp.int32), n_used - 1)
    tile_expert = jnp.sum(tidx[:, None] >= tile_end[None, :], axis=1).astype(jnp.int32)
    slot_ids = jnp.arange(n_tiles * tm_exp, dtype=jnp.int32)
    spare = 2 * t + ((slot_ids // tm_exp) % 2) * tm_exp + slot_ids % tm_exp
    src, dst = _slot_map(pos, jnp.zeros_like(slot_ids), spare, tm_route)

    ew = w_gate.shape
    wg = w_gate.reshape(N_EXPERTS, ew[-2], ew[-1])
    wu = w_up.reshape(N_EXPERTS, ew[-2], ew[-1])
    wd = w_down.reshape(N_EXPERTS, ew[-1], ew[-2])
    y2 = _experts(tile_expert, n_used.reshape(1).astype(jnp.int32),
                  src.reshape(n_tiles, 1, tm_exp), dst.reshape(n_tiles, 1, tm_exp),
                  h2, wg, wu, wd, 2 * t + 3 * tm_exp, tm_exp, 2 * t + 2 * tm_exp)
    return _moe_final(info, x1, mod, g_post, y2, seq, tm_fin)


def _permute_w_in(w_in):
    sizes = [ATT_W] + [KV_W] * 6 + [3 * N_ATT_HEADS] + [ATT_W] * 4
    cuts = np.cumsum(sizes)[:-1]
    q, kc, vc, ks, vs, kw, vw, gates, hq, hf, hi, hg = jnp.split(w_in.astype(BF16), cuts, axis=1)
    w_p = jnp.concatenate([q, kc, ks, kw, vc, vs, vw, hq, hf, hi, hg], axis=1)
    w_g = jnp.pad(gates, ((0, 0), (0, LANES - gates.shape[1])))
    return w_p, w_g


def _block(x, c, positions, w_ada, b_ada, g_pre_mix, g_post_mix, g_pre_ffn, g_post_ffn,
           w_in, w_out, pe_k, w1_k, w2_k, pe_v, w1_v, w2_v, lb_logits, g_norm,
           w_group, b_group, w_router, b_router, w_gate, w_up, w_down, cfg):
    batch, seq, d = x.shape
    t = batch * seq
    x2 = x.reshape(t, d)

    c8 = jnp.zeros((8, d), F32).at[:batch].set(c)
    mod = _ada_mod(c8, w_ada, b_ada)
    cos_t, sin_t = _rope_tables(positions, cfg["tm_rope"])
    w_p, w_g = _permute_w_in(w_in)
    proj, gates, hbk, hbv = _inproj(x2, mod, g_pre_mix.reshape(1, d), w_p, w_g, cos_t, sin_t,
                                    seq, cfg["tm_in"])

    half_blocks = (batch * N_KV_GROUPS, seq // CMP_STRIDE, CMP_STRIDE * HEAD_DIM)
    kc = _compress(hbk.reshape(half_blocks), w1_k, pe_k, w2_k)
    vc = _compress(hbv.reshape(half_blocks), w1_v, pe_v, w2_v)
    ocmp, selb = _cmp_attn(proj, kc, vc, gates, batch, seq, cfg["tq_cmp"])
    o_att = _selwin_attn(proj, selb, ocmp, gates, batch, seq, cfg["tq"], cfg["tk"])
    o_rec = _hgrn(proj, lb_logits, g_norm.reshape(1, -1), batch, seq, cfg["ts_hgrn"],
                  cfg["hb_hgrn"])

    wr = jnp.concatenate([w_router, w_group], axis=1)
    wr = jnp.pad(wr, ((0, 0), (0, LANES - wr.shape[1])))
    br = jnp.pad(jnp.concatenate([b_router, b_group]), (0, LANES - N_EXPERTS - N_EXPERT_GROUPS))
    x1, h2, logits = _outproj(o_att, o_rec, w_out, x2, mod, g_post_mix.reshape(1, d),
                              g_pre_ffn.reshape(1, d), wr, br.reshape(1, LANES), seq,
                              cfg["tm_out"])
    out = _moe(h2, logits, x1, mod, g_post_ffn.reshape(1, d), w_gate, w_up, w_down, seq,
               cfg["tm_route"], cfg["tm_exp"], cfg["tm_fin"])
    return out.reshape(batch, seq, d)


def _config(seq):
    return dict(tm_rope=min(1024, seq), tm_in=min(1024, seq), tq_cmp=min(256, seq),
                tq=min(256, seq), tk=min(512, seq), ts_hgrn=min(256, seq), hb_hgrn=4,
                tm_out=min(256, seq), tm_route=min(256, seq), tm_exp=256,
                tm_fin=min(256, seq))


def kernel(x, c, positions, w_ada, b_ada, g_pre_mix, g_post_mix, g_pre_ffn, g_post_ffn, w_in, w_out, cmp_pe_k, cmp_w1_k, cmp_w2_k, cmp_pe_v, cmp_w1_v, cmp_w2_v, hgrn_lb_logits, hgrn_g_norm, w_group, b_group, w_router, b_router, w_gate, w_up, w_down):
    assert w_ada.shape[0] == 1, "single-layer block"
    return _block(x, c, positions, w_ada[0], b_ada[0], g_pre_mix[0], g_post_mix[0],
                  g_pre_ffn[0], g_post_ffn[0], w_in[0], w_out[0], cmp_pe_k[0], cmp_w1_k[0],
                  cmp_w2_k[0], cmp_pe_v[0], cmp_w1_v[0], cmp_w2_v[0], hgrn_lb_logits,
                  hgrn_g_norm[0], w_group[0], b_group[0], w_router[0], b_router[0],
                  w_gate[0], w_up[0], w_down[0], _config(x.shape[1]))
```

```python
import functools

import numpy as np
import jax
import jax.numpy as jnp
from jax import lax
from jax.experimental import pallas as pl
from jax.experimental.pallas import tpu as pltpu

F32 = jnp.float32
BF16 = jnp.bfloat16

HEAD_DIM = 128
N_KV_GROUPS = 2
HEADS_PER_GROUP = 4
N_ATT_HEADS = N_KV_GROUPS * HEADS_PER_GROUP
CMP_BLOCK = 32
CMP_STRIDE = 16
SEL_BLOCK = 64
N_SEL = 16
WINDOW = 512
FORCE_BONUS = 1.0e4
ROPE_THETA = 10000.0
N_HGRN_HEADS = 8
HGRN_CHUNK = 64
N_EXPERT_GROUPS = 4
EXPERTS_PER_GROUP = 8
N_EXPERTS = N_EXPERT_GROUPS * EXPERTS_PER_GROUP
EPS = 1e-6

LANES = 128
VMEM_LIMIT = 56 * 1024 * 1024

ATT_W = N_ATT_HEADS * HEAD_DIM
KV_W = N_KV_GROUPS * HEAD_DIM
COL_Q = 0
COL_KC = COL_Q + ATT_W
COL_KS = COL_KC + KV_W
COL_KW = COL_KS + KV_W
COL_VC = COL_KW + KV_W
COL_VS = COL_VC + KV_W
COL_VW = COL_VS + KV_W
COL_HQ = COL_VW + KV_W
COL_HF = COL_HQ + ATT_W
COL_HI = COL_HF + ATT_W
COL_HG = COL_HI + ATT_W
PROJ_W = COL_HG + ATT_W
ROPE_W = COL_VC

MASK_BIG = float(2 ** 30)
LOG2E = 1.4426950408889634
Q_PRESCALE = HEAD_DIM ** -0.5 * LOG2E

_NT = (((1,), (1,)), ((), ()))
_TN = (((0,), (0,)), ((), ()))


def _cparams(sem):
    return pltpu.CompilerParams(dimension_semantics=sem, vmem_limit_bytes=VMEM_LIMIT)


def _dot(a, b, **kw):
    return jnp.dot(a, b, preferred_element_type=F32, **kw)


def _dot_nt(a, b):
    return lax.dot_general(a, b, _NT, preferred_element_type=F32)


def _sigmoid(x):
    return 1.0 / (1.0 + jnp.exp(-x))


def _silu(x):
    return x * _sigmoid(x)


def _rms(x, g):
    return x * lax.rsqrt(jnp.mean(x * x, axis=-1, keepdims=True) + EPS) * g


def _mod_row(mod_ref, b, k):
    d = mod_ref.shape[1] // 6
    return mod_ref[pl.ds(b, 1), k * d:(k + 1) * d]


def _lane_col(x, lane, col):
    return jnp.sum(jnp.where(lane == col, x, 0.0), axis=-1, keepdims=True)


def _ada_kernel(c_ref, w_ref, b_ref, o_ref):
    s = _silu(c_ref[...])
    o_ref[...] = _dot(s, w_ref[...], precision=lax.Precision.HIGHEST) + b_ref[...]


def _ada_mod(c8, w_ada, b_ada, tn=512):
    rows, d = c8.shape
    n = w_ada.shape[1]
    return pl.pallas_call(
        _ada_kernel,
        grid=(n // tn,),
        in_specs=[pl.BlockSpec((rows, d), lambda j: (0, 0)),
                  pl.BlockSpec((d, tn), lambda j: (0, j)),
                  pl.BlockSpec((1, tn), lambda j: (0, j))],
        out_specs=pl.BlockSpec((rows, tn), lambda j: (0, j)),
        out_shape=jax.ShapeDtypeStruct((rows, n), F32),
        compiler_params=_cparams(("parallel",)),
        name="ada_mod",
    )(c8, w_ada, b_ada.reshape(1, n))


def _rope_kernel(ang_ref, cos_ref, sin_ref):
    a = ang_ref[...]
    lane = lax.broadcasted_iota(jnp.int32, a.shape, 1)
    cos_ref[...] = jnp.cos(a)
    sin_ref[...] = jnp.where(lane < HEAD_DIM // 2, -jnp.sin(a), jnp.sin(a))


def _rope_tables(positions, tm):
    t = positions.size
    inv_freq = ROPE_THETA ** (-jnp.arange(0, HEAD_DIM, 2, dtype=F32) / HEAD_DIM)
    ang = positions.reshape(t, 1).astype(F32) * jnp.concatenate([inv_freq, inv_freq])[None, :]
    spec = pl.BlockSpec((tm, HEAD_DIM), lambda i: (i, 0))
    return pl.pallas_call(
        _rope_kernel,
        grid=(t // tm,),
        in_specs=[spec],
        out_specs=[spec, spec],
        out_shape=[jax.ShapeDtypeStruct((t, HEAD_DIM), F32)] * 2,
        compiler_params=_cparams(("parallel",)),
        name="rope_tables",
    )(ang)


def _inproj_kernel(x_ref, mod_ref, g_ref, w_ref, wg_ref, cos_ref, sin_ref,
                   proj_ref, gates_ref, hbk_ref, hbv_ref, h_s, stage, *, tn, per_b):
    j = pl.program_id(1)

    @pl.when(j == 0)
    def _():
        b = pl.program_id(0) // per_b
        h = (_rms(x_ref[...], g_ref[...]) * (1.0 + _mod_row(mod_ref, b, 1))
             + _mod_row(mod_ref, b, 0))
        hb = h.astype(BF16)
        h_s[...] = hb
        gates_ref[...] = _dot(hb, wg_ref[...])

    heads = tn // HEAD_DIM
    rows = x_ref.shape[0]

    def store(jt):
        acc = _dot(h_s[...], w_ref[...])
        for c in range(heads):
            a = acc[:, c * HEAD_DIM:(c + 1) * HEAD_DIM]
            col = None if jt is None else jt * tn + c * HEAD_DIM
            if col is not None and col < ROPE_W:
                a = a * cos_ref[...] + pltpu.roll(a, HEAD_DIM // 2, axis=1) * sin_ref[...]
            if col is not None and col < ATT_W:
                a = a * Q_PRESCALE
            proj_ref[:, c * HEAD_DIM:(c + 1) * HEAD_DIM] = a.astype(BF16)
            for col0, hb_ref in ((COL_KC, hbk_ref), (COL_VC, hbv_ref)):
                if col is not None and col0 <= col < col0 + KV_W:
                    stage[...] = a
                    hb_ref[0, (col - col0) // HEAD_DIM] = jnp.concatenate(
                        [stage[pl.ds(i, rows // CMP_STRIDE, stride=CMP_STRIDE), :]
                         for i in range(CMP_STRIDE)], axis=1).astype(BF16)

    n_special = -(-COL_VS // tn)
    for jt in range(n_special):
        pl.when(j == jt)(functools.partial(store, jt))
    pl.when(j >= n_special)(functools.partial(store, None))


def _inproj(x2, mod, g_pre, w_in_p, w_gates, cos_t, sin_t, seq, tm, tn=512):
    t, d = x2.shape
    per_b = seq // tm
    assert COL_KC % tn + KV_W <= tn and COL_VC % tn + KV_W <= tn and tm % (16 * CMP_STRIDE) == 0
    hb_spec = pl.BlockSpec((1, N_KV_GROUPS, tm // CMP_STRIDE, CMP_STRIDE * HEAD_DIM),
                           lambda i, j: (i // per_b, 0, i % per_b, 0))
    hb_shape = jax.ShapeDtypeStruct(
        (t // seq, N_KV_GROUPS, seq // CMP_STRIDE, CMP_STRIDE * HEAD_DIM), BF16)
    return pl.pallas_call(
        functools.partial(_inproj_kernel, tn=tn, per_b=per_b),
        grid=(t // tm, PROJ_W // tn),
        in_specs=[pl.BlockSpec((tm, d), lambda i, j: (i, 0)),
                  pl.BlockSpec(mod.shape, lambda i, j: (0, 0)),
                  pl.BlockSpec((1, d), lambda i, j: (0, 0)),
                  pl.BlockSpec((d, tn), lambda i, j: (0, j)),
                  pl.BlockSpec((d, LANES), lambda i, j: (0, 0)),
                  pl.BlockSpec((tm, HEAD_DIM), lambda i, j: (i, 0)),
                  pl.BlockSpec((tm, HEAD_DIM), lambda i, j: (i, 0))],
        out_specs=[pl.BlockSpec((tm, tn), lambda i, j: (i, j)),
                   pl.BlockSpec((tm, LANES), lambda i, j: (i, 0)), hb_spec, hb_spec],
        out_shape=[jax.ShapeDtypeStruct((t, PROJ_W), BF16),
                   jax.ShapeDtypeStruct((t, LANES), F32), hb_shape, hb_shape],
        scratch_shapes=[pltpu.VMEM((tm, d), BF16), pltpu.VMEM((tm, HEAD_DIM), F32)],
        compiler_params=_cparams(("parallel", "arbitrary")),
        name="inproj",
    )(x2, mod, g_pre, w_in_p, w_gates, cos_t, sin_t)


def _compress_kernel(h_ref, w1_ref, pe_ref, w2_ref, o_ref):
    hb = h_ref[0]
    half = hb.shape[1]
    ya = _dot(hb, w1_ref[0:half, :])
    yb = _dot(hb, w1_ref[half:2 * half, :])
    const = _dot(pe_ref[...], w1_ref[...])
    n = ya.shape[0]
    yb_next = pltpu.roll(yb, n - 1, axis=0)
    hid = _silu(ya + yb_next + const[0:1, :])
    out = _dot(hid.astype(BF16), w2_ref[...])
    row = lax.broadcasted_iota(jnp.int32, out.shape, 0)
    o_ref[0] = jnp.where(row < n - 1, out, 0.0).astype(BF16)


def _compress(hblk, w1, pe, w2):
    bg, n, half = hblk.shape
    pe8 = jnp.zeros((8, 2 * half), BF16).at[0].set(pe.reshape(-1).astype(BF16))
    return pl.pallas_call(
        _compress_kernel,
        grid=(bg,),
        in_specs=[pl.BlockSpec((1, n, half), lambda i: (i, 0, 0)),
                  pl.BlockSpec((2 * half, HEAD_DIM), lambda i: (0, 0)),
                  pl.BlockSpec((8, 2 * half), lambda i: (0, 0)),
                  pl.BlockSpec((HEAD_DIM, HEAD_DIM), lambda i: (0, 0))],
        out_specs=pl.BlockSpec((1, n, HEAD_DIM), lambda i: (i, 0, 0)),
        out_shape=jax.ShapeDtypeStruct((bg, n, HEAD_DIM), BF16),
        compiler_params=_cparams(("parallel",)),
        name="compress",
    )(hblk, w1.astype(BF16), pe8, w2.astype(BF16))


def _cmp_kernel(q_ref, kc_ref, vc_ref, gates_ref, ov_ref, ocmp_ref, selb_ref, *, tq, nsel):
    g = pl.program_id(0) % N_KV_GROUPS
    q0 = pl.program_id(1) * tq
    kc = kc_ref[0]
    vc = vc_ref[0]
    nc = kc.shape[0]
    t_idx = q0 + lax.broadcasted_iota(jnp.int32, (tq, nc), 0)
    c_idx = lax.broadcasted_iota(jnp.int32, (tq, nc), 1)
    c_ok = c_idx * CMP_STRIDE + (CMP_BLOCK - 1) <= t_idx
    gl = gates_ref[...]
    lane = lax.broadcasted_iota(jnp.int32, (tq, LANES), 1)
    psum = jnp.zeros((tq, nc), F32)
    for h in range(HEADS_PER_GROUP):
        qh = q_ref[:, h * HEAD_DIM:(h + 1) * HEAD_DIM]
        s = jnp.where(c_ok, _dot_nt(qh, kc), -jnp.inf)
        m = jnp.max(s, axis=-1, keepdims=True)
        m = jnp.where(m == -jnp.inf, 0.0, m)
        p = jnp.exp2(s - m)
        p = p / jnp.maximum(jnp.sum(p, axis=-1, keepdims=True), 1e-30)
        psum = psum + p
        o = _dot(p.astype(BF16), vc)
        glog = _lane_col(gl, lane, (g * HEADS_PER_GROUP + h) * 3)
        ocmp_ref[:, h * HEAD_DIM:(h + 1) * HEAD_DIM] = (o * _sigmoid(glog)).astype(BF16)

    imp = _dot(psum, ov_ref[...], precision=lax.Precision.HIGHEST)
    cur = (q0 + lax.broadcasted_iota(jnp.int32, (tq, LANES), 0)) // SEL_BLOCK
    forced = (lane == 0) | (lane == cur) | (lane == cur - 1)
    imp = jnp.where(lane <= cur, imp + jnp.where(forced, FORCE_BONUS, 0.0), -jnp.inf)
    imp_t = imp.T[0:nsel, :]
    sub = 8
    groups = [imp_t[g * sub:(g + 1) * sub, :] for g in range(nsel // sub)]
    cnts = [jnp.zeros((sub, tq), F32) for _ in groups]
    row = lax.broadcasted_iota(jnp.int32, (sub, tq), 0)
    for k in range(nsel):
        vk = imp_t[k:k + 1, :]
        for g, grp in enumerate(groups):
            if k < g * sub:
                beats = vk >= grp
            elif k >= (g + 1) * sub:
                beats = vk > grp
            else:
                beats = (vk > grp) | ((vk == grp) & (row > k - g * sub))
            cnts[g] = cnts[g] + jnp.where(beats, 1.0, 0.0)
    cnt = jnp.concatenate(cnts, axis=0)
    sel = (cnt < float(N_SEL)) & (imp_t > -jnp.inf)
    bias_t = jnp.where(sel, 0.0, -MASK_BIG)
    if nsel < LANES:
        bias_t = jnp.concatenate([bias_t, jnp.zeros((LANES - nsel, tq), F32)], axis=0)
    selb_ref[...] = bias_t.T.astype(BF16)


def _cmp_attn(proj, kc, vc, gates, batch, seq, tq):
    t = proj.shape[0]
    nq = seq // tq
    nc = kc.shape[1]
    nsel = seq // SEL_BLOCK
    assert nsel <= LANES and nsel % 8 == 0
    c_start = np.arange(nc) * CMP_STRIDE
    j_start = np.arange(LANES) * SEL_BLOCK
    ov = ((c_start[:, None] < j_start[None, :] + SEL_BLOCK)
          & (c_start[:, None] + CMP_BLOCK > j_start[None, :])
          & (np.arange(LANES)[None, :] < nsel)).astype(np.float32)
    gq = HEADS_PER_GROUP * HEAD_DIM
    qmap = lambda bg, i: ((bg // N_KV_GROUPS) * nq + i, bg % N_KV_GROUPS)
    return pl.pallas_call(
        functools.partial(_cmp_kernel, tq=tq, nsel=nsel),
        grid=(batch * N_KV_GROUPS, nq),
        in_specs=[pl.BlockSpec((tq, gq), qmap),
                  pl.BlockSpec((1, nc, HEAD_DIM), lambda bg, i: (bg, 0, 0)),
                  pl.BlockSpec((1, nc, HEAD_DIM), lambda bg, i: (bg, 0, 0)),
                  pl.BlockSpec((tq, LANES), lambda bg, i: ((bg // N_KV_GROUPS) * nq + i, 0)),
                  pl.BlockSpec((nc, LANES), lambda bg, i: (0, 0))],
        out_specs=[pl.BlockSpec((tq, gq), qmap),
                   pl.BlockSpec((tq, LANES), lambda bg, i: (bg * nq + i, 0))],
        out_shape=[jax.ShapeDtypeStruct((t, ATT_W), BF16),
                   jax.ShapeDtypeStruct((batch * N_KV_GROUPS * seq, LANES), BF16)],
        compiler_params=_cparams(("parallel", "parallel")),
        name="cmp_attn",
    )(proj, kc, vc, gates, jnp.asarray(ov))


def _selwin_kernel(q_ref, selb_ref, ocmp_ref, gates_ref, ks_ref, vs_ref, kw_ref, vw_ref,
                   cb_ref, wb_ref, o_ref, kaug_s, vaug_s, vwaug_s, qaug_s, sbuf_s, m_s, acc_s,
                   swin_s, *, tq, tk, seq, nwin):
    g = pl.program_id(0) % N_KV_GROUPS
    qi = pl.program_id(1)
    q0 = qi * tq
    hq = HEADS_PER_GROUP * tq
    half = hq // 2

    @pl.when(qi == 0)
    def _():
        r = lax.broadcasted_iota(jnp.int32, (seq, HEAD_DIM), 0)
        ln = lax.broadcasted_iota(jnp.int32, (seq, HEAD_DIM), 1)
        ones_col = jnp.where(ln == 0, 1.0, 0.0).astype(BF16)
        kaug_s[:, 0:HEAD_DIM] = ks_ref[...]
        kaug_s[:, HEAD_DIM:2 * HEAD_DIM] = jnp.where(r // SEL_BLOCK == ln, 1.0, 0.0).astype(BF16)
        vaug_s[:, 0:HEAD_DIM] = vs_ref[...]
        vaug_s[:, HEAD_DIM:2 * HEAD_DIM] = ones_col
        vwaug_s[:, 0:HEAD_DIM] = vw_ref[...]
        vwaug_s[:, HEAD_DIM:2 * HEAD_DIM] = ones_col

    for h in range(HEADS_PER_GROUP):
        qaug_s[h * tq:(h + 1) * tq, 0:HEAD_DIM] = q_ref[:, h * HEAD_DIM:(h + 1) * HEAD_DIM]
        qaug_s[h * tq:(h + 1) * tq, HEAD_DIM:2 * HEAD_DIM] = selb_ref[...]
    qa = qaug_s[...]

    def scores(kt, slot):
        ks = kaug_s[pl.ds(pl.multiple_of(kt * tk, tk), tk), :]
        sbuf_s[slot] = _dot_nt(qa, ks)

    def absorb(kt, slot, diagonal=False):
        vs = vaug_s[pl.ds(pl.multiple_of(kt * tk, tk), tk), :]
        s = sbuf_s[slot]
        if diagonal:
            bias = cb_ref[(q0 - kt * tk) // tq]
            s = (s.reshape(HEADS_PER_GROUP, tq, tk) + bias[None]).reshape(hq, tk)
        m = m_s[...]
        m_new = jnp.maximum(m, jnp.max(s, axis=-1, keepdims=True))
        m_s[...] = m_new
        pb = jnp.exp2(s - m_new[:, 0:1]).astype(BF16)
        if diagonal:
            pv = jnp.concatenate([_dot(pb[0:half], vs), _dot(pb[half:hq], vs)], axis=0)
        else:
            pv = _dot(pb, vs)
        acc_s[...] = jnp.exp2(m - m_new)[:, 0:1] * acc_s[...] + pv

    n_full = q0 // tk
    m_s[...] = jnp.full(m_s.shape, -jnp.inf, F32)
    acc_s[...] = jnp.zeros(acc_s.shape, F32)
    scores(0, 0)

    def pair(j, c):
        scores(2 * j + 1, 1)
        absorb(2 * j, 0)
        scores(2 * j + 2, 0)
        absorb(2 * j + 1, 1)
        return c

    lax.fori_loop(0, n_full // 2, pair, 0)
    odd = n_full % 2

    @pl.when(odd == 1)
    def _():
        scores(n_full, 1)
        absorb(n_full - 1, 0)

    nq = seq // tq
    w0 = jnp.clip(qi - (nwin - 1), 0, nq - nwin) * tq
    wlen = nwin * tq
    kwin = kw_ref[pl.ds(pl.multiple_of(w0, tq), wlen), :]
    vwin = vwaug_s[pl.ds(pl.multiple_of(w0, tq), wlen), :]
    swin_s[0:half, :] = _dot_nt(qaug_s[0:half, 0:HEAD_DIM], kwin)
    swin_s[half:hq, :] = _dot_nt(qaug_s[half:hq, 0:HEAD_DIM], kwin)
    absorb(n_full, odd, diagonal=True)
    acc = acc_s[...]
    o_sel = acc[:, 0:HEAD_DIM] / acc[:, HEAD_DIM:HEAD_DIM + 1]

    wbias = wb_ref[(q0 - w0) // tq]
    sw = (swin_s[...].reshape(HEADS_PER_GROUP, tq, wlen) + wbias[None]).reshape(hq, wlen)
    pwb = jnp.exp2(sw - jnp.max(sw, axis=-1, keepdims=True)).astype(BF16)
    ow = jnp.concatenate([_dot(pwb[0:half], vwin), _dot(pwb[half:hq], vwin)], axis=0)
    o_win = ow[:, 0:HEAD_DIM] / ow[:, HEAD_DIM:HEAD_DIM + 1]

    gl = gates_ref[...]
    lane = lax.broadcasted_iota(jnp.int32, (tq, LANES), 1)
    for h in range(HEADS_PER_GROUP):
        gcol = (g * HEADS_PER_GROUP + h) * 3
        g_s = _sigmoid(_lane_col(gl, lane, gcol + 1))
        g_w = _sigmoid(_lane_col(gl, lane, gcol + 2))
        o = (ocmp_ref[:, h * HEAD_DIM:(h + 1) * HEAD_DIM].astype(F32)
             + g_s * o_sel[h * tq:(h + 1) * tq, :] + g_w * o_win[h * tq:(h + 1) * tq, :])
        o_ref[:, h * HEAD_DIM:(h + 1) * HEAD_DIM] = o.astype(BF16)


def _selwin_attn(proj, selb, ocmp, gates, batch, seq, tq, tk):
    t = proj.shape[0]
    nq = seq // tq
    assert tk % tq == 0 and tq & (tq - 1) == 0 and seq % tk == 0
    nwin = min(WINDOW // tq + 1, nq)
    assert WINDOW % tq == 0
    gq = HEADS_PER_GROUP * HEAD_DIM
    hq = HEADS_PER_GROUP * tq
    r = np.arange(tq)[:, None]
    cbias = np.stack([np.where(np.arange(tk)[None, :] <= r + off * tq, 0.0, -np.inf)
                      for off in range(tk // tq)]).astype(np.float32)
    dist = [r + off * tq - np.arange(nwin * tq)[None, :] for off in range(nwin)]
    wbias = np.stack([np.where((d >= 0) & (d < WINDOW), 0.0, -np.inf)
                      for d in dist]).astype(np.float32)
    qmap = lambda bg, i: ((bg // N_KV_GROUPS) * nq + i, bg % N_KV_GROUPS)

    def kvspec(col):
        return pl.BlockSpec((seq, HEAD_DIM),
                            lambda bg, i: (bg // N_KV_GROUPS, col // HEAD_DIM + bg % N_KV_GROUPS))

    return pl.pallas_call(
        functools.partial(_selwin_kernel, tq=tq, tk=tk, seq=seq, nwin=nwin),
        grid=(batch * N_KV_GROUPS, nq),
        in_specs=[pl.BlockSpec((tq, gq), qmap),
                  pl.BlockSpec((tq, LANES), lambda bg, i: (bg * nq + i, 0)),
                  pl.BlockSpec((tq, gq), qmap),
                  pl.BlockSpec((tq, LANES), lambda bg, i: ((bg // N_KV_GROUPS) * nq + i, 0)),
                  kvspec(COL_KS), kvspec(COL_VS), kvspec(COL_KW), kvspec(COL_VW),
                  pl.BlockSpec(cbias.shape, lambda bg, i: (0, 0, 0)),
                  pl.BlockSpec(wbias.shape, lambda bg, i: (0, 0, 0))],
        out_specs=pl.BlockSpec((tq, gq), qmap),
        out_shape=jax.ShapeDtypeStruct((t, ATT_W), BF16),
        scratch_shapes=[pltpu.VMEM((seq, 2 * HEAD_DIM), BF16),
                        pltpu.VMEM((seq, 2 * HEAD_DIM), BF16),
                        pltpu.VMEM((seq, 2 * HEAD_DIM), BF16),
                        pltpu.VMEM((hq, 2 * HEAD_DIM), BF16),
                        pltpu.VMEM((2, hq, tk), F32),
                        pltpu.VMEM((hq, LANES), F32),
                        pltpu.VMEM((hq, 2 * HEAD_DIM), F32),
                        pltpu.VMEM((hq, nwin * tq), F32)],
        compiler_params=_cparams(("parallel", "arbitrary")),
        name="selwin_attn",
    )(proj, selb, ocmp, gates, proj, proj, proj, proj, jnp.asarray(cbias), jnp.asarray(wbias))


_HGRN_LEVELS = (32, 16, 8, 4, 2, 1)


def _hgrn_tables():
    c = HGRN_CHUNK
    nl = len(_HGRN_LEVELS)
    idx = np.arange(c)
    i, j = idx[:, None], idx[None, :]
    mats = [j <= i, j > i]
    lvl = np.full((c, c), nl + 1, np.int32)
    for li, s in enumerate(_HGRN_LEVELS):
        mats.append((j >= (i // s) * s) & (j <= i))
        mats.append((j > i) & (j <= (i // s) * s + s - 1))
        lvl[((i // (2 * s)) == (j // (2 * s))) & (((i // s) % 2) == 1) & (((j // s) % 2) == 0)] = li
    lvl[i == j] = nl
    w = np.concatenate(mats, 0).astype(np.float32)
    return np.concatenate([w, w], axis=1), lvl


def _hgrn_head(hq_ref, hf_ref, hi_ref, hg_ref, lbl_ref, gn_ref, w_ref, lvl_ref, o_ref, st_s,
               hh, ts):
    c = HGRN_CHUNK
    nc = ts // c
    nl = len(_HGRN_LEVELS)
    cs = slice(hh * HEAD_DIM, (hh + 1) * HEAD_DIM)
    lbl = lbl_ref[:, cs]
    e = jnp.exp(lbl - jnp.max(lbl, axis=0, keepdims=True))
    lb = e[0:1, :] / jnp.sum(e, axis=0, keepdims=True)

    f = lb + (1.0 - lb) * _sigmoid(hf_ref[:, cs].astype(F32))
    k = 1.0 - f
    qf = _silu(hq_ref[:, cs].astype(F32))
    v = hi_ref[:, cs]

    lf = jnp.log(f) * LOG2E
    lf_hi = lf.astype(BF16)
    lf_lo = (lf - lf_hi.astype(F32)).astype(BF16)
    side = lambda a: jnp.concatenate([a[ci * c:(ci + 1) * c, :] for ci in range(nc)], axis=1)
    ex = _dot(w_ref[...], jnp.concatenate([side(lf_hi), side(lf_lo)], axis=0))

    def rows(bi):
        return jnp.concatenate(
            [ex[bi * c:(bi + 1) * c, ci * HEAD_DIM:(ci + 1) * HEAD_DIM] for ci in range(nc)], axis=0)

    b = rows(0)
    qhat = (qf * jnp.exp2(b)).astype(BF16)
    khat = (k * jnp.exp2(rows(1))).astype(BF16)
    lvl = lvl_ref[...]
    masks = [lvl == li for li in range(nl + 1)]
    qs = [(qf * jnp.exp2(rows(2 + 2 * li))).astype(BF16) for li in range(nl)] + [qf.astype(BF16)]
    ks = [(k * jnp.exp2(rows(3 + 2 * li))).astype(BF16) for li in range(nl)] + [k.astype(BF16)]
    o_intra = []
    for ci in range(nc):
        sl = slice(ci * c, (ci + 1) * c)
        a = jnp.zeros((c, c), F32)
        for li in range(nl + 1):
            a = jnp.where(masks[li], _dot_nt(qs[li][sl, :], ks[li][sl, :]), a)
        o_intra.append(_dot(a.astype(BF16), v[sl, :]))

    st = st_s[hh]
    outs = []
    for ci in range(nc):
        sl = slice(ci * c, (ci + 1) * c)
        outs.append(o_intra[ci] + _dot_nt(qhat[sl, :], st.astype(BF16)))
        st = st * jnp.exp2(b[(ci + 1) * c - 1:(ci + 1) * c, :]) + lax.dot_general(
            v[sl, :], khat[sl, :], _TN, preferred_element_type=F32)
    st_s[hh] = st
    o = jnp.concatenate(outs, axis=0)
    o = o * lax.rsqrt(jnp.mean(o * o, axis=-1, keepdims=True) + EPS) * gn_ref[:, cs]
    o_ref[:, cs] = (o * _silu(hg_ref[:, cs].astype(F32))).astype(BF16)


def _hgrn_kernel(hq_ref, hf_ref, hi_ref, hg_ref, lbl_ref, gn_ref, w_ref, lvl_ref,
                 o_ref, st_s, *, ts, hb):
    @pl.when(pl.program_id(2) == 0)
    def _():
        st_s[...] = jnp.zeros_like(st_s)

    for hh in range(hb):
        _hgrn_head(hq_ref, hf_ref, hi_ref, hg_ref, lbl_ref, gn_ref, w_ref, lvl_ref, o_ref, st_s,
                   hh, ts)


def _hgrn(proj, lb_logits, g_norm, batch, seq, ts, hb):
    t = proj.shape[0]
    ns = seq // ts
    wmat, lvl = _hgrn_tables()
    w = hb * HEAD_DIM
    assert all(col % w == 0 for col in (COL_HQ, COL_HF, COL_HI, COL_HG))

    def colspec(col):
        return pl.BlockSpec((ts, w), lambda b, h, s: (b * ns + s, col // w + h))

    nl = lb_logits.shape[0]
    return pl.pallas_call(
        functools.partial(_hgrn_kernel, ts=ts, hb=hb),
        grid=(batch, N_HGRN_HEADS // hb, ns),
        in_specs=[colspec(COL_HQ), colspec(COL_HF), colspec(COL_HI), colspec(COL_HG),
                  pl.BlockSpec((nl, w), lambda b, h, s: (0, h)),
                  pl.BlockSpec((1, w), lambda b, h, s: (0, h)),
                  pl.BlockSpec(wmat.shape, lambda b, h, s: (0, 0)),
                  pl.BlockSpec(lvl.shape, lambda b, h, s: (0, 0))],
        out_specs=pl.BlockSpec((ts, w), lambda b, h, s: (b * ns + s, h)),
        out_shape=jax.ShapeDtypeStruct((t, ATT_W), BF16),
        scratch_shapes=[pltpu.VMEM((hb, HEAD_DIM, HEAD_DIM), F32)],
        compiler_params=_cparams(("parallel", "parallel", "arbitrary")),
        name="hgrn2",
    )(proj, proj, proj, proj, lb_logits, g_norm, jnp.asarray(wmat, BF16), jnp.asarray(lvl))


def _outproj_kernel(oa_ref, or_ref, wa_ref, wb_ref, x_ref, mod_ref, gpost_ref, gpre_ref,
                    wr_ref, br_ref, x1_ref, h2_ref, lg_ref, stage, *, per_b):
    b = pl.program_id(0) // per_b
    y = _dot(oa_ref[...], wa_ref[...]) + _dot(or_ref[...], wb_ref[...])
    x1 = x_ref[...] + _mod_row(mod_ref, b, 2) * _rms(y, gpost_ref[...])
    x1_ref[...] = x1
    h = _rms(x1, gpre_ref[...]) * (1.0 + _mod_row(mod_ref, b, 4)) + _mod_row(mod_ref, b, 3)
    h2_ref[...] = _to_slabs(stage, h)
    h_hi = h.astype(BF16)
    h_lo = (h - h_hi.astype(F32)).astype(BF16)
    hh = _dot(h_hi, wr_ref[...])
    lg_ref[...] = (hh[:, 0:LANES] + hh[:, LANES:2 * LANES]
                   + _dot(h_lo, wr_ref[:, 0:LANES]) + br_ref[...])


def _outproj(o_att, o_rec, w_out, x2, mod, g_post, g_pre, wr, br, seq, tm):
    t, d = x2.shape
    per_b = seq // tm
    wr_hi = wr.astype(BF16)
    wr_cat = jnp.concatenate([wr_hi, (wr - wr_hi.astype(F32)).astype(BF16)], axis=1)
    wa = w_out[:ATT_W].astype(BF16)
    wb = w_out[ATT_W:].astype(BF16)
    row = lambda i: (i, 0)
    fixed = lambda i: (0, 0)
    return pl.pallas_call(
        functools.partial(_outproj_kernel, per_b=per_b),
        grid=(t // tm,),
        in_specs=[pl.BlockSpec((tm, ATT_W), row), pl.BlockSpec((tm, ATT_W), row),
                  pl.BlockSpec((ATT_W, d), fixed), pl.BlockSpec((ATT_W, d), fixed),
                  pl.BlockSpec((tm, d), row),
                  pl.BlockSpec(mod.shape, fixed),
                  pl.BlockSpec((1, d), fixed), pl.BlockSpec((1, d), fixed),
                  pl.BlockSpec((d, 2 * LANES), fixed), pl.BlockSpec((1, LANES), fixed)],
        out_specs=[pl.BlockSpec((tm, d), row), pl.BlockSpec((tm * (d // LANES), LANES), row),
                   pl.BlockSpec((tm, LANES), row)],
        out_shape=[jax.ShapeDtypeStruct((t, d), F32),
                   jax.ShapeDtypeStruct((t * (d // LANES), LANES), BF16),
                   jax.ShapeDtypeStruct((t, LANES), F32)],
        scratch_shapes=[pltpu.VMEM((tm * (d // LANES), LANES), F32)],
        compiler_params=_cparams(("parallel",)),
        name="outproj",
    )(o_att, o_rec, wa, wb, x2, mod, g_post, g_pre, wr_cat, br)


G_LANE0 = N_EXPERTS


def _route_kernel(lg_ref, tri_ref, info_ref, cnt_ref, carry_s):
    @pl.when(pl.program_id(0) == 0)
    def _():
        carry_s[...] = jnp.zeros_like(carry_s)

    x = lg_ref[...]
    lane = lax.broadcasted_iota(jnp.int32, x.shape, 1)
    big = jnp.int32(10 ** 6)
    rmax = lambda a: jnp.max(a, axis=-1, keepdims=True)
    rmin = lambda a: jnp.min(a, axis=-1, keepdims=True)
    rsum = lambda a: jnp.sum(a, axis=-1, keepdims=True)

    is_g = (lane >= G_LANE0) & (lane < G_LANE0 + N_EXPERT_GROUPS)
    lgm = jnp.where(is_g, x, -jnp.inf)
    mg = rmax(lgm)
    pg_sel = 1.0 / rsum(jnp.where(is_g, jnp.exp(lgm - mg), 0.0))
    gsel = rmin(jnp.where(lgm == mg, lane, big)) - G_LANE0

    is_e = (lane >= gsel * EXPERTS_PER_GROUP) & (lane < (gsel + 1) * EXPERTS_PER_GROUP)
    lem = jnp.where(is_e, x, -jnp.inf)
    pe = jnp.where(is_e, jnp.exp(lem - rmax(lem)), 0.0)
    pe = pe / rsum(pe)
    pe = jnp.where(is_e, pe, -1.0)
    v1 = rmax(pe)
    i1 = rmin(jnp.where(pe == v1, lane, big))
    pe2 = jnp.where(lane == i1, -1.0, pe)
    v2 = rmax(pe2)
    i2 = rmin(jnp.where(pe2 == v2, lane, big))
    w1 = v1 / (v1 + v2) * pg_sel
    w2 = v2 / (v1 + v2) * pg_sel

    oh1 = jnp.where(lane == i1, 1.0, 0.0)
    oh2 = jnp.where(lane == i2, 1.0, 0.0)
    both = oh1 + oh2
    before = _dot(tri_ref[...], both.astype(BF16)) + carry_s[0:1, :]
    r1 = rsum(oh1 * before)
    r2 = rsum(oh2 * before)
    carry_s[0:1, :] = carry_s[0:1, :] + jnp.sum(both, axis=0, keepdims=True)
    cnt_ref[...] = carry_s[...]

    cols = (i1.astype(F32), i2.astype(F32), w1, w2, r1, r2)
    info = jnp.zeros(x.shape, F32)
    for ci, col in enumerate(cols):
        info = jnp.where(lane == ci, col, info)
    info_ref[...] = info


def _route(logits, tm):
    t = logits.shape[0]
    tri = jnp.asarray(np.tril(np.ones((tm, tm), np.float32), -1), BF16)
    return pl.pallas_call(
        _route_kernel,
        grid=(t // tm,),
        in_specs=[pl.BlockSpec((tm, LANES), lambda i: (i, 0)),
                  pl.BlockSpec((tm, tm), lambda i: (0, 0))],
        out_specs=[pl.BlockSpec((tm, LANES), lambda i: (i, 0)),
                   pl.BlockSpec((8, LANES), lambda i: (0, 0))],
        out_shape=[jax.ShapeDtypeStruct((t, LANES), F32),
                   jax.ShapeDtypeStruct((8, LANES), F32)],
        scratch_shapes=[pltpu.VMEM((8, LANES), F32)],
        compiler_params=_cparams(("arbitrary",)),
        name="route",
    )(logits, tri)


def _to_slabs(stage, x):
    n, d = x.shape
    slab = d // LANES
    for c in range(slab):
        stage[pl.ds(c, n, stride=slab), :] = x[:, c * LANES:(c + 1) * LANES]
    return stage[...].astype(BF16)


def _from_slabs(stage, slabs, n):
    slab = slabs.shape[0] // n
    stage[...] = slabs.astype(F32)
    return jnp.concatenate([stage[pl.ds(c, n, stride=slab), :] for c in range(slab)], axis=1)


def _expert_kernel(te_ref, nu_ref, dst_ref, xs_ref, wg_ref, wu_ref, wd_ref,
                   y_ref, obuf, stage, ssem, wg_s, wu_s, wd_s, *, tm, standin_row):
    t = pl.program_id(0)
    nu = nu_ref[0]
    slot = t % 2
    slab = wg_s.shape[0] // LANES
    tile = tm * slab

    def token(ref, i):
        return ref.at[pl.ds(pl.multiple_of(i * slab, slab), slab)]

    def scatter(s):
        for r in range(tm):
            pltpu.make_async_copy(token(obuf, s * tm + r), token(y_ref, dst_ref[0, 0, r]),
                                  ssem.at[s]).start()

    def tile_of(buf, s):
        return buf.at[pl.ds(pl.multiple_of(s * tile, tile), tile)]

    def wait_tile(s):
        pltpu.make_async_copy(tile_of(obuf, s), tile_of(obuf, s), ssem.at[s]).wait()

    @pl.when(t < nu)
    def _():
        @pl.when(t == 0)
        def _():
            obuf[tile:2 * tile, :] = jnp.zeros((tile, LANES), obuf.dtype)

            def fill(row0, s):
                return pltpu.make_async_copy(tile_of(obuf, 1),
                                             y_ref.at[pl.ds(row0 * slab, tile)], ssem.at[s])

            fill(standin_row - 2 * tm, 0).start()
            fill(standin_row - tm, 0).start()
            fill(standin_row - 2 * tm, 0).wait()
            fill(standin_row - tm, 0).wait()
            fill(standin_row, 1).start()

        @pl.when((t == 0) | (te_ref[t] != te_ref[jnp.maximum(t - 1, 0)]))
        def _():
            wg_s[...] = wg_ref[0].astype(BF16)
            wu_s[...] = wu_ref[0].astype(BF16)
            wd_s[...] = wd_ref[0].astype(BF16)

        x = _from_slabs(stage, xs_ref[...], tm).astype(BF16)
        hid = (_silu(_dot(x, wg_s[...])) * _dot(x, wu_s[...])).astype(BF16)
        row0 = pl.multiple_of(slot * tile, tile)
        obuf[pl.ds(row0, tile), :] = _to_slabs(stage, _dot(hid, wd_s[...]))
        scatter(slot)
        wait_tile(1 - slot)

        @pl.when(t == nu - 1)
        def _():
            wait_tile(slot)


def _experts(tile_expert, n_used, dst3, xs, w_gate, w_up, w_down, y_rows, tm, standin_row):
    n_tiles = dst3.shape[0]
    d, f = w_gate.shape[1:]
    slab = d // LANES
    wmap = lambda t, te, nu: (te[t], 0, 0)
    return pl.pallas_call(
        functools.partial(_expert_kernel, tm=tm, standin_row=standin_row),
        grid_spec=pltpu.PrefetchScalarGridSpec(
            num_scalar_prefetch=2,
            grid=(n_tiles,),
            in_specs=[pl.BlockSpec((1, 1, tm), lambda t, te, nu: (t, 0, 0),
                                   memory_space=pltpu.SMEM),
                      pl.BlockSpec((tm * slab, LANES),
                                   lambda t, te, nu: (jnp.minimum(t, nu[0] - 1), 0)),
                      pl.BlockSpec((1, d, f), wmap),
                      pl.BlockSpec((1, d, f), wmap),
                      pl.BlockSpec((1, f, d), wmap)],
            out_specs=pl.BlockSpec(memory_space=pl.ANY),
            scratch_shapes=[pltpu.VMEM((2 * tm * slab, LANES), BF16),
                            pltpu.VMEM((tm * slab, LANES), F32),
                            pltpu.SemaphoreType.DMA((2,)),
                            pltpu.VMEM((d, f), BF16), pltpu.VMEM((d, f), BF16),
                            pltpu.VMEM((f, d), BF16)]),
        out_shape=jax.ShapeDtypeStruct((y_rows * slab, LANES), BF16),
        compiler_params=_cparams(("arbitrary",)),
        name="moe_experts",
    )(tile_expert, n_used, dst3, xs, w_gate, w_up, w_down)


def _dispatch_kernel(pos0_ref, pos1_ref, end_ref, nu_ref, h_ref, xs_ref, zbuf, zsem, sem, *,
                     tm, slab, n_tiles):
    tile = tm * slab

    @pl.when(pl.program_id(0) == 0)
    def _():
        zbuf[...] = jnp.zeros(zbuf.shape, zbuf.dtype)

        def zero(row_end, go):
            cp = pltpu.make_async_copy(
                zbuf, xs_ref.at[pl.ds(pl.multiple_of((row_end - tm) * slab, slab), tile)], zsem)
            pl.when(go)(cp.start)
            return cp, go

        pending = [zero(end_ref[e], end_ref[e] > (end_ref[e - 1] if e else 0))
                   for e in range(N_EXPERTS)]
        pending += [zero((i + 1) * tm, i >= nu_ref[0]) for i in range(n_tiles)]
        for cp, go in pending:
            pl.when(go)(cp.wait)

    def token(ref, i):
        return ref.at[pl.ds(pl.multiple_of(i * slab, slab), slab)]

    def start(r, c):
        pltpu.make_async_copy(token(h_ref, r), token(xs_ref, pos0_ref[0, 0, r]), sem).start()
        pltpu.make_async_copy(token(h_ref, r), token(xs_ref, pos1_ref[0, 0, r]), sem).start()
        return c

    lax.fori_loop(0, tm, start, 0, unroll=8)
    for _ in range(2):
        pltpu.make_async_copy(h_ref, h_ref, sem).wait()


def _dispatch(pos0, pos1, tile_end, n_used, h2, n_tiles, tm_exp, tm):
    rows = h2.shape[0]
    slab = rows // pos0.shape[0]
    t = pos0.shape[0]
    assert tm == tm_exp
    per_tile = pl.BlockSpec((1, 1, tm), lambda i: (i, 0, 0), memory_space=pltpu.SMEM)
    whole = pl.BlockSpec(memory_space=pltpu.SMEM)
    return pl.pallas_call(
        functools.partial(_dispatch_kernel, tm=tm, slab=slab, n_tiles=n_tiles),
        grid=(t // tm,),
        in_specs=[per_tile, per_tile, whole, whole,
                  pl.BlockSpec((tm * slab, LANES), lambda i: (i, 0))],
        out_specs=pl.BlockSpec(memory_space=pl.ANY),
        out_shape=jax.ShapeDtypeStruct((n_tiles * tm_exp * slab, LANES), h2.dtype),
        scratch_shapes=[pltpu.VMEM((tm * slab, LANES), h2.dtype),
                        pltpu.SemaphoreType.DMA(()), pltpu.SemaphoreType.DMA(())],
        compiler_params=_cparams(("arbitrary",)),
        name="moe_dispatch",
    )(pos0.reshape(t // tm, 1, tm), pos1.reshape(t // tm, 1, tm), tile_end * tm_exp, n_used, h2)


def _final_kernel(info_ref, x1_ref, mod_ref, g_ref, ya_ref, yb_ref, o_ref, stage_a, stage_b, *,
                  per_b):
    info = info_ref[...]
    tm = x1_ref.shape[0]
    ya = _from_slabs(stage_a, ya_ref[...], tm)
    yb = _from_slabs(stage_b, yb_ref[...], tm)
    y = info[:, 2:3] * ya + info[:, 3:4] * yb
    gate = _mod_row(mod_ref, pl.program_id(0) // per_b, 5)
    o_ref[...] = x1_ref[...] + gate * _rms(y, g_ref[...])


def _moe_final(info, x1, mod, g_post, y2, seq, tm):
    t, d = x1.shape
    per_b = seq // tm
    nb = t // tm
    return pl.pallas_call(
        functools.partial(_final_kernel, per_b=per_b),
        grid=(nb,),
        in_specs=[pl.BlockSpec((tm, LANES), lambda i: (i, 0)),
                  pl.BlockSpec((tm, d), lambda i: (i, 0)),
                  pl.BlockSpec(mod.shape, lambda i: (0, 0)),
                  pl.BlockSpec((1, d), lambda i: (0, 0)),
                  pl.BlockSpec((tm * (d // LANES), LANES), lambda i: (i, 0)),
                  pl.BlockSpec((tm * (d // LANES), LANES), lambda i: (nb + i, 0))],
        out_specs=pl.BlockSpec((tm, d), lambda i: (i, 0)),
        out_shape=jax.ShapeDtypeStruct((t, d), F32),
        scratch_shapes=[pltpu.VMEM((tm * (d // LANES), LANES), F32)] * 2,
        compiler_params=_cparams(("parallel",)),
        name="moe_final",
    )(info, x1, mod, g_post, y2, y2)


def _slot_map_kernel(pos0_ref, pos1_ref, dst_in, dst_ref, sem, *, tm, n_tokens):
    @pl.when(pl.program_id(0) == 0)
    def _():
        cp = pltpu.make_async_copy(dst_in, dst_ref, sem)
        cp.start()
        cp.wait()

    base = pl.program_id(0) * tm

    def body(i, c):
        dst_ref[pos0_ref[0, 0, i]] = base + i
        dst_ref[pos1_ref[0, 0, i]] = n_tokens + base + i
        return c

    lax.fori_loop(0, tm, body, 0, unroll=8)


def _slot_map(pos0, pos1, dst0, tm):
    t = pos0.shape[0]
    per_tile = pl.BlockSpec((1, 1, tm), lambda i: (i, 0, 0), memory_space=pltpu.SMEM)
    return pl.pallas_call(
        functools.partial(_slot_map_kernel, tm=tm, n_tokens=t),
        grid=(t // tm,),
        in_specs=[per_tile, per_tile, pl.BlockSpec(memory_space=pl.ANY)],
        out_specs=pl.BlockSpec(memory_space=pltpu.SMEM),
        out_shape=jax.ShapeDtypeStruct(dst0.shape, jnp.int32),
        scratch_shapes=[pltpu.SemaphoreType.DMA(())],
        compiler_params=_cparams(("arbitrary",)),
        name="slot_map",
    )(pos0.reshape(t // tm, 1, tm), pos1.reshape(t // tm, 1, tm), dst0)


def _moe(h2, logits, x1, mod, g_post, w_gate, w_up, w_down, seq, tm_route, tm_exp, tm_fin):
    t = x1.shape[0]
    info, cnt = _route(logits, tm_route)
    counts = cnt[0, :N_EXPERTS].astype(jnp.int32)
    tiles_e = (counts + tm_exp - 1) // tm_exp
    tile_end = jnp.cumsum(tiles_e)
    offs = (tile_end - tiles_e) * tm_exp
    n_used = tile_end[-1]

    def slot_of(k):
        e = info[:, k].astype(jnp.int32)
        start = jnp.sum(jnp.where(e[:, None] == jnp.arange(N_EXPERTS)[None, :], offs[None, :], 0),
                        axis=1)
        return start + info[:, 4 + k].astype(jnp.int32)

    n_tiles = (2 * t) // tm_exp + N_EXPERTS
    tidx = jnp.minimum(jnp.arange(n_tiles, dtype=jnp.int32), n_used - 1)
    tile_expert = jnp.sum(tidx[:, None] >= tile_end[None, :], axis=1).astype(jnp.int32)
    pos0, pos1 = slot_of(0), slot_of(1)
    slot_ids = jnp.arange(n_tiles * tm_exp, dtype=jnp.int32)
    spare = 2 * t + ((slot_ids // tm_exp) % 2) * tm_exp + slot_ids % tm_exp
    dst = _slot_map(pos0, pos1, spare, tm_route)
    n_used1 = n_used.reshape(1).astype(jnp.int32)
    xs = _dispatch(pos0, pos1, tile_end.astype(jnp.int32), n_used1, h2, n_tiles, tm_exp, tm_route)

    ew = w_gate.shape
    wg = w_gate.reshape(N_EXPERTS, ew[-2], ew[-1])
    wu = w_up.reshape(N_EXPERTS, ew[-2], ew[-1])
    wd = w_down.reshape(N_EXPERTS, ew[-1], ew[-2])
    y2 = _experts(tile_expert, n_used1, dst.reshape(n_tiles, 1, tm_exp), xs, wg, wu, wd,
                  2 * t + 3 * tm_exp, tm_exp, 2 * t + 2 * tm_exp)
    return _moe_final(info, x1, mod, g_post, y2, seq, tm_fin)


def _permute_w_in(w_in):
    sizes = [ATT_W] + [KV_W] * 6 + [3 * N_ATT_HEADS] + [ATT_W] * 4
    cuts = np.cumsum(sizes)[:-1]
    q, kc, vc, ks, vs, kw, vw, gates, hq, hf, hi, hg = [
        p.astype(BF16) for p in jnp.split(w_in, cuts, axis=1)]
    w_p = jnp.concatenate([q, kc, ks, kw, vc, vs, vw, hq, hf, hi, hg], axis=1)
    w_g = jnp.pad(gates, ((0, 0), (0, LANES - gates.shape[1])))
    return w_p, w_g


def _block(x, c, positions, w_ada, b_ada, g_pre_mix, g_post_mix, g_pre_ffn, g_post_ffn,
           w_in, w_out, pe_k, w1_k, w2_k, pe_v, w1_v, w2_v, lb_logits, g_norm,
           w_group, b_group, w_router, b_router, w_gate, w_up, w_down, cfg):
    batch, seq, d = x.shape
    t = batch * seq
    x2 = x.reshape(t, d)

    c8 = jnp.zeros((8, d), F32).at[:batch].set(c)
    mod = _ada_mod(c8, w_ada, b_ada)
    cos_t, sin_t = _rope_tables(positions, cfg["tm_rope"])
    w_p, w_g = _permute_w_in(w_in)
    proj, gates, hbk, hbv = _inproj(x2, mod, g_pre_mix.reshape(1, d), w_p, w_g, cos_t, sin_t,
                                    seq, cfg["tm_in"])

    half_blocks = (batch * N_KV_GROUPS, seq // CMP_STRIDE, CMP_STRIDE * HEAD_DIM)
    kc = _compress(hbk.reshape(half_blocks), w1_k, pe_k, w2_k)
    vc = _compress(hbv.reshape(half_blocks), w1_v, pe_v, w2_v)
    ocmp, selb = _cmp_attn(proj, kc, vc, gates, batch, seq, cfg["tq_cmp"])
    o_att = _selwin_attn(proj, selb, ocmp, gates, batch, seq, cfg["tq"], cfg["tk"])
    o_rec = _hgrn(proj, lb_logits, g_norm.reshape(1, -1), batch, seq, cfg["ts_hgrn"],
                  cfg["hb_hgrn"])

    wr = jnp.concatenate([w_router, w_group], axis=1)
    wr = jnp.pad(wr, ((0, 0), (0, LANES - wr.shape[1])))
    br = jnp.pad(jnp.concatenate([b_router, b_group]), (0, LANES - N_EXPERTS - N_EXPERT_GROUPS))
    x1, h2, logits = _outproj(o_att, o_rec, w_out, x2, mod, g_post_mix.reshape(1, d),
                              g_pre_ffn.reshape(1, d), wr, br.reshape(1, LANES), seq,
                              cfg["tm_out"])
    out = _moe(h2, logits, x1, mod, g_post_ffn.reshape(1, d), w_gate, w_up, w_down, seq,
               cfg["tm_route"], cfg["tm_exp"], cfg["tm_fin"])
    return out.reshape(batch, seq, d)


def _config(seq):
    return dict(tm_rope=min(1024, seq), tm_in=min(1024, seq), tq_cmp=min(256, seq),
                tq=min(256, seq), tk=min(512, seq), ts_hgrn=min(256, seq), hb_hgrn=4,
                tm_out=min(256, seq), tm_route=min(256, seq), tm_exp=256,
                tm_fin=min(256, seq))


def kernel(x, c, positions, w_ada, b_ada, g_pre_mix, g_post_mix, g_pre_ffn, g_post_ffn, w_in, w_out, cmp_pe_k, cmp_w1_k, cmp_w2_k, cmp_pe_v, cmp_w1_v, cmp_w2_v, hgrn_lb_logits, hgrn_g_norm, w_group, b_group, w_router, b_router, w_gate, w_up, w_down):
    assert w_ada.shape[0] == 1, "single-layer block"
    return _block(x, c, positions, w_ada[0], b_ada[0], g_pre_mix[0], g_post_mix[0],
                  g_pre_ffn[0], g_post_ffn[0], w_in[0], w_out[0], cmp_pe_k[0], cmp_w1_k[0],
                  cmp_w2_k[0], cmp_pe_v[0], cmp_w1_v[0], cmp_w2_v[0], hgrn_lb_logits,
                  hgrn_g_norm[0], w_group[0], b_group[0], w_router[0], b_router[0],
                  w_gate[0], w_up[0], w_down[0], _config(x.shape[1]))
```

```python
import functools

import numpy as np
import jax
import jax.numpy as jnp
from jax import lax
from jax.experimental import pallas as pl
from jax.experimental.pallas import tpu as pltpu

F32 = jnp.float32
BF16 = jnp.bfloat16

HEAD_DIM = 128
N_KV_GROUPS = 2
HEADS_PER_GROUP = 4
N_ATT_HEADS = N_KV_GROUPS * HEADS_PER_GROUP
CMP_BLOCK = 32
CMP_STRIDE = 16
SEL_BLOCK = 64
N_SEL = 16
WINDOW = 512
FORCE_BONUS = 1.0e4
ROPE_THETA = 10000.0
N_HGRN_HEADS = 8
HGRN_CHUNK = 64
N_EXPERT_GROUPS = 4
EXPERTS_PER_GROUP = 8
N_EXPERTS = N_EXPERT_GROUPS * EXPERTS_PER_GROUP
EPS = 1e-6

LANES = 128
VMEM_LIMIT = 56 * 1024 * 1024

ATT_W = N_ATT_HEADS * HEAD_DIM
KV_W = N_KV_GROUPS * HEAD_DIM
COL_Q = 0
COL_KC = COL_Q + ATT_W
COL_KS = COL_KC + KV_W
COL_KW = COL_KS + KV_W
COL_VC = COL_KW + KV_W
COL_VS = COL_VC + KV_W
COL_VW = COL_VS + KV_W
COL_HQ = COL_VW + KV_W
COL_HF = COL_HQ + ATT_W
COL_HI = COL_HF + ATT_W
COL_HG = COL_HI + ATT_W
PROJ_W = COL_HG + ATT_W
ROPE_W = COL_VC

MASK_BIG = float(2 ** 30)
LOG2E = 1.4426950408889634
Q_PRESCALE = HEAD_DIM ** -0.5 * LOG2E

_NT = (((1,), (1,)), ((), ()))
_TN = (((0,), (0,)), ((), ()))


def _cparams(sem):
    return pltpu.CompilerParams(dimension_semantics=sem, vmem_limit_bytes=VMEM_LIMIT)


def _dot(a, b, **kw):
    return jnp.dot(a, b, preferred_element_type=F32, **kw)


def _dot_nt(a, b):
    return lax.dot_general(a, b, _NT, preferred_element_type=F32)


def _sigmoid(x):
    return 1.0 / (1.0 + jnp.exp(-x))


def _silu(x):
    return x * _sigmoid(x)


def _rms(x, g):
    return x * lax.rsqrt(jnp.mean(x * x, axis=-1, keepdims=True) + EPS) * g


def _mod_row(mod_ref, b, k):
    d = mod_ref.shape[1] // 6
    return mod_ref[pl.ds(b, 1), k * d:(k + 1) * d]


def _lane_col(x, lane, col):
    return jnp.sum(jnp.where(lane == col, x, 0.0), axis=-1, keepdims=True)


def _ada_kernel(c_ref, w_ref, b_ref, o_ref):
    s = _silu(c_ref[...])
    o_ref[...] = _dot(s, w_ref[...], precision=lax.Precision.HIGHEST) + b_ref[...]


def _ada_mod(c8, w_ada, b_ada, tn=512):
    rows, d = c8.shape
    n = w_ada.shape[1]
    return pl.pallas_call(
        _ada_kernel,
        grid=(n // tn,),
        in_specs=[pl.BlockSpec((rows, d), lambda j: (0, 0)),
                  pl.BlockSpec((d, tn), lambda j: (0, j)),
                  pl.BlockSpec((1, tn), lambda j: (0, j))],
        out_specs=pl.BlockSpec((rows, tn), lambda j: (0, j)),
        out_shape=jax.ShapeDtypeStruct((rows, n), F32),
        compiler_params=_cparams(("parallel",)),
        name="ada_mod",
    )(c8, w_ada, b_ada.reshape(1, n))


def _rope_kernel(ang_ref, cos_ref, sin_ref):
    a = ang_ref[...]
    lane = lax.broadcasted_iota(jnp.int32, a.shape, 1)
    cos_ref[...] = jnp.cos(a)
    sin_ref[...] = jnp.where(lane < HEAD_DIM // 2, -jnp.sin(a), jnp.sin(a))


def _rope_tables(positions, tm):
    t = positions.size
    inv_freq = ROPE_THETA ** (-jnp.arange(0, HEAD_DIM, 2, dtype=F32) / HEAD_DIM)
    ang = positions.reshape(t, 1).astype(F32) * jnp.concatenate([inv_freq, inv_freq])[None, :]
    spec = pl.BlockSpec((tm, HEAD_DIM), lambda i: (i, 0))
    return pl.pallas_call(
        _rope_kernel,
        grid=(t // tm,),
        in_specs=[spec],
        out_specs=[spec, spec],
        out_shape=[jax.ShapeDtypeStruct((t, HEAD_DIM), F32)] * 2,
        compiler_params=_cparams(("parallel",)),
        name="rope_tables",
    )(ang)


def _inproj_kernel(x_ref, mod_ref, g_ref, w_ref, wg_ref, cos_ref, sin_ref,
                   proj_ref, gates_ref, hbk_ref, hbv_ref, h_s, stage, *, tn, per_b):
    j = pl.program_id(1)

    @pl.when(j == 0)
    def _():
        b = pl.program_id(0) // per_b
        h = (_rms(x_ref[...], g_ref[...]) * (1.0 + _mod_row(mod_ref, b, 1))
             + _mod_row(mod_ref, b, 0))
        hb = h.astype(BF16)
        h_s[...] = hb
        gates_ref[...] = _dot(hb, wg_ref[...])

    heads = tn // HEAD_DIM
    rows = x_ref.shape[0]

    def store(jt):
        acc = _dot(h_s[...], w_ref[...])
        for c in range(heads):
            a = acc[:, c * HEAD_DIM:(c + 1) * HEAD_DIM]
            col = None if jt is None else jt * tn + c * HEAD_DIM
            if col is not None and col < ROPE_W:
                a = a * cos_ref[...] + pltpu.roll(a, HEAD_DIM // 2, axis=1) * sin_ref[...]
            if col is not None and col < ATT_W:
                a = a * Q_PRESCALE
            proj_ref[:, c * HEAD_DIM:(c + 1) * HEAD_DIM] = a.astype(BF16)
            for col0, hb_ref in ((COL_KC, hbk_ref), (COL_VC, hbv_ref)):
                if col is not None and col0 <= col < col0 + KV_W:
                    stage[...] = a
                    hb_ref[0, (col - col0) // HEAD_DIM] = jnp.concatenate(
                        [stage[pl.ds(i, rows // CMP_STRIDE, stride=CMP_STRIDE), :]
                         for i in range(CMP_STRIDE)], axis=1).astype(BF16)

    n_special = -(-COL_VS // tn)
    for jt in range(n_special):
        pl.when(j == jt)(functools.partial(store, jt))
    pl.when(j >= n_special)(functools.partial(store, None))


def _inproj(x2, mod, g_pre, w_in_p, w_gates, cos_t, sin_t, seq, tm, tn=512):
    t, d = x2.shape
    per_b = seq // tm
    assert COL_KC % tn + KV_W <= tn and COL_VC % tn + KV_W <= tn and tm % (16 * CMP_STRIDE) == 0
    hb_spec = pl.BlockSpec((1, N_KV_GROUPS, tm // CMP_STRIDE, CMP_STRIDE * HEAD_DIM),
                           lambda i, j: (i // per_b, 0, i % per_b, 0))
    hb_shape = jax.ShapeDtypeStruct(
        (t // seq, N_KV_GROUPS, seq // CMP_STRIDE, CMP_STRIDE * HEAD_DIM), BF16)
    return pl.pallas_call(
        functools.partial(_inproj_kernel, tn=tn, per_b=per_b),
        grid=(t // tm, PROJ_W // tn),
        in_specs=[pl.BlockSpec((tm, d), lambda i, j: (i, 0)),
                  pl.BlockSpec(mod.shape, lambda i, j: (0, 0)),
                  pl.BlockSpec((1, d), lambda i, j: (0, 0)),
                  pl.BlockSpec((d, tn), lambda i, j: (0, j)),
                  pl.BlockSpec((d, LANES), lambda i, j: (0, 0)),
                  pl.BlockSpec((tm, HEAD_DIM), lambda i, j: (i, 0)),
                  pl.BlockSpec((tm, HEAD_DIM), lambda i, j: (i, 0))],
        out_specs=[pl.BlockSpec((tm, tn), lambda i, j: (i, j)),
                   pl.BlockSpec((tm, LANES), lambda i, j: (i, 0)), hb_spec, hb_spec],
        out_shape=[jax.ShapeDtypeStruct((t, PROJ_W), BF16),
                   jax.ShapeDtypeStruct((t, LANES), F32), hb_shape, hb_shape],
        scratch_shapes=[pltpu.VMEM((tm, d), BF16), pltpu.VMEM((tm, HEAD_DIM), F32)],
        compiler_params=_cparams(("parallel", "arbitrary")),
        name="inproj",
    )(x2, mod, g_pre, w_in_p, w_gates, cos_t, sin_t)


def _compress_kernel(h_ref, w1_ref, pe_ref, w2_ref, o_ref):
    hb = h_ref[0]
    half = hb.shape[1]
    ya = _dot(hb, w1_ref[0:half, :])
    yb = _dot(hb, w1_ref[half:2 * half, :])
    const = _dot(pe_ref[...], w1_ref[...])
    n = ya.shape[0]
    yb_next = pltpu.roll(yb, n - 1, axis=0)
    hid = _silu(ya + yb_next + const[0:1, :])
    out = _dot(hid.astype(BF16), w2_ref[...])
    row = lax.broadcasted_iota(jnp.int32, out.shape, 0)
    o_ref[0] = jnp.where(row < n - 1, out, 0.0).astype(BF16)


def _compress(hblk, w1, pe, w2):
    bg, n, half = hblk.shape
    pe8 = jnp.zeros((8, 2 * half), BF16).at[0].set(pe.reshape(-1).astype(BF16))
    return pl.pallas_call(
        _compress_kernel,
        grid=(bg,),
        in_specs=[pl.BlockSpec((1, n, half), lambda i: (i, 0, 0)),
                  pl.BlockSpec((2 * half, HEAD_DIM), lambda i: (0, 0)),
                  pl.BlockSpec((8, 2 * half), lambda i: (0, 0)),
                  pl.BlockSpec((HEAD_DIM, HEAD_DIM), lambda i: (0, 0))],
        out_specs=pl.BlockSpec((1, n, HEAD_DIM), lambda i: (i, 0, 0)),
        out_shape=jax.ShapeDtypeStruct((bg, n, HEAD_DIM), BF16),
        compiler_params=_cparams(("parallel",)),
        name="compress",
    )(hblk, w1.astype(BF16), pe8, w2.astype(BF16))


def _cmp_kernel(q_ref, kc_ref, vc_ref, gates_ref, ov_ref, ocmp_ref, selb_ref, *, tq, nsel):
    g = pl.program_id(0) % N_KV_GROUPS
    q0 = pl.program_id(1) * tq
    kc = kc_ref[0]
    vc = vc_ref[0]
    nc = kc.shape[0]
    t_idx = q0 + lax.broadcasted_iota(jnp.int32, (tq, nc), 0)
    c_idx = lax.broadcasted_iota(jnp.int32, (tq, nc), 1)
    c_ok = c_idx * CMP_STRIDE + (CMP_BLOCK - 1) <= t_idx
    gl = gates_ref[...]
    lane = lax.broadcasted_iota(jnp.int32, (tq, LANES), 1)
    psum = jnp.zeros((tq, nc), F32)
    for h in range(HEADS_PER_GROUP):
        qh = q_ref[:, h * HEAD_DIM:(h + 1) * HEAD_DIM]
        s = jnp.where(c_ok, _dot_nt(qh, kc), -jnp.inf)
        m = jnp.max(s, axis=-1, keepdims=True)
        m = jnp.where(m == -jnp.inf, 0.0, m)
        p = jnp.exp2(s - m)
        p = p / jnp.maximum(jnp.sum(p, axis=-1, keepdims=True), 1e-30)
        psum = psum + p
        o = _dot(p.astype(BF16), vc)
        glog = _lane_col(gl, lane, (g * HEADS_PER_GROUP + h) * 3)
        ocmp_ref[:, h * HEAD_DIM:(h + 1) * HEAD_DIM] = (o * _sigmoid(glog)).astype(BF16)

    imp = _dot(psum, ov_ref[...], precision=lax.Precision.HIGHEST)
    cur = (q0 + lax.broadcasted_iota(jnp.int32, (tq, LANES), 0)) // SEL_BLOCK
    forced = (lane == 0) | (lane == cur) | (lane == cur - 1)
    imp = jnp.where(lane <= cur, imp + jnp.where(forced, FORCE_BONUS, 0.0), -jnp.inf)
    imp_t = imp.T[0:nsel, :]
    sub = 8
    groups = [imp_t[g * sub:(g + 1) * sub, :] for g in range(nsel // sub)]
    cnts = [jnp.zeros((sub, tq), F32) for _ in groups]
    row = lax.broadcasted_iota(jnp.int32, (sub, tq), 0)
    for k in range(nsel):
        vk = imp_t[k:k + 1, :]
        for g, grp in enumerate(groups):
            if k < g * sub:
                beats = vk >= grp
            elif k >= (g + 1) * sub:
                beats = vk > grp
            else:
                beats = (vk > grp) | ((vk == grp) & (row > k - g * sub))
            cnts[g] = cnts[g] + jnp.where(beats, 1.0, 0.0)
    cnt = jnp.concatenate(cnts, axis=0)
    sel = (cnt < float(N_SEL)) & (imp_t > -jnp.inf)
    bias_t = jnp.where(sel, 0.0, -MASK_BIG)
    if nsel < LANES:
        bias_t = jnp.concatenate([bias_t, jnp.zeros((LANES - nsel, tq), F32)], axis=0)
    selb_ref[...] = bias_t.T.astype(BF16)


def _cmp_attn(proj, kc, vc, gates, batch, seq, tq):
    t = proj.shape[0]
    nq = seq // tq
    nc = kc.shape[1]
    nsel = seq // SEL_BLOCK
    assert nsel <= LANES and nsel % 8 == 0
    c_start = np.arange(nc) * CMP_STRIDE
    j_start = np.arange(LANES) * SEL_BLOCK
    ov = ((c_start[:, None] < j_start[None, :] + SEL_BLOCK)
          & (c_start[:, None] + CMP_BLOCK > j_start[None, :])
          & (np.arange(LANES)[None, :] < nsel)).astype(np.float32)
    gq = HEADS_PER_GROUP * HEAD_DIM
    qmap = lambda bg, i: ((bg // N_KV_GROUPS) * nq + i, bg % N_KV_GROUPS)
    return pl.pallas_call(
        functools.partial(_cmp_kernel, tq=tq, nsel=nsel),
        grid=(batch * N_KV_GROUPS, nq),
        in_specs=[pl.BlockSpec((tq, gq), qmap),
                  pl.BlockSpec((1, nc, HEAD_DIM), lambda bg, i: (bg, 0, 0)),
                  pl.BlockSpec((1, nc, HEAD_DIM), lambda bg, i: (bg, 0, 0)),
                  pl.BlockSpec((tq, LANES), lambda bg, i: ((bg // N_KV_GROUPS) * nq + i, 0)),
                  pl.BlockSpec((nc, LANES), lambda bg, i: (0, 0))],
        out_specs=[pl.BlockSpec((tq, gq), qmap),
                   pl.BlockSpec((tq, LANES), lambda bg, i: (bg * nq + i, 0))],
        out_shape=[jax.ShapeDtypeStruct((t, ATT_W), BF16),
                   jax.ShapeDtypeStruct((batch * N_KV_GROUPS * seq, LANES), BF16)],
        compiler_params=_cparams(("parallel", "parallel")),
        name="cmp_attn",
    )(proj, kc, vc, gates, jnp.asarray(ov))


def _selwin_kernel(q_ref, selb_ref, ocmp_ref, gates_ref, ks_ref, vs_ref, kw_ref, vw_ref,
                   cb_ref, wb_ref, o_ref, kaug_s, vaug_s, vwaug_s, qaug_s, sbuf_s, m_s, acc_s,
                   swin_s, *, tq, tk, seq, nwin):
    g = pl.program_id(0) % N_KV_GROUPS
    qi = pl.program_id(1)
    q0 = qi * tq
    hq = HEADS_PER_GROUP * tq
    half = hq // 2

    @pl.when(qi == 0)
    def _():
        r = lax.broadcasted_iota(jnp.int32, (seq, HEAD_DIM), 0)
        ln = lax.broadcasted_iota(jnp.int32, (seq, HEAD_DIM), 1)
        ones_col = jnp.where(ln == 0, 1.0, 0.0).astype(BF16)
        kaug_s[:, 0:HEAD_DIM] = ks_ref[...]
        kaug_s[:, HEAD_DIM:2 * HEAD_DIM] = jnp.where(r // SEL_BLOCK == ln, 1.0, 0.0).astype(BF16)
        vaug_s[:, 0:HEAD_DIM] = vs_ref[...]
        vaug_s[:, HEAD_DIM:2 * HEAD_DIM] = ones_col
        vwaug_s[:, 0:HEAD_DIM] = vw_ref[...]
        vwaug_s[:, HEAD_DIM:2 * HEAD_DIM] = ones_col

    for h in range(HEADS_PER_GROUP):
        qaug_s[h * tq:(h + 1) * tq, 0:HEAD_DIM] = q_ref[:, h * HEAD_DIM:(h + 1) * HEAD_DIM]
        qaug_s[h * tq:(h + 1) * tq, HEAD_DIM:2 * HEAD_DIM] = selb_ref[...]
    qa = qaug_s[...]

    def scores(kt, slot):
        ks = kaug_s[pl.ds(pl.multiple_of(kt * tk, tk), tk), :]
        sbuf_s[slot] = _dot_nt(qa, ks)

    def absorb(kt, slot, diagonal=False):
        vs = vaug_s[pl.ds(pl.multiple_of(kt * tk, tk), tk), :]
        s = sbuf_s[slot]
        if diagonal:
            bias = cb_ref[(q0 - kt * tk) // tq]
            s = (s.reshape(HEADS_PER_GROUP, tq, tk) + bias[None]).reshape(hq, tk)
        m = m_s[...]
        m_new = jnp.maximum(m, jnp.max(s, axis=-1, keepdims=True))
        m_s[...] = m_new
        pb = jnp.exp2(s - m_new[:, 0:1]).astype(BF16)
        if diagonal:
            pv = jnp.concatenate([_dot(pb[0:half], vs), _dot(pb[half:hq], vs)], axis=0)
        else:
            pv = _dot(pb, vs)
        acc_s[...] = jnp.exp2(m - m_new)[:, 0:1] * acc_s[...] + pv

    n_full = q0 // tk
    m_s[...] = jnp.full(m_s.shape, -jnp.inf, F32)
    acc_s[...] = jnp.zeros(acc_s.shape, F32)
    scores(0, 0)

    def pair(j, c):
        scores(2 * j + 1, 1)
        absorb(2 * j, 0)
        scores(2 * j + 2, 0)
        absorb(2 * j + 1, 1)
        return c

    lax.fori_loop(0, n_full // 2, pair, 0)
    odd = n_full % 2

    @pl.when(odd == 1)
    def _():
        scores(n_full, 1)
        absorb(n_full - 1, 0)

    nq = seq // tq
    w0 = jnp.clip(qi - (nwin - 1), 0, nq - nwin) * tq
    wlen = nwin * tq
    kwin = kw_ref[pl.ds(pl.multiple_of(w0, tq), wlen), :]
    vwin = vwaug_s[pl.ds(pl.multiple_of(w0, tq), wlen), :]
    swin_s[0:half, :] = _dot_nt(qaug_s[0:half, 0:HEAD_DIM], kwin)
    swin_s[half:hq, :] = _dot_nt(qaug_s[half:hq, 0:HEAD_DIM], kwin)
    absorb(n_full, odd, diagonal=True)
    acc = acc_s[...]
    o_sel = acc[:, 0:HEAD_DIM] / acc[:, HEAD_DIM:HEAD_DIM + 1]

    wbias = wb_ref[(q0 - w0) // tq]
    sw = (swin_s[...].reshape(HEADS_PER_GROUP, tq, wlen) + wbias[None]).reshape(hq, wlen)
    pwb = jnp.exp2(sw - jnp.max(sw, axis=-1, keepdims=True)).astype(BF16)
    ow = jnp.concatenate([_dot(pwb[0:half], vwin), _dot(pwb[half:hq], vwin)], axis=0)
    o_win = ow[:, 0:HEAD_DIM] / ow[:, HEAD_DIM:HEAD_DIM + 1]

    gl = gates_ref[...]
    lane = lax.broadcasted_iota(jnp.int32, (tq, LANES), 1)
    for h in range(HEADS_PER_GROUP):
        gcol = (g * HEADS_PER_GROUP + h) * 3
        g_s = _sigmoid(_lane_col(gl, lane, gcol + 1))
        g_w = _sigmoid(_lane_col(gl, lane, gcol + 2))
        o = (ocmp_ref[:, h * HEAD_DIM:(h + 1) * HEAD_DIM].astype(F32)
             + g_s * o_sel[h * tq:(h + 1) * tq, :] + g_w * o_win[h * tq:(h + 1) * tq, :])
        o_ref[:, h * HEAD_DIM:(h + 1) * HEAD_DIM] = o.astype(BF16)


def _selwin_attn(proj, selb, ocmp, gates, batch, seq, tq, tk):
    t = proj.shape[0]
    nq = seq // tq
    assert tk % tq == 0 and tq & (tq - 1) == 0 and seq % tk == 0
    nwin = min(WINDOW // tq + 1, nq)
    assert WINDOW % tq == 0
    gq = HEADS_PER_GROUP * HEAD_DIM
    hq = HEADS_PER_GROUP * tq
    r = np.arange(tq)[:, None]
    cbias = np.stack([np.where(np.arange(tk)[None, :] <= r + off * tq, 0.0, -np.inf)
                      for off in range(tk // tq)]).astype(np.float32)
    dist = [r + off * tq - np.arange(nwin * tq)[None, :] for off in range(nwin)]
    wbias = np.stack([np.where((d >= 0) & (d < WINDOW), 0.0, -np.inf)
                      for d in dist]).astype(np.float32)
    qmap = lambda bg, i: ((bg // N_KV_GROUPS) * nq + i, bg % N_KV_GROUPS)

    def kvspec(col):
        return pl.BlockSpec((seq, HEAD_DIM),
                            lambda bg, i: (bg // N_KV_GROUPS, col // HEAD_DIM + bg % N_KV_GROUPS))

    return pl.pallas_call(
        functools.partial(_selwin_kernel, tq=tq, tk=tk, seq=seq, nwin=nwin),
        grid=(batch * N_KV_GROUPS, nq),
        in_specs=[pl.BlockSpec((tq, gq), qmap),
                  pl.BlockSpec((tq, LANES), lambda bg, i: (bg * nq + i, 0)),
                  pl.BlockSpec((tq, gq), qmap),
                  pl.BlockSpec((tq, LANES), lambda bg, i: ((bg // N_KV_GROUPS) * nq + i, 0)),
                  kvspec(COL_KS), kvspec(COL_VS), kvspec(COL_KW), kvspec(COL_VW),
                  pl.BlockSpec(cbias.shape, lambda bg, i: (0, 0, 0)),
                  pl.BlockSpec(wbias.shape, lambda bg, i: (0, 0, 0))],
        out_specs=pl.BlockSpec((tq, gq), qmap),
        out_shape=jax.ShapeDtypeStruct((t, ATT_W), BF16),
        scratch_shapes=[pltpu.VMEM((seq, 2 * HEAD_DIM), BF16),
                        pltpu.VMEM((seq, 2 * HEAD_DIM), BF16),
                        pltpu.VMEM((seq, 2 * HEAD_DIM), BF16),
                        pltpu.VMEM((hq, 2 * HEAD_DIM), BF16),
                        pltpu.VMEM((2, hq, tk), F32),
                        pltpu.VMEM((hq, LANES), F32),
                        pltpu.VMEM((hq, 2 * HEAD_DIM), F32),
                        pltpu.VMEM((hq, nwin * tq), F32)],
        compiler_params=_cparams(("parallel", "arbitrary")),
        name="selwin_attn",
    )(proj, selb, ocmp, gates, proj, proj, proj, proj, jnp.asarray(cbias), jnp.asarray(wbias))


_HGRN_LEVELS = (32, 16, 8, 4, 2, 1)


def _hgrn_tables():
    c = HGRN_CHUNK
    nl = len(_HGRN_LEVELS)
    idx = np.arange(c)
    i, j = idx[:, None], idx[None, :]
    mats = [j <= i, j > i]
    lvl = np.full((c, c), nl + 1, np.int32)
    for li, s in enumerate(_HGRN_LEVELS):
        mats.append((j >= (i // s) * s) & (j <= i))
        mats.append((j > i) & (j <= (i // s) * s + s - 1))
        lvl[((i // (2 * s)) == (j // (2 * s))) & (((i // s) % 2) == 1) & (((j // s) % 2) == 0)] = li
    lvl[i == j] = nl
    w = np.concatenate(mats, 0).astype(np.float32)
    return np.concatenate([w, w], axis=1), lvl


def _hgrn_head(hq_ref, hf_ref, hi_ref, hg_ref, lbl_ref, gn_ref, w_ref, lvl_ref, o_ref, st_s,
               hh, ts):
    c = HGRN_CHUNK
    nc = ts // c
    nl = len(_HGRN_LEVELS)
    cs = slice(hh * HEAD_DIM, (hh + 1) * HEAD_DIM)
    lbl = lbl_ref[:, cs]
    e = jnp.exp(lbl - jnp.max(lbl, axis=0, keepdims=True))
    lb = e[0:1, :] / jnp.sum(e, axis=0, keepdims=True)

    f = lb + (1.0 - lb) * _sigmoid(hf_ref[:, cs].astype(F32))
    k = 1.0 - f
    qf = _silu(hq_ref[:, cs].astype(F32))
    v = hi_ref[:, cs]

    lf = jnp.log(f) * LOG2E
    lf_hi = lf.astype(BF16)
    lf_lo = (lf - lf_hi.astype(F32)).astype(BF16)
    side = lambda a: jnp.concatenate([a[ci * c:(ci + 1) * c, :] for ci in range(nc)], axis=1)
    ex = _dot(w_ref[...], jnp.concatenate([side(lf_hi), side(lf_lo)], axis=0))

    def rows(bi):
        return jnp.concatenate(
            [ex[bi * c:(bi + 1) * c, ci * HEAD_DIM:(ci + 1) * HEAD_DIM] for ci in range(nc)], axis=0)

    b = rows(0)
    qhat = (qf * jnp.exp2(b)).astype(BF16)
    khat = (k * jnp.exp2(rows(1))).astype(BF16)
    lvl = lvl_ref[...]
    masks = [lvl == li for li in range(nl + 1)]
    qs = [(qf * jnp.exp2(rows(2 + 2 * li))).astype(BF16) for li in range(nl)] + [qf.astype(BF16)]
    ks = [(k * jnp.exp2(rows(3 + 2 * li))).astype(BF16) for li in range(nl)] + [k.astype(BF16)]
    o_intra = []
    for ci in range(nc):
        sl = slice(ci * c, (ci + 1) * c)
        a = jnp.zeros((c, c), F32)
        for li in range(nl + 1):
            a = jnp.where(masks[li], _dot_nt(qs[li][sl, :], ks[li][sl, :]), a)
        o_intra.append(_dot(a.astype(BF16), v[sl, :]))

    st = st_s[hh]
    outs = []
    for ci in range(nc):
        sl = slice(ci * c, (ci + 1) * c)
        outs.append(o_intra[ci] + _dot_nt(qhat[sl, :], st.astype(BF16)))
        st = st * jnp.exp2(b[(ci + 1) * c - 1:(ci + 1) * c, :]) + lax.dot_general(
            v[sl, :], khat[sl, :], _TN, preferred_element_type=F32)
    st_s[hh] = st
    o = jnp.concatenate(outs, axis=0)
    o = o * lax.rsqrt(jnp.mean(o * o, axis=-1, keepdims=True) + EPS) * gn_ref[:, cs]
    o_ref[:, cs] = (o * _silu(hg_ref[:, cs].astype(F32))).astype(BF16)


def _hgrn_kernel(hq_ref, hf_ref, hi_ref, hg_ref, lbl_ref, gn_ref, w_ref, lvl_ref,
                 o_ref, st_s, *, ts, hb):
    @pl.when(pl.program_id(2) == 0)
    def _():
        st_s[...] = jnp.zeros_like(st_s)

    for hh in range(hb):
        _hgrn_head(hq_ref, hf_ref, hi_ref, hg_ref, lbl_ref, gn_ref, w_ref, lvl_ref, o_ref, st_s,
                   hh, ts)


def _hgrn(proj, lb_logits, g_norm, batch, seq, ts, hb):
    t = proj.shape[0]
    ns = seq // ts
    wmat, lvl = _hgrn_tables()
    w = hb * HEAD_DIM
    assert all(col % w == 0 for col in (COL_HQ, COL_HF, COL_HI, COL_HG))

    def colspec(col):
        return pl.BlockSpec((ts, w), lambda b, h, s: (b * ns + s, col // w + h))

    nl = lb_logits.shape[0]
    return pl.pallas_call(
        functools.partial(_hgrn_kernel, ts=ts, hb=hb),
        grid=(batch, N_HGRN_HEADS // hb, ns),
        in_specs=[colspec(COL_HQ), colspec(COL_HF), colspec(COL_HI), colspec(COL_HG),
                  pl.BlockSpec((nl, w), lambda b, h, s: (0, h)),
                  pl.BlockSpec((1, w), lambda b, h, s: (0, h)),
                  pl.BlockSpec(wmat.shape, lambda b, h, s: (0, 0)),
                  pl.BlockSpec(lvl.shape, lambda b, h, s: (0, 0))],
        out_specs=pl.BlockSpec((ts, w), lambda b, h, s: (b * ns + s, h)),
        out_shape=jax.ShapeDtypeStruct((t, ATT_W), BF16),
        scratch_shapes=[pltpu.VMEM((hb, HEAD_DIM, HEAD_DIM), F32)],
        compiler_params=_cparams(("parallel", "parallel", "arbitrary")),
        name="hgrn2",
    )(proj, proj, proj, proj, lb_logits, g_norm, jnp.asarray(wmat, BF16), jnp.asarray(lvl))


def _outproj_kernel(oa_ref, or_ref, wa_ref, wb_ref, x_ref, mod_ref, gpost_ref, gpre_ref,
                    wr_ref, br_ref, x1_ref, h2_ref, lg_ref, stage, *, per_b):
    b = pl.program_id(0) // per_b
    y = _dot(oa_ref[...], wa_ref[...]) + _dot(or_ref[...], wb_ref[...])
    x1 = x_ref[...] + _mod_row(mod_ref, b, 2) * _rms(y, gpost_ref[...])
    x1_ref[...] = x1
    h = _rms(x1, gpre_ref[...]) * (1.0 + _mod_row(mod_ref, b, 4)) + _mod_row(mod_ref, b, 3)
    h2_ref[...] = _to_slabs(stage, h)
    h_hi = h.astype(BF16)
    h_lo = (h - h_hi.astype(F32)).astype(BF16)
    hh = _dot(h_hi, wr_ref[...])
    lg_ref[...] = (hh[:, 0:LANES] + hh[:, LANES:2 * LANES]
                   + _dot(h_lo, wr_ref[:, 0:LANES]) + br_ref[...])


def _outproj(o_att, o_rec, w_out, x2, mod, g_post, g_pre, wr, br, seq, tm):
    t, d = x2.shape
    per_b = seq // tm
    wr_hi = wr.astype(BF16)
    wr_cat = jnp.concatenate([wr_hi, (wr - wr_hi.astype(F32)).astype(BF16)], axis=1)
    wa = w_out[:ATT_W].astype(BF16)
    wb = w_out[ATT_W:].astype(BF16)
    row = lambda i: (i, 0)
    fixed = lambda i: (0, 0)
    return pl.pallas_call(
        functools.partial(_outproj_kernel, per_b=per_b),
        grid=(t // tm,),
        in_specs=[pl.BlockSpec((tm, ATT_W), row), pl.BlockSpec((tm, ATT_W), row),
                  pl.BlockSpec((ATT_W, d), fixed), pl.BlockSpec((ATT_W, d), fixed),
                  pl.BlockSpec((tm, d), row),
                  pl.BlockSpec(mod.shape, fixed),
                  pl.BlockSpec((1, d), fixed), pl.BlockSpec((1, d), fixed),
                  pl.BlockSpec((d, 2 * LANES), fixed), pl.BlockSpec((1, LANES), fixed)],
        out_specs=[pl.BlockSpec((tm, d), row), pl.BlockSpec((tm * (d // LANES), LANES), row),
                   pl.BlockSpec((tm, LANES), row)],
        out_shape=[jax.ShapeDtypeStruct((t, d), F32),
                   jax.ShapeDtypeStruct((t * (d // LANES), LANES), BF16),
                   jax.ShapeDtypeStruct((t, LANES), F32)],
        scratch_shapes=[pltpu.VMEM((tm * (d // LANES), LANES), F32)],
        compiler_params=_cparams(("parallel",)),
        name="outproj",
    )(o_att, o_rec, wa, wb, x2, mod, g_post, g_pre, wr_cat, br)


G_LANE0 = N_EXPERTS


def _route_kernel(lg_ref, tri_ref, info_ref, cnt_ref, carry_s):
    @pl.when(pl.program_id(0) == 0)
    def _():
        carry_s[...] = jnp.zeros_like(carry_s)

    x = lg_ref[...]
    lane = lax.broadcasted_iota(jnp.int32, x.shape, 1)
    big = jnp.int32(10 ** 6)
    rmax = lambda a: jnp.max(a, axis=-1, keepdims=True)
    rmin = lambda a: jnp.min(a, axis=-1, keepdims=True)
    rsum = lambda a: jnp.sum(a, axis=-1, keepdims=True)

    is_g = (lane >= G_LANE0) & (lane < G_LANE0 + N_EXPERT_GROUPS)
    lgm = jnp.where(is_g, x, -jnp.inf)
    mg = rmax(lgm)
    pg_sel = 1.0 / rsum(jnp.where(is_g, jnp.exp(lgm - mg), 0.0))
    gsel = rmin(jnp.where(lgm == mg, lane, big)) - G_LANE0

    is_e = (lane >= gsel * EXPERTS_PER_GROUP) & (lane < (gsel + 1) * EXPERTS_PER_GROUP)
    lem = jnp.where(is_e, x, -jnp.inf)
    pe = jnp.where(is_e, jnp.exp(lem - rmax(lem)), 0.0)
    pe = pe / rsum(pe)
    pe = jnp.where(is_e, pe, -1.0)
    v1 = rmax(pe)
    i1 = rmin(jnp.where(pe == v1, lane, big))
    pe2 = jnp.where(lane == i1, -1.0, pe)
    v2 = rmax(pe2)
    i2 = rmin(jnp.where(pe2 == v2, lane, big))
    w1 = v1 / (v1 + v2) * pg_sel
    w2 = v2 / (v1 + v2) * pg_sel

    oh1 = jnp.where(lane == i1, 1.0, 0.0)
    oh2 = jnp.where(lane == i2, 1.0, 0.0)
    both = oh1 + oh2
    before = _dot(tri_ref[...], both.astype(BF16)) + carry_s[0:1, :]
    r1 = rsum(oh1 * before)
    r2 = rsum(oh2 * before)
    carry_s[0:1, :] = carry_s[0:1, :] + jnp.sum(both, axis=0, keepdims=True)
    cnt_ref[...] = carry_s[...]

    cols = (i1.astype(F32), i2.astype(F32), w1, w2, r1, r2)
    info = jnp.zeros(x.shape, F32)
    for ci, col in enumerate(cols):
        info = jnp.where(lane == ci, col, info)
    info_ref[...] = info


def _route(logits, tm):
    t = logits.shape[0]
    tri = jnp.asarray(np.tril(np.ones((tm, tm), np.float32), -1), BF16)
    return pl.pallas_call(
        _route_kernel,
        grid=(t // tm,),
        in_specs=[pl.BlockSpec((tm, LANES), lambda i: (i, 0)),
                  pl.BlockSpec((tm, tm), lambda i: (0, 0))],
        out_specs=[pl.BlockSpec((tm, LANES), lambda i: (i, 0)),
                   pl.BlockSpec((8, LANES), lambda i: (0, 0))],
        out_shape=[jax.ShapeDtypeStruct((t, LANES), F32),
                   jax.ShapeDtypeStruct((8, LANES), F32)],
        scratch_shapes=[pltpu.VMEM((8, LANES), F32)],
        compiler_params=_cparams(("arbitrary",)),
        name="route",
    )(logits, tri)


def _to_slabs(stage, x):
    n, d = x.shape
    slab = d // LANES
    for c in range(slab):
        stage[pl.ds(c, n, stride=slab), :] = x[:, c * LANES:(c + 1) * LANES]
    return stage[...].astype(BF16)


def _from_slabs(stage, slabs, n):
    slab = slabs.shape[0] // n
    stage[...] = slabs.astype(F32)
    return jnp.concatenate([stage[pl.ds(c, n, stride=slab), :] for c in range(slab)], axis=1)


def _expert_kernel(te_ref, nu_ref, par_ref, nxt_ref, dst_ref, xs_ref, wg_hbm, wu_hbm, wd_hbm,
                   y_ref, obuf, stage, ssem, wg_f, wu_f, wd_f, wsem, wg_s, wu_s, wd_s, *,
                   tm, standin_row):
    t = pl.program_id(0)
    nu = nu_ref[0]
    slot = t % 2

    def weights(e, buf):
        return [pltpu.make_async_copy(src.at[e], dst.at[buf], wsem.at[buf])
                for src, dst in ((wg_hbm, wg_f), (wu_hbm, wu_f), (wd_hbm, wd_f))]
    slab = wg_s.shape[0] // LANES
    tile = tm * slab

    def token(ref, i):
        return ref.at[pl.ds(pl.multiple_of(i * slab, slab), slab)]

    def scatter(s):
        for r in range(tm):
            pltpu.make_async_copy(token(obuf, s * tm + r), token(y_ref, dst_ref[0, 0, r]),
                                  ssem.at[s]).start()

    def tile_of(buf, s):
        return buf.at[pl.ds(pl.multiple_of(s * tile, tile), tile)]

    def wait_tile(s):
        pltpu.make_async_copy(tile_of(obuf, s), tile_of(obuf, s), ssem.at[s]).wait()

    @pl.when(t < nu)
    def _():
        @pl.when(t == 0)
        def _():
            obuf[tile:2 * tile, :] = jnp.zeros((tile, LANES), obuf.dtype)

            def fill(row0, s):
                return pltpu.make_async_copy(tile_of(obuf, 1),
                                             y_ref.at[pl.ds(row0 * slab, tile)], ssem.at[s])

            fill(standin_row - 2 * tm, 0).start()
            fill(standin_row - tm, 0).start()
            fill(standin_row - 2 * tm, 0).wait()
            fill(standin_row - tm, 0).wait()
            fill(standin_row, 1).start()

        par = par_ref[t]

        @pl.when(t == 0)
        def _():
            for cp in weights(te_ref[0], par):
                cp.start()

        @pl.when((t == 0) | (te_ref[t] != te_ref[jnp.maximum(t - 1, 0)]))
        def _():
            for cp in weights(te_ref[t], par):
                cp.wait()
            wg_s[...] = wg_f[par].astype(BF16)
            wu_s[...] = wu_f[par].astype(BF16)
            wd_s[...] = wd_f[par].astype(BF16)

            @pl.when(nxt_ref[t] != te_ref[t])
            def _():
                for cp in weights(nxt_ref[t], 1 - par):
                    cp.start()

        x = _from_slabs(stage, xs_ref[...], tm).astype(BF16)
        hid = (_silu(_dot(x, wg_s[...])) * _dot(x, wu_s[...])).astype(BF16)
        row0 = pl.multiple_of(slot * tile, tile)
        obuf[pl.ds(row0, tile), :] = _to_slabs(stage, _dot(hid, wd_s[...]))
        scatter(slot)
        wait_tile(1 - slot)

        @pl.when(t == nu - 1)
        def _():
            wait_tile(slot)


def _experts(tile_expert, n_used, parity, next_expert, dst3, xs, w_gate, w_up, w_down, y_rows,
             tm, standin_row):
    n_tiles = dst3.shape[0]
    d, f = w_gate.shape[1:]
    slab = d // LANES
    hbm = pl.BlockSpec(memory_space=pl.ANY)
    return pl.pallas_call(
        functools.partial(_expert_kernel, tm=tm, standin_row=standin_row),
        grid_spec=pltpu.PrefetchScalarGridSpec(
            num_scalar_prefetch=4,
            grid=(n_tiles,),
            in_specs=[pl.BlockSpec((1, 1, tm), lambda t, te, nu, pa, nx: (t, 0, 0),
                                   memory_space=pltpu.SMEM),
                      pl.BlockSpec((tm * slab, LANES),
                                   lambda t, te, nu, pa, nx: (jnp.minimum(t, nu[0] - 1), 0)),
                      hbm, hbm, hbm],
            out_specs=pl.BlockSpec(memory_space=pl.ANY),
            scratch_shapes=[pltpu.VMEM((2 * tm * slab, LANES), BF16),
                            pltpu.VMEM((tm * slab, LANES), F32),
                            pltpu.SemaphoreType.DMA((2,)),
                            pltpu.VMEM((2, d, f), F32), pltpu.VMEM((2, d, f), F32),
                            pltpu.VMEM((2, f, d), F32),
                            pltpu.SemaphoreType.DMA((2,)),
                            pltpu.VMEM((d, f), BF16), pltpu.VMEM((d, f), BF16),
                            pltpu.VMEM((f, d), BF16)]),
        out_shape=jax.ShapeDtypeStruct((y_rows * slab, LANES), BF16),
        compiler_params=_cparams(("arbitrary",)),
        name="moe_experts",
    )(tile_expert, n_used, parity, next_expert, dst3, xs, w_gate, w_up, w_down)


def _dispatch_kernel(pos0_ref, pos1_ref, end_ref, nu_ref, h_ref, xs_ref, zbuf, zsem, sem, *,
                     tm, slab, n_tiles):
    tile = tm * slab

    @pl.when(pl.program_id(0) == 0)
    def _():
        zbuf[...] = jnp.zeros(zbuf.shape, zbuf.dtype)

        def zero(row_end, go):
            cp = pltpu.make_async_copy(
                zbuf, xs_ref.at[pl.ds(pl.multiple_of((row_end - tm) * slab, slab), tile)], zsem)
            pl.when(go)(cp.start)
            return cp, go

        pending = [zero(end_ref[e], end_ref[e] > (end_ref[e - 1] if e else 0))
                   for e in range(N_EXPERTS)]
        pending += [zero((i + 1) * tm, i >= nu_ref[0]) for i in range(n_tiles)]
        for cp, go in pending:
            pl.when(go)(cp.wait)

    def token(ref, i):
        return ref.at[pl.ds(pl.multiple_of(i * slab, slab), slab)]

    def start(r, c):
        pltpu.make_async_copy(token(h_ref, r), token(xs_ref, pos0_ref[0, 0, r]), sem).start()
        pltpu.make_async_copy(token(h_ref, r), token(xs_ref, pos1_ref[0, 0, r]), sem).start()
        return c

    lax.fori_loop(0, tm, start, 0, unroll=8)
    for _ in range(2):
        pltpu.make_async_copy(h_ref, h_ref, sem).wait()


def _dispatch(pos0, pos1, tile_end, n_used, h2, n_tiles, tm_exp, tm):
    rows = h2.shape[0]
    slab = rows // pos0.shape[0]
    t = pos0.shape[0]
    assert tm == tm_exp
    per_tile = pl.BlockSpec((1, 1, tm), lambda i: (i, 0, 0), memory_space=pltpu.SMEM)
    whole = pl.BlockSpec(memory_space=pltpu.SMEM)
    return pl.pallas_call(
        functools.partial(_dispatch_kernel, tm=tm, slab=slab, n_tiles=n_tiles),
        grid=(t // tm,),
        in_specs=[per_tile, per_tile, whole, whole,
                  pl.BlockSpec((tm * slab, LANES), lambda i: (i, 0))],
        out_specs=pl.BlockSpec(memory_space=pl.ANY),
        out_shape=jax.ShapeDtypeStruct((n_tiles * tm_exp * slab, LANES), h2.dtype),
        scratch_shapes=[pltpu.VMEM((tm * slab, LANES), h2.dtype),
                        pltpu.SemaphoreType.DMA(()), pltpu.SemaphoreType.DMA(())],
        compiler_params=_cparams(("arbitrary",)),
        name="moe_dispatch",
    )(pos0.reshape(t // tm, 1, tm), pos1.reshape(t // tm, 1, tm), tile_end * tm_exp, n_used, h2)


def _final_kernel(info_ref, x1_ref, mod_ref, g_ref, ya_ref, yb_ref, o_ref, stage_a, stage_b, *,
                  per_b):
    info = info_ref[...]
    tm = x1_ref.shape[0]
    ya = _from_slabs(stage_a, ya_ref[...], tm)
    yb = _from_slabs(stage_b, yb_ref[...], tm)
    y = info[:, 2:3] * ya + info[:, 3:4] * yb
    gate = _mod_row(mod_ref, pl.program_id(0) // per_b, 5)
    o_ref[...] = x1_ref[...] + gate * _rms(y, g_ref[...])


def _moe_final(info, x1, mod, g_post, y2, seq, tm):
    t, d = x1.shape
    per_b = seq // tm
    nb = t // tm
    return pl.pallas_call(
        functools.partial(_final_kernel, per_b=per_b),
        grid=(nb,),
        in_specs=[pl.BlockSpec((tm, LANES), lambda i: (i, 0)),
                  pl.BlockSpec((tm, d), lambda i: (i, 0)),
                  pl.BlockSpec(mod.shape, lambda i: (0, 0)),
                  pl.BlockSpec((1, d), lambda i: (0, 0)),
                  pl.BlockSpec((tm * (d // LANES), LANES), lambda i: (i, 0)),
                  pl.BlockSpec((tm * (d // LANES), LANES), lambda i: (nb + i, 0))],
        out_specs=pl.BlockSpec((tm, d), lambda i: (i, 0)),
        out_shape=jax.ShapeDtypeStruct((t, d), F32),
        scratch_shapes=[pltpu.VMEM((tm * (d // LANES), LANES), F32)] * 2,
        compiler_params=_cparams(("parallel",)),
        name="moe_final",
    )(info, x1, mod, g_post, y2, y2)


def _slot_map_kernel(pos0_ref, pos1_ref, dst_in, dst_ref, sem, *, tm, n_tokens):
    @pl.when(pl.program_id(0) == 0)
    def _():
        cp = pltpu.make_async_copy(dst_in, dst_ref, sem)
        cp.start()
        cp.wait()

    base = pl.program_id(0) * tm

    def body(i, c):
        dst_ref[pos0_ref[0, 0, i]] = base + i
        dst_ref[pos1_ref[0, 0, i]] = n_tokens + base + i
        return c

    lax.fori_loop(0, tm, body, 0, unroll=8)


def _slot_map(pos0, pos1, dst0, tm):
    t = pos0.shape[0]
    per_tile = pl.BlockSpec((1, 1, tm), lambda i: (i, 0, 0), memory_space=pltpu.SMEM)
    return pl.pallas_call(
        functools.partial(_slot_map_kernel, tm=tm, n_tokens=t),
        grid=(t // tm,),
        in_specs=[per_tile, per_tile, pl.BlockSpec(memory_space=pl.ANY)],
        out_specs=pl.BlockSpec(memory_space=pltpu.SMEM),
        out_shape=jax.ShapeDtypeStruct(dst0.shape, jnp.int32),
        scratch_shapes=[pltpu.SemaphoreType.DMA(())],
        compiler_params=_cparams(("arbitrary",)),
        name="slot_map",
    )(pos0.reshape(t // tm, 1, tm), pos1.reshape(t // tm, 1, tm), dst0)


def _moe(h2, logits, x1, mod, g_post, w_gate, w_up, w_down, seq, tm_route, tm_exp, tm_fin):
    t = x1.shape[0]
    info, cnt = _route(logits, tm_route)
    counts = cnt[0, :N_EXPERTS].astype(jnp.int32)
    tiles_e = (counts + tm_exp - 1) // tm_exp
    tile_end = jnp.cumsum(tiles_e)
    offs = (tile_end - tiles_e) * tm_exp
    n_used = tile_end[-1]

    def slot_of(k):
        e = info[:, k].astype(jnp.int32)
        start = jnp.sum(jnp.where(e[:, None] == jnp.arange(N_EXPERTS)[None, :], offs[None, :], 0),
                        axis=1)
        return start + info[:, 4 + k].astype(jnp.int32)

    n_tiles = (2 * t) // tm_exp + N_EXPERTS
    tidx = jnp.minimum(jnp.arange(n_tiles, dtype=jnp.int32), n_used - 1)
    tile_expert = jnp.sum(tidx[:, None] >= tile_end[None, :], axis=1).astype(jnp.int32)
    switches = jnp.concatenate([jnp.zeros((1,), jnp.int32),
                                (tile_expert[1:] != tile_expert[:-1]).astype(jnp.int32)])
    parity = jnp.cumsum(switches) % 2
    next_expert = tile_expert[jnp.minimum(tile_end[tile_expert], n_used - 1)]
    pos0, pos1 = slot_of(0), slot_of(1)
    slot_ids = jnp.arange(n_tiles * tm_exp, dtype=jnp.int32)
    spare = 2 * t + ((slot_ids // tm_exp) % 2) * tm_exp + slot_ids % tm_exp
    dst = _slot_map(pos0, pos1, spare, tm_route)
    n_used1 = n_used.reshape(1).astype(jnp.int32)
    xs = _dispatch(pos0, pos1, tile_end.astype(jnp.int32), n_used1, h2, n_tiles, tm_exp, tm_route)

    ew = w_gate.shape
    wg = w_gate.reshape(N_EXPERTS, ew[-2], ew[-1])
    wu = w_up.reshape(N_EXPERTS, ew[-2], ew[-1])
    wd = w_down.reshape(N_EXPERTS, ew[-1], ew[-2])
    y2 = _experts(tile_expert, n_used1, parity.astype(jnp.int32), next_expert.astype(jnp.int32),
                  dst.reshape(n_tiles, 1, tm_exp), xs, wg, wu, wd,
                  2 * t + 3 * tm_exp, tm_exp, 2 * t + 2 * tm_exp)
    return _moe_final(info, x1, mod, g_post, y2, seq, tm_fin)


def _permute_w_in(w_in):
    sizes = [ATT_W] + [KV_W] * 6 + [3 * N_ATT_HEADS] + [ATT_W] * 4
    cuts = np.cumsum(sizes)[:-1]
    q, kc, vc, ks, vs, kw, vw, gates, hq, hf, hi, hg = [
        p.astype(BF16) for p in jnp.split(w_in, cuts, axis=1)]
    w_p = jnp.concatenate([q, kc, ks, kw, vc, vs, vw, hq, hf, hi, hg], axis=1)
    w_g = jnp.pad(gates, ((0, 0), (0, LANES - gates.shape[1])))
    return w_p, w_g


def _block(x, c, positions, w_ada, b_ada, g_pre_mix, g_post_mix, g_pre_ffn, g_post_ffn,
           w_in, w_out, pe_k, w1_k, w2_k, pe_v, w1_v, w2_v, lb_logits, g_norm,
           w_group, b_group, w_router, b_router, w_gate, w_up, w_down, cfg):
    batch, seq, d = x.shape
    t = batch * seq
    x2 = x.reshape(t, d)

    c8 = jnp.zeros((8, d), F32).at[:batch].set(c)
    mod = _ada_mod(c8, w_ada, b_ada)
    cos_t, sin_t = _rope_tables(positions, cfg["tm_rope"])
    w_p, w_g = _permute_w_in(w_in)
    proj, gates, hbk, hbv = _inproj(x2, mod, g_pre_mix.reshape(1, d), w_p, w_g, cos_t, sin_t,
                                    seq, cfg["tm_in"])

    half_blocks = (batch * N_KV_GROUPS, seq // CMP_STRIDE, CMP_STRIDE * HEAD_DIM)
    kc = _compress(hbk.reshape(half_blocks), w1_k, pe_k, w2_k)
    vc = _compress(hbv.reshape(half_blocks), w1_v, pe_v, w2_v)
    ocmp, selb = _cmp_attn(proj, kc, vc, gates, batch, seq, cfg["tq_cmp"])
    o_att = _selwin_attn(proj, selb, ocmp, gates, batch, seq, cfg["tq"], cfg["tk"])
    o_rec = _hgrn(proj, lb_logits, g_norm.reshape(1, -1), batch, seq, cfg["ts_hgrn"],
                  cfg["hb_hgrn"])

    wr = jnp.concatenate([w_router, w_group], axis=1)
    wr = jnp.pad(wr, ((0, 0), (0, LANES - wr.shape[1])))
    br = jnp.pad(jnp.concatenate([b_router, b_group]), (0, LANES - N_EXPERTS - N_EXPERT_GROUPS))
    x1, h2, logits = _outproj(o_att, o_rec, w_out, x2, mod, g_post_mix.reshape(1, d),
                              g_pre_ffn.reshape(1, d), wr, br.reshape(1, LANES), seq,
                              cfg["tm_out"])
    out = _moe(h2, logits, x1, mod, g_post_ffn.reshape(1, d), w_gate, w_up, w_down, seq,
               cfg["tm_route"], cfg["tm_exp"], cfg["tm_fin"])
    return out.reshape(batch, seq, d)


def _config(seq):
    return dict(tm_rope=min(1024, seq), tm_in=min(1024, seq), tq_cmp=min(256, seq),
                tq=min(256, seq), tk=min(512, seq), ts_hgrn=min(256, seq), hb_hgrn=4,
                tm_out=min(256, seq), tm_route=min(256, seq), tm_exp=256,
                tm_fin=min(256, seq))


def kernel(x, c, positions, w_ada, b_ada, g_pre_mix, g_post_mix, g_pre_ffn, g_post_ffn, w_in, w_out, cmp_pe_k, cmp_w1_k, cmp_w2_k, cmp_pe_v, cmp_w1_v, cmp_w2_v, hgrn_lb_logits, hgrn_g_norm, w_group, b_group, w_router, b_router, w_gate, w_up, w_down):
    assert w_ada.shape[0] == 1, "single-layer block"
    return _block(x, c, positions, w_ada[0], b_ada[0], g_pre_mix[0], g_post_mix[0],
                  g_pre_ffn[0], g_post_ffn[0], w_in[0], w_out[0], cmp_pe_k[0], cmp_w1_k[0],
                  cmp_w2_k[0], cmp_pe_v[0], cmp_w1_v[0], cmp_w2_v[0], hgrn_lb_logits,
                  hgrn_g_norm[0], w_group[0], b_group[0], w_router[0], b_router[0],
                  w_gate[0], w_up[0], w_down[0], _config(x.shape[1]))
```

```python
import functools

import numpy as np
import jax
import jax.numpy as jnp
from jax import lax
from jax.experimental import pallas as pl
from jax.experimental.pallas import tpu as pltpu

F32 = jnp.float32
BF16 = jnp.bfloat16

HEAD_DIM = 128
N_KV_GROUPS = 2
HEADS_PER_GROUP = 4
N_ATT_HEADS = N_KV_GROUPS * HEADS_PER_GROUP
CMP_BLOCK = 32
CMP_STRIDE = 16
SEL_BLOCK = 64
N_SEL = 16
WINDOW = 512
FORCE_BONUS = 1.0e4
ROPE_THETA = 10000.0
N_HGRN_HEADS = 8
HGRN_CHUNK = 64
N_EXPERT_GROUPS = 4
EXPERTS_PER_GROUP = 8
N_EXPERTS = N_EXPERT_GROUPS * EXPERTS_PER_GROUP
EPS = 1e-6

LANES = 128
VMEM_LIMIT = 56 * 1024 * 1024

ATT_W = N_ATT_HEADS * HEAD_DIM
KV_W = N_KV_GROUPS * HEAD_DIM
COL_Q = 0
COL_KC = COL_Q + ATT_W
COL_KS = COL_KC + KV_W
COL_KW = COL_KS + KV_W
COL_VC = COL_KW + KV_W
COL_VS = COL_VC + KV_W
COL_VW = COL_VS + KV_W
COL_HQ = COL_VW + KV_W
COL_HF = COL_HQ + ATT_W
COL_HI = COL_HF + ATT_W
COL_HG = COL_HI + ATT_W
PROJ_W = COL_HG + ATT_W
ROPE_W = COL_VC

MASK_BIG = float(2 ** 30)
LOG2E = 1.4426950408889634
Q_PRESCALE = HEAD_DIM ** -0.5 * LOG2E

_NT = (((1,), (1,)), ((), ()))
_TN = (((0,), (0,)), ((), ()))


def _cparams(sem):
    return pltpu.CompilerParams(dimension_semantics=sem, vmem_limit_bytes=VMEM_LIMIT)


def _dot(a, b, **kw):
    return jnp.dot(a, b, preferred_element_type=F32, **kw)


def _dot_nt(a, b):
    return lax.dot_general(a, b, _NT, preferred_element_type=F32)


def _sigmoid(x):
    return 1.0 / (1.0 + jnp.exp(-x))


def _silu(x):
    return x * _sigmoid(x)


def _rms(x, g):
    return x * lax.rsqrt(jnp.mean(x * x, axis=-1, keepdims=True) + EPS) * g


def _mod_row(mod_ref, b, k):
    d = mod_ref.shape[1] // 6
    return mod_ref[pl.ds(b, 1), k * d:(k + 1) * d]


def _lane_col(x, lane, col):
    return jnp.sum(jnp.where(lane == col, x, 0.0), axis=-1, keepdims=True)


def _ada_kernel(c_ref, w_ref, b_ref, o_ref):
    s = _silu(c_ref[...])
    o_ref[...] = _dot(s, w_ref[...], precision=lax.Precision.HIGHEST) + b_ref[...]


def _ada_mod(c8, w_ada, b_ada, tn=512):
    rows, d = c8.shape
    n = w_ada.shape[1]
    return pl.pallas_call(
        _ada_kernel,
        grid=(n // tn,),
        in_specs=[pl.BlockSpec((rows, d), lambda j: (0, 0)),
                  pl.BlockSpec((d, tn), lambda j: (0, j)),
                  pl.BlockSpec((1, tn), lambda j: (0, j))],
        out_specs=pl.BlockSpec((rows, tn), lambda j: (0, j)),
        out_shape=jax.ShapeDtypeStruct((rows, n), F32),
        compiler_params=_cparams(("parallel",)),
        name="ada_mod",
    )(c8, w_ada, b_ada.reshape(1, n))


def _rope_kernel(ang_ref, cos_ref, sin_ref):
    a = ang_ref[...]
    lane = lax.broadcasted_iota(jnp.int32, a.shape, 1)
    cos_ref[...] = jnp.cos(a)
    sin_ref[...] = jnp.where(lane < HEAD_DIM // 2, -jnp.sin(a), jnp.sin(a))


def _rope_tables(positions, tm):
    t = positions.size
    inv_freq = ROPE_THETA ** (-jnp.arange(0, HEAD_DIM, 2, dtype=F32) / HEAD_DIM)
    ang = positions.reshape(t, 1).astype(F32) * jnp.concatenate([inv_freq, inv_freq])[None, :]
    spec = pl.BlockSpec((tm, HEAD_DIM), lambda i: (i, 0))
    return pl.pallas_call(
        _rope_kernel,
        grid=(t // tm,),
        in_specs=[spec],
        out_specs=[spec, spec],
        out_shape=[jax.ShapeDtypeStruct((t, HEAD_DIM), F32)] * 2,
        compiler_params=_cparams(("parallel",)),
        name="rope_tables",
    )(ang)


def _inproj_kernel(x_ref, mod_ref, g_ref, w_ref, wg_ref, cos_ref, sin_ref,
                   proj_ref, gates_ref, hbk_ref, hbv_ref, h_s, stage, *, tn, per_b):
    j = pl.program_id(1)

    @pl.when(j == 0)
    def _():
        b = pl.program_id(0) // per_b
        h = (_rms(x_ref[...], g_ref[...]) * (1.0 + _mod_row(mod_ref, b, 1))
             + _mod_row(mod_ref, b, 0))
        hb = h.astype(BF16)
        h_s[...] = hb
        gates_ref[...] = _dot(hb, wg_ref[...])

    heads = tn // HEAD_DIM
    rows = x_ref.shape[0]

    def store(jt):
        acc = _dot(h_s[...], w_ref[...])
        for c in range(heads):
            a = acc[:, c * HEAD_DIM:(c + 1) * HEAD_DIM]
            col = None if jt is None else jt * tn + c * HEAD_DIM
            if col is not None and col < ROPE_W:
                a = a * cos_ref[...] + pltpu.roll(a, HEAD_DIM // 2, axis=1) * sin_ref[...]
            if col is not None and col < ATT_W:
                a = a * Q_PRESCALE
            proj_ref[:, c * HEAD_DIM:(c + 1) * HEAD_DIM] = a.astype(BF16)
            for col0, hb_ref in ((COL_KC, hbk_ref), (COL_VC, hbv_ref)):
                if col is not None and col0 <= col < col0 + KV_W:
                    stage[...] = a
                    hb_ref[0, (col - col0) // HEAD_DIM] = jnp.concatenate(
                        [stage[pl.ds(i, rows // CMP_STRIDE, stride=CMP_STRIDE), :]
                         for i in range(CMP_STRIDE)], axis=1).astype(BF16)

    n_special = -(-COL_VS // tn)
    for jt in range(n_special):
        pl.when(j == jt)(functools.partial(store, jt))
    pl.when(j >= n_special)(functools.partial(store, None))


def _inproj(x2, mod, g_pre, w_in_p, w_gates, cos_t, sin_t, seq, tm, tn=512):
    t, d = x2.shape
    per_b = seq // tm
    assert COL_KC % tn + KV_W <= tn and COL_VC % tn + KV_W <= tn and tm % (16 * CMP_STRIDE) == 0
    hb_spec = pl.BlockSpec((1, N_KV_GROUPS, tm // CMP_STRIDE, CMP_STRIDE * HEAD_DIM),
                           lambda i, j: (i // per_b, 0, i % per_b, 0))
    hb_shape = jax.ShapeDtypeStruct(
        (t // seq, N_KV_GROUPS, seq // CMP_STRIDE, CMP_STRIDE * HEAD_DIM), BF16)
    return pl.pallas_call(
        functools.partial(_inproj_kernel, tn=tn, per_b=per_b),
        grid=(t // tm, PROJ_W // tn),
        in_specs=[pl.BlockSpec((tm, d), lambda i, j: (i, 0)),
                  pl.BlockSpec(mod.shape, lambda i, j: (0, 0)),
                  pl.BlockSpec((1, d), lambda i, j: (0, 0)),
                  pl.BlockSpec((d, tn), lambda i, j: (0, j)),
                  pl.BlockSpec((d, LANES), lambda i, j: (0, 0)),
                  pl.BlockSpec((tm, HEAD_DIM), lambda i, j: (i, 0)),
                  pl.BlockSpec((tm, HEAD_DIM), lambda i, j: (i, 0))],
        out_specs=[pl.BlockSpec((tm, tn), lambda i, j: (i, j)),
                   pl.BlockSpec((tm, LANES), lambda i, j: (i, 0)), hb_spec, hb_spec],
        out_shape=[jax.ShapeDtypeStruct((t, PROJ_W), BF16),
                   jax.ShapeDtypeStruct((t, LANES), F32), hb_shape, hb_shape],
        scratch_shapes=[pltpu.VMEM((tm, d), BF16), pltpu.VMEM((tm, HEAD_DIM), F32)],
        compiler_params=_cparams(("parallel", "arbitrary")),
        name="inproj",
    )(x2, mod, g_pre, w_in_p, w_gates, cos_t, sin_t)


def _compress_kernel(h_ref, w1_ref, pe_ref, w2_ref, o_ref):
    hb = h_ref[0]
    half = hb.shape[1]
    ya = _dot(hb, w1_ref[0:half, :])
    yb = _dot(hb, w1_ref[half:2 * half, :])
    const = _dot(pe_ref[...], w1_ref[...])
    n = ya.shape[0]
    yb_next = pltpu.roll(yb, n - 1, axis=0)
    hid = _silu(ya + yb_next + const[0:1, :])
    out = _dot(hid.astype(BF16), w2_ref[...])
    row = lax.broadcasted_iota(jnp.int32, out.shape, 0)
    o_ref[0] = jnp.where(row < n - 1, out, 0.0).astype(BF16)


def _compress(hblk, w1, pe, w2):
    bg, n, half = hblk.shape
    pe8 = jnp.zeros((8, 2 * half), BF16).at[0].set(pe.reshape(-1).astype(BF16))
    return pl.pallas_call(
        _compress_kernel,
        grid=(bg,),
        in_specs=[pl.BlockSpec((1, n, half), lambda i: (i, 0, 0)),
                  pl.BlockSpec((2 * half, HEAD_DIM), lambda i: (0, 0)),
                  pl.BlockSpec((8, 2 * half), lambda i: (0, 0)),
                  pl.BlockSpec((HEAD_DIM, HEAD_DIM), lambda i: (0, 0))],
        out_specs=pl.BlockSpec((1, n, HEAD_DIM), lambda i: (i, 0, 0)),
        out_shape=jax.ShapeDtypeStruct((bg, n, HEAD_DIM), BF16),
        compiler_params=_cparams(("parallel",)),
        name="compress",
    )(hblk, w1.astype(BF16), pe8, w2.astype(BF16))


def _cmp_kernel(q_ref, kc_ref, vc_ref, gates_ref, ov_ref, ocmp_ref, selb_ref, *, tq, nsel):
    g = pl.program_id(0) % N_KV_GROUPS
    q0 = pl.program_id(1) * tq
    kc = kc_ref[0]
    vc = vc_ref[0]
    nc = kc.shape[0]
    t_idx = q0 + lax.broadcasted_iota(jnp.int32, (tq, nc), 0)
    c_idx = lax.broadcasted_iota(jnp.int32, (tq, nc), 1)
    c_ok = c_idx * CMP_STRIDE + (CMP_BLOCK - 1) <= t_idx
    gl = gates_ref[...]
    lane = lax.broadcasted_iota(jnp.int32, (tq, LANES), 1)
    psum = jnp.zeros((tq, nc), F32)
    for h in range(HEADS_PER_GROUP):
        qh = q_ref[:, h * HEAD_DIM:(h + 1) * HEAD_DIM]
        s = jnp.where(c_ok, _dot_nt(qh, kc), -jnp.inf)
        m = jnp.max(s, axis=-1, keepdims=True)
        m = jnp.where(m == -jnp.inf, 0.0, m)
        p = jnp.exp2(s - m)
        p = p / jnp.maximum(jnp.sum(p, axis=-1, keepdims=True), 1e-30)
        psum = psum + p
        o = _dot(p.astype(BF16), vc)
        glog = _lane_col(gl, lane, (g * HEADS_PER_GROUP + h) * 3)
        ocmp_ref[:, h * HEAD_DIM:(h + 1) * HEAD_DIM] = (o * _sigmoid(glog)).astype(BF16)

    imp = _dot(psum, ov_ref[...], precision=lax.Precision.HIGHEST)
    cur = (q0 + lax.broadcasted_iota(jnp.int32, (tq, LANES), 0)) // SEL_BLOCK
    forced = (lane == 0) | (lane == cur) | (lane == cur - 1)
    imp = jnp.where(lane <= cur, imp + jnp.where(forced, FORCE_BONUS, 0.0), -jnp.inf)
    imp_t = imp.T[0:nsel, :]
    sub = 8
    groups = [imp_t[g * sub:(g + 1) * sub, :] for g in range(nsel // sub)]
    cnts = [jnp.zeros((sub, tq), F32) for _ in groups]
    row = lax.broadcasted_iota(jnp.int32, (sub, tq), 0)
    for k in range(nsel):
        vk = imp_t[k:k + 1, :]
        for g, grp in enumerate(groups):
            if k < g * sub:
                beats = vk >= grp
            elif k >= (g + 1) * sub:
                beats = vk > grp
            else:
                beats = (vk > grp) | ((vk == grp) & (row > k - g * sub))
            cnts[g] = cnts[g] + jnp.where(beats, 1.0, 0.0)
    cnt = jnp.concatenate(cnts, axis=0)
    sel = (cnt < float(N_SEL)) & (imp_t > -jnp.inf)
    bias_t = jnp.where(sel, 0.0, -MASK_BIG)
    if nsel < LANES:
        bias_t = jnp.concatenate([bias_t, jnp.zeros((LANES - nsel, tq), F32)], axis=0)
    selb_ref[...] = bias_t.T.astype(BF16)


def _cmp_attn(proj, kc, vc, gates, batch, seq, tq):
    t = proj.shape[0]
    nq = seq // tq
    nc = kc.shape[1]
    nsel = seq // SEL_BLOCK
    assert nsel <= LANES and nsel % 8 == 0
    c_start = np.arange(nc) * CMP_STRIDE
    j_start = np.arange(LANES) * SEL_BLOCK
    ov = ((c_start[:, None] < j_start[None, :] + SEL_BLOCK)
          & (c_start[:, None] + CMP_BLOCK > j_start[None, :])
          & (np.arange(LANES)[None, :] < nsel)).astype(np.float32)
    gq = HEADS_PER_GROUP * HEAD_DIM
    qmap = lambda bg, i: ((bg // N_KV_GROUPS) * nq + i, bg % N_KV_GROUPS)
    return pl.pallas_call(
        functools.partial(_cmp_kernel, tq=tq, nsel=nsel),
        grid=(batch * N_KV_GROUPS, nq),
        in_specs=[pl.BlockSpec((tq, gq), qmap),
                  pl.BlockSpec((1, nc, HEAD_DIM), lambda bg, i: (bg, 0, 0)),
                  pl.BlockSpec((1, nc, HEAD_DIM), lambda bg, i: (bg, 0, 0)),
                  pl.BlockSpec((tq, LANES), lambda bg, i: ((bg // N_KV_GROUPS) * nq + i, 0)),
                  pl.BlockSpec((nc, LANES), lambda bg, i: (0, 0))],
        out_specs=[pl.BlockSpec((tq, gq), qmap),
                   pl.BlockSpec((tq, LANES), lambda bg, i: (bg * nq + i, 0))],
        out_shape=[jax.ShapeDtypeStruct((t, ATT_W), BF16),
                   jax.ShapeDtypeStruct((batch * N_KV_GROUPS * seq, LANES), BF16)],
        compiler_params=_cparams(("parallel", "parallel")),
        name="cmp_attn",
    )(proj, kc, vc, gates, jnp.asarray(ov))


def _selwin_kernel(q_ref, selb_ref, ocmp_ref, gates_ref, ks_ref, vs_ref, kw_ref, vw_ref,
                   cb_ref, wb_ref, o_ref, kaug_s, vaug_s, vwaug_s, qaug_s, sbuf_s, m_s, acc_s,
                   swin_s, *, tq, tk, seq, nwin):
    g = pl.program_id(0) % N_KV_GROUPS
    qi = pl.program_id(1)
    q0 = qi * tq
    hq = HEADS_PER_GROUP * tq
    half = hq // 2

    @pl.when(qi == 0)
    def _():
        r = lax.broadcasted_iota(jnp.int32, (seq, HEAD_DIM), 0)
        ln = lax.broadcasted_iota(jnp.int32, (seq, HEAD_DIM), 1)
        ones_col = jnp.where(ln == 0, 1.0, 0.0).astype(BF16)
        kaug_s[:, 0:HEAD_DIM] = ks_ref[...]
        kaug_s[:, HEAD_DIM:2 * HEAD_DIM] = jnp.where(r // SEL_BLOCK == ln, 1.0, 0.0).astype(BF16)
        vaug_s[:, 0:HEAD_DIM] = vs_ref[...]
        vaug_s[:, HEAD_DIM:2 * HEAD_DIM] = ones_col
        vwaug_s[:, 0:HEAD_DIM] = vw_ref[...]
        vwaug_s[:, HEAD_DIM:2 * HEAD_DIM] = ones_col

    for h in range(HEADS_PER_GROUP):
        qaug_s[h * tq:(h + 1) * tq, 0:HEAD_DIM] = q_ref[:, h * HEAD_DIM:(h + 1) * HEAD_DIM]
        qaug_s[h * tq:(h + 1) * tq, HEAD_DIM:2 * HEAD_DIM] = selb_ref[...]
    qa = qaug_s[...]

    def scores(kt, slot):
        ks = kaug_s[pl.ds(pl.multiple_of(kt * tk, tk), tk), :]
        sbuf_s[slot] = _dot_nt(qa, ks)

    def absorb(kt, slot, diagonal=False):
        vs = vaug_s[pl.ds(pl.multiple_of(kt * tk, tk), tk), :]
        s = sbuf_s[slot]
        if diagonal:
            bias = cb_ref[(q0 - kt * tk) // tq]
            s = (s.reshape(HEADS_PER_GROUP, tq, tk) + bias[None]).reshape(hq, tk)
        m = m_s[...]
        m_new = jnp.maximum(m, jnp.max(s, axis=-1, keepdims=True))
        m_s[...] = m_new
        pb = jnp.exp2(s - m_new[:, 0:1]).astype(BF16)
        if diagonal:
            pv = jnp.concatenate([_dot(pb[0:half], vs), _dot(pb[half:hq], vs)], axis=0)
        else:
            pv = _dot(pb, vs)
        acc_s[...] = jnp.exp2(m - m_new)[:, 0:1] * acc_s[...] + pv

    n_full = q0 // tk
    m_s[...] = jnp.full(m_s.shape, -jnp.inf, F32)
    acc_s[...] = jnp.zeros(acc_s.shape, F32)
    scores(0, 0)

    def pair(j, c):
        scores(2 * j + 1, 1)
        absorb(2 * j, 0)
        scores(2 * j + 2, 0)
        absorb(2 * j + 1, 1)
        return c

    lax.fori_loop(0, n_full // 2, pair, 0)
    odd = n_full % 2

    @pl.when(odd == 1)
    def _():
        scores(n_full, 1)
        absorb(n_full - 1, 0)

    nq = seq // tq
    w0 = jnp.clip(qi - (nwin - 1), 0, nq - nwin) * tq
    wlen = nwin * tq
    kwin = kw_ref[pl.ds(pl.multiple_of(w0, tq), wlen), :]
    vwin = vwaug_s[pl.ds(pl.multiple_of(w0, tq), wlen), :]
    swin_s[0:half, :] = _dot_nt(qaug_s[0:half, 0:HEAD_DIM], kwin)
    swin_s[half:hq, :] = _dot_nt(qaug_s[half:hq, 0:HEAD_DIM], kwin)
    absorb(n_full, odd, diagonal=True)
    acc = acc_s[...]
    o_sel = acc[:, 0:HEAD_DIM] / acc[:, HEAD_DIM:HEAD_DIM + 1]

    wbias = wb_ref[(q0 - w0) // tq]
    sw = (swin_s[...].reshape(HEADS_PER_GROUP, tq, wlen) + wbias[None]).reshape(hq, wlen)
    pwb = jnp.exp2(sw - jnp.max(sw, axis=-1, keepdims=True)).astype(BF16)
    ow = jnp.concatenate([_dot(pwb[0:half], vwin), _dot(pwb[half:hq], vwin)], axis=0)
    o_win = ow[:, 0:HEAD_DIM] / ow[:, HEAD_DIM:HEAD_DIM + 1]

    gl = gates_ref[...]
    lane = lax.broadcasted_iota(jnp.int32, (tq, LANES), 1)
    for h in range(HEADS_PER_GROUP):
        gcol = (g * HEADS_PER_GROUP + h) * 3
        g_s = _sigmoid(_lane_col(gl, lane, gcol + 1))
        g_w = _sigmoid(_lane_col(gl, lane, gcol + 2))
        o = (ocmp_ref[:, h * HEAD_DIM:(h + 1) * HEAD_DIM].astype(F32)
             + g_s * o_sel[h * tq:(h + 1) * tq, :] + g_w * o_win[h * tq:(h + 1) * tq, :])
        o_ref[:, h * HEAD_DIM:(h + 1) * HEAD_DIM] = o.astype(BF16)


def _selwin_attn(proj, selb, ocmp, gates, batch, seq, tq, tk):
    t = proj.shape[0]
    nq = seq // tq
    assert tk % tq == 0 and tq & (tq - 1) == 0 and seq % tk == 0
    nwin = min(WINDOW // tq + 1, nq)
    assert WINDOW % tq == 0
    gq = HEADS_PER_GROUP * HEAD_DIM
    hq = HEADS_PER_GROUP * tq
    r = np.arange(tq)[:, None]
    cbias = np.stack([np.where(np.arange(tk)[None, :] <= r + off * tq, 0.0, -np.inf)
                      for off in range(tk // tq)]).astype(np.float32)
    dist = [r + off * tq - np.arange(nwin * tq)[None, :] for off in range(nwin)]
    wbias = np.stack([np.where((d >= 0) & (d < WINDOW), 0.0, -np.inf)
                      for d in dist]).astype(np.float32)
    qmap = lambda bg, i: ((bg // N_KV_GROUPS) * nq + i, bg % N_KV_GROUPS)

    def kvspec(col):
        return pl.BlockSpec((seq, HEAD_DIM),
                            lambda bg, i: (bg // N_KV_GROUPS, col // HEAD_DIM + bg % N_KV_GROUPS))

    return pl.pallas_call(
        functools.partial(_selwin_kernel, tq=tq, tk=tk, seq=seq, nwin=nwin),
        grid=(batch * N_KV_GROUPS, nq),
        in_specs=[pl.BlockSpec((tq, gq), qmap),
                  pl.BlockSpec((tq, LANES), lambda bg, i: (bg * nq + i, 0)),
                  pl.BlockSpec((tq, gq), qmap),
                  pl.BlockSpec((tq, LANES), lambda bg, i: ((bg // N_KV_GROUPS) * nq + i, 0)),
                  kvspec(COL_KS), kvspec(COL_VS), kvspec(COL_KW), kvspec(COL_VW),
                  pl.BlockSpec(cbias.shape, lambda bg, i: (0, 0, 0)),
                  pl.BlockSpec(wbias.shape, lambda bg, i: (0, 0, 0))],
        out_specs=pl.BlockSpec((tq, gq), qmap),
        out_shape=jax.ShapeDtypeStruct((t, ATT_W), BF16),
        scratch_shapes=[pltpu.VMEM((seq, 2 * HEAD_DIM), BF16),
                        pltpu.VMEM((seq, 2 * HEAD_DIM), BF16),
                        pltpu.VMEM((seq, 2 * HEAD_DIM), BF16),
                        pltpu.VMEM((hq, 2 * HEAD_DIM), BF16),
                        pltpu.VMEM((2, hq, tk), F32),
                        pltpu.VMEM((hq, LANES), F32),
                        pltpu.VMEM((hq, 2 * HEAD_DIM), F32),
                        pltpu.VMEM((hq, nwin * tq), F32)],
        compiler_params=_cparams(("parallel", "arbitrary")),
        name="selwin_attn",
    )(proj, selb, ocmp, gates, proj, proj, proj, proj, jnp.asarray(cbias), jnp.asarray(wbias))


_HGRN_LEVELS = (32, 16, 8, 4, 2, 1)


def _hgrn_tables():
    c = HGRN_CHUNK
    nl = len(_HGRN_LEVELS)
    idx = np.arange(c)
    i, j = idx[:, None], idx[None, :]
    mats = [j <= i, j > i]
    lvl = np.full((c, c), nl + 1, np.int32)
    for li, s in enumerate(_HGRN_LEVELS):
        mats.append((j >= (i // s) * s) & (j <= i))
        mats.append((j > i) & (j <= (i // s) * s + s - 1))
        lvl[((i // (2 * s)) == (j // (2 * s))) & (((i // s) % 2) == 1) & (((j // s) % 2) == 0)] = li
    lvl[i == j] = nl
    w = np.concatenate(mats, 0).astype(np.float32)
    return np.concatenate([w, w], axis=1), lvl


def _hgrn_head(hq_ref, hf_ref, hi_ref, hg_ref, lbl_ref, gn_ref, w_ref, lvl_ref, o_ref, st_s,
               hh, ts):
    c = HGRN_CHUNK
    nc = ts // c
    nl = len(_HGRN_LEVELS)
    cs = slice(hh * HEAD_DIM, (hh + 1) * HEAD_DIM)
    lbl = lbl_ref[:, cs]
    e = jnp.exp(lbl - jnp.max(lbl, axis=0, keepdims=True))
    lb = e[0:1, :] / jnp.sum(e, axis=0, keepdims=True)

    f = lb + (1.0 - lb) * _sigmoid(hf_ref[:, cs].astype(F32))
    k = 1.0 - f
    qf = _silu(hq_ref[:, cs].astype(F32))
    v = hi_ref[:, cs]

    lf = jnp.log(f) * LOG2E
    lf_hi = lf.astype(BF16)
    lf_lo = (lf - lf_hi.astype(F32)).astype(BF16)
    side = lambda a: jnp.concatenate([a[ci * c:(ci + 1) * c, :] for ci in range(nc)], axis=1)
    ex = _dot(w_ref[...], jnp.concatenate([side(lf_hi), side(lf_lo)], axis=0))

    def rows(bi):
        return jnp.concatenate(
            [ex[bi * c:(bi + 1) * c, ci * HEAD_DIM:(ci + 1) * HEAD_DIM] for ci in range(nc)], axis=0)

    b = rows(0)
    qhat = (qf * jnp.exp2(b)).astype(BF16)
    khat = (k * jnp.exp2(rows(1))).astype(BF16)
    lvl = lvl_ref[...]
    masks = [lvl == li for li in range(nl + 1)]
    qs = [(qf * jnp.exp2(rows(2 + 2 * li))).astype(BF16) for li in range(nl)] + [qf.astype(BF16)]
    ks = [(k * jnp.exp2(rows(3 + 2 * li))).astype(BF16) for li in range(nl)] + [k.astype(BF16)]
    o_intra = []
    for ci in range(nc):
        sl = slice(ci * c, (ci + 1) * c)
        a = jnp.zeros((c, c), F32)
        for li in range(nl + 1):
            a = jnp.where(masks[li], _dot_nt(qs[li][sl, :], ks[li][sl, :]), a)
        o_intra.append(_dot(a.astype(BF16), v[sl, :]))

    st = st_s[hh]
    outs = []
    for ci in range(nc):
        sl = slice(ci * c, (ci + 1) * c)
        outs.append(o_intra[ci] + _dot_nt(qhat[sl, :], st.astype(BF16)))
        st = st * jnp.exp2(b[(ci + 1) * c - 1:(ci + 1) * c, :]) + lax.dot_general(
            v[sl, :], khat[sl, :], _TN, preferred_element_type=F32)
    st_s[hh] = st
    o = jnp.concatenate(outs, axis=0)
    o = o * lax.rsqrt(jnp.mean(o * o, axis=-1, keepdims=True) + EPS) * gn_ref[:, cs]
    o_ref[:, cs] = (o * _silu(hg_ref[:, cs].astype(F32))).astype(BF16)


def _hgrn_kernel(hq_ref, hf_ref, hi_ref, hg_ref, lbl_ref, gn_ref, w_ref, lvl_ref,
                 o_ref, st_s, *, ts, hb):
    @pl.when(pl.program_id(2) == 0)
    def _():
        st_s[...] = jnp.zeros_like(st_s)

    for hh in range(hb):
        _hgrn_head(hq_ref, hf_ref, hi_ref, hg_ref, lbl_ref, gn_ref, w_ref, lvl_ref, o_ref, st_s,
                   hh, ts)


def _hgrn(proj, lb_logits, g_norm, batch, seq, ts, hb):
    t = proj.shape[0]
    ns = seq // ts
    wmat, lvl = _hgrn_tables()
    w = hb * HEAD_DIM
    assert all(col % w == 0 for col in (COL_HQ, COL_HF, COL_HI, COL_HG))

    def colspec(col):
        return pl.BlockSpec((ts, w), lambda b, h, s: (b * ns + s, col // w + h))

    nl = lb_logits.shape[0]
    return pl.pallas_call(
        functools.partial(_hgrn_kernel, ts=ts, hb=hb),
        grid=(batch, N_HGRN_HEADS // hb, ns),
        in_specs=[colspec(COL_HQ), colspec(COL_HF), colspec(COL_HI), colspec(COL_HG),
                  pl.BlockSpec((nl, w), lambda b, h, s: (0, h)),
                  pl.BlockSpec((1, w), lambda b, h, s: (0, h)),
                  pl.BlockSpec(wmat.shape, lambda b, h, s: (0, 0)),
                  pl.BlockSpec(lvl.shape, lambda b, h, s: (0, 0))],
        out_specs=pl.BlockSpec((ts, w), lambda b, h, s: (b * ns + s, h)),
        out_shape=jax.ShapeDtypeStruct((t, ATT_W), BF16),
        scratch_shapes=[pltpu.VMEM((hb, HEAD_DIM, HEAD_DIM), F32)],
        compiler_params=_cparams(("parallel", "parallel", "arbitrary")),
        name="hgrn2",
    )(proj, proj, proj, proj, lb_logits, g_norm, jnp.asarray(wmat, BF16), jnp.asarray(lvl))


def _outproj_kernel(oa_ref, or_ref, wa_ref, wb_ref, x_ref, mod_ref, gpost_ref, gpre_ref,
                    wr_ref, br_ref, x1_ref, h2_ref, lg_ref, stage, *, per_b):
    b = pl.program_id(0) // per_b
    y = _dot(oa_ref[...], wa_ref[...]) + _dot(or_ref[...], wb_ref[...])
    x1 = x_ref[...] + _mod_row(mod_ref, b, 2) * _rms(y, gpost_ref[...])
    x1_ref[...] = x1
    h = _rms(x1, gpre_ref[...]) * (1.0 + _mod_row(mod_ref, b, 4)) + _mod_row(mod_ref, b, 3)
    h2_ref[...] = _to_slabs(stage, h)
    h_hi = h.astype(BF16)
    h_lo = (h - h_hi.astype(F32)).astype(BF16)
    hh = _dot(h_hi, wr_ref[...])
    lg_ref[...] = (hh[:, 0:LANES] + hh[:, LANES:2 * LANES]
                   + _dot(h_lo, wr_ref[:, 0:LANES]) + br_ref[...])


def _outproj(o_att, o_rec, w_out, x2, mod, g_post, g_pre, wr, br, seq, tm):
    t, d = x2.shape
    per_b = seq // tm
    wr_hi = wr.astype(BF16)
    wr_cat = jnp.concatenate([wr_hi, (wr - wr_hi.astype(F32)).astype(BF16)], axis=1)
    wa = w_out[:ATT_W].astype(BF16)
    wb = w_out[ATT_W:].astype(BF16)
    row = lambda i: (i, 0)
    fixed = lambda i: (0, 0)
    return pl.pallas_call(
        functools.partial(_outproj_kernel, per_b=per_b),
        grid=(t // tm,),
        in_specs=[pl.BlockSpec((tm, ATT_W), row), pl.BlockSpec((tm, ATT_W), row),
                  pl.BlockSpec((ATT_W, d), fixed), pl.BlockSpec((ATT_W, d), fixed),
                  pl.BlockSpec((tm, d), row),
                  pl.BlockSpec(mod.shape, fixed),
                  pl.BlockSpec((1, d), fixed), pl.BlockSpec((1, d), fixed),
                  pl.BlockSpec((d, 2 * LANES), fixed), pl.BlockSpec((1, LANES), fixed)],
        out_specs=[pl.BlockSpec((tm, d), row), pl.BlockSpec((tm * (d // LANES), LANES), row),
                   pl.BlockSpec((tm, LANES), row)],
        out_shape=[jax.ShapeDtypeStruct((t, d), F32),
                   jax.ShapeDtypeStruct((t * (d // LANES), LANES), BF16),
                   jax.ShapeDtypeStruct((t, LANES), F32)],
        scratch_shapes=[pltpu.VMEM((tm * (d // LANES), LANES), F32)],
        compiler_params=_cparams(("parallel",)),
        name="outproj",
    )(o_att, o_rec, wa, wb, x2, mod, g_post, g_pre, wr_cat, br)


G_LANE0 = N_EXPERTS


def _route_kernel(lg_ref, tri_ref, info_ref, cnt_ref, carry_s):
    @pl.when(pl.program_id(0) == 0)
    def _():
        carry_s[...] = jnp.zeros_like(carry_s)

    x = lg_ref[...]
    lane = lax.broadcasted_iota(jnp.int32, x.shape, 1)
    big = jnp.int32(10 ** 6)
    rmax = lambda a: jnp.max(a, axis=-1, keepdims=True)
    rmin = lambda a: jnp.min(a, axis=-1, keepdims=True)
    rsum = lambda a: jnp.sum(a, axis=-1, keepdims=True)

    is_g = (lane >= G_LANE0) & (lane < G_LANE0 + N_EXPERT_GROUPS)
    lgm = jnp.where(is_g, x, -jnp.inf)
    mg = rmax(lgm)
    pg_sel = 1.0 / rsum(jnp.where(is_g, jnp.exp(lgm - mg), 0.0))
    gsel = rmin(jnp.where(lgm == mg, lane, big)) - G_LANE0

    is_e = (lane >= gsel * EXPERTS_PER_GROUP) & (lane < (gsel + 1) * EXPERTS_PER_GROUP)
    lem = jnp.where(is_e, x, -jnp.inf)
    pe = jnp.where(is_e, jnp.exp(lem - rmax(lem)), 0.0)
    pe = pe / rsum(pe)
    pe = jnp.where(is_e, pe, -1.0)
    v1 = rmax(pe)
    i1 = rmin(jnp.where(pe == v1, lane, big))
    pe2 = jnp.where(lane == i1, -1.0, pe)
    v2 = rmax(pe2)
    i2 = rmin(jnp.where(pe2 == v2, lane, big))
    w1 = v1 / (v1 + v2) * pg_sel
    w2 = v2 / (v1 + v2) * pg_sel

    oh1 = jnp.where(lane == i1, 1.0, 0.0)
    oh2 = jnp.where(lane == i2, 1.0, 0.0)
    both = oh1 + oh2
    before = _dot(tri_ref[...], both.astype(BF16)) + carry_s[0:1, :]
    r1 = rsum(oh1 * before)
    r2 = rsum(oh2 * before)
    carry_s[0:1, :] = carry_s[0:1, :] + jnp.sum(both, axis=0, keepdims=True)
    cnt_ref[...] = carry_s[...]

    cols = (i1.astype(F32), i2.astype(F32), w1, w2, r1, r2)
    info = jnp.zeros(x.shape, F32)
    for ci, col in enumerate(cols):
        info = jnp.where(lane == ci, col, info)
    info_ref[...] = info


def _route(logits, tm):
    t = logits.shape[0]
    tri = jnp.asarray(np.tril(np.ones((tm, tm), np.float32), -1), BF16)
    return pl.pallas_call(
        _route_kernel,
        grid=(t // tm,),
        in_specs=[pl.BlockSpec((tm, LANES), lambda i: (i, 0)),
                  pl.BlockSpec((tm, tm), lambda i: (0, 0))],
        out_specs=[pl.BlockSpec((tm, LANES), lambda i: (i, 0)),
                   pl.BlockSpec((8, LANES), lambda i: (0, 0))],
        out_shape=[jax.ShapeDtypeStruct((t, LANES), F32),
                   jax.ShapeDtypeStruct((8, LANES), F32)],
        scratch_shapes=[pltpu.VMEM((8, LANES), F32)],
        compiler_params=_cparams(("arbitrary",)),
        name="route",
    )(logits, tri)


def _to_slabs(stage, x):
    n, d = x.shape
    slab = d // LANES
    for c in range(slab):
        stage[pl.ds(c, n, stride=slab), :] = x[:, c * LANES:(c + 1) * LANES]
    return stage[...].astype(BF16)


def _from_slabs(stage, slabs, n):
    slab = slabs.shape[0] // n
    stage[...] = slabs.astype(F32)
    return jnp.concatenate([stage[pl.ds(c, n, stride=slab), :] for c in range(slab)], axis=1)


def _expert_kernel(te_ref, nu_ref, par_ref, nxt_ref, dst_ref, xs_ref, wg_hbm, wu_hbm, wd_hbm,
                   y_ref, obuf, stage, ssem, wg_f, wu_f, wd_f, wsem, wg_s, wu_s, wd_s, *,
                   tm, standin_row):
    t = pl.program_id(0)
    nu = nu_ref[0]
    slot = t % 2

    def weights(e, buf):
        return [pltpu.make_async_copy(src.at[e], dst.at[buf], wsem.at[buf])
                for src, dst in ((wg_hbm, wg_f), (wu_hbm, wu_f), (wd_hbm, wd_f))]
    slab = wg_s.shape[0] // LANES
    tile = tm * slab

    def token(ref, i):
        return ref.at[pl.ds(pl.multiple_of(i * slab, slab), slab)]

    def scatter(s):
        for r in range(tm):
            pltpu.make_async_copy(token(obuf, s * tm + r), token(y_ref, dst_ref[0, 0, r]),
                                  ssem.at[s]).start()

    def tile_of(buf, s):
        return buf.at[pl.ds(pl.multiple_of(s * tile, tile), tile)]

    def wait_tile(s):
        pltpu.make_async_copy(tile_of(obuf, s), tile_of(obuf, s), ssem.at[s]).wait()

    @pl.when(t < nu)
    def _():
        @pl.when(t == 0)
        def _():
            obuf[tile:2 * tile, :] = jnp.zeros((tile, LANES), obuf.dtype)

            def fill(row0, s):
                return pltpu.make_async_copy(tile_of(obuf, 1),
                                             y_ref.at[pl.ds(row0 * slab, tile)], ssem.at[s])

            fill(standin_row - 2 * tm, 0).start()
            fill(standin_row - tm, 0).start()
            fill(standin_row - 2 * tm, 0).wait()
            fill(standin_row - tm, 0).wait()
            fill(standin_row, 1).start()

        par = par_ref[t]

        @pl.when(t == 0)
        def _():
            for cp in weights(te_ref[0], par):
                cp.start()

        @pl.when((t == 0) | (te_ref[t] != te_ref[jnp.maximum(t - 1, 0)]))
        def _():
            for cp in weights(te_ref[t], par):
                cp.wait()
            wg_s[...] = wg_f[par].astype(BF16)
            wu_s[...] = wu_f[par].astype(BF16)
            wd_s[...] = wd_f[par].astype(BF16)

            @pl.when(nxt_ref[t] != te_ref[t])
            def _():
                for cp in weights(nxt_ref[t], 1 - par):
                    cp.start()

        x = _from_slabs(stage, xs_ref[...], tm).astype(BF16)
        hid = (_silu(_dot(x, wg_s[...])) * _dot(x, wu_s[...])).astype(BF16)
        row0 = pl.multiple_of(slot * tile, tile)
        obuf[pl.ds(row0, tile), :] = _to_slabs(stage, _dot(hid, wd_s[...]))
        scatter(slot)
        wait_tile(1 - slot)

        @pl.when(t == nu - 1)
        def _():
            wait_tile(slot)


def _experts(tile_expert, n_used, parity, next_expert, dst3, xs, w_gate, w_up, w_down, y_rows,
             tm, standin_row):
    n_tiles = dst3.shape[0]
    d, f = w_gate.shape[1:]
    slab = d // LANES
    hbm = pl.BlockSpec(memory_space=pl.ANY)
    return pl.pallas_call(
        functools.partial(_expert_kernel, tm=tm, standin_row=standin_row),
        grid_spec=pltpu.PrefetchScalarGridSpec(
            num_scalar_prefetch=4,
            grid=(n_tiles,),
            in_specs=[pl.BlockSpec((1, 1, tm), lambda t, te, nu, pa, nx: (t, 0, 0),
                                   memory_space=pltpu.SMEM),
                      pl.BlockSpec((tm * slab, LANES),
                                   lambda t, te, nu, pa, nx: (jnp.minimum(t, nu[0] - 1), 0)),
                      hbm, hbm, hbm],
            out_specs=pl.BlockSpec(memory_space=pl.ANY),
            scratch_shapes=[pltpu.VMEM((2 * tm * slab, LANES), BF16),
                            pltpu.VMEM((tm * slab, LANES), F32),
                            pltpu.SemaphoreType.DMA((2,)),
                            pltpu.VMEM((2, d, f), F32), pltpu.VMEM((2, d, f), F32),
                            pltpu.VMEM((2, f, d), F32),
                            pltpu.SemaphoreType.DMA((2,)),
                            pltpu.VMEM((d, f), BF16), pltpu.VMEM((d, f), BF16),
                            pltpu.VMEM((f, d), BF16)]),
        out_shape=jax.ShapeDtypeStruct((y_rows * slab, LANES), BF16),
        compiler_params=_cparams(("arbitrary",)),
        name="moe_experts",
    )(tile_expert, n_used, parity, next_expert, dst3, xs, w_gate, w_up, w_down)


def _dispatch_kernel(pos0_ref, pos1_ref, end_ref, nu_ref, h_ref, dst_in, xs_ref, dst_ref,
                     zbuf, zsem, sem, *, tm, slab, n_tiles, n_tokens):
    tile = tm * slab
    base = pl.program_id(0) * tm

    @pl.when(pl.program_id(0) == 0)
    def _():
        init = pltpu.make_async_copy(dst_in, dst_ref, zsem)
        init.start()
        init.wait()
        zbuf[...] = jnp.zeros(zbuf.shape, zbuf.dtype)

        def zero(row_end, go):
            cp = pltpu.make_async_copy(
                zbuf, xs_ref.at[pl.ds(pl.multiple_of((row_end - tm) * slab, slab), tile)], zsem)
            pl.when(go)(cp.start)
            return cp, go

        pending = [zero(end_ref[e], end_ref[e] > (end_ref[e - 1] if e else 0))
                   for e in range(N_EXPERTS)]
        pending += [zero((i + 1) * tm, i >= nu_ref[0]) for i in range(n_tiles)]
        for cp, go in pending:
            pl.when(go)(cp.wait)

    def token(ref, i):
        return ref.at[pl.ds(pl.multiple_of(i * slab, slab), slab)]

    def start(r, c):
        p0 = pos0_ref[0, 0, r]
        p1 = pos1_ref[0, 0, r]
        pltpu.make_async_copy(token(h_ref, r), token(xs_ref, p0), sem).start()
        pltpu.make_async_copy(token(h_ref, r), token(xs_ref, p1), sem).start()
        dst_ref[p0] = base + r
        dst_ref[p1] = n_tokens + base + r
        return c

    lax.fori_loop(0, tm, start, 0, unroll=8)
    for _ in range(2):
        pltpu.make_async_copy(h_ref, h_ref, sem).wait()


def _dispatch(pos0, pos1, tile_end, n_used, h2, dst0, n_tiles, tm_exp, tm):
    rows = h2.shape[0]
    slab = rows // pos0.shape[0]
    t = pos0.shape[0]
    assert tm == tm_exp
    per_tile = pl.BlockSpec((1, 1, tm), lambda i: (i, 0, 0), memory_space=pltpu.SMEM)
    whole = pl.BlockSpec(memory_space=pltpu.SMEM)
    return pl.pallas_call(
        functools.partial(_dispatch_kernel, tm=tm, slab=slab, n_tiles=n_tiles, n_tokens=t),
        grid=(t // tm,),
        in_specs=[per_tile, per_tile, whole, whole,
                  pl.BlockSpec((tm * slab, LANES), lambda i: (i, 0)),
                  pl.BlockSpec(memory_space=pl.ANY)],
        out_specs=[pl.BlockSpec(memory_space=pl.ANY), whole],
        out_shape=[jax.ShapeDtypeStruct((n_tiles * tm_exp * slab, LANES), h2.dtype),
                   jax.ShapeDtypeStruct(dst0.shape, jnp.int32)],
        scratch_shapes=[pltpu.VMEM((tm * slab, LANES), h2.dtype),
                        pltpu.SemaphoreType.DMA(()), pltpu.SemaphoreType.DMA(())],
        compiler_params=_cparams(("arbitrary",)),
        name="moe_dispatch",
    )(pos0.reshape(t // tm, 1, tm), pos1.reshape(t // tm, 1, tm), tile_end * tm_exp, n_used, h2,
      dst0)


def _final_kernel(info_ref, x1_ref, mod_ref, g_ref, ya_ref, yb_ref, o_ref, stage_a, stage_b, *,
                  per_b):
    info = info_ref[...]
    tm = x1_ref.shape[0]
    ya = _from_slabs(stage_a, ya_ref[...], tm)
    yb = _from_slabs(stage_b, yb_ref[...], tm)
    y = info[:, 2:3] * ya + info[:, 3:4] * yb
    gate = _mod_row(mod_ref, pl.program_id(0) // per_b, 5)
    o_ref[...] = x1_ref[...] + gate * _rms(y, g_ref[...])


def _moe_final(info, x1, mod, g_post, y2, seq, tm):
    t, d = x1.shape
    per_b = seq // tm
    nb = t // tm
    return pl.pallas_call(
        functools.partial(_final_kernel, per_b=per_b),
        grid=(nb,),
        in_specs=[pl.BlockSpec((tm, LANES), lambda i: (i, 0)),
                  pl.BlockSpec((tm, d), lambda i: (i, 0)),
                  pl.BlockSpec(mod.shape, lambda i: (0, 0)),
                  pl.BlockSpec((1, d), lambda i: (0, 0)),
                  pl.BlockSpec((tm * (d // LANES), LANES), lambda i: (i, 0)),
                  pl.BlockSpec((tm * (d // LANES), LANES), lambda i: (nb + i, 0))],
        out_specs=pl.BlockSpec((tm, d), lambda i: (i, 0)),
        out_shape=jax.ShapeDtypeStruct((t, d), F32),
        scratch_shapes=[pltpu.VMEM((tm * (d // LANES), LANES), F32)] * 2,
        compiler_params=_cparams(("parallel",)),
        name="moe_final",
    )(info, x1, mod, g_post, y2, y2)


def _moe(h2, logits, x1, mod, g_post, w_gate, w_up, w_down, seq, tm_route, tm_exp, tm_fin):
    t = x1.shape[0]
    info, cnt = _route(logits, tm_route)
    counts = cnt[0, :N_EXPERTS].astype(jnp.int32)
    tiles_e = (counts + tm_exp - 1) // tm_exp
    tile_end = jnp.cumsum(tiles_e)
    offs = (tile_end - tiles_e) * tm_exp
    n_used = tile_end[-1]

    def slot_of(k):
        e = info[:, k].astype(jnp.int32)
        start = jnp.sum(jnp.where(e[:, None] == jnp.arange(N_EXPERTS)[None, :], offs[None, :], 0),
                        axis=1)
        return start + info[:, 4 + k].astype(jnp.int32)

    n_tiles = (2 * t) // tm_exp + N_EXPERTS
    tidx = jnp.minimum(jnp.arange(n_tiles, dtype=jnp.int32), n_used - 1)
    tile_expert = jnp.sum(tidx[:, None] >= tile_end[None, :], axis=1).astype(jnp.int32)
    switches = jnp.concatenate([jnp.zeros((1,), jnp.int32),
                                (tile_expert[1:] != tile_expert[:-1]).astype(jnp.int32)])
    parity = jnp.cumsum(switches) % 2
    next_expert = tile_expert[jnp.minimum(tile_end[tile_expert], n_used - 1)]
    pos0, pos1 = slot_of(0), slot_of(1)
    slot_ids = jnp.arange(n_tiles * tm_exp, dtype=jnp.int32)
    spare = 2 * t + ((slot_ids // tm_exp) % 2) * tm_exp + slot_ids % tm_exp
    n_used1 = n_used.reshape(1).astype(jnp.int32)
    xs, dst = _dispatch(pos0, pos1, tile_end.astype(jnp.int32), n_used1, h2, spare, n_tiles,
                        tm_exp, tm_route)

    ew = w_gate.shape
    wg = w_gate.reshape(N_EXPERTS, ew[-2], ew[-1])
    wu = w_up.reshape(N_EXPERTS, ew[-2], ew[-1])
    wd = w_down.reshape(N_EXPERTS, ew[-1], ew[-2])
    y2 = _experts(tile_expert, n_used1, parity.astype(jnp.int32), next_expert.astype(jnp.int32),
                  dst.reshape(n_tiles, 1, tm_exp), xs, wg, wu, wd,
                  2 * t + 3 * tm_exp, tm_exp, 2 * t + 2 * tm_exp)
    return _moe_final(info, x1, mod, g_post, y2, seq, tm_fin)


def _permute_w_in(w_in):
    sizes = [ATT_W] + [KV_W] * 6 + [3 * N_ATT_HEADS] + [ATT_W] * 4
    cuts = np.cumsum(sizes)[:-1]
    q, kc, vc, ks, vs, kw, vw, gates, hq, hf, hi, hg = [
        p.astype(BF16) for p in jnp.split(w_in, cuts, axis=1)]
    w_p = jnp.concatenate([q, kc, ks, kw, vc, vs, vw, hq, hf, hi, hg], axis=1)
    w_g = jnp.pad(gates, ((0, 0), (0, LANES - gates.shape[1])))
    return w_p, w_g


def _block(x, c, positions, w_ada, b_ada, g_pre_mix, g_post_mix, g_pre_ffn, g_post_ffn,
           w_in, w_out, pe_k, w1_k, w2_k, pe_v, w1_v, w2_v, lb_logits, g_norm,
           w_group, b_group, w_router, b_router, w_gate, w_up, w_down, cfg):
    batch, seq, d = x.shape
    t = batch * seq
    x2 = x.reshape(t, d)

    c8 = jnp.zeros((8, d), F32).at[:batch].set(c)
    mod = _ada_mod(c8, w_ada, b_ada)
    cos_t, sin_t = _rope_tables(positions, cfg["tm_rope"])
    w_p, w_g = _permute_w_in(w_in)
    proj, gates, hbk, hbv = _inproj(x2, mod, g_pre_mix.reshape(1, d), w_p, w_g, cos_t, sin_t,
                                    seq, cfg["tm_in"])

    half_blocks = (batch * N_KV_GROUPS, seq // CMP_STRIDE, CMP_STRIDE * HEAD_DIM)
    kc = _compress(hbk.reshape(half_blocks), w1_k, pe_k, w2_k)
    vc = _compress(hbv.reshape(half_blocks), w1_v, pe_v, w2_v)
    ocmp, selb = _cmp_attn(proj, kc, vc, gates, batch, seq, cfg["tq_cmp"])
    o_att = _selwin_attn(proj, selb, ocmp, gates, batch, seq, cfg["tq"], cfg["tk"])
    o_rec = _hgrn(proj, lb_logits, g_norm.reshape(1, -1), batch, seq, cfg["ts_hgrn"],
                  cfg["hb_hgrn"])

    wr = jnp.concatenate([w_router, w_group], axis=1)
    wr = jnp.pad(wr, ((0, 0), (0, LANES - wr.shape[1])))
    br = jnp.pad(jnp.concatenate([b_router, b_group]), (0, LANES - N_EXPERTS - N_EXPERT_GROUPS))
    x1, h2, logits = _outproj(o_att, o_rec, w_out, x2, mod, g_post_mix.reshape(1, d),
                              g_pre_ffn.reshape(1, d), wr, br.reshape(1, LANES), seq,
                              cfg["tm_out"])
    out = _moe(h2, logits, x1, mod, g_post_ffn.reshape(1, d), w_gate, w_up, w_down, seq,
               cfg["tm_route"], cfg["tm_exp"], cfg["tm_fin"])
    return out.reshape(batch, seq, d)


def _config(seq):
    return dict(tm_rope=min(1024, seq), tm_in=min(1024, seq), tq_cmp=min(256, seq),
                tq=min(256, seq), tk=min(512, seq), ts_hgrn=min(256, seq), hb_hgrn=4,
                tm_out=min(256, seq), tm_route=min(256, seq), tm_exp=256,
                tm_fin=min(256, seq))


def kernel(x, c, positions, w_ada, b_ada, g_pre_mix, g_post_mix, g_pre_ffn, g_post_ffn, w_in, w_out, cmp_pe_k, cmp_w1_k, cmp_w2_k, cmp_pe_v, cmp_w1_v, cmp_w2_v, hgrn_lb_logits, hgrn_g_norm, w_group, b_group, w_router, b_router, w_gate, w_up, w_down):
    assert w_ada.shape[0] == 1, "single-layer block"
    return _block(x, c, positions, w_ada[0], b_ada[0], g_pre_mix[0], g_post_mix[0],
                  g_pre_ffn[0], g_post_ffn[0], w_in[0], w_out[0], cmp_pe_k[0], cmp_w1_k[0],
                  cmp_w2_k[0], cmp_pe_v[0], cmp_w1_v[0], cmp_w2_v[0], hgrn_lb_logits,
                  hgrn_g_norm[0], w_group[0], b_group[0], w_router[0], b_router[0],
                  w_gate[0], w_up[0], w_down[0], _config(x.shape[1]))
```

```python
import functools

import numpy as np
import jax
import jax.numpy as jnp
from jax import lax
from jax.experimental import pallas as pl
from jax.experimental.pallas import tpu as pltpu

F32 = jnp.float32
BF16 = jnp.bfloat16

HEAD_DIM = 128
N_KV_GROUPS = 2
HEADS_PER_GROUP = 4
N_ATT_HEADS = N_KV_GROUPS * HEADS_PER_GROUP
CMP_BLOCK = 32
CMP_STRIDE = 16
SEL_BLOCK = 64
N_SEL = 16
WINDOW = 512
FORCE_BONUS = 1.0e4
ROPE_THETA = 10000.0
N_HGRN_HEADS = 8
HGRN_CHUNK = 64
N_EXPERT_GROUPS = 4
EXPERTS_PER_GROUP = 8
N_EXPERTS = N_EXPERT_GROUPS * EXPERTS_PER_GROUP
EPS = 1e-6

LANES = 128
VMEM_LIMIT = 56 * 1024 * 1024

ATT_W = N_ATT_HEADS * HEAD_DIM
KV_W = N_KV_GROUPS * HEAD_DIM
COL_Q = 0
COL_KC = COL_Q + ATT_W
COL_KS = COL_KC + KV_W
COL_KW = COL_KS + KV_W
COL_VC = COL_KW + KV_W
COL_VS = COL_VC + KV_W
COL_VW = COL_VS + KV_W
COL_HQ = COL_VW + KV_W
COL_HF = COL_HQ + ATT_W
COL_HI = COL_HF + ATT_W
COL_HG = COL_HI + ATT_W
PROJ_W = COL_HG + ATT_W
ROPE_W = COL_VC

MASK_BIG = float(2 ** 30)
LOG2E = 1.4426950408889634
Q_PRESCALE = HEAD_DIM ** -0.5 * LOG2E

_NT = (((1,), (1,)), ((), ()))
_TN = (((0,), (0,)), ((), ()))


def _cparams(sem):
    return pltpu.CompilerParams(dimension_semantics=sem, vmem_limit_bytes=VMEM_LIMIT)


def _dot(a, b, **kw):
    return jnp.dot(a, b, preferred_element_type=F32, **kw)


def _dot_nt(a, b):
    return lax.dot_general(a, b, _NT, preferred_element_type=F32)


def _sigmoid(x):
    return 1.0 / (1.0 + jnp.exp(-x))


def _silu(x):
    return x * _sigmoid(x)


def _rms(x, g):
    return x * lax.rsqrt(jnp.mean(x * x, axis=-1, keepdims=True) + EPS) * g


def _mod_row(mod_ref, b, k):
    d = mod_ref.shape[1] // 6
    return mod_ref[pl.ds(b, 1), k * d:(k + 1) * d]


def _lane_col(x, lane, col):
    return jnp.sum(jnp.where(lane == col, x, 0.0), axis=-1, keepdims=True)


def _ada_kernel(c_ref, w_ref, b_ref, o_ref):
    s = _silu(c_ref[...])
    o_ref[...] = _dot(s, w_ref[...], precision=lax.Precision.HIGHEST) + b_ref[...]


def _ada_mod(c8, w_ada, b_ada, tn=512):
    rows, d = c8.shape
    n = w_ada.shape[1]
    return pl.pallas_call(
        _ada_kernel,
        grid=(n // tn,),
        in_specs=[pl.BlockSpec((rows, d), lambda j: (0, 0)),
                  pl.BlockSpec((d, tn), lambda j: (0, j)),
                  pl.BlockSpec((1, tn), lambda j: (0, j))],
        out_specs=pl.BlockSpec((rows, tn), lambda j: (0, j)),
        out_shape=jax.ShapeDtypeStruct((rows, n), F32),
        compiler_params=_cparams(("parallel",)),
        name="ada_mod",
    )(c8, w_ada, b_ada.reshape(1, n))


def _rope_kernel(ang_ref, cos_ref, sin_ref):
    a = ang_ref[...]
    lane = lax.broadcasted_iota(jnp.int32, a.shape, 1)
    cos_ref[...] = jnp.cos(a)
    sin_ref[...] = jnp.where(lane < HEAD_DIM // 2, -jnp.sin(a), jnp.sin(a))


def _rope_tables(positions, tm):
    t = positions.size
    inv_freq = ROPE_THETA ** (-jnp.arange(0, HEAD_DIM, 2, dtype=F32) / HEAD_DIM)
    ang = positions.reshape(t, 1).astype(F32) * jnp.concatenate([inv_freq, inv_freq])[None, :]
    spec = pl.BlockSpec((tm, HEAD_DIM), lambda i: (i, 0))
    return pl.pallas_call(
        _rope_kernel,
        grid=(t // tm,),
        in_specs=[spec],
        out_specs=[spec, spec],
        out_shape=[jax.ShapeDtypeStruct((t, HEAD_DIM), F32)] * 2,
        compiler_params=_cparams(("parallel",)),
        name="rope_tables",
    )(ang)


def _inproj_kernel(x_ref, mod_ref, g_ref, w_ref, wg_ref, cos_ref, sin_ref,
                   proj_ref, gates_ref, hbk_ref, hbv_ref, h_s, stage, *, tn, per_b):
    j = pl.program_id(1)

    @pl.when(j == 0)
    def _():
        b = pl.program_id(0) // per_b
        h = (_rms(x_ref[...], g_ref[...]) * (1.0 + _mod_row(mod_ref, b, 1))
             + _mod_row(mod_ref, b, 0))
        hb = h.astype(BF16)
        h_s[...] = hb
        gates_ref[...] = _dot(hb, wg_ref[...])

    heads = tn // HEAD_DIM
    rows = x_ref.shape[0]

    def store(jt):
        acc = _dot(h_s[...], w_ref[...])
        for c in range(heads):
            a = acc[:, c * HEAD_DIM:(c + 1) * HEAD_DIM]
            col = None if jt is None else jt * tn + c * HEAD_DIM
            if col is not None and col < ROPE_W:
                a = a * cos_ref[...] + pltpu.roll(a, HEAD_DIM // 2, axis=1) * sin_ref[...]
            if col is not None and col < ATT_W:
                a = a * Q_PRESCALE
            proj_ref[:, c * HEAD_DIM:(c + 1) * HEAD_DIM] = a.astype(BF16)
            for col0, hb_ref in ((COL_KC, hbk_ref), (COL_VC, hbv_ref)):
                if col is not None and col0 <= col < col0 + KV_W:
                    stage[...] = a
                    hb_ref[0, (col - col0) // HEAD_DIM] = jnp.concatenate(
                        [stage[pl.ds(i, rows // CMP_STRIDE, stride=CMP_STRIDE), :]
                         for i in range(CMP_STRIDE)], axis=1).astype(BF16)

    n_special = -(-COL_VS // tn)
    for jt in range(n_special):
        pl.when(j == jt)(functools.partial(store, jt))
    pl.when(j >= n_special)(functools.partial(store, None))


def _inproj(x2, mod, g_pre, w_in_p, w_gates, cos_t, sin_t, seq, tm, tn=512):
    t, d = x2.shape
    per_b = seq // tm
    assert COL_KC % tn + KV_W <= tn and COL_VC % tn + KV_W <= tn and tm % (16 * CMP_STRIDE) == 0
    hb_spec = pl.BlockSpec((1, N_KV_GROUPS, tm // CMP_STRIDE, CMP_STRIDE * HEAD_DIM),
                           lambda i, j: (i // per_b, 0, i % per_b, 0))
    hb_shape = jax.ShapeDtypeStruct(
        (t // seq, N_KV_GROUPS, seq // CMP_STRIDE, CMP_STRIDE * HEAD_DIM), BF16)
    return pl.pallas_call(
        functools.partial(_inproj_kernel, tn=tn, per_b=per_b),
        grid=(t // tm, PROJ_W // tn),
        in_specs=[pl.BlockSpec((tm, d), lambda i, j: (i, 0)),
                  pl.BlockSpec(mod.shape, lambda i, j: (0, 0)),
                  pl.BlockSpec((1, d), lambda i, j: (0, 0)),
                  pl.BlockSpec((d, tn), lambda i, j: (0, j)),
                  pl.BlockSpec((d, LANES), lambda i, j: (0, 0)),
                  pl.BlockSpec((tm, HEAD_DIM), lambda i, j: (i, 0)),
                  pl.BlockSpec((tm, HEAD_DIM), lambda i, j: (i, 0))],
        out_specs=[pl.BlockSpec((tm, tn), lambda i, j: (i, j)),
                   pl.BlockSpec((tm, LANES), lambda i, j: (i, 0)), hb_spec, hb_spec],
        out_shape=[jax.ShapeDtypeStruct((t, PROJ_W), BF16),
                   jax.ShapeDtypeStruct((t, LANES), F32), hb_shape, hb_shape],
        scratch_shapes=[pltpu.VMEM((tm, d), BF16), pltpu.VMEM((tm, HEAD_DIM), F32)],
        compiler_params=_cparams(("parallel", "arbitrary")),
        name="inproj",
    )(x2, mod, g_pre, w_in_p, w_gates, cos_t, sin_t)


def _compress_kernel(h_ref, w1_ref, pe_ref, w2_ref, o_ref):
    hb = h_ref[0]
    half = hb.shape[1]
    ya = _dot(hb, w1_ref[0:half, :])
    yb = _dot(hb, w1_ref[half:2 * half, :])
    const = _dot(pe_ref[...], w1_ref[...])
    n = ya.shape[0]
    yb_next = pltpu.roll(yb, n - 1, axis=0)
    hid = _silu(ya + yb_next + const[0:1, :])
    out = _dot(hid.astype(BF16), w2_ref[...])
    row = lax.broadcasted_iota(jnp.int32, out.shape, 0)
    o_ref[0] = jnp.where(row < n - 1, out, 0.0).astype(BF16)


def _compress(hblk, w1, pe, w2):
    bg, n, half = hblk.shape
    pe8 = jnp.zeros((8, 2 * half), BF16).at[0].set(pe.reshape(-1).astype(BF16))
    return pl.pallas_call(
        _compress_kernel,
        grid=(bg,),
        in_specs=[pl.BlockSpec((1, n, half), lambda i: (i, 0, 0)),
                  pl.BlockSpec((2 * half, HEAD_DIM), lambda i: (0, 0)),
                  pl.BlockSpec((8, 2 * half), lambda i: (0, 0)),
                  pl.BlockSpec((HEAD_DIM, HEAD_DIM), lambda i: (0, 0))],
        out_specs=pl.BlockSpec((1, n, HEAD_DIM), lambda i: (i, 0, 0)),
        out_shape=jax.ShapeDtypeStruct((bg, n, HEAD_DIM), BF16),
        compiler_params=_cparams(("parallel",)),
        name="compress",
    )(hblk, w1.astype(BF16), pe8, w2.astype(BF16))


def _cmp_kernel(q_ref, kc_ref, vc_ref, gates_ref, ov_ref, ocmp_ref, selb_ref, imp_s, cnt_s, *,
                tq, nsel):
    g = pl.program_id(0) % N_KV_GROUPS
    q0 = pl.program_id(1) * tq
    kc = kc_ref[0]
    vc = vc_ref[0]
    nc = kc.shape[0]
    t_idx = q0 + lax.broadcasted_iota(jnp.int32, (tq, nc), 0)
    c_idx = lax.broadcasted_iota(jnp.int32, (tq, nc), 1)
    c_ok = c_idx * CMP_STRIDE + (CMP_BLOCK - 1) <= t_idx
    gl = gates_ref[...]
    lane = lax.broadcasted_iota(jnp.int32, (tq, LANES), 1)
    psum = jnp.zeros((tq, nc), F32)
    for h in range(HEADS_PER_GROUP):
        qh = q_ref[:, h * HEAD_DIM:(h + 1) * HEAD_DIM]
        s = jnp.where(c_ok, _dot_nt(qh, kc), -jnp.inf)
        m = jnp.max(s, axis=-1, keepdims=True)
        m = jnp.where(m == -jnp.inf, 0.0, m)
        p = jnp.exp2(s - m)
        p = p * (1.0 / jnp.maximum(jnp.sum(p, axis=-1, keepdims=True), 1e-30))
        psum = psum + p
        o = _dot(p.astype(BF16), vc)
        glog = _lane_col(gl, lane, (g * HEADS_PER_GROUP + h) * 3)
        ocmp_ref[:, h * HEAD_DIM:(h + 1) * HEAD_DIM] = (o * _sigmoid(glog)).astype(BF16)

    imp = _dot(psum, ov_ref[...], precision=lax.Precision.HIGHEST)
    cur = (q0 + lax.broadcasted_iota(jnp.int32, (tq, LANES), 0)) // SEL_BLOCK
    forced = (lane == 0) | (lane == cur) | (lane == cur - 1)
    imp = jnp.where(lane <= cur, imp + jnp.where(forced, FORCE_BONUS, 0.0), -jnp.inf)
    imp_s[...] = imp.T[0:nsel, :]
    cnt_s[...] = jnp.zeros(cnt_s.shape, F32)
    sub = 8
    n_groups = nsel // sub
    row = lax.broadcasted_iota(jnp.int32, (sub, tq), 0)

    def count_against(kg):
        groups = [imp_s[g * sub:(g + 1) * sub, :] for g in range(n_groups)]
        cnts = [cnt_s[g * sub:(g + 1) * sub, :] for g in range(n_groups)]
        for k in range(kg * sub, (kg + 1) * sub):
            vk = imp_s[k:k + 1, :]
            for g, grp in enumerate(groups):
                if k < g * sub:
                    beats = vk >= grp
                elif k >= (g + 1) * sub:
                    beats = vk > grp
                else:
                    beats = (vk > grp) | ((vk == grp) & (row > k - g * sub))
                cnts[g] = cnts[g] + jnp.where(beats, 1.0, 0.0)
        for g in range(n_groups):
            cnt_s[g * sub:(g + 1) * sub, :] = cnts[g]

    last_block = (q0 + tq - 1) // SEL_BLOCK
    for kg in range(n_groups):
        pl.when(kg * sub <= last_block)(functools.partial(count_against, kg))
    imp_t = imp_s[...]
    sel = (cnt_s[...] < float(N_SEL)) & (imp_t > -jnp.inf)
    bias_t = jnp.where(sel, 0.0, -MASK_BIG)
    if nsel < LANES:
        bias_t = jnp.concatenate([bias_t, jnp.zeros((LANES - nsel, tq), F32)], axis=0)
    selb_ref[...] = bias_t.T.astype(BF16)


def _cmp_attn(proj, kc, vc, gates, batch, seq, tq):
    t = proj.shape[0]
    nq = seq // tq
    nc = kc.shape[1]
    nsel = seq // SEL_BLOCK
    assert nsel <= LANES and nsel % 8 == 0
    c_start = np.arange(nc) * CMP_STRIDE
    j_start = np.arange(LANES) * SEL_BLOCK
    ov = ((c_start[:, None] < j_start[None, :] + SEL_BLOCK)
          & (c_start[:, None] + CMP_BLOCK > j_start[None, :])
          & (np.arange(LANES)[None, :] < nsel)).astype(np.float32)
    gq = HEADS_PER_GROUP * HEAD_DIM
    qmap = lambda bg, i: ((bg // N_KV_GROUPS) * nq + i, bg % N_KV_GROUPS)
    return pl.pallas_call(
        functools.partial(_cmp_kernel, tq=tq, nsel=nsel),
        grid=(batch * N_KV_GROUPS, nq),
        in_specs=[pl.BlockSpec((tq, gq), qmap),
                  pl.BlockSpec((1, nc, HEAD_DIM), lambda bg, i: (bg, 0, 0)),
                  pl.BlockSpec((1, nc, HEAD_DIM), lambda bg, i: (bg, 0, 0)),
                  pl.BlockSpec((tq, LANES), lambda bg, i: ((bg // N_KV_GROUPS) * nq + i, 0)),
                  pl.BlockSpec((nc, LANES), lambda bg, i: (0, 0))],
        out_specs=[pl.BlockSpec((tq, gq), qmap),
                   pl.BlockSpec((tq, LANES), lambda bg, i: (bg * nq + i, 0))],
        out_shape=[jax.ShapeDtypeStruct((t, ATT_W), BF16),
                   jax.ShapeDtypeStruct((batch * N_KV_GROUPS * seq, LANES), BF16)],
        scratch_shapes=[pltpu.VMEM((nsel, tq), F32), pltpu.VMEM((nsel, tq), F32)],
        compiler_params=_cparams(("parallel", "parallel")),
        name="cmp_attn",
    )(proj, kc, vc, gates, jnp.asarray(ov))


def _selwin_kernel(q_ref, selb_ref, ocmp_ref, gates_ref, ks_ref, vs_ref, kw_ref, vw_ref,
                   cb_ref, wb_ref, o_ref, kaug_s, vaug_s, vwaug_s, qaug_s, sbuf_s, m_s, acc_s,
                   swin_s, *, tq, tk, seq, nwin):
    g = pl.program_id(0) % N_KV_GROUPS
    qi = pl.program_id(1)
    q0 = qi * tq
    hq = HEADS_PER_GROUP * tq
    half = hq // 2

    @pl.when(qi == 0)
    def _():
        r = lax.broadcasted_iota(jnp.int32, (seq, HEAD_DIM), 0)
        ln = lax.broadcasted_iota(jnp.int32, (seq, HEAD_DIM), 1)
        ones_col = jnp.where(ln == 0, 1.0, 0.0).astype(BF16)
        kaug_s[:, 0:HEAD_DIM] = ks_ref[...]
        kaug_s[:, HEAD_DIM:2 * HEAD_DIM] = jnp.where(r // SEL_BLOCK == ln, 1.0, 0.0).astype(BF16)
        vaug_s[:, 0:HEAD_DIM] = vs_ref[...]
        vaug_s[:, HEAD_DIM:2 * HEAD_DIM] = ones_col
        vwaug_s[:, 0:HEAD_DIM] = vw_ref[...]
        vwaug_s[:, HEAD_DIM:2 * HEAD_DIM] = ones_col

    for h in range(HEADS_PER_GROUP):
        qaug_s[h * tq:(h + 1) * tq, 0:HEAD_DIM] = q_ref[:, h * HEAD_DIM:(h + 1) * HEAD_DIM]
        qaug_s[h * tq:(h + 1) * tq, HEAD_DIM:2 * HEAD_DIM] = selb_ref[...]
    qa = qaug_s[...]

    def scores(kt, slot):
        ks = kaug_s[pl.ds(pl.multiple_of(kt * tk, tk), tk), :]
        sbuf_s[slot] = _dot_nt(qa, ks)

    def absorb(kt, slot, diagonal=False):
        vs = vaug_s[pl.ds(pl.multiple_of(kt * tk, tk), tk), :]
        s = sbuf_s[slot]
        if diagonal:
            bias = cb_ref[(q0 - kt * tk) // tq]
            s = (s.reshape(HEADS_PER_GROUP, tq, tk) + bias[None]).reshape(hq, tk)
        m = m_s[...]
        m_new = jnp.maximum(m, jnp.max(s, axis=-1, keepdims=True))
        m_s[...] = m_new
        pb = jnp.exp2(s - m_new[:, 0:1]).astype(BF16)
        if diagonal:
            pv = jnp.concatenate([_dot(pb[0:half], vs), _dot(pb[half:hq], vs)], axis=0)
        else:
            pv = _dot(pb, vs)
        acc_s[...] = jnp.exp2(m - m_new)[:, 0:1] * acc_s[...] + pv

    n_full = q0 // tk
    m_s[...] = jnp.full(m_s.shape, -jnp.inf, F32)
    acc_s[...] = jnp.zeros(acc_s.shape, F32)
    scores(0, 0)

    def pair(j, c):
        scores(2 * j + 1, 1)
        absorb(2 * j, 0)
        scores(2 * j + 2, 0)
        absorb(2 * j + 1, 1)
        return c

    lax.fori_loop(0, n_full // 2, pair, 0)
    odd = n_full % 2

    @pl.when(odd == 1)
    def _():
        scores(n_full, 1)
        absorb(n_full - 1, 0)

    nq = seq // tq
    w0 = jnp.clip(qi - (nwin - 1), 0, nq - nwin) * tq
    wlen = nwin * tq
    kwin = kw_ref[pl.ds(pl.multiple_of(w0, tq), wlen), :]
    vwin = vwaug_s[pl.ds(pl.multiple_of(w0, tq), wlen), :]
    swin_s[0:half, :] = _dot_nt(qaug_s[0:half, 0:HEAD_DIM], kwin)
    swin_s[half:hq, :] = _dot_nt(qaug_s[half:hq, 0:HEAD_DIM], kwin)
    absorb(n_full, odd, diagonal=True)
    acc = acc_s[...]
    o_sel = acc[:, 0:HEAD_DIM] / acc[:, HEAD_DIM:HEAD_DIM + 1]

    wbias = wb_ref[(q0 - w0) // tq]
    sw = (swin_s[...].reshape(HEADS_PER_GROUP, tq, wlen) + wbias[None]).reshape(hq, wlen)
    pwb = jnp.exp2(sw - jnp.max(sw, axis=-1, keepdims=True)).astype(BF16)
    ow = jnp.concatenate([_dot(pwb[0:half], vwin), _dot(pwb[half:hq], vwin)], axis=0)
    o_win = ow[:, 0:HEAD_DIM] / ow[:, HEAD_DIM:HEAD_DIM + 1]

    gl = gates_ref[...]
    lane = lax.broadcasted_iota(jnp.int32, (tq, LANES), 1)
    for h in range(HEADS_PER_GROUP):
        gcol = (g * HEADS_PER_GROUP + h) * 3
        g_s = _sigmoid(_lane_col(gl, lane, gcol + 1))
        g_w = _sigmoid(_lane_col(gl, lane, gcol + 2))
        o = (ocmp_ref[:, h * HEAD_DIM:(h + 1) * HEAD_DIM].astype(F32)
             + g_s * o_sel[h * tq:(h + 1) * tq, :] + g_w * o_win[h * tq:(h + 1) * tq, :])
        o_ref[:, h * HEAD_DIM:(h + 1) * HEAD_DIM] = o.astype(BF16)


def _selwin_attn(proj, selb, ocmp, gates, batch, seq, tq, tk):
    t = proj.shape[0]
    nq = seq // tq
    assert tk % tq == 0 and tq & (tq - 1) == 0 and seq % tk == 0
    nwin = min(WINDOW // tq + 1, nq)
    assert WINDOW % tq == 0
    gq = HEADS_PER_GROUP * HEAD_DIM
    hq = HEADS_PER_GROUP * tq
    r = np.arange(tq)[:, None]
    cbias = np.stack([np.where(np.arange(tk)[None, :] <= r + off * tq, 0.0, -np.inf)
                      for off in range(tk // tq)]).astype(np.float32)
    dist = [r + off * tq - np.arange(nwin * tq)[None, :] for off in range(nwin)]
    wbias = np.stack([np.where((d >= 0) & (d < WINDOW), 0.0, -np.inf)
                      for d in dist]).astype(np.float32)
    qmap = lambda bg, i: ((bg // N_KV_GROUPS) * nq + i, bg % N_KV_GROUPS)

    def kvspec(col):
        return pl.BlockSpec((seq, HEAD_DIM),
                            lambda bg, i: (bg // N_KV_GROUPS, col // HEAD_DIM + bg % N_KV_GROUPS))

    return pl.pallas_call(
        functools.partial(_selwin_kernel, tq=tq, tk=tk, seq=seq, nwin=nwin),
        grid=(batch * N_KV_GROUPS, nq),
        in_specs=[pl.BlockSpec((tq, gq), qmap),
                  pl.BlockSpec((tq, LANES), lambda bg, i: (bg * nq + i, 0)),
                  pl.BlockSpec((tq, gq), qmap),
                  pl.BlockSpec((tq, LANES), lambda bg, i: ((bg // N_KV_GROUPS) * nq + i, 0)),
                  kvspec(COL_KS), kvspec(COL_VS), kvspec(COL_KW), kvspec(COL_VW),
                  pl.BlockSpec(cbias.shape, lambda bg, i: (0, 0, 0)),
                  pl.BlockSpec(wbias.shape, lambda bg, i: (0, 0, 0))],
        out_specs=pl.BlockSpec((tq, gq), qmap),
        out_shape=jax.ShapeDtypeStruct((t, ATT_W), BF16),
        scratch_shapes=[pltpu.VMEM((seq, 2 * HEAD_DIM), BF16),
                        pltpu.VMEM((seq, 2 * HEAD_DIM), BF16),
                        pltpu.VMEM((seq, 2 * HEAD_DIM), BF16),
                        pltpu.VMEM((hq, 2 * HEAD_DIM), BF16),
                        pltpu.VMEM((2, hq, tk), F32),
                        pltpu.VMEM((hq, LANES), F32),
                        pltpu.VMEM((hq, 2 * HEAD_DIM), F32),
                        pltpu.VMEM((hq, nwin * tq), F32)],
        compiler_params=_cparams(("parallel", "arbitrary")),
        name="selwin_attn",
    )(proj, selb, ocmp, gates, proj, proj, proj, proj, jnp.asarray(cbias), jnp.asarray(wbias))


_HGRN_LEVELS = (32, 16, 8, 4, 2, 1)


def _hgrn_tables():
    c = HGRN_CHUNK
    nl = len(_HGRN_LEVELS)
    idx = np.arange(c)
    i, j = idx[:, None], idx[None, :]
    mats = [j <= i, j > i]
    lvl = np.full((c, c), nl + 1, np.int32)
    for li, s in enumerate(_HGRN_LEVELS):
        mats.append((j >= (i // s) * s) & (j <= i))
        mats.append((j > i) & (j <= (i // s) * s + s - 1))
        lvl[((i // (2 * s)) == (j // (2 * s))) & (((i // s) % 2) == 1) & (((j // s) % 2) == 0)] = li
    lvl[i == j] = nl
    w = np.concatenate(mats, 0).astype(np.float32)
    return np.concatenate([w, w], axis=1), lvl


def _hgrn_head(hq_ref, hf_ref, hi_ref, hg_ref, lbl_ref, gn_ref, w_ref, lvl_ref, o_ref, st_s,
               hh, ts):
    c = HGRN_CHUNK
    nc = ts // c
    nl = len(_HGRN_LEVELS)
    cs = slice(hh * HEAD_DIM, (hh + 1) * HEAD_DIM)
    lbl = lbl_ref[:, cs]
    e = jnp.exp(lbl - jnp.max(lbl, axis=0, keepdims=True))
    lb = e[0:1, :] / jnp.sum(e, axis=0, keepdims=True)

    f = lb + (1.0 - lb) * _sigmoid(hf_ref[:, cs].astype(F32))
    k = 1.0 - f
    qf = _silu(hq_ref[:, cs].astype(F32))
    v = hi_ref[:, cs]

    lf = jnp.log(f) * LOG2E
    lf_hi = lf.astype(BF16)
    lf_lo = (lf - lf_hi.astype(F32)).astype(BF16)
    side = lambda a: jnp.concatenate([a[ci * c:(ci + 1) * c, :] for ci in range(nc)], axis=1)
    ex = _dot(w_ref[...], jnp.concatenate([side(lf_hi), side(lf_lo)], axis=0))

    def rows(bi):
        return jnp.concatenate(
            [ex[bi * c:(bi + 1) * c, ci * HEAD_DIM:(ci + 1) * HEAD_DIM] for ci in range(nc)], axis=0)

    b = rows(0)
    qhat = (qf * jnp.exp2(b)).astype(BF16)
    khat = (k * jnp.exp2(rows(1))).astype(BF16)
    lvl = lvl_ref[...]
    masks = [lvl == li for li in range(nl + 1)]
    qs = [(qf * jnp.exp2(rows(2 + 2 * li))).astype(BF16) for li in range(nl)] + [qf.astype(BF16)]
    ks = [(k * jnp.exp2(rows(3 + 2 * li))).astype(BF16) for li in range(nl)] + [k.astype(BF16)]
    o_intra = []
    for ci in range(nc):
        sl = slice(ci * c, (ci + 1) * c)
        a = jnp.zeros((c, c), F32)
        for li in range(nl + 1):
            a = jnp.where(masks[li], _dot_nt(qs[li][sl, :], ks[li][sl, :]), a)
        o_intra.append(_dot(a.astype(BF16), v[sl, :]))

    st = st_s[hh]
    outs = []
    for ci in range(nc):
        sl = slice(ci * c, (ci + 1) * c)
        outs.append(o_intra[ci] + _dot_nt(qhat[sl, :], st.astype(BF16)))
        st = st * jnp.exp2(b[(ci + 1) * c - 1:(ci + 1) * c, :]) + lax.dot_general(
            v[sl, :], khat[sl, :], _TN, preferred_element_type=F32)
    st_s[hh] = st
    o = jnp.concatenate(outs, axis=0)
    o = o * lax.rsqrt(jnp.mean(o * o, axis=-1, keepdims=True) + EPS) * gn_ref[:, cs]
    o_ref[:, cs] = (o * _silu(hg_ref[:, cs].astype(F32))).astype(BF16)


def _hgrn_kernel(hq_ref, hf_ref, hi_ref, hg_ref, lbl_ref, gn_ref, w_ref, lvl_ref,
                 o_ref, st_s, *, ts, hb):
    @pl.when(pl.program_id(2) == 0)
    def _():
        st_s[...] = jnp.zeros_like(st_s)

    for hh in range(hb):
        _hgrn_head(hq_ref, hf_ref, hi_ref, hg_ref, lbl_ref, gn_ref, w_ref, lvl_ref, o_ref, st_s,
                   hh, ts)


def _hgrn(proj, lb_logits, g_norm, batch, seq, ts, hb):
    t = proj.shape[0]
    ns = seq // ts
    wmat, lvl = _hgrn_tables()
    w = hb * HEAD_DIM
    assert all(col % w == 0 for col in (COL_HQ, COL_HF, COL_HI, COL_HG))

    def colspec(col):
        return pl.BlockSpec((ts, w), lambda b, h, s: (b * ns + s, col // w + h))

    nl = lb_logits.shape[0]
    return pl.pallas_call(
        functools.partial(_hgrn_kernel, ts=ts, hb=hb),
        grid=(batch, N_HGRN_HEADS // hb, ns),
        in_specs=[colspec(COL_HQ), colspec(COL_HF), colspec(COL_HI), colspec(COL_HG),
                  pl.BlockSpec((nl, w), lambda b, h, s: (0, h)),
                  pl.BlockSpec((1, w), lambda b, h, s: (0, h)),
                  pl.BlockSpec(wmat.shape, lambda b, h, s: (0, 0)),
                  pl.BlockSpec(lvl.shape, lambda b, h, s: (0, 0))],
        out_specs=pl.BlockSpec((ts, w), lambda b, h, s: (b * ns + s, h)),
        out_shape=jax.ShapeDtypeStruct((t, ATT_W), BF16),
        scratch_shapes=[pltpu.VMEM((hb, HEAD_DIM, HEAD_DIM), F32)],
        compiler_params=_cparams(("parallel", "parallel", "arbitrary")),
        name="hgrn2",
    )(proj, proj, proj, proj, lb_logits, g_norm, jnp.asarray(wmat, BF16), jnp.asarray(lvl))


def _outproj_kernel(oa_ref, or_ref, wa_ref, wb_ref, x_ref, mod_ref, gpost_ref, gpre_ref,
                    wr_ref, br_ref, x1_ref, h2_ref, lg_ref, stage, *, per_b):
    b = pl.program_id(0) // per_b
    y = _dot(oa_ref[...], wa_ref[...]) + _dot(or_ref[...], wb_ref[...])
    x1 = x_ref[...] + _mod_row(mod_ref, b, 2) * _rms(y, gpost_ref[...])
    x1_ref[...] = x1
    h = _rms(x1, gpre_ref[...]) * (1.0 + _mod_row(mod_ref, b, 4)) + _mod_row(mod_ref, b, 3)
    h2_ref[...] = _to_slabs(stage, h)
    h_hi = h.astype(BF16)
    h_lo = (h - h_hi.astype(F32)).astype(BF16)
    hh = _dot(h_hi, wr_ref[...])
    lg_ref[...] = (hh[:, 0:LANES] + hh[:, LANES:2 * LANES]
                   + _dot(h_lo, wr_ref[:, 0:LANES]) + br_ref[...])


def _outproj(o_att, o_rec, w_out, x2, mod, g_post, g_pre, wr, br, seq, tm):
    t, d = x2.shape
    per_b = seq // tm
    wr_hi = wr.astype(BF16)
    wr_cat = jnp.concatenate([wr_hi, (wr - wr_hi.astype(F32)).astype(BF16)], axis=1)
    wa = w_out[:ATT_W].astype(BF16)
    wb = w_out[ATT_W:].astype(BF16)
    row = lambda i: (i, 0)
    fixed = lambda i: (0, 0)
    return pl.pallas_call(
        functools.partial(_outproj_kernel, per_b=per_b),
        grid=(t // tm,),
        in_specs=[pl.BlockSpec((tm, ATT_W), row), pl.BlockSpec((tm, ATT_W), row),
                  pl.BlockSpec((ATT_W, d), fixed), pl.BlockSpec((ATT_W, d), fixed),
                  pl.BlockSpec((tm, d), row),
                  pl.BlockSpec(mod.shape, fixed),
                  pl.BlockSpec((1, d), fixed), pl.BlockSpec((1, d), fixed),
                  pl.BlockSpec((d, 2 * LANES), fixed), pl.BlockSpec((1, LANES), fixed)],
        out_specs=[pl.BlockSpec((tm, d), row), pl.BlockSpec((tm * (d // LANES), LANES), row),
                   pl.BlockSpec((tm, LANES), row)],
        out_shape=[jax.ShapeDtypeStruct((t, d), F32),
                   jax.ShapeDtypeStruct((t * (d // LANES), LANES), BF16),
                   jax.ShapeDtypeStruct((t, LANES), F32)],
        scratch_shapes=[pltpu.VMEM((tm * (d // LANES), LANES), F32)],
        compiler_params=_cparams(("parallel",)),
        name="outproj",
    )(o_att, o_rec, wa, wb, x2, mod, g_post, g_pre, wr_cat, br)


G_LANE0 = N_EXPERTS


def _route_kernel(lg_ref, tri_ref, info_ref, cnt_ref, carry_s):
    @pl.when(pl.program_id(0) == 0)
    def _():
        carry_s[...] = jnp.zeros_like(carry_s)

    x = lg_ref[...]
    lane = lax.broadcasted_iota(jnp.int32, x.shape, 1)
    big = jnp.int32(10 ** 6)
    rmax = lambda a: jnp.max(a, axis=-1, keepdims=True)
    rmin = lambda a: jnp.min(a, axis=-1, keepdims=True)
    rsum = lambda a: jnp.sum(a, axis=-1, keepdims=True)

    is_g = (lane >= G_LANE0) & (lane < G_LANE0 + N_EXPERT_GROUPS)
    lgm = jnp.where(is_g, x, -jnp.inf)
    mg = rmax(lgm)
    pg_sel = 1.0 / rsum(jnp.where(is_g, jnp.exp(lgm - mg), 0.0))
    gsel = rmin(jnp.where(lgm == mg, lane, big)) - G_LANE0

    is_e = (lane >= gsel * EXPERTS_PER_GROUP) & (lane < (gsel + 1) * EXPERTS_PER_GROUP)
    lem = jnp.where(is_e, x, -jnp.inf)
    pe = jnp.where(is_e, jnp.exp(lem - rmax(lem)), 0.0)
    pe = pe / rsum(pe)
    pe = jnp.where(is_e, pe, -1.0)
    v1 = rmax(pe)
    i1 = rmin(jnp.where(pe == v1, lane, big))
    pe2 = jnp.where(lane == i1, -1.0, pe)
    v2 = rmax(pe2)
    i2 = rmin(jnp.where(pe2 == v2, lane, big))
    w1 = v1 / (v1 + v2) * pg_sel
    w2 = v2 / (v1 + v2) * pg_sel

    oh1 = jnp.where(lane == i1, 1.0, 0.0)
    oh2 = jnp.where(lane == i2, 1.0, 0.0)
    both = oh1 + oh2
    before = _dot(tri_ref[...], both.astype(BF16)) + carry_s[0:1, :]
    r1 = rsum(oh1 * before)
    r2 = rsum(oh2 * before)
    carry_s[0:1, :] = carry_s[0:1, :] + jnp.sum(both, axis=0, keepdims=True)
    cnt_ref[...] = carry_s[...]

    cols = (i1.astype(F32), i2.astype(F32), w1, w2, r1, r2)
    info = jnp.zeros(x.shape, F32)
    for ci, col in enumerate(cols):
        info = jnp.where(lane == ci, col, info)
    info_ref[...] = info


def _route(logits, tm):
    t = logits.shape[0]
    tri = jnp.asarray(np.tril(np.ones((tm, tm), np.float32), -1), BF16)
    return pl.pallas_call(
        _route_kernel,
        grid=(t // tm,),
        in_specs=[pl.BlockSpec((tm, LANES), lambda i: (i, 0)),
                  pl.BlockSpec((tm, tm), lambda i: (0, 0))],
        out_specs=[pl.BlockSpec((tm, LANES), lambda i: (i, 0)),
                   pl.BlockSpec((8, LANES), lambda i: (0, 0))],
        out_shape=[jax.ShapeDtypeStruct((t, LANES), F32),
                   jax.ShapeDtypeStruct((8, LANES), F32)],
        scratch_shapes=[pltpu.VMEM((8, LANES), F32)],
        compiler_params=_cparams(("arbitrary",)),
        name="route",
    )(logits, tri)


def _to_slabs(stage, x):
    n, d = x.shape
    slab = d // LANES
    for c in range(slab):
        stage[pl.ds(c, n, stride=slab), :] = x[:, c * LANES:(c + 1) * LANES]
    return stage[...].astype(BF16)


def _from_slabs(stage, slabs, n):
    slab = slabs.shape[0] // n
    stage[...] = slabs.astype(F32)
    return jnp.concatenate([stage[pl.ds(c, n, stride=slab), :] for c in range(slab)], axis=1)


def _expert_kernel(te_ref, nu_ref, par_ref, nxt_ref, dst_ref, xs_ref, wg_hbm, wu_hbm, wd_hbm,
                   y_ref, obuf, stage, ssem, wg_f, wu_f, wd_f, wsem, wg_s, wu_s, wd_s, *,
                   tm, standin_row):
    t = pl.program_id(0)
    nu = nu_ref[0]
    slot = t % 2

    def weights(e, buf):
        return [pltpu.make_async_copy(src.at[e], dst.at[buf], wsem.at[buf])
                for src, dst in ((wg_hbm, wg_f), (wu_hbm, wu_f), (wd_hbm, wd_f))]
    slab = wg_s.shape[0] // LANES
    tile = tm * slab

    def token(ref, i):
        return ref.at[pl.ds(pl.multiple_of(i * slab, slab), slab)]

    def scatter(s):
        for r in range(tm):
            pltpu.make_async_copy(token(obuf, s * tm + r), token(y_ref, dst_ref[0, 0, r]),
                                  ssem.at[s]).start()

    def tile_of(buf, s):
        return buf.at[pl.ds(pl.multiple_of(s * tile, tile), tile)]

    def wait_tile(s):
        pltpu.make_async_copy(tile_of(obuf, s), tile_of(obuf, s), ssem.at[s]).wait()

    @pl.when(t < nu)
    def _():
        @pl.when(t == 0)
        def _():
            obuf[tile:2 * tile, :] = jnp.zeros((tile, LANES), obuf.dtype)

            def fill(row0, s):
                return pltpu.make_async_copy(tile_of(obuf, 1),
                                             y_ref.at[pl.ds(row0 * slab, tile)], ssem.at[s])

            fill(standin_row - 2 * tm, 0).start()
            fill(standin_row - tm, 0).start()
            fill(standin_row - 2 * tm, 0).wait()
            fill(standin_row - tm, 0).wait()
            fill(standin_row, 1).start()

        par = par_ref[t]

        @pl.when(t == 0)
        def _():
            for cp in weights(te_ref[0], par):
                cp.start()

        @pl.when((t == 0) | (te_ref[t] != te_ref[jnp.maximum(t - 1, 0)]))
        def _():
            for cp in weights(te_ref[t], par):
                cp.wait()
            wg_s[...] = wg_f[par].astype(BF16)
            wu_s[...] = wu_f[par].astype(BF16)
            wd_s[...] = wd_f[par].astype(BF16)

            @pl.when(nxt_ref[t] != te_ref[t])
            def _():
                for cp in weights(nxt_ref[t], 1 - par):
                    cp.start()

        x = _from_slabs(stage, xs_ref[...], tm).astype(BF16)
        hid = (_silu(_dot(x, wg_s[...])) * _dot(x, wu_s[...])).astype(BF16)
        row0 = pl.multiple_of(slot * tile, tile)
        obuf[pl.ds(row0, tile), :] = _to_slabs(stage, _dot(hid, wd_s[...]))
        scatter(slot)
        wait_tile(1 - slot)

        @pl.when(t == nu - 1)
        def _():
            wait_tile(slot)


def _experts(tile_expert, n_used, parity, next_expert, dst3, xs, w_gate, w_up, w_down, y_rows,
             tm, standin_row):
    n_tiles = dst3.shape[0]
    d, f = w_gate.shape[1:]
    slab = d // LANES
    hbm = pl.BlockSpec(memory_space=pl.ANY)
    return pl.pallas_call(
        functools.partial(_expert_kernel, tm=tm, standin_row=standin_row),
        grid_spec=pltpu.PrefetchScalarGridSpec(
            num_scalar_prefetch=4,
            grid=(n_tiles,),
            in_specs=[pl.BlockSpec((1, 1, tm), lambda t, te, nu, pa, nx: (t, 0, 0),
                                   memory_space=pltpu.SMEM),
                      pl.BlockSpec((tm * slab, LANES),
                                   lambda t, te, nu, pa, nx: (jnp.minimum(t, nu[0] - 1), 0)),
                      hbm, hbm, hbm],
            out_specs=pl.BlockSpec(memory_space=pl.ANY),
            scratch_shapes=[pltpu.VMEM((2 * tm * slab, LANES), BF16),
                            pltpu.VMEM((tm * slab, LANES), F32),
                            pltpu.SemaphoreType.DMA((2,)),
                            pltpu.VMEM((2, d, f), F32), pltpu.VMEM((2, d, f), F32),
                            pltpu.VMEM((2, f, d), F32),
                            pltpu.SemaphoreType.DMA((2,)),
                            pltpu.VMEM((d, f), BF16), pltpu.VMEM((d, f), BF16),
                            pltpu.VMEM((f, d), BF16)]),
        out_shape=jax.ShapeDtypeStruct((y_rows * slab, LANES), BF16),
        compiler_params=_cparams(("arbitrary",)),
        name="moe_experts",
    )(tile_expert, n_used, parity, next_expert, dst3, xs, w_gate, w_up, w_down)


def _dispatch_kernel(pos0_ref, pos1_ref, end_ref, nu_ref, h_ref, dst_in, xs_ref, dst_ref,
                     zbuf, zsem, sem, *, tm, slab, n_tiles, n_tokens):
    tile = tm * slab
    base = pl.program_id(0) * tm

    @pl.when(pl.program_id(0) == 0)
    def _():
        init = pltpu.make_async_copy(dst_in, dst_ref, zsem)
        init.start()
        init.wait()
        zbuf[...] = jnp.zeros(zbuf.shape, zbuf.dtype)

        def zero(row_end, go):
            cp = pltpu.make_async_copy(
                zbuf, xs_ref.at[pl.ds(pl.multiple_of((row_end - tm) * slab, slab), tile)], zsem)
            pl.when(go)(cp.start)
            return cp, go

        pending = [zero(end_ref[e], end_ref[e] > (end_ref[e - 1] if e else 0))
                   for e in range(N_EXPERTS)]
        pending += [zero((i + 1) * tm, i >= nu_ref[0]) for i in range(n_tiles)]
        for cp, go in pending:
            pl.when(go)(cp.wait)

    def token(ref, i):
        return ref.at[pl.ds(pl.multiple_of(i * slab, slab), slab)]

    def start(r, c):
        p0 = pos0_ref[0, 0, r]
        p1 = pos1_ref[0, 0, r]
        pltpu.make_async_copy(token(h_ref, r), token(xs_ref, p0), sem).start()
        pltpu.make_async_copy(token(h_ref, r), token(xs_ref, p1), sem).start()
        dst_ref[p0] = base + r
        dst_ref[p1] = n_tokens + base + r
        return c

    lax.fori_loop(0, tm, start, 0, unroll=8)
    for _ in range(2):
        pltpu.make_async_copy(h_ref, h_ref, sem).wait()


def _dispatch(pos0, pos1, tile_end, n_used, h2, dst0, n_tiles, tm_exp, tm):
    rows = h2.shape[0]
    slab = rows // pos0.shape[0]
    t = pos0.shape[0]
    assert tm == tm_exp
    per_tile = pl.BlockSpec((1, 1, tm), lambda i: (i, 0, 0), memory_space=pltpu.SMEM)
    whole = pl.BlockSpec(memory_space=pltpu.SMEM)
    return pl.pallas_call(
        functools.partial(_dispatch_kernel, tm=tm, slab=slab, n_tiles=n_tiles, n_tokens=t),
        grid=(t // tm,),
        in_specs=[per_tile, per_tile, whole, whole,
                  pl.BlockSpec((tm * slab, LANES), lambda i: (i, 0)),
                  pl.BlockSpec(memory_space=pl.ANY)],
        out_specs=[pl.BlockSpec(memory_space=pl.ANY), whole],
        out_shape=[jax.ShapeDtypeStruct((n_tiles * tm_exp * slab, LANES), h2.dtype),
                   jax.ShapeDtypeStruct(dst0.shape, jnp.int32)],
        scratch_shapes=[pltpu.VMEM((tm * slab, LANES), h2.dtype),
                        pltpu.SemaphoreType.DMA(()), pltpu.SemaphoreType.DMA(())],
        compiler_params=_cparams(("arbitrary",)),
        name="moe_dispatch",
    )(pos0.reshape(t // tm, 1, tm), pos1.reshape(t // tm, 1, tm), tile_end * tm_exp, n_used, h2,
      dst0)


def _final_kernel(info_ref, x1_ref, mod_ref, g_ref, ya_ref, yb_ref, o_ref, stage_a, stage_b, *,
                  per_b):
    info = info_ref[...]
    tm = x1_ref.shape[0]
    ya = _from_slabs(stage_a, ya_ref[...], tm)
    yb = _from_slabs(stage_b, yb_ref[...], tm)
    y = info[:, 2:3] * ya + info[:, 3:4] * yb
    gate = _mod_row(mod_ref, pl.program_id(0) // per_b, 5)
    o_ref[...] = x1_ref[...] + gate * _rms(y, g_ref[...])


def _moe_final(info, x1, mod, g_post, y2, seq, tm):
    t, d = x1.shape
    per_b = seq // tm
    nb = t // tm
    return pl.pallas_call(
        functools.partial(_final_kernel, per_b=per_b),
        grid=(nb,),
        in_specs=[pl.BlockSpec((tm, LANES), lambda i: (i, 0)),
                  pl.BlockSpec((tm, d), lambda i: (i, 0)),
                  pl.BlockSpec(mod.shape, lambda i: (0, 0)),
                  pl.BlockSpec((1, d), lambda i: (0, 0)),
                  pl.BlockSpec((tm * (d // LANES), LANES), lambda i: (i, 0)),
                  pl.BlockSpec((tm * (d // LANES), LANES), lambda i: (nb + i, 0))],
        out_specs=pl.BlockSpec((tm, d), lambda i: (i, 0)),
        out_shape=jax.ShapeDtypeStruct((t, d), F32),
        scratch_shapes=[pltpu.VMEM((tm * (d // LANES), LANES), F32)] * 2,
        compiler_params=_cparams(("parallel",)),
        name="moe_final",
    )(info, x1, mod, g_post, y2, y2)


def _moe(h2, logits, x1, mod, g_post, w_gate, w_up, w_down, seq, tm_route, tm_exp, tm_fin):
    t = x1.shape[0]
    info, cnt = _route(logits, tm_route)
    counts = cnt[0, :N_EXPERTS].astype(jnp.int32)
    tiles_e = (counts + tm_exp - 1) // tm_exp
    tile_end = jnp.cumsum(tiles_e)
    offs = (tile_end - tiles_e) * tm_exp
    n_used = tile_end[-1]

    def slot_of(k):
        e = info[:, k].astype(jnp.int32)
        start = jnp.sum(jnp.where(e[:, None] == jnp.arange(N_EXPERTS)[None, :], offs[None, :], 0),
                        axis=1)
        return start + info[:, 4 + k].astype(jnp.int32)

    n_tiles = (2 * t) // tm_exp + N_EXPERTS
    tidx = jnp.minimum(jnp.arange(n_tiles, dtype=jnp.int32), n_used - 1)
    tile_expert = jnp.sum(tidx[:, None] >= tile_end[None, :], axis=1).astype(jnp.int32)
    switches = jnp.concatenate([jnp.zeros((1,), jnp.int32),
                                (tile_expert[1:] != tile_expert[:-1]).astype(jnp.int32)])
    parity = jnp.cumsum(switches) % 2
    next_expert = tile_expert[jnp.minimum(tile_end[tile_expert], n_used - 1)]
    pos0, pos1 = slot_of(0), slot_of(1)
    slot_ids = jnp.arange(n_tiles * tm_exp, dtype=jnp.int32)
    spare = 2 * t + ((slot_ids // tm_exp) % 2) * tm_exp + slot_ids % tm_exp
    n_used1 = n_used.reshape(1).astype(jnp.int32)
    xs, dst = _dispatch(pos0, pos1, tile_end.astype(jnp.int32), n_used1, h2, spare, n_tiles,
                        tm_exp, tm_route)

    ew = w_gate.shape
    wg = w_gate.reshape(N_EXPERTS, ew[-2], ew[-1])
    wu = w_up.reshape(N_EXPERTS, ew[-2], ew[-1])
    wd = w_down.reshape(N_EXPERTS, ew[-1], ew[-2])
    y2 = _experts(tile_expert, n_used1, parity.astype(jnp.int32), next_expert.astype(jnp.int32),
                  dst.reshape(n_tiles, 1, tm_exp), xs, wg, wu, wd,
                  2 * t + 3 * tm_exp, tm_exp, 2 * t + 2 * tm_exp)
    return _moe_final(info, x1, mod, g_post, y2, seq, tm_fin)


def _permute_w_in(w_in):
    sizes = [ATT_W] + [KV_W] * 6 + [3 * N_ATT_HEADS] + [ATT_W] * 4
    cuts = np.cumsum(sizes)[:-1]
    q, kc, vc, ks, vs, kw, vw, gates, hq, hf, hi, hg = [
        p.astype(BF16) for p in jnp.split(w_in, cuts, axis=1)]
    w_p = jnp.concatenate([q, kc, ks, kw, vc, vs, vw, hq, hf, hi, hg], axis=1)
    w_g = jnp.pad(gates, ((0, 0), (0, LANES - gates.shape[1])))
    return w_p, w_g


def _block(x, c, positions, w_ada, b_ada, g_pre_mix, g_post_mix, g_pre_ffn, g_post_ffn,
           w_in, w_out, pe_k, w1_k, w2_k, pe_v, w1_v, w2_v, lb_logits, g_norm,
           w_group, b_group, w_router, b_router, w_gate, w_up, w_down, cfg):
    batch, seq, d = x.shape
    t = batch * seq
    x2 = x.reshape(t, d)

    c8 = jnp.zeros((8, d), F32).at[:batch].set(c)
    mod = _ada_mod(c8, w_ada, b_ada)
    cos_t, sin_t = _rope_tables(positions, cfg["tm_rope"])
    w_p, w_g = _permute_w_in(w_in)
    proj, gates, hbk, hbv = _inproj(x2, mod, g_pre_mix.reshape(1, d), w_p, w_g, cos_t, sin_t,
                                    seq, cfg["tm_in"])

    half_blocks = (batch * N_KV_GROUPS, seq // CMP_STRIDE, CMP_STRIDE * HEAD_DIM)
    kc = _compress(hbk.reshape(half_blocks), w1_k, pe_k, w2_k)
    vc = _compress(hbv.reshape(half_blocks), w1_v, pe_v, w2_v)
    ocmp, selb = _cmp_attn(proj, kc, vc, gates, batch, seq, cfg["tq_cmp"])
    o_att = _selwin_attn(proj, selb, ocmp, gates, batch, seq, cfg["tq"], cfg["tk"])
    o_rec = _hgrn(proj, lb_logits, g_norm.reshape(1, -1), batch, seq, cfg["ts_hgrn"],
                  cfg["hb_hgrn"])

    wr = jnp.concatenate([w_router, w_group], axis=1)
    wr = jnp.pad(wr, ((0, 0), (0, LANES - wr.shape[1])))
    br = jnp.pad(jnp.concatenate([b_router, b_group]), (0, LANES - N_EXPERTS - N_EXPERT_GROUPS))
    x1, h2, logits = _outproj(o_att, o_rec, w_out, x2, mod, g_post_mix.reshape(1, d),
                              g_pre_ffn.reshape(1, d), wr, br.reshape(1, LANES), seq,
                              cfg["tm_out"])
    out = _moe(h2, logits, x1, mod, g_post_ffn.reshape(1, d), w_gate, w_up, w_down, seq,
               cfg["tm_route"], cfg["tm_exp"], cfg["tm_fin"])
    return out.reshape(batch, seq, d)


def _config(seq):
    return dict(tm_rope=min(1024, seq), tm_in=min(1024, seq), tq_cmp=min(256, seq),
                tq=min(256, seq), tk=min(512, seq), ts_hgrn=min(512, seq), hb_hgrn=4,
                tm_out=min(256, seq), tm_route=min(256, seq), tm_exp=256,
                tm_fin=min(256, seq))


def kernel(x, c, positions, w_ada, b_ada, g_pre_mix, g_post_mix, g_pre_ffn, g_post_ffn, w_in, w_out, cmp_pe_k, cmp_w1_k, cmp_w2_k, cmp_pe_v, cmp_w1_v, cmp_w2_v, hgrn_lb_logits, hgrn_g_norm, w_group, b_group, w_router, b_router, w_gate, w_up, w_down):
    assert w_ada.shape[0] == 1, "single-layer block"
    return _block(x, c, positions, w_ada[0], b_ada[0], g_pre_mix[0], g_post_mix[0],
                  g_pre_ffn[0], g_post_ffn[0], w_in[0], w_out[0], cmp_pe_k[0], cmp_w1_k[0],
                  cmp_w2_k[0], cmp_pe_v[0], cmp_w1_v[0], cmp_w2_v[0], hgrn_lb_logits,
                  hgrn_g_norm[0], w_group[0], b_group[0], w_router[0], b_router[0],
                  w_gate[0], w_up[0], w_down[0], _config(x.shape[1]))
```

```python
import functools

import numpy as np
import jax
import jax.numpy as jnp
from jax import lax
from jax.experimental import pallas as pl
from jax.experimental.pallas import tpu as pltpu

F32 = jnp.float32
BF16 = jnp.bfloat16

HEAD_DIM = 128
N_KV_GROUPS = 2
HEADS_PER_GROUP = 4
N_ATT_HEADS = N_KV_GROUPS * HEADS_PER_GROUP
CMP_BLOCK = 32
CMP_STRIDE = 16
SEL_BLOCK = 64
N_SEL = 16
WINDOW = 512
FORCE_BONUS = 1.0e4
ROPE_THETA = 10000.0
N_HGRN_HEADS = 8
HGRN_CHUNK = 64
N_EXPERT_GROUPS = 4
EXPERTS_PER_GROUP = 8
N_EXPERTS = N_EXPERT_GROUPS * EXPERTS_PER_GROUP
EPS = 1e-6

LANES = 128
VMEM_LIMIT = 56 * 1024 * 1024

ATT_W = N_ATT_HEADS * HEAD_DIM
KV_W = N_KV_GROUPS * HEAD_DIM
COL_Q = 0
COL_KC = COL_Q + ATT_W
COL_KS = COL_KC + KV_W
COL_KW = COL_KS + KV_W
COL_VC = COL_KW + KV_W
COL_VS = COL_VC + KV_W
COL_VW = COL_VS + KV_W
COL_HQ = COL_VW + KV_W
COL_HF = COL_HQ + ATT_W
COL_HI = COL_HF + ATT_W
COL_HG = COL_HI + ATT_W
PROJ_W = COL_HG + ATT_W
ROPE_W = COL_VC

MASK_BIG = float(2 ** 30)
LOG2E = 1.4426950408889634
Q_PRESCALE = HEAD_DIM ** -0.5 * LOG2E

_NT = (((1,), (1,)), ((), ()))
_TN = (((0,), (0,)), ((), ()))


def _cparams(sem):
    return pltpu.CompilerParams(dimension_semantics=sem, vmem_limit_bytes=VMEM_LIMIT)


def _dot(a, b, **kw):
    return jnp.dot(a, b, preferred_element_type=F32, **kw)


def _dot_nt(a, b):
    return lax.dot_general(a, b, _NT, preferred_element_type=F32)


def _sigmoid(x):
    return 1.0 / (1.0 + jnp.exp(-x))


def _silu(x):
    return x * _sigmoid(x)


def _rms(x, g):
    return x * lax.rsqrt(jnp.mean(x * x, axis=-1, keepdims=True) + EPS) * g


def _mod_row(mod_ref, b, k):
    d = mod_ref.shape[1] // 6
    return mod_ref[pl.ds(b, 1), k * d:(k + 1) * d]


def _lane_col(x, lane, col):
    return jnp.sum(jnp.where(lane == col, x, 0.0), axis=-1, keepdims=True)


def _ada_kernel(c_ref, w_ref, b_ref, o_ref):
    s = _silu(c_ref[...])
    o_ref[...] = _dot(s, w_ref[...], precision=lax.Precision.HIGHEST) + b_ref[...]


def _ada_mod(c8, w_ada, b_ada, tn=512):
    rows, d = c8.shape
    n = w_ada.shape[1]
    return pl.pallas_call(
        _ada_kernel,
        grid=(n // tn,),
        in_specs=[pl.BlockSpec((rows, d), lambda j: (0, 0)),
                  pl.BlockSpec((d, tn), lambda j: (0, j)),
                  pl.BlockSpec((1, tn), lambda j: (0, j))],
        out_specs=pl.BlockSpec((rows, tn), lambda j: (0, j)),
        out_shape=jax.ShapeDtypeStruct((rows, n), F32),
        compiler_params=_cparams(("parallel",)),
        name="ada_mod",
    )(c8, w_ada, b_ada.reshape(1, n))


def _rope_kernel(ang_ref, cos_ref, sin_ref):
    a = ang_ref[...]
    lane = lax.broadcasted_iota(jnp.int32, a.shape, 1)
    cos_ref[...] = jnp.cos(a)
    sin_ref[...] = jnp.where(lane < HEAD_DIM // 2, -jnp.sin(a), jnp.sin(a))


def _rope_tables(positions, tm):
    t = positions.size
    inv_freq = ROPE_THETA ** (-jnp.arange(0, HEAD_DIM, 2, dtype=F32) / HEAD_DIM)
    ang = positions.reshape(t, 1).astype(F32) * jnp.concatenate([inv_freq, inv_freq])[None, :]
    spec = pl.BlockSpec((tm, HEAD_DIM), lambda i: (i, 0))
    return pl.pallas_call(
        _rope_kernel,
        grid=(t // tm,),
        in_specs=[spec],
        out_specs=[spec, spec],
        out_shape=[jax.ShapeDtypeStruct((t, HEAD_DIM), F32)] * 2,
        compiler_params=_cparams(("parallel",)),
        name="rope_tables",
    )(ang)


def _inproj_kernel(x_ref, mod_ref, g_ref, w_ref, wg_ref, cos_ref, sin_ref,
                   proj_ref, gates_ref, hbk_ref, hbv_ref, h_s, stage, *, tn, per_b):
    j = pl.program_id(1)

    @pl.when(j == 0)
    def _():
        b = pl.program_id(0) // per_b
        h = (_rms(x_ref[...], g_ref[...]) * (1.0 + _mod_row(mod_ref, b, 1))
             + _mod_row(mod_ref, b, 0))
        hb = h.astype(BF16)
        h_s[...] = hb
        gates_ref[...] = _dot(hb, wg_ref[...])

    heads = tn // HEAD_DIM
    rows = x_ref.shape[0]

    def store(jt):
        acc = _dot(h_s[...], w_ref[...])
        for c in range(heads):
            a = acc[:, c * HEAD_DIM:(c + 1) * HEAD_DIM]
            col = None if jt is None else jt * tn + c * HEAD_DIM
            if col is not None and col < ROPE_W:
                a = a * cos_ref[...] + pltpu.roll(a, HEAD_DIM // 2, axis=1) * sin_ref[...]
            if col is not None and col < ATT_W:
                a = a * Q_PRESCALE
            proj_ref[:, c * HEAD_DIM:(c + 1) * HEAD_DIM] = a.astype(BF16)
            for col0, hb_ref in ((COL_KC, hbk_ref), (COL_VC, hbv_ref)):
                if col is not None and col0 <= col < col0 + KV_W:
                    stage[...] = a
                    hb_ref[0, (col - col0) // HEAD_DIM] = jnp.concatenate(
                        [stage[pl.ds(i, rows // CMP_STRIDE, stride=CMP_STRIDE), :]
                         for i in range(CMP_STRIDE)], axis=1).astype(BF16)

    n_special = -(-COL_VS // tn)
    for jt in range(n_special):
        pl.when(j == jt)(functools.partial(store, jt))
    pl.when(j >= n_special)(functools.partial(store, None))


def _inproj(x2, mod, g_pre, w_in_p, w_gates, cos_t, sin_t, seq, tm, tn=512):
    t, d = x2.shape
    per_b = seq // tm
    assert COL_KC % tn + KV_W <= tn and COL_VC % tn + KV_W <= tn and tm % (16 * CMP_STRIDE) == 0
    hb_spec = pl.BlockSpec((1, N_KV_GROUPS, tm // CMP_STRIDE, CMP_STRIDE * HEAD_DIM),
                           lambda i, j: (i // per_b, 0, i % per_b, 0))
    hb_shape = jax.ShapeDtypeStruct(
        (t // seq, N_KV_GROUPS, seq // CMP_STRIDE, CMP_STRIDE * HEAD_DIM), BF16)
    return pl.pallas_call(
        functools.partial(_inproj_kernel, tn=tn, per_b=per_b),
        grid=(t // tm, PROJ_W // tn),
        in_specs=[pl.BlockSpec((tm, d), lambda i, j: (i, 0)),
                  pl.BlockSpec(mod.shape, lambda i, j: (0, 0)),
                  pl.BlockSpec((1, d), lambda i, j: (0, 0)),
                  pl.BlockSpec((d, tn), lambda i, j: (0, j)),
                  pl.BlockSpec((d, LANES), lambda i, j: (0, 0)),
                  pl.BlockSpec((tm, HEAD_DIM), lambda i, j: (i, 0)),
                  pl.BlockSpec((tm, HEAD_DIM), lambda i, j: (i, 0))],
        out_specs=[pl.BlockSpec((tm, tn), lambda i, j: (i, j)),
                   pl.BlockSpec((tm, LANES), lambda i, j: (i, 0)), hb_spec, hb_spec],
        out_shape=[jax.ShapeDtypeStruct((t, PROJ_W), BF16),
                   jax.ShapeDtypeStruct((t, LANES), F32), hb_shape, hb_shape],
        scratch_shapes=[pltpu.VMEM((tm, d), BF16), pltpu.VMEM((tm, HEAD_DIM), F32)],
        compiler_params=_cparams(("parallel", "arbitrary")),
        name="inproj",
    )(x2, mod, g_pre, w_in_p, w_gates, cos_t, sin_t)


def _compress_kernel(h_ref, w1_ref, pe_ref, w2_ref, o_ref):
    hb = h_ref[0]
    half = hb.shape[1]
    ya = _dot(hb, w1_ref[0:half, :])
    yb = _dot(hb, w1_ref[half:2 * half, :])
    const = _dot(pe_ref[...], w1_ref[...])
    n = ya.shape[0]
    yb_next = pltpu.roll(yb, n - 1, axis=0)
    hid = _silu(ya + yb_next + const[0:1, :])
    out = _dot(hid.astype(BF16), w2_ref[...])
    row = lax.broadcasted_iota(jnp.int32, out.shape, 0)
    o_ref[0] = jnp.where(row < n - 1, out, 0.0).astype(BF16)


def _compress(hblk, w1, pe, w2):
    bg, n, half = hblk.shape
    pe8 = jnp.zeros((8, 2 * half), BF16).at[0].set(pe.reshape(-1).astype(BF16))
    return pl.pallas_call(
        _compress_kernel,
        grid=(bg,),
        in_specs=[pl.BlockSpec((1, n, half), lambda i: (i, 0, 0)),
                  pl.BlockSpec((2 * half, HEAD_DIM), lambda i: (0, 0)),
                  pl.BlockSpec((8, 2 * half), lambda i: (0, 0)),
                  pl.BlockSpec((HEAD_DIM, HEAD_DIM), lambda i: (0, 0))],
        out_specs=pl.BlockSpec((1, n, HEAD_DIM), lambda i: (i, 0, 0)),
        out_shape=jax.ShapeDtypeStruct((bg, n, HEAD_DIM), BF16),
        compiler_params=_cparams(("parallel",)),
        name="compress",
    )(hblk, w1.astype(BF16), pe8, w2.astype(BF16))


def _cmp_kernel(q_ref, kc_ref, vc_ref, gates_ref, ov_ref, ocmp_ref, selb_ref, imp_s, cnt_s, *,
                tq, nsel):
    g = pl.program_id(0) % N_KV_GROUPS
    q0 = pl.program_id(1) * tq
    kc = kc_ref[0]
    vc = vc_ref[0]
    nc = kc.shape[0]
    t_idx = q0 + lax.broadcasted_iota(jnp.int32, (tq, nc), 0)
    c_idx = lax.broadcasted_iota(jnp.int32, (tq, nc), 1)
    c_ok = c_idx * CMP_STRIDE + (CMP_BLOCK - 1) <= t_idx
    gl = gates_ref[...]
    lane = lax.broadcasted_iota(jnp.int32, (tq, LANES), 1)
    psum = jnp.zeros((tq, nc), F32)
    for h in range(HEADS_PER_GROUP):
        qh = q_ref[:, h * HEAD_DIM:(h + 1) * HEAD_DIM]
        s = jnp.where(c_ok, _dot_nt(qh, kc), -jnp.inf)
        m = jnp.max(s, axis=-1, keepdims=True)
        m = jnp.where(m == -jnp.inf, 0.0, m)
        p = jnp.exp2(s - m)
        p = p * (1.0 / jnp.maximum(jnp.sum(p, axis=-1, keepdims=True), 1e-30))
        psum = psum + p
        o = _dot(p.astype(BF16), vc)
        glog = _lane_col(gl, lane, (g * HEADS_PER_GROUP + h) * 3)
        ocmp_ref[:, h * HEAD_DIM:(h + 1) * HEAD_DIM] = (o * _sigmoid(glog)).astype(BF16)

    imp = _dot(psum, ov_ref[...], precision=lax.Precision.HIGHEST)
    cur = (q0 + lax.broadcasted_iota(jnp.int32, (tq, LANES), 0)) // SEL_BLOCK
    forced = (lane == 0) | (lane == cur) | (lane == cur - 1)
    imp = jnp.where(lane <= cur, imp + jnp.where(forced, FORCE_BONUS, 0.0), -jnp.inf)
    imp_s[...] = imp.T[0:nsel, :]
    cnt_s[...] = jnp.zeros(cnt_s.shape, F32)
    sub = 8
    n_groups = nsel // sub
    row = lax.broadcasted_iota(jnp.int32, (sub, tq), 0)

    def count_against(kg):
        groups = [imp_s[g * sub:(g + 1) * sub, :] for g in range(n_groups)]
        cnts = [cnt_s[g * sub:(g + 1) * sub, :] for g in range(n_groups)]
        for k in range(kg * sub, (kg + 1) * sub):
            vk = imp_s[k:k + 1, :]
            for g, grp in enumerate(groups):
                if k < g * sub:
                    beats = vk >= grp
                elif k >= (g + 1) * sub:
                    beats = vk > grp
                else:
                    beats = (vk > grp) | ((vk == grp) & (row > k - g * sub))
                cnts[g] = cnts[g] + jnp.where(beats, 1.0, 0.0)
        for g in range(n_groups):
            cnt_s[g * sub:(g + 1) * sub, :] = cnts[g]

    last_block = (q0 + tq - 1) // SEL_BLOCK
    for kg in range(n_groups):
        pl.when(kg * sub <= last_block)(functools.partial(count_against, kg))
    imp_t = imp_s[...]
    sel = (cnt_s[...] < float(N_SEL)) & (imp_t > -jnp.inf)
    bias_t = jnp.where(sel, 0.0, -MASK_BIG)
    if nsel < LANES:
        bias_t = jnp.concatenate([bias_t, jnp.zeros((LANES - nsel, tq), F32)], axis=0)
    selb_ref[...] = bias_t.T.astype(BF16)


def _cmp_attn(proj, kc, vc, gates, batch, seq, tq):
    t = proj.shape[0]
    nq = seq // tq
    nc = kc.shape[1]
    nsel = seq // SEL_BLOCK
    assert nsel <= LANES and nsel % 8 == 0
    c_start = np.arange(nc) * CMP_STRIDE
    j_start = np.arange(LANES) * SEL_BLOCK
    ov = ((c_start[:, None] < j_start[None, :] + SEL_BLOCK)
          & (c_start[:, None] + CMP_BLOCK > j_start[None, :])
          & (np.arange(LANES)[None, :] < nsel)).astype(np.float32)
    gq = HEADS_PER_GROUP * HEAD_DIM
    qmap = lambda bg, i: ((bg // N_KV_GROUPS) * nq + i, bg % N_KV_GROUPS)
    return pl.pallas_call(
        functools.partial(_cmp_kernel, tq=tq, nsel=nsel),
        grid=(batch * N_KV_GROUPS, nq),
        in_specs=[pl.BlockSpec((tq, gq), qmap),
                  pl.BlockSpec((1, nc, HEAD_DIM), lambda bg, i: (bg, 0, 0)),
                  pl.BlockSpec((1, nc, HEAD_DIM), lambda bg, i: (bg, 0, 0)),
                  pl.BlockSpec((tq, LANES), lambda bg, i: ((bg // N_KV_GROUPS) * nq + i, 0)),
                  pl.BlockSpec((nc, LANES), lambda bg, i: (0, 0))],
        out_specs=[pl.BlockSpec((tq, gq), qmap),
                   pl.BlockSpec((tq, LANES), lambda bg, i: (bg * nq + i, 0))],
        out_shape=[jax.ShapeDtypeStruct((t, ATT_W), BF16),
                   jax.ShapeDtypeStruct((batch * N_KV_GROUPS * seq, LANES), BF16)],
        scratch_shapes=[pltpu.VMEM((nsel, tq), F32), pltpu.VMEM((nsel, tq), F32)],
        compiler_params=_cparams(("parallel", "parallel")),
        name="cmp_attn",
    )(proj, kc, vc, gates, jnp.asarray(ov))


def _selwin_kernel(q_ref, selb_ref, ocmp_ref, gates_ref, ks_ref, vs_ref, kw_ref, vw_ref,
                   cb_ref, wb_ref, o_ref, kaug_s, vaug_s, vwaug_s, qaug_s, sbuf_s, m_s, acc_s,
                   swin_s, *, tq, tk, seq, nwin):
    g = pl.program_id(0) % N_KV_GROUPS
    qi = pl.program_id(1)
    q0 = qi * tq
    hq = HEADS_PER_GROUP * tq
    half = hq // 2

    @pl.when(qi == 0)
    def _():
        r = lax.broadcasted_iota(jnp.int32, (seq, HEAD_DIM), 0)
        ln = lax.broadcasted_iota(jnp.int32, (seq, HEAD_DIM), 1)
        ones_col = jnp.where(ln == 0, 1.0, 0.0).astype(BF16)
        kaug_s[:, 0:HEAD_DIM] = ks_ref[...]
        kaug_s[:, HEAD_DIM:2 * HEAD_DIM] = jnp.where(r // SEL_BLOCK == ln, 1.0, 0.0).astype(BF16)
        vaug_s[:, 0:HEAD_DIM] = vs_ref[...]
        vaug_s[:, HEAD_DIM:2 * HEAD_DIM] = ones_col
        vwaug_s[:, 0:HEAD_DIM] = vw_ref[...]
        vwaug_s[:, HEAD_DIM:2 * HEAD_DIM] = ones_col

    for h in range(HEADS_PER_GROUP):
        qaug_s[h * tq:(h + 1) * tq, 0:HEAD_DIM] = q_ref[:, h * HEAD_DIM:(h + 1) * HEAD_DIM]
        qaug_s[h * tq:(h + 1) * tq, HEAD_DIM:2 * HEAD_DIM] = selb_ref[...]
    qa = qaug_s[...]

    def scores(kt, slot):
        ks = kaug_s[pl.ds(pl.multiple_of(kt * tk, tk), tk), :]
        sbuf_s[slot] = _dot_nt(qa, ks)

    def absorb(kt, slot, diagonal=False):
        vs = vaug_s[pl.ds(pl.multiple_of(kt * tk, tk), tk), :]
        s = sbuf_s[slot]
        if diagonal:
            bias = cb_ref[(q0 - kt * tk) // tq]
            s = (s.reshape(HEADS_PER_GROUP, tq, tk) + bias[None]).reshape(hq, tk)
        m = m_s[...]
        m_new = jnp.maximum(m, jnp.max(s, axis=-1, keepdims=True))
        m_s[...] = m_new
        pb = jnp.exp2(s - m_new[:, 0:1]).astype(BF16)
        if diagonal:
            pv = jnp.concatenate([_dot(pb[0:half], vs), _dot(pb[half:hq], vs)], axis=0)
        else:
            pv = _dot(pb, vs)
        acc_s[...] = jnp.exp2(m - m_new)[:, 0:1] * acc_s[...] + pv

    n_full = q0 // tk
    m_s[...] = jnp.full(m_s.shape, -jnp.inf, F32)
    acc_s[...] = jnp.zeros(acc_s.shape, F32)
    scores(0, 0)

    def pair(j, c):
        scores(2 * j + 1, 1)
        absorb(2 * j, 0)
        scores(2 * j + 2, 0)
        absorb(2 * j + 1, 1)
        return c

    lax.fori_loop(0, n_full // 2, pair, 0)
    odd = n_full % 2

    @pl.when(odd == 1)
    def _():
        scores(n_full, 1)
        absorb(n_full - 1, 0)

    nq = seq // tq
    w0 = jnp.clip(qi - (nwin - 1), 0, nq - nwin) * tq
    wlen = nwin * tq
    kwin = kw_ref[pl.ds(pl.multiple_of(w0, tq), wlen), :]
    vwin = vwaug_s[pl.ds(pl.multiple_of(w0, tq), wlen), :]
    swin_s[0:half, :] = _dot_nt(qaug_s[0:half, 0:HEAD_DIM], kwin)
    swin_s[half:hq, :] = _dot_nt(qaug_s[half:hq, 0:HEAD_DIM], kwin)
    absorb(n_full, odd, diagonal=True)
    acc = acc_s[...]
    o_sel = acc[:, 0:HEAD_DIM] / acc[:, HEAD_DIM:HEAD_DIM + 1]

    wbias = wb_ref[(q0 - w0) // tq]
    sw = (swin_s[...].reshape(HEADS_PER_GROUP, tq, wlen) + wbias[None]).reshape(hq, wlen)
    pwb = jnp.exp2(sw - jnp.max(sw, axis=-1, keepdims=True)).astype(BF16)
    ow = jnp.concatenate([_dot(pwb[0:half], vwin), _dot(pwb[half:hq], vwin)], axis=0)
    o_win = ow[:, 0:HEAD_DIM] / ow[:, HEAD_DIM:HEAD_DIM + 1]

    gl = gates_ref[...]
    lane = lax.broadcasted_iota(jnp.int32, (tq, LANES), 1)
    for h in range(HEADS_PER_GROUP):
        gcol = (g * HEADS_PER_GROUP + h) * 3
        g_s = _sigmoid(_lane_col(gl, lane, gcol + 1))
        g_w = _sigmoid(_lane_col(gl, lane, gcol + 2))
        o = (ocmp_ref[:, h * HEAD_DIM:(h + 1) * HEAD_DIM].astype(F32)
             + g_s * o_sel[h * tq:(h + 1) * tq, :] + g_w * o_win[h * tq:(h + 1) * tq, :])
        o_ref[:, h * HEAD_DIM:(h + 1) * HEAD_DIM] = o.astype(BF16)


def _selwin_attn(proj, selb, ocmp, gates, batch, seq, tq, tk):
    t = proj.shape[0]
    nq = seq // tq
    assert tk % tq == 0 and tq & (tq - 1) == 0 and seq % tk == 0
    nwin = min(WINDOW // tq + 1, nq)
    assert WINDOW % tq == 0
    gq = HEADS_PER_GROUP * HEAD_DIM
    hq = HEADS_PER_GROUP * tq
    r = np.arange(tq)[:, None]
    cbias = np.stack([np.where(np.arange(tk)[None, :] <= r + off * tq, 0.0, -np.inf)
                      for off in range(tk // tq)]).astype(np.float32)
    dist = [r + off * tq - np.arange(nwin * tq)[None, :] for off in range(nwin)]
    wbias = np.stack([np.where((d >= 0) & (d < WINDOW), 0.0, -np.inf)
                      for d in dist]).astype(np.float32)
    qmap = lambda bg, i: ((bg // N_KV_GROUPS) * nq + i, bg % N_KV_GROUPS)

    def kvspec(col):
        return pl.BlockSpec((seq, HEAD_DIM),
                            lambda bg, i: (bg // N_KV_GROUPS, col // HEAD_DIM + bg % N_KV_GROUPS))

    return pl.pallas_call(
        functools.partial(_selwin_kernel, tq=tq, tk=tk, seq=seq, nwin=nwin),
        grid=(batch * N_KV_GROUPS, nq),
        in_specs=[pl.BlockSpec((tq, gq), qmap),
                  pl.BlockSpec((tq, LANES), lambda bg, i: (bg * nq + i, 0)),
                  pl.BlockSpec((tq, gq), qmap),
                  pl.BlockSpec((tq, LANES), lambda bg, i: ((bg // N_KV_GROUPS) * nq + i, 0)),
                  kvspec(COL_KS), kvspec(COL_VS), kvspec(COL_KW), kvspec(COL_VW),
                  pl.BlockSpec(cbias.shape, lambda bg, i: (0, 0, 0)),
                  pl.BlockSpec(wbias.shape, lambda bg, i: (0, 0, 0))],
        out_specs=pl.BlockSpec((tq, gq), qmap),
        out_shape=jax.ShapeDtypeStruct((t, ATT_W), BF16),
        scratch_shapes=[pltpu.VMEM((seq, 2 * HEAD_DIM), BF16),
                        pltpu.VMEM((seq, 2 * HEAD_DIM), BF16),
                        pltpu.VMEM((seq, 2 * HEAD_DIM), BF16),
                        pltpu.VMEM((hq, 2 * HEAD_DIM), BF16),
                        pltpu.VMEM((2, hq, tk), F32),
                        pltpu.VMEM((hq, LANES), F32),
                        pltpu.VMEM((hq, 2 * HEAD_DIM), F32),
                        pltpu.VMEM((hq, nwin * tq), F32)],
        compiler_params=_cparams(("parallel", "arbitrary")),
        name="selwin_attn",
    )(proj, selb, ocmp, gates, proj, proj, proj, proj, jnp.asarray(cbias), jnp.asarray(wbias))


_HGRN_LEVELS = (32, 16, 8, 4, 2, 1)


def _hgrn_tables():
    c = HGRN_CHUNK
    nl = len(_HGRN_LEVELS)
    idx = np.arange(c)
    i, j = idx[:, None], idx[None, :]
    mats = [j <= i, j > i]
    lvl = np.full((c, c), nl + 1, np.int32)
    for li, s in enumerate(_HGRN_LEVELS):
        mats.append((j >= (i // s) * s) & (j <= i))
        mats.append((j > i) & (j <= (i // s) * s + s - 1))
        lvl[((i // (2 * s)) == (j // (2 * s))) & (((i // s) % 2) == 1) & (((j // s) % 2) == 0)] = li
    lvl[i == j] = nl
    w = np.concatenate(mats, 0).astype(np.float32)
    return np.concatenate([w, w], axis=1), lvl


def _hgrn_head(hq_ref, hf_ref, hi_ref, hg_ref, lbl_ref, gn_ref, w_ref, lvl_ref, o_ref, st_s,
               hh, ts):
    c = HGRN_CHUNK
    nc = ts // c
    nl = len(_HGRN_LEVELS)
    cs = slice(hh * HEAD_DIM, (hh + 1) * HEAD_DIM)
    lbl = lbl_ref[:, cs]
    e = jnp.exp(lbl - jnp.max(lbl, axis=0, keepdims=True))
    lb = e[0:1, :] / jnp.sum(e, axis=0, keepdims=True)

    f = lb + (1.0 - lb) * _sigmoid(hf_ref[:, cs].astype(F32))
    k = 1.0 - f
    qf = _silu(hq_ref[:, cs].astype(F32))
    v = hi_ref[:, cs]

    lf = jnp.log(f) * LOG2E
    lf_hi = lf.astype(BF16)
    lf_lo = (lf - lf_hi.astype(F32)).astype(BF16)
    side = lambda a: jnp.concatenate([a[ci * c:(ci + 1) * c, :] for ci in range(nc)], axis=1)
    ex = _dot(w_ref[...], jnp.concatenate([side(lf_hi), side(lf_lo)], axis=0))

    def rows(bi):
        return jnp.concatenate(
            [ex[bi * c:(bi + 1) * c, ci * HEAD_DIM:(ci + 1) * HEAD_DIM] for ci in range(nc)], axis=0)

    b = rows(0)
    qhat = (qf * jnp.exp2(b)).astype(BF16)
    khat = (k * jnp.exp2(rows(1))).astype(BF16)
    lvl = lvl_ref[...]
    masks = [lvl == li for li in range(nl + 1)]
    qs = [(qf * jnp.exp2(rows(2 + 2 * li))).astype(BF16) for li in range(nl)] + [qf.astype(BF16)]
    ks = [(k * jnp.exp2(rows(3 + 2 * li))).astype(BF16) for li in range(nl)] + [k.astype(BF16)]
    o_intra = []
    for ci in range(nc):
        sl = slice(ci * c, (ci + 1) * c)
        a = jnp.zeros((c, c), F32)
        for li in range(nl + 1):
            a = jnp.where(masks[li], _dot_nt(qs[li][sl, :], ks[li][sl, :]), a)
        o_intra.append(_dot(a.astype(BF16), v[sl, :]))

    st = st_s[hh]
    outs = []
    for ci in range(nc):
        sl = slice(ci * c, (ci + 1) * c)
        outs.append(o_intra[ci] + _dot_nt(qhat[sl, :], st.astype(BF16)))
        st = st * jnp.exp2(b[(ci + 1) * c - 1:(ci + 1) * c, :]) + lax.dot_general(
            v[sl, :], khat[sl, :], _TN, preferred_element_type=F32)
    st_s[hh] = st
    o = jnp.concatenate(outs, axis=0)
    o = o * lax.rsqrt(jnp.mean(o * o, axis=-1, keepdims=True) + EPS) * gn_ref[:, cs]
    o_ref[:, cs] = (o * _silu(hg_ref[:, cs].astype(F32))).astype(BF16)


def _hgrn_kernel(hq_ref, hf_ref, hi_ref, hg_ref, lbl_ref, gn_ref, w_ref, lvl_ref,
                 o_ref, st_s, *, ts, hb):
    @pl.when(pl.program_id(2) == 0)
    def _():
        st_s[...] = jnp.zeros_like(st_s)

    for hh in range(hb):
        _hgrn_head(hq_ref, hf_ref, hi_ref, hg_ref, lbl_ref, gn_ref, w_ref, lvl_ref, o_ref, st_s,
                   hh, ts)


def _hgrn(proj, lb_logits, g_norm, batch, seq, ts, hb):
    t = proj.shape[0]
    ns = seq // ts
    wmat, lvl = _hgrn_tables()
    w = hb * HEAD_DIM
    assert all(col % w == 0 for col in (COL_HQ, COL_HF, COL_HI, COL_HG))

    def colspec(col):
        return pl.BlockSpec((ts, w), lambda b, h, s: (b * ns + s, col // w + h))

    nl = lb_logits.shape[0]
    return pl.pallas_call(
        functools.partial(_hgrn_kernel, ts=ts, hb=hb),
        grid=(batch, N_HGRN_HEADS // hb, ns),
        in_specs=[colspec(COL_HQ), colspec(COL_HF), colspec(COL_HI), colspec(COL_HG),
                  pl.BlockSpec((nl, w), lambda b, h, s: (0, h)),
                  pl.BlockSpec((1, w), lambda b, h, s: (0, h)),
                  pl.BlockSpec(wmat.shape, lambda b, h, s: (0, 0)),
                  pl.BlockSpec(lvl.shape, lambda b, h, s: (0, 0))],
        out_specs=pl.BlockSpec((ts, w), lambda b, h, s: (b * ns + s, h)),
        out_shape=jax.ShapeDtypeStruct((t, ATT_W), BF16),
        scratch_shapes=[pltpu.VMEM((hb, HEAD_DIM, HEAD_DIM), F32)],
        compiler_params=_cparams(("parallel", "parallel", "arbitrary")),
        name="hgrn2",
    )(proj, proj, proj, proj, lb_logits, g_norm, jnp.asarray(wmat, BF16), jnp.asarray(lvl))


def _outproj_kernel(oa_ref, or_ref, wa_ref, wb_ref, x_ref, mod_ref, gpost_ref, gpre_ref,
                    wr_ref, br_ref, x1_ref, h2_ref, lg_ref, stage, *, per_b):
    b = pl.program_id(0) // per_b
    y = _dot(oa_ref[...], wa_ref[...]) + _dot(or_ref[...], wb_ref[...])
    x1 = x_ref[...] + _mod_row(mod_ref, b, 2) * _rms(y, gpost_ref[...])
    x1_ref[...] = x1
    h = _rms(x1, gpre_ref[...]) * (1.0 + _mod_row(mod_ref, b, 4)) + _mod_row(mod_ref, b, 3)
    h2_ref[...] = _to_slabs(stage, h)
    h_hi = h.astype(BF16)
    h_lo = (h - h_hi.astype(F32)).astype(BF16)
    hh = _dot(h_hi, wr_ref[...])
    lg_ref[...] = (hh[:, 0:LANES] + hh[:, LANES:2 * LANES]
                   + _dot(h_lo, wr_ref[:, 0:LANES]) + br_ref[...])


def _outproj(o_att, o_rec, w_out, x2, mod, g_post, g_pre, wr, br, seq, tm):
    t, d = x2.shape
    per_b = seq // tm
    wr_hi = wr.astype(BF16)
    wr_cat = jnp.concatenate([wr_hi, (wr - wr_hi.astype(F32)).astype(BF16)], axis=1)
    wa = w_out[:ATT_W].astype(BF16)
    wb = w_out[ATT_W:].astype(BF16)
    row = lambda i: (i, 0)
    fixed = lambda i: (0, 0)
    return pl.pallas_call(
        functools.partial(_outproj_kernel, per_b=per_b),
        grid=(t // tm,),
        in_specs=[pl.BlockSpec((tm, ATT_W), row), pl.BlockSpec((tm, ATT_W), row),
                  pl.BlockSpec((ATT_W, d), fixed), pl.BlockSpec((ATT_W, d), fixed),
                  pl.BlockSpec((tm, d), row),
                  pl.BlockSpec(mod.shape, fixed),
                  pl.BlockSpec((1, d), fixed), pl.BlockSpec((1, d), fixed),
                  pl.BlockSpec((d, 2 * LANES), fixed), pl.BlockSpec((1, LANES), fixed)],
        out_specs=[pl.BlockSpec((tm, d), row), pl.BlockSpec((tm * (d // LANES), LANES), row),
                   pl.BlockSpec((tm, LANES), row)],
        out_shape=[jax.ShapeDtypeStruct((t, d), F32),
                   jax.ShapeDtypeStruct((t * (d // LANES), LANES), BF16),
                   jax.ShapeDtypeStruct((t, LANES), F32)],
        scratch_shapes=[pltpu.VMEM((tm * (d // LANES), LANES), F32)],
        compiler_params=_cparams(("parallel",)),
        name="outproj",
    )(o_att, o_rec, wa, wb, x2, mod, g_post, g_pre, wr_cat, br)


G_LANE0 = N_EXPERTS


def _route_kernel(lg_ref, tri_ref, info_ref, cnt_ref, carry_s):
    @pl.when(pl.program_id(0) == 0)
    def _():
        carry_s[...] = jnp.zeros_like(carry_s)

    x = lg_ref[...]
    lane = lax.broadcasted_iota(jnp.int32, x.shape, 1)
    big = jnp.int32(10 ** 6)
    rmax = lambda a: jnp.max(a, axis=-1, keepdims=True)
    rmin = lambda a: jnp.min(a, axis=-1, keepdims=True)
    rsum = lambda a: jnp.sum(a, axis=-1, keepdims=True)

    is_g = (lane >= G_LANE0) & (lane < G_LANE0 + N_EXPERT_GROUPS)
    lgm = jnp.where(is_g, x, -jnp.inf)
    mg = rmax(lgm)
    pg_sel = 1.0 / rsum(jnp.where(is_g, jnp.exp(lgm - mg), 0.0))
    gsel = rmin(jnp.where(lgm == mg, lane, big)) - G_LANE0

    is_e = (lane >= gsel * EXPERTS_PER_GROUP) & (lane < (gsel + 1) * EXPERTS_PER_GROUP)
    lem = jnp.where(is_e, x, -jnp.inf)
    pe = jnp.where(is_e, jnp.exp(lem - rmax(lem)), 0.0)
    pe = pe / rsum(pe)
    pe = jnp.where(is_e, pe, -1.0)
    v1 = rmax(pe)
    i1 = rmin(jnp.where(pe == v1, lane, big))
    pe2 = jnp.where(lane == i1, -1.0, pe)
    v2 = rmax(pe2)
    i2 = rmin(jnp.where(pe2 == v2, lane, big))
    w1 = v1 / (v1 + v2) * pg_sel
    w2 = v2 / (v1 + v2) * pg_sel

    oh1 = jnp.where(lane == i1, 1.0, 0.0)
    oh2 = jnp.where(lane == i2, 1.0, 0.0)
    both = oh1 + oh2
    before = _dot(tri_ref[...], both.astype(BF16)) + carry_s[0:1, :]
    r1 = rsum(oh1 * before)
    r2 = rsum(oh2 * before)
    carry_s[0:1, :] = carry_s[0:1, :] + jnp.sum(both, axis=0, keepdims=True)
    cnt_ref[...] = carry_s[...]

    cols = (i1.astype(F32), i2.astype(F32), w1, w2, r1, r2)
    info = jnp.zeros(x.shape, F32)
    for ci, col in enumerate(cols):
        info = jnp.where(lane == ci, col, info)
    info_ref[...] = info


def _route(logits, tm):
    t = logits.shape[0]
    tri = jnp.asarray(np.tril(np.ones((tm, tm), np.float32), -1), BF16)
    return pl.pallas_call(
        _route_kernel,
        grid=(t // tm,),
        in_specs=[pl.BlockSpec((tm, LANES), lambda i: (i, 0)),
                  pl.BlockSpec((tm, tm), lambda i: (0, 0))],
        out_specs=[pl.BlockSpec((tm, LANES), lambda i: (i, 0)),
                   pl.BlockSpec((8, LANES), lambda i: (0, 0))],
        out_shape=[jax.ShapeDtypeStruct((t, LANES), F32),
                   jax.ShapeDtypeStruct((8, LANES), F32)],
        scratch_shapes=[pltpu.VMEM((8, LANES), F32)],
        compiler_params=_cparams(("arbitrary",)),
        name="route",
    )(logits, tri)


def _to_slabs(stage, x):
    n, d = x.shape
    slab = d // LANES
    for c in range(slab):
        stage[pl.ds(c, n, stride=slab), :] = x[:, c * LANES:(c + 1) * LANES]
    return stage[...].astype(BF16)


def _from_slabs(stage, slabs, n):
    slab = slabs.shape[0] // n
    stage[...] = slabs.astype(F32)
    return jnp.concatenate([stage[pl.ds(c, n, stride=slab), :] for c in range(slab)], axis=1)


def _expert_kernel(te_ref, nu_ref, par_ref, nxt_ref, dst_ref, xs_ref, wg_hbm, wu_hbm, wd_hbm,
                   y_ref, obuf, stage, ssem, wg_f, wu_f, wd_f, wsem, wg_s, wu_s, wd_s, *,
                   tm, standin_row):
    t = pl.program_id(0)
    nu = nu_ref[0]
    slot = t % 2

    def weights(e, buf):
        return [pltpu.make_async_copy(src.at[e], dst.at[buf], wsem.at[buf])
                for src, dst in ((wg_hbm, wg_f), (wu_hbm, wu_f), (wd_hbm, wd_f))]
    slab = wg_s.shape[0] // LANES
    tile = tm * slab

    def token(ref, i):
        return ref.at[pl.ds(pl.multiple_of(i * slab, slab), slab)]

    def scatter(s):
        for r in range(tm):
            pltpu.make_async_copy(token(obuf, s * tm + r), token(y_ref, dst_ref[0, 0, r]),
                                  ssem.at[s]).start()

    def tile_of(buf, s):
        return buf.at[pl.ds(pl.multiple_of(s * tile, tile), tile)]

    def wait_tile(s):
        pltpu.make_async_copy(tile_of(obuf, s), tile_of(obuf, s), ssem.at[s]).wait()

    @pl.when(t < nu)
    def _():
        @pl.when(t == 0)
        def _():
            obuf[tile:2 * tile, :] = jnp.zeros((tile, LANES), obuf.dtype)

            def fill(row0, s):
                return pltpu.make_async_copy(tile_of(obuf, 1),
                                             y_ref.at[pl.ds(row0 * slab, tile)], ssem.at[s])

            fill(standin_row - 2 * tm, 0).start()
            fill(standin_row - tm, 0).start()
            fill(standin_row - 2 * tm, 0).wait()
            fill(standin_row - tm, 0).wait()
            fill(standin_row, 1).start()

        par = par_ref[t]

        @pl.when(t == 0)
        def _():
            for cp in weights(te_ref[0], par):
                cp.start()

        @pl.when((t == 0) | (te_ref[t] != te_ref[jnp.maximum(t - 1, 0)]))
        def _():
            for cp in weights(te_ref[t], par):
                cp.wait()
            wg_s[...] = wg_f[par].astype(BF16)
            wu_s[...] = wu_f[par].astype(BF16)
            wd_s[...] = wd_f[par].astype(BF16)

            @pl.when(nxt_ref[t] != te_ref[t])
            def _():
                for cp in weights(nxt_ref[t], 1 - par):
                    cp.start()

        x = _from_slabs(stage, xs_ref[...], tm).astype(BF16)
        hid = (_silu(_dot(x, wg_s[...])) * _dot(x, wu_s[...])).astype(BF16)
        row0 = pl.multiple_of(slot * tile, tile)
        obuf[pl.ds(row0, tile), :] = _to_slabs(stage, _dot(hid, wd_s[...]))
        scatter(slot)
        wait_tile(1 - slot)

        @pl.when(t == nu - 1)
        def _():
            wait_tile(slot)


def _experts(tile_expert, n_used, parity, next_expert, dst3, xs, w_gate, w_up, w_down, y_rows,
             tm, standin_row):
    n_tiles = dst3.shape[0]
    d, f = w_gate.shape[1:]
    slab = d // LANES
    hbm = pl.BlockSpec(memory_space=pl.ANY)
    return pl.pallas_call(
        functools.partial(_expert_kernel, tm=tm, standin_row=standin_row),
        grid_spec=pltpu.PrefetchScalarGridSpec(
            num_scalar_prefetch=4,
            grid=(n_tiles,),
            in_specs=[pl.BlockSpec((1, 1, tm), lambda t, te, nu, pa, nx: (t, 0, 0),
                                   memory_space=pltpu.SMEM),
                      pl.BlockSpec((tm * slab, LANES),
                                   lambda t, te, nu, pa, nx: (jnp.minimum(t, nu[0] - 1), 0)),
                      hbm, hbm, hbm],
            out_specs=pl.BlockSpec(memory_space=pl.ANY),
            scratch_shapes=[pltpu.VMEM((2 * tm * slab, LANES), BF16),
                            pltpu.VMEM((tm * slab, LANES), F32),
                            pltpu.SemaphoreType.DMA((2,)),
                            pltpu.VMEM((2, d, f), F32), pltpu.VMEM((2, d, f), F32),
                            pltpu.VMEM((2, f, d), F32),
                            pltpu.SemaphoreType.DMA((2,)),
                            pltpu.VMEM((d, f), BF16), pltpu.VMEM((d, f), BF16),
                            pltpu.VMEM((f, d), BF16)]),
        out_shape=jax.ShapeDtypeStruct((y_rows * slab, LANES), BF16),
        compiler_params=_cparams(("arbitrary",)),
        name="moe_experts",
    )(tile_expert, n_used, parity, next_expert, dst3, xs, w_gate, w_up, w_down)


def _dispatch_kernel(pos0_ref, pos1_ref, end_ref, nu_ref, h_ref, dst_in, xs_ref, dst_ref,
                     zbuf, zsem, sem, *, tm, slab, n_tiles, n_tokens):
    tile = tm * slab
    base = pl.program_id(0) * tm

    @pl.when(pl.program_id(0) == 0)
    def _():
        init = pltpu.make_async_copy(dst_in, dst_ref, zsem)
        init.start()
        init.wait()
        zbuf[...] = jnp.zeros(zbuf.shape, zbuf.dtype)

        def zero(row_end, go):
            cp = pltpu.make_async_copy(
                zbuf, xs_ref.at[pl.ds(pl.multiple_of((row_end - tm) * slab, slab), tile)], zsem)
            pl.when(go)(cp.start)
            return cp, go

        pending = [zero(end_ref[e], end_ref[e] > (end_ref[e - 1] if e else 0))
                   for e in range(N_EXPERTS)]
        pending += [zero((i + 1) * tm, i >= nu_ref[0]) for i in range(n_tiles)]
        for cp, go in pending:
            pl.when(go)(cp.wait)

    def token(ref, i):
        return ref.at[pl.ds(pl.multiple_of(i * slab, slab), slab)]

    def start(r, c):
        p0 = pos0_ref[0, 0, r]
        p1 = pos1_ref[0, 0, r]
        pltpu.make_async_copy(token(h_ref, r), token(xs_ref, p0), sem).start()
        pltpu.make_async_copy(token(h_ref, r), token(xs_ref, p1), sem).start()
        dst_ref[p0] = base + r
        dst_ref[p1] = n_tokens + base + r
        return c

    lax.fori_loop(0, tm, start, 0, unroll=8)
    for _ in range(2):
        pltpu.make_async_copy(h_ref, h_ref, sem).wait()


def _dispatch(pos0, pos1, tile_end, n_used, h2, dst0, n_tiles, tm_exp, tm):
    rows = h2.shape[0]
    slab = rows // pos0.shape[0]
    t = pos0.shape[0]
    assert tm == tm_exp
    per_tile = pl.BlockSpec((1, 1, tm), lambda i: (i, 0, 0), memory_space=pltpu.SMEM)
    whole = pl.BlockSpec(memory_space=pltpu.SMEM)
    return pl.pallas_call(
        functools.partial(_dispatch_kernel, tm=tm, slab=slab, n_tiles=n_tiles, n_tokens=t),
        grid=(t // tm,),
        in_specs=[per_tile, per_tile, whole, whole,
                  pl.BlockSpec((tm * slab, LANES), lambda i: (i, 0)),
                  pl.BlockSpec(memory_space=pl.ANY)],
        out_specs=[pl.BlockSpec(memory_space=pl.ANY), whole],
        out_shape=[jax.ShapeDtypeStruct((n_tiles * tm_exp * slab, LANES), h2.dtype),
                   jax.ShapeDtypeStruct(dst0.shape, jnp.int32)],
        scratch_shapes=[pltpu.VMEM((tm * slab, LANES), h2.dtype),
                        pltpu.SemaphoreType.DMA(()), pltpu.SemaphoreType.DMA(())],
        compiler_params=_cparams(("arbitrary",)),
        name="moe_dispatch",
    )(pos0.reshape(t // tm, 1, tm), pos1.reshape(t // tm, 1, tm), tile_end * tm_exp, n_used, h2,
      dst0)


def _final_kernel(info_ref, x1_ref, mod_ref, g_ref, ya_ref, yb_ref, o_ref, stage_a, stage_b, *,
                  per_b):
    info = info_ref[...]
    tm = x1_ref.shape[0]
    ya = _from_slabs(stage_a, ya_ref[...], tm)
    yb = _from_slabs(stage_b, yb_ref[...], tm)
    y = info[:, 2:3] * ya + info[:, 3:4] * yb
    gain = _mod_row(mod_ref, pl.program_id(0) // per_b, 5) * g_ref[...]
    inv = lax.rsqrt(jnp.mean(y * y, axis=-1, keepdims=True) + EPS)
    o_ref[...] = x1_ref[...] + (y * inv) * gain


def _moe_final(info, x1, mod, g_post, y2, seq, tm):
    t, d = x1.shape
    per_b = seq // tm
    nb = t // tm
    return pl.pallas_call(
        functools.partial(_final_kernel, per_b=per_b),
        grid=(nb,),
        in_specs=[pl.BlockSpec((tm, LANES), lambda i: (i, 0)),
                  pl.BlockSpec((tm, d), lambda i: (i, 0)),
                  pl.BlockSpec(mod.shape, lambda i: (0, 0)),
                  pl.BlockSpec((1, d), lambda i: (0, 0)),
                  pl.BlockSpec((tm * (d // LANES), LANES), lambda i: (i, 0)),
                  pl.BlockSpec((tm * (d // LANES), LANES), lambda i: (nb + i, 0))],
        out_specs=pl.BlockSpec((tm, d), lambda i: (i, 0)),
        out_shape=jax.ShapeDtypeStruct((t, d), F32),
        scratch_shapes=[pltpu.VMEM((tm * (d // LANES), LANES), F32)] * 2,
        compiler_params=_cparams(("parallel",)),
        name="moe_final",
    )(info, x1, mod, g_post, y2, y2)


def _moe(h2, logits, x1, mod, g_post, w_gate, w_up, w_down, seq, tm_route, tm_exp, tm_fin):
    t = x1.shape[0]
    info, cnt = _route(logits, tm_route)
    counts = cnt[0, :N_EXPERTS].astype(jnp.int32)
    tiles_e = (counts + tm_exp - 1) // tm_exp
    tile_end = jnp.cumsum(tiles_e)
    offs = (tile_end - tiles_e) * tm_exp
    n_used = tile_end[-1]

    def slot_of(k):
        e = info[:, k].astype(jnp.int32)
        start = jnp.sum(jnp.where(e[:, None] == jnp.arange(N_EXPERTS)[None, :], offs[None, :], 0),
                        axis=1)
        return start + info[:, 4 + k].astype(jnp.int32)

    n_tiles = (2 * t) // tm_exp + N_EXPERTS
    tidx = jnp.minimum(jnp.arange(n_tiles, dtype=jnp.int32), n_used - 1)
    tile_expert = jnp.sum(tidx[:, None] >= tile_end[None, :], axis=1).astype(jnp.int32)
    switches = jnp.concatenate([jnp.zeros((1,), jnp.int32),
                                (tile_expert[1:] != tile_expert[:-1]).astype(jnp.int32)])
    parity = jnp.cumsum(switches) % 2
    next_expert = tile_expert[jnp.minimum(tile_end[tile_expert], n_used - 1)]
    pos0, pos1 = slot_of(0), slot_of(1)
    slot_ids = jnp.arange(n_tiles * tm_exp, dtype=jnp.int32)
    spare = 2 * t + ((slot_ids // tm_exp) % 2) * tm_exp + slot_ids % tm_exp
    n_used1 = n_used.reshape(1).astype(jnp.int32)
    xs, dst = _dispatch(pos0, pos1, tile_end.astype(jnp.int32), n_used1, h2, spare, n_tiles,
                        tm_exp, tm_exp)

    ew = w_gate.shape
    wg = w_gate.reshape(N_EXPERTS, ew[-2], ew[-1])
    wu = w_up.reshape(N_EXPERTS, ew[-2], ew[-1])
    wd = w_down.reshape(N_EXPERTS, ew[-1], ew[-2])
    y2 = _experts(tile_expert, n_used1, parity.astype(jnp.int32), next_expert.astype(jnp.int32),
                  dst.reshape(n_tiles, 1, tm_exp), xs, wg, wu, wd,
                  2 * t + 3 * tm_exp, tm_exp, 2 * t + 2 * tm_exp)
    return _moe_final(info, x1, mod, g_post, y2, seq, tm_fin)


def _permute_w_in(w_in):
    sizes = [ATT_W] + [KV_W] * 6 + [3 * N_ATT_HEADS] + [ATT_W] * 4
    cuts = np.cumsum(sizes)[:-1]
    q, kc, vc, ks, vs, kw, vw, gates, hq, hf, hi, hg = [
        p.astype(BF16) for p in jnp.split(w_in, cuts, axis=1)]
    w_p = jnp.concatenate([q, kc, ks, kw, vc, vs, vw, hq, hf, hi, hg], axis=1)
    w_g = jnp.pad(gates, ((0, 0), (0, LANES - gates.shape[1])))
    return w_p, w_g


def _block(x, c, positions, w_ada, b_ada, g_pre_mix, g_post_mix, g_pre_ffn, g_post_ffn,
           w_in, w_out, pe_k, w1_k, w2_k, pe_v, w1_v, w2_v, lb_logits, g_norm,
           w_group, b_group, w_router, b_router, w_gate, w_up, w_down, cfg):
    batch, seq, d = x.shape
    t = batch * seq
    x2 = x.reshape(t, d)

    c8 = jnp.zeros((8, d), F32).at[:batch].set(c)
    mod = _ada_mod(c8, w_ada, b_ada)
    cos_t, sin_t = _rope_tables(positions, cfg["tm_rope"])
    w_p, w_g = _permute_w_in(w_in)
    proj, gates, hbk, hbv = _inproj(x2, mod, g_pre_mix.reshape(1, d), w_p, w_g, cos_t, sin_t,
                                    seq, cfg["tm_in"])

    half_blocks = (batch * N_KV_GROUPS, seq // CMP_STRIDE, CMP_STRIDE * HEAD_DIM)
    kc = _compress(hbk.reshape(half_blocks), w1_k, pe_k, w2_k)
    vc = _compress(hbv.reshape(half_blocks), w1_v, pe_v, w2_v)
    ocmp, selb = _cmp_attn(proj, kc, vc, gates, batch, seq, cfg["tq_cmp"])
    o_att = _selwin_attn(proj, selb, ocmp, gates, batch, seq, cfg["tq"], cfg["tk"])
    o_rec = _hgrn(proj, lb_logits, g_norm.reshape(1, -1), batch, seq, cfg["ts_hgrn"],
                  cfg["hb_hgrn"])

    wr = jnp.concatenate([w_router, w_group], axis=1)
    wr = jnp.pad(wr, ((0, 0), (0, LANES - wr.shape[1])))
    br = jnp.pad(jnp.concatenate([b_router, b_group]), (0, LANES - N_EXPERTS - N_EXPERT_GROUPS))
    x1, h2, logits = _outproj(o_att, o_rec, w_out, x2, mod, g_post_mix.reshape(1, d),
                              g_pre_ffn.reshape(1, d), wr, br.reshape(1, LANES), seq,
                              cfg["tm_out"])
    out = _moe(h2, logits, x1, mod, g_post_ffn.reshape(1, d), w_gate, w_up, w_down, seq,
               cfg["tm_route"], cfg["tm_exp"], cfg["tm_fin"])
    return out.reshape(batch, seq, d)


def _config(seq):
    return dict(tm_rope=min(1024, seq), tm_in=min(1024, seq), tq_cmp=min(256, seq),
                tq=min(256, seq), tk=min(512, seq), ts_hgrn=min(512, seq), hb_hgrn=4,
                tm_out=min(256, seq), tm_route=min(1024, seq), tm_exp=256,
                tm_fin=min(256, seq))


def kernel(x, c, positions, w_ada, b_ada, g_pre_mix, g_post_mix, g_pre_ffn, g_post_ffn, w_in, w_out, cmp_pe_k, cmp_w1_k, cmp_w2_k, cmp_pe_v, cmp_w1_v, cmp_w2_v, hgrn_lb_logits, hgrn_g_norm, w_group, b_group, w_router, b_router, w_gate, w_up, w_down):
    assert w_ada.shape[0] == 1, "single-layer block"
    return _block(x, c, positions, w_ada[0], b_ada[0], g_pre_mix[0], g_post_mix[0],
                  g_pre_ffn[0], g_post_ffn[0], w_in[0], w_out[0], cmp_pe_k[0], cmp_w1_k[0],
                  cmp_w2_k[0], cmp_pe_v[0], cmp_w1_v[0], cmp_w2_v[0], hgrn_lb_logits,
                  hgrn_g_norm[0], w_group[0], b_group[0], w_router[0], b_router[0],
                  w_gate[0], w_up[0], w_down[0], _config(x.shape[1]))
```

```python
import functools

import numpy as np
import jax
import jax.numpy as jnp
from jax import lax
from jax.experimental import pallas as pl
from jax.experimental.pallas import tpu as pltpu

F32 = jnp.float32
BF16 = jnp.bfloat16

HEAD_DIM = 128
N_KV_GROUPS = 2
HEADS_PER_GROUP = 4
N_ATT_HEADS = N_KV_GROUPS * HEADS_PER_GROUP
CMP_BLOCK = 32
CMP_STRIDE = 16
SEL_BLOCK = 64
N_SEL = 16
WINDOW = 512
FORCE_BONUS = 1.0e4
ROPE_THETA = 10000.0
N_HGRN_HEADS = 8
HGRN_CHUNK = 64
N_EXPERT_GROUPS = 4
EXPERTS_PER_GROUP = 8
N_EXPERTS = N_EXPERT_GROUPS * EXPERTS_PER_GROUP
EPS = 1e-6

LANES = 128
SUBLANES = 8
BF16_TILE_ROWS = 16
VMEM_LIMIT = 56 * 1024 * 1024

ATT_W = N_ATT_HEADS * HEAD_DIM
KV_W = N_KV_GROUPS * HEAD_DIM
COL_Q = 0
COL_KC = COL_Q + ATT_W
COL_KS = COL_KC + KV_W
COL_KW = COL_KS + KV_W
COL_VC = COL_KW + KV_W
COL_VS = COL_VC + KV_W
COL_VW = COL_VS + KV_W
COL_HQ = COL_VW + KV_W
COL_HF = COL_HQ + ATT_W
COL_HI = COL_HF + ATT_W
COL_HG = COL_HI + ATT_W
PROJ_W = COL_HG + ATT_W
ROPE_W = COL_VC

MASK_BIG = float(2 ** 30)
LOG2E = 1.4426950408889634
Q_PRESCALE = HEAD_DIM ** -0.5 * LOG2E

_NT = (((1,), (1,)), ((), ()))
_TN = (((0,), (0,)), ((), ()))


def _cparams(sem):
    return pltpu.CompilerParams(dimension_semantics=sem, vmem_limit_bytes=VMEM_LIMIT)


def _dot(a, b, **kw):
    return jnp.dot(a, b, preferred_element_type=F32, **kw)


def _dot_nt(a, b):
    return lax.dot_general(a, b, _NT, preferred_element_type=F32)


def _sigmoid(x):
    return 1.0 / (1.0 + jnp.exp(-x))


def _silu(x):
    return x * _sigmoid(x)


def _rms(x, g):
    return x * lax.rsqrt(jnp.mean(x * x, axis=-1, keepdims=True) + EPS) * g


def _mod_row(mod_ref, b, k):
    d = mod_ref.shape[1] // 6
    return mod_ref[pl.ds(b, 1), k * d:(k + 1) * d]


def _lane_col(x, lane, col):
    return jnp.sum(jnp.where(lane == col, x, 0.0), axis=-1, keepdims=True)


def _ada_kernel(c_ref, w_ref, b_ref, o_ref):
    s = _silu(c_ref[...])
    s_hi = s.astype(BF16)
    s_lo = (s - s_hi.astype(F32)).astype(BF16)
    w = w_ref[...]
    w_hi = w.astype(BF16)
    w_lo = (w - w_hi.astype(F32)).astype(BF16)
    o_ref[...] = _dot(s_hi, w_hi) + _dot(s_lo, w_hi) + _dot(s_hi, w_lo) + b_ref[...]


def _ada_mod(c8, w_ada, b_ada, tn=512):
    rows, d = c8.shape
    n = w_ada.shape[1]
    return pl.pallas_call(
        _ada_kernel,
        grid=(n // tn,),
        in_specs=[pl.BlockSpec((rows, d), lambda j: (0, 0)),
                  pl.BlockSpec((d, tn), lambda j: (0, j)),
                  pl.BlockSpec((1, tn), lambda j: (0, j))],
        out_specs=pl.BlockSpec((rows, tn), lambda j: (0, j)),
        out_shape=jax.ShapeDtypeStruct((rows, n), F32),
        compiler_params=_cparams(("parallel",)),
        name="ada_mod",
    )(c8, w_ada, b_ada.reshape(1, n))


def _rope_kernel(ang_ref, cos_ref, sin_ref):
    a = ang_ref[...]
    n = a.shape[0]
    half = HEAD_DIM // 2
    lane = lax.broadcasted_iota(jnp.int32, a.shape, 1)
    first = lane < half
    c = jnp.cos(a)
    s = jnp.sin(a)
    c_sw = pltpu.roll(c, half, axis=1)
    s_sw = pltpu.roll(s, half, axis=1)
    cos_ref[pl.ds(0, n, stride=2), :] = jnp.where(first, c, c_sw)
    cos_ref[pl.ds(1, n, stride=2), :] = jnp.where(first, c_sw, c)
    sin_ref[pl.ds(0, n, stride=2), :] = jnp.where(first, -s, s_sw)
    sin_ref[pl.ds(1, n, stride=2), :] = jnp.where(first, -s_sw, s)


def _rope_tables(positions, tm):
    t = positions.size
    inv_freq = ROPE_THETA ** (-jnp.arange(0, HEAD_DIM, 2, dtype=F32) / HEAD_DIM)
    ang = (positions.reshape(t // 2, 2, 1).astype(F32) * inv_freq[None, None, :]).reshape(
        t // 2, HEAD_DIM)
    spec = pl.BlockSpec((tm, HEAD_DIM), lambda i: (i, 0))
    return pl.pallas_call(
        _rope_kernel,
        grid=(t // tm,),
        in_specs=[pl.BlockSpec((tm // 2, HEAD_DIM), lambda i: (i, 0))],
        out_specs=[spec, spec],
        out_shape=[jax.ShapeDtypeStruct((t, HEAD_DIM), F32)] * 2,
        compiler_params=_cparams(("parallel",)),
        name="rope_tables",
    )(ang)


def _inproj_kernel(x_ref, mod_ref, g_ref, w_ref, wg_ref, cos_ref, sin_ref,
                   proj_ref, gates_ref, hbk_ref, hbv_ref, h_s, stage, *, tn, per_b):
    j = pl.program_id(1)

    @pl.when(j == 0)
    def _():
        b = pl.program_id(0) // per_b
        h = (_rms(x_ref[...], g_ref[...]) * (1.0 + _mod_row(mod_ref, b, 1))
             + _mod_row(mod_ref, b, 0))
        hb = h.astype(BF16)
        h_s[...] = hb
        gates_ref[...] = _dot(hb, wg_ref[...])

    heads = tn // HEAD_DIM
    rows = x_ref.shape[0]

    def store(jt):
        acc = _dot(h_s[...], w_ref[...])
        for c in range(heads):
            a = acc[:, c * HEAD_DIM:(c + 1) * HEAD_DIM]
            col = None if jt is None else jt * tn + c * HEAD_DIM
            if col is not None and col < ROPE_W:
                a = a * cos_ref[...] + pltpu.roll(a, HEAD_DIM // 2, axis=1) * sin_ref[...]
            if col is not None and col < ATT_W:
                a = a * Q_PRESCALE
            proj_ref[:, c * HEAD_DIM:(c + 1) * HEAD_DIM] = a.astype(BF16)
            for col0, hb_ref in ((COL_KC, hbk_ref), (COL_VC, hbv_ref)):
                if col is not None and col0 <= col < col0 + KV_W:
                    stage[...] = a
                    hb_ref[0, (col - col0) // HEAD_DIM] = jnp.concatenate(
                        [stage[pl.ds(i, rows // CMP_STRIDE, stride=CMP_STRIDE), :]
                         for i in range(CMP_STRIDE)], axis=1).astype(BF16)

    n_special = -(-COL_VS // tn)
    for jt in range(n_special):
        pl.when(j == jt)(functools.partial(store, jt))
    pl.when(j >= n_special)(functools.partial(store, None))


def _inproj(x2, mod, g_pre, w_in_p, w_gates, cos_t, sin_t, seq, tm, tn=512):
    t, d = x2.shape
    per_b = seq // tm
    assert COL_KC % tn + KV_W <= tn and COL_VC % tn + KV_W <= tn
    assert tm % (BF16_TILE_ROWS * CMP_STRIDE) == 0
    hb_spec = pl.BlockSpec((1, N_KV_GROUPS, tm // CMP_STRIDE, CMP_STRIDE * HEAD_DIM),
                           lambda i, j: (i // per_b, 0, i % per_b, 0))
    hb_shape = jax.ShapeDtypeStruct(
        (t // seq, N_KV_GROUPS, seq // CMP_STRIDE, CMP_STRIDE * HEAD_DIM), BF16)
    return pl.pallas_call(
        functools.partial(_inproj_kernel, tn=tn, per_b=per_b),
        grid=(t // tm, PROJ_W // tn),
        in_specs=[pl.BlockSpec((tm, d), lambda i, j: (i, 0)),
                  pl.BlockSpec(mod.shape, lambda i, j: (0, 0)),
                  pl.BlockSpec((1, d), lambda i, j: (0, 0)),
                  pl.BlockSpec((d, tn), lambda i, j: (0, j)),
                  pl.BlockSpec((d, LANES), lambda i, j: (0, 0)),
                  pl.BlockSpec((tm, HEAD_DIM), lambda i, j: (i, 0)),
                  pl.BlockSpec((tm, HEAD_DIM), lambda i, j: (i, 0))],
        out_specs=[pl.BlockSpec((tm, tn), lambda i, j: (i, j)),
                   pl.BlockSpec((tm, LANES), lambda i, j: (i, 0)), hb_spec, hb_spec],
        out_shape=[jax.ShapeDtypeStruct((t, PROJ_W), BF16),
                   jax.ShapeDtypeStruct((t, LANES), F32), hb_shape, hb_shape],
        scratch_shapes=[pltpu.VMEM((tm, d), BF16), pltpu.VMEM((tm, HEAD_DIM), F32)],
        compiler_params=_cparams(("parallel", "arbitrary")),
        name="inproj",
    )(x2, mod, g_pre, w_in_p, w_gates, cos_t, sin_t)


def _compress_kernel(h_ref, w1_ref, pe_ref, w2_ref, o_ref):
    hb = h_ref[0]
    half = hb.shape[1]
    ya = _dot(hb, w1_ref[0:half, :])
    yb = _dot(hb, w1_ref[half:2 * half, :])
    const = _dot(pe_ref[...], w1_ref[...])
    n = ya.shape[0]
    yb_next = pltpu.roll(yb, n - 1, axis=0)
    hid = _silu(ya + yb_next + const[0:1, :])
    out = _dot(hid.astype(BF16), w2_ref[...])
    row = lax.broadcasted_iota(jnp.int32, out.shape, 0)
    o_ref[0] = jnp.where(row < n - 1, out, 0.0).astype(BF16)


def _compress(hblk, w1, pe, w2):
    bg, n, half = hblk.shape
    pe8 = jnp.zeros((8, 2 * half), BF16).at[0].set(pe.reshape(-1).astype(BF16))
    return pl.pallas_call(
        _compress_kernel,
        grid=(bg,),
        in_specs=[pl.BlockSpec((1, n, half), lambda i: (i, 0, 0)),
                  pl.BlockSpec((2 * half, HEAD_DIM), lambda i: (0, 0)),
                  pl.BlockSpec((8, 2 * half), lambda i: (0, 0)),
                  pl.BlockSpec((HEAD_DIM, HEAD_DIM), lambda i: (0, 0))],
        out_specs=pl.BlockSpec((1, n, HEAD_DIM), lambda i: (i, 0, 0)),
        out_shape=jax.ShapeDtypeStruct((bg, n, HEAD_DIM), BF16),
        compiler_params=_cparams(("parallel",)),
        name="compress",
    )(hblk, w1.astype(BF16), pe8, w2.astype(BF16))


def _cmp_kernel(q_ref, kc_ref, vc_ref, gates_ref, ov_ref, ocmp_ref, selb_ref, imp_s, cnt_s, *,
                tq, nsel):
    g = pl.program_id(0) % N_KV_GROUPS
    q0 = pl.program_id(1) * tq
    kc = kc_ref[0]
    vc = vc_ref[0]
    nc = kc.shape[0]
    t_idx = q0 + lax.broadcasted_iota(jnp.int32, (tq, nc), 0)
    c_idx = lax.broadcasted_iota(jnp.int32, (tq, nc), 1)
    c_ok = c_idx * CMP_STRIDE + (CMP_BLOCK - 1) <= t_idx
    gl = gates_ref[...]
    lane = lax.broadcasted_iota(jnp.int32, (tq, LANES), 1)
    psum = jnp.zeros((tq, nc), F32)
    for h in range(HEADS_PER_GROUP):
        qh = q_ref[:, h * HEAD_DIM:(h + 1) * HEAD_DIM]
        s = jnp.where(c_ok, _dot_nt(qh, kc), -jnp.inf)
        m = jnp.max(s, axis=-1, keepdims=True)
        m = jnp.where(m == -jnp.inf, 0.0, m)
        p = jnp.exp2(s - m)
        p = p * (1.0 / jnp.maximum(jnp.sum(p, axis=-1, keepdims=True), 1e-30))
        psum = psum + p
        o = _dot(p.astype(BF16), vc)
        glog = _lane_col(gl, lane, (g * HEADS_PER_GROUP + h) * 3)
        ocmp_ref[:, h * HEAD_DIM:(h + 1) * HEAD_DIM] = (o * _sigmoid(glog)).astype(BF16)

    imp = _dot(psum, ov_ref[...], precision=lax.Precision.HIGHEST)
    cur = (q0 + lax.broadcasted_iota(jnp.int32, (tq, LANES), 0)) // SEL_BLOCK
    forced = (lane == 0) | (lane == cur) | (lane == cur - 1)
    imp = jnp.where(lane <= cur, imp + jnp.where(forced, FORCE_BONUS, 0.0), -jnp.inf)
    imp_s[...] = imp.T[0:nsel, :]
    cnt_s[...] = jnp.zeros(cnt_s.shape, F32)
    sub = SUBLANES
    n_groups = nsel // sub
    row = lax.broadcasted_iota(jnp.int32, (sub, tq), 0)

    def count_against(kg):
        groups = [imp_s[g * sub:(g + 1) * sub, :] for g in range(n_groups)]
        cnts = [cnt_s[g * sub:(g + 1) * sub, :] for g in range(n_groups)]
        for k in range(kg * sub, (kg + 1) * sub):
            vk = imp_s[k:k + 1, :]
            for g, grp in enumerate(groups):
                if k < g * sub:
                    beats = vk >= grp
                elif k >= (g + 1) * sub:
                    beats = vk > grp
                else:
                    beats = (vk > grp) | ((vk == grp) & (row > k - g * sub))
                cnts[g] = cnts[g] + jnp.where(beats, 1.0, 0.0)
        for g in range(n_groups):
            cnt_s[g * sub:(g + 1) * sub, :] = cnts[g]

    last_block = (q0 + tq - 1) // SEL_BLOCK
    for kg in range(n_groups):
        pl.when(kg * sub <= last_block)(functools.partial(count_against, kg))
    imp_t = imp_s[...]
    sel = (cnt_s[...] < float(N_SEL)) & (imp_t > -jnp.inf)
    bias_t = jnp.where(sel, 0.0, -MASK_BIG)
    if nsel < LANES:
        bias_t = jnp.concatenate([bias_t, jnp.zeros((LANES - nsel, tq), F32)], axis=0)
    selb_ref[...] = bias_t.T.astype(BF16)


def _cmp_attn(proj, kc, vc, gates, batch, seq, tq):
    t = proj.shape[0]
    nq = seq // tq
    nc = kc.shape[1]
    nsel = seq // SEL_BLOCK
    assert nsel <= LANES and nsel % 8 == 0
    c_start = np.arange(nc) * CMP_STRIDE
    j_start = np.arange(LANES) * SEL_BLOCK
    ov = ((c_start[:, None] < j_start[None, :] + SEL_BLOCK)
          & (c_start[:, None] + CMP_BLOCK > j_start[None, :])
          & (np.arange(LANES)[None, :] < nsel)).astype(np.float32)
    gq = HEADS_PER_GROUP * HEAD_DIM
    qmap = lambda bg, i: ((bg // N_KV_GROUPS) * nq + i, bg % N_KV_GROUPS)
    return pl.pallas_call(
        functools.partial(_cmp_kernel, tq=tq, nsel=nsel),
        grid=(batch * N_KV_GROUPS, nq),
        in_specs=[pl.BlockSpec((tq, gq), qmap),
                  pl.BlockSpec((1, nc, HEAD_DIM), lambda bg, i: (bg, 0, 0)),
                  pl.BlockSpec((1, nc, HEAD_DIM), lambda bg, i: (bg, 0, 0)),
                  pl.BlockSpec((tq, LANES), lambda bg, i: ((bg // N_KV_GROUPS) * nq + i, 0)),
                  pl.BlockSpec((nc, LANES), lambda bg, i: (0, 0))],
        out_specs=[pl.BlockSpec((tq, gq), qmap),
                   pl.BlockSpec((tq, LANES), lambda bg, i: (bg * nq + i, 0))],
        out_shape=[jax.ShapeDtypeStruct((t, ATT_W), BF16),
                   jax.ShapeDtypeStruct((batch * N_KV_GROUPS * seq, LANES), BF16)],
        scratch_shapes=[pltpu.VMEM((nsel, tq), F32), pltpu.VMEM((nsel, tq), F32)],
        compiler_params=_cparams(("parallel", "parallel")),
        name="cmp_attn",
    )(proj, kc, vc, gates, jnp.asarray(ov))


def _selwin_kernel(q_ref, selb_ref, ocmp_ref, gates_ref, ks_ref, vs_ref, kw_ref, vw_ref,
                   cb_ref, wb_ref, o_ref, kaug_s, vaug_s, vwaug_s, qaug_s, sbuf_s, m_s, acc_s,
                   swin_s, *, tq, tk, seq, nwin):
    g = pl.program_id(0) % N_KV_GROUPS
    qi = pl.program_id(1)
    q0 = qi * tq
    hq = HEADS_PER_GROUP * tq
    half = hq // 2

    @pl.when(qi == 0)
    def _():
        r = lax.broadcasted_iota(jnp.int32, (seq, HEAD_DIM), 0)
        ln = lax.broadcasted_iota(jnp.int32, (seq, HEAD_DIM), 1)
        ones_col = jnp.where(ln == 0, 1.0, 0.0).astype(BF16)
        kaug_s[:, 0:HEAD_DIM] = ks_ref[...]
        kaug_s[:, HEAD_DIM:2 * HEAD_DIM] = jnp.where(r // SEL_BLOCK == ln, 1.0, 0.0).astype(BF16)
        vaug_s[:, 0:HEAD_DIM] = vs_ref[...]
        vaug_s[:, HEAD_DIM:2 * HEAD_DIM] = ones_col
        vwaug_s[:, 0:HEAD_DIM] = vw_ref[...]
        vwaug_s[:, HEAD_DIM:2 * HEAD_DIM] = ones_col

    for h in range(HEADS_PER_GROUP):
        qaug_s[h * tq:(h + 1) * tq, 0:HEAD_DIM] = q_ref[:, h * HEAD_DIM:(h + 1) * HEAD_DIM]
        qaug_s[h * tq:(h + 1) * tq, HEAD_DIM:2 * HEAD_DIM] = selb_ref[...]
    qa = qaug_s[...]

    def scores(kt, slot):
        ks = kaug_s[pl.ds(pl.multiple_of(kt * tk, tk), tk), :]
        sbuf_s[slot] = _dot_nt(qa, ks)

    def absorb(kt, slot, diagonal=False):
        vs = vaug_s[pl.ds(pl.multiple_of(kt * tk, tk), tk), :]
        s = sbuf_s[slot]
        if diagonal:
            bias = cb_ref[(q0 - kt * tk) // tq]
            s = (s.reshape(HEADS_PER_GROUP, tq, tk) + bias[None]).reshape(hq, tk)
        m = m_s[...]
        m_new = jnp.maximum(m, jnp.max(s, axis=-1, keepdims=True))
        m_s[...] = m_new
        pb = jnp.exp2(s - m_new[:, 0:1]).astype(BF16)
        if diagonal:
            pv = jnp.concatenate([_dot(pb[0:half], vs), _dot(pb[half:hq], vs)], axis=0)
        else:
            pv = _dot(pb, vs)
        acc_s[...] = jnp.exp2(m - m_new)[:, 0:1] * acc_s[...] + pv

    n_full = q0 // tk
    m_s[...] = jnp.full(m_s.shape, -jnp.inf, F32)
    acc_s[...] = jnp.zeros(acc_s.shape, F32)
    scores(0, 0)

    def pair(j, c):
        scores(2 * j + 1, 1)
        absorb(2 * j, 0)
        scores(2 * j + 2, 0)
        absorb(2 * j + 1, 1)
        return c

    lax.fori_loop(0, n_full // 2, pair, 0)
    odd = n_full % 2

    @pl.when(odd == 1)
    def _():
        scores(n_full, 1)
        absorb(n_full - 1, 0)

    nq = seq // tq
    w0 = jnp.clip(qi - (nwin - 1), 0, nq - nwin) * tq
    wlen = nwin * tq
    kwin = kw_ref[pl.ds(pl.multiple_of(w0, tq), wlen), :]
    vwin = vwaug_s[pl.ds(pl.multiple_of(w0, tq), wlen), :]
    swin_s[0:half, :] = _dot_nt(qaug_s[0:half, 0:HEAD_DIM], kwin)
    swin_s[half:hq, :] = _dot_nt(qaug_s[half:hq, 0:HEAD_DIM], kwin)
    absorb(n_full, odd, diagonal=True)
    acc = acc_s[...]
    o_sel = acc[:, 0:HEAD_DIM] / acc[:, HEAD_DIM:HEAD_DIM + 1]

    wbias = wb_ref[(q0 - w0) // tq]
    sw = (swin_s[...].reshape(HEADS_PER_GROUP, tq, wlen) + wbias[None]).reshape(hq, wlen)
    pwb = jnp.exp2(sw - jnp.max(sw, axis=-1, keepdims=True)).astype(BF16)
    ow = jnp.concatenate([_dot(pwb[0:half], vwin), _dot(pwb[half:hq], vwin)], axis=0)
    o_win = ow[:, 0:HEAD_DIM] / ow[:, HEAD_DIM:HEAD_DIM + 1]

    gl = gates_ref[...]
    lane = lax.broadcasted_iota(jnp.int32, (tq, LANES), 1)
    for h in range(HEADS_PER_GROUP):
        gcol = (g * HEADS_PER_GROUP + h) * 3
        g_s = _sigmoid(_lane_col(gl, lane, gcol + 1))
        g_w = _sigmoid(_lane_col(gl, lane, gcol + 2))
        o = (ocmp_ref[:, h * HEAD_DIM:(h + 1) * HEAD_DIM].astype(F32)
             + g_s * o_sel[h * tq:(h + 1) * tq, :] + g_w * o_win[h * tq:(h + 1) * tq, :])
        o_ref[:, h * HEAD_DIM:(h + 1) * HEAD_DIM] = o.astype(BF16)


def _selwin_attn(proj, selb, ocmp, gates, batch, seq, tq, tk):
    t = proj.shape[0]
    nq = seq // tq
    assert tk % tq == 0 and tq & (tq - 1) == 0 and seq % tk == 0
    nwin = min(WINDOW // tq + 1, nq)
    assert WINDOW % tq == 0
    gq = HEADS_PER_GROUP * HEAD_DIM
    hq = HEADS_PER_GROUP * tq
    r = np.arange(tq)[:, None]
    cbias = np.stack([np.where(np.arange(tk)[None, :] <= r + off * tq, 0.0, -np.inf)
                      for off in range(tk // tq)]).astype(np.float32)
    dist = [r + off * tq - np.arange(nwin * tq)[None, :] for off in range(nwin)]
    wbias = np.stack([np.where((d >= 0) & (d < WINDOW), 0.0, -np.inf)
                      for d in dist]).astype(np.float32)
    qmap = lambda bg, i: ((bg // N_KV_GROUPS) * nq + i, bg % N_KV_GROUPS)

    def kvspec(col):
        return pl.BlockSpec((seq, HEAD_DIM),
                            lambda bg, i: (bg // N_KV_GROUPS, col // HEAD_DIM + bg % N_KV_GROUPS))

    return pl.pallas_call(
        functools.partial(_selwin_kernel, tq=tq, tk=tk, seq=seq, nwin=nwin),
        grid=(batch * N_KV_GROUPS, nq),
        in_specs=[pl.BlockSpec((tq, gq), qmap),
                  pl.BlockSpec((tq, LANES), lambda bg, i: (bg * nq + i, 0)),
                  pl.BlockSpec((tq, gq), qmap),
                  pl.BlockSpec((tq, LANES), lambda bg, i: ((bg // N_KV_GROUPS) * nq + i, 0)),
                  kvspec(COL_KS), kvspec(COL_VS), kvspec(COL_KW), kvspec(COL_VW),
                  pl.BlockSpec(cbias.shape, lambda bg, i: (0, 0, 0)),
                  pl.BlockSpec(wbias.shape, lambda bg, i: (0, 0, 0))],
        out_specs=pl.BlockSpec((tq, gq), qmap),
        out_shape=jax.ShapeDtypeStruct((t, ATT_W), BF16),
        scratch_shapes=[pltpu.VMEM((seq, 2 * HEAD_DIM), BF16),
                        pltpu.VMEM((seq, 2 * HEAD_DIM), BF16),
                        pltpu.VMEM((seq, 2 * HEAD_DIM), BF16),
                        pltpu.VMEM((hq, 2 * HEAD_DIM), BF16),
                        pltpu.VMEM((2, hq, tk), F32),
                        pltpu.VMEM((hq, LANES), F32),
                        pltpu.VMEM((hq, 2 * HEAD_DIM), F32),
                        pltpu.VMEM((hq, nwin * tq), F32)],
        compiler_params=_cparams(("parallel", "arbitrary")),
        name="selwin_attn",
    )(proj, selb, ocmp, gates, proj, proj, proj, proj, jnp.asarray(cbias), jnp.asarray(wbias))


_HGRN_LEVELS = (32, 16, 8, 4, 2, 1)


def _hgrn_tables():
    c = HGRN_CHUNK
    nl = len(_HGRN_LEVELS)
    idx = np.arange(c)
    i, j = idx[:, None], idx[None, :]
    mats = [j <= i, j > i]
    lvl = np.full((c, c), nl + 1, np.int32)
    for li, s in enumerate(_HGRN_LEVELS):
        mats.append((j >= (i // s) * s) & (j <= i))
        mats.append((j > i) & (j <= (i // s) * s + s - 1))
        lvl[((i // (2 * s)) == (j // (2 * s))) & (((i // s) % 2) == 1) & (((j // s) % 2) == 0)] = li
    lvl[i == j] = nl
    w = np.concatenate(mats, 0).astype(np.float32)
    return np.concatenate([w, w], axis=1), lvl


def _hgrn_head(hq_ref, hf_ref, hi_ref, hg_ref, lbl_ref, gn_ref, w_ref, lvl_ref, o_ref, st_s,
               hh, ts):
    c = HGRN_CHUNK
    nc = ts // c
    nl = len(_HGRN_LEVELS)
    cs = slice(hh * HEAD_DIM, (hh + 1) * HEAD_DIM)
    lbl = lbl_ref[:, cs]
    e = jnp.exp(lbl - jnp.max(lbl, axis=0, keepdims=True))
    lb = e[0:1, :] / jnp.sum(e, axis=0, keepdims=True)

    f = lb + (1.0 - lb) * _sigmoid(hf_ref[:, cs].astype(F32))
    k = 1.0 - f
    qf = _silu(hq_ref[:, cs].astype(F32))
    v = hi_ref[:, cs]

    lf = jnp.log(f) * LOG2E
    lf_hi = lf.astype(BF16)
    lf_lo = (lf - lf_hi.astype(F32)).astype(BF16)
    side = lambda a: jnp.concatenate([a[ci * c:(ci + 1) * c, :] for ci in range(nc)], axis=1)
    ex = _dot(w_ref[...], jnp.concatenate([side(lf_hi), side(lf_lo)], axis=0))

    def rows(bi):
        return jnp.concatenate(
            [ex[bi * c:(bi + 1) * c, ci * HEAD_DIM:(ci + 1) * HEAD_DIM] for ci in range(nc)], axis=0)

    b = rows(0)
    qhat = (qf * jnp.exp2(b)).astype(BF16)
    khat = (k * jnp.exp2(rows(1))).astype(BF16)
    lvl = lvl_ref[...]
    masks = [lvl == li for li in range(nl + 1)]
    qs = [(qf * jnp.exp2(rows(2 + 2 * li))).astype(BF16) for li in range(nl)] + [qf.astype(BF16)]
    ks = [(k * jnp.exp2(rows(3 + 2 * li))).astype(BF16) for li in range(nl)] + [k.astype(BF16)]
    o_intra = []
    for ci in range(nc):
        sl = slice(ci * c, (ci + 1) * c)
        a = jnp.zeros((c, c), F32)
        for li in range(nl + 1):
            a = jnp.where(masks[li], _dot_nt(qs[li][sl, :], ks[li][sl, :]), a)
        o_intra.append(_dot(a.astype(BF16), v[sl, :]))

    st = st_s[hh]
    outs = []
    for ci in range(nc):
        sl = slice(ci * c, (ci + 1) * c)
        outs.append(o_intra[ci] + _dot_nt(qhat[sl, :], st.astype(BF16)))
        st = st * jnp.exp2(b[(ci + 1) * c - 1:(ci + 1) * c, :]) + lax.dot_general(
            v[sl, :], khat[sl, :], _TN, preferred_element_type=F32)
    st_s[hh] = st
    o = jnp.concatenate(outs, axis=0)
    o = o * lax.rsqrt(jnp.mean(o * o, axis=-1, keepdims=True) + EPS) * gn_ref[:, cs]
    o_ref[:, cs] = (o * _silu(hg_ref[:, cs].astype(F32))).astype(BF16)


def _hgrn_kernel(hq_ref, hf_ref, hi_ref, hg_ref, lbl_ref, gn_ref, w_ref, lvl_ref,
                 o_ref, st_s, *, ts, hb):
    @pl.when(pl.program_id(2) == 0)
    def _():
        st_s[...] = jnp.zeros_like(st_s)

    for hh in range(hb):
        _hgrn_head(hq_ref, hf_ref, hi_ref, hg_ref, lbl_ref, gn_ref, w_ref, lvl_ref, o_ref, st_s,
                   hh, ts)


def _hgrn(proj, lb_logits, g_norm, batch, seq, ts, hb):
    t = proj.shape[0]
    ns = seq // ts
    wmat, lvl = _hgrn_tables()
    w = hb * HEAD_DIM
    assert all(col % w == 0 for col in (COL_HQ, COL_HF, COL_HI, COL_HG))

    def colspec(col):
        return pl.BlockSpec((ts, w), lambda b, h, s: (b * ns + s, col // w + h))

    nl = lb_logits.shape[0]
    return pl.pallas_call(
        functools.partial(_hgrn_kernel, ts=ts, hb=hb),
        grid=(batch, N_HGRN_HEADS // hb, ns),
        in_specs=[colspec(COL_HQ), colspec(COL_HF), colspec(COL_HI), colspec(COL_HG),
                  pl.BlockSpec((nl, w), lambda b, h, s: (0, h)),
                  pl.BlockSpec((1, w), lambda b, h, s: (0, h)),
                  pl.BlockSpec(wmat.shape, lambda b, h, s: (0, 0)),
                  pl.BlockSpec(lvl.shape, lambda b, h, s: (0, 0))],
        out_specs=pl.BlockSpec((ts, w), lambda b, h, s: (b * ns + s, h)),
        out_shape=jax.ShapeDtypeStruct((t, ATT_W), BF16),
        scratch_shapes=[pltpu.VMEM((hb, HEAD_DIM, HEAD_DIM), F32)],
        compiler_params=_cparams(("parallel", "parallel", "arbitrary")),
        name="hgrn2",
    )(proj, proj, proj, proj, lb_logits, g_norm, jnp.asarray(wmat, BF16), jnp.asarray(lvl))


def _outproj_kernel(oa_ref, or_ref, wa_ref, wb_ref, x_ref, mod_ref, gpost_ref, gpre_ref,
                    wr_ref, br_ref, x1_ref, h2_ref, lg_ref, stage, *, per_b):
    b = pl.program_id(0) // per_b
    y = _dot(oa_ref[...], wa_ref[...]) + _dot(or_ref[...], wb_ref[...])
    x1 = x_ref[...] + _mod_row(mod_ref, b, 2) * _rms(y, gpost_ref[...])
    x1_ref[...] = x1
    h = _rms(x1, gpre_ref[...]) * (1.0 + _mod_row(mod_ref, b, 4)) + _mod_row(mod_ref, b, 3)
    h2_ref[...] = _to_slabs(stage, h)
    h_hi = h.astype(BF16)
    h_lo = (h - h_hi.astype(F32)).astype(BF16)
    hh = _dot(h_hi, wr_ref[...])
    lg_ref[...] = (hh[:, 0:LANES] + hh[:, LANES:2 * LANES]
                   + _dot(h_lo, wr_ref[:, 0:LANES]) + br_ref[...])


def _outproj(o_att, o_rec, w_out, x2, mod, g_post, g_pre, wr, br, seq, tm):
    t, d = x2.shape
    per_b = seq // tm
    wr_hi = wr.astype(BF16)
    wr_cat = jnp.concatenate([wr_hi, (wr - wr_hi.astype(F32)).astype(BF16)], axis=1)
    wa = w_out[:ATT_W].astype(BF16)
    wb = w_out[ATT_W:].astype(BF16)
    row = lambda i: (i, 0)
    fixed = lambda i: (0, 0)
    return pl.pallas_call(
        functools.partial(_outproj_kernel, per_b=per_b),
        grid=(t // tm,),
        in_specs=[pl.BlockSpec((tm, ATT_W), row), pl.BlockSpec((tm, ATT_W), row),
                  pl.BlockSpec((ATT_W, d), fixed), pl.BlockSpec((ATT_W, d), fixed),
                  pl.BlockSpec((tm, d), row),
                  pl.BlockSpec(mod.shape, fixed),
                  pl.BlockSpec((1, d), fixed), pl.BlockSpec((1, d), fixed),
                  pl.BlockSpec((d, 2 * LANES), fixed), pl.BlockSpec((1, LANES), fixed)],
        out_specs=[pl.BlockSpec((tm, d), row), pl.BlockSpec((tm * (d // LANES), LANES), row),
                   pl.BlockSpec((tm, LANES), row)],
        out_shape=[jax.ShapeDtypeStruct((t, d), F32),
                   jax.ShapeDtypeStruct((t * (d // LANES), LANES), BF16),
                   jax.ShapeDtypeStruct((t, LANES), F32)],
        scratch_shapes=[pltpu.VMEM((tm * (d // LANES), LANES), F32)],
        compiler_params=_cparams(("parallel",)),
        name="outproj",
    )(o_att, o_rec, wa, wb, x2, mod, g_post, g_pre, wr_cat, br)


G_LANE0 = N_EXPERTS


def _route_kernel(lg_ref, tri_ref, info_ref, cnt_ref, carry_s):
    @pl.when(pl.program_id(0) == 0)
    def _():
        carry_s[...] = jnp.zeros_like(carry_s)

    x = lg_ref[...]
    lane = lax.broadcasted_iota(jnp.int32, x.shape, 1)
    big = jnp.int32(10 ** 6)
    rmax = lambda a: jnp.max(a, axis=-1, keepdims=True)
    rmin = lambda a: jnp.min(a, axis=-1, keepdims=True)
    rsum = lambda a: jnp.sum(a, axis=-1, keepdims=True)

    is_g = (lane >= G_LANE0) & (lane < G_LANE0 + N_EXPERT_GROUPS)
    lgm = jnp.where(is_g, x, -jnp.inf)
    mg = rmax(lgm)
    pg_sel = 1.0 / rsum(jnp.where(is_g, jnp.exp(lgm - mg), 0.0))
    gsel = rmin(jnp.where(lgm == mg, lane, big)) - G_LANE0

    is_e = (lane >= gsel * EXPERTS_PER_GROUP) & (lane < (gsel + 1) * EXPERTS_PER_GROUP)
    lem = jnp.where(is_e, x, -jnp.inf)
    pe = jnp.where(is_e, jnp.exp(lem - rmax(lem)), 0.0)
    pe = pe / rsum(pe)
    pe = jnp.where(is_e, pe, -1.0)
    v1 = rmax(pe)
    i1 = rmin(jnp.where(pe == v1, lane, big))
    pe2 = jnp.where(lane == i1, -1.0, pe)
    v2 = rmax(pe2)
    i2 = rmin(jnp.where(pe2 == v2, lane, big))
    w1 = v1 / (v1 + v2) * pg_sel
    w2 = v2 / (v1 + v2) * pg_sel

    oh1 = jnp.where(lane == i1, 1.0, 0.0)
    oh2 = jnp.where(lane == i2, 1.0, 0.0)
    both = oh1 + oh2
    before = _dot(tri_ref[...], both.astype(BF16)) + carry_s[0:1, :]
    r1 = rsum(oh1 * before)
    r2 = rsum(oh2 * before)
    carry_s[0:1, :] = carry_s[0:1, :] + jnp.sum(both, axis=0, keepdims=True)
    cnt_ref[...] = carry_s[...]

    cols = (i1.astype(F32), i2.astype(F32), w1, w2, r1, r2)
    info = jnp.zeros(x.shape, F32)
    for ci, col in enumerate(cols):
        info = jnp.where(lane == ci, col, info)
    info_ref[...] = info


def _route(logits, tm):
    t = logits.shape[0]
    tri = jnp.asarray(np.tril(np.ones((tm, tm), np.float32), -1), BF16)
    return pl.pallas_call(
        _route_kernel,
        grid=(t // tm,),
        in_specs=[pl.BlockSpec((tm, LANES), lambda i: (i, 0)),
                  pl.BlockSpec((tm, tm), lambda i: (0, 0))],
        out_specs=[pl.BlockSpec((tm, LANES), lambda i: (i, 0)),
                   pl.BlockSpec((8, LANES), lambda i: (0, 0))],
        out_shape=[jax.ShapeDtypeStruct((t, LANES), F32),
                   jax.ShapeDtypeStruct((8, LANES), F32)],
        scratch_shapes=[pltpu.VMEM((8, LANES), F32)],
        compiler_params=_cparams(("arbitrary",)),
        name="route",
    )(logits, tri)


def _to_slabs(stage, x):
    n, d = x.shape
    slab = d // LANES
    for c in range(slab):
        stage[pl.ds(c, n, stride=slab), :] = x[:, c * LANES:(c + 1) * LANES]
    return stage[...].astype(BF16)


def _from_slabs(stage, slabs, n):
    slab = slabs.shape[0] // n
    stage[...] = slabs.astype(F32)
    return jnp.concatenate([stage[pl.ds(c, n, stride=slab), :] for c in range(slab)], axis=1)


def _expert_kernel(te_ref, nu_ref, par_ref, nxt_ref, dst_ref, xs_ref, wg_hbm, wu_hbm, wd_hbm,
                   y_ref, obuf, stage, ssem, wg_f, wu_f, wd_f, wsem, wg_s, wu_s, wd_s, *,
                   tm, standin_row):
    t = pl.program_id(0)
    nu = nu_ref[0]
    slot = t % 2

    def weights(e, buf):
        return [pltpu.make_async_copy(src.at[e], dst.at[buf], wsem.at[buf])
                for src, dst in ((wg_hbm, wg_f), (wu_hbm, wu_f), (wd_hbm, wd_f))]
    slab = wg_s.shape[0] // LANES
    tile = tm * slab

    def token(ref, i):
        return ref.at[pl.ds(pl.multiple_of(i * slab, slab), slab)]

    def scatter(s):
        for r in range(tm):
            pltpu.make_async_copy(token(obuf, s * tm + r), token(y_ref, dst_ref[0, 0, r]),
                                  ssem.at[s]).start()

    def tile_of(buf, s):
        return buf.at[pl.ds(pl.multiple_of(s * tile, tile), tile)]

    def wait_tile(s):
        pltpu.make_async_copy(tile_of(obuf, s), tile_of(obuf, s), ssem.at[s]).wait()

    @pl.when(t < nu)
    def _():
        @pl.when(t == 0)
        def _():
            obuf[tile:2 * tile, :] = jnp.zeros((tile, LANES), obuf.dtype)

            def fill(row0, s):
                return pltpu.make_async_copy(tile_of(obuf, 1),
                                             y_ref.at[pl.ds(row0 * slab, tile)], ssem.at[s])

            fill(standin_row - 2 * tm, 0).start()
            fill(standin_row - tm, 0).start()
            fill(standin_row - 2 * tm, 0).wait()
            fill(standin_row - tm, 0).wait()
            fill(standin_row, 1).start()

        par = par_ref[t]

        @pl.when(t == 0)
        def _():
            for cp in weights(te_ref[0], par):
                cp.start()

        @pl.when((t == 0) | (te_ref[t] != te_ref[jnp.maximum(t - 1, 0)]))
        def _():
            for cp in weights(te_ref[t], par):
                cp.wait()
            wg_s[...] = wg_f[par].astype(BF16)
            wu_s[...] = wu_f[par].astype(BF16)
            wd_s[...] = wd_f[par].astype(BF16)

            @pl.when(nxt_ref[t] != te_ref[t])
            def _():
                for cp in weights(nxt_ref[t], 1 - par):
                    cp.start()

        x = _from_slabs(stage, xs_ref[...], tm).astype(BF16)
        hid = (_silu(_dot(x, wg_s[...])) * _dot(x, wu_s[...])).astype(BF16)
        row0 = pl.multiple_of(slot * tile, tile)
        obuf[pl.ds(row0, tile), :] = _to_slabs(stage, _dot(hid, wd_s[...]))
        scatter(slot)
        wait_tile(1 - slot)

        @pl.when(t == nu - 1)
        def _():
            wait_tile(slot)


def _experts(tile_expert, n_used, parity, next_expert, dst3, xs, w_gate, w_up, w_down, y_rows,
             tm, standin_row):
    n_tiles = dst3.shape[0]
    d, f = w_gate.shape[1:]
    slab = d // LANES
    hbm = pl.BlockSpec(memory_space=pl.ANY)
    return pl.pallas_call(
        functools.partial(_expert_kernel, tm=tm, standin_row=standin_row),
        grid_spec=pltpu.PrefetchScalarGridSpec(
            num_scalar_prefetch=4,
            grid=(n_tiles,),
            in_specs=[pl.BlockSpec((1, 1, tm), lambda t, te, nu, pa, nx: (t, 0, 0),
                                   memory_space=pltpu.SMEM),
                      pl.BlockSpec((tm * slab, LANES),
                                   lambda t, te, nu, pa, nx: (jnp.minimum(t, nu[0] - 1), 0)),
                      hbm, hbm, hbm],
            out_specs=pl.BlockSpec(memory_space=pl.ANY),
            scratch_shapes=[pltpu.VMEM((2 * tm * slab, LANES), BF16),
                            pltpu.VMEM((tm * slab, LANES), F32),
                            pltpu.SemaphoreType.DMA((2,)),
                            pltpu.VMEM((2, d, f), F32), pltpu.VMEM((2, d, f), F32),
                            pltpu.VMEM((2, f, d), F32),
                            pltpu.SemaphoreType.DMA((2,)),
                            pltpu.VMEM((d, f), BF16), pltpu.VMEM((d, f), BF16),
                            pltpu.VMEM((f, d), BF16)]),
        out_shape=jax.ShapeDtypeStruct((y_rows * slab, LANES), BF16),
        compiler_params=_cparams(("arbitrary",)),
        name="moe_experts",
    )(tile_expert, n_used, parity, next_expert, dst3, xs, w_gate, w_up, w_down)


def _dispatch_kernel(pos0_ref, pos1_ref, end_ref, nu_ref, h_ref, dst_in, xs_ref, dst_ref,
                     zbuf, zsem, sem, *, tm, slab, n_tiles, n_tokens):
    tile = tm * slab
    base = pl.program_id(0) * tm

    @pl.when(pl.program_id(0) == 0)
    def _():
        init = pltpu.make_async_copy(dst_in, dst_ref, zsem)
        init.start()
        init.wait()
        zbuf[...] = jnp.zeros(zbuf.shape, zbuf.dtype)

        def zero(row_end, go):
            cp = pltpu.make_async_copy(
                zbuf, xs_ref.at[pl.ds(pl.multiple_of((row_end - tm) * slab, slab), tile)], zsem)
            pl.when(go)(cp.start)
            return cp, go

        pending = [zero(end_ref[e], end_ref[e] > (end_ref[e - 1] if e else 0))
                   for e in range(N_EXPERTS)]
        pending += [zero((i + 1) * tm, i >= nu_ref[0]) for i in range(n_tiles)]
        for cp, go in pending:
            pl.when(go)(cp.wait)

    def token(ref, i):
        return ref.at[pl.ds(pl.multiple_of(i * slab, slab), slab)]

    def start(r, c):
        p0 = pos0_ref[0, 0, r]
        p1 = pos1_ref[0, 0, r]
        pltpu.make_async_copy(token(h_ref, r), token(xs_ref, p0), sem).start()
        pltpu.make_async_copy(token(h_ref, r), token(xs_ref, p1), sem).start()
        dst_ref[p0] = base + r
        dst_ref[p1] = n_tokens + base + r
        return c

    lax.fori_loop(0, tm, start, 0, unroll=8)
    for _ in range(2):
        pltpu.make_async_copy(h_ref, h_ref, sem).wait()


def _dispatch(pos0, pos1, tile_end, n_used, h2, dst0, n_tiles, tm):
    rows = h2.shape[0]
    slab = rows // pos0.shape[0]
    t = pos0.shape[0]
    per_tile = pl.BlockSpec((1, 1, tm), lambda i: (i, 0, 0), memory_space=pltpu.SMEM)
    whole = pl.BlockSpec(memory_space=pltpu.SMEM)
    return pl.pallas_call(
        functools.partial(_dispatch_kernel, tm=tm, slab=slab, n_tiles=n_tiles, n_tokens=t),
        grid=(t // tm,),
        in_specs=[per_tile, per_tile, whole, whole,
                  pl.BlockSpec((tm * slab, LANES), lambda i: (i, 0)),
                  pl.BlockSpec(memory_space=pl.ANY)],
        out_specs=[pl.BlockSpec(memory_space=pl.ANY), whole],
        out_shape=[jax.ShapeDtypeStruct((n_tiles * tm * slab, LANES), h2.dtype),
                   jax.ShapeDtypeStruct(dst0.shape, jnp.int32)],
        scratch_shapes=[pltpu.VMEM((tm * slab, LANES), h2.dtype),
                        pltpu.SemaphoreType.DMA(()), pltpu.SemaphoreType.DMA(())],
        compiler_params=_cparams(("arbitrary",)),
        name="moe_dispatch",
    )(pos0.reshape(t // tm, 1, tm), pos1.reshape(t // tm, 1, tm), tile_end * tm, n_used, h2,
      dst0)


def _final_kernel(info_ref, x1_ref, mod_ref, g_ref, ya_ref, yb_ref, o_ref, stage_a, stage_b, *,
                  per_b):
    info = info_ref[...]
    tm = x1_ref.shape[0]
    ya = _from_slabs(stage_a, ya_ref[...], tm)
    yb = _from_slabs(stage_b, yb_ref[...], tm)
    y = info[:, 2:3] * ya + info[:, 3:4] * yb
    gain = _mod_row(mod_ref, pl.program_id(0) // per_b, 5) * g_ref[...]
    inv = lax.rsqrt(jnp.mean(y * y, axis=-1, keepdims=True) + EPS)
    o_ref[...] = x1_ref[...] + (y * inv) * gain


def _moe_final(info, x1, mod, g_post, y2, seq, tm):
    t, d = x1.shape
    per_b = seq // tm
    nb = t // tm
    return pl.pallas_call(
        functools.partial(_final_kernel, per_b=per_b),
        grid=(nb,),
        in_specs=[pl.BlockSpec((tm, LANES), lambda i: (i, 0)),
                  pl.BlockSpec((tm, d), lambda i: (i, 0)),
                  pl.BlockSpec(mod.shape, lambda i: (0, 0)),
                  pl.BlockSpec((1, d), lambda i: (0, 0)),
                  pl.BlockSpec((tm * (d // LANES), LANES), lambda i: (i, 0)),
                  pl.BlockSpec((tm * (d // LANES), LANES), lambda i: (nb + i, 0))],
        out_specs=pl.BlockSpec((tm, d), lambda i: (i, 0)),
        out_shape=jax.ShapeDtypeStruct((t, d), F32),
        scratch_shapes=[pltpu.VMEM((tm * (d // LANES), LANES), F32)] * 2,
        compiler_params=_cparams(("parallel",)),
        name="moe_final",
    )(info, x1, mod, g_post, y2, y2)


def _moe(h2, logits, x1, mod, g_post, w_gate, w_up, w_down, seq, tm_route, tm_exp, tm_fin):
    t = x1.shape[0]
    info, cnt = _route(logits, tm_route)
    counts = cnt[0, :N_EXPERTS].astype(jnp.int32)
    tiles_e = (counts + tm_exp - 1) // tm_exp
    tile_end = jnp.cumsum(tiles_e)
    offs = (tile_end - tiles_e) * tm_exp
    n_used = tile_end[-1]

    def slot_of(k):
        e = info[:, k].astype(jnp.int32)
        start = jnp.sum(jnp.where(e[:, None] == jnp.arange(N_EXPERTS)[None, :], offs[None, :], 0),
                        axis=1)
        return start + info[:, 4 + k].astype(jnp.int32)

    n_tiles = (2 * t) // tm_exp + N_EXPERTS
    tidx = jnp.minimum(jnp.arange(n_tiles, dtype=jnp.int32), n_used - 1)
    tile_expert = jnp.sum(tidx[:, None] >= tile_end[None, :], axis=1).astype(jnp.int32)
    switches = jnp.concatenate([jnp.zeros((1,), jnp.int32),
                                (tile_expert[1:] != tile_expert[:-1]).astype(jnp.int32)])
    parity = jnp.cumsum(switches) % 2
    next_expert = tile_expert[jnp.minimum(tile_end[tile_expert], n_used - 1)]
    pos0, pos1 = slot_of(0), slot_of(1)
    slot_ids = jnp.arange(n_tiles * tm_exp, dtype=jnp.int32)
    spare = 2 * t + ((slot_ids // tm_exp) % 2) * tm_exp + slot_ids % tm_exp
    n_used1 = n_used.reshape(1).astype(jnp.int32)
    xs, dst = _dispatch(pos0, pos1, tile_end.astype(jnp.int32), n_used1, h2, spare, n_tiles,
                        tm_exp)

    ew = w_gate.shape
    wg = w_gate.reshape(N_EXPERTS, ew[-2], ew[-1])
    wu = w_up.reshape(N_EXPERTS, ew[-2], ew[-1])
    wd = w_down.reshape(N_EXPERTS, ew[-1], ew[-2])
    y2 = _experts(tile_expert, n_used1, parity.astype(jnp.int32), next_expert.astype(jnp.int32),
                  dst.reshape(n_tiles, 1, tm_exp), xs, wg, wu, wd,
                  2 * t + 3 * tm_exp, tm_exp, 2 * t + 2 * tm_exp)
    return _moe_final(info, x1, mod, g_post, y2, seq, tm_fin)


def _permute_w_in(w_in):
    sizes = [ATT_W] + [KV_W] * 6 + [3 * N_ATT_HEADS] + [ATT_W] * 4
    cuts = np.cumsum(sizes)[:-1]
    q, kc, vc, ks, vs, kw, vw, gates, hq, hf, hi, hg = [
        p.astype(BF16) for p in jnp.split(w_in, cuts, axis=1)]
    w_p = jnp.concatenate([q, kc, ks, kw, vc, vs, vw, hq, hf, hi, hg], axis=1)
    w_g = jnp.pad(gates, ((0, 0), (0, LANES - gates.shape[1])))
    return w_p, w_g


def _block(x, c, positions, w_ada, b_ada, g_pre_mix, g_post_mix, g_pre_ffn, g_post_ffn,
           w_in, w_out, pe_k, w1_k, w2_k, pe_v, w1_v, w2_v, lb_logits, g_norm,
           w_group, b_group, w_router, b_router, w_gate, w_up, w_down, cfg):
    batch, seq, d = x.shape
    t = batch * seq
    x2 = x.reshape(t, d)

    c8 = jnp.zeros((8, d), F32).at[:batch].set(c)
    mod = _ada_mod(c8, w_ada, b_ada)
    cos_t, sin_t = _rope_tables(positions, cfg["tm_rope"])
    w_p, w_g = _permute_w_in(w_in)
    proj, gates, hbk, hbv = _inproj(x2, mod, g_pre_mix.reshape(1, d), w_p, w_g, cos_t, sin_t,
                                    seq, cfg["tm_in"])

    half_blocks = (batch * N_KV_GROUPS, seq // CMP_STRIDE, CMP_STRIDE * HEAD_DIM)
    kc = _compress(hbk.reshape(half_blocks), w1_k, pe_k, w2_k)
    vc = _compress(hbv.reshape(half_blocks), w1_v, pe_v, w2_v)
    ocmp, selb = _cmp_attn(proj, kc, vc, gates, batch, seq, cfg["tq_cmp"])
    o_att = _selwin_attn(proj, selb, ocmp, gates, batch, seq, cfg["tq"], cfg["tk"])
    o_rec = _hgrn(proj, lb_logits, g_norm.reshape(1, -1), batch, seq, cfg["ts_hgrn"],
                  cfg["hb_hgrn"])

    wr = jnp.concatenate([w_router, w_group], axis=1)
    wr = jnp.pad(wr, ((0, 0), (0, LANES - wr.shape[1])))
    br = jnp.pad(jnp.concatenate([b_router, b_group]), (0, LANES - N_EXPERTS - N_EXPERT_GROUPS))
    x1, h2, logits = _outproj(o_att, o_rec, w_out, x2, mod, g_post_mix.reshape(1, d),
                              g_pre_ffn.reshape(1, d), wr, br.reshape(1, LANES), seq,
                              cfg["tm_out"])
    out = _moe(h2, logits, x1, mod, g_post_ffn.reshape(1, d), w_gate, w_up, w_down, seq,
               cfg["tm_route"], cfg["tm_exp"], cfg["tm_fin"])
    return out.reshape(batch, seq, d)


def _config(seq):
    return dict(tm_rope=min(1024, seq), tm_in=min(1024, seq), tq_cmp=min(256, seq),
                tq=min(256, seq), tk=min(512, seq), ts_hgrn=min(512, seq), hb_hgrn=4,
                tm_out=min(256, seq), tm_route=min(1024, seq), tm_exp=256,
                tm_fin=min(256, seq))


def kernel(x, c, positions, w_ada, b_ada, g_pre_mix, g_post_mix, g_pre_ffn, g_post_ffn, w_in, w_out, cmp_pe_k, cmp_w1_k, cmp_w2_k, cmp_pe_v, cmp_w1_v, cmp_w2_v, hgrn_lb_logits, hgrn_g_norm, w_group, b_group, w_router, b_router, w_gate, w_up, w_down):
    assert w_ada.shape[0] == 1, "single-layer block"
    return _block(x, c, positions, w_ada[0], b_ada[0], g_pre_mix[0], g_post_mix[0],
                  g_pre_ffn[0], g_post_ffn[0], w_in[0], w_out[0], cmp_pe_k[0], cmp_w1_k[0],
                  cmp_w2_k[0], cmp_pe_v[0], cmp_w1_v[0], cmp_w2_v[0], hgrn_lb_logits,
                  hgrn_g_norm[0], w_group[0], b_group[0], w_router[0], b_router[0],
                  w_gate[0], w_up[0], w_down[0], _config(x.shape[1]))
```

```python
import functools

import numpy as np
import jax
import jax.numpy as jnp
from jax import lax
from jax.experimental import pallas as pl
from jax.experimental.pallas import tpu as pltpu

F32 = jnp.float32
BF16 = jnp.bfloat16

HEAD_DIM = 128
N_KV_GROUPS = 2
HEADS_PER_GROUP = 4
N_ATT_HEADS = N_KV_GROUPS * HEADS_PER_GROUP
CMP_BLOCK = 32
CMP_STRIDE = 16
SEL_BLOCK = 64
N_SEL = 16
WINDOW = 512
FORCE_BONUS = 1.0e4
ROPE_THETA = 10000.0
N_HGRN_HEADS = 8
HGRN_CHUNK = 64
N_EXPERT_GROUPS = 4
EXPERTS_PER_GROUP = 8
N_EXPERTS = N_EXPERT_GROUPS * EXPERTS_PER_GROUP
EPS = 1e-6

LANES = 128
SUBLANES = 8
BF16_TILE_ROWS = 16
VMEM_LIMIT = 56 * 1024 * 1024

ATT_W = N_ATT_HEADS * HEAD_DIM
KV_W = N_KV_GROUPS * HEAD_DIM
COL_Q = 0
COL_KC = COL_Q + ATT_W
COL_KS = COL_KC + KV_W
COL_KW = COL_KS + KV_W
COL_VC = COL_KW + KV_W
COL_VS = COL_VC + KV_W
COL_VW = COL_VS + KV_W
COL_HQ = COL_VW + KV_W
COL_HF = COL_HQ + ATT_W
COL_HI = COL_HF + ATT_W
COL_HG = COL_HI + ATT_W
PROJ_W = COL_HG + ATT_W
ROPE_W = COL_VC

MASK_BIG = float(2 ** 30)
LOG2E = 1.4426950408889634
Q_PRESCALE = HEAD_DIM ** -0.5 * LOG2E

_NT = (((1,), (1,)), ((), ()))
_TN = (((0,), (0,)), ((), ()))


def _cparams(sem):
    return pltpu.CompilerParams(dimension_semantics=sem, vmem_limit_bytes=VMEM_LIMIT)


def _dot(a, b, **kw):
    return jnp.dot(a, b, preferred_element_type=F32, **kw)


def _dot_nt(a, b):
    return lax.dot_general(a, b, _NT, preferred_element_type=F32)


def _sigmoid(x):
    return 1.0 / (1.0 + jnp.exp(-x))


def _silu(x):
    return x * _sigmoid(x)


def _rms(x, g):
    return x * lax.rsqrt(jnp.mean(x * x, axis=-1, keepdims=True) + EPS) * g


def _mod_row(mod_ref, b, k):
    d = mod_ref.shape[1] // 6
    return mod_ref[pl.ds(b, 1), k * d:(k + 1) * d]


def _lane_col(x, lane, col):
    return jnp.sum(jnp.where(lane == col, x, 0.0), axis=-1, keepdims=True)


def _ada_kernel(c_ref, w_ref, b_ref, o_ref):
    s = _silu(c_ref[...])
    s_hi = s.astype(BF16)
    s_lo = (s - s_hi.astype(F32)).astype(BF16)
    w = w_ref[...]
    w_hi = w.astype(BF16)
    w_lo = (w - w_hi.astype(F32)).astype(BF16)
    o_ref[...] = _dot(s_hi, w_hi) + _dot(s_lo, w_hi) + _dot(s_hi, w_lo) + b_ref[...]


def _ada_mod(c8, w_ada, b_ada, tn=512):
    rows, d = c8.shape
    n = w_ada.shape[1]
    return pl.pallas_call(
        _ada_kernel,
        grid=(n // tn,),
        in_specs=[pl.BlockSpec((rows, d), lambda j: (0, 0)),
                  pl.BlockSpec((d, tn), lambda j: (0, j)),
                  pl.BlockSpec((1, tn), lambda j: (0, j))],
        out_specs=pl.BlockSpec((rows, tn), lambda j: (0, j)),
        out_shape=jax.ShapeDtypeStruct((rows, n), F32),
        compiler_params=_cparams(("parallel",)),
        name="ada_mod",
    )(c8, w_ada, b_ada.reshape(1, n))


def _rope_kernel(ang_ref, cos_ref, sin_ref):
    a = ang_ref[...]
    n = a.shape[0]
    half = HEAD_DIM // 2
    lane = lax.broadcasted_iota(jnp.int32, a.shape, 1)
    first = lane < half
    c = jnp.cos(a)
    s = jnp.sin(a)
    c_sw = pltpu.roll(c, half, axis=1)
    s_sw = pltpu.roll(s, half, axis=1)
    cos_ref[pl.ds(0, n, stride=2), :] = jnp.where(first, c, c_sw)
    cos_ref[pl.ds(1, n, stride=2), :] = jnp.where(first, c_sw, c)
    sin_ref[pl.ds(0, n, stride=2), :] = jnp.where(first, -s, s_sw)
    sin_ref[pl.ds(1, n, stride=2), :] = jnp.where(first, -s_sw, s)


def _rope_tables(positions, tm):
    t = positions.size
    inv_freq = ROPE_THETA ** (-jnp.arange(0, HEAD_DIM, 2, dtype=F32) / HEAD_DIM)
    ang = (positions.reshape(t // 2, 2, 1).astype(F32) * inv_freq[None, None, :]).reshape(
        t // 2, HEAD_DIM)
    spec = pl.BlockSpec((tm, HEAD_DIM), lambda i: (i, 0))
    return pl.pallas_call(
        _rope_kernel,
        grid=(t // tm,),
        in_specs=[pl.BlockSpec((tm // 2, HEAD_DIM), lambda i: (i, 0))],
        out_specs=[spec, spec],
        out_shape=[jax.ShapeDtypeStruct((t, HEAD_DIM), F32)] * 2,
        compiler_params=_cparams(("parallel",)),
        name="rope_tables",
    )(ang)


def _inproj_kernel(x_ref, mod_ref, g_ref, w_ref, wg_ref, cos_ref, sin_ref,
                   proj_ref, gates_ref, hbk_ref, hbv_ref, h_s, stage, *, tn, per_b):
    j = pl.program_id(1)

    @pl.when(j == 0)
    def _():
        b = pl.program_id(0) // per_b
        h = (_rms(x_ref[...], g_ref[...]) * (1.0 + _mod_row(mod_ref, b, 1))
             + _mod_row(mod_ref, b, 0))
        hb = h.astype(BF16)
        h_s[...] = hb
        gates_ref[...] = _dot(hb, wg_ref[...])

    heads = tn // HEAD_DIM
    rows = x_ref.shape[0]

    def store(jt):
        acc = _dot(h_s[...], w_ref[...])
        for c in range(heads):
            a = acc[:, c * HEAD_DIM:(c + 1) * HEAD_DIM]
            col = None if jt is None else jt * tn + c * HEAD_DIM
            if col is not None and col < ROPE_W:
                a = a * cos_ref[...] + pltpu.roll(a, HEAD_DIM // 2, axis=1) * sin_ref[...]
            if col is not None and col < ATT_W:
                a = a * Q_PRESCALE
            proj_ref[:, c * HEAD_DIM:(c + 1) * HEAD_DIM] = a.astype(BF16)
            for col0, hb_ref in ((COL_KC, hbk_ref), (COL_VC, hbv_ref)):
                if col is not None and col0 <= col < col0 + KV_W:
                    stage[...] = a
                    hb_ref[0, (col - col0) // HEAD_DIM] = jnp.concatenate(
                        [stage[pl.ds(i, rows // CMP_STRIDE, stride=CMP_STRIDE), :]
                         for i in range(CMP_STRIDE)], axis=1).astype(BF16)

    n_special = -(-COL_VS // tn)
    for jt in range(n_special):
        pl.when(j == jt)(functools.partial(store, jt))
    pl.when(j >= n_special)(functools.partial(store, None))


def _inproj(x2, mod, g_pre, w_in_p, w_gates, cos_t, sin_t, seq, tm, tn=512):
    t, d = x2.shape
    per_b = seq // tm
    assert COL_KC % tn + KV_W <= tn and COL_VC % tn + KV_W <= tn
    assert tm % (BF16_TILE_ROWS * CMP_STRIDE) == 0
    hb_spec = pl.BlockSpec((1, N_KV_GROUPS, tm // CMP_STRIDE, CMP_STRIDE * HEAD_DIM),
                           lambda i, j: (i // per_b, 0, i % per_b, 0))
    hb_shape = jax.ShapeDtypeStruct(
        (t // seq, N_KV_GROUPS, seq // CMP_STRIDE, CMP_STRIDE * HEAD_DIM), BF16)
    return pl.pallas_call(
        functools.partial(_inproj_kernel, tn=tn, per_b=per_b),
        grid=(t // tm, PROJ_W // tn),
        in_specs=[pl.BlockSpec((tm, d), lambda i, j: (i, 0)),
                  pl.BlockSpec(mod.shape, lambda i, j: (0, 0)),
                  pl.BlockSpec((1, d), lambda i, j: (0, 0)),
                  pl.BlockSpec((d, tn), lambda i, j: (0, j)),
                  pl.BlockSpec((d, LANES), lambda i, j: (0, 0)),
                  pl.BlockSpec((tm, HEAD_DIM), lambda i, j: (i, 0)),
                  pl.BlockSpec((tm, HEAD_DIM), lambda i, j: (i, 0))],
        out_specs=[pl.BlockSpec((tm, tn), lambda i, j: (i, j)),
                   pl.BlockSpec((tm, LANES), lambda i, j: (i, 0)), hb_spec, hb_spec],
        out_shape=[jax.ShapeDtypeStruct((t, PROJ_W), BF16),
                   jax.ShapeDtypeStruct((t, LANES), F32), hb_shape, hb_shape],
        scratch_shapes=[pltpu.VMEM((tm, d), BF16), pltpu.VMEM((tm, HEAD_DIM), F32)],
        compiler_params=_cparams(("parallel", "arbitrary")),
        name="inproj",
    )(x2, mod, g_pre, w_in_p, w_gates, cos_t, sin_t)


def _compress_kernel(h_ref, w1_ref, pe_ref, w2_ref, o_ref):
    hb = h_ref[0]
    half = hb.shape[1]
    ya = _dot(hb, w1_ref[0:half, :])
    yb = _dot(hb, w1_ref[half:2 * half, :])
    const = _dot(pe_ref[...], w1_ref[...])
    n = ya.shape[0]
    yb_next = pltpu.roll(yb, n - 1, axis=0)
    hid = _silu(ya + yb_next + const[0:1, :])
    out = _dot(hid.astype(BF16), w2_ref[...])
    row = lax.broadcasted_iota(jnp.int32, out.shape, 0)
    o_ref[0] = jnp.where(row < n - 1, out, 0.0).astype(BF16)


def _compress(hblk, w1, pe, w2):
    bg, n, half = hblk.shape
    pe8 = jnp.zeros((8, 2 * half), BF16).at[0].set(pe.reshape(-1).astype(BF16))
    return pl.pallas_call(
        _compress_kernel,
        grid=(bg,),
        in_specs=[pl.BlockSpec((1, n, half), lambda i: (i, 0, 0)),
                  pl.BlockSpec((2 * half, HEAD_DIM), lambda i: (0, 0)),
                  pl.BlockSpec((8, 2 * half), lambda i: (0, 0)),
                  pl.BlockSpec((HEAD_DIM, HEAD_DIM), lambda i: (0, 0))],
        out_specs=pl.BlockSpec((1, n, HEAD_DIM), lambda i: (i, 0, 0)),
        out_shape=jax.ShapeDtypeStruct((bg, n, HEAD_DIM), BF16),
        compiler_params=_cparams(("parallel",)),
        name="compress",
    )(hblk, w1.astype(BF16), pe8, w2.astype(BF16))


def _cmp_kernel(q_ref, kc_ref, vc_ref, gates_ref, ov_ref, ocmp_ref, selb_ref, imp_s, cnt_s, *,
                tq, nsel):
    g = pl.program_id(0) % N_KV_GROUPS
    q0 = pl.program_id(1) * tq
    kc = kc_ref[0]
    vc = vc_ref[0]
    nc = kc.shape[0]
    t_idx = q0 + lax.broadcasted_iota(jnp.int32, (tq, nc), 0)
    c_idx = lax.broadcasted_iota(jnp.int32, (tq, nc), 1)
    c_ok = c_idx * CMP_STRIDE + (CMP_BLOCK - 1) <= t_idx
    gl = gates_ref[...]
    lane = lax.broadcasted_iota(jnp.int32, (tq, LANES), 1)
    psum = jnp.zeros((tq, nc), F32)
    for h in range(HEADS_PER_GROUP):
        qh = q_ref[:, h * HEAD_DIM:(h + 1) * HEAD_DIM]
        s = jnp.where(c_ok, _dot_nt(qh, kc), -jnp.inf)
        m = jnp.max(s, axis=-1, keepdims=True)
        m = jnp.where(m == -jnp.inf, 0.0, m)
        p = jnp.exp2(s - m)
        p = p * (1.0 / jnp.maximum(jnp.sum(p, axis=-1, keepdims=True), 1e-30))
        psum = psum + p
        o = _dot(p.astype(BF16), vc)
        glog = _lane_col(gl, lane, (g * HEADS_PER_GROUP + h) * 3)
        ocmp_ref[:, h * HEAD_DIM:(h + 1) * HEAD_DIM] = (o * _sigmoid(glog)).astype(BF16)

    imp = _dot(psum, ov_ref[...], precision=lax.Precision.HIGHEST)
    cur = (q0 + lax.broadcasted_iota(jnp.int32, (tq, LANES), 0)) // SEL_BLOCK
    forced = (lane == 0) | (lane == cur) | (lane == cur - 1)
    imp = jnp.where(lane <= cur, imp + jnp.where(forced, FORCE_BONUS, 0.0), -jnp.inf)
    imp_s[...] = imp.T[0:nsel, :]
    cnt_s[...] = jnp.zeros(cnt_s.shape, F32)
    sub = SUBLANES
    n_groups = nsel // sub
    row = lax.broadcasted_iota(jnp.int32, (sub, tq), 0)

    def count_against(kg):
        groups = [imp_s[g * sub:(g + 1) * sub, :] for g in range(n_groups)]
        cnts = [cnt_s[g * sub:(g + 1) * sub, :] for g in range(n_groups)]
        for k in range(kg * sub, (kg + 1) * sub):
            vk = imp_s[k:k + 1, :]
            for g, grp in enumerate(groups):
                if k < g * sub:
                    beats = vk >= grp
                elif k >= (g + 1) * sub:
                    beats = vk > grp
                else:
                    beats = (vk > grp) | ((vk == grp) & (row > k - g * sub))
                cnts[g] = cnts[g] + jnp.where(beats, 1.0, 0.0)
        for g in range(n_groups):
            cnt_s[g * sub:(g + 1) * sub, :] = cnts[g]

    last_block = (q0 + tq - 1) // SEL_BLOCK
    for kg in range(n_groups):
        pl.when(kg * sub <= last_block)(functools.partial(count_against, kg))
    imp_t = imp_s[...]
    sel = (cnt_s[...] < float(N_SEL)) & (imp_t > -jnp.inf)
    bias_t = jnp.where(sel, 0.0, -MASK_BIG)
    if nsel < LANES:
        bias_t = jnp.concatenate([bias_t, jnp.zeros((LANES - nsel, tq), F32)], axis=0)
    selb_ref[...] = bias_t.T.astype(BF16)


def _cmp_attn(proj, kc, vc, gates, batch, seq, tq):
    t = proj.shape[0]
    nq = seq // tq
    nc = kc.shape[1]
    nsel = seq // SEL_BLOCK
    assert nsel <= LANES and nsel % 8 == 0
    c_start = np.arange(nc) * CMP_STRIDE
    j_start = np.arange(LANES) * SEL_BLOCK
    ov = ((c_start[:, None] < j_start[None, :] + SEL_BLOCK)
          & (c_start[:, None] + CMP_BLOCK > j_start[None, :])
          & (np.arange(LANES)[None, :] < nsel)).astype(np.float32)
    gq = HEADS_PER_GROUP * HEAD_DIM
    qmap = lambda bg, i: ((bg // N_KV_GROUPS) * nq + i, bg % N_KV_GROUPS)
    return pl.pallas_call(
        functools.partial(_cmp_kernel, tq=tq, nsel=nsel),
        grid=(batch * N_KV_GROUPS, nq),
        in_specs=[pl.BlockSpec((tq, gq), qmap),
                  pl.BlockSpec((1, nc, HEAD_DIM), lambda bg, i: (bg, 0, 0)),
                  pl.BlockSpec((1, nc, HEAD_DIM), lambda bg, i: (bg, 0, 0)),
                  pl.BlockSpec((tq, LANES), lambda bg, i: ((bg // N_KV_GROUPS) * nq + i, 0)),
                  pl.BlockSpec((nc, LANES), lambda bg, i: (0, 0))],
        out_specs=[pl.BlockSpec((tq, gq), qmap),
                   pl.BlockSpec((tq, LANES), lambda bg, i: (bg * nq + i, 0))],
        out_shape=[jax.ShapeDtypeStruct((t, ATT_W), BF16),
                   jax.ShapeDtypeStruct((batch * N_KV_GROUPS * seq, LANES), BF16)],
        scratch_shapes=[pltpu.VMEM((nsel, tq), F32), pltpu.VMEM((nsel, tq), F32)],
        compiler_params=_cparams(("parallel", "parallel")),
        name="cmp_attn",
    )(proj, kc, vc, gates, jnp.asarray(ov))


def _selwin_kernel(q_ref, selb_ref, ocmp_ref, gates_ref, ks_ref, vs_ref, kw_ref, vw_ref,
                   cb_ref, wb_ref, o_ref, kaug_s, vaug_s, vwaug_s, qaug_s, sbuf_s, m_s, acc_s,
                   swin_s, *, tq, tk, seq, nwin):
    g = pl.program_id(0) % N_KV_GROUPS
    qi = pl.program_id(1)
    q0 = qi * tq
    hq = HEADS_PER_GROUP * tq
    half = hq // 2

    @pl.when(qi == 0)
    def _():
        r = lax.broadcasted_iota(jnp.int32, (seq, HEAD_DIM), 0)
        ln = lax.broadcasted_iota(jnp.int32, (seq, HEAD_DIM), 1)
        ones_col = jnp.where(ln == 0, 1.0, 0.0).astype(BF16)
        kaug_s[:, 0:HEAD_DIM] = ks_ref[...]
        kaug_s[:, HEAD_DIM:2 * HEAD_DIM] = jnp.where(r // SEL_BLOCK == ln, 1.0, 0.0).astype(BF16)
        vaug_s[:, 0:HEAD_DIM] = vs_ref[...]
        vaug_s[:, HEAD_DIM:2 * HEAD_DIM] = ones_col
        vwaug_s[:, 0:HEAD_DIM] = vw_ref[...]
        vwaug_s[:, HEAD_DIM:2 * HEAD_DIM] = ones_col

    for h in range(HEADS_PER_GROUP):
        qaug_s[h * tq:(h + 1) * tq, 0:HEAD_DIM] = q_ref[:, h * HEAD_DIM:(h + 1) * HEAD_DIM]
        qaug_s[h * tq:(h + 1) * tq, HEAD_DIM:2 * HEAD_DIM] = selb_ref[...]
    qa = qaug_s[...]

    def scores(kt, slot):
        ks = kaug_s[pl.ds(pl.multiple_of(kt * tk, tk), tk), :]
        sbuf_s[slot] = _dot_nt(qa, ks)

    def absorb(kt, slot, diagonal=False):
        vs = vaug_s[pl.ds(pl.multiple_of(kt * tk, tk), tk), :]
        s = sbuf_s[slot]
        if diagonal:
            bias = cb_ref[(q0 - kt * tk) // tq]
            s = (s.reshape(HEADS_PER_GROUP, tq, tk) + bias[None]).reshape(hq, tk)
        m = m_s[...]
        m_new = jnp.maximum(m, jnp.max(s, axis=-1, keepdims=True))
        m_s[...] = m_new
        pb = jnp.exp2(s - m_new[:, 0:1]).astype(BF16)
        if diagonal:
            pv = jnp.concatenate([_dot(pb[0:half], vs), _dot(pb[half:hq], vs)], axis=0)
        else:
            pv = _dot(pb, vs)
        acc_s[...] = jnp.exp2(m - m_new)[:, 0:1] * acc_s[...] + pv

    n_full = q0 // tk
    m_s[...] = jnp.full(m_s.shape, -jnp.inf, F32)
    acc_s[...] = jnp.zeros(acc_s.shape, F32)
    scores(0, 0)

    def pair(j, c):
        scores(2 * j + 1, 1)
        absorb(2 * j, 0)
        scores(2 * j + 2, 0)
        absorb(2 * j + 1, 1)
        return c

    lax.fori_loop(0, n_full // 2, pair, 0)
    odd = n_full % 2

    @pl.when(odd == 1)
    def _():
        scores(n_full, 1)
        absorb(n_full - 1, 0)

    nq = seq // tq
    w0 = jnp.clip(qi - (nwin - 1), 0, nq - nwin) * tq
    wlen = nwin * tq
    kwin = kw_ref[pl.ds(pl.multiple_of(w0, tq), wlen), :]
    vwin = vwaug_s[pl.ds(pl.multiple_of(w0, tq), wlen), :]
    swin_s[0:half, :] = _dot_nt(qaug_s[0:half, 0:HEAD_DIM], kwin)
    swin_s[half:hq, :] = _dot_nt(qaug_s[half:hq, 0:HEAD_DIM], kwin)
    absorb(n_full, odd, diagonal=True)
    acc = acc_s[...]
    o_sel = acc[:, 0:HEAD_DIM] / acc[:, HEAD_DIM:HEAD_DIM + 1]

    wbias = wb_ref[(q0 - w0) // tq]
    sw = (swin_s[...].reshape(HEADS_PER_GROUP, tq, wlen) + wbias[None]).reshape(hq, wlen)
    pwb = jnp.exp2(sw - jnp.max(sw, axis=-1, keepdims=True)).astype(BF16)
    ow = jnp.concatenate([_dot(pwb[0:half], vwin), _dot(pwb[half:hq], vwin)], axis=0)
    o_win = ow[:, 0:HEAD_DIM] / ow[:, HEAD_DIM:HEAD_DIM + 1]

    gl = gates_ref[...]
    lane = lax.broadcasted_iota(jnp.int32, (tq, LANES), 1)
    for h in range(HEADS_PER_GROUP):
        gcol = (g * HEADS_PER_GROUP + h) * 3
        g_s = _sigmoid(_lane_col(gl, lane, gcol + 1))
        g_w = _sigmoid(_lane_col(gl, lane, gcol + 2))
        o = (ocmp_ref[:, h * HEAD_DIM:(h + 1) * HEAD_DIM].astype(F32)
             + g_s * o_sel[h * tq:(h + 1) * tq, :] + g_w * o_win[h * tq:(h + 1) * tq, :])
        o_ref[:, h * HEAD_DIM:(h + 1) * HEAD_DIM] = o.astype(BF16)


def _selwin_attn(proj, selb, ocmp, gates, batch, seq, tq, tk):
    t = proj.shape[0]
    nq = seq // tq
    assert tk % tq == 0 and tq & (tq - 1) == 0 and seq % tk == 0
    nwin = min(WINDOW // tq + 1, nq)
    assert WINDOW % tq == 0
    gq = HEADS_PER_GROUP * HEAD_DIM
    hq = HEADS_PER_GROUP * tq
    r = np.arange(tq)[:, None]
    cbias = np.stack([np.where(np.arange(tk)[None, :] <= r + off * tq, 0.0, -np.inf)
                      for off in range(tk // tq)]).astype(np.float32)
    dist = [r + off * tq - np.arange(nwin * tq)[None, :] for off in range(nwin)]
    wbias = np.stack([np.where((d >= 0) & (d < WINDOW), 0.0, -np.inf)
                      for d in dist]).astype(np.float32)
    qmap = lambda bg, i: ((bg // N_KV_GROUPS) * nq + i, bg % N_KV_GROUPS)

    def kvspec(col):
        return pl.BlockSpec((seq, HEAD_DIM),
                            lambda bg, i: (bg // N_KV_GROUPS, col // HEAD_DIM + bg % N_KV_GROUPS))

    return pl.pallas_call(
        functools.partial(_selwin_kernel, tq=tq, tk=tk, seq=seq, nwin=nwin),
        grid=(batch * N_KV_GROUPS, nq),
        in_specs=[pl.BlockSpec((tq, gq), qmap),
                  pl.BlockSpec((tq, LANES), lambda bg, i: (bg * nq + i, 0)),
                  pl.BlockSpec((tq, gq), qmap),
                  pl.BlockSpec((tq, LANES), lambda bg, i: ((bg // N_KV_GROUPS) * nq + i, 0)),
                  kvspec(COL_KS), kvspec(COL_VS), kvspec(COL_KW), kvspec(COL_VW),
                  pl.BlockSpec(cbias.shape, lambda bg, i: (0, 0, 0)),
                  pl.BlockSpec(wbias.shape, lambda bg, i: (0, 0, 0))],
        out_specs=pl.BlockSpec((tq, gq), qmap),
        out_shape=jax.ShapeDtypeStruct((t, ATT_W), BF16),
        scratch_shapes=[pltpu.VMEM((seq, 2 * HEAD_DIM), BF16),
                        pltpu.VMEM((seq, 2 * HEAD_DIM), BF16),
                        pltpu.VMEM((seq, 2 * HEAD_DIM), BF16),
                        pltpu.VMEM((hq, 2 * HEAD_DIM), BF16),
                        pltpu.VMEM((2, hq, tk), F32),
                        pltpu.VMEM((hq, LANES), F32),
                        pltpu.VMEM((hq, 2 * HEAD_DIM), F32),
                        pltpu.VMEM((hq, nwin * tq), F32)],
        compiler_params=_cparams(("parallel", "arbitrary")),
        name="selwin_attn",
    )(proj, selb, ocmp, gates, proj, proj, proj, proj, jnp.asarray(cbias), jnp.asarray(wbias))


_HGRN_LEVELS = (32, 16, 8, 4, 2, 1)


def _hgrn_tables():
    c = HGRN_CHUNK
    nl = len(_HGRN_LEVELS)
    idx = np.arange(c)
    i, j = idx[:, None], idx[None, :]
    mats = [j <= i, j > i]
    lvl = np.full((c, c), nl + 1, np.int32)
    for li, s in enumerate(_HGRN_LEVELS):
        mats.append((j >= (i // s) * s) & (j <= i))
        mats.append((j > i) & (j <= (i // s) * s + s - 1))
        lvl[((i // (2 * s)) == (j // (2 * s))) & (((i // s) % 2) == 1) & (((j // s) % 2) == 0)] = li
    lvl[i == j] = nl
    w = np.concatenate(mats, 0).astype(np.float32)
    return np.concatenate([w, w], axis=1), lvl


def _hgrn_head(hq_ref, hf_ref, hi_ref, hg_ref, lbl_ref, gn_ref, w_ref, lvl_ref, o_ref, st_s,
               hh, ts):
    c = HGRN_CHUNK
    nc = ts // c
    nl = len(_HGRN_LEVELS)
    cs = slice(hh * HEAD_DIM, (hh + 1) * HEAD_DIM)
    lbl = lbl_ref[:, cs]
    e = jnp.exp(lbl - jnp.max(lbl, axis=0, keepdims=True))
    lb = e[0:1, :] / jnp.sum(e, axis=0, keepdims=True)

    f = lb + (1.0 - lb) * _sigmoid(hf_ref[:, cs].astype(F32))
    k = 1.0 - f
    qf = _silu(hq_ref[:, cs].astype(F32))
    v = hi_ref[:, cs]

    lf = jnp.log(f) * LOG2E
    lf_hi = lf.astype(BF16)
    lf_lo = (lf - lf_hi.astype(F32)).astype(BF16)
    side = lambda a: jnp.concatenate([a[ci * c:(ci + 1) * c, :] for ci in range(nc)], axis=1)
    ex = _dot(w_ref[...], jnp.concatenate([side(lf_hi), side(lf_lo)], axis=0))

    def rows(bi):
        return jnp.concatenate(
            [ex[bi * c:(bi + 1) * c, ci * HEAD_DIM:(ci + 1) * HEAD_DIM] for ci in range(nc)], axis=0)

    b = rows(0)
    qhat = (qf * jnp.exp2(b)).astype(BF16)
    khat = (k * jnp.exp2(rows(1))).astype(BF16)
    lvl = lvl_ref[...]
    masks = [lvl == li for li in range(nl + 1)]
    qs = [(qf * jnp.exp2(rows(2 + 2 * li))).astype(BF16) for li in range(nl)] + [qf.astype(BF16)]
    ks = [(k * jnp.exp2(rows(3 + 2 * li))).astype(BF16) for li in range(nl)] + [k.astype(BF16)]
    o_intra = []
    for ci in range(nc):
        sl = slice(ci * c, (ci + 1) * c)
        a = jnp.zeros((c, c), F32)
        for li in range(nl + 1):
            a = jnp.where(masks[li], _dot_nt(qs[li][sl, :], ks[li][sl, :]), a)
        o_intra.append(_dot(a.astype(BF16), v[sl, :]))

    st = st_s[hh]
    outs = []
    for ci in range(nc):
        sl = slice(ci * c, (ci + 1) * c)
        outs.append(o_intra[ci] + _dot_nt(qhat[sl, :], st.astype(BF16)))
        st = st * jnp.exp2(b[(ci + 1) * c - 1:(ci + 1) * c, :]) + lax.dot_general(
            v[sl, :], khat[sl, :], _TN, preferred_element_type=F32)
    st_s[hh] = st
    o = jnp.concatenate(outs, axis=0)
    o = o * lax.rsqrt(jnp.mean(o * o, axis=-1, keepdims=True) + EPS) * gn_ref[:, cs]
    o_ref[:, cs] = (o * _silu(hg_ref[:, cs].astype(F32))).astype(BF16)


def _hgrn_kernel(hq_ref, hf_ref, hi_ref, hg_ref, lbl_ref, gn_ref, w_ref, lvl_ref,
                 o_ref, st_s, *, ts, hb):
    @pl.when(pl.program_id(2) == 0)
    def _():
        st_s[...] = jnp.zeros_like(st_s)

    for hh in range(hb):
        _hgrn_head(hq_ref, hf_ref, hi_ref, hg_ref, lbl_ref, gn_ref, w_ref, lvl_ref, o_ref, st_s,
                   hh, ts)


def _hgrn(proj, lb_logits, g_norm, batch, seq, ts, hb):
    t = proj.shape[0]
    ns = seq // ts
    wmat, lvl = _hgrn_tables()
    w = hb * HEAD_DIM
    assert all(col % w == 0 for col in (COL_HQ, COL_HF, COL_HI, COL_HG))

    def colspec(col):
        return pl.BlockSpec((ts, w), lambda b, h, s: (b * ns + s, col // w + h))

    nl = lb_logits.shape[0]
    return pl.pallas_call(
        functools.partial(_hgrn_kernel, ts=ts, hb=hb),
        grid=(batch, N_HGRN_HEADS // hb, ns),
        in_specs=[colspec(COL_HQ), colspec(COL_HF), colspec(COL_HI), colspec(COL_HG),
                  pl.BlockSpec((nl, w), lambda b, h, s: (0, h)),
                  pl.BlockSpec((1, w), lambda b, h, s: (0, h)),
                  pl.BlockSpec(wmat.shape, lambda b, h, s: (0, 0)),
                  pl.BlockSpec(lvl.shape, lambda b, h, s: (0, 0))],
        out_specs=pl.BlockSpec((ts, w), lambda b, h, s: (b * ns + s, h)),
        out_shape=jax.ShapeDtypeStruct((t, ATT_W), BF16),
        scratch_shapes=[pltpu.VMEM((hb, HEAD_DIM, HEAD_DIM), F32)],
        compiler_params=_cparams(("parallel", "parallel", "arbitrary")),
        name="hgrn2",
    )(proj, proj, proj, proj, lb_logits, g_norm, jnp.asarray(wmat, BF16), jnp.asarray(lvl))


def _outproj_kernel(oa_ref, or_ref, wa_ref, wb_ref, x_ref, mod_ref, gpost_ref, gpre_ref,
                    wr_ref, br_ref, x1_ref, h2_ref, lg_ref, stage, *, per_b):
    b = pl.program_id(0) // per_b
    y = _dot(oa_ref[...], wa_ref[...]) + _dot(or_ref[...], wb_ref[...])
    x1 = x_ref[...] + _mod_row(mod_ref, b, 2) * _rms(y, gpost_ref[...])
    x1_ref[...] = x1
    h = _rms(x1, gpre_ref[...]) * (1.0 + _mod_row(mod_ref, b, 4)) + _mod_row(mod_ref, b, 3)
    h2_ref[...] = _to_slabs(stage, h)
    h_hi = h.astype(BF16)
    h_lo = (h - h_hi.astype(F32)).astype(BF16)
    hh = _dot(h_hi, wr_ref[...])
    lg_ref[...] = (hh[:, 0:LANES] + hh[:, LANES:2 * LANES]
                   + _dot(h_lo, wr_ref[:, 0:LANES]) + br_ref[...])


def _outproj(o_att, o_rec, w_out, x2, mod, g_post, g_pre, wr, br, seq, tm):
    t, d = x2.shape
    per_b = seq // tm
    wr_hi = wr.astype(BF16)
    wr_cat = jnp.concatenate([wr_hi, (wr - wr_hi.astype(F32)).astype(BF16)], axis=1)
    wa = w_out[:ATT_W].astype(BF16)
    wb = w_out[ATT_W:].astype(BF16)
    row = lambda i: (i, 0)
    fixed = lambda i: (0, 0)
    return pl.pallas_call(
        functools.partial(_outproj_kernel, per_b=per_b),
        grid=(t // tm,),
        in_specs=[pl.BlockSpec((tm, ATT_W), row), pl.BlockSpec((tm, ATT_W), row),
                  pl.BlockSpec((ATT_W, d), fixed), pl.BlockSpec((ATT_W, d), fixed),
                  pl.BlockSpec((tm, d), row),
                  pl.BlockSpec(mod.shape, fixed),
                  pl.BlockSpec((1, d), fixed), pl.BlockSpec((1, d), fixed),
                  pl.BlockSpec((d, 2 * LANES), fixed), pl.BlockSpec((1, LANES), fixed)],
        out_specs=[pl.BlockSpec((tm, d), row), pl.BlockSpec((tm * (d // LANES), LANES), row),
                   pl.BlockSpec((tm, LANES), row)],
        out_shape=[jax.ShapeDtypeStruct((t, d), F32),
                   jax.ShapeDtypeStruct((t * (d // LANES), LANES), BF16),
                   jax.ShapeDtypeStruct((t, LANES), F32)],
        scratch_shapes=[pltpu.VMEM((tm * (d // LANES), LANES), F32)],
        compiler_params=_cparams(("parallel",)),
        name="outproj",
    )(o_att, o_rec, wa, wb, x2, mod, g_post, g_pre, wr_cat, br)


G_LANE0 = N_EXPERTS


def _route_kernel(lg_ref, tri_ref, info_ref, cnt_ref, carry_s):
    @pl.when(pl.program_id(0) == 0)
    def _():
        carry_s[...] = jnp.zeros_like(carry_s)

    x = lg_ref[...]
    lane = lax.broadcasted_iota(jnp.int32, x.shape, 1)
    big = jnp.int32(10 ** 6)
    rmax = lambda a: jnp.max(a, axis=-1, keepdims=True)
    rmin = lambda a: jnp.min(a, axis=-1, keepdims=True)
    rsum = lambda a: jnp.sum(a, axis=-1, keepdims=True)

    is_g = (lane >= G_LANE0) & (lane < G_LANE0 + N_EXPERT_GROUPS)
    lgm = jnp.where(is_g, x, -jnp.inf)
    mg = rmax(lgm)
    pg_sel = 1.0 / rsum(jnp.where(is_g, jnp.exp(lgm - mg), 0.0))
    gsel = rmin(jnp.where(lgm == mg, lane, big)) - G_LANE0

    is_e = (lane >= gsel * EXPERTS_PER_GROUP) & (lane < (gsel + 1) * EXPERTS_PER_GROUP)
    lem = jnp.where(is_e, x, -jnp.inf)
    pe = jnp.where(is_e, jnp.exp(lem - rmax(lem)), 0.0)
    pe = pe / rsum(pe)
    pe = jnp.where(is_e, pe, -1.0)
    v1 = rmax(pe)
    i1 = rmin(jnp.where(pe == v1, lane, big))
    pe2 = jnp.where(lane == i1, -1.0, pe)
    v2 = rmax(pe2)
    i2 = rmin(jnp.where(pe2 == v2, lane, big))
    w1 = v1 / (v1 + v2) * pg_sel
    w2 = v2 / (v1 + v2) * pg_sel

    oh1 = jnp.where(lane == i1, 1.0, 0.0)
    oh2 = jnp.where(lane == i2, 1.0, 0.0)
    both = oh1 + oh2
    before = _dot(tri_ref[...], both.astype(BF16)) + carry_s[0:1, :]
    r1 = rsum(oh1 * before)
    r2 = rsum(oh2 * before)
    carry_s[0:1, :] = carry_s[0:1, :] + jnp.sum(both, axis=0, keepdims=True)
    cnt_ref[...] = carry_s[...]

    cols = (i1.astype(F32), i2.astype(F32), w1, w2, r1, r2)
    info = jnp.zeros(x.shape, F32)
    for ci, col in enumerate(cols):
        info = jnp.where(lane == ci, col, info)
    info_ref[...] = info


def _route(logits, tm):
    t = logits.shape[0]
    tri = jnp.asarray(np.tril(np.ones((tm, tm), np.float32), -1), BF16)
    return pl.pallas_call(
        _route_kernel,
        grid=(t // tm,),
        in_specs=[pl.BlockSpec((tm, LANES), lambda i: (i, 0)),
                  pl.BlockSpec((tm, tm), lambda i: (0, 0))],
        out_specs=[pl.BlockSpec((tm, LANES), lambda i: (i, 0)),
                   pl.BlockSpec((8, LANES), lambda i: (0, 0))],
        out_shape=[jax.ShapeDtypeStruct((t, LANES), F32),
                   jax.ShapeDtypeStruct((8, LANES), F32)],
        scratch_shapes=[pltpu.VMEM((8, LANES), F32)],
        compiler_params=_cparams(("arbitrary",)),
        name="route",
    )(logits, tri)


def _to_slabs(stage, x):
    n, d = x.shape
    slab = d // LANES
    for c in range(slab):
        stage[pl.ds(c, n, stride=slab), :] = x[:, c * LANES:(c + 1) * LANES]
    return stage[...].astype(BF16)


def _from_slabs(stage, slabs, n):
    slab = slabs.shape[0] // n
    stage[...] = slabs.astype(F32)
    return jnp.concatenate([stage[pl.ds(c, n, stride=slab), :] for c in range(slab)], axis=1)


def _expert_kernel(te_ref, nu_ref, par_ref, nxt_ref, dst_ref, xs_ref, wg_hbm, wu_hbm, wd_hbm,
                   y_ref, obuf, stage, ssem, wg_f, wu_f, wd_f, wsem, wg_s, wu_s, wd_s, *,
                   tm, standin_row):
    t = pl.program_id(0)
    nu = nu_ref[0]
    slot = t % 2

    def weights(e, buf):
        return [pltpu.make_async_copy(src.at[e], dst.at[buf], wsem.at[buf])
                for src, dst in ((wg_hbm, wg_f), (wu_hbm, wu_f), (wd_hbm, wd_f))]
    slab = wg_s.shape[0] // LANES
    tile = tm * slab

    def token(ref, i):
        return ref.at[pl.ds(pl.multiple_of(i * slab, slab), slab)]

    def scatter(s):
        for r in range(tm):
            pltpu.make_async_copy(token(obuf, s * tm + r), token(y_ref, dst_ref[0, 0, r]),
                                  ssem.at[s]).start()

    def tile_of(buf, s):
        return buf.at[pl.ds(pl.multiple_of(s * tile, tile), tile)]

    def wait_tile(s):
        pltpu.make_async_copy(tile_of(obuf, s), tile_of(obuf, s), ssem.at[s]).wait()

    @pl.when(t < nu)
    def _():
        @pl.when(t == 0)
        def _():
            obuf[tile:2 * tile, :] = jnp.zeros((tile, LANES), obuf.dtype)

            def fill(row0, s):
                return pltpu.make_async_copy(tile_of(obuf, 1),
                                             y_ref.at[pl.ds(row0 * slab, tile)], ssem.at[s])

            fill(standin_row - 2 * tm, 0).start()
            fill(standin_row - tm, 0).start()
            fill(standin_row - 2 * tm, 0).wait()
            fill(standin_row - tm, 0).wait()
            fill(standin_row, 1).start()

        par = par_ref[t]

        @pl.when(t == 0)
        def _():
            for cp in weights(te_ref[0], par):
                cp.start()

        @pl.when((t == 0) | (te_ref[t] != te_ref[jnp.maximum(t - 1, 0)]))
        def _():
            for cp in weights(te_ref[t], par):
                cp.wait()
            wg_s[...] = wg_f[par].astype(BF16)
            wu_s[...] = wu_f[par].astype(BF16)
            wd_s[...] = wd_f[par].astype(BF16)

            @pl.when(nxt_ref[t] != te_ref[t])
            def _():
                for cp in weights(nxt_ref[t], 1 - par):
                    cp.start()

        x = _from_slabs(stage, xs_ref[...], tm).astype(BF16)
        hid = (_silu(_dot(x, wg_s[...])) * _dot(x, wu_s[...])).astype(BF16)
        row0 = pl.multiple_of(slot * tile, tile)
        obuf[pl.ds(row0, tile), :] = _to_slabs(stage, _dot(hid, wd_s[...]))
        scatter(slot)
        wait_tile(1 - slot)

        @pl.when(t == nu - 1)
        def _():
            wait_tile(slot)


def _experts(tile_expert, n_used, parity, next_expert, dst3, xs, w_gate, w_up, w_down, y_rows,
             tm, standin_row):
    n_tiles = dst3.shape[0]
    d, f = w_gate.shape[1:]
    slab = d // LANES
    hbm = pl.BlockSpec(memory_space=pl.ANY)
    return pl.pallas_call(
        functools.partial(_expert_kernel, tm=tm, standin_row=standin_row),
        grid_spec=pltpu.PrefetchScalarGridSpec(
            num_scalar_prefetch=4,
            grid=(n_tiles,),
            in_specs=[pl.BlockSpec((1, 1, tm), lambda t, te, nu, pa, nx: (t, 0, 0),
                                   memory_space=pltpu.SMEM),
                      pl.BlockSpec((tm * slab, LANES),
                                   lambda t, te, nu, pa, nx: (jnp.minimum(t, nu[0] - 1), 0)),
                      hbm, hbm, hbm],
            out_specs=pl.BlockSpec(memory_space=pl.ANY),
            scratch_shapes=[pltpu.VMEM((2 * tm * slab, LANES), BF16),
                            pltpu.VMEM((tm * slab, LANES), F32),
                            pltpu.SemaphoreType.DMA((2,)),
                            pltpu.VMEM((2, d, f), F32), pltpu.VMEM((2, d, f), F32),
                            pltpu.VMEM((2, f, d), F32),
                            pltpu.SemaphoreType.DMA((2,)),
                            pltpu.VMEM((d, f), BF16), pltpu.VMEM((d, f), BF16),
                            pltpu.VMEM((f, d), BF16)]),
        out_shape=jax.ShapeDtypeStruct((y_rows * slab, LANES), BF16),
        compiler_params=_cparams(("arbitrary",)),
        name="moe_experts",
    )(tile_expert, n_used, parity, next_expert, dst3, xs, w_gate, w_up, w_down)


def _dispatch_kernel(pos0_ref, pos1_ref, end_ref, nu_ref, h_ref, dst_in, xs_ref, dst_ref,
                     zbuf, zsem, sem, *, tm, slab, n_tiles, n_tokens):
    tile = tm * slab
    base = pl.program_id(0) * tm

    @pl.when(pl.program_id(0) == 0)
    def _():
        init = pltpu.make_async_copy(dst_in, dst_ref, zsem)
        init.start()
        init.wait()
        zbuf[...] = jnp.zeros(zbuf.shape, zbuf.dtype)

        def zero(row_end, go):
            cp = pltpu.make_async_copy(
                zbuf, xs_ref.at[pl.ds(pl.multiple_of((row_end - tm) * slab, slab), tile)], zsem)
            pl.when(go)(cp.start)
            return cp, go

        pending = [zero(end_ref[e], end_ref[e] > (end_ref[e - 1] if e else 0))
                   for e in range(N_EXPERTS)]
        pending += [zero((i + 1) * tm, i >= nu_ref[0]) for i in range(n_tiles)]
        for cp, go in pending:
            pl.when(go)(cp.wait)

    def token(ref, i):
        return ref.at[pl.ds(pl.multiple_of(i * slab, slab), slab)]

    def start(r, c):
        p0 = pos0_ref[0, 0, r]
        p1 = pos1_ref[0, 0, r]
        pltpu.make_async_copy(token(h_ref, r), token(xs_ref, p0), sem).start(priority=0)
        pltpu.make_async_copy(token(h_ref, r), token(xs_ref, p1), sem).start(priority=1)
        dst_ref[p0] = base + r
        dst_ref[p1] = n_tokens + base + r
        return c

    lax.fori_loop(0, tm, start, 0, unroll=8)
    for _ in range(2):
        pltpu.make_async_copy(h_ref, h_ref, sem).wait()


def _dispatch(pos0, pos1, tile_end, n_used, h2, dst0, n_tiles, tm):
    rows = h2.shape[0]
    slab = rows // pos0.shape[0]
    t = pos0.shape[0]
    per_tile = pl.BlockSpec((1, 1, tm), lambda i: (i, 0, 0), memory_space=pltpu.SMEM)
    whole = pl.BlockSpec(memory_space=pltpu.SMEM)
    return pl.pallas_call(
        functools.partial(_dispatch_kernel, tm=tm, slab=slab, n_tiles=n_tiles, n_tokens=t),
        grid=(t // tm,),
        in_specs=[per_tile, per_tile, whole, whole,
                  pl.BlockSpec((tm * slab, LANES), lambda i: (i, 0)),
                  pl.BlockSpec(memory_space=pl.ANY)],
        out_specs=[pl.BlockSpec(memory_space=pl.ANY), whole],
        out_shape=[jax.ShapeDtypeStruct((n_tiles * tm * slab, LANES), h2.dtype),
                   jax.ShapeDtypeStruct(dst0.shape, jnp.int32)],
        scratch_shapes=[pltpu.VMEM((tm * slab, LANES), h2.dtype),
                        pltpu.SemaphoreType.DMA(()), pltpu.SemaphoreType.DMA(())],
        compiler_params=_cparams(("arbitrary",)),
        name="moe_dispatch",
    )(pos0.reshape(t // tm, 1, tm), pos1.reshape(t // tm, 1, tm), tile_end * tm, n_used, h2,
      dst0)


def _final_kernel(info_ref, x1_ref, mod_ref, g_ref, ya_ref, yb_ref, o_ref, stage_a, stage_b, *,
                  per_b):
    info = info_ref[...]
    tm = x1_ref.shape[0]
    ya = _from_slabs(stage_a, ya_ref[...], tm)
    yb = _from_slabs(stage_b, yb_ref[...], tm)
    y = info[:, 2:3] * ya + info[:, 3:4] * yb
    gain = _mod_row(mod_ref, pl.program_id(0) // per_b, 5) * g_ref[...]
    inv = lax.rsqrt(jnp.mean(y * y, axis=-1, keepdims=True) + EPS)
    o_ref[...] = x1_ref[...] + (y * inv) * gain


def _moe_final(info, x1, mod, g_post, y2, seq, tm):
    t, d = x1.shape
    per_b = seq // tm
    nb = t // tm
    return pl.pallas_call(
        functools.partial(_final_kernel, per_b=per_b),
        grid=(nb,),
        in_specs=[pl.BlockSpec((tm, LANES), lambda i: (i, 0)),
                  pl.BlockSpec((tm, d), lambda i: (i, 0)),
                  pl.BlockSpec(mod.shape, lambda i: (0, 0)),
                  pl.BlockSpec((1, d), lambda i: (0, 0)),
                  pl.BlockSpec((tm * (d // LANES), LANES), lambda i: (i, 0)),
                  pl.BlockSpec((tm * (d // LANES), LANES), lambda i: (nb + i, 0))],
        out_specs=pl.BlockSpec((tm, d), lambda i: (i, 0)),
        out_shape=jax.ShapeDtypeStruct((t, d), F32),
        scratch_shapes=[pltpu.VMEM((tm * (d // LANES), LANES), F32)] * 2,
        compiler_params=_cparams(("parallel",)),
        name="moe_final",
    )(info, x1, mod, g_post, y2, y2)


def _moe(h2, logits, x1, mod, g_post, w_gate, w_up, w_down, seq, tm_route, tm_exp, tm_fin):
    t = x1.shape[0]
    info, cnt = _route(logits, tm_route)
    counts = cnt[0, :N_EXPERTS].astype(jnp.int32)
    tiles_e = (counts + tm_exp - 1) // tm_exp
    tile_end = jnp.cumsum(tiles_e)
    offs = (tile_end - tiles_e) * tm_exp
    n_used = tile_end[-1]

    def slot_of(k):
        e = info[:, k].astype(jnp.int32)
        start = jnp.sum(jnp.where(e[:, None] == jnp.arange(N_EXPERTS)[None, :], offs[None, :], 0),
                        axis=1)
        return start + info[:, 4 + k].astype(jnp.int32)

    n_tiles = (2 * t) // tm_exp + N_EXPERTS
    tidx = jnp.minimum(jnp.arange(n_tiles, dtype=jnp.int32), n_used - 1)
    tile_expert = jnp.sum(tidx[:, None] >= tile_end[None, :], axis=1).astype(jnp.int32)
    switches = jnp.concatenate([jnp.zeros((1,), jnp.int32),
                                (tile_expert[1:] != tile_expert[:-1]).astype(jnp.int32)])
    parity = jnp.cumsum(switches) % 2
    next_expert = tile_expert[jnp.minimum(tile_end[tile_expert], n_used - 1)]
    pos0, pos1 = slot_of(0), slot_of(1)
    slot_ids = jnp.arange(n_tiles * tm_exp, dtype=jnp.int32)
    spare = 2 * t + ((slot_ids // tm_exp) % 2) * tm_exp + slot_ids % tm_exp
    n_used1 = n_used.reshape(1).astype(jnp.int32)
    xs, dst = _dispatch(pos0, pos1, tile_end.astype(jnp.int32), n_used1, h2, spare, n_tiles,
                        tm_exp)

    ew = w_gate.shape
    wg = w_gate.reshape(N_EXPERTS, ew[-2], ew[-1])
    wu = w_up.reshape(N_EXPERTS, ew[-2], ew[-1])
    wd = w_down.reshape(N_EXPERTS, ew[-1], ew[-2])
    y2 = _experts(tile_expert, n_used1, parity.astype(jnp.int32), next_expert.astype(jnp.int32),
                  dst.reshape(n_tiles, 1, tm_exp), xs, wg, wu, wd,
                  2 * t + 3 * tm_exp, tm_exp, 2 * t + 2 * tm_exp)
    return _moe_final(info, x1, mod, g_post, y2, seq, tm_fin)


def _permute_w_in(w_in):
    sizes = [ATT_W] + [KV_W] * 6 + [3 * N_ATT_HEADS] + [ATT_W] * 4
    cuts = np.cumsum(sizes)[:-1]
    q, kc, vc, ks, vs, kw, vw, gates, hq, hf, hi, hg = [
        p.astype(BF16) for p in jnp.split(w_in, cuts, axis=1)]
    w_p = jnp.concatenate([q, kc, ks, kw, vc, vs, vw, hq, hf, hi, hg], axis=1)
    w_g = jnp.pad(gates, ((0, 0), (0, LANES - gates.shape[1])))
    return w_p, w_g


def _block(x, c, positions, w_ada, b_ada, g_pre_mix, g_post_mix, g_pre_ffn, g_post_ffn,
           w_in, w_out, pe_k, w1_k, w2_k, pe_v, w1_v, w2_v, lb_logits, g_norm,
           w_group, b_group, w_router, b_router, w_gate, w_up, w_down, cfg):
    batch, seq, d = x.shape
    t = batch * seq
    x2 = x.reshape(t, d)

    c8 = jnp.zeros((8, d), F32).at[:batch].set(c)
    mod = _ada_mod(c8, w_ada, b_ada)
    cos_t, sin_t = _rope_tables(positions, cfg["tm_rope"])
    w_p, w_g = _permute_w_in(w_in)
    proj, gates, hbk, hbv = _inproj(x2, mod, g_pre_mix.reshape(1, d), w_p, w_g, cos_t, sin_t,
                                    seq, cfg["tm_in"])

    half_blocks = (batch * N_KV_GROUPS, seq // CMP_STRIDE, CMP_STRIDE * HEAD_DIM)
    kc = _compress(hbk.reshape(half_blocks), w1_k, pe_k, w2_k)
    vc = _compress(hbv.reshape(half_blocks), w1_v, pe_v, w2_v)
    ocmp, selb = _cmp_attn(proj, kc, vc, gates, batch, seq, cfg["tq_cmp"])
    o_att = _selwin_attn(proj, selb, ocmp, gates, batch, seq, cfg["tq"], cfg["tk"])
    o_rec = _hgrn(proj, lb_logits, g_norm.reshape(1, -1), batch, seq, cfg["ts_hgrn"],
                  cfg["hb_hgrn"])

    wr = jnp.concatenate([w_router, w_group], axis=1)
    wr = jnp.pad(wr, ((0, 0), (0, LANES - wr.shape[1])))
    br = jnp.pad(jnp.concatenate([b_router, b_group]), (0, LANES - N_EXPERTS - N_EXPERT_GROUPS))
    x1, h2, logits = _outproj(o_att, o_rec, w_out, x2, mod, g_post_mix.reshape(1, d),
                              g_pre_ffn.reshape(1, d), wr, br.reshape(1, LANES), seq,
                              cfg["tm_out"])
    out = _moe(h2, logits, x1, mod, g_post_ffn.reshape(1, d), w_gate, w_up, w_down, seq,
               cfg["tm_route"], cfg["tm_exp"], cfg["tm_fin"])
    return out.reshape(batch, seq, d)


def _config(seq):
    return dict(tm_rope=min(1024, seq), tm_in=min(1024, seq), tq_cmp=min(256, seq),
                tq=min(256, seq), tk=min(512, seq), ts_hgrn=min(512, seq), hb_hgrn=4,
                tm_out=min(256, seq), tm_route=min(1024, seq), tm_exp=256,
                tm_fin=min(256, seq))


def kernel(x, c, positions, w_ada, b_ada, g_pre_mix, g_post_mix, g_pre_ffn, g_post_ffn, w_in, w_out, cmp_pe_k, cmp_w1_k, cmp_w2_k, cmp_pe_v, cmp_w1_v, cmp_w2_v, hgrn_lb_logits, hgrn_g_norm, w_group, b_group, w_router, b_router, w_gate, w_up, w_down):
    assert w_ada.shape[0] == 1, "single-layer block"
    return _block(x, c, positions, w_ada[0], b_ada[0], g_pre_mix[0], g_post_mix[0],
                  g_pre_ffn[0], g_post_ffn[0], w_in[0], w_out[0], cmp_pe_k[0], cmp_w1_k[0],
                  cmp_w2_k[0], cmp_pe_v[0], cmp_w1_v[0], cmp_w2_v[0], hgrn_lb_logits,
                  hgrn_g_norm[0], w_group[0], b_group[0], w_router[0], b_router[0],
                  w_gate[0], w_up[0], w_down[0], _config(x.shape[1]))
```

```python
import functools

import numpy as np
import jax
import jax.numpy as jnp
from jax import lax
from jax.experimental import pallas as pl
from jax.experimental.pallas import tpu as pltpu

F32 = jnp.float32
BF16 = jnp.bfloat16

HEAD_DIM = 128
N_KV_GROUPS = 2
HEADS_PER_GROUP = 4
N_ATT_HEADS = N_KV_GROUPS * HEADS_PER_GROUP
CMP_BLOCK = 32
CMP_STRIDE = 16
SEL_BLOCK = 64
N_SEL = 16
WINDOW = 512
FORCE_BONUS = 1.0e4
ROPE_THETA = 10000.0
N_HGRN_HEADS = 8
HGRN_CHUNK = 64
N_EXPERT_GROUPS = 4
EXPERTS_PER_GROUP = 8
N_EXPERTS = N_EXPERT_GROUPS * EXPERTS_PER_GROUP
EPS = 1e-6

LANES = 128
SUBLANES = 8
BF16_TILE_ROWS = 16
VMEM_LIMIT = 56 * 1024 * 1024

ATT_W = N_ATT_HEADS * HEAD_DIM
KV_W = N_KV_GROUPS * HEAD_DIM
COL_Q = 0
COL_KC = COL_Q + ATT_W
COL_KS = COL_KC + KV_W
COL_KW = COL_KS + KV_W
COL_VC = COL_KW + KV_W
COL_VS = COL_VC + KV_W
COL_VW = COL_VS + KV_W
COL_HQ = COL_VW + KV_W
COL_HF = COL_HQ + ATT_W
COL_HI = COL_HF + ATT_W
COL_HG = COL_HI + ATT_W
PROJ_W = COL_HG + ATT_W
ROPE_W = COL_VC

MASK_BIG = float(2 ** 30)
LOG2E = 1.4426950408889634
Q_PRESCALE = HEAD_DIM ** -0.5 * LOG2E

_NT = (((1,), (1,)), ((), ()))
_TN = (((0,), (0,)), ((), ()))


def _cparams(sem):
    return pltpu.CompilerParams(dimension_semantics=sem, vmem_limit_bytes=VMEM_LIMIT)


def _dot(a, b, **kw):
    return jnp.dot(a, b, preferred_element_type=F32, **kw)


def _dot_nt(a, b):
    return lax.dot_general(a, b, _NT, preferred_element_type=F32)


def _sigmoid(x):
    return 1.0 / (1.0 + jnp.exp(-x))


def _silu(x):
    return x * _sigmoid(x)


def _rms(x, g):
    return x * lax.rsqrt(jnp.mean(x * x, axis=-1, keepdims=True) + EPS) * g


def _mod_row(mod_ref, b, k):
    d = mod_ref.shape[1] // 6
    return mod_ref[pl.ds(b, 1), k * d:(k + 1) * d]


def _lane_col(x, lane, col):
    return jnp.sum(jnp.where(lane == col, x, 0.0), axis=-1, keepdims=True)


def _ada_kernel(c_ref, w_ref, b_ref, o_ref):
    s = _silu(c_ref[...])
    s_hi = s.astype(BF16)
    s_lo = (s - s_hi.astype(F32)).astype(BF16)
    w = w_ref[...]
    w_hi = w.astype(BF16)
    w_lo = (w - w_hi.astype(F32)).astype(BF16)
    o_ref[...] = _dot(s_hi, w_hi) + _dot(s_lo, w_hi) + _dot(s_hi, w_lo) + b_ref[...]


def _ada_mod(c8, w_ada, b_ada, tn=512):
    rows, d = c8.shape
    n = w_ada.shape[1]
    return pl.pallas_call(
        _ada_kernel,
        grid=(n // tn,),
        in_specs=[pl.BlockSpec((rows, d), lambda j: (0, 0)),
                  pl.BlockSpec((d, tn), lambda j: (0, j)),
                  pl.BlockSpec((1, tn), lambda j: (0, j))],
        out_specs=pl.BlockSpec((rows, tn), lambda j: (0, j)),
        out_shape=jax.ShapeDtypeStruct((rows, n), F32),
        compiler_params=_cparams(("parallel",)),
        name="ada_mod",
    )(c8, w_ada, b_ada.reshape(1, n))


def _rope_kernel(ang_ref, cos_ref, sin_ref):
    a = ang_ref[...]
    n = a.shape[0]
    half = HEAD_DIM // 2
    lane = lax.broadcasted_iota(jnp.int32, a.shape, 1)
    first = lane < half
    c = jnp.cos(a)
    s = jnp.sin(a)
    c_sw = pltpu.roll(c, half, axis=1)
    s_sw = pltpu.roll(s, half, axis=1)
    cos_ref[pl.ds(0, n, stride=2), :] = jnp.where(first, c, c_sw)
    cos_ref[pl.ds(1, n, stride=2), :] = jnp.where(first, c_sw, c)
    sin_ref[pl.ds(0, n, stride=2), :] = jnp.where(first, -s, s_sw)
    sin_ref[pl.ds(1, n, stride=2), :] = jnp.where(first, -s_sw, s)


def _rope_tables(positions, tm):
    t = positions.size
    inv_freq = ROPE_THETA ** (-jnp.arange(0, HEAD_DIM, 2, dtype=F32) / HEAD_DIM)
    ang = (positions.reshape(t // 2, 2, 1).astype(F32) * inv_freq[None, None, :]).reshape(
        t // 2, HEAD_DIM)
    spec = pl.BlockSpec((tm, HEAD_DIM), lambda i: (i, 0))
    return pl.pallas_call(
        _rope_kernel,
        grid=(t // tm,),
        in_specs=[pl.BlockSpec((tm // 2, HEAD_DIM), lambda i: (i, 0))],
        out_specs=[spec, spec],
        out_shape=[jax.ShapeDtypeStruct((t, HEAD_DIM), F32)] * 2,
        compiler_params=_cparams(("parallel",)),
        name="rope_tables",
    )(ang)


def _inproj_kernel(x_ref, mod_ref, g_ref, w_ref, wg_ref, cos_ref, sin_ref,
                   proj_ref, gates_ref, hbk_ref, hbv_ref, h_s, stage, *, tn, per_b):
    j = pl.program_id(1)

    @pl.when(j == 0)
    def _():
        b = pl.program_id(0) // per_b
        h = (_rms(x_ref[...], g_ref[...]) * (1.0 + _mod_row(mod_ref, b, 1))
             + _mod_row(mod_ref, b, 0))
        hb = h.astype(BF16)
        h_s[...] = hb
        gates_ref[...] = _dot(hb, wg_ref[...])

    heads = tn // HEAD_DIM
    rows = x_ref.shape[0]

    def store(jt):
        acc = _dot(h_s[...], w_ref[...])
        for c in range(heads):
            a = acc[:, c * HEAD_DIM:(c + 1) * HEAD_DIM]
            col = None if jt is None else jt * tn + c * HEAD_DIM
            if col is not None and col < ROPE_W:
                a = a * cos_ref[...] + pltpu.roll(a, HEAD_DIM // 2, axis=1) * sin_ref[...]
            if col is not None and col < ATT_W:
                a = a * Q_PRESCALE
            proj_ref[:, c * HEAD_DIM:(c + 1) * HEAD_DIM] = a.astype(BF16)
            for col0, hb_ref in ((COL_KC, hbk_ref), (COL_VC, hbv_ref)):
                if col is not None and col0 <= col < col0 + KV_W:
                    stage[...] = a
                    hb_ref[0, (col - col0) // HEAD_DIM] = jnp.concatenate(
                        [stage[pl.ds(i, rows // CMP_STRIDE, stride=CMP_STRIDE), :]
                         for i in range(CMP_STRIDE)], axis=1).astype(BF16)

    n_special = -(-COL_VS // tn)
    for jt in range(n_special):
        pl.when(j == jt)(functools.partial(store, jt))
    pl.when(j >= n_special)(functools.partial(store, None))


def _inproj(x2, mod, g_pre, w_in_p, w_gates, cos_t, sin_t, seq, tm, tn=512):
    t, d = x2.shape
    per_b = seq // tm
    assert COL_KC % tn + KV_W <= tn and COL_VC % tn + KV_W <= tn
    assert tm % (BF16_TILE_ROWS * CMP_STRIDE) == 0
    hb_spec = pl.BlockSpec((1, N_KV_GROUPS, tm // CMP_STRIDE, CMP_STRIDE * HEAD_DIM),
                           lambda i, j: (i // per_b, 0, i % per_b, 0))
    hb_shape = jax.ShapeDtypeStruct(
        (t // seq, N_KV_GROUPS, seq // CMP_STRIDE, CMP_STRIDE * HEAD_DIM), BF16)
    return pl.pallas_call(
        functools.partial(_inproj_kernel, tn=tn, per_b=per_b),
        grid=(t // tm, PROJ_W // tn),
        in_specs=[pl.BlockSpec((tm, d), lambda i, j: (i, 0)),
                  pl.BlockSpec(mod.shape, lambda i, j: (0, 0)),
                  pl.BlockSpec((1, d), lambda i, j: (0, 0)),
                  pl.BlockSpec((d, tn), lambda i, j: (0, j)),
                  pl.BlockSpec((d, LANES), lambda i, j: (0, 0)),
                  pl.BlockSpec((tm, HEAD_DIM), lambda i, j: (i, 0)),
                  pl.BlockSpec((tm, HEAD_DIM), lambda i, j: (i, 0))],
        out_specs=[pl.BlockSpec((tm, tn), lambda i, j: (i, j)),
                   pl.BlockSpec((tm, LANES), lambda i, j: (i, 0)), hb_spec, hb_spec],
        out_shape=[jax.ShapeDtypeStruct((t, PROJ_W), BF16),
                   jax.ShapeDtypeStruct((t, LANES), F32), hb_shape, hb_shape],
        scratch_shapes=[pltpu.VMEM((tm, d), BF16), pltpu.VMEM((tm, HEAD_DIM), F32)],
        compiler_params=_cparams(("parallel", "arbitrary")),
        name="inproj",
    )(x2, mod, g_pre, w_in_p, w_gates, cos_t, sin_t)


def _compress_kernel(h_ref, w1_ref, pe_ref, w2_ref, o_ref):
    hb = h_ref[0]
    half = hb.shape[1]
    ya = _dot(hb, w1_ref[0:half, :])
    yb = _dot(hb, w1_ref[half:2 * half, :])
    const = _dot(pe_ref[...], w1_ref[...])
    n = ya.shape[0]
    yb_next = pltpu.roll(yb, n - 1, axis=0)
    hid = _silu(ya + yb_next + const[0:1, :])
    out = _dot(hid.astype(BF16), w2_ref[...])
    row = lax.broadcasted_iota(jnp.int32, out.shape, 0)
    o_ref[0] = jnp.where(row < n - 1, out, 0.0).astype(BF16)


def _compress(hblk, w1, pe, w2):
    bg, n, half = hblk.shape
    pe8 = jnp.zeros((8, 2 * half), BF16).at[0].set(pe.reshape(-1).astype(BF16))
    return pl.pallas_call(
        _compress_kernel,
        grid=(bg,),
        in_specs=[pl.BlockSpec((1, n, half), lambda i: (i, 0, 0)),
                  pl.BlockSpec((2 * half, HEAD_DIM), lambda i: (0, 0)),
                  pl.BlockSpec((8, 2 * half), lambda i: (0, 0)),
                  pl.BlockSpec((HEAD_DIM, HEAD_DIM), lambda i: (0, 0))],
        out_specs=pl.BlockSpec((1, n, HEAD_DIM), lambda i: (i, 0, 0)),
        out_shape=jax.ShapeDtypeStruct((bg, n, HEAD_DIM), BF16),
        compiler_params=_cparams(("parallel",)),
        name="compress",
    )(hblk, w1.astype(BF16), pe8, w2.astype(BF16))


def _cmp_kernel(q_ref, kc_ref, vc_ref, gates_ref, ov_ref, ocmp_ref, selb_ref, imp_s, cnt_s, *,
                tq, nsel):
    g = pl.program_id(0) % N_KV_GROUPS
    q0 = pl.program_id(1) * tq
    kc = kc_ref[0]
    vc = vc_ref[0]
    nc = kc.shape[0]
    t_idx = q0 + lax.broadcasted_iota(jnp.int32, (tq, nc), 0)
    c_idx = lax.broadcasted_iota(jnp.int32, (tq, nc), 1)
    c_ok = c_idx * CMP_STRIDE + (CMP_BLOCK - 1) <= t_idx
    gl = gates_ref[...]
    lane = lax.broadcasted_iota(jnp.int32, (tq, LANES), 1)
    psum = jnp.zeros((tq, nc), F32)
    for h in range(HEADS_PER_GROUP):
        qh = q_ref[:, h * HEAD_DIM:(h + 1) * HEAD_DIM]
        s = jnp.where(c_ok, _dot_nt(qh, kc), -jnp.inf)
        m = jnp.max(s, axis=-1, keepdims=True)
        m = jnp.where(m == -jnp.inf, 0.0, m)
        p = jnp.exp2(s - m)
        p = p * (1.0 / jnp.maximum(jnp.sum(p, axis=-1, keepdims=True), 1e-30))
        psum = psum + p
        o = _dot(p.astype(BF16), vc)
        glog = _lane_col(gl, lane, (g * HEADS_PER_GROUP + h) * 3)
        ocmp_ref[:, h * HEAD_DIM:(h + 1) * HEAD_DIM] = (o * _sigmoid(glog)).astype(BF16)

    imp = _dot(psum, ov_ref[...], precision=lax.Precision.HIGHEST)
    cur = (q0 + lax.broadcasted_iota(jnp.int32, (tq, LANES), 0)) // SEL_BLOCK
    forced = (lane == 0) | (lane == cur) | (lane == cur - 1)
    imp = jnp.where(lane <= cur, imp + jnp.where(forced, FORCE_BONUS, 0.0), -jnp.inf)
    imp_s[...] = imp.T[0:nsel, :]
    cnt_s[...] = jnp.zeros(cnt_s.shape, F32)
    sub = SUBLANES
    n_groups = nsel // sub
    row = lax.broadcasted_iota(jnp.int32, (sub, tq), 0)

    def count_against(kg):
        groups = [imp_s[g * sub:(g + 1) * sub, :] for g in range(n_groups)]
        cnts = [cnt_s[g * sub:(g + 1) * sub, :] for g in range(n_groups)]
        for k in range(kg * sub, (kg + 1) * sub):
            vk = imp_s[k:k + 1, :]
            for g, grp in enumerate(groups):
                if k < g * sub:
                    beats = vk >= grp
                elif k >= (g + 1) * sub:
                    beats = vk > grp
                else:
                    beats = (vk > grp) | ((vk == grp) & (row > k - g * sub))
                cnts[g] = cnts[g] + jnp.where(beats, 1.0, 0.0)
        for g in range(n_groups):
            cnt_s[g * sub:(g + 1) * sub, :] = cnts[g]

    last_block = (q0 + tq - 1) // SEL_BLOCK
    for kg in range(n_groups):
        pl.when(kg * sub <= last_block)(functools.partial(count_against, kg))
    imp_t = imp_s[...]
    sel = (cnt_s[...] < float(N_SEL)) & (imp_t > -jnp.inf)
    bias_t = jnp.where(sel, 0.0, -MASK_BIG)
    if nsel < LANES:
        bias_t = jnp.concatenate([bias_t, jnp.zeros((LANES - nsel, tq), F32)], axis=0)
    selb_ref[...] = bias_t.T.astype(BF16)


def _cmp_attn(proj, kc, vc, gates, batch, seq, tq):
    t = proj.shape[0]
    nq = seq // tq
    nc = kc.shape[1]
    nsel = seq // SEL_BLOCK
    assert nsel <= LANES and nsel % 8 == 0
    c_start = np.arange(nc) * CMP_STRIDE
    j_start = np.arange(LANES) * SEL_BLOCK
    ov = ((c_start[:, None] < j_start[None, :] + SEL_BLOCK)
          & (c_start[:, None] + CMP_BLOCK > j_start[None, :])
          & (np.arange(LANES)[None, :] < nsel)).astype(np.float32)
    gq = HEADS_PER_GROUP * HEAD_DIM
    qmap = lambda bg, i: ((bg // N_KV_GROUPS) * nq + i, bg % N_KV_GROUPS)
    return pl.pallas_call(
        functools.partial(_cmp_kernel, tq=tq, nsel=nsel),
        grid=(batch * N_KV_GROUPS, nq),
        in_specs=[pl.BlockSpec((tq, gq), qmap),
                  pl.BlockSpec((1, nc, HEAD_DIM), lambda bg, i: (bg, 0, 0)),
                  pl.BlockSpec((1, nc, HEAD_DIM), lambda bg, i: (bg, 0, 0)),
                  pl.BlockSpec((tq, LANES), lambda bg, i: ((bg // N_KV_GROUPS) * nq + i, 0)),
                  pl.BlockSpec((nc, LANES), lambda bg, i: (0, 0))],
        out_specs=[pl.BlockSpec((tq, gq), qmap),
                   pl.BlockSpec((tq, LANES), lambda bg, i: (bg * nq + i, 0))],
        out_shape=[jax.ShapeDtypeStruct((t, ATT_W), BF16),
                   jax.ShapeDtypeStruct((batch * N_KV_GROUPS * seq, LANES), BF16)],
        scratch_shapes=[pltpu.VMEM((nsel, tq), F32), pltpu.VMEM((nsel, tq), F32)],
        compiler_params=_cparams(("parallel", "parallel")),
        name="cmp_attn",
    )(proj, kc, vc, gates, jnp.asarray(ov))


def _selwin_kernel(q_ref, selb_ref, ocmp_ref, gates_ref, ks_ref, vs_ref, kw_ref, vw_ref,
                   cb_ref, wb_ref, o_ref, kaug_s, vaug_s, vwaug_s, qaug_s, sbuf_s, m_s, acc_s,
                   swin_s, *, tq, tk, seq, nwin):
    g = pl.program_id(0) % N_KV_GROUPS
    qi = pl.program_id(1)
    q0 = qi * tq
    hq = HEADS_PER_GROUP * tq
    half = hq // 2

    @pl.when(qi == 0)
    def _():
        r = lax.broadcasted_iota(jnp.int32, (seq, HEAD_DIM), 0)
        ln = lax.broadcasted_iota(jnp.int32, (seq, HEAD_DIM), 1)
        ones_col = jnp.where(ln == 0, 1.0, 0.0).astype(BF16)
        kaug_s[:, 0:HEAD_DIM] = ks_ref[...]
        kaug_s[:, HEAD_DIM:2 * HEAD_DIM] = jnp.where(r // SEL_BLOCK == ln, 1.0, 0.0).astype(BF16)
        vaug_s[:, 0:HEAD_DIM] = vs_ref[...]
        vaug_s[:, HEAD_DIM:2 * HEAD_DIM] = ones_col
        vwaug_s[:, 0:HEAD_DIM] = vw_ref[...]
        vwaug_s[:, HEAD_DIM:2 * HEAD_DIM] = ones_col

    for h in range(HEADS_PER_GROUP):
        qaug_s[h * tq:(h + 1) * tq, 0:HEAD_DIM] = q_ref[:, h * HEAD_DIM:(h + 1) * HEAD_DIM]
        qaug_s[h * tq:(h + 1) * tq, HEAD_DIM:2 * HEAD_DIM] = selb_ref[...]
    qa = qaug_s[...]

    def scores(kt, slot):
        ks = kaug_s[pl.ds(pl.multiple_of(kt * tk, tk), tk), :]
        sbuf_s[slot] = _dot_nt(qa, ks)

    def absorb(kt, slot, diagonal=False):
        vs = vaug_s[pl.ds(pl.multiple_of(kt * tk, tk), tk), :]
        s = sbuf_s[slot]
        if diagonal:
            bias = cb_ref[(q0 - kt * tk) // tq]
            s = (s.reshape(HEADS_PER_GROUP, tq, tk) + bias[None]).reshape(hq, tk)
        m = m_s[...]
        m_new = jnp.maximum(m, jnp.max(s, axis=-1, keepdims=True))
        m_s[...] = m_new
        pb = jnp.exp2(s - m_new[:, 0:1]).astype(BF16)
        if diagonal:
            pv = jnp.concatenate([_dot(pb[0:half], vs), _dot(pb[half:hq], vs)], axis=0)
        else:
            pv = _dot(pb, vs)
        acc_s[...] = jnp.exp2(m - m_new)[:, 0:1] * acc_s[...] + pv

    n_full = q0 // tk
    m_s[...] = jnp.full(m_s.shape, -jnp.inf, F32)
    acc_s[...] = jnp.zeros(acc_s.shape, F32)
    scores(0, 0)

    def pair(j, c):
        scores(2 * j + 1, 1)
        absorb(2 * j, 0)
        scores(2 * j + 2, 0)
        absorb(2 * j + 1, 1)
        return c

    lax.fori_loop(0, n_full // 2, pair, 0)
    odd = n_full % 2

    @pl.when(odd == 1)
    def _():
        scores(n_full, 1)
        absorb(n_full - 1, 0)

    nq = seq // tq
    w0 = jnp.clip(qi - (nwin - 1), 0, nq - nwin) * tq
    wlen = nwin * tq
    kwin = kw_ref[pl.ds(pl.multiple_of(w0, tq), wlen), :]
    vwin = vwaug_s[pl.ds(pl.multiple_of(w0, tq), wlen), :]
    swin_s[0:half, :] = _dot_nt(qaug_s[0:half, 0:HEAD_DIM], kwin)
    swin_s[half:hq, :] = _dot_nt(qaug_s[half:hq, 0:HEAD_DIM], kwin)
    absorb(n_full, odd, diagonal=True)
    acc = acc_s[...]
    o_sel = acc[:, 0:HEAD_DIM] / acc[:, HEAD_DIM:HEAD_DIM + 1]

    wbias = wb_ref[(q0 - w0) // tq]
    sw = (swin_s[...].reshape(HEADS_PER_GROUP, tq, wlen) + wbias[None]).reshape(hq, wlen)
    pwb = jnp.exp2(sw - jnp.max(sw, axis=-1, keepdims=True)).astype(BF16)
    ow = jnp.concatenate([_dot(pwb[0:half], vwin), _dot(pwb[half:hq], vwin)], axis=0)
    o_win = ow[:, 0:HEAD_DIM] / ow[:, HEAD_DIM:HEAD_DIM + 1]

    gl = gates_ref[...]
    lane = lax.broadcasted_iota(jnp.int32, (tq, LANES), 1)
    for h in range(HEADS_PER_GROUP):
        gcol = (g * HEADS_PER_GROUP + h) * 3
        g_s = _sigmoid(_lane_col(gl, lane, gcol + 1))
        g_w = _sigmoid(_lane_col(gl, lane, gcol + 2))
        o = (ocmp_ref[:, h * HEAD_DIM:(h + 1) * HEAD_DIM].astype(F32)
             + g_s * o_sel[h * tq:(h + 1) * tq, :] + g_w * o_win[h * tq:(h + 1) * tq, :])
        o_ref[:, h * HEAD_DIM:(h + 1) * HEAD_DIM] = o.astype(BF16)


def _selwin_attn(proj, selb, ocmp, gates, batch, seq, tq, tk):
    t = proj.shape[0]
    nq = seq // tq
    assert tk % tq == 0 and tq & (tq - 1) == 0 and seq % tk == 0
    nwin = min(WINDOW // tq + 1, nq)
    assert WINDOW % tq == 0
    gq = HEADS_PER_GROUP * HEAD_DIM
    hq = HEADS_PER_GROUP * tq
    r = np.arange(tq)[:, None]
    cbias = np.stack([np.where(np.arange(tk)[None, :] <= r + off * tq, 0.0, -np.inf)
                      for off in range(tk // tq)]).astype(np.float32)
    dist = [r + off * tq - np.arange(nwin * tq)[None, :] for off in range(nwin)]
    wbias = np.stack([np.where((d >= 0) & (d < WINDOW), 0.0, -np.inf)
                      for d in dist]).astype(np.float32)
    qmap = lambda bg, i: ((bg // N_KV_GROUPS) * nq + i, bg % N_KV_GROUPS)

    def kvspec(col):
        return pl.BlockSpec((seq, HEAD_DIM),
                            lambda bg, i: (bg // N_KV_GROUPS, col // HEAD_DIM + bg % N_KV_GROUPS))

    return pl.pallas_call(
        functools.partial(_selwin_kernel, tq=tq, tk=tk, seq=seq, nwin=nwin),
        grid=(batch * N_KV_GROUPS, nq),
        in_specs=[pl.BlockSpec((tq, gq), qmap),
                  pl.BlockSpec((tq, LANES), lambda bg, i: (bg * nq + i, 0)),
                  pl.BlockSpec((tq, gq), qmap),
                  pl.BlockSpec((tq, LANES), lambda bg, i: ((bg // N_KV_GROUPS) * nq + i, 0)),
                  kvspec(COL_KS), kvspec(COL_VS), kvspec(COL_KW), kvspec(COL_VW),
                  pl.BlockSpec(cbias.shape, lambda bg, i: (0, 0, 0)),
                  pl.BlockSpec(wbias.shape, lambda bg, i: (0, 0, 0))],
        out_specs=pl.BlockSpec((tq, gq), qmap),
        out_shape=jax.ShapeDtypeStruct((t, ATT_W), BF16),
        scratch_shapes=[pltpu.VMEM((seq, 2 * HEAD_DIM), BF16),
                        pltpu.VMEM((seq, 2 * HEAD_DIM), BF16),
                        pltpu.VMEM((seq, 2 * HEAD_DIM), BF16),
                        pltpu.VMEM((hq, 2 * HEAD_DIM), BF16),
                        pltpu.VMEM((2, hq, tk), F32),
                        pltpu.VMEM((hq, LANES), F32),
                        pltpu.VMEM((hq, 2 * HEAD_DIM), F32),
                        pltpu.VMEM((hq, nwin * tq), F32)],
        compiler_params=_cparams(("parallel", "arbitrary")),
        name="selwin_attn",
    )(proj, selb, ocmp, gates, proj, proj, proj, proj, jnp.asarray(cbias), jnp.asarray(wbias))


_HGRN_LEVELS = (32, 16, 8, 4, 2, 1)


def _hgrn_tables():
    c = HGRN_CHUNK
    nl = len(_HGRN_LEVELS)
    idx = np.arange(c)
    i, j = idx[:, None], idx[None, :]
    mats = [j <= i, j > i]
    lvl = np.full((c, c), nl + 1, np.int32)
    for li, s in enumerate(_HGRN_LEVELS):
        mats.append((j >= (i // s) * s) & (j <= i))
        mats.append((j > i) & (j <= (i // s) * s + s - 1))
        lvl[((i // (2 * s)) == (j // (2 * s))) & (((i // s) % 2) == 1) & (((j // s) % 2) == 0)] = li
    lvl[i == j] = nl
    w = np.concatenate(mats, 0).astype(np.float32)
    return np.concatenate([w, w], axis=1), lvl


def _hgrn_head(hq_ref, hf_ref, hi_ref, hg_ref, lbl_ref, gn_ref, w_ref, lvl_ref, o_ref, st_s,
               hh, ts):
    c = HGRN_CHUNK
    nc = ts // c
    nl = len(_HGRN_LEVELS)
    cs = slice(hh * HEAD_DIM, (hh + 1) * HEAD_DIM)
    lbl = lbl_ref[:, cs]
    e = jnp.exp(lbl - jnp.max(lbl, axis=0, keepdims=True))
    lb = e[0:1, :] / jnp.sum(e, axis=0, keepdims=True)

    f = lb + (1.0 - lb) * _sigmoid(hf_ref[:, cs].astype(F32))
    k = 1.0 - f
    qf = _silu(hq_ref[:, cs].astype(F32))
    v = hi_ref[:, cs]

    lf = jnp.log(f) * LOG2E
    lf_hi = lf.astype(BF16)
    lf_lo = (lf - lf_hi.astype(F32)).astype(BF16)
    side = lambda a: jnp.concatenate([a[ci * c:(ci + 1) * c, :] for ci in range(nc)], axis=1)
    ex = _dot(w_ref[...], jnp.concatenate([side(lf_hi), side(lf_lo)], axis=0))

    def rows(bi):
        return jnp.concatenate(
            [ex[bi * c:(bi + 1) * c, ci * HEAD_DIM:(ci + 1) * HEAD_DIM] for ci in range(nc)], axis=0)

    b = rows(0)
    qhat = (qf * jnp.exp2(b)).astype(BF16)
    khat = (k * jnp.exp2(rows(1))).astype(BF16)
    lvl = lvl_ref[...]
    masks = [lvl == li for li in range(nl + 1)]
    qs = [(qf * jnp.exp2(rows(2 + 2 * li))).astype(BF16) for li in range(nl)] + [qf.astype(BF16)]
    ks = [(k * jnp.exp2(rows(3 + 2 * li))).astype(BF16) for li in range(nl)] + [k.astype(BF16)]
    o_intra = []
    for ci in range(nc):
        sl = slice(ci * c, (ci + 1) * c)
        a = jnp.zeros((c, c), F32)
        for li in range(nl + 1):
            a = jnp.where(masks[li], _dot_nt(qs[li][sl, :], ks[li][sl, :]), a)
        o_intra.append(_dot(a.astype(BF16), v[sl, :]))

    st = st_s[hh]
    outs = []
    for ci in range(nc):
        sl = slice(ci * c, (ci + 1) * c)
        outs.append(o_intra[ci] + _dot_nt(qhat[sl, :], st.astype(BF16)))
        st = st * jnp.exp2(b[(ci + 1) * c - 1:(ci + 1) * c, :]) + lax.dot_general(
            v[sl, :], khat[sl, :], _TN, preferred_element_type=F32)
    st_s[hh] = st
    o = jnp.concatenate(outs, axis=0)
    o = o * lax.rsqrt(jnp.mean(o * o, axis=-1, keepdims=True) + EPS) * gn_ref[:, cs]
    o_ref[:, cs] = (o * _silu(hg_ref[:, cs].astype(F32))).astype(BF16)


def _hgrn_kernel(hq_ref, hf_ref, hi_ref, hg_ref, lbl_ref, gn_ref, w_ref, lvl_ref,
                 o_ref, st_s, *, ts, hb):
    @pl.when(pl.program_id(2) == 0)
    def _():
        st_s[...] = jnp.zeros_like(st_s)

    for hh in range(hb):
        _hgrn_head(hq_ref, hf_ref, hi_ref, hg_ref, lbl_ref, gn_ref, w_ref, lvl_ref, o_ref, st_s,
                   hh, ts)


def _hgrn(proj, lb_logits, g_norm, batch, seq, ts, hb):
    t = proj.shape[0]
    ns = seq // ts
    wmat, lvl = _hgrn_tables()
    w = hb * HEAD_DIM
    assert all(col % w == 0 for col in (COL_HQ, COL_HF, COL_HI, COL_HG))

    def colspec(col):
        return pl.BlockSpec((ts, w), lambda b, h, s: (b * ns + s, col // w + h))

    nl = lb_logits.shape[0]
    return pl.pallas_call(
        functools.partial(_hgrn_kernel, ts=ts, hb=hb),
        grid=(batch, N_HGRN_HEADS // hb, ns),
        in_specs=[colspec(COL_HQ), colspec(COL_HF), colspec(COL_HI), colspec(COL_HG),
                  pl.BlockSpec((nl, w), lambda b, h, s: (0, h)),
                  pl.BlockSpec((1, w), lambda b, h, s: (0, h)),
                  pl.BlockSpec(wmat.shape, lambda b, h, s: (0, 0)),
                  pl.BlockSpec(lvl.shape, lambda b, h, s: (0, 0))],
        out_specs=pl.BlockSpec((ts, w), lambda b, h, s: (b * ns + s, h)),
        out_shape=jax.ShapeDtypeStruct((t, ATT_W), BF16),
        scratch_shapes=[pltpu.VMEM((hb, HEAD_DIM, HEAD_DIM), F32)],
        compiler_params=_cparams(("parallel", "parallel", "arbitrary")),
        name="hgrn2",
    )(proj, proj, proj, proj, lb_logits, g_norm, jnp.asarray(wmat, BF16), jnp.asarray(lvl))


def _outproj_kernel(oa_ref, or_ref, wa_ref, wb_ref, x_ref, mod_ref, gpost_ref, gpre_ref,
                    wr_ref, br_ref, x1_ref, h2_ref, lg_ref, stage, *, per_b):
    b = pl.program_id(0) // per_b
    y = _dot(oa_ref[...], wa_ref[...]) + _dot(or_ref[...], wb_ref[...])
    x1 = x_ref[...] + _mod_row(mod_ref, b, 2) * _rms(y, gpost_ref[...])
    x1_ref[...] = x1
    h = _rms(x1, gpre_ref[...]) * (1.0 + _mod_row(mod_ref, b, 4)) + _mod_row(mod_ref, b, 3)
    h2_ref[...] = _to_slabs(stage, h)
    h_hi = h.astype(BF16)
    h_lo = (h - h_hi.astype(F32)).astype(BF16)
    hh = _dot(h_hi, wr_ref[...])
    lg_ref[...] = (hh[:, 0:LANES] + hh[:, LANES:2 * LANES]
                   + _dot(h_lo, wr_ref[:, 0:LANES]) + br_ref[...])


def _outproj(o_att, o_rec, w_out, x2, mod, g_post, g_pre, wr, br, seq, tm):
    t, d = x2.shape
    per_b = seq // tm
    wr_hi = wr.astype(BF16)
    wr_cat = jnp.concatenate([wr_hi, (wr - wr_hi.astype(F32)).astype(BF16)], axis=1)
    wa = w_out[:ATT_W].astype(BF16)
    wb = w_out[ATT_W:].astype(BF16)
    row = lambda i: (i, 0)
    fixed = lambda i: (0, 0)
    return pl.pallas_call(
        functools.partial(_outproj_kernel, per_b=per_b),
        grid=(t // tm,),
        in_specs=[pl.BlockSpec((tm, ATT_W), row), pl.BlockSpec((tm, ATT_W), row),
                  pl.BlockSpec((ATT_W, d), fixed), pl.BlockSpec((ATT_W, d), fixed),
                  pl.BlockSpec((tm, d), row),
                  pl.BlockSpec(mod.shape, fixed),
                  pl.BlockSpec((1, d), fixed), pl.BlockSpec((1, d), fixed),
                  pl.BlockSpec((d, 2 * LANES), fixed), pl.BlockSpec((1, LANES), fixed)],
        out_specs=[pl.BlockSpec((tm, d), row), pl.BlockSpec((tm * (d // LANES), LANES), row),
                   pl.BlockSpec((tm, LANES), row)],
        out_shape=[jax.ShapeDtypeStruct((t, d), F32),
                   jax.ShapeDtypeStruct((t * (d // LANES), LANES), BF16),
                   jax.ShapeDtypeStruct((t, LANES), F32)],
        scratch_shapes=[pltpu.VMEM((tm * (d // LANES), LANES), F32)],
        compiler_params=_cparams(("parallel",)),
        name="outproj",
    )(o_att, o_rec, wa, wb, x2, mod, g_post, g_pre, wr_cat, br)


G_LANE0 = N_EXPERTS


def _route_kernel(lg_ref, tri_ref, info_ref, cnt_ref, carry_s):
    @pl.when(pl.program_id(0) == 0)
    def _():
        carry_s[...] = jnp.zeros_like(carry_s)

    x = lg_ref[...]
    lane = lax.broadcasted_iota(jnp.int32, x.shape, 1)
    big = jnp.int32(10 ** 6)
    rmax = lambda a: jnp.max(a, axis=-1, keepdims=True)
    rmin = lambda a: jnp.min(a, axis=-1, keepdims=True)
    rsum = lambda a: jnp.sum(a, axis=-1, keepdims=True)

    is_g = (lane >= G_LANE0) & (lane < G_LANE0 + N_EXPERT_GROUPS)
    lgm = jnp.where(is_g, x, -jnp.inf)
    mg = rmax(lgm)
    pg_sel = 1.0 / rsum(jnp.where(is_g, jnp.exp(lgm - mg), 0.0))
    gsel = rmin(jnp.where(lgm == mg, lane, big)) - G_LANE0

    is_e = (lane >= gsel * EXPERTS_PER_GROUP) & (lane < (gsel + 1) * EXPERTS_PER_GROUP)
    lem = jnp.where(is_e, x, -jnp.inf)
    pe = jnp.where(is_e, jnp.exp(lem - rmax(lem)), 0.0)
    pe = pe / rsum(pe)
    pe = jnp.where(is_e, pe, -1.0)
    v1 = rmax(pe)
    i1 = rmin(jnp.where(pe == v1, lane, big))
    pe2 = jnp.where(lane == i1, -1.0, pe)
    v2 = rmax(pe2)
    i2 = rmin(jnp.where(pe2 == v2, lane, big))
    w1 = v1 / (v1 + v2) * pg_sel
    w2 = v2 / (v1 + v2) * pg_sel

    oh1 = jnp.where(lane == i1, 1.0, 0.0)
    oh2 = jnp.where(lane == i2, 1.0, 0.0)
    both = oh1 + oh2
    before = _dot(tri_ref[...], both.astype(BF16)) + carry_s[0:1, :]
    r1 = rsum(oh1 * before)
    r2 = rsum(oh2 * before)
    carry_s[0:1, :] = carry_s[0:1, :] + jnp.sum(both, axis=0, keepdims=True)
    cnt_ref[...] = carry_s[...]

    cols = (i1.astype(F32), i2.astype(F32), w1, w2, r1, r2)
    info = jnp.zeros(x.shape, F32)
    for ci, col in enumerate(cols):
        info = jnp.where(lane == ci, col, info)
    info_ref[...] = info


def _route(logits, tm):
    t = logits.shape[0]
    tri = jnp.asarray(np.tril(np.ones((tm, tm), np.float32), -1), BF16)
    return pl.pallas_call(
        _route_kernel,
        grid=(t // tm,),
        in_specs=[pl.BlockSpec((tm, LANES), lambda i: (i, 0)),
                  pl.BlockSpec((tm, tm), lambda i: (0, 0))],
        out_specs=[pl.BlockSpec((tm, LANES), lambda i: (i, 0)),
                   pl.BlockSpec((8, LANES), lambda i: (0, 0))],
        out_shape=[jax.ShapeDtypeStruct((t, LANES), F32),
                   jax.ShapeDtypeStruct((8, LANES), F32)],
        scratch_shapes=[pltpu.VMEM((8, LANES), F32)],
        compiler_params=_cparams(("arbitrary",)),
        name="route",
    )(logits, tri)


def _to_slabs(stage, x):
    n, d = x.shape
    slab = d // LANES
    for c in range(slab):
        stage[pl.ds(c, n, stride=slab), :] = x[:, c * LANES:(c + 1) * LANES]
    return stage[...].astype(BF16)


def _from_slabs(stage, slabs, n):
    slab = slabs.shape[0] // n
    stage[...] = slabs.astype(F32)
    return jnp.concatenate([stage[pl.ds(c, n, stride=slab), :] for c in range(slab)], axis=1)


def _expert_kernel(te_ref, nu_ref, par_ref, nxt_ref, dst_ref, xs_ref, wg_hbm, wu_hbm, wd_hbm,
                   y_ref, obuf, stage, ssem, wg_f, wu_f, wd_f, wsem, wg_s, wu_s, wd_s, *,
                   tm, standin_row):
    t = pl.program_id(0)
    nu = nu_ref[0]
    slot = t % 2

    def weights(e, buf):
        return [pltpu.make_async_copy(src.at[e], dst.at[buf], wsem.at[buf])
                for src, dst in ((wg_hbm, wg_f), (wu_hbm, wu_f), (wd_hbm, wd_f))]
    slab = wg_s.shape[0] // LANES
    tile = tm * slab

    def token(ref, i):
        return ref.at[pl.ds(pl.multiple_of(i * slab, slab), slab)]

    def scatter(s):
        for r in range(tm):
            pltpu.make_async_copy(token(obuf, s * tm + r), token(y_ref, dst_ref[0, 0, r]),
                                  ssem.at[s]).start()

    def tile_of(buf, s):
        return buf.at[pl.ds(pl.multiple_of(s * tile, tile), tile)]

    def wait_tile(s):
        pltpu.make_async_copy(tile_of(obuf, s), tile_of(obuf, s), ssem.at[s]).wait()

    @pl.when(t < nu)
    def _():
        @pl.when(t == 0)
        def _():
            obuf[tile:2 * tile, :] = jnp.zeros((tile, LANES), obuf.dtype)

            def fill(row0, s):
                return pltpu.make_async_copy(tile_of(obuf, 1),
                                             y_ref.at[pl.ds(row0 * slab, tile)], ssem.at[s])

            fill(standin_row - 2 * tm, 0).start()
            fill(standin_row - tm, 0).start()
            fill(standin_row - 2 * tm, 0).wait()
            fill(standin_row - tm, 0).wait()
            fill(standin_row, 1).start()

        par = par_ref[t]

        @pl.when(t == 0)
        def _():
            for cp in weights(te_ref[0], par):
                cp.start()

        @pl.when((t == 0) | (te_ref[t] != te_ref[jnp.maximum(t - 1, 0)]))
        def _():
            for cp in weights(te_ref[t], par):
                cp.wait()
            wg_s[...] = wg_f[par].astype(BF16)
            wu_s[...] = wu_f[par].astype(BF16)
            wd_s[...] = wd_f[par].astype(BF16)

            @pl.when(nxt_ref[t] != te_ref[t])
            def _():
                for cp in weights(nxt_ref[t], 1 - par):
                    cp.start()

        x = _from_slabs(stage, xs_ref[...], tm).astype(BF16)
        hid = (_silu(_dot(x, wg_s[...])) * _dot(x, wu_s[...])).astype(BF16)
        row0 = pl.multiple_of(slot * tile, tile)
        obuf[pl.ds(row0, tile), :] = _to_slabs(stage, _dot(hid, wd_s[...]))
        scatter(slot)
        wait_tile(1 - slot)

        @pl.when(t == nu - 1)
        def _():
            wait_tile(slot)


def _experts(tile_expert, n_used, parity, next_expert, dst3, xs, w_gate, w_up, w_down, y_rows,
             tm, standin_row):
    n_tiles = dst3.shape[0]
    d, f = w_gate.shape[1:]
    slab = d // LANES
    hbm = pl.BlockSpec(memory_space=pl.ANY)
    return pl.pallas_call(
        functools.partial(_expert_kernel, tm=tm, standin_row=standin_row),
        grid_spec=pltpu.PrefetchScalarGridSpec(
            num_scalar_prefetch=4,
            grid=(n_tiles,),
            in_specs=[pl.BlockSpec((1, 1, tm), lambda t, te, nu, pa, nx: (t, 0, 0),
                                   memory_space=pltpu.SMEM),
                      pl.BlockSpec((tm * slab, LANES),
                                   lambda t, te, nu, pa, nx: (jnp.minimum(t, nu[0] - 1), 0)),
                      hbm, hbm, hbm],
            out_specs=pl.BlockSpec(memory_space=pl.ANY),
            scratch_shapes=[pltpu.VMEM((2 * tm * slab, LANES), BF16),
                            pltpu.VMEM((tm * slab, LANES), F32),
                            pltpu.SemaphoreType.DMA((2,)),
                            pltpu.VMEM((2, d, f), F32), pltpu.VMEM((2, d, f), F32),
                            pltpu.VMEM((2, f, d), F32),
                            pltpu.SemaphoreType.DMA((2,)),
                            pltpu.VMEM((d, f), BF16), pltpu.VMEM((d, f), BF16),
                            pltpu.VMEM((f, d), BF16)]),
        out_shape=jax.ShapeDtypeStruct((y_rows * slab, LANES), BF16),
        compiler_params=_cparams(("arbitrary",)),
        name="moe_experts",
    )(tile_expert, n_used, parity, next_expert, dst3, xs, w_gate, w_up, w_down)


def _dispatch_kernel(pos0_ref, pos1_ref, end_ref, nu_ref, h_ref, dst_in, xs_ref, dst_ref,
                     zbuf, zsem, sem, *, tm, slab, n_tiles, n_tokens):
    tile = tm * slab
    base = pl.program_id(0) * tm

    @pl.when(pl.program_id(0) == 0)
    def _():
        init = pltpu.make_async_copy(dst_in, dst_ref, zsem)
        init.start()
        init.wait()
        zbuf[...] = jnp.zeros(zbuf.shape, zbuf.dtype)

        def zero(row_end, go):
            cp = pltpu.make_async_copy(
                zbuf, xs_ref.at[pl.ds(pl.multiple_of((row_end - tm) * slab, slab), tile)], zsem)
            pl.when(go)(cp.start)
            return cp, go

        pending = [zero(end_ref[e], end_ref[e] > (end_ref[e - 1] if e else 0))
                   for e in range(N_EXPERTS)]
        pending += [zero((i + 1) * tm, i >= nu_ref[0]) for i in range(n_tiles)]
        for cp, go in pending:
            pl.when(go)(cp.wait)

    def token(ref, i):
        return ref.at[pl.ds(pl.multiple_of(i * slab, slab), slab)]

    def start(r, c):
        p0 = pos0_ref[0, 0, r]
        p1 = pos1_ref[0, 0, r]
        pltpu.make_async_copy(token(h_ref, r), token(xs_ref, p0), sem).start()
        pltpu.make_async_copy(token(h_ref, r), token(xs_ref, p1), sem).start()
        dst_ref[p0] = base + r
        dst_ref[p1] = n_tokens + base + r
        return c

    lax.fori_loop(0, tm, start, 0, unroll=8)
    for _ in range(2):
        pltpu.make_async_copy(h_ref, h_ref, sem).wait()


def _dispatch(pos0, pos1, tile_end, n_used, h2, dst0, n_tiles, tm):
    rows = h2.shape[0]
    slab = rows // pos0.shape[0]
    t = pos0.shape[0]
    per_tile = pl.BlockSpec((1, 1, tm), lambda i: (i, 0, 0), memory_space=pltpu.SMEM)
    whole = pl.BlockSpec(memory_space=pltpu.SMEM)
    return pl.pallas_call(
        functools.partial(_dispatch_kernel, tm=tm, slab=slab, n_tiles=n_tiles, n_tokens=t),
        grid=(t // tm,),
        in_specs=[per_tile, per_tile, whole, whole,
                  pl.BlockSpec((tm * slab, LANES), lambda i: (i, 0)),
                  pl.BlockSpec(memory_space=pl.ANY)],
        out_specs=[pl.BlockSpec(memory_space=pl.ANY), whole],
        out_shape=[jax.ShapeDtypeStruct((n_tiles * tm * slab, LANES), h2.dtype),
                   jax.ShapeDtypeStruct(dst0.shape, jnp.int32)],
        scratch_shapes=[pltpu.VMEM((tm * slab, LANES), h2.dtype),
                        pltpu.SemaphoreType.DMA(()), pltpu.SemaphoreType.DMA(())],
        compiler_params=_cparams(("arbitrary",)),
        name="moe_dispatch",
    )(pos0.reshape(t // tm, 1, tm), pos1.reshape(t // tm, 1, tm), tile_end * tm, n_used, h2,
      dst0)


def _final_kernel(info_ref, x1_ref, mod_ref, g_ref, ya_ref, yb_ref, o_ref, stage_a, stage_b, *,
                  per_b):
    info = info_ref[...]
    tm = x1_ref.shape[0]
    ya = _from_slabs(stage_a, ya_ref[...], tm)
    yb = _from_slabs(stage_b, yb_ref[...], tm)
    y = info[:, 2:3] * ya + info[:, 3:4] * yb
    gain = _mod_row(mod_ref, pl.program_id(0) // per_b, 5) * g_ref[...]
    inv = lax.rsqrt(jnp.mean(y * y, axis=-1, keepdims=True) + EPS)
    o_ref[...] = x1_ref[...] + (y * inv) * gain


def _moe_final(info, x1, mod, g_post, y2, seq, tm):
    t, d = x1.shape
    per_b = seq // tm
    nb = t // tm
    return pl.pallas_call(
        functools.partial(_final_kernel, per_b=per_b),
        grid=(nb,),
        in_specs=[pl.BlockSpec((tm, LANES), lambda i: (i, 0)),
                  pl.BlockSpec((tm, d), lambda i: (i, 0)),
                  pl.BlockSpec(mod.shape, lambda i: (0, 0)),
                  pl.BlockSpec((1, d), lambda i: (0, 0)),
                  pl.BlockSpec((tm * (d // LANES), LANES), lambda i: (i, 0)),
                  pl.BlockSpec((tm * (d // LANES), LANES), lambda i: (nb + i, 0))],
        out_specs=pl.BlockSpec((tm, d), lambda i: (i, 0)),
        out_shape=jax.ShapeDtypeStruct((t, d), F32),
        scratch_shapes=[pltpu.VMEM((tm * (d // LANES), LANES), F32)] * 2,
        compiler_params=_cparams(("parallel",)),
        name="moe_final",
    )(info, x1, mod, g_post, y2, y2)


def _moe(h2, logits, x1, mod, g_post, w_gate, w_up, w_down, seq, tm_route, tm_exp, tm_fin):
    t = x1.shape[0]
    info, cnt = _route(logits, tm_route)
    counts = cnt[0, :N_EXPERTS].astype(jnp.int32)
    tiles_e = (counts + tm_exp - 1) // tm_exp
    tile_end = jnp.cumsum(tiles_e)
    offs = (tile_end - tiles_e) * tm_exp
    n_used = tile_end[-1]

    def slot_of(k):
        e = info[:, k].astype(jnp.int32)
        start = jnp.sum(jnp.where(e[:, None] == jnp.arange(N_EXPERTS)[None, :], offs[None, :], 0),
                        axis=1)
        return start + info[:, 4 + k].astype(jnp.int32)

    n_tiles = (2 * t) // tm_exp + N_EXPERTS
    tidx = jnp.minimum(jnp.arange(n_tiles, dtype=jnp.int32), n_used - 1)
    tile_expert = jnp.sum(tidx[:, None] >= tile_end[None, :], axis=1).astype(jnp.int32)
    switches = jnp.concatenate([jnp.zeros((1,), jnp.int32),
                                (tile_expert[1:] != tile_expert[:-1]).astype(jnp.int32)])
    parity = jnp.cumsum(switches) % 2
    next_expert = tile_expert[jnp.minimum(tile_end[tile_expert], n_used - 1)]
    pos0, pos1 = slot_of(0), slot_of(1)
    slot_ids = jnp.arange(n_tiles * tm_exp, dtype=jnp.int32)
    spare = 2 * t + ((slot_ids // tm_exp) % 2) * tm_exp + slot_ids % tm_exp
    n_used1 = n_used.reshape(1).astype(jnp.int32)
    xs, dst = _dispatch(pos0, pos1, tile_end.astype(jnp.int32), n_used1, h2, spare, n_tiles,
                        tm_exp)

    ew = w_gate.shape
    wg = w_gate.reshape(N_EXPERTS, ew[-2], ew[-1])
    wu = w_up.reshape(N_EXPERTS, ew[-2], ew[-1])
    wd = w_down.reshape(N_EXPERTS, ew[-1], ew[-2])
    y2 = _experts(tile_expert, n_used1, parity.astype(jnp.int32), next_expert.astype(jnp.int32),
                  dst.reshape(n_tiles, 1, tm_exp), xs, wg, wu, wd,
                  2 * t + 3 * tm_exp, tm_exp, 2 * t + 2 * tm_exp)
    return _moe_final(info, x1, mod, g_post, y2, seq, tm_fin)


def _permute_w_in(w_in):
    sizes = [ATT_W] + [KV_W] * 6 + [3 * N_ATT_HEADS] + [ATT_W] * 4
    cuts = np.cumsum(sizes)[:-1]
    q, kc, vc, ks, vs, kw, vw, gates, hq, hf, hi, hg = [
        p.astype(BF16) for p in jnp.split(w_in, cuts, axis=1)]
    w_p = jnp.concatenate([q, kc, ks, kw, vc, vs, vw, hq, hf, hi, hg], axis=1)
    w_g = jnp.pad(gates, ((0, 0), (0, LANES - gates.shape[1])))
    return w_p, w_g


def _block(x, c, positions, w_ada, b_ada, g_pre_mix, g_post_mix, g_pre_ffn, g_post_ffn,
           w_in, w_out, pe_k, w1_k, w2_k, pe_v, w1_v, w2_v, lb_logits, g_norm,
           w_group, b_group, w_router, b_router, w_gate, w_up, w_down, cfg):
    batch, seq, d = x.shape
    t = batch * seq
    x2 = x.reshape(t, d)

    c8 = jnp.zeros((8, d), F32).at[:batch].set(c)
    mod = _ada_mod(c8, w_ada, b_ada)
    cos_t, sin_t = _rope_tables(positions, cfg["tm_rope"])
    w_p, w_g = _permute_w_in(w_in)
    proj, gates, hbk, hbv = _inproj(x2, mod, g_pre_mix.reshape(1, d), w_p, w_g, cos_t, sin_t,
                                    seq, cfg["tm_in"])

    half_blocks = (batch * N_KV_GROUPS, seq // CMP_STRIDE, CMP_STRIDE * HEAD_DIM)
    kc = _compress(hbk.reshape(half_blocks), w1_k, pe_k, w2_k)
    vc = _compress(hbv.reshape(half_blocks), w1_v, pe_v, w2_v)
    ocmp, selb = _cmp_attn(proj, kc, vc, gates, batch, seq, cfg["tq_cmp"])
    o_att = _selwin_attn(proj, selb, ocmp, gates, batch, seq, cfg["tq"], cfg["tk"])
    o_rec = _hgrn(proj, lb_logits, g_norm.reshape(1, -1), batch, seq, cfg["ts_hgrn"],
                  cfg["hb_hgrn"])

    wr = jnp.concatenate([w_router, w_group], axis=1)
    wr = jnp.pad(wr, ((0, 0), (0, LANES - wr.shape[1])))
    br = jnp.pad(jnp.concatenate([b_router, b_group]), (0, LANES - N_EXPERTS - N_EXPERT_GROUPS))
    x1, h2, logits = _outproj(o_att, o_rec, w_out, x2, mod, g_post_mix.reshape(1, d),
                              g_pre_ffn.reshape(1, d), wr, br.reshape(1, LANES), seq,
                              cfg["tm_out"])
    out = _moe(h2, logits, x1, mod, g_post_ffn.reshape(1, d), w_gate, w_up, w_down, seq,
               cfg["tm_route"], cfg["tm_exp"], cfg["tm_fin"])
    return out.reshape(batch, seq, d)


def _config(seq):
    return dict(tm_rope=min(1024, seq), tm_in=min(1024, seq), tq_cmp=min(512, seq),
                tq=min(256, seq), tk=min(512, seq), ts_hgrn=min(512, seq), hb_hgrn=4,
                tm_out=min(256, seq), tm_route=min(1024, seq), tm_exp=256,
                tm_fin=min(512, seq))


def kernel(x, c, positions, w_ada, b_ada, g_pre_mix, g_post_mix, g_pre_ffn, g_post_ffn, w_in, w_out, cmp_pe_k, cmp_w1_k, cmp_w2_k, cmp_pe_v, cmp_w1_v, cmp_w2_v, hgrn_lb_logits, hgrn_g_norm, w_group, b_group, w_router, b_router, w_gate, w_up, w_down):
    assert w_ada.shape[0] == 1, "single-layer block"
    return _block(x, c, positions, w_ada[0], b_ada[0], g_pre_mix[0], g_post_mix[0],
                  g_pre_ffn[0], g_post_ffn[0], w_in[0], w_out[0], cmp_pe_k[0], cmp_w1_k[0],
                  cmp_w2_k[0], cmp_pe_v[0], cmp_w1_v[0], cmp_w2_v[0], hgrn_lb_logits,
                  hgrn_g_norm[0], w_group[0], b_group[0], w_router[0], b_router[0],
                  w_gate[0], w_up[0], w_down[0], _config(x.shape[1]))
```

```python
import functools

import numpy as np
import jax
import jax.numpy as jnp
from jax import lax
from jax.experimental import pallas as pl
from jax.experimental.pallas import tpu as pltpu

F32 = jnp.float32
BF16 = jnp.bfloat16

HEAD_DIM = 128
N_KV_GROUPS = 2
HEADS_PER_GROUP = 4
N_ATT_HEADS = N_KV_GROUPS * HEADS_PER_GROUP
CMP_BLOCK = 32
CMP_STRIDE = 16
SEL_BLOCK = 64
N_SEL = 16
WINDOW = 512
FORCE_BONUS = 1.0e4
ROPE_THETA = 10000.0
N_HGRN_HEADS = 8
HGRN_CHUNK = 64
N_EXPERT_GROUPS = 4
EXPERTS_PER_GROUP = 8
N_EXPERTS = N_EXPERT_GROUPS * EXPERTS_PER_GROUP
EPS = 1e-6

LANES = 128
SUBLANES = 8
BF16_TILE_ROWS = 16
VMEM_LIMIT = 56 * 1024 * 1024

ATT_W = N_ATT_HEADS * HEAD_DIM
KV_W = N_KV_GROUPS * HEAD_DIM
COL_Q = 0
COL_KC = COL_Q + ATT_W
COL_KS = COL_KC + KV_W
COL_KW = COL_KS + KV_W
COL_VC = COL_KW + KV_W
COL_VS = COL_VC + KV_W
COL_VW = COL_VS + KV_W
COL_HQ = COL_VW + KV_W
COL_HF = COL_HQ + ATT_W
COL_HI = COL_HF + ATT_W
COL_HG = COL_HI + ATT_W
PROJ_W = COL_HG + ATT_W
ROPE_W = COL_VC

MASK_BIG = float(2 ** 30)
LOG2E = 1.4426950408889634
Q_PRESCALE = HEAD_DIM ** -0.5 * LOG2E

_NT = (((1,), (1,)), ((), ()))
_TN = (((0,), (0,)), ((), ()))


def _cparams(sem):
    return pltpu.CompilerParams(dimension_semantics=sem, vmem_limit_bytes=VMEM_LIMIT)


def _dot(a, b, **kw):
    return jnp.dot(a, b, preferred_element_type=F32, **kw)


def _dot_nt(a, b):
    return lax.dot_general(a, b, _NT, preferred_element_type=F32)


def _sigmoid(x):
    return 1.0 / (1.0 + jnp.exp(-x))


def _silu(x):
    return x * _sigmoid(x)


def _rms(x, g):
    return x * lax.rsqrt(jnp.mean(x * x, axis=-1, keepdims=True) + EPS) * g


def _mod_row(mod_ref, b, k):
    d = mod_ref.shape[1] // 6
    return mod_ref[pl.ds(b, 1), k * d:(k + 1) * d]


def _lane_col(x, lane, col):
    return jnp.sum(jnp.where(lane == col, x, 0.0), axis=-1, keepdims=True)


def _ada_kernel(c_ref, w_ref, b_ref, o_ref):
    s = _silu(c_ref[...])
    s_hi = s.astype(BF16)
    s_lo = (s - s_hi.astype(F32)).astype(BF16)
    w = w_ref[...]
    w_hi = w.astype(BF16)
    w_lo = (w - w_hi.astype(F32)).astype(BF16)
    o_ref[...] = _dot(s_hi, w_hi) + _dot(s_lo, w_hi) + _dot(s_hi, w_lo) + b_ref[...]


def _ada_mod(c8, w_ada, b_ada, tn=512):
    rows, d = c8.shape
    n = w_ada.shape[1]
    return pl.pallas_call(
        _ada_kernel,
        grid=(n // tn,),
        in_specs=[pl.BlockSpec((rows, d), lambda j: (0, 0)),
                  pl.BlockSpec((d, tn), lambda j: (0, j)),
                  pl.BlockSpec((1, tn), lambda j: (0, j))],
        out_specs=pl.BlockSpec((rows, tn), lambda j: (0, j)),
        out_shape=jax.ShapeDtypeStruct((rows, n), F32),
        compiler_params=_cparams(("parallel",)),
        name="ada_mod",
    )(c8, w_ada, b_ada.reshape(1, n))


def _rope_kernel(ang_ref, cos_ref, sin_ref):
    a = ang_ref[...]
    n = a.shape[0]
    half = HEAD_DIM // 2
    lane = lax.broadcasted_iota(jnp.int32, a.shape, 1)
    first = lane < half
    c = jnp.cos(a)
    s = jnp.sin(a)
    c_sw = pltpu.roll(c, half, axis=1)
    s_sw = pltpu.roll(s, half, axis=1)
    cos_ref[pl.ds(0, n, stride=2), :] = jnp.where(first, c, c_sw)
    cos_ref[pl.ds(1, n, stride=2), :] = jnp.where(first, c_sw, c)
    sin_ref[pl.ds(0, n, stride=2), :] = jnp.where(first, -s, s_sw)
    sin_ref[pl.ds(1, n, stride=2), :] = jnp.where(first, -s_sw, s)


def _rope_tables(positions, tm):
    t = positions.size
    inv_freq = ROPE_THETA ** (-jnp.arange(0, HEAD_DIM, 2, dtype=F32) / HEAD_DIM)
    ang = (positions.reshape(t // 2, 2, 1).astype(F32) * inv_freq[None, None, :]).reshape(
        t // 2, HEAD_DIM)
    spec = pl.BlockSpec((tm, HEAD_DIM), lambda i: (i, 0))
    return pl.pallas_call(
        _rope_kernel,
        grid=(t // tm,),
        in_specs=[pl.BlockSpec((tm // 2, HEAD_DIM), lambda i: (i, 0))],
        out_specs=[spec, spec],
        out_shape=[jax.ShapeDtypeStruct((t, HEAD_DIM), F32)] * 2,
        compiler_params=_cparams(("parallel",)),
        name="rope_tables",
    )(ang)


def _inproj_kernel(x_ref, mod_ref, g_ref, w_ref, wg_ref, cos_ref, sin_ref,
                   proj_ref, gates_ref, hbk_ref, hbv_ref, h_s, stage, *, tn, per_b):
    j = pl.program_id(1)

    @pl.when(j == 0)
    def _():
        b = pl.program_id(0) // per_b
        h = (_rms(x_ref[...], g_ref[...]) * (1.0 + _mod_row(mod_ref, b, 1))
             + _mod_row(mod_ref, b, 0))
        hb = h.astype(BF16)
        h_s[...] = hb
        gates_ref[...] = _dot(hb, wg_ref[...])

    heads = tn // HEAD_DIM
    rows = x_ref.shape[0]

    def store(jt):
        acc = _dot(h_s[...], w_ref[...])
        for c in range(heads):
            a = acc[:, c * HEAD_DIM:(c + 1) * HEAD_DIM]
            col = None if jt is None else jt * tn + c * HEAD_DIM
            if col is not None and col < ROPE_W:
                a = a * cos_ref[...] + pltpu.roll(a, HEAD_DIM // 2, axis=1) * sin_ref[...]
            if col is not None and col < ATT_W:
                a = a * Q_PRESCALE
            proj_ref[:, c * HEAD_DIM:(c + 1) * HEAD_DIM] = a.astype(BF16)
            for col0, hb_ref in ((COL_KC, hbk_ref), (COL_VC, hbv_ref)):
                if col is not None and col0 <= col < col0 + KV_W:
                    stage[...] = a
                    hb_ref[0, (col - col0) // HEAD_DIM] = jnp.concatenate(
                        [stage[pl.ds(i, rows // CMP_STRIDE, stride=CMP_STRIDE), :]
                         for i in range(CMP_STRIDE)], axis=1).astype(BF16)

    n_special = -(-COL_VS // tn)
    for jt in range(n_special):
        pl.when(j == jt)(functools.partial(store, jt))
    pl.when(j >= n_special)(functools.partial(store, None))


def _inproj(x2, mod, g_pre, w_in_p, w_gates, cos_t, sin_t, seq, tm, tn=512):
    t, d = x2.shape
    per_b = seq // tm
    assert COL_KC % tn + KV_W <= tn and COL_VC % tn + KV_W <= tn
    assert tm % (BF16_TILE_ROWS * CMP_STRIDE) == 0
    hb_spec = pl.BlockSpec((1, N_KV_GROUPS, tm // CMP_STRIDE, CMP_STRIDE * HEAD_DIM),
                           lambda i, j: (i // per_b, 0, i % per_b, 0))
    hb_shape = jax.ShapeDtypeStruct(
        (t // seq, N_KV_GROUPS, seq // CMP_STRIDE, CMP_STRIDE * HEAD_DIM), BF16)
    return pl.pallas_call(
        functools.partial(_inproj_kernel, tn=tn, per_b=per_b),
        grid=(t // tm, PROJ_W // tn),
        in_specs=[pl.BlockSpec((tm, d), lambda i, j: (i, 0)),
                  pl.BlockSpec(mod.shape, lambda i, j: (0, 0)),
                  pl.BlockSpec((1, d), lambda i, j: (0, 0)),
                  pl.BlockSpec((d, tn), lambda i, j: (0, j)),
                  pl.BlockSpec((d, LANES), lambda i, j: (0, 0)),
                  pl.BlockSpec((tm, HEAD_DIM), lambda i, j: (i, 0)),
                  pl.BlockSpec((tm, HEAD_DIM), lambda i, j: (i, 0))],
        out_specs=[pl.BlockSpec((tm, tn), lambda i, j: (i, j)),
                   pl.BlockSpec((tm, LANES), lambda i, j: (i, 0)), hb_spec, hb_spec],
        out_shape=[jax.ShapeDtypeStruct((t, PROJ_W), BF16),
                   jax.ShapeDtypeStruct((t, LANES), F32), hb_shape, hb_shape],
        scratch_shapes=[pltpu.VMEM((tm, d), BF16), pltpu.VMEM((tm, HEAD_DIM), F32)],
        compiler_params=_cparams(("parallel", "arbitrary")),
        name="inproj",
    )(x2, mod, g_pre, w_in_p, w_gates, cos_t, sin_t)


def _compress_kernel(h_ref, w1_ref, pe_ref, w2_ref, o_ref):
    hb = h_ref[0]
    half = hb.shape[1]
    ya = _dot(hb, w1_ref[0:half, :])
    yb = _dot(hb, w1_ref[half:2 * half, :])
    const = _dot(pe_ref[...], w1_ref[...])
    n = ya.shape[0]
    yb_next = pltpu.roll(yb, n - 1, axis=0)
    hid = _silu(ya + yb_next + const[0:1, :])
    out = _dot(hid.astype(BF16), w2_ref[...])
    row = lax.broadcasted_iota(jnp.int32, out.shape, 0)
    o_ref[0] = jnp.where(row < n - 1, out, 0.0).astype(BF16)


def _compress(hblk, w1, pe, w2):
    bg, n, half = hblk.shape
    pe8 = jnp.zeros((8, 2 * half), BF16).at[0].set(pe.reshape(-1).astype(BF16))
    return pl.pallas_call(
        _compress_kernel,
        grid=(bg,),
        in_specs=[pl.BlockSpec((1, n, half), lambda i: (i, 0, 0)),
                  pl.BlockSpec((2 * half, HEAD_DIM), lambda i: (0, 0)),
                  pl.BlockSpec((8, 2 * half), lambda i: (0, 0)),
                  pl.BlockSpec((HEAD_DIM, HEAD_DIM), lambda i: (0, 0))],
        out_specs=pl.BlockSpec((1, n, HEAD_DIM), lambda i: (i, 0, 0)),
        out_shape=jax.ShapeDtypeStruct((bg, n, HEAD_DIM), BF16),
        compiler_params=_cparams(("parallel",)),
        name="compress",
    )(hblk, w1.astype(BF16), pe8, w2.astype(BF16))


def _cmp_kernel(q_ref, kc_ref, vc_ref, gates_ref, ov_ref, ocmp_ref, selb_ref, imp_s, cnt_s, *,
                tq, nsel):
    g = pl.program_id(0) % N_KV_GROUPS
    q0 = pl.program_id(1) * tq
    kc = kc_ref[0]
    vc = vc_ref[0]
    nc = kc.shape[0]
    t_idx = q0 + lax.broadcasted_iota(jnp.int32, (tq, nc), 0)
    c_idx = lax.broadcasted_iota(jnp.int32, (tq, nc), 1)
    c_ok = c_idx * CMP_STRIDE + (CMP_BLOCK - 1) <= t_idx
    gl = gates_ref[...]
    lane = lax.broadcasted_iota(jnp.int32, (tq, LANES), 1)
    psum = jnp.zeros((tq, nc), F32)
    for h in range(HEADS_PER_GROUP):
        qh = q_ref[:, h * HEAD_DIM:(h + 1) * HEAD_DIM]
        s = jnp.where(c_ok, _dot_nt(qh, kc), -jnp.inf)
        m = jnp.max(s, axis=-1, keepdims=True)
        m = jnp.where(m == -jnp.inf, 0.0, m)
        p = jnp.exp2(s - m)
        p = p * (1.0 / jnp.maximum(jnp.sum(p, axis=-1, keepdims=True), 1e-30))
        psum = psum + p
        o = _dot(p.astype(BF16), vc)
        glog = _lane_col(gl, lane, (g * HEADS_PER_GROUP + h) * 3)
        ocmp_ref[:, h * HEAD_DIM:(h + 1) * HEAD_DIM] = (o * _sigmoid(glog)).astype(BF16)

    imp = _dot(psum, ov_ref[...], precision=lax.Precision.HIGHEST)
    cur = (q0 + lax.broadcasted_iota(jnp.int32, (tq, LANES), 0)) // SEL_BLOCK
    forced = (lane == 0) | (lane == cur) | (lane == cur - 1)
    imp = jnp.where(lane <= cur, imp + jnp.where(forced, FORCE_BONUS, 0.0), -jnp.inf)
    imp_s[...] = imp.T[0:nsel, :]
    cnt_s[...] = jnp.zeros(cnt_s.shape, F32)
    sub = SUBLANES
    n_groups = nsel // sub
    row = lax.broadcasted_iota(jnp.int32, (sub, tq), 0)

    def count_against(kg):
        groups = [imp_s[g * sub:(g + 1) * sub, :] for g in range(n_groups)]
        cnts = [cnt_s[g * sub:(g + 1) * sub, :] for g in range(n_groups)]
        for k in range(kg * sub, (kg + 1) * sub):
            vk = imp_s[k:k + 1, :]
            for g, grp in enumerate(groups):
                if k < g * sub:
                    beats = vk >= grp
                elif k >= (g + 1) * sub:
                    beats = vk > grp
                else:
                    beats = (vk > grp) | ((vk == grp) & (row > k - g * sub))
                cnts[g] = cnts[g] + jnp.where(beats, 1.0, 0.0)
        for g in range(n_groups):
            cnt_s[g * sub:(g + 1) * sub, :] = cnts[g]

    last_block = (q0 + tq - 1) // SEL_BLOCK
    for kg in range(n_groups):
        pl.when(kg * sub <= last_block)(functools.partial(count_against, kg))
    imp_t = imp_s[...]
    sel = (cnt_s[...] < float(N_SEL)) & (imp_t > -jnp.inf)
    bias_t = jnp.where(sel, 0.0, -MASK_BIG)
    if nsel < LANES:
        bias_t = jnp.concatenate([bias_t, jnp.zeros((LANES - nsel, tq), F32)], axis=0)
    selb_ref[...] = bias_t.T.astype(BF16)


def _cmp_attn(proj, kc, vc, gates, batch, seq, tq):
    t = proj.shape[0]
    nq = seq // tq
    nc = kc.shape[1]
    nsel = seq // SEL_BLOCK
    assert nsel <= LANES and nsel % 8 == 0
    c_start = np.arange(nc) * CMP_STRIDE
    j_start = np.arange(LANES) * SEL_BLOCK
    ov = ((c_start[:, None] < j_start[None, :] + SEL_BLOCK)
          & (c_start[:, None] + CMP_BLOCK > j_start[None, :])
          & (np.arange(LANES)[None, :] < nsel)).astype(np.float32)
    gq = HEADS_PER_GROUP * HEAD_DIM
    qmap = lambda bg, i: ((bg // N_KV_GROUPS) * nq + i, bg % N_KV_GROUPS)
    return pl.pallas_call(
        functools.partial(_cmp_kernel, tq=tq, nsel=nsel),
        grid=(batch * N_KV_GROUPS, nq),
        in_specs=[pl.BlockSpec((tq, gq), qmap),
                  pl.BlockSpec((1, nc, HEAD_DIM), lambda bg, i: (bg, 0, 0)),
                  pl.BlockSpec((1, nc, HEAD_DIM), lambda bg, i: (bg, 0, 0)),
                  pl.BlockSpec((tq, LANES), lambda bg, i: ((bg // N_KV_GROUPS) * nq + i, 0)),
                  pl.BlockSpec((nc, LANES), lambda bg, i: (0, 0))],
        out_specs=[pl.BlockSpec((tq, gq), qmap),
                   pl.BlockSpec((tq, LANES), lambda bg, i: (bg * nq + i, 0))],
        out_shape=[jax.ShapeDtypeStruct((t, ATT_W), BF16),
                   jax.ShapeDtypeStruct((batch * N_KV_GROUPS * seq, LANES), BF16)],
        scratch_shapes=[pltpu.VMEM((nsel, tq), F32), pltpu.VMEM((nsel, tq), F32)],
        compiler_params=_cparams(("parallel", "parallel")),
        name="cmp_attn",
    )(proj, kc, vc, gates, jnp.asarray(ov))


def _selwin_kernel(q_ref, selb_ref, ocmp_ref, gates_ref, ks_ref, vs_ref, kw_ref, vw_ref,
                   cb_ref, wb_ref, o_ref, kaug_s, vaug_s, vwaug_s, qaug_s, sbuf_s, m_s, acc_s,
                   swin_s, *, tq, tk, seq, nwin):
    g = pl.program_id(0) % N_KV_GROUPS
    qi = pl.program_id(1)
    q0 = qi * tq
    hq = HEADS_PER_GROUP * tq
    half = hq // 2

    @pl.when(qi == 0)
    def _():
        r = lax.broadcasted_iota(jnp.int32, (seq, HEAD_DIM), 0)
        ln = lax.broadcasted_iota(jnp.int32, (seq, HEAD_DIM), 1)
        ones_col = jnp.where(ln == 0, 1.0, 0.0).astype(BF16)
        kaug_s[:, 0:HEAD_DIM] = ks_ref[...]
        kaug_s[:, HEAD_DIM:2 * HEAD_DIM] = jnp.where(r // SEL_BLOCK == ln, 1.0, 0.0).astype(BF16)
        vaug_s[:, 0:HEAD_DIM] = vs_ref[...]
        vaug_s[:, HEAD_DIM:2 * HEAD_DIM] = ones_col
        vwaug_s[:, 0:HEAD_DIM] = vw_ref[...]
        vwaug_s[:, HEAD_DIM:2 * HEAD_DIM] = ones_col

    for h in range(HEADS_PER_GROUP):
        qaug_s[h * tq:(h + 1) * tq, 0:HEAD_DIM] = q_ref[:, h * HEAD_DIM:(h + 1) * HEAD_DIM]
        qaug_s[h * tq:(h + 1) * tq, HEAD_DIM:2 * HEAD_DIM] = selb_ref[...]
    qa = qaug_s[...]

    def scores(kt, slot):
        ks = kaug_s[pl.ds(pl.multiple_of(kt * tk, tk), tk), :]
        sbuf_s[slot] = _dot_nt(qa, ks)

    def absorb(kt, slot, diagonal=False):
        vs = vaug_s[pl.ds(pl.multiple_of(kt * tk, tk), tk), :]
        s = sbuf_s[slot]
        if diagonal:
            bias = cb_ref[(q0 - kt * tk) // tq]
            s = (s.reshape(HEADS_PER_GROUP, tq, tk) + bias[None]).reshape(hq, tk)
        m = m_s[...]
        m_new = jnp.maximum(m, jnp.max(s, axis=-1, keepdims=True))
        m_s[...] = m_new
        pb = jnp.exp2(s - m_new[:, 0:1]).astype(BF16)
        if diagonal:
            pv = jnp.concatenate([_dot(pb[0:half], vs), _dot(pb[half:hq], vs)], axis=0)
        else:
            pv = _dot(pb, vs)
        acc_s[...] = jnp.exp2(m - m_new)[:, 0:1] * acc_s[...] + pv

    n_full = q0 // tk
    m_s[...] = jnp.full(m_s.shape, -jnp.inf, F32)
    acc_s[...] = jnp.zeros(acc_s.shape, F32)
    scores(0, 0)

    def pair(j, c):
        scores(2 * j + 1, 1)
        absorb(2 * j, 0)
        scores(2 * j + 2, 0)
        absorb(2 * j + 1, 1)
        return c

    lax.fori_loop(0, n_full // 2, pair, 0)
    odd = n_full % 2

    @pl.when(odd == 1)
    def _():
        scores(n_full, 1)
        absorb(n_full - 1, 0)

    nq = seq // tq
    w0 = jnp.clip(qi - (nwin - 1), 0, nq - nwin) * tq
    wlen = nwin * tq
    kwin = kw_ref[pl.ds(pl.multiple_of(w0, tq), wlen), :]
    vwin = vwaug_s[pl.ds(pl.multiple_of(w0, tq), wlen), :]
    swin_s[0:half, :] = _dot_nt(qaug_s[0:half, 0:HEAD_DIM], kwin)
    swin_s[half:hq, :] = _dot_nt(qaug_s[half:hq, 0:HEAD_DIM], kwin)
    absorb(n_full, odd, diagonal=True)
    acc = acc_s[...]
    o_sel = acc[:, 0:HEAD_DIM] / acc[:, HEAD_DIM:HEAD_DIM + 1]

    wbias = wb_ref[(q0 - w0) // tq]
    sw = (swin_s[...].reshape(HEADS_PER_GROUP, tq, wlen) + wbias[None]).reshape(hq, wlen)
    pwb = jnp.exp2(sw - jnp.max(sw, axis=-1, keepdims=True)).astype(BF16)
    ow = jnp.concatenate([_dot(pwb[0:half], vwin), _dot(pwb[half:hq], vwin)], axis=0)
    o_win = ow[:, 0:HEAD_DIM] / ow[:, HEAD_DIM:HEAD_DIM + 1]

    gl = gates_ref[...]
    lane = lax.broadcasted_iota(jnp.int32, (tq, LANES), 1)
    for h in range(HEADS_PER_GROUP):
        gcol = (g * HEADS_PER_GROUP + h) * 3
        g_s = _sigmoid(_lane_col(gl, lane, gcol + 1))
        g_w = _sigmoid(_lane_col(gl, lane, gcol + 2))
        o = (ocmp_ref[:, h * HEAD_DIM:(h + 1) * HEAD_DIM].astype(F32)
             + g_s * o_sel[h * tq:(h + 1) * tq, :] + g_w * o_win[h * tq:(h + 1) * tq, :])
        o_ref[:, h * HEAD_DIM:(h + 1) * HEAD_DIM] = o.astype(BF16)


def _selwin_attn(proj, selb, ocmp, gates, batch, seq, tq, tk):
    t = proj.shape[0]
    nq = seq // tq
    assert tk % tq == 0 and tq & (tq - 1) == 0 and seq % tk == 0
    nwin = min(WINDOW // tq + 1, nq)
    assert WINDOW % tq == 0
    gq = HEADS_PER_GROUP * HEAD_DIM
    hq = HEADS_PER_GROUP * tq
    r = np.arange(tq)[:, None]
    cbias = np.stack([np.where(np.arange(tk)[None, :] <= r + off * tq, 0.0, -np.inf)
                      for off in range(tk // tq)]).astype(np.float32)
    dist = [r + off * tq - np.arange(nwin * tq)[None, :] for off in range(nwin)]
    wbias = np.stack([np.where((d >= 0) & (d < WINDOW), 0.0, -np.inf)
                      for d in dist]).astype(np.float32)
    qmap = lambda bg, i: ((bg // N_KV_GROUPS) * nq + i, bg % N_KV_GROUPS)

    def kvspec(col):
        return pl.BlockSpec((seq, HEAD_DIM),
                            lambda bg, i: (bg // N_KV_GROUPS, col // HEAD_DIM + bg % N_KV_GROUPS))

    return pl.pallas_call(
        functools.partial(_selwin_kernel, tq=tq, tk=tk, seq=seq, nwin=nwin),
        grid=(batch * N_KV_GROUPS, nq),
        in_specs=[pl.BlockSpec((tq, gq), qmap),
                  pl.BlockSpec((tq, LANES), lambda bg, i: (bg * nq + i, 0)),
                  pl.BlockSpec((tq, gq), qmap),
                  pl.BlockSpec((tq, LANES), lambda bg, i: ((bg // N_KV_GROUPS) * nq + i, 0)),
                  kvspec(COL_KS), kvspec(COL_VS), kvspec(COL_KW), kvspec(COL_VW),
                  pl.BlockSpec(cbias.shape, lambda bg, i: (0, 0, 0)),
                  pl.BlockSpec(wbias.shape, lambda bg, i: (0, 0, 0))],
        out_specs=pl.BlockSpec((tq, gq), qmap),
        out_shape=jax.ShapeDtypeStruct((t, ATT_W), BF16),
        scratch_shapes=[pltpu.VMEM((seq, 2 * HEAD_DIM), BF16),
                        pltpu.VMEM((seq, 2 * HEAD_DIM), BF16),
                        pltpu.VMEM((seq, 2 * HEAD_DIM), BF16),
                        pltpu.VMEM((hq, 2 * HEAD_DIM), BF16),
                        pltpu.VMEM((2, hq, tk), F32),
                        pltpu.VMEM((hq, LANES), F32),
                        pltpu.VMEM((hq, 2 * HEAD_DIM), F32),
                        pltpu.VMEM((hq, nwin * tq), F32)],
        compiler_params=_cparams(("parallel", "arbitrary")),
        name="selwin_attn",
    )(proj, selb, ocmp, gates, proj, proj, proj, proj, jnp.asarray(cbias), jnp.asarray(wbias))


_HGRN_LEVELS = (32, 16, 8, 4, 2, 1)


def _hgrn_tables():
    c = HGRN_CHUNK
    nl = len(_HGRN_LEVELS)
    idx = np.arange(c)
    i, j = idx[:, None], idx[None, :]
    mats = [j <= i, j > i]
    lvl = np.full((c, c), nl + 1, np.int32)
    for li, s in enumerate(_HGRN_LEVELS):
        mats.append((j >= (i // s) * s) & (j <= i))
        mats.append((j > i) & (j <= (i // s) * s + s - 1))
        lvl[((i // (2 * s)) == (j // (2 * s))) & (((i // s) % 2) == 1) & (((j // s) % 2) == 0)] = li
    lvl[i == j] = nl
    w = np.concatenate(mats, 0).astype(np.float32)
    return np.concatenate([w, w], axis=1), lvl


def _hgrn_head(hq_ref, hf_ref, hi_ref, hg_ref, lbl_ref, gn_ref, w_ref, lvl_ref, o_ref, st_s,
               hh, ts):
    c = HGRN_CHUNK
    nc = ts // c
    nl = len(_HGRN_LEVELS)
    cs = slice(hh * HEAD_DIM, (hh + 1) * HEAD_DIM)
    lbl = lbl_ref[:, cs]
    e = jnp.exp(lbl - jnp.max(lbl, axis=0, keepdims=True))
    lb = e[0:1, :] / jnp.sum(e, axis=0, keepdims=True)

    f = lb + (1.0 - lb) * _sigmoid(hf_ref[:, cs].astype(F32))
    k = 1.0 - f
    qf = _silu(hq_ref[:, cs].astype(F32))
    v = hi_ref[:, cs]

    lf = jnp.log(f) * LOG2E
    lf_hi = lf.astype(BF16)
    lf_lo = (lf - lf_hi.astype(F32)).astype(BF16)
    side = lambda a: jnp.concatenate([a[ci * c:(ci + 1) * c, :] for ci in range(nc)], axis=1)
    ex = _dot(w_ref[...], jnp.concatenate([side(lf_hi), side(lf_lo)], axis=0))

    def rows(bi):
        return jnp.concatenate(
            [ex[bi * c:(bi + 1) * c, ci * HEAD_DIM:(ci + 1) * HEAD_DIM] for ci in range(nc)], axis=0)

    b = rows(0)
    qhat = (qf * jnp.exp2(b)).astype(BF16)
    khat = (k * jnp.exp2(rows(1))).astype(BF16)
    lvl = lvl_ref[...]
    masks = [lvl == li for li in range(nl + 1)]
    qs = [(qf * jnp.exp2(rows(2 + 2 * li))).astype(BF16) for li in range(nl)] + [qf.astype(BF16)]
    ks = [(k * jnp.exp2(rows(3 + 2 * li))).astype(BF16) for li in range(nl)] + [k.astype(BF16)]
    o_intra = []
    for ci in range(nc):
        sl = slice(ci * c, (ci + 1) * c)
        a = jnp.zeros((c, c), F32)
        for li in range(nl + 1):
            a = jnp.where(masks[li], _dot_nt(qs[li][sl, :], ks[li][sl, :]), a)
        o_intra.append(_dot(a.astype(BF16), v[sl, :]))

    st = st_s[hh]
    outs = []
    for ci in range(nc):
        sl = slice(ci * c, (ci + 1) * c)
        outs.append(o_intra[ci] + _dot_nt(qhat[sl, :], st.astype(BF16)))
        st = st * jnp.exp2(b[(ci + 1) * c - 1:(ci + 1) * c, :]) + lax.dot_general(
            v[sl, :], khat[sl, :], _TN, preferred_element_type=F32)
    st_s[hh] = st
    o = jnp.concatenate(outs, axis=0)
    o = o * lax.rsqrt(jnp.mean(o * o, axis=-1, keepdims=True) + EPS) * gn_ref[:, cs]
    o_ref[:, cs] = (o * _silu(hg_ref[:, cs].astype(F32))).astype(BF16)


def _hgrn_kernel(hq_ref, hf_ref, hi_ref, hg_ref, lbl_ref, gn_ref, w_ref, lvl_ref,
                 o_ref, st_s, *, ts, hb):
    @pl.when(pl.program_id(2) == 0)
    def _():
        st_s[...] = jnp.zeros_like(st_s)

    for hh in range(hb):
        _hgrn_head(hq_ref, hf_ref, hi_ref, hg_ref, lbl_ref, gn_ref, w_ref, lvl_ref, o_ref, st_s,
                   hh, ts)


def _hgrn(proj, lb_logits, g_norm, batch, seq, ts, hb):
    t = proj.shape[0]
    ns = seq // ts
    wmat, lvl = _hgrn_tables()
    w = hb * HEAD_DIM
    assert all(col % w == 0 for col in (COL_HQ, COL_HF, COL_HI, COL_HG))

    def colspec(col):
        return pl.BlockSpec((ts, w), lambda b, h, s: (b * ns + s, col // w + h))

    nl = lb_logits.shape[0]
    return pl.pallas_call(
        functools.partial(_hgrn_kernel, ts=ts, hb=hb),
        grid=(batch, N_HGRN_HEADS // hb, ns),
        in_specs=[colspec(COL_HQ), colspec(COL_HF), colspec(COL_HI), colspec(COL_HG),
                  pl.BlockSpec((nl, w), lambda b, h, s: (0, h)),
                  pl.BlockSpec((1, w), lambda b, h, s: (0, h)),
                  pl.BlockSpec(wmat.shape, lambda b, h, s: (0, 0)),
                  pl.BlockSpec(lvl.shape, lambda b, h, s: (0, 0))],
        out_specs=pl.BlockSpec((ts, w), lambda b, h, s: (b * ns + s, h)),
        out_shape=jax.ShapeDtypeStruct((t, ATT_W), BF16),
        scratch_shapes=[pltpu.VMEM((hb, HEAD_DIM, HEAD_DIM), F32)],
        compiler_params=_cparams(("parallel", "parallel", "arbitrary")),
        name="hgrn2",
    )(proj, proj, proj, proj, lb_logits, g_norm, jnp.asarray(wmat, BF16), jnp.asarray(lvl))


def _outproj_kernel(oa_ref, or_ref, wa_ref, wb_ref, x_ref, mod_ref, gpost_ref, gpre_ref,
                    wr_ref, br_ref, x1_ref, h2_ref, lg_ref, stage, *, per_b):
    b = pl.program_id(0) // per_b
    y = _dot(oa_ref[...], wa_ref[...]) + _dot(or_ref[...], wb_ref[...])
    x1 = x_ref[...] + _mod_row(mod_ref, b, 2) * _rms(y, gpost_ref[...])
    x1_ref[...] = x1
    h = _rms(x1, gpre_ref[...]) * (1.0 + _mod_row(mod_ref, b, 4)) + _mod_row(mod_ref, b, 3)
    h2_ref[...] = _to_slabs(stage, h)
    h_hi = h.astype(BF16)
    h_lo = (h - h_hi.astype(F32)).astype(BF16)
    hh = _dot(h_hi, wr_ref[...])
    lg_ref[...] = (hh[:, 0:LANES] + hh[:, LANES:2 * LANES]
                   + _dot(h_lo, wr_ref[:, 0:LANES]) + br_ref[...])


def _outproj(o_att, o_rec, w_out, x2, mod, g_post, g_pre, wr, br, seq, tm):
    t, d = x2.shape
    per_b = seq // tm
    wr_hi = wr.astype(BF16)
    wr_cat = jnp.concatenate([wr_hi, (wr - wr_hi.astype(F32)).astype(BF16)], axis=1)
    wa = w_out[:ATT_W].astype(BF16)
    wb = w_out[ATT_W:].astype(BF16)
    row = lambda i: (i, 0)
    fixed = lambda i: (0, 0)
    return pl.pallas_call(
        functools.partial(_outproj_kernel, per_b=per_b),
        grid=(t // tm,),
        in_specs=[pl.BlockSpec((tm, ATT_W), row), pl.BlockSpec((tm, ATT_W), row),
                  pl.BlockSpec((ATT_W, d), fixed), pl.BlockSpec((ATT_W, d), fixed),
                  pl.BlockSpec((tm, d), row),
                  pl.BlockSpec(mod.shape, fixed),
                  pl.BlockSpec((1, d), fixed), pl.BlockSpec((1, d), fixed),
                  pl.BlockSpec((d, 2 * LANES), fixed), pl.BlockSpec((1, LANES), fixed)],
        out_specs=[pl.BlockSpec((tm, d), row), pl.BlockSpec((tm * (d // LANES), LANES), row),
                   pl.BlockSpec((tm, LANES), row)],
        out_shape=[jax.ShapeDtypeStruct((t, d), F32),
                   jax.ShapeDtypeStruct((t * (d // LANES), LANES), BF16),
                   jax.ShapeDtypeStruct((t, LANES), F32)],
        scratch_shapes=[pltpu.VMEM((tm * (d // LANES), LANES), F32)],
        compiler_params=_cparams(("parallel",)),
        name="outproj",
    )(o_att, o_rec, wa, wb, x2, mod, g_post, g_pre, wr_cat, br)


G_LANE0 = N_EXPERTS


def _route_kernel(lg_ref, tri_ref, info_ref, cnt_ref, carry_s):
    @pl.when(pl.program_id(0) == 0)
    def _():
        carry_s[...] = jnp.zeros_like(carry_s)

    x = lg_ref[...]
    lane = lax.broadcasted_iota(jnp.int32, x.shape, 1)
    big = jnp.int32(10 ** 6)
    rmax = lambda a: jnp.max(a, axis=-1, keepdims=True)
    rmin = lambda a: jnp.min(a, axis=-1, keepdims=True)
    rsum = lambda a: jnp.sum(a, axis=-1, keepdims=True)

    is_g = (lane >= G_LANE0) & (lane < G_LANE0 + N_EXPERT_GROUPS)
    lgm = jnp.where(is_g, x, -jnp.inf)
    mg = rmax(lgm)
    pg_sel = 1.0 / rsum(jnp.where(is_g, jnp.exp(lgm - mg), 0.0))
    gsel = rmin(jnp.where(lgm == mg, lane, big)) - G_LANE0

    is_e = (lane >= gsel * EXPERTS_PER_GROUP) & (lane < (gsel + 1) * EXPERTS_PER_GROUP)
    lem = jnp.where(is_e, x, -jnp.inf)
    pe = jnp.where(is_e, jnp.exp(lem - rmax(lem)), 0.0)
    pe = pe / rsum(pe)
    pe = jnp.where(is_e, pe, -1.0)
    v1 = rmax(pe)
    i1 = rmin(jnp.where(pe == v1, lane, big))
    pe2 = jnp.where(lane == i1, -1.0, pe)
    v2 = rmax(pe2)
    i2 = rmin(jnp.where(pe2 == v2, lane, big))
    w1 = v1 / (v1 + v2) * pg_sel
    w2 = v2 / (v1 + v2) * pg_sel

    oh1 = jnp.where(lane == i1, 1.0, 0.0)
    oh2 = jnp.where(lane == i2, 1.0, 0.0)
    both = oh1 + oh2
    before = _dot(tri_ref[...], both.astype(BF16)) + carry_s[0:1, :]
    r1 = rsum(oh1 * before)
    r2 = rsum(oh2 * before)
    carry_s[0:1, :] = carry_s[0:1, :] + jnp.sum(both, axis=0, keepdims=True)
    cnt_ref[...] = carry_s[...]

    cols = (i1.astype(F32), i2.astype(F32), w1, w2, r1, r2)
    info = jnp.zeros(x.shape, F32)
    for ci, col in enumerate(cols):
        info = jnp.where(lane == ci, col, info)
    info_ref[...] = info


def _route(logits, tm):
    t = logits.shape[0]
    tri = jnp.asarray(np.tril(np.ones((tm, tm), np.float32), -1), BF16)
    return pl.pallas_call(
        _route_kernel,
        grid=(t // tm,),
        in_specs=[pl.BlockSpec((tm, LANES), lambda i: (i, 0)),
                  pl.BlockSpec((tm, tm), lambda i: (0, 0))],
        out_specs=[pl.BlockSpec((tm, LANES), lambda i: (i, 0)),
                   pl.BlockSpec((8, LANES), lambda i: (0, 0))],
        out_shape=[jax.ShapeDtypeStruct((t, LANES), F32),
                   jax.ShapeDtypeStruct((8, LANES), F32)],
        scratch_shapes=[pltpu.VMEM((8, LANES), F32)],
        compiler_params=_cparams(("arbitrary",)),
        name="route",
    )(logits, tri)


def _to_slabs(stage, x):
    n, d = x.shape
    slab = d // LANES
    for c in range(slab):
        stage[pl.ds(c, n, stride=slab), :] = x[:, c * LANES:(c + 1) * LANES]
    return stage[...].astype(BF16)


def _from_slabs(stage, slabs, n):
    slab = slabs.shape[0] // n
    stage[...] = slabs.astype(F32)
    return jnp.concatenate([stage[pl.ds(c, n, stride=slab), :] for c in range(slab)], axis=1)


def _expert_kernel(te_ref, nu_ref, par_ref, nxt_ref, dst_ref, xs_ref, wg_hbm, wu_hbm, wd_hbm,
                   y_ref, obuf, stage, ssem, wg_f, wu_f, wd_f, wsem, wg_s, wu_s, wd_s, *,
                   tm, standin_row):
    t = pl.program_id(0)
    nu = nu_ref[0]
    slot = t % 2

    def weights(e, buf):
        return [pltpu.make_async_copy(src.at[e], dst.at[buf], wsem.at[buf])
                for src, dst in ((wg_hbm, wg_f), (wu_hbm, wu_f), (wd_hbm, wd_f))]
    slab = wg_s.shape[0] // LANES
    tile = tm * slab

    def token(ref, i):
        return ref.at[pl.ds(pl.multiple_of(i * slab, slab), slab)]

    def scatter(s):
        for r in range(tm):
            pltpu.make_async_copy(token(obuf, s * tm + r), token(y_ref, dst_ref[0, 0, r]),
                                  ssem.at[s]).start()

    def tile_of(buf, s):
        return buf.at[pl.ds(pl.multiple_of(s * tile, tile), tile)]

    def wait_tile(s):
        pltpu.make_async_copy(tile_of(obuf, s), tile_of(obuf, s), ssem.at[s]).wait()

    @pl.when(t < nu)
    def _():
        @pl.when(t == 0)
        def _():
            obuf[tile:2 * tile, :] = jnp.zeros((tile, LANES), obuf.dtype)

            def fill(row0, s):
                return pltpu.make_async_copy(tile_of(obuf, 1),
                                             y_ref.at[pl.ds(row0 * slab, tile)], ssem.at[s])

            fill(standin_row - 2 * tm, 0).start()
            fill(standin_row - tm, 0).start()
            fill(standin_row - 2 * tm, 0).wait()
            fill(standin_row - tm, 0).wait()
            fill(standin_row, 1).start()

        par = par_ref[t]

        @pl.when(t == 0)
        def _():
            for cp in weights(te_ref[0], par):
                cp.start()

        @pl.when((t == 0) | (te_ref[t] != te_ref[jnp.maximum(t - 1, 0)]))
        def _():
            for cp in weights(te_ref[t], par):
                cp.wait()
            wg_s[...] = wg_f[par].astype(BF16)
            wu_s[...] = wu_f[par].astype(BF16)
            wd_s[...] = wd_f[par].astype(BF16)

            @pl.when(nxt_ref[t] != te_ref[t])
            def _():
                for cp in weights(nxt_ref[t], 1 - par):
                    cp.start()

        x = _from_slabs(stage, xs_ref[...], tm).astype(BF16)
        hid = (_silu(_dot(x, wg_s[...])) * _dot(x, wu_s[...])).astype(BF16)
        row0 = pl.multiple_of(slot * tile, tile)
        obuf[pl.ds(row0, tile), :] = _to_slabs(stage, _dot(hid, wd_s[...]))
        scatter(slot)
        wait_tile(1 - slot)

        @pl.when(t == nu - 1)
        def _():
            wait_tile(slot)


def _experts(tile_expert, n_used, parity, next_expert, dst3, xs, w_gate, w_up, w_down, y_rows,
             tm, standin_row):
    n_tiles = dst3.shape[0]
    d, f = w_gate.shape[1:]
    slab = d // LANES
    hbm = pl.BlockSpec(memory_space=pl.ANY)
    return pl.pallas_call(
        functools.partial(_expert_kernel, tm=tm, standin_row=standin_row),
        grid_spec=pltpu.PrefetchScalarGridSpec(
            num_scalar_prefetch=4,
            grid=(n_tiles,),
            in_specs=[pl.BlockSpec((1, 1, tm), lambda t, te, nu, pa, nx: (t, 0, 0),
                                   memory_space=pltpu.SMEM),
                      pl.BlockSpec((tm * slab, LANES),
                                   lambda t, te, nu, pa, nx: (jnp.minimum(t, nu[0] - 1), 0)),
                      hbm, hbm, hbm],
            out_specs=pl.BlockSpec(memory_space=pl.ANY),
            scratch_shapes=[pltpu.VMEM((2 * tm * slab, LANES), BF16),
                            pltpu.VMEM((tm * slab, LANES), F32),
                            pltpu.SemaphoreType.DMA((2,)),
                            pltpu.VMEM((2, d, f), F32), pltpu.VMEM((2, d, f), F32),
                            pltpu.VMEM((2, f, d), F32),
                            pltpu.SemaphoreType.DMA((2,)),
                            pltpu.VMEM((d, f), BF16), pltpu.VMEM((d, f), BF16),
                            pltpu.VMEM((f, d), BF16)]),
        out_shape=jax.ShapeDtypeStruct((y_rows * slab, LANES), BF16),
        compiler_params=_cparams(("arbitrary",)),
        name="moe_experts",
    )(tile_expert, n_used, parity, next_expert, dst3, xs, w_gate, w_up, w_down)


def _dispatch_kernel(pos0_ref, pos1_ref, end_ref, nu_ref, h_ref, dst_in, xs_ref, dst_ref,
                     zbuf, zsem, sem, *, tm, slab, n_tiles, n_tokens):
    tile = tm * slab
    base = pl.program_id(0) * tm

    @pl.when(pl.program_id(0) == 0)
    def _():
        init = pltpu.make_async_copy(dst_in, dst_ref, zsem)
        init.start()
        init.wait()
        zbuf[...] = jnp.zeros(zbuf.shape, zbuf.dtype)

        def zero(row_end, go):
            cp = pltpu.make_async_copy(
                zbuf, xs_ref.at[pl.ds(pl.multiple_of((row_end - tm) * slab, slab), tile)], zsem)
            pl.when(go)(cp.start)
            return cp, go

        pending = [zero(end_ref[e], end_ref[e] > (end_ref[e - 1] if e else 0))
                   for e in range(N_EXPERTS)]
        pending += [zero((i + 1) * tm, i >= nu_ref[0]) for i in range(n_tiles)]
        for cp, go in pending:
            pl.when(go)(cp.wait)

    def token(ref, i):
        return ref.at[pl.ds(pl.multiple_of(i * slab, slab), slab)]

    def start(r, c):
        p0 = pos0_ref[0, 0, r]
        p1 = pos1_ref[0, 0, r]
        pltpu.make_async_copy(token(h_ref, r), token(xs_ref, p0), sem).start()
        pltpu.make_async_copy(token(h_ref, r), token(xs_ref, p1), sem).start()
        dst_ref[p0] = base + r
        dst_ref[p1] = n_tokens + base + r
        return c

    lax.fori_loop(0, tm, start, 0, unroll=8)
    for _ in range(2):
        pltpu.make_async_copy(h_ref, h_ref, sem).wait()


def _dispatch(pos0, pos1, tile_end, n_used, h2, dst0, n_tiles, tm):
    rows = h2.shape[0]
    slab = rows // pos0.shape[0]
    t = pos0.shape[0]
    per_tile = pl.BlockSpec((1, 1, tm), lambda i: (i, 0, 0), memory_space=pltpu.SMEM)
    whole = pl.BlockSpec(memory_space=pltpu.SMEM)
    return pl.pallas_call(
        functools.partial(_dispatch_kernel, tm=tm, slab=slab, n_tiles=n_tiles, n_tokens=t),
        grid=(t // tm,),
        in_specs=[per_tile, per_tile, whole, whole,
                  pl.BlockSpec((tm * slab, LANES), lambda i: (i, 0)),
                  pl.BlockSpec(memory_space=pl.ANY)],
        out_specs=[pl.BlockSpec(memory_space=pl.ANY), whole],
        out_shape=[jax.ShapeDtypeStruct((n_tiles * tm * slab, LANES), h2.dtype),
                   jax.ShapeDtypeStruct(dst0.shape, jnp.int32)],
        scratch_shapes=[pltpu.VMEM((tm * slab, LANES), h2.dtype),
                        pltpu.SemaphoreType.DMA(()), pltpu.SemaphoreType.DMA(())],
        compiler_params=_cparams(("arbitrary",)),
        name="moe_dispatch",
    )(pos0.reshape(t // tm, 1, tm), pos1.reshape(t // tm, 1, tm), tile_end * tm, n_used, h2,
      dst0)


def _final_kernel(info_ref, x1_ref, mod_ref, g_ref, ya_ref, yb_ref, o_ref, stage_a, stage_b, *,
                  per_b):
    info = info_ref[...]
    tm = x1_ref.shape[0]
    ya = _from_slabs(stage_a, ya_ref[...], tm)
    yb = _from_slabs(stage_b, yb_ref[...], tm)
    y = info[:, 2:3] * ya + info[:, 3:4] * yb
    gain = _mod_row(mod_ref, pl.program_id(0) // per_b, 5) * g_ref[...]
    inv = lax.rsqrt(jnp.mean(y * y, axis=-1, keepdims=True) + EPS)
    o_ref[...] = x1_ref[...] + (y * inv) * gain


def _moe_final(info, x1, mod, g_post, y2, seq, tm):
    t, d = x1.shape
    per_b = seq // tm
    nb = t // tm
    return pl.pallas_call(
        functools.partial(_final_kernel, per_b=per_b),
        grid=(nb,),
        in_specs=[pl.BlockSpec((tm, LANES), lambda i: (i, 0)),
                  pl.BlockSpec((tm, d), lambda i: (i, 0)),
                  pl.BlockSpec(mod.shape, lambda i: (0, 0)),
                  pl.BlockSpec((1, d), lambda i: (0, 0)),
                  pl.BlockSpec((tm * (d // LANES), LANES), lambda i: (i, 0)),
                  pl.BlockSpec((tm * (d // LANES), LANES), lambda i: (nb + i, 0))],
        out_specs=pl.BlockSpec((tm, d), lambda i: (i, 0)),
        out_shape=jax.ShapeDtypeStruct((t, d), F32),
        scratch_shapes=[pltpu.VMEM((tm * (d // LANES), LANES), F32)] * 2,
        compiler_params=_cparams(("parallel",)),
        name="moe_final",
    )(info, x1, mod, g_post, y2, y2)


def _moe(h2, logits, x1, mod, g_post, w_gate, w_up, w_down, seq, tm_route, tm_exp, tm_fin):
    t = x1.shape[0]
    info, cnt = _route(logits, tm_route)
    counts = cnt[0, :N_EXPERTS].astype(jnp.int32)
    tiles_e = (counts + tm_exp - 1) // tm_exp
    tile_end = jnp.cumsum(tiles_e)
    offs = (tile_end - tiles_e) * tm_exp
    n_used = tile_end[-1]

    def slot_of(k):
        e = info[:, k].astype(jnp.int32)
        start = jnp.sum(jnp.where(e[:, None] == jnp.arange(N_EXPERTS)[None, :], offs[None, :], 0),
                        axis=1)
        return start + info[:, 4 + k].astype(jnp.int32)

    n_tiles = (2 * t) // tm_exp + N_EXPERTS
    tidx = jnp.minimum(jnp.arange(n_tiles, dtype=jnp.int32), n_used - 1)
    tile_expert = jnp.sum(tidx[:, None] >= tile_end[None, :], axis=1).astype(jnp.int32)
    switches = jnp.concatenate([jnp.zeros((1,), jnp.int32),
                                (tile_expert[1:] != tile_expert[:-1]).astype(jnp.int32)])
    parity = jnp.cumsum(switches) % 2
    next_expert = tile_expert[jnp.minimum(tile_end[tile_expert], n_used - 1)]
    pos0, pos1 = slot_of(0), slot_of(1)
    slot_ids = jnp.arange(n_tiles * tm_exp, dtype=jnp.int32)
    spare = 2 * t + ((slot_ids // tm_exp) % 2) * tm_exp + slot_ids % tm_exp
    n_used1 = n_used.reshape(1).astype(jnp.int32)
    xs, dst = _dispatch(pos0, pos1, tile_end.astype(jnp.int32), n_used1, h2, spare, n_tiles,
                        tm_exp)

    ew = w_gate.shape
    wg = w_gate.reshape(N_EXPERTS, ew[-2], ew[-1])
    wu = w_up.reshape(N_EXPERTS, ew[-2], ew[-1])
    wd = w_down.reshape(N_EXPERTS, ew[-1], ew[-2])
    y2 = _experts(tile_expert, n_used1, parity.astype(jnp.int32), next_expert.astype(jnp.int32),
                  dst.reshape(n_tiles, 1, tm_exp), xs, wg, wu, wd,
                  2 * t + 3 * tm_exp, tm_exp, 2 * t + 2 * tm_exp)
    return _moe_final(info, x1, mod, g_post, y2, seq, tm_fin)


def _permute_w_in(w_in):
    sizes = [ATT_W] + [KV_W] * 6 + [3 * N_ATT_HEADS] + [ATT_W] * 4
    cuts = np.cumsum(sizes)[:-1]
    q, kc, vc, ks, vs, kw, vw, gates, hq, hf, hi, hg = [
        p.astype(BF16) for p in jnp.split(w_in, cuts, axis=1)]
    w_p = jnp.concatenate([q, kc, ks, kw, vc, vs, vw, hq, hf, hi, hg], axis=1)
    w_g = jnp.pad(gates, ((0, 0), (0, LANES - gates.shape[1])))
    return w_p, w_g


def _block(x, c, positions, w_ada, b_ada, g_pre_mix, g_post_mix, g_pre_ffn, g_post_ffn,
           w_in, w_out, pe_k, w1_k, w2_k, pe_v, w1_v, w2_v, lb_logits, g_norm,
           w_group, b_group, w_router, b_router, w_gate, w_up, w_down, cfg):
    batch, seq, d = x.shape
    t = batch * seq
    x2 = x.reshape(t, d)

    c8 = jnp.zeros((8, d), F32).at[:batch].set(c)
    mod = _ada_mod(c8, w_ada, b_ada)
    cos_t, sin_t = _rope_tables(positions, cfg["tm_rope"])
    w_p, w_g = _permute_w_in(w_in)
    proj, gates, hbk, hbv = _inproj(x2, mod, g_pre_mix.reshape(1, d), w_p, w_g, cos_t, sin_t,
                                    seq, cfg["tm_in"])

    half_blocks = (batch * N_KV_GROUPS, seq // CMP_STRIDE, CMP_STRIDE * HEAD_DIM)
    kc = _compress(hbk.reshape(half_blocks), w1_k, pe_k, w2_k)
    vc = _compress(hbv.reshape(half_blocks), w1_v, pe_v, w2_v)
    ocmp, selb = _cmp_attn(proj, kc, vc, gates, batch, seq, cfg["tq_cmp"])
    o_att = _selwin_attn(proj, selb, ocmp, gates, batch, seq, cfg["tq"], cfg["tk"])
    o_rec = _hgrn(proj, lb_logits, g_norm.reshape(1, -1), batch, seq, cfg["ts_hgrn"],
                  cfg["hb_hgrn"])

    wr = jnp.concatenate([w_router, w_group], axis=1)
    wr = jnp.pad(wr, ((0, 0), (0, LANES - wr.shape[1])))
    br = jnp.pad(jnp.concatenate([b_router, b_group]), (0, LANES - N_EXPERTS - N_EXPERT_GROUPS))
    x1, h2, logits = _outproj(o_att, o_rec, w_out, x2, mod, g_post_mix.reshape(1, d),
                              g_pre_ffn.reshape(1, d), wr, br.reshape(1, LANES), seq,
                              cfg["tm_out"])
    out = _moe(h2, logits, x1, mod, g_post_ffn.reshape(1, d), w_gate, w_up, w_down, seq,
               cfg["tm_route"], cfg["tm_exp"], cfg["tm_fin"])
    return out.reshape(batch, seq, d)


def _config(seq):
    return dict(tm_rope=min(1024, seq), tm_in=min(1024, seq), tq_cmp=min(1024, seq),
                tq=min(256, seq), tk=min(512, seq), ts_hgrn=min(512, seq), hb_hgrn=4,
                tm_out=min(512, seq), tm_route=min(1024, seq), tm_exp=256,
                tm_fin=min(512, seq))


def kernel(x, c, positions, w_ada, b_ada, g_pre_mix, g_post_mix, g_pre_ffn, g_post_ffn, w_in, w_out, cmp_pe_k, cmp_w1_k, cmp_w2_k, cmp_pe_v, cmp_w1_v, cmp_w2_v, hgrn_lb_logits, hgrn_g_norm, w_group, b_group, w_router, b_router, w_gate, w_up, w_down):
    assert w_ada.shape[0] == 1, "single-layer block"
    return _block(x, c, positions, w_ada[0], b_ada[0], g_pre_mix[0], g_post_mix[0],
                  g_pre_ffn[0], g_post_ffn[0], w_in[0], w_out[0], cmp_pe_k[0], cmp_w1_k[0],
                  cmp_w2_k[0], cmp_pe_v[0], cmp_w1_v[0], cmp_w2_v[0], hgrn_lb_logits,
                  hgrn_g_norm[0], w_group[0], b_group[0], w_router[0], b_router[0],
                  w_gate[0], w_up[0], w_down[0], _config(x.shape[1]))
```

```python
import functools

import numpy as np
import jax
import jax.numpy as jnp
from jax import lax
from jax.experimental import pallas as pl
from jax.experimental.pallas import tpu as pltpu

F32 = jnp.float32
BF16 = jnp.bfloat16

HEAD_DIM = 128
N_KV_GROUPS = 2
HEADS_PER_GROUP = 4
N_ATT_HEADS = N_KV_GROUPS * HEADS_PER_GROUP
CMP_BLOCK = 32
CMP_STRIDE = 16
SEL_BLOCK = 64
N_SEL = 16
WINDOW = 512
FORCE_BONUS = 1.0e4
ROPE_THETA = 10000.0
N_HGRN_HEADS = 8
HGRN_CHUNK = 64
N_EXPERT_GROUPS = 4
EXPERTS_PER_GROUP = 8
N_EXPERTS = N_EXPERT_GROUPS * EXPERTS_PER_GROUP
EPS = 1e-6

LANES = 128
SUBLANES = 8
BF16_TILE_ROWS = 16
VMEM_LIMIT = 56 * 1024 * 1024

ATT_W = N_ATT_HEADS * HEAD_DIM
KV_W = N_KV_GROUPS * HEAD_DIM
COL_Q = 0
COL_KC = COL_Q + ATT_W
COL_KS = COL_KC + KV_W
COL_KW = COL_KS + KV_W
COL_VC = COL_KW + KV_W
COL_VS = COL_VC + KV_W
COL_VW = COL_VS + KV_W
COL_HQ = COL_VW + KV_W
COL_HF = COL_HQ + ATT_W
COL_HI = COL_HF + ATT_W
COL_HG = COL_HI + ATT_W
PROJ_W = COL_HG + ATT_W
ROPE_W = COL_VC

MASK_BIG = float(2 ** 30)
LOG2E = 1.4426950408889634
Q_PRESCALE = HEAD_DIM ** -0.5 * LOG2E

_NT = (((1,), (1,)), ((), ()))
_TN = (((0,), (0,)), ((), ()))


def _cparams(sem):
    return pltpu.CompilerParams(dimension_semantics=sem, vmem_limit_bytes=VMEM_LIMIT)


def _dot(a, b, **kw):
    return jnp.dot(a, b, preferred_element_type=F32, **kw)


def _dot_nt(a, b):
    return lax.dot_general(a, b, _NT, preferred_element_type=F32)


def _sigmoid(x):
    return 1.0 / (1.0 + jnp.exp(-x))


def _silu(x):
    return x * _sigmoid(x)


def _rms(x, g):
    return x * lax.rsqrt(jnp.mean(x * x, axis=-1, keepdims=True) + EPS) * g


def _mod_row(mod_ref, b, k):
    d = mod_ref.shape[1] // 6
    return mod_ref[pl.ds(b, 1), k * d:(k + 1) * d]


def _lane_col(x, lane, col):
    return jnp.sum(jnp.where(lane == col, x, 0.0), axis=-1, keepdims=True)


def _ada_kernel(c_ref, w_ref, b_ref, o_ref):
    s = _silu(c_ref[...])
    s_hi = s.astype(BF16)
    s_lo = (s - s_hi.astype(F32)).astype(BF16)
    w = w_ref[...]
    w_hi = w.astype(BF16)
    w_lo = (w - w_hi.astype(F32)).astype(BF16)
    o_ref[...] = _dot(s_hi, w_hi) + _dot(s_lo, w_hi) + _dot(s_hi, w_lo) + b_ref[...]


def _ada_mod(c8, w_ada, b_ada, tn=512):
    rows, d = c8.shape
    n = w_ada.shape[1]
    return pl.pallas_call(
        _ada_kernel,
        grid=(n // tn,),
        in_specs=[pl.BlockSpec((rows, d), lambda j: (0, 0)),
                  pl.BlockSpec((d, tn), lambda j: (0, j)),
                  pl.BlockSpec((1, tn), lambda j: (0, j))],
        out_specs=pl.BlockSpec((rows, tn), lambda j: (0, j)),
        out_shape=jax.ShapeDtypeStruct((rows, n), F32),
        compiler_params=_cparams(("parallel",)),
        name="ada_mod",
    )(c8, w_ada, b_ada.reshape(1, n))


def _rope_kernel(ang_ref, cos_ref, sin_ref):
    a = ang_ref[...]
    n = a.shape[0]
    half = HEAD_DIM // 2
    lane = lax.broadcasted_iota(jnp.int32, a.shape, 1)
    first = lane < half
    c = jnp.cos(a)
    s = jnp.sin(a)
    c_sw = pltpu.roll(c, half, axis=1)
    s_sw = pltpu.roll(s, half, axis=1)
    cos_ref[pl.ds(0, n, stride=2), :] = jnp.where(first, c, c_sw)
    cos_ref[pl.ds(1, n, stride=2), :] = jnp.where(first, c_sw, c)
    sin_ref[pl.ds(0, n, stride=2), :] = jnp.where(first, -s, s_sw)
    sin_ref[pl.ds(1, n, stride=2), :] = jnp.where(first, -s_sw, s)


def _rope_tables(positions, tm):
    t = positions.size
    inv_freq = ROPE_THETA ** (-jnp.arange(0, HEAD_DIM, 2, dtype=F32) / HEAD_DIM)
    pos = positions.reshape(t).astype(F32)
    ang = jnp.concatenate([pos[0::2, None] * inv_freq[None, :], pos[1::2, None] * inv_freq[None, :]],
                          axis=1)
    spec = pl.BlockSpec((tm, HEAD_DIM), lambda i: (i, 0))
    return pl.pallas_call(
        _rope_kernel,
        grid=(t // tm,),
        in_specs=[pl.BlockSpec((tm // 2, HEAD_DIM), lambda i: (i, 0))],
        out_specs=[spec, spec],
        out_shape=[jax.ShapeDtypeStruct((t, HEAD_DIM), F32)] * 2,
        compiler_params=_cparams(("parallel",)),
        name="rope_tables",
    )(ang)


def _inproj_kernel(x_ref, mod_ref, g_ref, w_ref, wg_ref, cos_ref, sin_ref,
                   proj_ref, gates_ref, hbk_ref, hbv_ref, h_s, stage, *, tn, per_b):
    j = pl.program_id(1)

    @pl.when(j == 0)
    def _():
        b = pl.program_id(0) // per_b
        h = (_rms(x_ref[...], g_ref[...]) * (1.0 + _mod_row(mod_ref, b, 1))
             + _mod_row(mod_ref, b, 0))
        hb = h.astype(BF16)
        h_s[...] = hb
        gates_ref[...] = _dot(hb, wg_ref[...])

    heads = tn // HEAD_DIM
    rows = x_ref.shape[0]

    def store(jt):
        acc = _dot(h_s[...], w_ref[...])
        for c in range(heads):
            a = acc[:, c * HEAD_DIM:(c + 1) * HEAD_DIM]
            col = None if jt is None else jt * tn + c * HEAD_DIM
            if col is not None and col < ROPE_W:
                a = a * cos_ref[...] + pltpu.roll(a, HEAD_DIM // 2, axis=1) * sin_ref[...]
            if col is not None and col < ATT_W:
                a = a * Q_PRESCALE
            proj_ref[:, c * HEAD_DIM:(c + 1) * HEAD_DIM] = a.astype(BF16)
            for col0, hb_ref in ((COL_KC, hbk_ref), (COL_VC, hbv_ref)):
                if col is not None and col0 <= col < col0 + KV_W:
                    stage[...] = a
                    hb_ref[0, (col - col0) // HEAD_DIM] = jnp.concatenate(
                        [stage[pl.ds(i, rows // CMP_STRIDE, stride=CMP_STRIDE), :]
                         for i in range(CMP_STRIDE)], axis=1).astype(BF16)

    n_special = -(-COL_VS // tn)
    for jt in range(n_special):
        pl.when(j == jt)(functools.partial(store, jt))
    pl.when(j >= n_special)(functools.partial(store, None))


def _inproj(x2, mod, g_pre, w_in_p, w_gates, cos_t, sin_t, seq, tm, tn=512):
    t, d = x2.shape
    per_b = seq // tm
    assert COL_KC % tn + KV_W <= tn and COL_VC % tn + KV_W <= tn
    assert tm % (BF16_TILE_ROWS * CMP_STRIDE) == 0
    hb_spec = pl.BlockSpec((1, N_KV_GROUPS, tm // CMP_STRIDE, CMP_STRIDE * HEAD_DIM),
                           lambda i, j: (i // per_b, 0, i % per_b, 0))
    hb_shape = jax.ShapeDtypeStruct(
        (t // seq, N_KV_GROUPS, seq // CMP_STRIDE, CMP_STRIDE * HEAD_DIM), BF16)
    return pl.pallas_call(
        functools.partial(_inproj_kernel, tn=tn, per_b=per_b),
        grid=(t // tm, PROJ_W // tn),
        in_specs=[pl.BlockSpec((tm, d), lambda i, j: (i, 0)),
                  pl.BlockSpec(mod.shape, lambda i, j: (0, 0)),
                  pl.BlockSpec((1, d), lambda i, j: (0, 0)),
                  pl.BlockSpec((d, tn), lambda i, j: (0, j)),
                  pl.BlockSpec((d, LANES), lambda i, j: (0, 0)),
                  pl.BlockSpec((tm, HEAD_DIM), lambda i, j: (i, 0)),
                  pl.BlockSpec((tm, HEAD_DIM), lambda i, j: (i, 0))],
        out_specs=[pl.BlockSpec((tm, tn), lambda i, j: (i, j)),
                   pl.BlockSpec((tm, LANES), lambda i, j: (i, 0)), hb_spec, hb_spec],
        out_shape=[jax.ShapeDtypeStruct((t, PROJ_W), BF16),
                   jax.ShapeDtypeStruct((t, LANES), F32), hb_shape, hb_shape],
        scratch_shapes=[pltpu.VMEM((tm, d), BF16), pltpu.VMEM((tm, HEAD_DIM), F32)],
        compiler_params=_cparams(("parallel", "arbitrary")),
        name="inproj",
    )(x2, mod, g_pre, w_in_p, w_gates, cos_t, sin_t)


def _compress_kernel(h_ref, w1_ref, pe_ref, w2_ref, o_ref):
    hb = h_ref[0]
    half = hb.shape[1]
    ya = _dot(hb, w1_ref[0:half, :])
    yb = _dot(hb, w1_ref[half:2 * half, :])
    const = _dot(pe_ref[...], w1_ref[...])
    n = ya.shape[0]
    yb_next = pltpu.roll(yb, n - 1, axis=0)
    hid = _silu(ya + yb_next + const[0:1, :])
    out = _dot(hid.astype(BF16), w2_ref[...])
    row = lax.broadcasted_iota(jnp.int32, out.shape, 0)
    o_ref[0] = jnp.where(row < n - 1, out, 0.0).astype(BF16)


def _compress(hblk, w1, pe, w2):
    bg, n, half = hblk.shape
    pe8 = jnp.zeros((8, 2 * half), BF16).at[0].set(pe.reshape(-1).astype(BF16))
    return pl.pallas_call(
        _compress_kernel,
        grid=(bg,),
        in_specs=[pl.BlockSpec((1, n, half), lambda i: (i, 0, 0)),
                  pl.BlockSpec((2 * half, HEAD_DIM), lambda i: (0, 0)),
                  pl.BlockSpec((8, 2 * half), lambda i: (0, 0)),
                  pl.BlockSpec((HEAD_DIM, HEAD_DIM), lambda i: (0, 0))],
        out_specs=pl.BlockSpec((1, n, HEAD_DIM), lambda i: (i, 0, 0)),
        out_shape=jax.ShapeDtypeStruct((bg, n, HEAD_DIM), BF16),
        compiler_params=_cparams(("parallel",)),
        name="compress",
    )(hblk, w1.astype(BF16), pe8, w2.astype(BF16))


def _cmp_kernel(q_ref, kc_ref, vc_ref, gates_ref, ov_ref, ocmp_ref, selb_ref, imp_s, cnt_s, *,
                tq, nsel):
    g = pl.program_id(0) % N_KV_GROUPS
    q0 = pl.program_id(1) * tq
    kc = kc_ref[0]
    vc = vc_ref[0]
    nc = kc.shape[0]
    t_idx = q0 + lax.broadcasted_iota(jnp.int32, (tq, nc), 0)
    c_idx = lax.broadcasted_iota(jnp.int32, (tq, nc), 1)
    c_ok = c_idx * CMP_STRIDE + (CMP_BLOCK - 1) <= t_idx
    gl = gates_ref[...]
    lane = lax.broadcasted_iota(jnp.int32, (tq, LANES), 1)
    psum = jnp.zeros((tq, nc), F32)
    for h in range(HEADS_PER_GROUP):
        qh = q_ref[:, h * HEAD_DIM:(h + 1) * HEAD_DIM]
        s = jnp.where(c_ok, _dot_nt(qh, kc), -jnp.inf)
        m = jnp.max(s, axis=-1, keepdims=True)
        m = jnp.where(m == -jnp.inf, 0.0, m)
        p = jnp.exp2(s - m)
        p = p * (1.0 / jnp.maximum(jnp.sum(p, axis=-1, keepdims=True), 1e-30))
        psum = psum + p
        o = _dot(p.astype(BF16), vc)
        glog = _lane_col(gl, lane, (g * HEADS_PER_GROUP + h) * 3)
        ocmp_ref[:, h * HEAD_DIM:(h + 1) * HEAD_DIM] = (o * _sigmoid(glog)).astype(BF16)

    imp = _dot(psum, ov_ref[...], precision=lax.Precision.HIGHEST)
    cur = (q0 + lax.broadcasted_iota(jnp.int32, (tq, LANES), 0)) // SEL_BLOCK
    forced = (lane == 0) | (lane == cur) | (lane == cur - 1)
    imp = jnp.where(lane <= cur, imp + jnp.where(forced, FORCE_BONUS, 0.0), -jnp.inf)
    imp_s[...] = imp.T[0:nsel, :]
    cnt_s[...] = jnp.zeros(cnt_s.shape, F32)
    sub = SUBLANES
    n_groups = nsel // sub
    row = lax.broadcasted_iota(jnp.int32, (sub, tq), 0)

    def count_against(kg):
        groups = [imp_s[g * sub:(g + 1) * sub, :] for g in range(n_groups)]
        cnts = [cnt_s[g * sub:(g + 1) * sub, :] for g in range(n_groups)]
        for k in range(kg * sub, (kg + 1) * sub):
            vk = imp_s[k:k + 1, :]
            for g, grp in enumerate(groups):
                if k < g * sub:
                    beats = vk >= grp
                elif k >= (g + 1) * sub:
                    beats = vk > grp
                else:
                    beats = (vk > grp) | ((vk == grp) & (row > k - g * sub))
                cnts[g] = cnts[g] + jnp.where(beats, 1.0, 0.0)
        for g in range(n_groups):
            cnt_s[g * sub:(g + 1) * sub, :] = cnts[g]

    last_block = (q0 + tq - 1) // SEL_BLOCK
    for kg in range(n_groups):
        pl.when(kg * sub <= last_block)(functools.partial(count_against, kg))
    imp_t = imp_s[...]
    sel = (cnt_s[...] < float(N_SEL)) & (imp_t > -jnp.inf)
    bias_t = jnp.where(sel, 0.0, -MASK_BIG)
    if nsel < LANES:
        bias_t = jnp.concatenate([bias_t, jnp.zeros((LANES - nsel, tq), F32)], axis=0)
    selb_ref[...] = bias_t.T.astype(BF16)


def _cmp_attn(proj, kc, vc, gates, batch, seq, tq):
    t = proj.shape[0]
    nq = seq // tq
    nc = kc.shape[1]
    nsel = seq // SEL_BLOCK
    assert nsel <= LANES and nsel % 8 == 0
    c_start = np.arange(nc) * CMP_STRIDE
    j_start = np.arange(LANES) * SEL_BLOCK
    ov = ((c_start[:, None] < j_start[None, :] + SEL_BLOCK)
          & (c_start[:, None] + CMP_BLOCK > j_start[None, :])
          & (np.arange(LANES)[None, :] < nsel)).astype(np.float32)
    gq = HEADS_PER_GROUP * HEAD_DIM
    qmap = lambda bg, i: ((bg // N_KV_GROUPS) * nq + i, bg % N_KV_GROUPS)
    return pl.pallas_call(
        functools.partial(_cmp_kernel, tq=tq, nsel=nsel),
        grid=(batch * N_KV_GROUPS, nq),
        in_specs=[pl.BlockSpec((tq, gq), qmap),
                  pl.BlockSpec((1, nc, HEAD_DIM), lambda bg, i: (bg, 0, 0)),
                  pl.BlockSpec((1, nc, HEAD_DIM), lambda bg, i: (bg, 0, 0)),
                  pl.BlockSpec((tq, LANES), lambda bg, i: ((bg // N_KV_GROUPS) * nq + i, 0)),
                  pl.BlockSpec((nc, LANES), lambda bg, i: (0, 0))],
        out_specs=[pl.BlockSpec((tq, gq), qmap),
                   pl.BlockSpec((tq, LANES), lambda bg, i: (bg * nq + i, 0))],
        out_shape=[jax.ShapeDtypeStruct((t, ATT_W), BF16),
                   jax.ShapeDtypeStruct((batch * N_KV_GROUPS * seq, LANES), BF16)],
        scratch_shapes=[pltpu.VMEM((nsel, tq), F32), pltpu.VMEM((nsel, tq), F32)],
        compiler_params=_cparams(("parallel", "parallel")),
        name="cmp_attn",
    )(proj, kc, vc, gates, jnp.asarray(ov))


def _selwin_kernel(q_ref, selb_ref, ocmp_ref, gates_ref, ks_ref, vs_ref, kw_ref, vw_ref,
                   cb_ref, wb_ref, o_ref, kaug_s, vaug_s, vwaug_s, qaug_s, sbuf_s, m_s, acc_s,
                   swin_s, *, tq, tk, seq, nwin):
    g = pl.program_id(0) % N_KV_GROUPS
    qi = pl.program_id(1)
    q0 = qi * tq
    hq = HEADS_PER_GROUP * tq
    half = hq // 2

    @pl.when(qi == 0)
    def _():
        r = lax.broadcasted_iota(jnp.int32, (seq, HEAD_DIM), 0)
        ln = lax.broadcasted_iota(jnp.int32, (seq, HEAD_DIM), 1)
        ones_col = jnp.where(ln == 0, 1.0, 0.0).astype(BF16)
        kaug_s[:, 0:HEAD_DIM] = ks_ref[...]
        kaug_s[:, HEAD_DIM:2 * HEAD_DIM] = jnp.where(r // SEL_BLOCK == ln, 1.0, 0.0).astype(BF16)
        vaug_s[:, 0:HEAD_DIM] = vs_ref[...]
        vaug_s[:, HEAD_DIM:2 * HEAD_DIM] = ones_col
        vwaug_s[:, 0:HEAD_DIM] = vw_ref[...]
        vwaug_s[:, HEAD_DIM:2 * HEAD_DIM] = ones_col

    for h in range(HEADS_PER_GROUP):
        qaug_s[h * tq:(h + 1) * tq, 0:HEAD_DIM] = q_ref[:, h * HEAD_DIM:(h + 1) * HEAD_DIM]
        qaug_s[h * tq:(h + 1) * tq, HEAD_DIM:2 * HEAD_DIM] = selb_ref[...]
    qa = qaug_s[...]

    def scores(kt, slot):
        ks = kaug_s[pl.ds(pl.multiple_of(kt * tk, tk), tk), :]
        sbuf_s[slot] = _dot_nt(qa, ks)

    def absorb(kt, slot, diagonal=False):
        vs = vaug_s[pl.ds(pl.multiple_of(kt * tk, tk), tk), :]
        s = sbuf_s[slot]
        if diagonal:
            bias = cb_ref[(q0 - kt * tk) // tq]
            s = (s.reshape(HEADS_PER_GROUP, tq, tk) + bias[None]).reshape(hq, tk)
        m = m_s[...]
        m_new = jnp.maximum(m, jnp.max(s, axis=-1, keepdims=True))
        m_s[...] = m_new
        pb = jnp.exp2(s - m_new[:, 0:1]).astype(BF16)
        if diagonal:
            pv = jnp.concatenate([_dot(pb[0:half], vs), _dot(pb[half:hq], vs)], axis=0)
        else:
            pv = _dot(pb, vs)
        acc_s[...] = jnp.exp2(m - m_new)[:, 0:1] * acc_s[...] + pv

    n_full = q0 // tk
    m_s[...] = jnp.full(m_s.shape, -jnp.inf, F32)
    acc_s[...] = jnp.zeros(acc_s.shape, F32)
    scores(0, 0)

    def pair(j, c):
        scores(2 * j + 1, 1)
        absorb(2 * j, 0)
        scores(2 * j + 2, 0)
        absorb(2 * j + 1, 1)
        return c

    lax.fori_loop(0, n_full // 2, pair, 0)
    odd = n_full % 2

    @pl.when(odd == 1)
    def _():
        scores(n_full, 1)
        absorb(n_full - 1, 0)

    nq = seq // tq
    w0 = jnp.clip(qi - (nwin - 1), 0, nq - nwin) * tq
    wlen = nwin * tq
    kwin = kw_ref[pl.ds(pl.multiple_of(w0, tq), wlen), :]
    vwin = vwaug_s[pl.ds(pl.multiple_of(w0, tq), wlen), :]
    swin_s[0:half, :] = _dot_nt(qaug_s[0:half, 0:HEAD_DIM], kwin)
    swin_s[half:hq, :] = _dot_nt(qaug_s[half:hq, 0:HEAD_DIM], kwin)
    absorb(n_full, odd, diagonal=True)
    acc = acc_s[...]
    o_sel = acc[:, 0:HEAD_DIM] / acc[:, HEAD_DIM:HEAD_DIM + 1]

    wbias = wb_ref[(q0 - w0) // tq]
    sw = (swin_s[...].reshape(HEADS_PER_GROUP, tq, wlen) + wbias[None]).reshape(hq, wlen)
    pwb = jnp.exp2(sw - jnp.max(sw, axis=-1, keepdims=True)).astype(BF16)
    ow = jnp.concatenate([_dot(pwb[0:half], vwin), _dot(pwb[half:hq], vwin)], axis=0)
    o_win = ow[:, 0:HEAD_DIM] / ow[:, HEAD_DIM:HEAD_DIM + 1]

    gl = gates_ref[...]
    lane = lax.broadcasted_iota(jnp.int32, (tq, LANES), 1)
    for h in range(HEADS_PER_GROUP):
        gcol = (g * HEADS_PER_GROUP + h) * 3
        g_s = _sigmoid(_lane_col(gl, lane, gcol + 1))
        g_w = _sigmoid(_lane_col(gl, lane, gcol + 2))
        o = (ocmp_ref[:, h * HEAD_DIM:(h + 1) * HEAD_DIM].astype(F32)
             + g_s * o_sel[h * tq:(h + 1) * tq, :] + g_w * o_win[h * tq:(h + 1) * tq, :])
        o_ref[:, h * HEAD_DIM:(h + 1) * HEAD_DIM] = o.astype(BF16)


def _selwin_attn(proj, selb, ocmp, gates, batch, seq, tq, tk):
    t = proj.shape[0]
    nq = seq // tq
    assert tk % tq == 0 and tq & (tq - 1) == 0 and seq % tk == 0
    nwin = min(WINDOW // tq + 1, nq)
    assert WINDOW % tq == 0
    gq = HEADS_PER_GROUP * HEAD_DIM
    hq = HEADS_PER_GROUP * tq
    r = np.arange(tq)[:, None]
    cbias = np.stack([np.where(np.arange(tk)[None, :] <= r + off * tq, 0.0, -np.inf)
                      for off in range(tk // tq)]).astype(np.float32)
    dist = [r + off * tq - np.arange(nwin * tq)[None, :] for off in range(nwin)]
    wbias = np.stack([np.where((d >= 0) & (d < WINDOW), 0.0, -np.inf)
                      for d in dist]).astype(np.float32)
    qmap = lambda bg, i: ((bg // N_KV_GROUPS) * nq + i, bg % N_KV_GROUPS)

    def kvspec(col):
        return pl.BlockSpec((seq, HEAD_DIM),
                            lambda bg, i: (bg // N_KV_GROUPS, col // HEAD_DIM + bg % N_KV_GROUPS))

    return pl.pallas_call(
        functools.partial(_selwin_kernel, tq=tq, tk=tk, seq=seq, nwin=nwin),
        grid=(batch * N_KV_GROUPS, nq),
        in_specs=[pl.BlockSpec((tq, gq), qmap),
                  pl.BlockSpec((tq, LANES), lambda bg, i: (bg * nq + i, 0)),
                  pl.BlockSpec((tq, gq), qmap),
                  pl.BlockSpec((tq, LANES), lambda bg, i: ((bg // N_KV_GROUPS) * nq + i, 0)),
                  kvspec(COL_KS), kvspec(COL_VS), kvspec(COL_KW), kvspec(COL_VW),
                  pl.BlockSpec(cbias.shape, lambda bg, i: (0, 0, 0)),
                  pl.BlockSpec(wbias.shape, lambda bg, i: (0, 0, 0))],
        out_specs=pl.BlockSpec((tq, gq), qmap),
        out_shape=jax.ShapeDtypeStruct((t, ATT_W), BF16),
        scratch_shapes=[pltpu.VMEM((seq, 2 * HEAD_DIM), BF16),
                        pltpu.VMEM((seq, 2 * HEAD_DIM), BF16),
                        pltpu.VMEM((seq, 2 * HEAD_DIM), BF16),
                        pltpu.VMEM((hq, 2 * HEAD_DIM), BF16),
                        pltpu.VMEM((2, hq, tk), F32),
                        pltpu.VMEM((hq, LANES), F32),
                        pltpu.VMEM((hq, 2 * HEAD_DIM), F32),
                        pltpu.VMEM((hq, nwin * tq), F32)],
        compiler_params=_cparams(("parallel", "arbitrary")),
        name="selwin_attn",
    )(proj, selb, ocmp, gates, proj, proj, proj, proj, jnp.asarray(cbias), jnp.asarray(wbias))


_HGRN_LEVELS = (32, 16, 8, 4, 2, 1)


def _hgrn_tables():
    c = HGRN_CHUNK
    nl = len(_HGRN_LEVELS)
    idx = np.arange(c)
    i, j = idx[:, None], idx[None, :]
    mats = [j <= i, j > i]
    lvl = np.full((c, c), nl + 1, np.int32)
    for li, s in enumerate(_HGRN_LEVELS):
        mats.append((j >= (i // s) * s) & (j <= i))
        mats.append((j > i) & (j <= (i // s) * s + s - 1))
        lvl[((i // (2 * s)) == (j // (2 * s))) & (((i // s) % 2) == 1) & (((j // s) % 2) == 0)] = li
    lvl[i == j] = nl
    w = np.concatenate(mats, 0).astype(np.float32)
    return np.concatenate([w, w], axis=1), lvl


def _hgrn_head(hq_ref, hf_ref, hi_ref, hg_ref, lbl_ref, gn_ref, w_ref, lvl_ref, o_ref, st_s,
               hh, ts):
    c = HGRN_CHUNK
    nc = ts // c
    nl = len(_HGRN_LEVELS)
    cs = slice(hh * HEAD_DIM, (hh + 1) * HEAD_DIM)
    lbl = lbl_ref[:, cs]
    e = jnp.exp(lbl - jnp.max(lbl, axis=0, keepdims=True))
    lb = e[0:1, :] / jnp.sum(e, axis=0, keepdims=True)

    f = lb + (1.0 - lb) * _sigmoid(hf_ref[:, cs].astype(F32))
    k = 1.0 - f
    qf = _silu(hq_ref[:, cs].astype(F32))
    v = hi_ref[:, cs]

    lf = jnp.log(f) * LOG2E
    lf_hi = lf.astype(BF16)
    lf_lo = (lf - lf_hi.astype(F32)).astype(BF16)
    side = lambda a: jnp.concatenate([a[ci * c:(ci + 1) * c, :] for ci in range(nc)], axis=1)
    ex = _dot(w_ref[...], jnp.concatenate([side(lf_hi), side(lf_lo)], axis=0))

    def rows(bi):
        return jnp.concatenate(
            [ex[bi * c:(bi + 1) * c, ci * HEAD_DIM:(ci + 1) * HEAD_DIM] for ci in range(nc)], axis=0)

    b = rows(0)
    qhat = (qf * jnp.exp2(b)).astype(BF16)
    khat = (k * jnp.exp2(rows(1))).astype(BF16)
    lvl = lvl_ref[...]
    masks = [lvl == li for li in range(nl + 1)]
    qs = [(qf * jnp.exp2(rows(2 + 2 * li))).astype(BF16) for li in range(nl)] + [qf.astype(BF16)]
    ks = [(k * jnp.exp2(rows(3 + 2 * li))).astype(BF16) for li in range(nl)] + [k.astype(BF16)]
    o_intra = []
    for ci in range(nc):
        sl = slice(ci * c, (ci + 1) * c)
        a = jnp.zeros((c, c), F32)
        for li in range(nl + 1):
            a = jnp.where(masks[li], _dot_nt(qs[li][sl, :], ks[li][sl, :]), a)
        o_intra.append(_dot(a.astype(BF16), v[sl, :]))

    st = st_s[hh]
    outs = []
    for ci in range(nc):
        sl = slice(ci * c, (ci + 1) * c)
        outs.append(o_intra[ci] + _dot_nt(qhat[sl, :], st.astype(BF16)))
        st = st * jnp.exp2(b[(ci + 1) * c - 1:(ci + 1) * c, :]) + lax.dot_general(
            v[sl, :], khat[sl, :], _TN, preferred_element_type=F32)
    st_s[hh] = st
    o = jnp.concatenate(outs, axis=0)
    o = o * lax.rsqrt(jnp.mean(o * o, axis=-1, keepdims=True) + EPS) * gn_ref[:, cs]
    o_ref[:, cs] = (o * _silu(hg_ref[:, cs].astype(F32))).astype(BF16)


def _hgrn_kernel(hq_ref, hf_ref, hi_ref, hg_ref, lbl_ref, gn_ref, w_ref, lvl_ref,
                 o_ref, st_s, *, ts, hb):
    @pl.when(pl.program_id(2) == 0)
    def _():
        st_s[...] = jnp.zeros_like(st_s)

    for hh in range(hb):
        _hgrn_head(hq_ref, hf_ref, hi_ref, hg_ref, lbl_ref, gn_ref, w_ref, lvl_ref, o_ref, st_s,
                   hh, ts)


def _hgrn(proj, lb_logits, g_norm, batch, seq, ts, hb):
    t = proj.shape[0]
    ns = seq // ts
    wmat, lvl = _hgrn_tables()
    w = hb * HEAD_DIM
    assert all(col % w == 0 for col in (COL_HQ, COL_HF, COL_HI, COL_HG))

    def colspec(col):
        return pl.BlockSpec((ts, w), lambda b, h, s: (b * ns + s, col // w + h))

    nl = lb_logits.shape[0]
    return pl.pallas_call(
        functools.partial(_hgrn_kernel, ts=ts, hb=hb),
        grid=(batch, N_HGRN_HEADS // hb, ns),
        in_specs=[colspec(COL_HQ), colspec(COL_HF), colspec(COL_HI), colspec(COL_HG),
                  pl.BlockSpec((nl, w), lambda b, h, s: (0, h)),
                  pl.BlockSpec((1, w), lambda b, h, s: (0, h)),
                  pl.BlockSpec(wmat.shape, lambda b, h, s: (0, 0)),
                  pl.BlockSpec(lvl.shape, lambda b, h, s: (0, 0))],
        out_specs=pl.BlockSpec((ts, w), lambda b, h, s: (b * ns + s, h)),
        out_shape=jax.ShapeDtypeStruct((t, ATT_W), BF16),
        scratch_shapes=[pltpu.VMEM((hb, HEAD_DIM, HEAD_DIM), F32)],
        compiler_params=_cparams(("parallel", "parallel", "arbitrary")),
        name="hgrn2",
    )(proj, proj, proj, proj, lb_logits, g_norm, jnp.asarray(wmat, BF16), jnp.asarray(lvl))


def _outproj_kernel(oa_ref, or_ref, wa_ref, wb_ref, x_ref, mod_ref, gpost_ref, gpre_ref,
                    wr_ref, br_ref, x1_ref, h2_ref, lg_ref, stage, *, per_b):
    b = pl.program_id(0) // per_b
    y = _dot(oa_ref[...], wa_ref[...]) + _dot(or_ref[...], wb_ref[...])
    x1 = x_ref[...] + _mod_row(mod_ref, b, 2) * _rms(y, gpost_ref[...])
    x1_ref[...] = x1
    h = _rms(x1, gpre_ref[...]) * (1.0 + _mod_row(mod_ref, b, 4)) + _mod_row(mod_ref, b, 3)
    h2_ref[...] = _to_slabs(stage, h)
    h_hi = h.astype(BF16)
    h_lo = (h - h_hi.astype(F32)).astype(BF16)
    hh = _dot(h_hi, wr_ref[...])
    lg_ref[...] = (hh[:, 0:LANES] + hh[:, LANES:2 * LANES]
                   + _dot(h_lo, wr_ref[:, 0:LANES]) + br_ref[...])


def _outproj(o_att, o_rec, w_out, x2, mod, g_post, g_pre, wr, br, seq, tm):
    t, d = x2.shape
    per_b = seq // tm
    wr_hi = wr.astype(BF16)
    wr_cat = jnp.concatenate([wr_hi, (wr - wr_hi.astype(F32)).astype(BF16)], axis=1)
    wa = w_out[:ATT_W].astype(BF16)
    wb = w_out[ATT_W:].astype(BF16)
    row = lambda i: (i, 0)
    fixed = lambda i: (0, 0)
    return pl.pallas_call(
        functools.partial(_outproj_kernel, per_b=per_b),
        grid=(t // tm,),
        in_specs=[pl.BlockSpec((tm, ATT_W), row), pl.BlockSpec((tm, ATT_W), row),
                  pl.BlockSpec((ATT_W, d), fixed), pl.BlockSpec((ATT_W, d), fixed),
                  pl.BlockSpec((tm, d), row),
                  pl.BlockSpec(mod.shape, fixed),
                  pl.BlockSpec((1, d), fixed), pl.BlockSpec((1, d), fixed),
                  pl.BlockSpec((d, 2 * LANES), fixed), pl.BlockSpec((1, LANES), fixed)],
        out_specs=[pl.BlockSpec((tm, d), row), pl.BlockSpec((tm * (d // LANES), LANES), row),
                   pl.BlockSpec((tm, LANES), row)],
        out_shape=[jax.ShapeDtypeStruct((t, d), F32),
                   jax.ShapeDtypeStruct((t * (d // LANES), LANES), BF16),
                   jax.ShapeDtypeStruct((t, LANES), F32)],
        scratch_shapes=[pltpu.VMEM((tm * (d // LANES), LANES), F32)],
        compiler_params=_cparams(("parallel",)),
        name="outproj",
    )(o_att, o_rec, wa, wb, x2, mod, g_post, g_pre, wr_cat, br)


G_LANE0 = N_EXPERTS


def _route_kernel(lg_ref, tri_ref, info_ref, cnt_ref, carry_s):
    @pl.when(pl.program_id(0) == 0)
    def _():
        carry_s[...] = jnp.zeros_like(carry_s)

    x = lg_ref[...]
    lane = lax.broadcasted_iota(jnp.int32, x.shape, 1)
    big = jnp.int32(10 ** 6)
    rmax = lambda a: jnp.max(a, axis=-1, keepdims=True)
    rmin = lambda a: jnp.min(a, axis=-1, keepdims=True)
    rsum = lambda a: jnp.sum(a, axis=-1, keepdims=True)

    is_g = (lane >= G_LANE0) & (lane < G_LANE0 + N_EXPERT_GROUPS)
    lgm = jnp.where(is_g, x, -jnp.inf)
    mg = rmax(lgm)
    pg_sel = 1.0 / rsum(jnp.where(is_g, jnp.exp(lgm - mg), 0.0))
    gsel = rmin(jnp.where(lgm == mg, lane, big)) - G_LANE0

    is_e = (lane >= gsel * EXPERTS_PER_GROUP) & (lane < (gsel + 1) * EXPERTS_PER_GROUP)
    lem = jnp.where(is_e, x, -jnp.inf)
    pe = jnp.where(is_e, jnp.exp(lem - rmax(lem)), 0.0)
    pe = pe / rsum(pe)
    pe = jnp.where(is_e, pe, -1.0)
    v1 = rmax(pe)
    i1 = rmin(jnp.where(pe == v1, lane, big))
    pe2 = jnp.where(lane == i1, -1.0, pe)
    v2 = rmax(pe2)
    i2 = rmin(jnp.where(pe2 == v2, lane, big))
    w1 = v1 / (v1 + v2) * pg_sel
    w2 = v2 / (v1 + v2) * pg_sel

    oh1 = jnp.where(lane == i1, 1.0, 0.0)
    oh2 = jnp.where(lane == i2, 1.0, 0.0)
    both = oh1 + oh2
    before = _dot(tri_ref[...], both.astype(BF16)) + carry_s[0:1, :]
    r1 = rsum(oh1 * before)
    r2 = rsum(oh2 * before)
    carry_s[0:1, :] = carry_s[0:1, :] + jnp.sum(both, axis=0, keepdims=True)
    cnt_ref[...] = carry_s[...]

    cols = (i1.astype(F32), i2.astype(F32), w1, w2, r1, r2)
    info = jnp.zeros(x.shape, F32)
    for ci, col in enumerate(cols):
        info = jnp.where(lane == ci, col, info)
    info_ref[...] = info


def _route(logits, tm):
    t = logits.shape[0]
    tri = jnp.asarray(np.tril(np.ones((tm, tm), np.float32), -1), BF16)
    return pl.pallas_call(
        _route_kernel,
        grid=(t // tm,),
        in_specs=[pl.BlockSpec((tm, LANES), lambda i: (i, 0)),
                  pl.BlockSpec((tm, tm), lambda i: (0, 0))],
        out_specs=[pl.BlockSpec((tm, LANES), lambda i: (i, 0)),
                   pl.BlockSpec((8, LANES), lambda i: (0, 0))],
        out_shape=[jax.ShapeDtypeStruct((t, LANES), F32),
                   jax.ShapeDtypeStruct((8, LANES), F32)],
        scratch_shapes=[pltpu.VMEM((8, LANES), F32)],
        compiler_params=_cparams(("arbitrary",)),
        name="route",
    )(logits, tri)


def _to_slabs(stage, x):
    n, d = x.shape
    slab = d // LANES
    for c in range(slab):
        stage[pl.ds(c, n, stride=slab), :] = x[:, c * LANES:(c + 1) * LANES]
    return stage[...].astype(BF16)


def _from_slabs(stage, slabs, n):
    slab = slabs.shape[0] // n
    stage[...] = slabs.astype(F32)
    return jnp.concatenate([stage[pl.ds(c, n, stride=slab), :] for c in range(slab)], axis=1)


def _expert_kernel(te_ref, nu_ref, par_ref, nxt_ref, dst_ref, xs_ref, wg_hbm, wu_hbm, wd_hbm,
                   y_ref, obuf, stage, ssem, wg_f, wu_f, wd_f, wsem, wg_s, wu_s, wd_s, *,
                   tm, standin_row):
    t = pl.program_id(0)
    nu = nu_ref[0]
    slot = t % 2

    def weights(e, buf):
        return [pltpu.make_async_copy(src.at[e], dst.at[buf], wsem.at[buf])
                for src, dst in ((wg_hbm, wg_f), (wu_hbm, wu_f), (wd_hbm, wd_f))]
    slab = wg_s.shape[0] // LANES
    tile = tm * slab

    def token(ref, i):
        return ref.at[pl.ds(pl.multiple_of(i * slab, slab), slab)]

    def scatter(s):
        for r in range(tm):
            pltpu.make_async_copy(token(obuf, s * tm + r), token(y_ref, dst_ref[0, 0, r]),
                                  ssem.at[s]).start()

    def tile_of(buf, s):
        return buf.at[pl.ds(pl.multiple_of(s * tile, tile), tile)]

    def wait_tile(s):
        pltpu.make_async_copy(tile_of(obuf, s), tile_of(obuf, s), ssem.at[s]).wait()

    @pl.when(t < nu)
    def _():
        @pl.when(t == 0)
        def _():
            obuf[tile:2 * tile, :] = jnp.zeros((tile, LANES), obuf.dtype)

            def fill(row0, s):
                return pltpu.make_async_copy(tile_of(obuf, 1),
                                             y_ref.at[pl.ds(row0 * slab, tile)], ssem.at[s])

            fill(standin_row - 2 * tm, 0).start()
            fill(standin_row - tm, 0).start()
            fill(standin_row - 2 * tm, 0).wait()
            fill(standin_row - tm, 0).wait()
            fill(standin_row, 1).start()

        par = par_ref[t]

        @pl.when(t == 0)
        def _():
            for cp in weights(te_ref[0], par):
                cp.start()

        @pl.when((t == 0) | (te_ref[t] != te_ref[jnp.maximum(t - 1, 0)]))
        def _():
            for cp in weights(te_ref[t], par):
                cp.wait()
            wg_s[...] = wg_f[par].astype(BF16)
            wu_s[...] = wu_f[par].astype(BF16)
            wd_s[...] = wd_f[par].astype(BF16)

            @pl.when(nxt_ref[t] != te_ref[t])
            def _():
                for cp in weights(nxt_ref[t], 1 - par):
                    cp.start()

        x = _from_slabs(stage, xs_ref[...], tm).astype(BF16)
        hid = (_silu(_dot(x, wg_s[...])) * _dot(x, wu_s[...])).astype(BF16)
        row0 = pl.multiple_of(slot * tile, tile)
        obuf[pl.ds(row0, tile), :] = _to_slabs(stage, _dot(hid, wd_s[...]))
        scatter(slot)
        wait_tile(1 - slot)

        @pl.when(t == nu - 1)
        def _():
            wait_tile(slot)


def _experts(tile_expert, n_used, parity, next_expert, dst3, xs, w_gate, w_up, w_down, y_rows,
             tm, standin_row):
    n_tiles = dst3.shape[0]
    d, f = w_gate.shape[1:]
    slab = d // LANES
    hbm = pl.BlockSpec(memory_space=pl.ANY)
    return pl.pallas_call(
        functools.partial(_expert_kernel, tm=tm, standin_row=standin_row),
        grid_spec=pltpu.PrefetchScalarGridSpec(
            num_scalar_prefetch=4,
            grid=(n_tiles,),
            in_specs=[pl.BlockSpec((1, 1, tm), lambda t, te, nu, pa, nx: (t, 0, 0),
                                   memory_space=pltpu.SMEM),
                      pl.BlockSpec((tm * slab, LANES),
                                   lambda t, te, nu, pa, nx: (jnp.minimum(t, nu[0] - 1), 0)),
                      hbm, hbm, hbm],
            out_specs=pl.BlockSpec(memory_space=pl.ANY),
            scratch_shapes=[pltpu.VMEM((2 * tm * slab, LANES), BF16),
                            pltpu.VMEM((tm * slab, LANES), F32),
                            pltpu.SemaphoreType.DMA((2,)),
                            pltpu.VMEM((2, d, f), F32), pltpu.VMEM((2, d, f), F32),
                            pltpu.VMEM((2, f, d), F32),
                            pltpu.SemaphoreType.DMA((2,)),
                            pltpu.VMEM((d, f), BF16), pltpu.VMEM((d, f), BF16),
                            pltpu.VMEM((f, d), BF16)]),
        out_shape=jax.ShapeDtypeStruct((y_rows * slab, LANES), BF16),
        compiler_params=_cparams(("arbitrary",)),
        name="moe_experts",
    )(tile_expert, n_used, parity, next_expert, dst3, xs, w_gate, w_up, w_down)


def _dispatch_kernel(pos0_ref, pos1_ref, end_ref, nu_ref, h_ref, dst_in, xs_ref, dst_ref,
                     zbuf, zsem, sem, *, tm, slab, n_tiles, n_tokens):
    tile = tm * slab
    base = pl.program_id(0) * tm

    @pl.when(pl.program_id(0) == 0)
    def _():
        init = pltpu.make_async_copy(dst_in, dst_ref, zsem)
        init.start()
        init.wait()
        zbuf[...] = jnp.zeros(zbuf.shape, zbuf.dtype)

        def zero(row_end, go):
            cp = pltpu.make_async_copy(
                zbuf, xs_ref.at[pl.ds(pl.multiple_of((row_end - tm) * slab, slab), tile)], zsem)
            pl.when(go)(cp.start)
            return cp, go

        pending = [zero(end_ref[e], end_ref[e] > (end_ref[e - 1] if e else 0))
                   for e in range(N_EXPERTS)]
        pending += [zero((i + 1) * tm, i >= nu_ref[0]) for i in range(n_tiles)]
        for cp, go in pending:
            pl.when(go)(cp.wait)

    def token(ref, i):
        return ref.at[pl.ds(pl.multiple_of(i * slab, slab), slab)]

    def start(r, c):
        p0 = pos0_ref[0, 0, r]
        p1 = pos1_ref[0, 0, r]
        pltpu.make_async_copy(token(h_ref, r), token(xs_ref, p0), sem).start()
        pltpu.make_async_copy(token(h_ref, r), token(xs_ref, p1), sem).start()
        dst_ref[p0] = base + r
        dst_ref[p1] = n_tokens + base + r
        return c

    lax.fori_loop(0, tm, start, 0, unroll=8)
    for _ in range(2):
        pltpu.make_async_copy(h_ref, h_ref, sem).wait()


def _dispatch(pos0, pos1, tile_end, n_used, h2, dst0, n_tiles, tm):
    rows = h2.shape[0]
    slab = rows // pos0.shape[0]
    t = pos0.shape[0]
    per_tile = pl.BlockSpec((1, 1, tm), lambda i: (i, 0, 0), memory_space=pltpu.SMEM)
    whole = pl.BlockSpec(memory_space=pltpu.SMEM)
    return pl.pallas_call(
        functools.partial(_dispatch_kernel, tm=tm, slab=slab, n_tiles=n_tiles, n_tokens=t),
        grid=(t // tm,),
        in_specs=[per_tile, per_tile, whole, whole,
                  pl.BlockSpec((tm * slab, LANES), lambda i: (i, 0)),
                  pl.BlockSpec(memory_space=pl.ANY)],
        out_specs=[pl.BlockSpec(memory_space=pl.ANY), whole],
        out_shape=[jax.ShapeDtypeStruct((n_tiles * tm * slab, LANES), h2.dtype),
                   jax.ShapeDtypeStruct(dst0.shape, jnp.int32)],
        scratch_shapes=[pltpu.VMEM((tm * slab, LANES), h2.dtype),
                        pltpu.SemaphoreType.DMA(()), pltpu.SemaphoreType.DMA(())],
        compiler_params=_cparams(("arbitrary",)),
        name="moe_dispatch",
    )(pos0.reshape(t // tm, 1, tm), pos1.reshape(t // tm, 1, tm), tile_end * tm, n_used, h2,
      dst0)


def _final_kernel(info_ref, x1_ref, mod_ref, g_ref, ya_ref, yb_ref, o_ref, stage_a, stage_b, *,
                  per_b):
    info = info_ref[...]
    tm = x1_ref.shape[0]
    ya = _from_slabs(stage_a, ya_ref[...], tm)
    yb = _from_slabs(stage_b, yb_ref[...], tm)
    y = info[:, 2:3] * ya + info[:, 3:4] * yb
    gain = _mod_row(mod_ref, pl.program_id(0) // per_b, 5) * g_ref[...]
    inv = lax.rsqrt(jnp.mean(y * y, axis=-1, keepdims=True) + EPS)
    o_ref[...] = x1_ref[...] + (y * inv) * gain


def _moe_final(info, x1, mod, g_post, y2, seq, tm):
    t, d = x1.shape
    per_b = seq // tm
    nb = t // tm
    return pl.pallas_call(
        functools.partial(_final_kernel, per_b=per_b),
        grid=(nb,),
        in_specs=[pl.BlockSpec((tm, LANES), lambda i: (i, 0)),
                  pl.BlockSpec((tm, d), lambda i: (i, 0)),
                  pl.BlockSpec(mod.shape, lambda i: (0, 0)),
                  pl.BlockSpec((1, d), lambda i: (0, 0)),
                  pl.BlockSpec((tm * (d // LANES), LANES), lambda i: (i, 0)),
                  pl.BlockSpec((tm * (d // LANES), LANES), lambda i: (nb + i, 0))],
        out_specs=pl.BlockSpec((tm, d), lambda i: (i, 0)),
        out_shape=jax.ShapeDtypeStruct((t, d), F32),
        scratch_shapes=[pltpu.VMEM((tm * (d // LANES), LANES), F32)] * 2,
        compiler_params=_cparams(("parallel",)),
        name="moe_final",
    )(info, x1, mod, g_post, y2, y2)


def _moe(h2, logits, x1, mod, g_post, w_gate, w_up, w_down, seq, tm_route, tm_exp, tm_fin):
    t = x1.shape[0]
    info, cnt = _route(logits, tm_route)
    counts = cnt[0, :N_EXPERTS].astype(jnp.int32)
    tiles_e = (counts + tm_exp - 1) // tm_exp
    tile_end = jnp.cumsum(tiles_e)
    offs = (tile_end - tiles_e) * tm_exp
    n_used = tile_end[-1]

    def slot_of(k):
        e = info[:, k].astype(jnp.int32)
        start = jnp.sum(jnp.where(e[:, None] == jnp.arange(N_EXPERTS)[None, :], offs[None, :], 0),
                        axis=1)
        return start + info[:, 4 + k].astype(jnp.int32)

    n_tiles = (2 * t) // tm_exp + N_EXPERTS
    tidx = jnp.minimum(jnp.arange(n_tiles, dtype=jnp.int32), n_used - 1)
    tile_expert = jnp.sum(tidx[:, None] >= tile_end[None, :], axis=1).astype(jnp.int32)
    switches = jnp.concatenate([jnp.zeros((1,), jnp.int32),
                                (tile_expert[1:] != tile_expert[:-1]).astype(jnp.int32)])
    parity = jnp.cumsum(switches) % 2
    next_expert = tile_expert[jnp.minimum(tile_end[tile_expert], n_used - 1)]
    pos0, pos1 = slot_of(0), slot_of(1)
    slot_ids = jnp.arange(n_tiles * tm_exp, dtype=jnp.int32)
    spare = 2 * t + ((slot_ids // tm_exp) % 2) * tm_exp + slot_ids % tm_exp
    n_used1 = n_used.reshape(1).astype(jnp.int32)
    xs, dst = _dispatch(pos0, pos1, tile_end.astype(jnp.int32), n_used1, h2, spare, n_tiles,
                        tm_exp)

    ew = w_gate.shape
    wg = w_gate.reshape(N_EXPERTS, ew[-2], ew[-1])
    wu = w_up.reshape(N_EXPERTS, ew[-2], ew[-1])
    wd = w_down.reshape(N_EXPERTS, ew[-1], ew[-2])
    y2 = _experts(tile_expert, n_used1, parity.astype(jnp.int32), next_expert.astype(jnp.int32),
                  dst.reshape(n_tiles, 1, tm_exp), xs, wg, wu, wd,
                  2 * t + 3 * tm_exp, tm_exp, 2 * t + 2 * tm_exp)
    return _moe_final(info, x1, mod, g_post, y2, seq, tm_fin)


def _permute_w_in(w_in):
    sizes = [ATT_W] + [KV_W] * 6 + [3 * N_ATT_HEADS] + [ATT_W] * 4
    cuts = np.cumsum(sizes)[:-1]
    q, kc, vc, ks, vs, kw, vw, gates, hq, hf, hi, hg = [
        p.astype(BF16) for p in jnp.split(w_in, cuts, axis=1)]
    w_p = jnp.concatenate([q, kc, ks, kw, vc, vs, vw, hq, hf, hi, hg], axis=1)
    w_g = jnp.pad(gates, ((0, 0), (0, LANES - gates.shape[1])))
    return w_p, w_g


def _block(x, c, positions, w_ada, b_ada, g_pre_mix, g_post_mix, g_pre_ffn, g_post_ffn,
           w_in, w_out, pe_k, w1_k, w2_k, pe_v, w1_v, w2_v, lb_logits, g_norm,
           w_group, b_group, w_router, b_router, w_gate, w_up, w_down, cfg):
    batch, seq, d = x.shape
    t = batch * seq
    x2 = x.reshape(t, d)

    c8 = jnp.zeros((8, d), F32).at[:batch].set(c)
    mod = _ada_mod(c8, w_ada, b_ada)
    cos_t, sin_t = _rope_tables(positions, cfg["tm_rope"])
    w_p, w_g = _permute_w_in(w_in)
    proj, gates, hbk, hbv = _inproj(x2, mod, g_pre_mix.reshape(1, d), w_p, w_g, cos_t, sin_t,
                                    seq, cfg["tm_in"])

    half_blocks = (batch * N_KV_GROUPS, seq // CMP_STRIDE, CMP_STRIDE * HEAD_DIM)
    kc = _compress(hbk.reshape(half_blocks), w1_k, pe_k, w2_k)
    vc = _compress(hbv.reshape(half_blocks), w1_v, pe_v, w2_v)
    ocmp, selb = _cmp_attn(proj, kc, vc, gates, batch, seq, cfg["tq_cmp"])
    o_att = _selwin_attn(proj, selb, ocmp, gates, batch, seq, cfg["tq"], cfg["tk"])
    o_rec = _hgrn(proj, lb_logits, g_norm.reshape(1, -1), batch, seq, cfg["ts_hgrn"],
                  cfg["hb_hgrn"])

    wr = jnp.concatenate([w_router, w_group], axis=1)
    wr = jnp.pad(wr, ((0, 0), (0, LANES - wr.shape[1])))
    br = jnp.pad(jnp.concatenate([b_router, b_group]), (0, LANES - N_EXPERTS - N_EXPERT_GROUPS))
    x1, h2, logits = _outproj(o_att, o_rec, w_out, x2, mod, g_post_mix.reshape(1, d),
                              g_pre_ffn.reshape(1, d), wr, br.reshape(1, LANES), seq,
                              cfg["tm_out"])
    out = _moe(h2, logits, x1, mod, g_post_ffn.reshape(1, d), w_gate, w_up, w_down, seq,
               cfg["tm_route"], cfg["tm_exp"], cfg["tm_fin"])
    return out.reshape(batch, seq, d)


def _config(seq):
    return dict(tm_rope=min(1024, seq), tm_in=min(1024, seq), tq_cmp=min(1024, seq),
                tq=min(256, seq), tk=min(512, seq), ts_hgrn=min(512, seq), hb_hgrn=4,
                tm_out=min(512, seq), tm_route=min(1024, seq), tm_exp=256,
                tm_fin=min(512, seq))


def kernel(x, c, positions, w_ada, b_ada, g_pre_mix, g_post_mix, g_pre_ffn, g_post_ffn, w_in, w_out, cmp_pe_k, cmp_w1_k, cmp_w2_k, cmp_pe_v, cmp_w1_v, cmp_w2_v, hgrn_lb_logits, hgrn_g_norm, w_group, b_group, w_router, b_router, w_gate, w_up, w_down):
    assert w_ada.shape[0] == 1, "single-layer block"
    return _block(x, c, positions, w_ada[0], b_ada[0], g_pre_mix[0], g_post_mix[0],
                  g_pre_ffn[0], g_post_ffn[0], w_in[0], w_out[0], cmp_pe_k[0], cmp_w1_k[0],
                  cmp_w2_k[0], cmp_pe_v[0], cmp_w1_v[0], cmp_w2_v[0], hgrn_lb_logits,
                  hgrn_g_norm[0], w_group[0], b_group[0], w_router[0], b_router[0],
                  w_gate[0], w_up[0], w_down[0], _config(x.shape[1]))
```

```python
import functools

import numpy as np
import jax
import jax.numpy as jnp
from jax import lax
from jax.experimental import pallas as pl
from jax.experimental.pallas import tpu as pltpu

F32 = jnp.float32
BF16 = jnp.bfloat16

HEAD_DIM = 128
N_KV_GROUPS = 2
HEADS_PER_GROUP = 4
N_ATT_HEADS = N_KV_GROUPS * HEADS_PER_GROUP
CMP_BLOCK = 32
CMP_STRIDE = 16
SEL_BLOCK = 64
N_SEL = 16
WINDOW = 512
FORCE_BONUS = 1.0e4
ROPE_THETA = 10000.0
N_HGRN_HEADS = 8
HGRN_CHUNK = 64
N_EXPERT_GROUPS = 4
EXPERTS_PER_GROUP = 8
N_EXPERTS = N_EXPERT_GROUPS * EXPERTS_PER_GROUP
EPS = 1e-6

LANES = 128
SUBLANES = 8
BF16_TILE_ROWS = 16
VMEM_LIMIT = 56 * 1024 * 1024

ATT_W = N_ATT_HEADS * HEAD_DIM
KV_W = N_KV_GROUPS * HEAD_DIM
COL_Q = 0
COL_KC = COL_Q + ATT_W
COL_KS = COL_KC + KV_W
COL_KW = COL_KS + KV_W
COL_VC = COL_KW + KV_W
COL_VS = COL_VC + KV_W
COL_VW = COL_VS + KV_W
COL_HQ = COL_VW + KV_W
COL_HF = COL_HQ + ATT_W
COL_HI = COL_HF + ATT_W
COL_HG = COL_HI + ATT_W
PROJ_W = COL_HG + ATT_W
ROPE_W = COL_VC

MASK_BIG = float(2 ** 30)
LOG2E = 1.4426950408889634
Q_PRESCALE = HEAD_DIM ** -0.5 * LOG2E

_NT = (((1,), (1,)), ((), ()))
_TN = (((0,), (0,)), ((), ()))


def _cparams(sem):
    return pltpu.CompilerParams(dimension_semantics=sem, vmem_limit_bytes=VMEM_LIMIT)


def _dot(a, b, **kw):
    return jnp.dot(a, b, preferred_element_type=F32, **kw)


def _dot_nt(a, b):
    return lax.dot_general(a, b, _NT, preferred_element_type=F32)


def _sigmoid(x):
    return 1.0 / (1.0 + jnp.exp(-x))


def _silu(x):
    return x * _sigmoid(x)


def _rms(x, g):
    return x * lax.rsqrt(jnp.mean(x * x, axis=-1, keepdims=True) + EPS) * g


def _mod_row(mod_ref, b, k):
    d = mod_ref.shape[1] // 6
    return mod_ref[pl.ds(b, 1), k * d:(k + 1) * d]


def _lane_col(x, lane, col):
    return jnp.sum(jnp.where(lane == col, x, 0.0), axis=-1, keepdims=True)


def _ada_kernel(c_ref, w_ref, b_ref, o_ref):
    s = _silu(c_ref[...])
    s_hi = s.astype(BF16)
    s_lo = (s - s_hi.astype(F32)).astype(BF16)
    w = w_ref[...]
    w_hi = w.astype(BF16)
    w_lo = (w - w_hi.astype(F32)).astype(BF16)
    o_ref[...] = _dot(s_hi, w_hi) + _dot(s_lo, w_hi) + _dot(s_hi, w_lo) + b_ref[...]


def _ada_mod(c8, w_ada, b_ada, tn=512):
    rows, d = c8.shape
    n = w_ada.shape[1]
    return pl.pallas_call(
        _ada_kernel,
        grid=(n // tn,),
        in_specs=[pl.BlockSpec((rows, d), lambda j: (0, 0)),
                  pl.BlockSpec((d, tn), lambda j: (0, j)),
                  pl.BlockSpec((1, tn), lambda j: (0, j))],
        out_specs=pl.BlockSpec((rows, tn), lambda j: (0, j)),
        out_shape=jax.ShapeDtypeStruct((rows, n), F32),
        compiler_params=_cparams(("parallel",)),
        name="ada_mod",
    )(c8, w_ada, b_ada.reshape(1, n))


def _rope_kernel(ang_ref, cos_ref, sin_ref):
    a = ang_ref[...]
    n = a.shape[0]
    half = HEAD_DIM // 2
    lane = lax.broadcasted_iota(jnp.int32, a.shape, 1)
    first = lane < half
    c = jnp.cos(a)
    s = jnp.sin(a)
    c_sw = pltpu.roll(c, half, axis=1)
    s_sw = pltpu.roll(s, half, axis=1)
    cos_ref[pl.ds(0, n, stride=2), :] = jnp.where(first, c, c_sw)
    cos_ref[pl.ds(1, n, stride=2), :] = jnp.where(first, c_sw, c)
    sin_ref[pl.ds(0, n, stride=2), :] = jnp.where(first, -s, s_sw)
    sin_ref[pl.ds(1, n, stride=2), :] = jnp.where(first, -s_sw, s)


def _rope_tables(positions, tm):
    t = positions.size
    inv_freq = ROPE_THETA ** (-jnp.arange(0, HEAD_DIM, 2, dtype=F32) / HEAD_DIM)
    pos = positions.reshape(t).astype(F32)
    ang = jnp.concatenate([pos[0::2, None] * inv_freq[None, :], pos[1::2, None] * inv_freq[None, :]],
                          axis=1)
    spec = pl.BlockSpec((tm, HEAD_DIM), lambda i: (i, 0))
    return pl.pallas_call(
        _rope_kernel,
        grid=(t // tm,),
        in_specs=[pl.BlockSpec((tm // 2, HEAD_DIM), lambda i: (i, 0))],
        out_specs=[spec, spec],
        out_shape=[jax.ShapeDtypeStruct((t, HEAD_DIM), F32)] * 2,
        compiler_params=_cparams(("parallel",)),
        name="rope_tables",
    )(ang)


def _inproj_kernel(x_ref, mod_ref, g_ref, w_ref, wg_ref, cos_ref, sin_ref,
                   proj_ref, gates_ref, hbk_ref, hbv_ref, h_s, stage, *, tn, per_b):
    j = pl.program_id(1)

    @pl.when(j == 0)
    def _():
        b = pl.program_id(0) // per_b
        h = (_rms(x_ref[...], g_ref[...]) * (1.0 + _mod_row(mod_ref, b, 1))
             + _mod_row(mod_ref, b, 0))
        hb = h.astype(BF16)
        h_s[...] = hb
        gates_ref[...] = _dot(hb, wg_ref[...])

    heads = tn // HEAD_DIM
    rows = x_ref.shape[0]

    def store(jt):
        acc = _dot(h_s[...], w_ref[...])
        for c in range(heads):
            a = acc[:, c * HEAD_DIM:(c + 1) * HEAD_DIM]
            col = None if jt is None else jt * tn + c * HEAD_DIM
            if col is not None and col < ROPE_W:
                a = a * cos_ref[...] + pltpu.roll(a, HEAD_DIM // 2, axis=1) * sin_ref[...]
            if col is not None and col < ATT_W:
                a = a * Q_PRESCALE
            proj_ref[:, c * HEAD_DIM:(c + 1) * HEAD_DIM] = a.astype(BF16)
            for col0, hb_ref in ((COL_KC, hbk_ref), (COL_VC, hbv_ref)):
                if col is not None and col0 <= col < col0 + KV_W:
                    stage[...] = a
                    hb_ref[0, (col - col0) // HEAD_DIM] = jnp.concatenate(
                        [stage[pl.ds(i, rows // CMP_STRIDE, stride=CMP_STRIDE), :]
                         for i in range(CMP_STRIDE)], axis=1).astype(BF16)

    n_special = -(-COL_VS // tn)
    for jt in range(n_special):
        pl.when(j == jt)(functools.partial(store, jt))
    pl.when(j >= n_special)(functools.partial(store, None))


def _inproj(x2, mod, g_pre, w_in_p, w_gates, cos_t, sin_t, seq, tm, tn=512):
    t, d = x2.shape
    per_b = seq // tm
    assert COL_KC % tn + KV_W <= tn and COL_VC % tn + KV_W <= tn
    assert tm % (BF16_TILE_ROWS * CMP_STRIDE) == 0
    hb_spec = pl.BlockSpec((1, N_KV_GROUPS, tm // CMP_STRIDE, CMP_STRIDE * HEAD_DIM),
                           lambda i, j: (i // per_b, 0, i % per_b, 0))
    hb_shape = jax.ShapeDtypeStruct(
        (t // seq, N_KV_GROUPS, seq // CMP_STRIDE, CMP_STRIDE * HEAD_DIM), BF16)
    return pl.pallas_call(
        functools.partial(_inproj_kernel, tn=tn, per_b=per_b),
        grid=(t // tm, PROJ_W // tn),
        in_specs=[pl.BlockSpec((tm, d), lambda i, j: (i, 0)),
                  pl.BlockSpec(mod.shape, lambda i, j: (0, 0)),
                  pl.BlockSpec((1, d), lambda i, j: (0, 0)),
                  pl.BlockSpec((d, tn), lambda i, j: (0, j)),
                  pl.BlockSpec((d, LANES), lambda i, j: (0, 0)),
                  pl.BlockSpec((tm, HEAD_DIM), lambda i, j: (i, 0)),
                  pl.BlockSpec((tm, HEAD_DIM), lambda i, j: (i, 0))],
        out_specs=[pl.BlockSpec((tm, tn), lambda i, j: (i, j)),
                   pl.BlockSpec((tm, LANES), lambda i, j: (i, 0)), hb_spec, hb_spec],
        out_shape=[jax.ShapeDtypeStruct((t, PROJ_W), BF16),
                   jax.ShapeDtypeStruct((t, LANES), F32), hb_shape, hb_shape],
        scratch_shapes=[pltpu.VMEM((tm, d), BF16), pltpu.VMEM((tm, HEAD_DIM), F32)],
        compiler_params=_cparams(("parallel", "arbitrary")),
        name="inproj",
    )(x2, mod, g_pre, w_in_p, w_gates, cos_t, sin_t)


def _compress_kernel(h_ref, w1_ref, pe_ref, w2_ref, o_ref):
    hb = h_ref[0]
    half = hb.shape[1]
    ya = _dot(hb, w1_ref[0:half, :])
    yb = _dot(hb, w1_ref[half:2 * half, :])
    const = _dot(pe_ref[...], w1_ref[...])
    n = ya.shape[0]
    yb_next = pltpu.roll(yb, n - 1, axis=0)
    hid = _silu(ya + yb_next + const[0:1, :])
    out = _dot(hid.astype(BF16), w2_ref[...])
    row = lax.broadcasted_iota(jnp.int32, out.shape, 0)
    o_ref[0] = jnp.where(row < n - 1, out, 0.0).astype(BF16)


def _compress(hblk, w1, pe, w2):
    bg, n, half = hblk.shape
    pe8 = jnp.zeros((8, 2 * half), BF16).at[0].set(pe.reshape(-1).astype(BF16))
    return pl.pallas_call(
        _compress_kernel,
        grid=(bg,),
        in_specs=[pl.BlockSpec((1, n, half), lambda i: (i, 0, 0)),
                  pl.BlockSpec((2 * half, HEAD_DIM), lambda i: (0, 0)),
                  pl.BlockSpec((8, 2 * half), lambda i: (0, 0)),
                  pl.BlockSpec((HEAD_DIM, HEAD_DIM), lambda i: (0, 0))],
        out_specs=pl.BlockSpec((1, n, HEAD_DIM), lambda i: (i, 0, 0)),
        out_shape=jax.ShapeDtypeStruct((bg, n, HEAD_DIM), BF16),
        compiler_params=_cparams(("parallel",)),
        name="compress",
    )(hblk, w1.astype(BF16), pe8, w2.astype(BF16))


def _cmp_kernel(q_ref, kc_ref, vc_ref, gates_ref, ov_ref, ocmp_ref, selb_ref, imp_s, cnt_s, *,
                tq, nsel):
    g = pl.program_id(0) % N_KV_GROUPS
    q0 = pl.program_id(1) * tq
    kc = kc_ref[0]
    vc = vc_ref[0]
    nc = kc.shape[0]
    t_idx = q0 + lax.broadcasted_iota(jnp.int32, (tq, nc), 0)
    c_idx = lax.broadcasted_iota(jnp.int32, (tq, nc), 1)
    c_ok = c_idx * CMP_STRIDE + (CMP_BLOCK - 1) <= t_idx
    gl = gates_ref[...]
    lane = lax.broadcasted_iota(jnp.int32, (tq, LANES), 1)
    psum = jnp.zeros((tq, nc), F32)
    for h in range(HEADS_PER_GROUP):
        qh = q_ref[:, h * HEAD_DIM:(h + 1) * HEAD_DIM]
        s = jnp.where(c_ok, _dot_nt(qh, kc), -jnp.inf)
        m = jnp.max(s, axis=-1, keepdims=True)
        m = jnp.where(m == -jnp.inf, 0.0, m)
        p = jnp.exp2(s - m)
        p = p * (1.0 / jnp.maximum(jnp.sum(p, axis=-1, keepdims=True), 1e-30))
        psum = psum + p
        o = _dot(p.astype(BF16), vc)
        glog = _lane_col(gl, lane, (g * HEADS_PER_GROUP + h) * 3)
        ocmp_ref[:, h * HEAD_DIM:(h + 1) * HEAD_DIM] = (o * _sigmoid(glog)).astype(BF16)

    imp = _dot(psum, ov_ref[...], precision=lax.Precision.HIGHEST)
    cur = (q0 + lax.broadcasted_iota(jnp.int32, (tq, LANES), 0)) // SEL_BLOCK
    forced = (lane == 0) | (lane == cur) | (lane == cur - 1)
    imp = jnp.where(lane <= cur, imp + jnp.where(forced, FORCE_BONUS, 0.0), -jnp.inf)
    imp_s[...] = imp.T[0:nsel, :]
    cnt_s[...] = jnp.zeros(cnt_s.shape, F32)
    sub = SUBLANES
    n_groups = nsel // sub
    row = lax.broadcasted_iota(jnp.int32, (sub, tq), 0)

    def count_against(kg):
        groups = [imp_s[g * sub:(g + 1) * sub, :] for g in range(n_groups)]
        cnts = [cnt_s[g * sub:(g + 1) * sub, :] for g in range(n_groups)]
        for k in range(kg * sub, (kg + 1) * sub):
            vk = imp_s[k:k + 1, :]
            for g, grp in enumerate(groups):
                if k < g * sub:
                    beats = vk >= grp
                elif k >= (g + 1) * sub:
                    beats = vk > grp
                else:
                    beats = (vk > grp) | ((vk == grp) & (row > k - g * sub))
                cnts[g] = cnts[g] + jnp.where(beats, 1.0, 0.0)
        for g in range(n_groups):
            cnt_s[g * sub:(g + 1) * sub, :] = cnts[g]

    last_block = (q0 + tq - 1) // SEL_BLOCK
    for kg in range(n_groups):
        pl.when(kg * sub <= last_block)(functools.partial(count_against, kg))
    imp_t = imp_s[...]
    sel = (cnt_s[...] < float(N_SEL)) & (imp_t > -jnp.inf)
    bias_t = jnp.where(sel, 0.0, -MASK_BIG)
    if nsel < LANES:
        bias_t = jnp.concatenate([bias_t, jnp.zeros((LANES - nsel, tq), F32)], axis=0)
    selb_ref[...] = bias_t.T.astype(BF16)


def _cmp_attn(proj, kc, vc, gates, batch, seq, tq):
    t = proj.shape[0]
    nq = seq // tq
    nc = kc.shape[1]
    nsel = seq // SEL_BLOCK
    assert nsel <= LANES and nsel % 8 == 0
    c_start = np.arange(nc) * CMP_STRIDE
    j_start = np.arange(LANES) * SEL_BLOCK
    ov = ((c_start[:, None] < j_start[None, :] + SEL_BLOCK)
          & (c_start[:, None] + CMP_BLOCK > j_start[None, :])
          & (np.arange(LANES)[None, :] < nsel)).astype(np.float32)
    gq = HEADS_PER_GROUP * HEAD_DIM
    qmap = lambda bg, i: ((bg // N_KV_GROUPS) * nq + i, bg % N_KV_GROUPS)
    return pl.pallas_call(
        functools.partial(_cmp_kernel, tq=tq, nsel=nsel),
        grid=(batch * N_KV_GROUPS, nq),
        in_specs=[pl.BlockSpec((tq, gq), qmap),
                  pl.BlockSpec((1, nc, HEAD_DIM), lambda bg, i: (bg, 0, 0)),
                  pl.BlockSpec((1, nc, HEAD_DIM), lambda bg, i: (bg, 0, 0)),
                  pl.BlockSpec((tq, LANES), lambda bg, i: ((bg // N_KV_GROUPS) * nq + i, 0)),
                  pl.BlockSpec((nc, LANES), lambda bg, i: (0, 0))],
        out_specs=[pl.BlockSpec((tq, gq), qmap),
                   pl.BlockSpec((tq, LANES), lambda bg, i: (bg * nq + i, 0))],
        out_shape=[jax.ShapeDtypeStruct((t, ATT_W), BF16),
                   jax.ShapeDtypeStruct((batch * N_KV_GROUPS * seq, LANES), BF16)],
        scratch_shapes=[pltpu.VMEM((nsel, tq), F32), pltpu.VMEM((nsel, tq), F32)],
        compiler_params=_cparams(("parallel", "parallel")),
        name="cmp_attn",
    )(proj, kc, vc, gates, jnp.asarray(ov))


def _selwin_kernel(q_ref, selb_ref, ocmp_ref, gates_ref, ks_ref, vs_ref, kw_ref, vw_ref,
                   cb_ref, wb_ref, o_ref, kaug_s, vaug_s, vwaug_s, qaug_s, sbuf_s, m_s, acc_s,
                   swin_s, *, tq, tk, seq, nwin):
    g = pl.program_id(0) % N_KV_GROUPS
    qi = pl.program_id(1)
    q0 = qi * tq
    hq = HEADS_PER_GROUP * tq
    half = hq // 2

    @pl.when(qi == 0)
    def _():
        r = lax.broadcasted_iota(jnp.int32, (seq, HEAD_DIM), 0)
        ln = lax.broadcasted_iota(jnp.int32, (seq, HEAD_DIM), 1)
        ones_col = jnp.where(ln == 0, 1.0, 0.0).astype(BF16)
        kaug_s[:, 0:HEAD_DIM] = ks_ref[...]
        kaug_s[:, HEAD_DIM:2 * HEAD_DIM] = jnp.where(r // SEL_BLOCK == ln, 1.0, 0.0).astype(BF16)
        vaug_s[:, 0:HEAD_DIM] = vs_ref[...]
        vaug_s[:, HEAD_DIM:2 * HEAD_DIM] = ones_col
        vwaug_s[:, 0:HEAD_DIM] = vw_ref[...]
        vwaug_s[:, HEAD_DIM:2 * HEAD_DIM] = ones_col

    for h in range(HEADS_PER_GROUP):
        qaug_s[h * tq:(h + 1) * tq, 0:HEAD_DIM] = q_ref[:, h * HEAD_DIM:(h + 1) * HEAD_DIM]
        qaug_s[h * tq:(h + 1) * tq, HEAD_DIM:2 * HEAD_DIM] = selb_ref[...]
    qa = qaug_s[...]

    def scores(kt, slot):
        ks = kaug_s[pl.ds(pl.multiple_of(kt * tk, tk), tk), :]
        sbuf_s[slot] = _dot_nt(qa, ks)

    def absorb(kt, slot, diagonal=False):
        vs = vaug_s[pl.ds(pl.multiple_of(kt * tk, tk), tk), :]
        s = sbuf_s[slot]
        if diagonal:
            bias = cb_ref[(q0 - kt * tk) // tq]
            s = (s.reshape(HEADS_PER_GROUP, tq, tk) + bias[None]).reshape(hq, tk)
        m = m_s[...]
        m_new = jnp.maximum(m, jnp.max(s, axis=-1, keepdims=True))
        m_s[...] = m_new
        pb = jnp.exp2(s - m_new[:, 0:1]).astype(BF16)
        if diagonal:
            pv = jnp.concatenate([_dot(pb[0:half], vs), _dot(pb[half:hq], vs)], axis=0)
        else:
            pv = _dot(pb, vs)
        acc_s[...] = jnp.exp2(m - m_new)[:, 0:1] * acc_s[...] + pv

    n_full = q0 // tk
    m_s[...] = jnp.full(m_s.shape, -jnp.inf, F32)
    acc_s[...] = jnp.zeros(acc_s.shape, F32)
    scores(0, 0)

    def pair(j, c):
        scores(2 * j + 1, 1)
        absorb(2 * j, 0)
        scores(2 * j + 2, 0)
        absorb(2 * j + 1, 1)
        return c

    lax.fori_loop(0, n_full // 2, pair, 0)
    odd = n_full % 2

    @pl.when(odd == 1)
    def _():
        scores(n_full, 1)
        absorb(n_full - 1, 0)

    nq = seq // tq
    w0 = jnp.clip(qi - (nwin - 1), 0, nq - nwin) * tq
    wlen = nwin * tq
    kwin = kw_ref[pl.ds(pl.multiple_of(w0, tq), wlen), :]
    vwin = vwaug_s[pl.ds(pl.multiple_of(w0, tq), wlen), :]
    swin_s[0:half, :] = _dot_nt(qaug_s[0:half, 0:HEAD_DIM], kwin)
    swin_s[half:hq, :] = _dot_nt(qaug_s[half:hq, 0:HEAD_DIM], kwin)
    absorb(n_full, odd, diagonal=True)
    acc = acc_s[...]
    o_sel = acc[:, 0:HEAD_DIM] / acc[:, HEAD_DIM:HEAD_DIM + 1]

    wbias = wb_ref[(q0 - w0) // tq]
    sw = (swin_s[...].reshape(HEADS_PER_GROUP, tq, wlen) + wbias[None]).reshape(hq, wlen)
    pwb = jnp.exp2(sw - jnp.max(sw, axis=-1, keepdims=True)).astype(BF16)
    ow = jnp.concatenate([_dot(pwb[0:half], vwin), _dot(pwb[half:hq], vwin)], axis=0)
    o_win = ow[:, 0:HEAD_DIM] / ow[:, HEAD_DIM:HEAD_DIM + 1]

    gl = gates_ref[...]
    lane = lax.broadcasted_iota(jnp.int32, (tq, LANES), 1)
    for h in range(HEADS_PER_GROUP):
        gcol = (g * HEADS_PER_GROUP + h) * 3
        g_s = _sigmoid(_lane_col(gl, lane, gcol + 1))
        g_w = _sigmoid(_lane_col(gl, lane, gcol + 2))
        o = (ocmp_ref[:, h * HEAD_DIM:(h + 1) * HEAD_DIM].astype(F32)
             + g_s * o_sel[h * tq:(h + 1) * tq, :] + g_w * o_win[h * tq:(h + 1) * tq, :])
        o_ref[:, h * HEAD_DIM:(h + 1) * HEAD_DIM] = o.astype(BF16)


def _selwin_attn(proj, selb, ocmp, gates, batch, seq, tq, tk):
    t = proj.shape[0]
    nq = seq // tq
    assert tk % tq == 0 and tq & (tq - 1) == 0 and seq % tk == 0
    nwin = min(WINDOW // tq + 1, nq)
    assert WINDOW % tq == 0
    gq = HEADS_PER_GROUP * HEAD_DIM
    hq = HEADS_PER_GROUP * tq
    r = np.arange(tq)[:, None]
    cbias = np.stack([np.where(np.arange(tk)[None, :] <= r + off * tq, 0.0, -np.inf)
                      for off in range(tk // tq)]).astype(np.float32)
    dist = [r + off * tq - np.arange(nwin * tq)[None, :] for off in range(nwin)]
    wbias = np.stack([np.where((d >= 0) & (d < WINDOW), 0.0, -np.inf)
                      for d in dist]).astype(np.float32)
    qmap = lambda bg, i: ((bg // N_KV_GROUPS) * nq + i, bg % N_KV_GROUPS)

    def kvspec(col):
        return pl.BlockSpec((seq, HEAD_DIM),
                            lambda bg, i: (bg // N_KV_GROUPS, col // HEAD_DIM + bg % N_KV_GROUPS))

    return pl.pallas_call(
        functools.partial(_selwin_kernel, tq=tq, tk=tk, seq=seq, nwin=nwin),
        grid=(batch * N_KV_GROUPS, nq),
        in_specs=[pl.BlockSpec((tq, gq), qmap),
                  pl.BlockSpec((tq, LANES), lambda bg, i: (bg * nq + i, 0)),
                  pl.BlockSpec((tq, gq), qmap),
                  pl.BlockSpec((tq, LANES), lambda bg, i: ((bg // N_KV_GROUPS) * nq + i, 0)),
                  kvspec(COL_KS), kvspec(COL_VS), kvspec(COL_KW), kvspec(COL_VW),
                  pl.BlockSpec(cbias.shape, lambda bg, i: (0, 0, 0)),
                  pl.BlockSpec(wbias.shape, lambda bg, i: (0, 0, 0))],
        out_specs=pl.BlockSpec((tq, gq), qmap),
        out_shape=jax.ShapeDtypeStruct((t, ATT_W), BF16),
        scratch_shapes=[pltpu.VMEM((seq, 2 * HEAD_DIM), BF16),
                        pltpu.VMEM((seq, 2 * HEAD_DIM), BF16),
                        pltpu.VMEM((seq, 2 * HEAD_DIM), BF16),
                        pltpu.VMEM((hq, 2 * HEAD_DIM), BF16),
                        pltpu.VMEM((2, hq, tk), F32),
                        pltpu.VMEM((hq, LANES), F32),
                        pltpu.VMEM((hq, 2 * HEAD_DIM), F32),
                        pltpu.VMEM((hq, nwin * tq), F32)],
        compiler_params=_cparams(("parallel", "arbitrary")),
        name="selwin_attn",
    )(proj, selb, ocmp, gates, proj, proj, proj, proj, jnp.asarray(cbias), jnp.asarray(wbias))


_HGRN_LEVELS = (32, 16, 8, 4, 2, 1)


def _hgrn_tables():
    c = HGRN_CHUNK
    nl = len(_HGRN_LEVELS)
    idx = np.arange(c)
    i, j = idx[:, None], idx[None, :]
    mats = [j <= i, j > i]
    lvl = np.full((c, c), nl + 1, np.int32)
    for li, s in enumerate(_HGRN_LEVELS):
        mats.append((j >= (i // s) * s) & (j <= i))
        mats.append((j > i) & (j <= (i // s) * s + s - 1))
        lvl[((i // (2 * s)) == (j // (2 * s))) & (((i // s) % 2) == 1) & (((j // s) % 2) == 0)] = li
    lvl[i == j] = nl
    w = np.concatenate(mats, 0).astype(np.float32)
    return np.concatenate([w, w], axis=1), lvl


def _hgrn_head(hq_ref, hf_ref, hi_ref, hg_ref, lbl_ref, gn_ref, w_ref, lvl_ref, o_ref, st_s,
               hh, ts):
    c = HGRN_CHUNK
    nc = ts // c
    nl = len(_HGRN_LEVELS)
    cs = slice(hh * HEAD_DIM, (hh + 1) * HEAD_DIM)
    lbl = lbl_ref[:, cs]
    e = jnp.exp(lbl - jnp.max(lbl, axis=0, keepdims=True))
    lb = e[0:1, :] / jnp.sum(e, axis=0, keepdims=True)

    f = lb + (1.0 - lb) * _sigmoid(hf_ref[:, cs].astype(F32))
    k = 1.0 - f
    qf = _silu(hq_ref[:, cs].astype(F32))
    v = hi_ref[:, cs]

    lf = jnp.log(f) * LOG2E
    lf_hi = lf.astype(BF16)
    lf_lo = (lf - lf_hi.astype(F32)).astype(BF16)
    side = lambda a: jnp.concatenate([a[ci * c:(ci + 1) * c, :] for ci in range(nc)], axis=1)
    ex = _dot(w_ref[...], jnp.concatenate([side(lf_hi), side(lf_lo)], axis=0))

    def rows(bi):
        return jnp.concatenate(
            [ex[bi * c:(bi + 1) * c, ci * HEAD_DIM:(ci + 1) * HEAD_DIM] for ci in range(nc)], axis=0)

    b = rows(0)
    qhat = (qf * jnp.exp2(b)).astype(BF16)
    khat = (k * jnp.exp2(rows(1))).astype(BF16)
    lvl = lvl_ref[...]
    masks = [lvl == li for li in range(nl + 1)]
    qs = [(qf * jnp.exp2(rows(2 + 2 * li))).astype(BF16) for li in range(nl)] + [qf.astype(BF16)]
    ks = [(k * jnp.exp2(rows(3 + 2 * li))).astype(BF16) for li in range(nl)] + [k.astype(BF16)]
    o_intra = []
    for ci in range(nc):
        sl = slice(ci * c, (ci + 1) * c)
        a = jnp.zeros((c, c), F32)
        for li in range(nl + 1):
            a = jnp.where(masks[li], _dot_nt(qs[li][sl, :], ks[li][sl, :]), a)
        o_intra.append(_dot(a.astype(BF16), v[sl, :]))

    st = st_s[hh]
    outs = []
    for ci in range(nc):
        sl = slice(ci * c, (ci + 1) * c)
        outs.append(o_intra[ci] + _dot_nt(qhat[sl, :], st.astype(BF16)))
        st = st * jnp.exp2(b[(ci + 1) * c - 1:(ci + 1) * c, :]) + lax.dot_general(
            v[sl, :], khat[sl, :], _TN, preferred_element_type=F32)
    st_s[hh] = st
    o = jnp.concatenate(outs, axis=0)
    o = o * lax.rsqrt(jnp.mean(o * o, axis=-1, keepdims=True) + EPS) * gn_ref[:, cs]
    o_ref[:, cs] = (o * _silu(hg_ref[:, cs].astype(F32))).astype(BF16)


def _hgrn_kernel(hq_ref, hf_ref, hi_ref, hg_ref, lbl_ref, gn_ref, w_ref, lvl_ref,
                 o_ref, st_s, *, ts, hb):
    @pl.when(pl.program_id(2) == 0)
    def _():
        st_s[...] = jnp.zeros_like(st_s)

    for hh in range(hb):
        _hgrn_head(hq_ref, hf_ref, hi_ref, hg_ref, lbl_ref, gn_ref, w_ref, lvl_ref, o_ref, st_s,
                   hh, ts)


def _hgrn(proj, lb_logits, g_norm, batch, seq, ts, hb):
    t = proj.shape[0]
    ns = seq // ts
    wmat, lvl = _hgrn_tables()
    w = hb * HEAD_DIM
    assert all(col % w == 0 for col in (COL_HQ, COL_HF, COL_HI, COL_HG))

    def colspec(col):
        return pl.BlockSpec((ts, w), lambda b, h, s: (b * ns + s, col // w + h))

    nl = lb_logits.shape[0]
    return pl.pallas_call(
        functools.partial(_hgrn_kernel, ts=ts, hb=hb),
        grid=(batch, N_HGRN_HEADS // hb, ns),
        in_specs=[colspec(COL_HQ), colspec(COL_HF), colspec(COL_HI), colspec(COL_HG),
                  pl.BlockSpec((nl, w), lambda b, h, s: (0, h)),
                  pl.BlockSpec((1, w), lambda b, h, s: (0, h)),
                  pl.BlockSpec(wmat.shape, lambda b, h, s: (0, 0)),
                  pl.BlockSpec(lvl.shape, lambda b, h, s: (0, 0))],
        out_specs=pl.BlockSpec((ts, w), lambda b, h, s: (b * ns + s, h)),
        out_shape=jax.ShapeDtypeStruct((t, ATT_W), BF16),
        scratch_shapes=[pltpu.VMEM((hb, HEAD_DIM, HEAD_DIM), F32)],
        compiler_params=_cparams(("parallel", "parallel", "arbitrary")),
        name="hgrn2",
    )(proj, proj, proj, proj, lb_logits, g_norm, jnp.asarray(wmat, BF16), jnp.asarray(lvl))


def _outproj_kernel(oa_ref, or_ref, wa_ref, wb_ref, x_ref, mod_ref, gpost_ref, gpre_ref,
                    wr_ref, br_ref, x1_ref, h2_ref, lg_ref, stage, *, per_b):
    b = pl.program_id(0) // per_b
    y = _dot(oa_ref[...], wa_ref[...]) + _dot(or_ref[...], wb_ref[...])
    x1 = x_ref[...] + _mod_row(mod_ref, b, 2) * _rms(y, gpost_ref[...])
    x1_ref[...] = x1
    h = _rms(x1, gpre_ref[...]) * (1.0 + _mod_row(mod_ref, b, 4)) + _mod_row(mod_ref, b, 3)
    h2_ref[...] = _to_slabs(stage, h)
    h_hi = h.astype(BF16)
    h_lo = (h - h_hi.astype(F32)).astype(BF16)
    hh = _dot(h_hi, wr_ref[...])
    lg_ref[...] = (hh[:, 0:LANES] + hh[:, LANES:2 * LANES]
                   + _dot(h_lo, wr_ref[:, 0:LANES]) + br_ref[...])


def _outproj(o_att, o_rec, w_out, x2, mod, g_post, g_pre, wr, br, seq, tm):
    t, d = x2.shape
    per_b = seq // tm
    wr_hi = wr.astype(BF16)
    wr_cat = jnp.concatenate([wr_hi, (wr - wr_hi.astype(F32)).astype(BF16)], axis=1)
    wa = w_out[:ATT_W].astype(BF16)
    wb = w_out[ATT_W:].astype(BF16)
    row = lambda i: (i, 0)
    fixed = lambda i: (0, 0)
    return pl.pallas_call(
        functools.partial(_outproj_kernel, per_b=per_b),
        grid=(t // tm,),
        in_specs=[pl.BlockSpec((tm, ATT_W), row), pl.BlockSpec((tm, ATT_W), row),
                  pl.BlockSpec((ATT_W, d), fixed), pl.BlockSpec((ATT_W, d), fixed),
                  pl.BlockSpec((tm, d), row),
                  pl.BlockSpec(mod.shape, fixed),
                  pl.BlockSpec((1, d), fixed), pl.BlockSpec((1, d), fixed),
                  pl.BlockSpec((d, 2 * LANES), fixed), pl.BlockSpec((1, LANES), fixed)],
        out_specs=[pl.BlockSpec((tm, d), row), pl.BlockSpec((tm * (d // LANES), LANES), row),
                   pl.BlockSpec((tm, LANES), row)],
        out_shape=[jax.ShapeDtypeStruct((t, d), F32),
                   jax.ShapeDtypeStruct((t * (d // LANES), LANES), BF16),
                   jax.ShapeDtypeStruct((t, LANES), F32)],
        scratch_shapes=[pltpu.VMEM((tm * (d // LANES), LANES), F32)],
        compiler_params=_cparams(("parallel",)),
        name="outproj",
    )(o_att, o_rec, wa, wb, x2, mod, g_post, g_pre, wr_cat, br)


G_LANE0 = N_EXPERTS


def _route_kernel(lg_ref, tri_ref, info_ref, cnt_ref, carry_s):
    @pl.when(pl.program_id(0) == 0)
    def _():
        carry_s[...] = jnp.zeros_like(carry_s)

    x = lg_ref[...]
    lane = lax.broadcasted_iota(jnp.int32, x.shape, 1)
    big = jnp.int32(10 ** 6)
    rmax = lambda a: jnp.max(a, axis=-1, keepdims=True)
    rmin = lambda a: jnp.min(a, axis=-1, keepdims=True)
    rsum = lambda a: jnp.sum(a, axis=-1, keepdims=True)

    is_g = (lane >= G_LANE0) & (lane < G_LANE0 + N_EXPERT_GROUPS)
    lgm = jnp.where(is_g, x, -jnp.inf)
    mg = rmax(lgm)
    pg_sel = 1.0 / rsum(jnp.where(is_g, jnp.exp(lgm - mg), 0.0))
    gsel = rmin(jnp.where(lgm == mg, lane, big)) - G_LANE0

    is_e = (lane >= gsel * EXPERTS_PER_GROUP) & (lane < (gsel + 1) * EXPERTS_PER_GROUP)
    lem = jnp.where(is_e, x, -jnp.inf)
    pe = jnp.where(is_e, jnp.exp(lem - rmax(lem)), 0.0)
    pe = pe / rsum(pe)
    pe = jnp.where(is_e, pe, -1.0)
    v1 = rmax(pe)
    i1 = rmin(jnp.where(pe == v1, lane, big))
    pe2 = jnp.where(lane == i1, -1.0, pe)
    v2 = rmax(pe2)
    i2 = rmin(jnp.where(pe2 == v2, lane, big))
    w1 = v1 / (v1 + v2) * pg_sel
    w2 = v2 / (v1 + v2) * pg_sel

    oh1 = jnp.where(lane == i1, 1.0, 0.0)
    oh2 = jnp.where(lane == i2, 1.0, 0.0)
    both = oh1 + oh2
    before = _dot(tri_ref[...], both.astype(BF16)) + carry_s[0:1, :]
    r1 = rsum(oh1 * before)
    r2 = rsum(oh2 * before)
    carry_s[0:1, :] = carry_s[0:1, :] + jnp.sum(both, axis=0, keepdims=True)
    cnt_ref[...] = carry_s[...]

    cols = (i1.astype(F32), i2.astype(F32), w1, w2, r1, r2)
    info = jnp.zeros(x.shape, F32)
    for ci, col in enumerate(cols):
        info = jnp.where(lane == ci, col, info)
    info_ref[...] = info


def _route(logits, tm):
    t = logits.shape[0]
    tri = jnp.asarray(np.tril(np.ones((tm, tm), np.float32), -1), BF16)
    return pl.pallas_call(
        _route_kernel,
        grid=(t // tm,),
        in_specs=[pl.BlockSpec((tm, LANES), lambda i: (i, 0)),
                  pl.BlockSpec((tm, tm), lambda i: (0, 0))],
        out_specs=[pl.BlockSpec((tm, LANES), lambda i: (i, 0)),
                   pl.BlockSpec((8, LANES), lambda i: (0, 0))],
        out_shape=[jax.ShapeDtypeStruct((t, LANES), F32),
                   jax.ShapeDtypeStruct((8, LANES), F32)],
        scratch_shapes=[pltpu.VMEM((8, LANES), F32)],
        compiler_params=_cparams(("arbitrary",)),
        name="route",
    )(logits, tri)


def _to_slabs(stage, x):
    n, d = x.shape
    slab = d // LANES
    for c in range(slab):
        stage[pl.ds(c, n, stride=slab), :] = x[:, c * LANES:(c + 1) * LANES]
    return stage[...].astype(BF16)


def _from_slabs(stage, slabs, n):
    slab = slabs.shape[0] // n
    stage[...] = slabs.astype(F32)
    return jnp.concatenate([stage[pl.ds(c, n, stride=slab), :] for c in range(slab)], axis=1)


def _expert_kernel(te_ref, nu_ref, par_ref, nxt_ref, dst_ref, xs_ref, wg_hbm, wu_hbm, wd_hbm,
                   y_ref, obuf, stage, ssem, wg_f, wu_f, wd_f, wsem, wg_s, wu_s, wd_s, *,
                   tm, standin_row):
    t = pl.program_id(0)
    nu = nu_ref[0]
    slot = t % 2

    def weights(e, buf):
        return [pltpu.make_async_copy(src.at[e], dst.at[buf], wsem.at[buf])
                for src, dst in ((wg_hbm, wg_f), (wu_hbm, wu_f), (wd_hbm, wd_f))]
    slab = wg_s.shape[0] // LANES
    tile = tm * slab

    def token(ref, i):
        return ref.at[pl.ds(pl.multiple_of(i * slab, slab), slab)]

    def scatter(s):
        for r in range(tm):
            pltpu.make_async_copy(token(obuf, s * tm + r), token(y_ref, dst_ref[0, 0, r]),
                                  ssem.at[s]).start()

    def tile_of(buf, s):
        return buf.at[pl.ds(pl.multiple_of(s * tile, tile), tile)]

    def wait_tile(s):
        pltpu.make_async_copy(tile_of(obuf, s), tile_of(obuf, s), ssem.at[s]).wait()

    @pl.when(t < nu)
    def _():
        @pl.when(t == 0)
        def _():
            obuf[tile:2 * tile, :] = jnp.zeros((tile, LANES), obuf.dtype)

            def fill(row0, s):
                return pltpu.make_async_copy(tile_of(obuf, 1),
                                             y_ref.at[pl.ds(row0 * slab, tile)], ssem.at[s])

            fill(standin_row - 2 * tm, 0).start()
            fill(standin_row - tm, 0).start()
            fill(standin_row - 2 * tm, 0).wait()
            fill(standin_row - tm, 0).wait()
            fill(standin_row, 1).start()

        par = par_ref[t]

        @pl.when(t == 0)
        def _():
            for cp in weights(te_ref[0], par):
                cp.start()

        @pl.when((t == 0) | (te_ref[t] != te_ref[jnp.maximum(t - 1, 0)]))
        def _():
            for cp in weights(te_ref[t], par):
                cp.wait()
            wg_s[...] = wg_f[par].astype(BF16)
            wu_s[...] = wu_f[par].astype(BF16)
            wd_s[...] = wd_f[par].astype(BF16)

            @pl.when(nxt_ref[t] != te_ref[t])
            def _():
                for cp in weights(nxt_ref[t], 1 - par):
                    cp.start()

        x = _from_slabs(stage, xs_ref[...], tm).astype(BF16)
        hid = (_silu(_dot(x, wg_s[...])) * _dot(x, wu_s[...])).astype(BF16)
        row0 = pl.multiple_of(slot * tile, tile)
        obuf[pl.ds(row0, tile), :] = _to_slabs(stage, _dot(hid, wd_s[...]))
        scatter(slot)
        wait_tile(1 - slot)

        @pl.when(t == nu - 1)
        def _():
            wait_tile(slot)


def _experts(tile_expert, n_used, parity, next_expert, dst3, xs, w_gate, w_up, w_down, y_rows,
             tm, standin_row):
    n_tiles = dst3.shape[0]
    d, f = w_gate.shape[1:]
    slab = d // LANES
    hbm = pl.BlockSpec(memory_space=pl.ANY)
    return pl.pallas_call(
        functools.partial(_expert_kernel, tm=tm, standin_row=standin_row),
        grid_spec=pltpu.PrefetchScalarGridSpec(
            num_scalar_prefetch=4,
            grid=(n_tiles,),
            in_specs=[pl.BlockSpec((1, 1, tm), lambda t, te, nu, pa, nx: (t, 0, 0),
                                   memory_space=pltpu.SMEM),
                      pl.BlockSpec((tm * slab, LANES),
                                   lambda t, te, nu, pa, nx: (jnp.minimum(t, nu[0] - 1), 0)),
                      hbm, hbm, hbm],
            out_specs=pl.BlockSpec(memory_space=pl.ANY),
            scratch_shapes=[pltpu.VMEM((2 * tm * slab, LANES), BF16),
                            pltpu.VMEM((tm * slab, LANES), F32),
                            pltpu.SemaphoreType.DMA((2,)),
                            pltpu.VMEM((2, d, f), F32), pltpu.VMEM((2, d, f), F32),
                            pltpu.VMEM((2, f, d), F32),
                            pltpu.SemaphoreType.DMA((2,)),
                            pltpu.VMEM((d, f), BF16), pltpu.VMEM((d, f), BF16),
                            pltpu.VMEM((f, d), BF16)]),
        out_shape=jax.ShapeDtypeStruct((y_rows * slab, LANES), BF16),
        compiler_params=_cparams(("arbitrary",)),
        name="moe_experts",
    )(tile_expert, n_used, parity, next_expert, dst3, xs, w_gate, w_up, w_down)


def _dispatch_kernel(pos0_ref, pos1_ref, end_ref, nu_ref, h_ref, dst_in, xs_ref, dst_ref,
                     zbuf, zsem, sem, *, tm, slab, n_tiles, n_tokens):
    tile = tm * slab
    base = pl.program_id(0) * tm

    @pl.when(pl.program_id(0) == 0)
    def _():
        init = pltpu.make_async_copy(dst_in, dst_ref, zsem)
        init.start()
        init.wait()
        zbuf[...] = jnp.zeros(zbuf.shape, zbuf.dtype)

        def zero(row_end, go):
            cp = pltpu.make_async_copy(
                zbuf, xs_ref.at[pl.ds(pl.multiple_of((row_end - tm) * slab, slab), tile)], zsem)
            pl.when(go)(cp.start)
            return cp, go

        pending = [zero(end_ref[e], end_ref[e] > (end_ref[e - 1] if e else 0))
                   for e in range(N_EXPERTS)]
        pending += [zero((i + 1) * tm, i >= nu_ref[0]) for i in range(n_tiles)]
        for cp, go in pending:
            pl.when(go)(cp.wait)

    def token(ref, i):
        return ref.at[pl.ds(pl.multiple_of(i * slab, slab), slab)]

    def start(r, c):
        p0 = pos0_ref[0, 0, r]
        p1 = pos1_ref[0, 0, r]
        pltpu.make_async_copy(token(h_ref, r), token(xs_ref, p0), sem).start()
        pltpu.make_async_copy(token(h_ref, r), token(xs_ref, p1), sem).start()
        dst_ref[p0] = base + r
        dst_ref[p1] = n_tokens + base + r
        return c

    lax.fori_loop(0, tm, start, 0, unroll=8)
    for _ in range(2):
        pltpu.make_async_copy(h_ref, h_ref, sem).wait()


def _dispatch(pos0, pos1, tile_end, n_used, h2, dst0, n_tiles, tm):
    rows = h2.shape[0]
    slab = rows // pos0.shape[0]
    t = pos0.shape[0]
    per_tile = pl.BlockSpec((1, 1, tm), lambda i: (i, 0, 0), memory_space=pltpu.SMEM)
    whole = pl.BlockSpec(memory_space=pltpu.SMEM)
    return pl.pallas_call(
        functools.partial(_dispatch_kernel, tm=tm, slab=slab, n_tiles=n_tiles, n_tokens=t),
        grid=(t // tm,),
        in_specs=[per_tile, per_tile, whole, whole,
                  pl.BlockSpec((tm * slab, LANES), lambda i: (i, 0)),
                  pl.BlockSpec(memory_space=pl.ANY)],
        out_specs=[pl.BlockSpec(memory_space=pl.ANY), whole],
        out_shape=[jax.ShapeDtypeStruct((n_tiles * tm * slab, LANES), h2.dtype),
                   jax.ShapeDtypeStruct(dst0.shape, jnp.int32)],
        scratch_shapes=[pltpu.VMEM((tm * slab, LANES), h2.dtype),
                        pltpu.SemaphoreType.DMA(()), pltpu.SemaphoreType.DMA(())],
        compiler_params=_cparams(("arbitrary",)),
        name="moe_dispatch",
    )(pos0.reshape(t // tm, 1, tm), pos1.reshape(t // tm, 1, tm), tile_end * tm, n_used, h2,
      dst0)


def _final_kernel(info_ref, x1_ref, mod_ref, g_ref, ya_ref, yb_ref, o_ref, stage_a, stage_b, *,
                  per_b):
    info = info_ref[...]
    tm = x1_ref.shape[0]
    ya = _from_slabs(stage_a, ya_ref[...], tm)
    yb = _from_slabs(stage_b, yb_ref[...], tm)
    y = info[:, 2:3] * ya + info[:, 3:4] * yb
    gain = _mod_row(mod_ref, pl.program_id(0) // per_b, 5) * g_ref[...]
    inv = lax.rsqrt(jnp.mean(y * y, axis=-1, keepdims=True) + EPS)
    o_ref[...] = x1_ref[...] + (y * inv) * gain


def _moe_final(info, x1, mod, g_post, y2, seq, tm):
    t, d = x1.shape
    per_b = seq // tm
    nb = t // tm
    return pl.pallas_call(
        functools.partial(_final_kernel, per_b=per_b),
        grid=(nb,),
        in_specs=[pl.BlockSpec((tm, LANES), lambda i: (i, 0)),
                  pl.BlockSpec((tm, d), lambda i: (i, 0)),
                  pl.BlockSpec(mod.shape, lambda i: (0, 0)),
                  pl.BlockSpec((1, d), lambda i: (0, 0)),
                  pl.BlockSpec((tm * (d // LANES), LANES), lambda i: (i, 0)),
                  pl.BlockSpec((tm * (d // LANES), LANES), lambda i: (nb + i, 0))],
        out_specs=pl.BlockSpec((tm, d), lambda i: (i, 0)),
        out_shape=jax.ShapeDtypeStruct((t, d), F32),
        scratch_shapes=[pltpu.VMEM((tm * (d // LANES), LANES), F32)] * 2,
        compiler_params=_cparams(("parallel",)),
        name="moe_final",
    )(info, x1, mod, g_post, y2, y2)


def _moe(h2, logits, x1, mod, g_post, w_gate, w_up, w_down, seq, tm_route, tm_exp, tm_fin):
    t = x1.shape[0]
    info, cnt = _route(logits, tm_route)
    counts = cnt[0, :N_EXPERTS].astype(jnp.int32)
    tiles_e = (counts + tm_exp - 1) // tm_exp
    eidx = jnp.arange(N_EXPERTS, dtype=jnp.int32)
    tile_end = jnp.sum(jnp.where(eidx[None, :] <= eidx[:, None], tiles_e[None, :], 0), axis=1)
    offs = (tile_end - tiles_e) * tm_exp
    n_used = jnp.sum(tiles_e)

    def slot_of(k):
        e = info[:, k].astype(jnp.int32)
        start = jnp.sum(jnp.where(e[:, None] == jnp.arange(N_EXPERTS)[None, :], offs[None, :], 0),
                        axis=1)
        return start + info[:, 4 + k].astype(jnp.int32)

    n_tiles = (2 * t) // tm_exp + N_EXPERTS
    tidx = jnp.minimum(jnp.arange(n_tiles, dtype=jnp.int32), n_used - 1)
    tile_expert = jnp.sum(tidx[:, None] >= tile_end[None, :], axis=1).astype(jnp.int32)
    used = (tiles_e > 0)[None, :]
    parity = jnp.sum(used & (eidx[None, :] < tile_expert[:, None]), axis=1) % 2
    later = jnp.min(jnp.where(used & (eidx[None, :] > tile_expert[:, None]), eidx[None, :],
                              N_EXPERTS), axis=1)
    next_expert = jnp.where(later == N_EXPERTS, tile_expert, later)
    pos0, pos1 = slot_of(0), slot_of(1)
    slot_ids = jnp.arange(n_tiles * tm_exp, dtype=jnp.int32)
    spare = 2 * t + ((slot_ids // tm_exp) % 2) * tm_exp + slot_ids % tm_exp
    n_used1 = n_used.reshape(1).astype(jnp.int32)
    xs, dst = _dispatch(pos0, pos1, tile_end.astype(jnp.int32), n_used1, h2, spare, n_tiles,
                        tm_exp)

    ew = w_gate.shape
    wg = w_gate.reshape(N_EXPERTS, ew[-2], ew[-1])
    wu = w_up.reshape(N_EXPERTS, ew[-2], ew[-1])
    wd = w_down.reshape(N_EXPERTS, ew[-1], ew[-2])
    y2 = _experts(tile_expert, n_used1, parity.astype(jnp.int32), next_expert.astype(jnp.int32),
                  dst.reshape(n_tiles, 1, tm_exp), xs, wg, wu, wd,
                  2 * t + 3 * tm_exp, tm_exp, 2 * t + 2 * tm_exp)
    return _moe_final(info, x1, mod, g_post, y2, seq, tm_fin)


def _permute_w_in(w_in):
    sizes = [ATT_W] + [KV_W] * 6 + [3 * N_ATT_HEADS] + [ATT_W] * 4
    cuts = np.cumsum(sizes)[:-1]
    q, kc, vc, ks, vs, kw, vw, gates, hq, hf, hi, hg = [
        p.astype(BF16) for p in jnp.split(w_in, cuts, axis=1)]
    w_p = jnp.concatenate([q, kc, ks, kw, vc, vs, vw, hq, hf, hi, hg], axis=1)
    w_g = jnp.pad(gates, ((0, 0), (0, LANES - gates.shape[1])))
    return w_p, w_g


def _block(x, c, positions, w_ada, b_ada, g_pre_mix, g_post_mix, g_pre_ffn, g_post_ffn,
           w_in, w_out, pe_k, w1_k, w2_k, pe_v, w1_v, w2_v, lb_logits, g_norm,
           w_group, b_group, w_router, b_router, w_gate, w_up, w_down, cfg):
    batch, seq, d = x.shape
    t = batch * seq
    x2 = x.reshape(t, d)

    c8 = jnp.zeros((8, d), F32).at[:batch].set(c)
    mod = _ada_mod(c8, w_ada, b_ada)
    cos_t, sin_t = _rope_tables(positions, cfg["tm_rope"])
    w_p, w_g = _permute_w_in(w_in)
    proj, gates, hbk, hbv = _inproj(x2, mod, g_pre_mix.reshape(1, d), w_p, w_g, cos_t, sin_t,
                                    seq, cfg["tm_in"])

    half_blocks = (batch * N_KV_GROUPS, seq // CMP_STRIDE, CMP_STRIDE * HEAD_DIM)
    kc = _compress(hbk.reshape(half_blocks), w1_k, pe_k, w2_k)
    vc = _compress(hbv.reshape(half_blocks), w1_v, pe_v, w2_v)
    ocmp, selb = _cmp_attn(proj, kc, vc, gates, batch, seq, cfg["tq_cmp"])
    o_att = _selwin_attn(proj, selb, ocmp, gates, batch, seq, cfg["tq"], cfg["tk"])
    o_rec = _hgrn(proj, lb_logits, g_norm.reshape(1, -1), batch, seq, cfg["ts_hgrn"],
                  cfg["hb_hgrn"])

    wr = jnp.concatenate([w_router, w_group], axis=1)
    wr = jnp.pad(wr, ((0, 0), (0, LANES - wr.shape[1])))
    br = jnp.pad(jnp.concatenate([b_router, b_group]), (0, LANES - N_EXPERTS - N_EXPERT_GROUPS))
    x1, h2, logits = _outproj(o_att, o_rec, w_out, x2, mod, g_post_mix.reshape(1, d),
                              g_pre_ffn.reshape(1, d), wr, br.reshape(1, LANES), seq,
                              cfg["tm_out"])
    out = _moe(h2, logits, x1, mod, g_post_ffn.reshape(1, d), w_gate, w_up, w_down, seq,
               cfg["tm_route"], cfg["tm_exp"], cfg["tm_fin"])
    return out.reshape(batch, seq, d)


def _config(seq):
    return dict(tm_rope=min(1024, seq), tm_in=min(1024, seq), tq_cmp=min(1024, seq),
                tq=min(256, seq), tk=min(512, seq), ts_hgrn=min(512, seq), hb_hgrn=4,
                tm_out=min(512, seq), tm_route=min(1024, seq), tm_exp=256,
                tm_fin=min(512, seq))


def kernel(x, c, positions, w_ada, b_ada, g_pre_mix, g_post_mix, g_pre_ffn, g_post_ffn, w_in, w_out, cmp_pe_k, cmp_w1_k, cmp_w2_k, cmp_pe_v, cmp_w1_v, cmp_w2_v, hgrn_lb_logits, hgrn_g_norm, w_group, b_group, w_router, b_router, w_gate, w_up, w_down):
    assert w_ada.shape[0] == 1, "single-layer block"
    return _block(x, c, positions, w_ada[0], b_ada[0], g_pre_mix[0], g_post_mix[0],
                  g_pre_ffn[0], g_post_ffn[0], w_in[0], w_out[0], cmp_pe_k[0], cmp_w1_k[0],
                  cmp_w2_k[0], cmp_pe_v[0], cmp_w1_v[0], cmp_w2_v[0], hgrn_lb_logits,
                  hgrn_g_norm[0], w_group[0], b_group[0], w_router[0], b_router[0],
                  w_gate[0], w_up[0], w_down[0], _config(x.shape[1]))
```

```python
import functools

import numpy as np
import jax
import jax.numpy as jnp
from jax import lax
from jax.experimental import pallas as pl
from jax.experimental.pallas import tpu as pltpu

F32 = jnp.float32
BF16 = jnp.bfloat16

HEAD_DIM = 128
N_KV_GROUPS = 2
HEADS_PER_GROUP = 4
N_ATT_HEADS = N_KV_GROUPS * HEADS_PER_GROUP
CMP_BLOCK = 32
CMP_STRIDE = 16
SEL_BLOCK = 64
N_SEL = 16
WINDOW = 512
FORCE_BONUS = 1.0e4
ROPE_THETA = 10000.0
N_HGRN_HEADS = 8
HGRN_CHUNK = 64
N_EXPERT_GROUPS = 4
EXPERTS_PER_GROUP = 8
N_EXPERTS = N_EXPERT_GROUPS * EXPERTS_PER_GROUP
EPS = 1e-6

LANES = 128
SUBLANES = 8
BF16_TILE_ROWS = 16
VMEM_LIMIT = 56 * 1024 * 1024

ATT_W = N_ATT_HEADS * HEAD_DIM
KV_W = N_KV_GROUPS * HEAD_DIM
COL_Q = 0
COL_KC = COL_Q + ATT_W
COL_KS = COL_KC + KV_W
COL_KW = COL_KS + KV_W
COL_VC = COL_KW + KV_W
COL_VS = COL_VC + KV_W
COL_VW = COL_VS + KV_W
COL_HQ = COL_VW + KV_W
COL_HF = COL_HQ + ATT_W
COL_HI = COL_HF + ATT_W
COL_HG = COL_HI + ATT_W
PROJ_W = COL_HG + ATT_W
ROPE_W = COL_VC

MASK_BIG = float(2 ** 30)
LOG2E = 1.4426950408889634
Q_PRESCALE = HEAD_DIM ** -0.5 * LOG2E

_NT = (((1,), (1,)), ((), ()))
_TN = (((0,), (0,)), ((), ()))


def _cparams(sem):
    return pltpu.CompilerParams(dimension_semantics=sem, vmem_limit_bytes=VMEM_LIMIT)


def _dot(a, b, **kw):
    return jnp.dot(a, b, preferred_element_type=F32, **kw)


def _dot_nt(a, b):
    return lax.dot_general(a, b, _NT, preferred_element_type=F32)


def _sigmoid(x):
    return 1.0 / (1.0 + jnp.exp(-x))


def _silu(x):
    return x * _sigmoid(x)


def _rms(x, g):
    return x * lax.rsqrt(jnp.mean(x * x, axis=-1, keepdims=True) + EPS) * g


def _mod_row(mod_ref, b, k):
    d = mod_ref.shape[1] // 6
    return mod_ref[pl.ds(b, 1), k * d:(k + 1) * d]


def _lane_col(x, lane, col):
    return jnp.sum(jnp.where(lane == col, x, 0.0), axis=-1, keepdims=True)


def _ada_kernel(c_ref, w_ref, b_ref, o_ref):
    s = _silu(c_ref[...])
    s_hi = s.astype(BF16)
    s_lo = (s - s_hi.astype(F32)).astype(BF16)
    w = w_ref[...]
    w_hi = w.astype(BF16)
    w_lo = (w - w_hi.astype(F32)).astype(BF16)
    o_ref[...] = _dot(s_hi, w_hi) + _dot(s_lo, w_hi) + _dot(s_hi, w_lo) + b_ref[...]


def _ada_mod(c8, w_ada, b_ada, tn=512):
    rows, d = c8.shape
    n = w_ada.shape[1]
    return pl.pallas_call(
        _ada_kernel,
        grid=(n // tn,),
        in_specs=[pl.BlockSpec((rows, d), lambda j: (0, 0)),
                  pl.BlockSpec((d, tn), lambda j: (0, j)),
                  pl.BlockSpec((1, tn), lambda j: (0, j))],
        out_specs=pl.BlockSpec((rows, tn), lambda j: (0, j)),
        out_shape=jax.ShapeDtypeStruct((rows, n), F32),
        compiler_params=_cparams(("parallel",)),
        name="ada_mod",
    )(c8, w_ada, b_ada.reshape(1, n))


def _rope_kernel(ang_ref, cos_ref, sin_ref):
    a = ang_ref[...]
    n = a.shape[0]
    half = HEAD_DIM // 2
    lane = lax.broadcasted_iota(jnp.int32, a.shape, 1)
    first = lane < half
    c = jnp.cos(a)
    s = jnp.sin(a)
    c_sw = pltpu.roll(c, half, axis=1)
    s_sw = pltpu.roll(s, half, axis=1)
    cos_ref[pl.ds(0, n, stride=2), :] = jnp.where(first, c, c_sw)
    cos_ref[pl.ds(1, n, stride=2), :] = jnp.where(first, c_sw, c)
    sin_ref[pl.ds(0, n, stride=2), :] = jnp.where(first, -s, s_sw)
    sin_ref[pl.ds(1, n, stride=2), :] = jnp.where(first, -s_sw, s)


def _rope_tables(positions, tm):
    t = positions.size
    inv_freq = ROPE_THETA ** (-jnp.arange(0, HEAD_DIM, 2, dtype=F32) / HEAD_DIM)
    pos = positions.reshape(t).astype(F32)
    ang = jnp.concatenate([pos[0::2, None] * inv_freq[None, :], pos[1::2, None] * inv_freq[None, :]],
                          axis=1)
    spec = pl.BlockSpec((tm, HEAD_DIM), lambda i: (i, 0))
    return pl.pallas_call(
        _rope_kernel,
        grid=(t // tm,),
        in_specs=[pl.BlockSpec((tm // 2, HEAD_DIM), lambda i: (i, 0))],
        out_specs=[spec, spec],
        out_shape=[jax.ShapeDtypeStruct((t, HEAD_DIM), F32)] * 2,
        compiler_params=_cparams(("parallel",)),
        name="rope_tables",
    )(ang)


def _inproj_kernel(x_hbm, mod_ref, g_ref, w_ref, wg_ref, cos_ref, sin_ref,
                   proj_ref, gates_ref, hbk_ref, hbv_ref, h_s, stage, x_s, xsem, *, tn, per_b):
    i = pl.program_id(0)
    j = pl.program_id(1)
    rows = h_s.shape[0]

    def x_copy(row_tile):
        return pltpu.make_async_copy(x_hbm.at[pl.ds(row_tile * rows, rows)], x_s, xsem.at[0])

    @pl.when(j == 0)
    def _():
        @pl.when(i == 0)
        def _():
            x_copy(0).start()

        x_copy(i).wait()
        b = i // per_b
        h = (_rms(x_s[...], g_ref[...]) * (1.0 + _mod_row(mod_ref, b, 1))
             + _mod_row(mod_ref, b, 0))
        hb = h.astype(BF16)
        h_s[...] = hb
        gates_ref[...] = _dot(hb, wg_ref[...])

        @pl.when(i + 1 < pl.num_programs(0))
        def _():
            x_copy(i + 1).start()

    heads = tn // HEAD_DIM

    def store(jt):
        acc = _dot(h_s[...], w_ref[...])
        for c in range(heads):
            a = acc[:, c * HEAD_DIM:(c + 1) * HEAD_DIM]
            col = None if jt is None else jt * tn + c * HEAD_DIM
            if col is not None and col < ROPE_W:
                a = a * cos_ref[...] + pltpu.roll(a, HEAD_DIM // 2, axis=1) * sin_ref[...]
            if col is not None and col < ATT_W:
                a = a * Q_PRESCALE
            proj_ref[:, c * HEAD_DIM:(c + 1) * HEAD_DIM] = a.astype(BF16)
            for col0, hb_ref in ((COL_KC, hbk_ref), (COL_VC, hbv_ref)):
                if col is not None and col0 <= col < col0 + KV_W:
                    stage[...] = a
                    hb_ref[0, (col - col0) // HEAD_DIM] = jnp.concatenate(
                        [stage[pl.ds(i, rows // CMP_STRIDE, stride=CMP_STRIDE), :]
                         for i in range(CMP_STRIDE)], axis=1).astype(BF16)

    n_special = -(-COL_VS // tn)
    for jt in range(n_special):
        pl.when(j == jt)(functools.partial(store, jt))
    pl.when(j >= n_special)(functools.partial(store, None))


def _inproj(x2, mod, g_pre, w_in_p, w_gates, cos_t, sin_t, seq, tm, tn=512):
    t, d = x2.shape
    per_b = seq // tm
    assert COL_KC % tn + KV_W <= tn and COL_VC % tn + KV_W <= tn
    assert tm % (BF16_TILE_ROWS * CMP_STRIDE) == 0
    hb_spec = pl.BlockSpec((1, N_KV_GROUPS, tm // CMP_STRIDE, CMP_STRIDE * HEAD_DIM),
                           lambda i, j: (i // per_b, 0, i % per_b, 0))
    hb_shape = jax.ShapeDtypeStruct(
        (t // seq, N_KV_GROUPS, seq // CMP_STRIDE, CMP_STRIDE * HEAD_DIM), BF16)
    return pl.pallas_call(
        functools.partial(_inproj_kernel, tn=tn, per_b=per_b),
        grid=(t // tm, PROJ_W // tn),
        in_specs=[pl.BlockSpec(memory_space=pl.ANY),
                  pl.BlockSpec(mod.shape, lambda i, j: (0, 0)),
                  pl.BlockSpec((1, d), lambda i, j: (0, 0)),
                  pl.BlockSpec((d, tn), lambda i, j: (0, j)),
                  pl.BlockSpec((d, LANES), lambda i, j: (0, 0)),
                  pl.BlockSpec((tm, HEAD_DIM), lambda i, j: (i, 0)),
                  pl.BlockSpec((tm, HEAD_DIM), lambda i, j: (i, 0))],
        out_specs=[pl.BlockSpec((tm, tn), lambda i, j: (i, j)),
                   pl.BlockSpec((tm, LANES), lambda i, j: (i, 0)), hb_spec, hb_spec],
        out_shape=[jax.ShapeDtypeStruct((t, PROJ_W), BF16),
                   jax.ShapeDtypeStruct((t, LANES), F32), hb_shape, hb_shape],
        scratch_shapes=[pltpu.VMEM((tm, d), BF16), pltpu.VMEM((tm, HEAD_DIM), F32),
                        pltpu.VMEM((tm, d), F32), pltpu.SemaphoreType.DMA((1,))],
        compiler_params=_cparams(("arbitrary", "arbitrary")),
        name="inproj",
    )(x2, mod, g_pre, w_in_p, w_gates, cos_t, sin_t)


def _compress_kernel(h_ref, w1_ref, pe_ref, w2_ref, o_ref):
    hb = h_ref[0]
    half = hb.shape[1]
    ya = _dot(hb, w1_ref[0:half, :])
    yb = _dot(hb, w1_ref[half:2 * half, :])
    const = _dot(pe_ref[...], w1_ref[...])
    n = ya.shape[0]
    yb_next = pltpu.roll(yb, n - 1, axis=0)
    hid = _silu(ya + yb_next + const[0:1, :])
    out = _dot(hid.astype(BF16), w2_ref[...])
    row = lax.broadcasted_iota(jnp.int32, out.shape, 0)
    o_ref[0] = jnp.where(row < n - 1, out, 0.0).astype(BF16)


def _compress(hblk, w1, pe, w2):
    bg, n, half = hblk.shape
    pe8 = jnp.zeros((8, 2 * half), BF16).at[0].set(pe.reshape(-1).astype(BF16))
    return pl.pallas_call(
        _compress_kernel,
        grid=(bg,),
        in_specs=[pl.BlockSpec((1, n, half), lambda i: (i, 0, 0)),
                  pl.BlockSpec((2 * half, HEAD_DIM), lambda i: (0, 0)),
                  pl.BlockSpec((8, 2 * half), lambda i: (0, 0)),
                  pl.BlockSpec((HEAD_DIM, HEAD_DIM), lambda i: (0, 0))],
        out_specs=pl.BlockSpec((1, n, HEAD_DIM), lambda i: (i, 0, 0)),
        out_shape=jax.ShapeDtypeStruct((bg, n, HEAD_DIM), BF16),
        compiler_params=_cparams(("parallel",)),
        name="compress",
    )(hblk, w1.astype(BF16), pe8, w2.astype(BF16))


def _cmp_kernel(q_ref, kc_ref, vc_ref, gates_ref, ov_ref, ocmp_ref, selb_ref, imp_s, cnt_s, *,
                tq, nsel):
    g = pl.program_id(0) % N_KV_GROUPS
    q0 = pl.program_id(1) * tq
    kc = kc_ref[0]
    vc = vc_ref[0]
    nc = kc.shape[0]
    t_idx = q0 + lax.broadcasted_iota(jnp.int32, (tq, nc), 0)
    c_idx = lax.broadcasted_iota(jnp.int32, (tq, nc), 1)
    c_ok = c_idx * CMP_STRIDE + (CMP_BLOCK - 1) <= t_idx
    gl = gates_ref[...]
    lane = lax.broadcasted_iota(jnp.int32, (tq, LANES), 1)
    psum = jnp.zeros((tq, nc), F32)
    for h in range(HEADS_PER_GROUP):
        qh = q_ref[:, h * HEAD_DIM:(h + 1) * HEAD_DIM]
        s = jnp.where(c_ok, _dot_nt(qh, kc), -jnp.inf)
        m = jnp.max(s, axis=-1, keepdims=True)
        m = jnp.where(m == -jnp.inf, 0.0, m)
        p = jnp.exp2(s - m)
        p = p * (1.0 / jnp.maximum(jnp.sum(p, axis=-1, keepdims=True), 1e-30))
        psum = psum + p
        o = _dot(p.astype(BF16), vc)
        glog = _lane_col(gl, lane, (g * HEADS_PER_GROUP + h) * 3)
        ocmp_ref[:, h * HEAD_DIM:(h + 1) * HEAD_DIM] = (o * _sigmoid(glog)).astype(BF16)

    imp = _dot(psum, ov_ref[...], precision=lax.Precision.HIGHEST)
    cur = (q0 + lax.broadcasted_iota(jnp.int32, (tq, LANES), 0)) // SEL_BLOCK
    forced = (lane == 0) | (lane == cur) | (lane == cur - 1)
    imp = jnp.where(lane <= cur, imp + jnp.where(forced, FORCE_BONUS, 0.0), -jnp.inf)
    imp_s[...] = imp.T[0:nsel, :]
    cnt_s[...] = jnp.zeros(cnt_s.shape, F32)
    sub = SUBLANES
    n_groups = nsel // sub
    row = lax.broadcasted_iota(jnp.int32, (sub, tq), 0)

    def count_against(kg):
        groups = [imp_s[g * sub:(g + 1) * sub, :] for g in range(n_groups)]
        cnts = [cnt_s[g * sub:(g + 1) * sub, :] for g in range(n_groups)]
        for k in range(kg * sub, (kg + 1) * sub):
            vk = imp_s[k:k + 1, :]
            for g, grp in enumerate(groups):
                if k < g * sub:
                    beats = vk >= grp
                elif k >= (g + 1) * sub:
                    beats = vk > grp
                else:
                    beats = (vk > grp) | ((vk == grp) & (row > k - g * sub))
                cnts[g] = cnts[g] + jnp.where(beats, 1.0, 0.0)
        for g in range(n_groups):
            cnt_s[g * sub:(g + 1) * sub, :] = cnts[g]

    last_block = (q0 + tq - 1) // SEL_BLOCK
    for kg in range(n_groups):
        pl.when(kg * sub <= last_block)(functools.partial(count_against, kg))
    imp_t = imp_s[...]
    sel = (cnt_s[...] < float(N_SEL)) & (imp_t > -jnp.inf)
    bias_t = jnp.where(sel, 0.0, -MASK_BIG)
    if nsel < LANES:
        bias_t = jnp.concatenate([bias_t, jnp.zeros((LANES - nsel, tq), F32)], axis=0)
    selb_ref[...] = bias_t.T.astype(BF16)


def _cmp_attn(proj, kc, vc, gates, batch, seq, tq):
    t = proj.shape[0]
    nq = seq // tq
    nc = kc.shape[1]
    nsel = seq // SEL_BLOCK
    assert nsel <= LANES and nsel % 8 == 0
    c_start = np.arange(nc) * CMP_STRIDE
    j_start = np.arange(LANES) * SEL_BLOCK
    ov = ((c_start[:, None] < j_start[None, :] + SEL_BLOCK)
          & (c_start[:, None] + CMP_BLOCK > j_start[None, :])
          & (np.arange(LANES)[None, :] < nsel)).astype(np.float32)
    gq = HEADS_PER_GROUP * HEAD_DIM
    qmap = lambda bg, i: ((bg // N_KV_GROUPS) * nq + i, bg % N_KV_GROUPS)
    return pl.pallas_call(
        functools.partial(_cmp_kernel, tq=tq, nsel=nsel),
        grid=(batch * N_KV_GROUPS, nq),
        in_specs=[pl.BlockSpec((tq, gq), qmap),
                  pl.BlockSpec((1, nc, HEAD_DIM), lambda bg, i: (bg, 0, 0)),
                  pl.BlockSpec((1, nc, HEAD_DIM), lambda bg, i: (bg, 0, 0)),
                  pl.BlockSpec((tq, LANES), lambda bg, i: ((bg // N_KV_GROUPS) * nq + i, 0)),
                  pl.BlockSpec((nc, LANES), lambda bg, i: (0, 0))],
        out_specs=[pl.BlockSpec((tq, gq), qmap),
                   pl.BlockSpec((tq, LANES), lambda bg, i: (bg * nq + i, 0))],
        out_shape=[jax.ShapeDtypeStruct((t, ATT_W), BF16),
                   jax.ShapeDtypeStruct((batch * N_KV_GROUPS * seq, LANES), BF16)],
        scratch_shapes=[pltpu.VMEM((nsel, tq), F32), pltpu.VMEM((nsel, tq), F32)],
        compiler_params=_cparams(("parallel", "parallel")),
        name="cmp_attn",
    )(proj, kc, vc, gates, jnp.asarray(ov))


def _selwin_kernel(q_ref, selb_ref, ocmp_ref, gates_ref, ks_ref, vs_ref, kw_ref, vw_ref,
                   cb_ref, wb_ref, o_ref, kaug_s, vaug_s, vwaug_s, qaug_s, sbuf_s, m_s, acc_s,
                   swin_s, *, tq, tk, seq, nwin):
    g = pl.program_id(0) % N_KV_GROUPS
    qi = pl.program_id(1)
    q0 = qi * tq
    hq = HEADS_PER_GROUP * tq
    half = hq // 2

    @pl.when(qi == 0)
    def _():
        r = lax.broadcasted_iota(jnp.int32, (seq, HEAD_DIM), 0)
        ln = lax.broadcasted_iota(jnp.int32, (seq, HEAD_DIM), 1)
        ones_col = jnp.where(ln == 0, 1.0, 0.0).astype(BF16)
        kaug_s[:, 0:HEAD_DIM] = ks_ref[...]
        kaug_s[:, HEAD_DIM:2 * HEAD_DIM] = jnp.where(r // SEL_BLOCK == ln, 1.0, 0.0).astype(BF16)
        vaug_s[:, 0:HEAD_DIM] = vs_ref[...]
        vaug_s[:, HEAD_DIM:2 * HEAD_DIM] = ones_col
        vwaug_s[:, 0:HEAD_DIM] = vw_ref[...]
        vwaug_s[:, HEAD_DIM:2 * HEAD_DIM] = ones_col

    for h in range(HEADS_PER_GROUP):
        qaug_s[h * tq:(h + 1) * tq, 0:HEAD_DIM] = q_ref[:, h * HEAD_DIM:(h + 1) * HEAD_DIM]
        qaug_s[h * tq:(h + 1) * tq, HEAD_DIM:2 * HEAD_DIM] = selb_ref[...]
    qa = qaug_s[...]

    def scores(kt, slot):
        ks = kaug_s[pl.ds(pl.multiple_of(kt * tk, tk), tk), :]
        sbuf_s[slot] = _dot_nt(qa, ks)

    def absorb(kt, slot, diagonal=False):
        vs = vaug_s[pl.ds(pl.multiple_of(kt * tk, tk), tk), :]
        s = sbuf_s[slot]
        if diagonal:
            bias = cb_ref[(q0 - kt * tk) // tq]
            s = (s.reshape(HEADS_PER_GROUP, tq, tk) + bias[None]).reshape(hq, tk)
        m = m_s[...]
        m_new = jnp.maximum(m, jnp.max(s, axis=-1, keepdims=True))
        m_s[...] = m_new
        pb = jnp.exp2(s - m_new[:, 0:1]).astype(BF16)
        if diagonal:
            pv = jnp.concatenate([_dot(pb[0:half], vs), _dot(pb[half:hq], vs)], axis=0)
        else:
            pv = _dot(pb, vs)
        acc_s[...] = jnp.exp2(m - m_new)[:, 0:1] * acc_s[...] + pv

    n_full = q0 // tk
    m_s[...] = jnp.full(m_s.shape, -jnp.inf, F32)
    acc_s[...] = jnp.zeros(acc_s.shape, F32)
    scores(0, 0)

    def pair(j, c):
        scores(2 * j + 1, 1)
        absorb(2 * j, 0)
        scores(2 * j + 2, 0)
        absorb(2 * j + 1, 1)
        return c

    lax.fori_loop(0, n_full // 2, pair, 0)
    odd = n_full % 2

    @pl.when(odd == 1)
    def _():
        scores(n_full, 1)
        absorb(n_full - 1, 0)

    nq = seq // tq
    w0 = jnp.clip(qi - (nwin - 1), 0, nq - nwin) * tq
    wlen = nwin * tq
    kwin = kw_ref[pl.ds(pl.multiple_of(w0, tq), wlen), :]
    vwin = vwaug_s[pl.ds(pl.multiple_of(w0, tq), wlen), :]
    swin_s[0:half, :] = _dot_nt(qaug_s[0:half, 0:HEAD_DIM], kwin)
    swin_s[half:hq, :] = _dot_nt(qaug_s[half:hq, 0:HEAD_DIM], kwin)
    absorb(n_full, odd, diagonal=True)
    acc = acc_s[...]
    o_sel = acc[:, 0:HEAD_DIM] / acc[:, HEAD_DIM:HEAD_DIM + 1]

    wbias = wb_ref[(q0 - w0) // tq]
    sw = (swin_s[...].reshape(HEADS_PER_GROUP, tq, wlen) + wbias[None]).reshape(hq, wlen)
    pwb = jnp.exp2(sw - jnp.max(sw, axis=-1, keepdims=True)).astype(BF16)
    ow = jnp.concatenate([_dot(pwb[0:half], vwin), _dot(pwb[half:hq], vwin)], axis=0)
    o_win = ow[:, 0:HEAD_DIM] / ow[:, HEAD_DIM:HEAD_DIM + 1]

    gl = gates_ref[...]
    lane = lax.broadcasted_iota(jnp.int32, (tq, LANES), 1)
    for h in range(HEADS_PER_GROUP):
        gcol = (g * HEADS_PER_GROUP + h) * 3
        g_s = _sigmoid(_lane_col(gl, lane, gcol + 1))
        g_w = _sigmoid(_lane_col(gl, lane, gcol + 2))
        o = (ocmp_ref[:, h * HEAD_DIM:(h + 1) * HEAD_DIM].astype(F32)
             + g_s * o_sel[h * tq:(h + 1) * tq, :] + g_w * o_win[h * tq:(h + 1) * tq, :])
        o_ref[:, h * HEAD_DIM:(h + 1) * HEAD_DIM] = o.astype(BF16)


def _selwin_attn(proj, selb, ocmp, gates, batch, seq, tq, tk):
    t = proj.shape[0]
    nq = seq // tq
    assert tk % tq == 0 and tq & (tq - 1) == 0 and seq % tk == 0
    nwin = min(WINDOW // tq + 1, nq)
    assert WINDOW % tq == 0
    gq = HEADS_PER_GROUP * HEAD_DIM
    hq = HEADS_PER_GROUP * tq
    r = np.arange(tq)[:, None]
    cbias = np.stack([np.where(np.arange(tk)[None, :] <= r + off * tq, 0.0, -np.inf)
                      for off in range(tk // tq)]).astype(np.float32)
    dist = [r + off * tq - np.arange(nwin * tq)[None, :] for off in range(nwin)]
    wbias = np.stack([np.where((d >= 0) & (d < WINDOW), 0.0, -np.inf)
                      for d in dist]).astype(np.float32)
    qmap = lambda bg, i: ((bg // N_KV_GROUPS) * nq + i, bg % N_KV_GROUPS)

    def kvspec(col):
        return pl.BlockSpec((seq, HEAD_DIM),
                            lambda bg, i: (bg // N_KV_GROUPS, col // HEAD_DIM + bg % N_KV_GROUPS))

    return pl.pallas_call(
        functools.partial(_selwin_kernel, tq=tq, tk=tk, seq=seq, nwin=nwin),
        grid=(batch * N_KV_GROUPS, nq),
        in_specs=[pl.BlockSpec((tq, gq), qmap),
                  pl.BlockSpec((tq, LANES), lambda bg, i: (bg * nq + i, 0)),
                  pl.BlockSpec((tq, gq), qmap),
                  pl.BlockSpec((tq, LANES), lambda bg, i: ((bg // N_KV_GROUPS) * nq + i, 0)),
                  kvspec(COL_KS), kvspec(COL_VS), kvspec(COL_KW), kvspec(COL_VW),
                  pl.BlockSpec(cbias.shape, lambda bg, i: (0, 0, 0)),
                  pl.BlockSpec(wbias.shape, lambda bg, i: (0, 0, 0))],
        out_specs=pl.BlockSpec((tq, gq), qmap),
        out_shape=jax.ShapeDtypeStruct((t, ATT_W), BF16),
        scratch_shapes=[pltpu.VMEM((seq, 2 * HEAD_DIM), BF16),
                        pltpu.VMEM((seq, 2 * HEAD_DIM), BF16),
                        pltpu.VMEM((seq, 2 * HEAD_DIM), BF16),
                        pltpu.VMEM((hq, 2 * HEAD_DIM), BF16),
                        pltpu.VMEM((2, hq, tk), F32),
                        pltpu.VMEM((hq, LANES), F32),
                        pltpu.VMEM((hq, 2 * HEAD_DIM), F32),
                        pltpu.VMEM((hq, nwin * tq), F32)],
        compiler_params=_cparams(("parallel", "arbitrary")),
        name="selwin_attn",
    )(proj, selb, ocmp, gates, proj, proj, proj, proj, jnp.asarray(cbias), jnp.asarray(wbias))


_HGRN_LEVELS = (32, 16, 8, 4, 2, 1)


def _hgrn_tables():
    c = HGRN_CHUNK
    nl = len(_HGRN_LEVELS)
    idx = np.arange(c)
    i, j = idx[:, None], idx[None, :]
    mats = [j <= i, j > i]
    lvl = np.full((c, c), nl + 1, np.int32)
    for li, s in enumerate(_HGRN_LEVELS):
        mats.append((j >= (i // s) * s) & (j <= i))
        mats.append((j > i) & (j <= (i // s) * s + s - 1))
        lvl[((i // (2 * s)) == (j // (2 * s))) & (((i // s) % 2) == 1) & (((j // s) % 2) == 0)] = li
    lvl[i == j] = nl
    w = np.concatenate(mats, 0).astype(np.float32)
    return np.concatenate([w, w], axis=1), lvl


def _hgrn_head(hq_ref, hf_ref, hi_ref, hg_ref, lbl_ref, gn_ref, w_ref, lvl_ref, o_ref, st_s,
               hh, ts):
    c = HGRN_CHUNK
    nc = ts // c
    nl = len(_HGRN_LEVELS)
    cs = slice(hh * HEAD_DIM, (hh + 1) * HEAD_DIM)
    lbl = lbl_ref[:, cs]
    e = jnp.exp(lbl - jnp.max(lbl, axis=0, keepdims=True))
    lb = e[0:1, :] / jnp.sum(e, axis=0, keepdims=True)

    f = lb + (1.0 - lb) * _sigmoid(hf_ref[:, cs].astype(F32))
    k = 1.0 - f
    qf = _silu(hq_ref[:, cs].astype(F32))
    v = hi_ref[:, cs]

    lf = jnp.log(f) * LOG2E
    lf_hi = lf.astype(BF16)
    lf_lo = (lf - lf_hi.astype(F32)).astype(BF16)
    side = lambda a: jnp.concatenate([a[ci * c:(ci + 1) * c, :] for ci in range(nc)], axis=1)
    ex = _dot(w_ref[...], jnp.concatenate([side(lf_hi), side(lf_lo)], axis=0))

    def rows(bi):
        return jnp.concatenate(
            [ex[bi * c:(bi + 1) * c, ci * HEAD_DIM:(ci + 1) * HEAD_DIM] for ci in range(nc)], axis=0)

    b = rows(0)
    qhat = (qf * jnp.exp2(b)).astype(BF16)
    khat = (k * jnp.exp2(rows(1))).astype(BF16)
    lvl = lvl_ref[...]
    masks = [lvl == li for li in range(nl + 1)]
    qs = [(qf * jnp.exp2(rows(2 + 2 * li))).astype(BF16) for li in range(nl)] + [qf.astype(BF16)]
    ks = [(k * jnp.exp2(rows(3 + 2 * li))).astype(BF16) for li in range(nl)] + [k.astype(BF16)]
    o_intra = []
    for ci in range(nc):
        sl = slice(ci * c, (ci + 1) * c)
        a = jnp.zeros((c, c), F32)
        for li in range(nl + 1):
            a = jnp.where(masks[li], _dot_nt(qs[li][sl, :], ks[li][sl, :]), a)
        o_intra.append(_dot(a.astype(BF16), v[sl, :]))

    st = st_s[hh]
    outs = []
    for ci in range(nc):
        sl = slice(ci * c, (ci + 1) * c)
        outs.append(o_intra[ci] + _dot_nt(qhat[sl, :], st.astype(BF16)))
        st = st * jnp.exp2(b[(ci + 1) * c - 1:(ci + 1) * c, :]) + lax.dot_general(
            v[sl, :], khat[sl, :], _TN, preferred_element_type=F32)
    st_s[hh] = st
    o = jnp.concatenate(outs, axis=0)
    o = o * lax.rsqrt(jnp.mean(o * o, axis=-1, keepdims=True) + EPS) * gn_ref[:, cs]
    o_ref[:, cs] = (o * _silu(hg_ref[:, cs].astype(F32))).astype(BF16)


def _hgrn_kernel(hq_ref, hf_ref, hi_ref, hg_ref, lbl_ref, gn_ref, w_ref, lvl_ref,
                 o_ref, st_s, *, ts, hb):
    @pl.when(pl.program_id(2) == 0)
    def _():
        st_s[...] = jnp.zeros_like(st_s)

    for hh in range(hb):
        _hgrn_head(hq_ref, hf_ref, hi_ref, hg_ref, lbl_ref, gn_ref, w_ref, lvl_ref, o_ref, st_s,
                   hh, ts)


def _hgrn(proj, lb_logits, g_norm, batch, seq, ts, hb):
    t = proj.shape[0]
    ns = seq // ts
    wmat, lvl = _hgrn_tables()
    w = hb * HEAD_DIM
    assert all(col % w == 0 for col in (COL_HQ, COL_HF, COL_HI, COL_HG))

    def colspec(col):
        return pl.BlockSpec((ts, w), lambda b, h, s: (b * ns + s, col // w + h))

    nl = lb_logits.shape[0]
    return pl.pallas_call(
        functools.partial(_hgrn_kernel, ts=ts, hb=hb),
        grid=(batch, N_HGRN_HEADS // hb, ns),
        in_specs=[colspec(COL_HQ), colspec(COL_HF), colspec(COL_HI), colspec(COL_HG),
                  pl.BlockSpec((nl, w), lambda b, h, s: (0, h)),
                  pl.BlockSpec((1, w), lambda b, h, s: (0, h)),
                  pl.BlockSpec(wmat.shape, lambda b, h, s: (0, 0)),
                  pl.BlockSpec(lvl.shape, lambda b, h, s: (0, 0))],
        out_specs=pl.BlockSpec((ts, w), lambda b, h, s: (b * ns + s, h)),
        out_shape=jax.ShapeDtypeStruct((t, ATT_W), BF16),
        scratch_shapes=[pltpu.VMEM((hb, HEAD_DIM, HEAD_DIM), F32)],
        compiler_params=_cparams(("parallel", "parallel", "arbitrary")),
        name="hgrn2",
    )(proj, proj, proj, proj, lb_logits, g_norm, jnp.asarray(wmat, BF16), jnp.asarray(lvl))


def _outproj_kernel(oa_ref, or_ref, wa_ref, wb_ref, x_ref, mod_ref, gpost_ref, gpre_ref,
                    wr_ref, br_ref, x1_ref, h2_ref, lg_ref, stage, *, per_b):
    b = pl.program_id(0) // per_b
    y = _dot(oa_ref[...], wa_ref[...]) + _dot(or_ref[...], wb_ref[...])
    x1 = x_ref[...] + _mod_row(mod_ref, b, 2) * _rms(y, gpost_ref[...])
    x1_ref[...] = x1
    h = _rms(x1, gpre_ref[...]) * (1.0 + _mod_row(mod_ref, b, 4)) + _mod_row(mod_ref, b, 3)
    h2_ref[...] = _to_slabs(stage, h)
    h_hi = h.astype(BF16)
    h_lo = (h - h_hi.astype(F32)).astype(BF16)
    hh = _dot(h_hi, wr_ref[...])
    lg_ref[...] = (hh[:, 0:LANES] + hh[:, LANES:2 * LANES]
                   + _dot(h_lo, wr_ref[:, 0:LANES]) + br_ref[...])


def _outproj(o_att, o_rec, w_out, x2, mod, g_post, g_pre, wr, br, seq, tm):
    t, d = x2.shape
    per_b = seq // tm
    wr_hi = wr.astype(BF16)
    wr_cat = jnp.concatenate([wr_hi, (wr - wr_hi.astype(F32)).astype(BF16)], axis=1)
    wa = w_out[:ATT_W].astype(BF16)
    wb = w_out[ATT_W:].astype(BF16)
    row = lambda i: (i, 0)
    fixed = lambda i: (0, 0)
    return pl.pallas_call(
        functools.partial(_outproj_kernel, per_b=per_b),
        grid=(t // tm,),
        in_specs=[pl.BlockSpec((tm, ATT_W), row), pl.BlockSpec((tm, ATT_W), row),
                  pl.BlockSpec((ATT_W, d), fixed), pl.BlockSpec((ATT_W, d), fixed),
                  pl.BlockSpec((tm, d), row),
                  pl.BlockSpec(mod.shape, fixed),
                  pl.BlockSpec((1, d), fixed), pl.BlockSpec((1, d), fixed),
                  pl.BlockSpec((d, 2 * LANES), fixed), pl.BlockSpec((1, LANES), fixed)],
        out_specs=[pl.BlockSpec((tm, d), row), pl.BlockSpec((tm * (d // LANES), LANES), row),
                   pl.BlockSpec((tm, LANES), row)],
        out_shape=[jax.ShapeDtypeStruct((t, d), F32),
                   jax.ShapeDtypeStruct((t * (d // LANES), LANES), BF16),
                   jax.ShapeDtypeStruct((t, LANES), F32)],
        scratch_shapes=[pltpu.VMEM((tm * (d // LANES), LANES), F32)],
        compiler_params=_cparams(("parallel",)),
        name="outproj",
    )(o_att, o_rec, wa, wb, x2, mod, g_post, g_pre, wr_cat, br)


G_LANE0 = N_EXPERTS


def _route_kernel(lg_ref, tri_ref, info_ref, cnt_ref, carry_s):
    @pl.when(pl.program_id(0) == 0)
    def _():
        carry_s[...] = jnp.zeros_like(carry_s)

    x = lg_ref[...]
    lane = lax.broadcasted_iota(jnp.int32, x.shape, 1)
    big = jnp.int32(10 ** 6)
    rmax = lambda a: jnp.max(a, axis=-1, keepdims=True)
    rmin = lambda a: jnp.min(a, axis=-1, keepdims=True)
    rsum = lambda a: jnp.sum(a, axis=-1, keepdims=True)

    is_g = (lane >= G_LANE0) & (lane < G_LANE0 + N_EXPERT_GROUPS)
    lgm = jnp.where(is_g, x, -jnp.inf)
    mg = rmax(lgm)
    pg_sel = 1.0 / rsum(jnp.where(is_g, jnp.exp(lgm - mg), 0.0))
    gsel = rmin(jnp.where(lgm == mg, lane, big)) - G_LANE0

    is_e = (lane >= gsel * EXPERTS_PER_GROUP) & (lane < (gsel + 1) * EXPERTS_PER_GROUP)
    lem = jnp.where(is_e, x, -jnp.inf)
    pe = jnp.where(is_e, jnp.exp(lem - rmax(lem)), 0.0)
    pe = pe / rsum(pe)
    pe = jnp.where(is_e, pe, -1.0)
    v1 = rmax(pe)
    i1 = rmin(jnp.where(pe == v1, lane, big))
    pe2 = jnp.where(lane == i1, -1.0, pe)
    v2 = rmax(pe2)
    i2 = rmin(jnp.where(pe2 == v2, lane, big))
    w1 = v1 / (v1 + v2) * pg_sel
    w2 = v2 / (v1 + v2) * pg_sel

    oh1 = jnp.where(lane == i1, 1.0, 0.0)
    oh2 = jnp.where(lane == i2, 1.0, 0.0)
    both = oh1 + oh2
    before = _dot(tri_ref[...], both.astype(BF16)) + carry_s[0:1, :]
    r1 = rsum(oh1 * before)
    r2 = rsum(oh2 * before)
    carry_s[0:1, :] = carry_s[0:1, :] + jnp.sum(both, axis=0, keepdims=True)
    cnt_ref[...] = carry_s[...]

    cols = (i1.astype(F32), i2.astype(F32), w1, w2, r1, r2)
    info = jnp.zeros(x.shape, F32)
    for ci, col in enumerate(cols):
        info = jnp.where(lane == ci, col, info)
    info_ref[...] = info


def _route(logits, tm):
    t = logits.shape[0]
    tri = jnp.asarray(np.tril(np.ones((tm, tm), np.float32), -1), BF16)
    return pl.pallas_call(
        _route_kernel,
        grid=(t // tm,),
        in_specs=[pl.BlockSpec((tm, LANES), lambda i: (i, 0)),
                  pl.BlockSpec((tm, tm), lambda i: (0, 0))],
        out_specs=[pl.BlockSpec((tm, LANES), lambda i: (i, 0)),
                   pl.BlockSpec((8, LANES), lambda i: (0, 0))],
        out_shape=[jax.ShapeDtypeStruct((t, LANES), F32),
                   jax.ShapeDtypeStruct((8, LANES), F32)],
        scratch_shapes=[pltpu.VMEM((8, LANES), F32)],
        compiler_params=_cparams(("arbitrary",)),
        name="route",
    )(logits, tri)


def _to_slabs(stage, x):
    n, d = x.shape
    slab = d // LANES
    for c in range(slab):
        stage[pl.ds(c, n, stride=slab), :] = x[:, c * LANES:(c + 1) * LANES]
    return stage[...].astype(BF16)


def _from_slabs(stage, slabs, n):
    slab = slabs.shape[0] // n
    stage[...] = slabs.astype(F32)
    return jnp.concatenate([stage[pl.ds(c, n, stride=slab), :] for c in range(slab)], axis=1)


def _expert_kernel(te_ref, nu_ref, par_ref, nxt_ref, dst_ref, xs_ref, wg_hbm, wu_hbm, wd_hbm,
                   y_ref, obuf, stage, ssem, wg_f, wu_f, wd_f, wsem, wg_s, wu_s, wd_s, *,
                   tm, standin_row):
    t = pl.program_id(0)
    nu = nu_ref[0]
    slot = t % 2

    def weights(e, buf):
        return [pltpu.make_async_copy(src.at[e], dst.at[buf], wsem.at[buf])
                for src, dst in ((wg_hbm, wg_f), (wu_hbm, wu_f), (wd_hbm, wd_f))]
    slab = wg_s.shape[0] // LANES
    tile = tm * slab

    def token(ref, i):
        return ref.at[pl.ds(pl.multiple_of(i * slab, slab), slab)]

    def scatter(s):
        for r in range(tm):
            pltpu.make_async_copy(token(obuf, s * tm + r), token(y_ref, dst_ref[0, 0, r]),
                                  ssem.at[s]).start()

    def tile_of(buf, s):
        return buf.at[pl.ds(pl.multiple_of(s * tile, tile), tile)]

    def wait_tile(s):
        pltpu.make_async_copy(tile_of(obuf, s), tile_of(obuf, s), ssem.at[s]).wait()

    @pl.when(t < nu)
    def _():
        @pl.when(t == 0)
        def _():
            obuf[tile:2 * tile, :] = jnp.zeros((tile, LANES), obuf.dtype)

            def fill(row0, s):
                return pltpu.make_async_copy(tile_of(obuf, 1),
                                             y_ref.at[pl.ds(row0 * slab, tile)], ssem.at[s])

            fill(standin_row - 2 * tm, 0).start()
            fill(standin_row - tm, 0).start()
            fill(standin_row - 2 * tm, 0).wait()
            fill(standin_row - tm, 0).wait()
            fill(standin_row, 1).start()

        par = par_ref[t]

        @pl.when(t == 0)
        def _():
            for cp in weights(te_ref[0], par):
                cp.start()

        @pl.when((t == 0) | (te_ref[t] != te_ref[jnp.maximum(t - 1, 0)]))
        def _():
            for cp in weights(te_ref[t], par):
                cp.wait()
            wg_s[...] = wg_f[par].astype(BF16)
            wu_s[...] = wu_f[par].astype(BF16)
            wd_s[...] = wd_f[par].astype(BF16)

            @pl.when(nxt_ref[t] != te_ref[t])
            def _():
                for cp in weights(nxt_ref[t], 1 - par):
                    cp.start()

        x = _from_slabs(stage, xs_ref[...], tm).astype(BF16)
        hid = (_silu(_dot(x, wg_s[...])) * _dot(x, wu_s[...])).astype(BF16)
        row0 = pl.multiple_of(slot * tile, tile)
        obuf[pl.ds(row0, tile), :] = _to_slabs(stage, _dot(hid, wd_s[...]))
        scatter(slot)
        wait_tile(1 - slot)

        @pl.when(t == nu - 1)
        def _():
            wait_tile(slot)


def _experts(tile_expert, n_used, parity, next_expert, dst3, xs, w_gate, w_up, w_down, y_rows,
             tm, standin_row):
    n_tiles = dst3.shape[0]
    d, f = w_gate.shape[1:]
    slab = d // LANES
    hbm = pl.BlockSpec(memory_space=pl.ANY)
    return pl.pallas_call(
        functools.partial(_expert_kernel, tm=tm, standin_row=standin_row),
        grid_spec=pltpu.PrefetchScalarGridSpec(
            num_scalar_prefetch=4,
            grid=(n_tiles,),
            in_specs=[pl.BlockSpec((1, 1, tm), lambda t, te, nu, pa, nx: (t, 0, 0),
                                   memory_space=pltpu.SMEM),
                      pl.BlockSpec((tm * slab, LANES),
                                   lambda t, te, nu, pa, nx: (jnp.minimum(t, nu[0] - 1), 0)),
                      hbm, hbm, hbm],
            out_specs=pl.BlockSpec(memory_space=pl.ANY),
            scratch_shapes=[pltpu.VMEM((2 * tm * slab, LANES), BF16),
                            pltpu.VMEM((tm * slab, LANES), F32),
                            pltpu.SemaphoreType.DMA((2,)),
                            pltpu.VMEM((2, d, f), F32), pltpu.VMEM((2, d, f), F32),
                            pltpu.VMEM((2, f, d), F32),
                            pltpu.SemaphoreType.DMA((2,)),
                            pltpu.VMEM((d, f), BF16), pltpu.VMEM((d, f), BF16),
                            pltpu.VMEM((f, d), BF16)]),
        out_shape=jax.ShapeDtypeStruct((y_rows * slab, LANES), BF16),
        compiler_params=_cparams(("arbitrary",)),
        name="moe_experts",
    )(tile_expert, n_used, parity, next_expert, dst3, xs, w_gate, w_up, w_down)


def _dispatch_kernel(pos0_ref, pos1_ref, end_ref, nu_ref, h_ref, dst_in, xs_ref, dst_ref,
                     zbuf, zsem, sem, *, tm, slab, n_tiles, n_tokens):
    tile = tm * slab
    base = pl.program_id(0) * tm

    @pl.when(pl.program_id(0) == 0)
    def _():
        init = pltpu.make_async_copy(dst_in, dst_ref, zsem)
        init.start()
        init.wait()
        zbuf[...] = jnp.zeros(zbuf.shape, zbuf.dtype)

        def zero(row_end, go):
            cp = pltpu.make_async_copy(
                zbuf, xs_ref.at[pl.ds(pl.multiple_of((row_end - tm) * slab, slab), tile)], zsem)
            pl.when(go)(cp.start)
            return cp, go

        pending = [zero(end_ref[e], end_ref[e] > (end_ref[e - 1] if e else 0))
                   for e in range(N_EXPERTS)]
        pending += [zero((i + 1) * tm, i >= nu_ref[0]) for i in range(n_tiles)]
        for cp, go in pending:
            pl.when(go)(cp.wait)

    def token(ref, i):
        return ref.at[pl.ds(pl.multiple_of(i * slab, slab), slab)]

    def start(r, c):
        p0 = pos0_ref[0, 0, r]
        p1 = pos1_ref[0, 0, r]
        pltpu.make_async_copy(token(h_ref, r), token(xs_ref, p0), sem).start()
        pltpu.make_async_copy(token(h_ref, r), token(xs_ref, p1), sem).start()
        dst_ref[p0] = base + r
        dst_ref[p1] = n_tokens + base + r
        return c

    lax.fori_loop(0, tm, start, 0, unroll=8)
    for _ in range(2):
        pltpu.make_async_copy(h_ref, h_ref, sem).wait()


def _dispatch(pos0, pos1, tile_end, n_used, h2, dst0, n_tiles, tm):
    rows = h2.shape[0]
    slab = rows // pos0.shape[0]
    t = pos0.shape[0]
    per_tile = pl.BlockSpec((1, 1, tm), lambda i: (i, 0, 0), memory_space=pltpu.SMEM)
    whole = pl.BlockSpec(memory_space=pltpu.SMEM)
    return pl.pallas_call(
        functools.partial(_dispatch_kernel, tm=tm, slab=slab, n_tiles=n_tiles, n_tokens=t),
        grid=(t // tm,),
        in_specs=[per_tile, per_tile, whole, whole,
                  pl.BlockSpec((tm * slab, LANES), lambda i: (i, 0)),
                  pl.BlockSpec(memory_space=pl.ANY)],
        out_specs=[pl.BlockSpec(memory_space=pl.ANY), whole],
        out_shape=[jax.ShapeDtypeStruct((n_tiles * tm * slab, LANES), h2.dtype),
                   jax.ShapeDtypeStruct(dst0.shape, jnp.int32)],
        scratch_shapes=[pltpu.VMEM((tm * slab, LANES), h2.dtype),
                        pltpu.SemaphoreType.DMA(()), pltpu.SemaphoreType.DMA(())],
        compiler_params=_cparams(("arbitrary",)),
        name="moe_dispatch",
    )(pos0.reshape(t // tm, 1, tm), pos1.reshape(t // tm, 1, tm), tile_end * tm, n_used, h2,
      dst0)


def _final_kernel(info_ref, x1_ref, mod_ref, g_ref, ya_ref, yb_ref, o_ref, stage_a, stage_b, *,
                  per_b):
    info = info_ref[...]
    tm = x1_ref.shape[0]
    ya = _from_slabs(stage_a, ya_ref[...], tm)
    yb = _from_slabs(stage_b, yb_ref[...], tm)
    y = info[:, 2:3] * ya + info[:, 3:4] * yb
    gain = _mod_row(mod_ref, pl.program_id(0) // per_b, 5) * g_ref[...]
    inv = lax.rsqrt(jnp.mean(y * y, axis=-1, keepdims=True) + EPS)
    o_ref[...] = x1_ref[...] + (y * inv) * gain


def _moe_final(info, x1, mod, g_post, y2, seq, tm):
    t, d = x1.shape
    per_b = seq // tm
    nb = t // tm
    return pl.pallas_call(
        functools.partial(_final_kernel, per_b=per_b),
        grid=(nb,),
        in_specs=[pl.BlockSpec((tm, LANES), lambda i: (i, 0)),
                  pl.BlockSpec((tm, d), lambda i: (i, 0)),
                  pl.BlockSpec(mod.shape, lambda i: (0, 0)),
                  pl.BlockSpec((1, d), lambda i: (0, 0)),
                  pl.BlockSpec((tm * (d // LANES), LANES), lambda i: (i, 0)),
                  pl.BlockSpec((tm * (d // LANES), LANES), lambda i: (nb + i, 0))],
        out_specs=pl.BlockSpec((tm, d), lambda i: (i, 0)),
        out_shape=jax.ShapeDtypeStruct((t, d), F32),
        scratch_shapes=[pltpu.VMEM((tm * (d // LANES), LANES), F32)] * 2,
        compiler_params=_cparams(("parallel",)),
        name="moe_final",
    )(info, x1, mod, g_post, y2, y2)


def _moe(h2, logits, x1, mod, g_post, w_gate, w_up, w_down, seq, tm_route, tm_exp, tm_fin):
    t = x1.shape[0]
    info, cnt = _route(logits, tm_route)
    counts = cnt[0, :N_EXPERTS].astype(jnp.int32)
    tiles_e = (counts + tm_exp - 1) // tm_exp
    eidx = jnp.arange(N_EXPERTS, dtype=jnp.int32)
    tile_end = jnp.sum(jnp.where(eidx[None, :] <= eidx[:, None], tiles_e[None, :], 0), axis=1)
    offs = (tile_end - tiles_e) * tm_exp
    n_used = jnp.sum(tiles_e)

    def slot_of(k):
        e = info[:, k].astype(jnp.int32)
        start = jnp.sum(jnp.where(e[:, None] == jnp.arange(N_EXPERTS)[None, :], offs[None, :], 0),
                        axis=1)
        return start + info[:, 4 + k].astype(jnp.int32)

    n_tiles = (2 * t) // tm_exp + N_EXPERTS
    tidx = jnp.minimum(jnp.arange(n_tiles, dtype=jnp.int32), n_used - 1)
    tile_expert = jnp.sum(tidx[:, None] >= tile_end[None, :], axis=1).astype(jnp.int32)
    used = (tiles_e > 0)[None, :]
    parity = jnp.sum(used & (eidx[None, :] < tile_expert[:, None]), axis=1) % 2
    later = jnp.min(jnp.where(used & (eidx[None, :] > tile_expert[:, None]), eidx[None, :],
                              N_EXPERTS), axis=1)
    next_expert = jnp.where(later == N_EXPERTS, tile_expert, later)
    pos0, pos1 = slot_of(0), slot_of(1)
    slot_ids = jnp.arange(n_tiles * tm_exp, dtype=jnp.int32)
    spare = 2 * t + ((slot_ids // tm_exp) % 2) * tm_exp + slot_ids % tm_exp
    n_used1 = n_used.reshape(1).astype(jnp.int32)
    xs, dst = _dispatch(pos0, pos1, tile_end.astype(jnp.int32), n_used1, h2, spare, n_tiles,
                        tm_exp)

    ew = w_gate.shape
    wg = w_gate.reshape(N_EXPERTS, ew[-2], ew[-1])
    wu = w_up.reshape(N_EXPERTS, ew[-2], ew[-1])
    wd = w_down.reshape(N_EXPERTS, ew[-1], ew[-2])
    y2 = _experts(tile_expert, n_used1, parity.astype(jnp.int32), next_expert.astype(jnp.int32),
                  dst.reshape(n_tiles, 1, tm_exp), xs, wg, wu, wd,
                  2 * t + 3 * tm_exp, tm_exp, 2 * t + 2 * tm_exp)
    return _moe_final(info, x1, mod, g_post, y2, seq, tm_fin)


def _permute_w_in(w_in):
    sizes = [ATT_W] + [KV_W] * 6 + [3 * N_ATT_HEADS] + [ATT_W] * 4
    cuts = np.cumsum(sizes)[:-1]
    q, kc, vc, ks, vs, kw, vw, gates, hq, hf, hi, hg = [
        p.astype(BF16) for p in jnp.split(w_in, cuts, axis=1)]
    w_p = jnp.concatenate([q, kc, ks, kw, vc, vs, vw, hq, hf, hi, hg], axis=1)
    w_g = jnp.pad(gates, ((0, 0), (0, LANES - gates.shape[1])))
    return w_p, w_g


def _block(x, c, positions, w_ada, b_ada, g_pre_mix, g_post_mix, g_pre_ffn, g_post_ffn,
           w_in, w_out, pe_k, w1_k, w2_k, pe_v, w1_v, w2_v, lb_logits, g_norm,
           w_group, b_group, w_router, b_router, w_gate, w_up, w_down, cfg):
    batch, seq, d = x.shape
    t = batch * seq
    x2 = x.reshape(t, d)

    c8 = jnp.zeros((8, d), F32).at[:batch].set(c)
    mod = _ada_mod(c8, w_ada, b_ada)
    cos_t, sin_t = _rope_tables(positions, cfg["tm_rope"])
    w_p, w_g = _permute_w_in(w_in)
    proj, gates, hbk, hbv = _inproj(x2, mod, g_pre_mix.reshape(1, d), w_p, w_g, cos_t, sin_t,
                                    seq, cfg["tm_in"])

    half_blocks = (batch * N_KV_GROUPS, seq // CMP_STRIDE, CMP_STRIDE * HEAD_DIM)
    kc = _compress(hbk.reshape(half_blocks), w1_k, pe_k, w2_k)
    vc = _compress(hbv.reshape(half_blocks), w1_v, pe_v, w2_v)
    ocmp, selb = _cmp_attn(proj, kc, vc, gates, batch, seq, cfg["tq_cmp"])
    o_att = _selwin_attn(proj, selb, ocmp, gates, batch, seq, cfg["tq"], cfg["tk"])
    o_rec = _hgrn(proj, lb_logits, g_norm.reshape(1, -1), batch, seq, cfg["ts_hgrn"],
                  cfg["hb_hgrn"])

    wr = jnp.concatenate([w_router, w_group], axis=1)
    wr = jnp.pad(wr, ((0, 0), (0, LANES - wr.shape[1])))
    br = jnp.pad(jnp.concatenate([b_router, b_group]), (0, LANES - N_EXPERTS - N_EXPERT_GROUPS))
    x1, h2, logits = _outproj(o_att, o_rec, w_out, x2, mod, g_post_mix.reshape(1, d),
                              g_pre_ffn.reshape(1, d), wr, br.reshape(1, LANES), seq,
                              cfg["tm_out"])
    out = _moe(h2, logits, x1, mod, g_post_ffn.reshape(1, d), w_gate, w_up, w_down, seq,
               cfg["tm_route"], cfg["tm_exp"], cfg["tm_fin"])
    return out.reshape(batch, seq, d)


def _config(seq):
    return dict(tm_rope=min(1024, seq), tm_in=min(1024, seq), tq_cmp=min(1024, seq),
                tq=min(256, seq), tk=min(512, seq), ts_hgrn=min(512, seq), hb_hgrn=4,
                tm_out=min(512, seq), tm_route=min(1024, seq), tm_exp=256,
                tm_fin=min(512, seq))


def kernel(x, c, positions, w_ada, b_ada, g_pre_mix, g_post_mix, g_pre_ffn, g_post_ffn, w_in, w_out, cmp_pe_k, cmp_w1_k, cmp_w2_k, cmp_pe_v, cmp_w1_v, cmp_w2_v, hgrn_lb_logits, hgrn_g_norm, w_group, b_group, w_router, b_router, w_gate, w_up, w_down):
    assert w_ada.shape[0] == 1, "single-layer block"
    return _block(x, c, positions, w_ada[0], b_ada[0], g_pre_mix[0], g_post_mix[0],
                  g_pre_ffn[0], g_post_ffn[0], w_in[0], w_out[0], cmp_pe_k[0], cmp_w1_k[0],
                  cmp_w2_k[0], cmp_pe_v[0], cmp_w1_v[0], cmp_w2_v[0], hgrn_lb_logits,
                  hgrn_g_norm[0], w_group[0], b_group[0], w_router[0], b_router[0],
                  w_gate[0], w_up[0], w_down[0], _config(x.shape[1]))
```

```python
import functools

import numpy as np
import jax
import jax.numpy as jnp
from jax import lax
from jax.experimental import pallas as pl
from jax.experimental.pallas import tpu as pltpu

F32 = jnp.float32
BF16 = jnp.bfloat16

HEAD_DIM = 128
N_KV_GROUPS = 2
HEADS_PER_GROUP = 4
N_ATT_HEADS = N_KV_GROUPS * HEADS_PER_GROUP
CMP_BLOCK = 32
CMP_STRIDE = 16
SEL_BLOCK = 64
N_SEL = 16
WINDOW = 512
FORCE_BONUS = 1.0e4
ROPE_THETA = 10000.0
N_HGRN_HEADS = 8
HGRN_CHUNK = 64
N_EXPERT_GROUPS = 4
EXPERTS_PER_GROUP = 8
N_EXPERTS = N_EXPERT_GROUPS * EXPERTS_PER_GROUP
EPS = 1e-6

LANES = 128
SUBLANES = 8
BF16_TILE_ROWS = 16
VMEM_LIMIT = 56 * 1024 * 1024

ATT_W = N_ATT_HEADS * HEAD_DIM
KV_W = N_KV_GROUPS * HEAD_DIM
COL_Q = 0
COL_KC = COL_Q + ATT_W
COL_KS = COL_KC + KV_W
COL_KW = COL_KS + KV_W
COL_VC = COL_KW + KV_W
COL_VS = COL_VC + KV_W
COL_VW = COL_VS + KV_W
COL_HQ = COL_VW + KV_W
COL_HF = COL_HQ + ATT_W
COL_HI = COL_HF + ATT_W
COL_HG = COL_HI + ATT_W
PROJ_W = COL_HG + ATT_W
ROPE_W = COL_VC

MASK_BIG = float(2 ** 30)
LOG2E = 1.4426950408889634
Q_PRESCALE = HEAD_DIM ** -0.5 * LOG2E

_NT = (((1,), (1,)), ((), ()))
_TN = (((0,), (0,)), ((), ()))


def _cparams(sem):
    return pltpu.CompilerParams(dimension_semantics=sem, vmem_limit_bytes=VMEM_LIMIT)


def _dot(a, b, **kw):
    return jnp.dot(a, b, preferred_element_type=F32, **kw)


def _dot_nt(a, b):
    return lax.dot_general(a, b, _NT, preferred_element_type=F32)


def _sigmoid(x):
    return 1.0 / (1.0 + jnp.exp(-x))


def _silu(x):
    return x * _sigmoid(x)


def _rms(x, g):
    return x * lax.rsqrt(jnp.mean(x * x, axis=-1, keepdims=True) + EPS) * g


def _mod_row(mod_ref, b, k):
    d = mod_ref.shape[1] // 6
    return mod_ref[pl.ds(b, 1), k * d:(k + 1) * d]


def _lane_col(x, lane, col):
    return jnp.sum(jnp.where(lane == col, x, 0.0), axis=-1, keepdims=True)


def _ada_kernel(c_ref, w_ref, b_ref, o_ref):
    s = _silu(c_ref[...])
    s_hi = s.astype(BF16)
    s_lo = (s - s_hi.astype(F32)).astype(BF16)
    w = w_ref[...]
    w_hi = w.astype(BF16)
    w_lo = (w - w_hi.astype(F32)).astype(BF16)
    o_ref[...] = _dot(s_hi, w_hi) + _dot(s_lo, w_hi) + _dot(s_hi, w_lo) + b_ref[...]


def _ada_mod(c8, w_ada, b_ada, tn=512):
    rows, d = c8.shape
    n = w_ada.shape[1]
    return pl.pallas_call(
        _ada_kernel,
        grid=(n // tn,),
        in_specs=[pl.BlockSpec((rows, d), lambda j: (0, 0)),
                  pl.BlockSpec((d, tn), lambda j: (0, j)),
                  pl.BlockSpec((1, tn), lambda j: (0, j))],
        out_specs=pl.BlockSpec((rows, tn), lambda j: (0, j)),
        out_shape=jax.ShapeDtypeStruct((rows, n), F32),
        compiler_params=_cparams(("parallel",)),
        name="ada_mod",
    )(c8, w_ada, b_ada.reshape(1, n))


def _rope_kernel(ang_ref, cos_ref, sin_ref):
    a = ang_ref[...]
    n = a.shape[0]
    half = HEAD_DIM // 2
    lane = lax.broadcasted_iota(jnp.int32, a.shape, 1)
    first = lane < half
    c = jnp.cos(a)
    s = jnp.sin(a)
    c_sw = pltpu.roll(c, half, axis=1)
    s_sw = pltpu.roll(s, half, axis=1)
    cos_ref[pl.ds(0, n, stride=2), :] = jnp.where(first, c, c_sw)
    cos_ref[pl.ds(1, n, stride=2), :] = jnp.where(first, c_sw, c)
    sin_ref[pl.ds(0, n, stride=2), :] = jnp.where(first, -s, s_sw)
    sin_ref[pl.ds(1, n, stride=2), :] = jnp.where(first, -s_sw, s)


def _rope_tables(positions, tm):
    t = positions.size
    inv_freq = ROPE_THETA ** (-jnp.arange(0, HEAD_DIM, 2, dtype=F32) / HEAD_DIM)
    pos = positions.reshape(t).astype(F32)
    ang = jnp.concatenate([pos[0::2, None] * inv_freq[None, :], pos[1::2, None] * inv_freq[None, :]],
                          axis=1)
    spec = pl.BlockSpec((tm, HEAD_DIM), lambda i: (i, 0))
    return pl.pallas_call(
        _rope_kernel,
        grid=(t // tm,),
        in_specs=[pl.BlockSpec((tm // 2, HEAD_DIM), lambda i: (i, 0))],
        out_specs=[spec, spec],
        out_shape=[jax.ShapeDtypeStruct((t, HEAD_DIM), F32)] * 2,
        compiler_params=_cparams(("parallel",)),
        name="rope_tables",
    )(ang)


def _inproj_kernel(x_hbm, mod_ref, g_ref, w_ref, wg_ref, cos_hbm, sin_hbm,
                   proj_ref, gates_ref, hbk_ref, hbv_ref, h_s, stage, x_s, xsem, cos_ref, sin_ref,
                   csem, *, tn, per_b):
    i = pl.program_id(0)
    j = pl.program_id(1)
    rows = h_s.shape[0]
    n_rope = -(-ROPE_W // tn)

    def table_copies(row_tile):
        return [pltpu.make_async_copy(src.at[pl.ds(row_tile * rows, rows)], dst, csem.at[k])
                for k, (src, dst) in enumerate(((cos_hbm, cos_ref), (sin_hbm, sin_ref)))]

    def x_copy(row_tile):
        return pltpu.make_async_copy(x_hbm.at[pl.ds(row_tile * rows, rows)], x_s, xsem.at[0])

    @pl.when(j == 0)
    def _():
        @pl.when(i == 0)
        def _():
            x_copy(0).start()
            for cp in table_copies(0):
                cp.start()

        for cp in table_copies(i):
            cp.wait()
        x_copy(i).wait()
        b = i // per_b
        h = (_rms(x_s[...], g_ref[...]) * (1.0 + _mod_row(mod_ref, b, 1))
             + _mod_row(mod_ref, b, 0))
        hb = h.astype(BF16)
        h_s[...] = hb
        gates_ref[...] = _dot(hb, wg_ref[...])

        @pl.when(i + 1 < pl.num_programs(0))
        def _():
            x_copy(i + 1).start()

    heads = tn // HEAD_DIM

    def store(jt):
        acc = _dot(h_s[...], w_ref[...])
        for c in range(heads):
            a = acc[:, c * HEAD_DIM:(c + 1) * HEAD_DIM]
            col = None if jt is None else jt * tn + c * HEAD_DIM
            if col is not None and col < ROPE_W:
                a = a * cos_ref[...] + pltpu.roll(a, HEAD_DIM // 2, axis=1) * sin_ref[...]
            if col is not None and col < ATT_W:
                a = a * Q_PRESCALE
            proj_ref[:, c * HEAD_DIM:(c + 1) * HEAD_DIM] = a.astype(BF16)
            for col0, hb_ref in ((COL_KC, hbk_ref), (COL_VC, hbv_ref)):
                if col is not None and col0 <= col < col0 + KV_W:
                    stage[...] = a
                    hb_ref[0, (col - col0) // HEAD_DIM] = jnp.concatenate(
                        [stage[pl.ds(i, rows // CMP_STRIDE, stride=CMP_STRIDE), :]
                         for i in range(CMP_STRIDE)], axis=1).astype(BF16)

    n_special = -(-COL_VS // tn)
    for jt in range(n_special):
        pl.when(j == jt)(functools.partial(store, jt))
    pl.when(j >= n_special)(functools.partial(store, None))

    @pl.when((j == n_rope - 1) & (i + 1 < pl.num_programs(0)))
    def _():
        for cp in table_copies(i + 1):
            cp.start()


def _inproj(x2, mod, g_pre, w_in_p, w_gates, cos_t, sin_t, seq, tm, tn=512):
    t, d = x2.shape
    per_b = seq // tm
    assert COL_KC % tn + KV_W <= tn and COL_VC % tn + KV_W <= tn
    assert tm % (BF16_TILE_ROWS * CMP_STRIDE) == 0
    hb_spec = pl.BlockSpec((1, N_KV_GROUPS, tm // CMP_STRIDE, CMP_STRIDE * HEAD_DIM),
                           lambda i, j: (i // per_b, 0, i % per_b, 0))
    hb_shape = jax.ShapeDtypeStruct(
        (t // seq, N_KV_GROUPS, seq // CMP_STRIDE, CMP_STRIDE * HEAD_DIM), BF16)
    return pl.pallas_call(
        functools.partial(_inproj_kernel, tn=tn, per_b=per_b),
        grid=(t // tm, PROJ_W // tn),
        in_specs=[pl.BlockSpec(memory_space=pl.ANY),
                  pl.BlockSpec(mod.shape, lambda i, j: (0, 0)),
                  pl.BlockSpec((1, d), lambda i, j: (0, 0)),
                  pl.BlockSpec((d, tn), lambda i, j: (0, j)),
                  pl.BlockSpec((d, LANES), lambda i, j: (0, 0)),
                  pl.BlockSpec(memory_space=pl.ANY),
                  pl.BlockSpec(memory_space=pl.ANY)],
        out_specs=[pl.BlockSpec((tm, tn), lambda i, j: (i, j)),
                   pl.BlockSpec((tm, LANES), lambda i, j: (i, 0)), hb_spec, hb_spec],
        out_shape=[jax.ShapeDtypeStruct((t, PROJ_W), BF16),
                   jax.ShapeDtypeStruct((t, LANES), F32), hb_shape, hb_shape],
        scratch_shapes=[pltpu.VMEM((tm, d), BF16), pltpu.VMEM((tm, HEAD_DIM), F32),
                        pltpu.VMEM((tm, d), F32), pltpu.SemaphoreType.DMA((1,)),
                        pltpu.VMEM((tm, HEAD_DIM), F32), pltpu.VMEM((tm, HEAD_DIM), F32),
                        pltpu.SemaphoreType.DMA((2,))],
        compiler_params=_cparams(("arbitrary", "arbitrary")),
        name="inproj",
    )(x2, mod, g_pre, w_in_p, w_gates, cos_t, sin_t)


def _compress_kernel(h_ref, w1_ref, pe_ref, w2_ref, o_ref):
    hb = h_ref[0]
    half = hb.shape[1]
    ya = _dot(hb, w1_ref[0:half, :])
    yb = _dot(hb, w1_ref[half:2 * half, :])
    const = _dot(pe_ref[...], w1_ref[...])
    n = ya.shape[0]
    yb_next = pltpu.roll(yb, n - 1, axis=0)
    hid = _silu(ya + yb_next + const[0:1, :])
    out = _dot(hid.astype(BF16), w2_ref[...])
    row = lax.broadcasted_iota(jnp.int32, out.shape, 0)
    o_ref[0] = jnp.where(row < n - 1, out, 0.0).astype(BF16)


def _compress(hblk, w1, pe, w2):
    bg, n, half = hblk.shape
    pe8 = jnp.zeros((8, 2 * half), BF16).at[0].set(pe.reshape(-1).astype(BF16))
    return pl.pallas_call(
        _compress_kernel,
        grid=(bg,),
        in_specs=[pl.BlockSpec((1, n, half), lambda i: (i, 0, 0)),
                  pl.BlockSpec((2 * half, HEAD_DIM), lambda i: (0, 0)),
                  pl.BlockSpec((8, 2 * half), lambda i: (0, 0)),
                  pl.BlockSpec((HEAD_DIM, HEAD_DIM), lambda i: (0, 0))],
        out_specs=pl.BlockSpec((1, n, HEAD_DIM), lambda i: (i, 0, 0)),
        out_shape=jax.ShapeDtypeStruct((bg, n, HEAD_DIM), BF16),
        compiler_params=_cparams(("parallel",)),
        name="compress",
    )(hblk, w1.astype(BF16), pe8, w2.astype(BF16))


def _cmp_kernel(q_ref, kc_ref, vc_ref, gates_ref, ov_ref, ocmp_ref, selb_ref, imp_s, cnt_s, *,
                tq, nsel):
    g = pl.program_id(0) % N_KV_GROUPS
    q0 = pl.program_id(1) * tq
    kc = kc_ref[0]
    vc = vc_ref[0]
    nc = kc.shape[0]
    t_idx = q0 + lax.broadcasted_iota(jnp.int32, (tq, nc), 0)
    c_idx = lax.broadcasted_iota(jnp.int32, (tq, nc), 1)
    c_ok = c_idx * CMP_STRIDE + (CMP_BLOCK - 1) <= t_idx
    gl = gates_ref[...]
    lane = lax.broadcasted_iota(jnp.int32, (tq, LANES), 1)
    psum = jnp.zeros((tq, nc), F32)
    for h in range(HEADS_PER_GROUP):
        qh = q_ref[:, h * HEAD_DIM:(h + 1) * HEAD_DIM]
        s = jnp.where(c_ok, _dot_nt(qh, kc), -jnp.inf)
        m = jnp.max(s, axis=-1, keepdims=True)
        m = jnp.where(m == -jnp.inf, 0.0, m)
        p = jnp.exp2(s - m)
        p = p * (1.0 / jnp.maximum(jnp.sum(p, axis=-1, keepdims=True), 1e-30))
        psum = psum + p
        o = _dot(p.astype(BF16), vc)
        glog = _lane_col(gl, lane, (g * HEADS_PER_GROUP + h) * 3)
        ocmp_ref[:, h * HEAD_DIM:(h + 1) * HEAD_DIM] = (o * _sigmoid(glog)).astype(BF16)

    imp = _dot(psum, ov_ref[...], precision=lax.Precision.HIGHEST)
    cur = (q0 + lax.broadcasted_iota(jnp.int32, (tq, LANES), 0)) // SEL_BLOCK
    forced = (lane == 0) | (lane == cur) | (lane == cur - 1)
    imp = jnp.where(lane <= cur, imp + jnp.where(forced, FORCE_BONUS, 0.0), -jnp.inf)
    imp_s[...] = imp.T[0:nsel, :]
    cnt_s[...] = jnp.zeros(cnt_s.shape, F32)
    sub = SUBLANES
    n_groups = nsel // sub
    row = lax.broadcasted_iota(jnp.int32, (sub, tq), 0)

    def count_against(kg):
        groups = [imp_s[g * sub:(g + 1) * sub, :] for g in range(n_groups)]
        cnts = [cnt_s[g * sub:(g + 1) * sub, :] for g in range(n_groups)]
        for k in range(kg * sub, (kg + 1) * sub):
            vk = imp_s[k:k + 1, :]
            for g, grp in enumerate(groups):
                if k < g * sub:
                    beats = vk >= grp
                elif k >= (g + 1) * sub:
                    beats = vk > grp
                else:
                    beats = (vk > grp) | ((vk == grp) & (row > k - g * sub))
                cnts[g] = cnts[g] + jnp.where(beats, 1.0, 0.0)
        for g in range(n_groups):
            cnt_s[g * sub:(g + 1) * sub, :] = cnts[g]

    last_block = (q0 + tq - 1) // SEL_BLOCK
    for kg in range(n_groups):
        pl.when(kg * sub <= last_block)(functools.partial(count_against, kg))
    imp_t = imp_s[...]
    sel = (cnt_s[...] < float(N_SEL)) & (imp_t > -jnp.inf)
    bias_t = jnp.where(sel, 0.0, -MASK_BIG)
    if nsel < LANES:
        bias_t = jnp.concatenate([bias_t, jnp.zeros((LANES - nsel, tq), F32)], axis=0)
    selb_ref[...] = bias_t.T.astype(BF16)


def _cmp_attn(proj, kc, vc, gates, batch, seq, tq):
    t = proj.shape[0]
    nq = seq // tq
    nc = kc.shape[1]
    nsel = seq // SEL_BLOCK
    assert nsel <= LANES and nsel % 8 == 0
    c_start = np.arange(nc) * CMP_STRIDE
    j_start = np.arange(LANES) * SEL_BLOCK
    ov = ((c_start[:, None] < j_start[None, :] + SEL_BLOCK)
          & (c_start[:, None] + CMP_BLOCK > j_start[None, :])
          & (np.arange(LANES)[None, :] < nsel)).astype(np.float32)
    gq = HEADS_PER_GROUP * HEAD_DIM
    qmap = lambda bg, i: ((bg // N_KV_GROUPS) * nq + i, bg % N_KV_GROUPS)
    return pl.pallas_call(
        functools.partial(_cmp_kernel, tq=tq, nsel=nsel),
        grid=(batch * N_KV_GROUPS, nq),
        in_specs=[pl.BlockSpec((tq, gq), qmap),
                  pl.BlockSpec((1, nc, HEAD_DIM), lambda bg, i: (bg, 0, 0)),
                  pl.BlockSpec((1, nc, HEAD_DIM), lambda bg, i: (bg, 0, 0)),
                  pl.BlockSpec((tq, LANES), lambda bg, i: ((bg // N_KV_GROUPS) * nq + i, 0)),
                  pl.BlockSpec((nc, LANES), lambda bg, i: (0, 0))],
        out_specs=[pl.BlockSpec((tq, gq), qmap),
                   pl.BlockSpec((tq, LANES), lambda bg, i: (bg * nq + i, 0))],
        out_shape=[jax.ShapeDtypeStruct((t, ATT_W), BF16),
                   jax.ShapeDtypeStruct((batch * N_KV_GROUPS * seq, LANES), BF16)],
        scratch_shapes=[pltpu.VMEM((nsel, tq), F32), pltpu.VMEM((nsel, tq), F32)],
        compiler_params=_cparams(("parallel", "parallel")),
        name="cmp_attn",
    )(proj, kc, vc, gates, jnp.asarray(ov))


def _selwin_kernel(q_ref, selb_ref, ocmp_ref, gates_ref, ks_ref, vs_ref, kw_ref, vw_ref,
                   cb_ref, wb_ref, o_ref, kaug_s, vaug_s, vwaug_s, qaug_s, sbuf_s, m_s, acc_s,
                   swin_s, *, tq, tk, seq, nwin):
    g = pl.program_id(0) % N_KV_GROUPS
    qi = pl.program_id(1)
    q0 = qi * tq
    hq = HEADS_PER_GROUP * tq
    half = hq // 2

    @pl.when(qi == 0)
    def _():
        r = lax.broadcasted_iota(jnp.int32, (seq, HEAD_DIM), 0)
        ln = lax.broadcasted_iota(jnp.int32, (seq, HEAD_DIM), 1)
        ones_col = jnp.where(ln == 0, 1.0, 0.0).astype(BF16)
        kaug_s[:, 0:HEAD_DIM] = ks_ref[...]
        kaug_s[:, HEAD_DIM:2 * HEAD_DIM] = jnp.where(r // SEL_BLOCK == ln, 1.0, 0.0).astype(BF16)
        vaug_s[:, 0:HEAD_DIM] = vs_ref[...]
        vaug_s[:, HEAD_DIM:2 * HEAD_DIM] = ones_col
        vwaug_s[:, 0:HEAD_DIM] = vw_ref[...]
        vwaug_s[:, HEAD_DIM:2 * HEAD_DIM] = ones_col

    for h in range(HEADS_PER_GROUP):
        qaug_s[h * tq:(h + 1) * tq, 0:HEAD_DIM] = q_ref[:, h * HEAD_DIM:(h + 1) * HEAD_DIM]
        qaug_s[h * tq:(h + 1) * tq, HEAD_DIM:2 * HEAD_DIM] = selb_ref[...]
    qa = qaug_s[...]

    def scores(kt, slot):
        ks = kaug_s[pl.ds(pl.multiple_of(kt * tk, tk), tk), :]
        sbuf_s[slot] = _dot_nt(qa, ks)

    def absorb(kt, slot, diagonal=False):
        vs = vaug_s[pl.ds(pl.multiple_of(kt * tk, tk), tk), :]
        s = sbuf_s[slot]
        if diagonal:
            bias = cb_ref[(q0 - kt * tk) // tq]
            s = (s.reshape(HEADS_PER_GROUP, tq, tk) + bias[None]).reshape(hq, tk)
        m = m_s[...]
        m_new = jnp.maximum(m, jnp.max(s, axis=-1, keepdims=True))
        m_s[...] = m_new
        pb = jnp.exp2(s - m_new[:, 0:1]).astype(BF16)
        if diagonal:
            pv = jnp.concatenate([_dot(pb[0:half], vs), _dot(pb[half:hq], vs)], axis=0)
        else:
            pv = _dot(pb, vs)
        acc_s[...] = jnp.exp2(m - m_new)[:, 0:1] * acc_s[...] + pv

    n_full = q0 // tk
    m_s[...] = jnp.full(m_s.shape, -jnp.inf, F32)
    acc_s[...] = jnp.zeros(acc_s.shape, F32)
    scores(0, 0)

    def pair(j, c):
        scores(2 * j + 1, 1)
        absorb(2 * j, 0)
        scores(2 * j + 2, 0)
        absorb(2 * j + 1, 1)
        return c

    lax.fori_loop(0, n_full // 2, pair, 0)
    odd = n_full % 2

    @pl.when(odd == 1)
    def _():
        scores(n_full, 1)
        absorb(n_full - 1, 0)

    nq = seq // tq
    w0 = jnp.clip(qi - (nwin - 1), 0, nq - nwin) * tq
    wlen = nwin * tq
    kwin = kw_ref[pl.ds(pl.multiple_of(w0, tq), wlen), :]
    vwin = vwaug_s[pl.ds(pl.multiple_of(w0, tq), wlen), :]
    swin_s[0:half, :] = _dot_nt(qaug_s[0:half, 0:HEAD_DIM], kwin)
    swin_s[half:hq, :] = _dot_nt(qaug_s[half:hq, 0:HEAD_DIM], kwin)
    absorb(n_full, odd, diagonal=True)
    acc = acc_s[...]
    o_sel = acc[:, 0:HEAD_DIM] / acc[:, HEAD_DIM:HEAD_DIM + 1]

    wbias = wb_ref[(q0 - w0) // tq]
    sw = (swin_s[...].reshape(HEADS_PER_GROUP, tq, wlen) + wbias[None]).reshape(hq, wlen)
    pwb = jnp.exp2(sw - jnp.max(sw, axis=-1, keepdims=True)).astype(BF16)
    ow = jnp.concatenate([_dot(pwb[0:half], vwin), _dot(pwb[half:hq], vwin)], axis=0)
    o_win = ow[:, 0:HEAD_DIM] / ow[:, HEAD_DIM:HEAD_DIM + 1]

    gl = gates_ref[...]
    lane = lax.broadcasted_iota(jnp.int32, (tq, LANES), 1)
    for h in range(HEADS_PER_GROUP):
        gcol = (g * HEADS_PER_GROUP + h) * 3
        g_s = _sigmoid(_lane_col(gl, lane, gcol + 1))
        g_w = _sigmoid(_lane_col(gl, lane, gcol + 2))
        o = (ocmp_ref[:, h * HEAD_DIM:(h + 1) * HEAD_DIM].astype(F32)
             + g_s * o_sel[h * tq:(h + 1) * tq, :] + g_w * o_win[h * tq:(h + 1) * tq, :])
        o_ref[:, h * HEAD_DIM:(h + 1) * HEAD_DIM] = o.astype(BF16)


def _selwin_attn(proj, selb, ocmp, gates, batch, seq, tq, tk):
    t = proj.shape[0]
    nq = seq // tq
    assert tk % tq == 0 and tq & (tq - 1) == 0 and seq % tk == 0
    nwin = min(WINDOW // tq + 1, nq)
    assert WINDOW % tq == 0
    gq = HEADS_PER_GROUP * HEAD_DIM
    hq = HEADS_PER_GROUP * tq
    r = np.arange(tq)[:, None]
    cbias = np.stack([np.where(np.arange(tk)[None, :] <= r + off * tq, 0.0, -np.inf)
                      for off in range(tk // tq)]).astype(np.float32)
    dist = [r + off * tq - np.arange(nwin * tq)[None, :] for off in range(nwin)]
    wbias = np.stack([np.where((d >= 0) & (d < WINDOW), 0.0, -np.inf)
                      for d in dist]).astype(np.float32)
    qmap = lambda bg, i: ((bg // N_KV_GROUPS) * nq + i, bg % N_KV_GROUPS)

    def kvspec(col):
        return pl.BlockSpec((seq, HEAD_DIM),
                            lambda bg, i: (bg // N_KV_GROUPS, col // HEAD_DIM + bg % N_KV_GROUPS))

    return pl.pallas_call(
        functools.partial(_selwin_kernel, tq=tq, tk=tk, seq=seq, nwin=nwin),
        grid=(batch * N_KV_GROUPS, nq),
        in_specs=[pl.BlockSpec((tq, gq), qmap),
                  pl.BlockSpec((tq, LANES), lambda bg, i: (bg * nq + i, 0)),
                  pl.BlockSpec((tq, gq), qmap),
                  pl.BlockSpec((tq, LANES), lambda bg, i: ((bg // N_KV_GROUPS) * nq + i, 0)),
                  kvspec(COL_KS), kvspec(COL_VS), kvspec(COL_KW), kvspec(COL_VW),
                  pl.BlockSpec(cbias.shape, lambda bg, i: (0, 0, 0)),
                  pl.BlockSpec(wbias.shape, lambda bg, i: (0, 0, 0))],
        out_specs=pl.BlockSpec((tq, gq), qmap),
        out_shape=jax.ShapeDtypeStruct((t, ATT_W), BF16),
        scratch_shapes=[pltpu.VMEM((seq, 2 * HEAD_DIM), BF16),
                        pltpu.VMEM((seq, 2 * HEAD_DIM), BF16),
                        pltpu.VMEM((seq, 2 * HEAD_DIM), BF16),
                        pltpu.VMEM((hq, 2 * HEAD_DIM), BF16),
                        pltpu.VMEM((2, hq, tk), F32),
                        pltpu.VMEM((hq, LANES), F32),
                        pltpu.VMEM((hq, 2 * HEAD_DIM), F32),
                        pltpu.VMEM((hq, nwin * tq), F32)],
        compiler_params=_cparams(("parallel", "arbitrary")),
        name="selwin_attn",
    )(proj, selb, ocmp, gates, proj, proj, proj, proj, jnp.asarray(cbias), jnp.asarray(wbias))


_HGRN_LEVELS = (32, 16, 8, 4, 2, 1)


def _hgrn_tables():
    c = HGRN_CHUNK
    nl = len(_HGRN_LEVELS)
    idx = np.arange(c)
    i, j = idx[:, None], idx[None, :]
    mats = [j <= i, j > i]
    lvl = np.full((c, c), nl + 1, np.int32)
    for li, s in enumerate(_HGRN_LEVELS):
        mats.append((j >= (i // s) * s) & (j <= i))
        mats.append((j > i) & (j <= (i // s) * s + s - 1))
        lvl[((i // (2 * s)) == (j // (2 * s))) & (((i // s) % 2) == 1) & (((j // s) % 2) == 0)] = li
    lvl[i == j] = nl
    w = np.concatenate(mats, 0).astype(np.float32)
    return np.concatenate([w, w], axis=1), lvl


def _hgrn_head(hq_ref, hf_ref, hi_ref, hg_ref, lbl_ref, gn_ref, w_ref, lvl_ref, o_ref, st_s,
               hh, ts):
    c = HGRN_CHUNK
    nc = ts // c
    nl = len(_HGRN_LEVELS)
    cs = slice(hh * HEAD_DIM, (hh + 1) * HEAD_DIM)
    lbl = lbl_ref[:, cs]
    e = jnp.exp(lbl - jnp.max(lbl, axis=0, keepdims=True))
    lb = e[0:1, :] / jnp.sum(e, axis=0, keepdims=True)

    f = lb + (1.0 - lb) * _sigmoid(hf_ref[:, cs].astype(F32))
    k = 1.0 - f
    qf = _silu(hq_ref[:, cs].astype(F32))
    v = hi_ref[:, cs]

    lf = jnp.log(f) * LOG2E
    lf_hi = lf.astype(BF16)
    lf_lo = (lf - lf_hi.astype(F32)).astype(BF16)
    side = lambda a: jnp.concatenate([a[ci * c:(ci + 1) * c, :] for ci in range(nc)], axis=1)
    ex = _dot(w_ref[...], jnp.concatenate([side(lf_hi), side(lf_lo)], axis=0))

    def rows(bi):
        return jnp.concatenate(
            [ex[bi * c:(bi + 1) * c, ci * HEAD_DIM:(ci + 1) * HEAD_DIM] for ci in range(nc)], axis=0)

    b = rows(0)
    qhat = (qf * jnp.exp2(b)).astype(BF16)
    khat = (k * jnp.exp2(rows(1))).astype(BF16)
    lvl = lvl_ref[...]
    masks = [lvl == li for li in range(nl + 1)]
    qs = [(qf * jnp.exp2(rows(2 + 2 * li))).astype(BF16) for li in range(nl)] + [qf.astype(BF16)]
    ks = [(k * jnp.exp2(rows(3 + 2 * li))).astype(BF16) for li in range(nl)] + [k.astype(BF16)]
    o_intra = []
    for ci in range(nc):
        sl = slice(ci * c, (ci + 1) * c)
        a = jnp.zeros((c, c), F32)
        for li in range(nl + 1):
            a = jnp.where(masks[li], _dot_nt(qs[li][sl, :], ks[li][sl, :]), a)
        o_intra.append(_dot(a.astype(BF16), v[sl, :]))

    st = st_s[hh]
    outs = []
    for ci in range(nc):
        sl = slice(ci * c, (ci + 1) * c)
        outs.append(o_intra[ci] + _dot_nt(qhat[sl, :], st.astype(BF16)))
        st = st * jnp.exp2(b[(ci + 1) * c - 1:(ci + 1) * c, :]) + lax.dot_general(
            v[sl, :], khat[sl, :], _TN, preferred_element_type=F32)
    st_s[hh] = st
    o = jnp.concatenate(outs, axis=0)
    o = o * lax.rsqrt(jnp.mean(o * o, axis=-1, keepdims=True) + EPS) * gn_ref[:, cs]
    o_ref[:, cs] = (o * _silu(hg_ref[:, cs].astype(F32))).astype(BF16)


def _hgrn_kernel(hq_ref, hf_ref, hi_ref, hg_ref, lbl_ref, gn_ref, w_ref, lvl_ref,
                 o_ref, st_s, *, ts, hb):
    @pl.when(pl.program_id(2) == 0)
    def _():
        st_s[...] = jnp.zeros_like(st_s)

    for hh in range(hb):
        _hgrn_head(hq_ref, hf_ref, hi_ref, hg_ref, lbl_ref, gn_ref, w_ref, lvl_ref, o_ref, st_s,
                   hh, ts)


def _hgrn(proj, lb_logits, g_norm, batch, seq, ts, hb):
    t = proj.shape[0]
    ns = seq // ts
    wmat, lvl = _hgrn_tables()
    w = hb * HEAD_DIM
    assert all(col % w == 0 for col in (COL_HQ, COL_HF, COL_HI, COL_HG))

    def colspec(col):
        return pl.BlockSpec((ts, w), lambda b, h, s: (b * ns + s, col // w + h))

    nl = lb_logits.shape[0]
    return pl.pallas_call(
        functools.partial(_hgrn_kernel, ts=ts, hb=hb),
        grid=(batch, N_HGRN_HEADS // hb, ns),
        in_specs=[colspec(COL_HQ), colspec(COL_HF), colspec(COL_HI), colspec(COL_HG),
                  pl.BlockSpec((nl, w), lambda b, h, s: (0, h)),
                  pl.BlockSpec((1, w), lambda b, h, s: (0, h)),
                  pl.BlockSpec(wmat.shape, lambda b, h, s: (0, 0)),
                  pl.BlockSpec(lvl.shape, lambda b, h, s: (0, 0))],
        out_specs=pl.BlockSpec((ts, w), lambda b, h, s: (b * ns + s, h)),
        out_shape=jax.ShapeDtypeStruct((t, ATT_W), BF16),
        scratch_shapes=[pltpu.VMEM((hb, HEAD_DIM, HEAD_DIM), F32)],
        compiler_params=_cparams(("parallel", "parallel", "arbitrary")),
        name="hgrn2",
    )(proj, proj, proj, proj, lb_logits, g_norm, jnp.asarray(wmat, BF16), jnp.asarray(lvl))


def _outproj_kernel(oa_ref, or_ref, wa_ref, wb_ref, x_ref, mod_ref, gpost_ref, gpre_ref,
                    wr_ref, br_ref, x1_ref, h2_ref, lg_ref, stage, *, per_b):
    b = pl.program_id(0) // per_b
    y = _dot(oa_ref[...], wa_ref[...]) + _dot(or_ref[...], wb_ref[...])
    x1 = x_ref[...] + _mod_row(mod_ref, b, 2) * _rms(y, gpost_ref[...])
    x1_ref[...] = x1
    h = _rms(x1, gpre_ref[...]) * (1.0 + _mod_row(mod_ref, b, 4)) + _mod_row(mod_ref, b, 3)
    h2_ref[...] = _to_slabs(stage, h)
    h_hi = h.astype(BF16)
    h_lo = (h - h_hi.astype(F32)).astype(BF16)
    hh = _dot(h_hi, wr_ref[...])
    lg_ref[...] = (hh[:, 0:LANES] + hh[:, LANES:2 * LANES]
                   + _dot(h_lo, wr_ref[:, 0:LANES]) + br_ref[...])


def _outproj(o_att, o_rec, w_out, x2, mod, g_post, g_pre, wr, br, seq, tm):
    t, d = x2.shape
    per_b = seq // tm
    wr_hi = wr.astype(BF16)
    wr_cat = jnp.concatenate([wr_hi, (wr - wr_hi.astype(F32)).astype(BF16)], axis=1)
    wa = w_out[:ATT_W].astype(BF16)
    wb = w_out[ATT_W:].astype(BF16)
    row = lambda i: (i, 0)
    fixed = lambda i: (0, 0)
    return pl.pallas_call(
        functools.partial(_outproj_kernel, per_b=per_b),
        grid=(t // tm,),
        in_specs=[pl.BlockSpec((tm, ATT_W), row), pl.BlockSpec((tm, ATT_W), row),
                  pl.BlockSpec((ATT_W, d), fixed), pl.BlockSpec((ATT_W, d), fixed),
                  pl.BlockSpec((tm, d), row),
                  pl.BlockSpec(mod.shape, fixed),
                  pl.BlockSpec((1, d), fixed), pl.BlockSpec((1, d), fixed),
                  pl.BlockSpec((d, 2 * LANES), fixed), pl.BlockSpec((1, LANES), fixed)],
        out_specs=[pl.BlockSpec((tm, d), row), pl.BlockSpec((tm * (d // LANES), LANES), row),
                   pl.BlockSpec((tm, LANES), row)],
        out_shape=[jax.ShapeDtypeStruct((t, d), F32),
                   jax.ShapeDtypeStruct((t * (d // LANES), LANES), BF16),
                   jax.ShapeDtypeStruct((t, LANES), F32)],
        scratch_shapes=[pltpu.VMEM((tm * (d // LANES), LANES), F32)],
        compiler_params=_cparams(("parallel",)),
        name="outproj",
    )(o_att, o_rec, wa, wb, x2, mod, g_post, g_pre, wr_cat, br)


G_LANE0 = N_EXPERTS


def _route_kernel(lg_ref, tri_ref, info_ref, cnt_ref, carry_s):
    @pl.when(pl.program_id(0) == 0)
    def _():
        carry_s[...] = jnp.zeros_like(carry_s)

    x = lg_ref[...]
    lane = lax.broadcasted_iota(jnp.int32, x.shape, 1)
    big = jnp.int32(10 ** 6)
    rmax = lambda a: jnp.max(a, axis=-1, keepdims=True)
    rmin = lambda a: jnp.min(a, axis=-1, keepdims=True)
    rsum = lambda a: jnp.sum(a, axis=-1, keepdims=True)

    is_g = (lane >= G_LANE0) & (lane < G_LANE0 + N_EXPERT_GROUPS)
    lgm = jnp.where(is_g, x, -jnp.inf)
    mg = rmax(lgm)
    pg_sel = 1.0 / rsum(jnp.where(is_g, jnp.exp(lgm - mg), 0.0))
    gsel = rmin(jnp.where(lgm == mg, lane, big)) - G_LANE0

    is_e = (lane >= gsel * EXPERTS_PER_GROUP) & (lane < (gsel + 1) * EXPERTS_PER_GROUP)
    lem = jnp.where(is_e, x, -jnp.inf)
    pe = jnp.where(is_e, jnp.exp(lem - rmax(lem)), 0.0)
    pe = pe / rsum(pe)
    pe = jnp.where(is_e, pe, -1.0)
    v1 = rmax(pe)
    i1 = rmin(jnp.where(pe == v1, lane, big))
    pe2 = jnp.where(lane == i1, -1.0, pe)
    v2 = rmax(pe2)
    i2 = rmin(jnp.where(pe2 == v2, lane, big))
    w1 = v1 / (v1 + v2) * pg_sel
    w2 = v2 / (v1 + v2) * pg_sel

    oh1 = jnp.where(lane == i1, 1.0, 0.0)
    oh2 = jnp.where(lane == i2, 1.0, 0.0)
    both = oh1 + oh2
    before = _dot(tri_ref[...], both.astype(BF16)) + carry_s[0:1, :]
    r1 = rsum(oh1 * before)
    r2 = rsum(oh2 * before)
    carry_s[0:1, :] = carry_s[0:1, :] + jnp.sum(both, axis=0, keepdims=True)
    cnt_ref[...] = carry_s[...]

    cols = (i1.astype(F32), i2.astype(F32), w1, w2, r1, r2)
    info = jnp.zeros(x.shape, F32)
    for ci, col in enumerate(cols):
        info = jnp.where(lane == ci, col, info)
    info_ref[...] = info


def _route(logits, tm):
    t = logits.shape[0]
    tri = jnp.asarray(np.tril(np.ones((tm, tm), np.float32), -1), BF16)
    return pl.pallas_call(
        _route_kernel,
        grid=(t // tm,),
        in_specs=[pl.BlockSpec((tm, LANES), lambda i: (i, 0)),
                  pl.BlockSpec((tm, tm), lambda i: (0, 0))],
        out_specs=[pl.BlockSpec((tm, LANES), lambda i: (i, 0)),
                   pl.BlockSpec((8, LANES), lambda i: (0, 0))],
        out_shape=[jax.ShapeDtypeStruct((t, LANES), F32),
                   jax.ShapeDtypeStruct((8, LANES), F32)],
        scratch_shapes=[pltpu.VMEM((8, LANES), F32)],
        compiler_params=_cparams(("arbitrary",)),
        name="route",
    )(logits, tri)


def _to_slabs(stage, x):
    n, d = x.shape
    slab = d // LANES
    for c in range(slab):
        stage[pl.ds(c, n, stride=slab), :] = x[:, c * LANES:(c + 1) * LANES]
    return stage[...].astype(BF16)


def _from_slabs(stage, slabs, n):
    slab = slabs.shape[0] // n
    stage[...] = slabs.astype(F32)
    return jnp.concatenate([stage[pl.ds(c, n, stride=slab), :] for c in range(slab)], axis=1)


def _expert_kernel(te_ref, nu_ref, par_ref, nxt_ref, dst_ref, xs_ref, wg_hbm, wu_hbm, wd_hbm,
                   y_ref, obuf, stage, ssem, wg_f, wu_f, wd_f, wsem, wg_s, wu_s, wd_s, *,
                   tm, standin_row):
    t = pl.program_id(0)
    nu = nu_ref[0]
    slot = t % 2

    def weights(e, buf):
        return [pltpu.make_async_copy(src.at[e], dst.at[buf], wsem.at[buf])
                for src, dst in ((wg_hbm, wg_f), (wu_hbm, wu_f), (wd_hbm, wd_f))]
    slab = wg_s.shape[0] // LANES
    tile = tm * slab

    def token(ref, i):
        return ref.at[pl.ds(pl.multiple_of(i * slab, slab), slab)]

    def scatter(s):
        for r in range(tm):
            pltpu.make_async_copy(token(obuf, s * tm + r), token(y_ref, dst_ref[0, 0, r]),
                                  ssem.at[s]).start()

    def tile_of(buf, s):
        return buf.at[pl.ds(pl.multiple_of(s * tile, tile), tile)]

    def wait_tile(s):
        pltpu.make_async_copy(tile_of(obuf, s), tile_of(obuf, s), ssem.at[s]).wait()

    @pl.when(t < nu)
    def _():
        @pl.when(t == 0)
        def _():
            obuf[tile:2 * tile, :] = jnp.zeros((tile, LANES), obuf.dtype)

            def fill(row0, s):
                return pltpu.make_async_copy(tile_of(obuf, 1),
                                             y_ref.at[pl.ds(row0 * slab, tile)], ssem.at[s])

            fill(standin_row - 2 * tm, 0).start()
            fill(standin_row - tm, 0).start()
            fill(standin_row - 2 * tm, 0).wait()
            fill(standin_row - tm, 0).wait()
            fill(standin_row, 1).start()

        par = par_ref[t]

        @pl.when(t == 0)
        def _():
            for cp in weights(te_ref[0], par):
                cp.start()

        @pl.when((t == 0) | (te_ref[t] != te_ref[jnp.maximum(t - 1, 0)]))
        def _():
            for cp in weights(te_ref[t], par):
                cp.wait()
            wg_s[...] = wg_f[par].astype(BF16)
            wu_s[...] = wu_f[par].astype(BF16)
            wd_s[...] = wd_f[par].astype(BF16)

            @pl.when(nxt_ref[t] != te_ref[t])
            def _():
                for cp in weights(nxt_ref[t], 1 - par):
                    cp.start()

        x = _from_slabs(stage, xs_ref[...], tm).astype(BF16)
        hid = (_silu(_dot(x, wg_s[...])) * _dot(x, wu_s[...])).astype(BF16)
        row0 = pl.multiple_of(slot * tile, tile)
        obuf[pl.ds(row0, tile), :] = _to_slabs(stage, _dot(hid, wd_s[...]))
        scatter(slot)
        wait_tile(1 - slot)

        @pl.when(t == nu - 1)
        def _():
            wait_tile(slot)


def _experts(tile_expert, n_used, parity, next_expert, dst3, xs, w_gate, w_up, w_down, y_rows,
             tm, standin_row):
    n_tiles = dst3.shape[0]
    d, f = w_gate.shape[1:]
    slab = d // LANES
    hbm = pl.BlockSpec(memory_space=pl.ANY)
    return pl.pallas_call(
        functools.partial(_expert_kernel, tm=tm, standin_row=standin_row),
        grid_spec=pltpu.PrefetchScalarGridSpec(
            num_scalar_prefetch=4,
            grid=(n_tiles,),
            in_specs=[pl.BlockSpec((1, 1, tm), lambda t, te, nu, pa, nx: (t, 0, 0),
                                   memory_space=pltpu.SMEM),
                      pl.BlockSpec((tm * slab, LANES),
                                   lambda t, te, nu, pa, nx: (jnp.minimum(t, nu[0] - 1), 0)),
                      hbm, hbm, hbm],
            out_specs=pl.BlockSpec(memory_space=pl.ANY),
            scratch_shapes=[pltpu.VMEM((2 * tm * slab, LANES), BF16),
                            pltpu.VMEM((tm * slab, LANES), F32),
                            pltpu.SemaphoreType.DMA((2,)),
                            pltpu.VMEM((2, d, f), F32), pltpu.VMEM((2, d, f), F32),
                            pltpu.VMEM((2, f, d), F32),
                            pltpu.SemaphoreType.DMA((2,)),
                            pltpu.VMEM((d, f), BF16), pltpu.VMEM((d, f), BF16),
                            pltpu.VMEM((f, d), BF16)]),
        out_shape=jax.ShapeDtypeStruct((y_rows * slab, LANES), BF16),
        compiler_params=_cparams(("arbitrary",)),
        name="moe_experts",
    )(tile_expert, n_used, parity, next_expert, dst3, xs, w_gate, w_up, w_down)


def _dispatch_kernel(pos0_ref, pos1_ref, end_ref, nu_ref, h_ref, dst_in, xs_ref, dst_ref,
                     zbuf, zsem, sem, *, tm, slab, n_tiles, n_tokens):
    tile = tm * slab
    base = pl.program_id(0) * tm

    @pl.when(pl.program_id(0) == 0)
    def _():
        init = pltpu.make_async_copy(dst_in, dst_ref, zsem)
        init.start()
        init.wait()
        zbuf[...] = jnp.zeros(zbuf.shape, zbuf.dtype)

        def zero(row_end, go):
            cp = pltpu.make_async_copy(
                zbuf, xs_ref.at[pl.ds(pl.multiple_of((row_end - tm) * slab, slab), tile)], zsem)
            pl.when(go)(cp.start)
            return cp, go

        pending = [zero(end_ref[e], end_ref[e] > (end_ref[e - 1] if e else 0))
                   for e in range(N_EXPERTS)]
        pending += [zero((i + 1) * tm, i >= nu_ref[0]) for i in range(n_tiles)]
        for cp, go in pending:
            pl.when(go)(cp.wait)

    def token(ref, i):
        return ref.at[pl.ds(pl.multiple_of(i * slab, slab), slab)]

    def start(r, c):
        p0 = pos0_ref[0, 0, r]
        p1 = pos1_ref[0, 0, r]
        pltpu.make_async_copy(token(h_ref, r), token(xs_ref, p0), sem).start()
        pltpu.make_async_copy(token(h_ref, r), token(xs_ref, p1), sem).start()
        dst_ref[p0] = base + r
        dst_ref[p1] = n_tokens + base + r
        return c

    lax.fori_loop(0, tm, start, 0, unroll=8)
    for _ in range(2):
        pltpu.make_async_copy(h_ref, h_ref, sem).wait()


def _dispatch(pos0, pos1, tile_end, n_used, h2, dst0, n_tiles, tm):
    rows = h2.shape[0]
    slab = rows // pos0.shape[0]
    t = pos0.shape[0]
    per_tile = pl.BlockSpec((1, 1, tm), lambda i: (i, 0, 0), memory_space=pltpu.SMEM)
    whole = pl.BlockSpec(memory_space=pltpu.SMEM)
    return pl.pallas_call(
        functools.partial(_dispatch_kernel, tm=tm, slab=slab, n_tiles=n_tiles, n_tokens=t),
        grid=(t // tm,),
        in_specs=[per_tile, per_tile, whole, whole,
                  pl.BlockSpec((tm * slab, LANES), lambda i: (i, 0)),
                  pl.BlockSpec(memory_space=pl.ANY)],
        out_specs=[pl.BlockSpec(memory_space=pl.ANY), whole],
        out_shape=[jax.ShapeDtypeStruct((n_tiles * tm * slab, LANES), h2.dtype),
                   jax.ShapeDtypeStruct(dst0.shape, jnp.int32)],
        scratch_shapes=[pltpu.VMEM((tm * slab, LANES), h2.dtype),
                        pltpu.SemaphoreType.DMA(()), pltpu.SemaphoreType.DMA(())],
        compiler_params=_cparams(("arbitrary",)),
        name="moe_dispatch",
    )(pos0.reshape(t // tm, 1, tm), pos1.reshape(t // tm, 1, tm), tile_end * tm, n_used, h2,
      dst0)


def _final_kernel(info_ref, x1_ref, mod_ref, g_ref, ya_ref, yb_ref, o_ref, stage_a, stage_b, *,
                  per_b):
    info = info_ref[...]
    tm = x1_ref.shape[0]
    ya = _from_slabs(stage_a, ya_ref[...], tm)
    yb = _from_slabs(stage_b, yb_ref[...], tm)
    y = info[:, 2:3] * ya + info[:, 3:4] * yb
    gain = _mod_row(mod_ref, pl.program_id(0) // per_b, 5) * g_ref[...]
    inv = lax.rsqrt(jnp.mean(y * y, axis=-1, keepdims=True) + EPS)
    o_ref[...] = x1_ref[...] + (y * inv) * gain


def _moe_final(info, x1, mod, g_post, y2, seq, tm):
    t, d = x1.shape
    per_b = seq // tm
    nb = t // tm
    return pl.pallas_call(
        functools.partial(_final_kernel, per_b=per_b),
        grid=(nb,),
        in_specs=[pl.BlockSpec((tm, LANES), lambda i: (i, 0)),
                  pl.BlockSpec((tm, d), lambda i: (i, 0)),
                  pl.BlockSpec(mod.shape, lambda i: (0, 0)),
                  pl.BlockSpec((1, d), lambda i: (0, 0)),
                  pl.BlockSpec((tm * (d // LANES), LANES), lambda i: (i, 0)),
                  pl.BlockSpec((tm * (d // LANES), LANES), lambda i: (nb + i, 0))],
        out_specs=pl.BlockSpec((tm, d), lambda i: (i, 0)),
        out_shape=jax.ShapeDtypeStruct((t, d), F32),
        scratch_shapes=[pltpu.VMEM((tm * (d // LANES), LANES), F32)] * 2,
        compiler_params=_cparams(("parallel",)),
        name="moe_final",
    )(info, x1, mod, g_post, y2, y2)


def _moe(h2, logits, x1, mod, g_post, w_gate, w_up, w_down, seq, tm_route, tm_exp, tm_fin):
    t = x1.shape[0]
    info, cnt = _route(logits, tm_route)
    counts = cnt[0, :N_EXPERTS].astype(jnp.int32)
    tiles_e = (counts + tm_exp - 1) // tm_exp
    eidx = jnp.arange(N_EXPERTS, dtype=jnp.int32)
    tile_end = jnp.sum(jnp.where(eidx[None, :] <= eidx[:, None], tiles_e[None, :], 0), axis=1)
    offs = (tile_end - tiles_e) * tm_exp
    n_used = jnp.sum(tiles_e)

    def slot_of(k):
        e = info[:, k].astype(jnp.int32)
        start = jnp.sum(jnp.where(e[:, None] == jnp.arange(N_EXPERTS)[None, :], offs[None, :], 0),
                        axis=1)
        return start + info[:, 4 + k].astype(jnp.int32)

    n_tiles = (2 * t) // tm_exp + N_EXPERTS
    tidx = jnp.minimum(jnp.arange(n_tiles, dtype=jnp.int32), n_used - 1)
    tile_expert = jnp.sum(tidx[:, None] >= tile_end[None, :], axis=1).astype(jnp.int32)
    used = (tiles_e > 0)[None, :]
    parity = jnp.sum(used & (eidx[None, :] < tile_expert[:, None]), axis=1) % 2
    later = jnp.min(jnp.where(used & (eidx[None, :] > tile_expert[:, None]), eidx[None, :],
                              N_EXPERTS), axis=1)
    next_expert = jnp.where(later == N_EXPERTS, tile_expert, later)
    pos0, pos1 = slot_of(0), slot_of(1)
    slot_ids = jnp.arange(n_tiles * tm_exp, dtype=jnp.int32)
    spare = 2 * t + ((slot_ids // tm_exp) % 2) * tm_exp + slot_ids % tm_exp
    n_used1 = n_used.reshape(1).astype(jnp.int32)
    xs, dst = _dispatch(pos0, pos1, tile_end.astype(jnp.int32), n_used1, h2, spare, n_tiles,
                        tm_exp)

    ew = w_gate.shape
    wg = w_gate.reshape(N_EXPERTS, ew[-2], ew[-1])
    wu = w_up.reshape(N_EXPERTS, ew[-2], ew[-1])
    wd = w_down.reshape(N_EXPERTS, ew[-1], ew[-2])
    y2 = _experts(tile_expert, n_used1, parity.astype(jnp.int32), next_expert.astype(jnp.int32),
                  dst.reshape(n_tiles, 1, tm_exp), xs, wg, wu, wd,
                  2 * t + 3 * tm_exp, tm_exp, 2 * t + 2 * tm_exp)
    return _moe_final(info, x1, mod, g_post, y2, seq, tm_fin)


def _permute_w_in(w_in):
    sizes = [ATT_W] + [KV_W] * 6 + [3 * N_ATT_HEADS] + [ATT_W] * 4
    cuts = np.cumsum(sizes)[:-1]
    q, kc, vc, ks, vs, kw, vw, gates, hq, hf, hi, hg = [
        p.astype(BF16) for p in jnp.split(w_in, cuts, axis=1)]
    w_p = jnp.concatenate([q, kc, ks, kw, vc, vs, vw, hq, hf, hi, hg], axis=1)
    w_g = jnp.pad(gates, ((0, 0), (0, LANES - gates.shape[1])))
    return w_p, w_g


def _block(x, c, positions, w_ada, b_ada, g_pre_mix, g_post_mix, g_pre_ffn, g_post_ffn,
           w_in, w_out, pe_k, w1_k, w2_k, pe_v, w1_v, w2_v, lb_logits, g_norm,
           w_group, b_group, w_router, b_router, w_gate, w_up, w_down, cfg):
    batch, seq, d = x.shape
    t = batch * seq
    x2 = x.reshape(t, d)

    c8 = jnp.zeros((8, d), F32).at[:batch].set(c)
    mod = _ada_mod(c8, w_ada, b_ada)
    cos_t, sin_t = _rope_tables(positions, cfg["tm_rope"])
    w_p, w_g = _permute_w_in(w_in)
    proj, gates, hbk, hbv = _inproj(x2, mod, g_pre_mix.reshape(1, d), w_p, w_g, cos_t, sin_t,
                                    seq, cfg["tm_in"])

    half_blocks = (batch * N_KV_GROUPS, seq // CMP_STRIDE, CMP_STRIDE * HEAD_DIM)
    kc = _compress(hbk.reshape(half_blocks), w1_k, pe_k, w2_k)
    vc = _compress(hbv.reshape(half_blocks), w1_v, pe_v, w2_v)
    ocmp, selb = _cmp_attn(proj, kc, vc, gates, batch, seq, cfg["tq_cmp"])
    o_att = _selwin_attn(proj, selb, ocmp, gates, batch, seq, cfg["tq"], cfg["tk"])
    o_rec = _hgrn(proj, lb_logits, g_norm.reshape(1, -1), batch, seq, cfg["ts_hgrn"],
                  cfg["hb_hgrn"])

    wr = jnp.concatenate([w_router, w_group], axis=1)
    wr = jnp.pad(wr, ((0, 0), (0, LANES - wr.shape[1])))
    br = jnp.pad(jnp.concatenate([b_router, b_group]), (0, LANES - N_EXPERTS - N_EXPERT_GROUPS))
    x1, h2, logits = _outproj(o_att, o_rec, w_out, x2, mod, g_post_mix.reshape(1, d),
                              g_pre_ffn.reshape(1, d), wr, br.reshape(1, LANES), seq,
                              cfg["tm_out"])
    out = _moe(h2, logits, x1, mod, g_post_ffn.reshape(1, d), w_gate, w_up, w_down, seq,
               cfg["tm_route"], cfg["tm_exp"], cfg["tm_fin"])
    return out.reshape(batch, seq, d)


def _config(seq):
    return dict(tm_rope=min(1024, seq), tm_in=min(1024, seq), tq_cmp=min(1024, seq),
                tq=min(256, seq), tk=min(512, seq), ts_hgrn=min(512, seq), hb_hgrn=4,
                tm_out=min(512, seq), tm_route=min(1024, seq), tm_exp=256,
                tm_fin=min(512, seq))


def kernel(x, c, positions, w_ada, b_ada, g_pre_mix, g_post_mix, g_pre_ffn, g_post_ffn, w_in, w_out, cmp_pe_k, cmp_w1_k, cmp_w2_k, cmp_pe_v, cmp_w1_v, cmp_w2_v, hgrn_lb_logits, hgrn_g_norm, w_group, b_group, w_router, b_router, w_gate, w_up, w_down):
    assert w_ada.shape[0] == 1, "single-layer block"
    return _block(x, c, positions, w_ada[0], b_ada[0], g_pre_mix[0], g_post_mix[0],
                  g_pre_ffn[0], g_post_ffn[0], w_in[0], w_out[0], cmp_pe_k[0], cmp_w1_k[0],
                  cmp_w2_k[0], cmp_pe_v[0], cmp_w1_v[0], cmp_w2_v[0], hgrn_lb_logits,
                  hgrn_g_norm[0], w_group[0], b_group[0], w_router[0], b_router[0],
                  w_gate[0], w_up[0], w_down[0], _config(x.shape[1]))
```
